```python
import math
import jax
import jax.numpy as jnp
from jax import lax
import numpy as np

D_MODEL = 1024
BATCH = 8
SEQ = 2048
DEPTH = 1

CTX_LEN = 256
GRID_W = 64

ATTN_HEADS = 8
ATTN_KV_HEADS = 2
ATTN_GROUP = ATTN_HEADS // ATTN_KV_HEADS
HEAD_DIM = 128
ROPE_AXIS_DIM = HEAD_DIM // 2
ROPE_THETA = 10000.0
Q_BLOCK = 128

GDN_HEADS = 8
GDN_DK = 128
GDN_DV = 128
GDN_CHUNK = 64
SHORT_CONV = 3

D_FF = 2816
FFN_CONV = 3

NORM_EPS = 1e-6

ATTN_Q_W = ATTN_HEADS * HEAD_DIM
ATTN_KV_W = ATTN_KV_HEADS * HEAD_DIM
GDN_QK_W = GDN_HEADS * GDN_DK
GDN_V_W = GDN_HEADS * GDN_DV
GDN_CONV_W = 2 * GDN_QK_W + GDN_V_W
IN_SPLITS = (ATTN_KV_W, ATTN_KV_W, GDN_CONV_W, 2 * GDN_HEADS, 2 * GDN_HEADS, ATTN_Q_W, GDN_V_W, 2 * D_MODEL)
CTX_COLS = 2 * ATTN_KV_W + GDN_CONV_W + 4 * GDN_HEADS
IN_COLS = CTX_COLS + ATTN_Q_W + GDN_V_W + 2 * D_MODEL

kernel_name = 'hybrid_gqa_gdn_convffn_prefix_block'


def rms_norm(x, gain=None):
    x32 = x.astype(jnp.float32)
    y = x32 * lax.rsqrt(jnp.mean(x32 * x32, axis=-1, keepdims=True) + NORM_EPS)
    if gain is not None:
        y = y * gain.astype(jnp.float32)
    return y.astype(x.dtype)


def l2_normalize(x):
    return x * lax.rsqrt(jnp.sum(x * x, axis=-1, keepdims=True) + NORM_EPS)


def modulate(h, shift, scale):
    return h * (1.0 + scale) + shift


def split_cols(p):
    parts, off = [], 0
    for size in IN_SPLITS:
        if off >= p.shape[-1]:
            break
        parts.append(p[..., off:off + size])
        off += size
    return parts


def dwconv_centred(x, w, b=None):
    width = w.shape[0]
    pad = width // 2
    n = x.shape[1]
    xp = jnp.pad(x, ((0, 0), (pad, pad), (0, 0)))
    y = xp[:, 0:n] * w[0]
    for j in range(1, width):
        y = y + xp[:, j:j + n] * w[j]
    return y if b is None else y + b


def axial_rope_tables(n):
    rows = n // GRID_W
    row_ids = jnp.broadcast_to(jnp.arange(rows, dtype=jnp.float32)[:, None], (rows, GRID_W)).reshape(n)
    col_ids = jnp.broadcast_to(jnp.arange(GRID_W, dtype=jnp.float32)[None, :], (rows, GRID_W)).reshape(n)
    inv_freq = ROPE_THETA ** (-jnp.arange(0, ROPE_AXIS_DIM, 2, dtype=jnp.float32) / ROPE_AXIS_DIM)
    ang = jnp.concatenate([row_ids[:, None] * inv_freq, col_ids[:, None] * inv_freq], axis=-1)
    ang = ang.reshape(n, 2, ROPE_AXIS_DIM // 2)
    return jnp.cos(ang), jnp.sin(ang)


def apply_axial_rope(x, cos, sin):
    x32 = x.astype(jnp.float32).reshape(*x.shape[:-1], 2, 2, ROPE_AXIS_DIM // 2)
    x1, x2 = x32[..., 0, :], x32[..., 1, :]
    cos = cos[None, :, None]
    sin = sin[None, :, None]
    out = jnp.stack([x1 * cos - x2 * sin, x2 * cos + x1 * sin], axis=-2)
    return out.reshape(x.shape).astype(x.dtype)


def sdpa_block(q, k, v):
    s = jnp.einsum('bqhgd,bkhd->bhgqk', q, k).astype(jnp.float32) * (HEAD_DIM ** -0.5)
    p = jax.nn.softmax(s, axis=-1).astype(v.dtype)
    return jnp.einsum('bhgqk,bkhd->bqhgd', p, v)


def latent_attention(q, k_all, v_all):
    b, n = q.shape[:2]
    nb = n // Q_BLOCK
    qb = jnp.moveaxis(q.reshape(b, nb, Q_BLOCK, *q.shape[2:]), 1, 0)
    o = lax.map(lambda q_blk: sdpa_block(q_blk, k_all, v_all), qb)
    return jnp.moveaxis(o, 0, 1).reshape(b, n, ATTN_Q_W)


def gdn_chunked(q, k, v, log_a, beta, s0, with_output):
    b, h, n, dk = q.shape
    dv = v.shape[-1]
    nc = n // GDN_CHUNK
    q = q.reshape(b, h, nc, GDN_CHUNK, dk)
    k = k.reshape(b, h, nc, GDN_CHUNK, dk)
    v = v.reshape(b, h, nc, GDN_CHUNK, dv)
    log_a = log_a.reshape(b, h, nc, GDN_CHUNK)
    beta = beta.reshape(b, h, nc, GDN_CHUNK)
    gam = jnp.cumsum(log_a, axis=-1)
    idx = jnp.arange(GDN_CHUNK)
    strict = idx[:, None] > idx[None, :]
    incl = idx[:, None] >= idx[None, :]
    dec = jnp.exp(jnp.where(incl, gam[..., :, None] - gam[..., None, :], -jnp.inf))
    kk = jnp.einsum('bhncd,bhnsd->bhncs', k, k)
    a_mat = jnp.where(strict, beta[..., :, None] * dec * kk, 0.0) + jnp.eye(GDN_CHUNK, dtype=jnp.float32)
    rhs = jnp.concatenate([beta[..., None] * v, (beta * jnp.exp(gam))[..., None] * k], axis=-1)
    uw = lax.linalg.triangular_solve(a_mat, rhs, left_side=True, lower=True, unit_diagonal=True)
    u, w = uw[..., :dv], uw[..., dv:]
    k_dec = k * jnp.exp(gam[..., -1:] - gam)[..., None]
    g_last = jnp.exp(gam[..., -1])
    xs = [u, w, k_dec, g_last]
    if with_output:
        q_dec = q * jnp.exp(gam)[..., None]
        p = dec * jnp.einsum('bhncd,bhnsd->bhncs', q, k)
        xs = xs + [q_dec, p]
    xs = tuple(jnp.moveaxis(t, 2, 0) for t in xs)

    def step(s, inp):
        u_c, w_c, kd_c, gl_c = inp[:4]
        delta = u_c - jnp.einsum('bhcd,bhde->bhce', w_c, s)
        s_new = gl_c[..., None, None] * s + jnp.einsum('bhcd,bhce->bhde', kd_c, delta)
        if with_output:
            qd_c, p_c = inp[4], inp[5]
            o = jnp.einsum('bhcd,bhde->bhce', qd_c, s) + jnp.einsum('bhcs,bhse->bhce', p_c, delta)
            return s_new, o
        return s_new, None

    s_fin, o = lax.scan(step, s0, xs)
    if with_output:
        o = jnp.moveaxis(o, 0, 2).reshape(b, h, n, dv)
    return o, s_fin


def gdn_prepare(qkv, db, da, conv_w, a_log, dt_bias):
    b, n, _ = qkv.shape
    qkv = jax.nn.silu(dwconv_centred(qkv, conv_w)).astype(jnp.float32)
    q, k, v = jnp.split(qkv, [GDN_QK_W, 2 * GDN_QK_W], axis=-1)
    q = l2_normalize(q.reshape(b, n, GDN_HEADS, GDN_DK)) * (GDN_DK ** -0.5)
    k = l2_normalize(k.reshape(b, n, GDN_HEADS, GDN_DK))
    v = v.reshape(b, n, GDN_HEADS, GDN_DV)
    q, k, v = (t.transpose(0, 2, 1, 3) for t in (q, k, v))
    beta = jax.nn.sigmoid(db.astype(jnp.float32)).reshape(b, n, 2, GDN_HEADS).transpose(2, 0, 3, 1)
    da = da.astype(jnp.float32).reshape(b, n, 2, GDN_HEADS).transpose(2, 0, 3, 1)
    log_a = -jnp.exp(a_log.astype(jnp.float32))[:, None, :, None] * jax.nn.softplus(
        da + dt_bias.astype(jnp.float32)[:, None, :, None])
    return q, k, v, log_a, beta


def gdn_bidirectional(ctx_in, lat_in, with_ctx_output):
    qc, kc, vc, lac, bc = ctx_in
    ql, kl, vl, lal, bl = lat_in
    s0 = jnp.zeros((ql.shape[0], GDN_HEADS, GDN_DK, GDN_DV), jnp.float32)
    o_lat, o_ctx = None, None
    for d in range(2):
        rev = (lambda t: jnp.flip(t, axis=2)) if d == 1 else (lambda t: t)
        oc, sc = gdn_chunked(rev(qc), rev(kc), rev(vc), rev(lac[d]), rev(bc[d]), s0, with_ctx_output)
        ol, _ = gdn_chunked(rev(ql), rev(kl), rev(vl), rev(lal[d]), rev(bl[d]), sc, True)
        o_lat = rev(ol) if o_lat is None else o_lat + rev(ol)
        if with_ctx_output:
            o_ctx = rev(oc) if o_ctx is None else o_ctx + rev(oc)
    return o_lat, o_ctx


def gdn_output(o, z, norm_w):
    b, n = z.shape[:2]
    o = rms_norm(o.transpose(0, 2, 1, 3), norm_w)
    y = o * jax.nn.silu(z.astype(jnp.float32).reshape(b, n, GDN_HEADS, GDN_DV))
    return y.reshape(b, n, GDN_V_W).astype(z.dtype)


def merge_branches(attn, gdn, gates, w_pa, w_pd, w_out):
    g_a, g_d = jnp.split(gates, 2, axis=-1)
    y = jax.nn.sigmoid(g_a) * (attn @ w_pa) + jax.nn.sigmoid(g_d) * (gdn @ w_pd)
    return y @ w_out


def conv_ffn(h, w_up, conv_w, conv_b, w_down):
    u = dwconv_centred(h @ w_up, conv_w, conv_b)
    g, val = jnp.split(u, 2, axis=-1)
    return (jax.nn.silu(g) * val) @ w_down


def hybrid_layer(x, ctx, c, c_ctx, w_mod, b_mod, w_in, q_norm_w, k_norm_w, conv_qkv_w, a_log, dt_bias,
                 gdn_norm_w, w_pa, w_pd, w_out, w_up, ffn_conv_w, ffn_conv_b, w_down, update_ctx):
    b, n, _ = x.shape
    cl = ctx.shape[1]
    mod_lat = jax.nn.silu(c) @ w_mod + b_mod
    mod_ctx = jax.nn.silu(c_ctx) @ w_mod + b_mod
    sh1, sc1, g1, sh2, sc2, g2 = [m[:, None, :] for m in jnp.split(mod_lat, 6, axis=-1)]
    csh1, csc1, cg1, csh2, csc2, cg2 = jnp.split(mod_ctx, 6, axis=-1)

    hx = modulate(rms_norm(x), sh1, sc1)
    hc = modulate(rms_norm(ctx), csh1, csc1)
    ak_x, av_x, qkv_x, db_x, da_x, aq_x, z_x, gate_x = split_cols(hx @ w_in)
    ak_c, av_c, qkv_c, db_c, da_c, *rest_c = split_cols(hc @ (w_in if update_ctx else w_in[:, :CTX_COLS]))

    cos, sin = axial_rope_tables(n)
    q_x = apply_axial_rope(rms_norm(aq_x.reshape(b, n, ATTN_HEADS, HEAD_DIM), q_norm_w), cos, sin)
    q_x = q_x.reshape(b, n, ATTN_KV_HEADS, ATTN_GROUP, HEAD_DIM)
    k_x = apply_axial_rope(rms_norm(ak_x.reshape(b, n, ATTN_KV_HEADS, HEAD_DIM), k_norm_w), cos, sin)
    v_x = av_x.reshape(b, n, ATTN_KV_HEADS, HEAD_DIM)
    k_c = rms_norm(ak_c.reshape(b, cl, ATTN_KV_HEADS, HEAD_DIM), k_norm_w)
    v_c = av_c.reshape(b, cl, ATTN_KV_HEADS, HEAD_DIM)
    attn_x = latent_attention(q_x, jnp.concatenate([k_c, k_x], axis=1), jnp.concatenate([v_c, v_x], axis=1))

    gdn_x_in = gdn_prepare(qkv_x, db_x, da_x, conv_qkv_w, a_log, dt_bias)
    gdn_c_in = gdn_prepare(qkv_c, db_c, da_c, conv_qkv_w, a_log, dt_bias)
    o_x, o_c = gdn_bidirectional(gdn_c_in, gdn_x_in, update_ctx)
    gdn_x = gdn_output(o_x, z_x, gdn_norm_w)

    x = x + g1 * merge_branches(attn_x, gdn_x, gate_x, w_pa, w_pd, w_out)
    x = x + g2 * conv_ffn(modulate(rms_norm(x), sh2, sc2), w_up, ffn_conv_w, ffn_conv_b, w_down)

    if update_ctx:
        aq_c, z_c, gate_c = rest_c
        q_c = rms_norm(aq_c.reshape(b, cl, ATTN_HEADS, HEAD_DIM), q_norm_w)
        q_c = q_c.reshape(b, cl, ATTN_KV_HEADS, ATTN_GROUP, HEAD_DIM)
        attn_c = sdpa_block(q_c, k_c, v_c).reshape(b, cl, ATTN_Q_W)
        gdn_c = gdn_output(o_c, z_c, gdn_norm_w)
        ctx = ctx + cg1 * merge_branches(attn_c, gdn_c, gate_c, w_pa, w_pd, w_out)
        ctx = ctx + cg2 * conv_ffn(modulate(rms_norm(ctx), csh2, csc2), w_up, ffn_conv_w, ffn_conv_b, w_down)
    return x, ctx


def _fwd_setup_inputs(seed: int = 0) -> dict:
    key = jax.random.key(seed)
    ks = jax.random.split(key, 22)
    f32 = jnp.float32

    def dense(k, shape, fan_in, s=1.0):
        return s * (fan_in ** -0.5) * jax.random.normal(k, shape, f32)

    dt = jnp.exp(jax.random.uniform(ks[10], (DEPTH, 2, GDN_HEADS), f32,
                                    minval=math.log(1e-3), maxval=math.log(1e-1)))
    return {
        'x': jax.random.normal(ks[0], (BATCH, SEQ, D_MODEL), f32),
        'c': jax.random.normal(ks[1], (BATCH, D_MODEL), f32),
        'ctx': jax.random.normal(ks[2], (BATCH, CTX_LEN, D_MODEL), f32),
        'c_ctx': jax.random.normal(ks[3], (D_MODEL,), f32),
        'w_mod': dense(ks[4], (DEPTH, D_MODEL, 6 * D_MODEL), D_MODEL, 0.5),
        'b_mod': 0.02 * jax.random.normal(ks[5], (DEPTH, 6 * D_MODEL), f32),
        'w_in': dense(ks[6], (DEPTH, D_MODEL, IN_COLS), D_MODEL),
        'q_norm_w': 1.0 + 0.05 * jax.random.normal(ks[7], (DEPTH, HEAD_DIM), f32),
        'k_norm_w': 1.0 + 0.05 * jax.random.normal(ks[8], (DEPTH, HEAD_DIM), f32),
        'conv_qkv_w': dense(ks[9], (DEPTH, SHORT_CONV, GDN_CONV_W), SHORT_CONV),
        'a_log': jnp.log(jax.random.uniform(ks[11], (DEPTH, 2, GDN_HEADS), f32, minval=1.0, maxval=16.0)),
        'dt_bias': dt + jnp.log(-jnp.expm1(-dt)),
        'gdn_norm_w': 1.0 + 0.05 * jax.random.normal(ks[12], (DEPTH, GDN_DV), f32),
        'w_pa': dense(ks[13], (DEPTH, ATTN_Q_W, D_MODEL), ATTN_Q_W),
        'w_pd': dense(ks[14], (DEPTH, GDN_V_W, D_MODEL), GDN_V_W),
        'w_out': dense(ks[15], (DEPTH, D_MODEL, D_MODEL), D_MODEL),
        'w_up': dense(ks[16], (DEPTH, D_MODEL, 2 * D_FF), D_MODEL),
        'ffn_conv_w': dense(ks[17], (DEPTH, FFN_CONV, 2 * D_FF), FFN_CONV),
        'ffn_conv_b': 0.02 * jax.random.normal(ks[18], (DEPTH, 2 * D_FF), f32),
        'w_down': dense(ks[19], (DEPTH, D_FF, D_MODEL), D_FF),
        'final_norm_w': 1.0 + 0.05 * jax.random.normal(ks[20], (D_MODEL,), f32),
    }


def _fwd_reference(x, c, ctx, c_ctx, w_mod, b_mod, w_in, q_norm_w, k_norm_w, conv_qkv_w, a_log, dt_bias,
              gdn_norm_w, w_pa, w_pd, w_out, w_up, ffn_conv_w, ffn_conv_b, w_down, final_norm_w):
    for layer in range(DEPTH):
        x, ctx = hybrid_layer(
            x, ctx, c, c_ctx, w_mod[layer], b_mod[layer], w_in[layer], q_norm_w[layer], k_norm_w[layer],
            conv_qkv_w[layer], a_log[layer], dt_bias[layer], gdn_norm_w[layer], w_pa[layer], w_pd[layer],
            w_out[layer], w_up[layer], ffn_conv_w[layer], ffn_conv_b[layer], w_down[layer],
            update_ctx=layer < DEPTH - 1)
    return rms_norm(x, final_norm_w)


import jax as _jax
import jax.numpy as _jnp

TWIN_FORMAT = 'train_step'
FWD_PARAMS = ['x', 'c', 'ctx', 'c_ctx', 'w_mod', 'b_mod', 'w_in', 'q_norm_w', 'k_norm_w', 'conv_qkv_w', 'a_log', 'dt_bias', 'gdn_norm_w', 'w_pa', 'w_pd', 'w_out', 'w_up', 'ffn_conv_w', 'ffn_conv_b', 'w_down', 'final_norm_w']
TWIN_WEIGHTS = ['c_ctx', 'w_mod', 'b_mod', 'w_in', 'q_norm_w', 'k_norm_w', 'conv_qkv_w', 'a_log', 'dt_bias', 'gdn_norm_w', 'w_pa', 'w_pd', 'w_out', 'w_up', 'ffn_conv_w', 'ffn_conv_b', 'w_down', 'final_norm_w']
TWIN_DIFF_INPUT = 'x'
TWIN_INPUTS = ['x', 'c', 'ctx', 'c_ctx', 'w_mod', 'b_mod', 'w_in', 'q_norm_w', 'k_norm_w', 'conv_qkv_w', 'a_log', 'dt_bias', 'gdn_norm_w', 'w_pa', 'w_pd', 'w_out', 'w_up', 'ffn_conv_w', 'ffn_conv_b', 'w_down', 'final_norm_w', 'loss_target', 'm_c_ctx', 'm_w_mod', 'm_b_mod', 'm_w_in', 'm_q_norm_w', 'm_k_norm_w', 'm_conv_qkv_w', 'm_a_log', 'm_dt_bias', 'm_gdn_norm_w', 'm_w_pa', 'm_w_pd', 'm_w_out', 'm_w_up', 'm_ffn_conv_w', 'm_ffn_conv_b', 'm_w_down', 'm_final_norm_w', 'v_c_ctx', 'v_w_mod', 'v_b_mod', 'v_w_in', 'v_q_norm_w', 'v_k_norm_w', 'v_conv_qkv_w', 'v_a_log', 'v_dt_bias', 'v_gdn_norm_w', 'v_w_pa', 'v_w_pd', 'v_w_out', 'v_w_up', 'v_ffn_conv_w', 'v_ffn_conv_b', 'v_w_down', 'v_final_norm_w']
TWIN_OUTPUTS = ['loss', 'grad_x', 'grad_c_ctx', 'grad_w_mod', 'grad_b_mod', 'grad_w_in', 'grad_q_norm_w', 'grad_k_norm_w', 'grad_conv_qkv_w', 'grad_a_log', 'grad_dt_bias', 'grad_gdn_norm_w', 'grad_w_pa', 'grad_w_pd', 'grad_w_out', 'grad_w_up', 'grad_ffn_conv_w', 'grad_ffn_conv_b', 'grad_w_down', 'grad_final_norm_w', 'delta_c_ctx', 'delta_w_mod', 'delta_b_mod', 'delta_w_in', 'delta_q_norm_w', 'delta_k_norm_w', 'delta_conv_qkv_w', 'delta_a_log', 'delta_dt_bias', 'delta_gdn_norm_w', 'delta_w_pa', 'delta_w_pd', 'delta_w_out', 'delta_w_up', 'delta_ffn_conv_w', 'delta_ffn_conv_b', 'delta_w_down', 'delta_final_norm_w', 'new_m_c_ctx', 'new_m_w_mod', 'new_m_b_mod', 'new_m_w_in', 'new_m_q_norm_w', 'new_m_k_norm_w', 'new_m_conv_qkv_w', 'new_m_a_log', 'new_m_dt_bias', 'new_m_gdn_norm_w', 'new_m_w_pa', 'new_m_w_pd', 'new_m_w_out', 'new_m_w_up', 'new_m_ffn_conv_w', 'new_m_ffn_conv_b', 'new_m_w_down', 'new_m_final_norm_w', 'new_v_c_ctx', 'new_v_w_mod', 'new_v_b_mod', 'new_v_w_in', 'new_v_q_norm_w', 'new_v_k_norm_w', 'new_v_conv_qkv_w', 'new_v_a_log', 'new_v_dt_bias', 'new_v_gdn_norm_w', 'new_v_w_pa', 'new_v_w_pd', 'new_v_w_out', 'new_v_w_up', 'new_v_ffn_conv_w', 'new_v_ffn_conv_b', 'new_v_w_down', 'new_v_final_norm_w']
TWIN_LEAF_KINDS = {'loss': 'loss', 'grad_x': 'grad_x', 'grad_c_ctx': 'grad_w', 'grad_w_mod': 'grad_w', 'grad_b_mod': 'grad_w', 'grad_w_in': 'grad_w', 'grad_q_norm_w': 'grad_w', 'grad_k_norm_w': 'grad_w', 'grad_conv_qkv_w': 'grad_w', 'grad_a_log': 'grad_w', 'grad_dt_bias': 'grad_w', 'grad_gdn_norm_w': 'grad_w', 'grad_w_pa': 'grad_w', 'grad_w_pd': 'grad_w', 'grad_w_out': 'grad_w', 'grad_w_up': 'grad_w', 'grad_ffn_conv_w': 'grad_w', 'grad_ffn_conv_b': 'grad_w', 'grad_w_down': 'grad_w', 'grad_final_norm_w': 'grad_w', 'delta_c_ctx': 'delta_w', 'delta_w_mod': 'delta_w', 'delta_b_mod': 'delta_w', 'delta_w_in': 'delta_w', 'delta_q_norm_w': 'delta_w', 'delta_k_norm_w': 'delta_w', 'delta_conv_qkv_w': 'delta_w', 'delta_a_log': 'delta_w', 'delta_dt_bias': 'delta_w', 'delta_gdn_norm_w': 'delta_w', 'delta_w_pa': 'delta_w', 'delta_w_pd': 'delta_w', 'delta_w_out': 'delta_w', 'delta_w_up': 'delta_w', 'delta_ffn_conv_w': 'delta_w', 'delta_ffn_conv_b': 'delta_w', 'delta_w_down': 'delta_w', 'delta_final_norm_w': 'delta_w', 'new_m_c_ctx': 'new_m', 'new_m_w_mod': 'new_m', 'new_m_b_mod': 'new_m', 'new_m_w_in': 'new_m', 'new_m_q_norm_w': 'new_m', 'new_m_k_norm_w': 'new_m', 'new_m_conv_qkv_w': 'new_m', 'new_m_a_log': 'new_m', 'new_m_dt_bias': 'new_m', 'new_m_gdn_norm_w': 'new_m', 'new_m_w_pa': 'new_m', 'new_m_w_pd': 'new_m', 'new_m_w_out': 'new_m', 'new_m_w_up': 'new_m', 'new_m_ffn_conv_w': 'new_m', 'new_m_ffn_conv_b': 'new_m', 'new_m_w_down': 'new_m', 'new_m_final_norm_w': 'new_m', 'new_v_c_ctx': 'new_v', 'new_v_w_mod': 'new_v', 'new_v_b_mod': 'new_v', 'new_v_w_in': 'new_v', 'new_v_q_norm_w': 'new_v', 'new_v_k_norm_w': 'new_v', 'new_v_conv_qkv_w': 'new_v', 'new_v_a_log': 'new_v', 'new_v_dt_bias': 'new_v', 'new_v_gdn_norm_w': 'new_v', 'new_v_w_pa': 'new_v', 'new_v_w_pd': 'new_v', 'new_v_w_out': 'new_v', 'new_v_w_up': 'new_v', 'new_v_ffn_conv_w': 'new_v', 'new_v_ffn_conv_b': 'new_v', 'new_v_w_down': 'new_v', 'new_v_final_norm_w': 'new_v'}


def _forward(args):
    return _fwd_reference(*[args[k] for k in FWD_PARAMS])


def _output_shape():
    out = _jax.eval_shape(lambda: _forward(_fwd_setup_inputs(0)))
    return out.shape, out.dtype

N_MICROBATCH = 1
ADAM_LR = 0.001
ADAM_B1 = 0.9
ADAM_B2 = 0.999
ADAM_EPS = 1e-08
ADAM_WD = 0.01
ADAM_STEP = 10
PER_EXAMPLE_BATCH_AXIS = {'x': 0, 'c': 0, 'ctx': 0, 'loss_target': 0}
SHARED_INPUTS = []
_WEIGHT_DTYPES = {'c_ctx': _jnp.float32, 'w_mod': _jnp.float32, 'b_mod': _jnp.float32, 'w_in': _jnp.float32, 'q_norm_w': _jnp.float32, 'k_norm_w': _jnp.float32, 'conv_qkv_w': _jnp.float32, 'a_log': _jnp.float32, 'dt_bias': _jnp.float32, 'gdn_norm_w': _jnp.float32, 'w_pa': _jnp.float32, 'w_pd': _jnp.float32, 'w_out': _jnp.float32, 'w_up': _jnp.float32, 'ffn_conv_w': _jnp.float32, 'ffn_conv_b': _jnp.float32, 'w_down': _jnp.float32, 'final_norm_w': _jnp.float32}
MOMENT_SCALE = {'c_ctx': 3.473190e-03, 'w_mod': 3.883999e-02, 'b_mod': 6.701305e-02, 'w_in': 9.061312e-03, 'q_norm_w': 5.448927e-03, 'k_norm_w': 5.463688e-03, 'conv_qkv_w': 1.076402e-02, 'a_log': 3.435875e-02, 'dt_bias': 3.265505e-02, 'gdn_norm_w': 5.235156e-02, 'w_pa': 6.098525e-03, 'w_pd': 1.446503e-02, 'w_out': 1.569235e-02, 'w_up': 1.740682e-02, 'ffn_conv_w': 1.720822e-02, 'ffn_conv_b': 1.572280e-02, 'w_down': 2.848334e-02, 'final_norm_w': 1.606866e+01}


def _to_microbatches(a, axis):
    t = _jnp.moveaxis(a, axis, 0)
    t = t.reshape((N_MICROBATCH, t.shape[0] // N_MICROBATCH) + t.shape[1:])
    return _jnp.moveaxis(t, 1, axis + 1)


def setup_inputs(seed: int = 0) -> dict:
    inp = _fwd_setup_inputs(seed)
    key = _jax.random.fold_in(_jax.random.key(seed), 7919)
    shape, _ = _output_shape()
    out = dict(inp)
    out["loss_target"] = _jax.random.normal(_jax.random.fold_in(key, 0), shape, _jnp.float32)
    for i, name in enumerate(TWIN_WEIGHTS):
        w = inp[name].astype(_jnp.float32)
        if MOMENT_SCALE is None:
            s = _jnp.sqrt(_jnp.mean(_jnp.square(w)) + 1e-30)
        else:
            s = MOMENT_SCALE[name]
        km, kv = _jax.random.split(_jax.random.fold_in(key, i + 1))
        out[name] = w
        out["m_" + name] = s * _jax.random.normal(km, w.shape, _jnp.float32)
        out["v_" + name] = (s * s) * _jax.random.uniform(kv, w.shape, _jnp.float32, 0.5, 1.5)
    if N_MICROBATCH > 1:
        for name, axis in PER_EXAMPLE_BATCH_AXIS.items():
            out[name] = _to_microbatches(out[name], axis)
    return {'x': out['x'], 'c': out['c'], 'ctx': out['ctx'], 'c_ctx': out['c_ctx'], 'w_mod': out['w_mod'], 'b_mod': out['b_mod'], 'w_in': out['w_in'], 'q_norm_w': out['q_norm_w'], 'k_norm_w': out['k_norm_w'], 'conv_qkv_w': out['conv_qkv_w'], 'a_log': out['a_log'], 'dt_bias': out['dt_bias'], 'gdn_norm_w': out['gdn_norm_w'], 'w_pa': out['w_pa'], 'w_pd': out['w_pd'], 'w_out': out['w_out'], 'w_up': out['w_up'], 'ffn_conv_w': out['ffn_conv_w'], 'ffn_conv_b': out['ffn_conv_b'], 'w_down': out['w_down'], 'final_norm_w': out['final_norm_w'], 'loss_target': out['loss_target'], 'm_c_ctx': out['m_c_ctx'], 'm_w_mod': out['m_w_mod'], 'm_b_mod': out['m_b_mod'], 'm_w_in': out['m_w_in'], 'm_q_norm_w': out['m_q_norm_w'], 'm_k_norm_w': out['m_k_norm_w'], 'm_conv_qkv_w': out['m_conv_qkv_w'], 'm_a_log': out['m_a_log'], 'm_dt_bias': out['m_dt_bias'], 'm_gdn_norm_w': out['m_gdn_norm_w'], 'm_w_pa': out['m_w_pa'], 'm_w_pd': out['m_w_pd'], 'm_w_out': out['m_w_out'], 'm_w_up': out['m_w_up'], 'm_ffn_conv_w': out['m_ffn_conv_w'], 'm_ffn_conv_b': out['m_ffn_conv_b'], 'm_w_down': out['m_w_down'], 'm_final_norm_w': out['m_final_norm_w'], 'v_c_ctx': out['v_c_ctx'], 'v_w_mod': out['v_w_mod'], 'v_b_mod': out['v_b_mod'], 'v_w_in': out['v_w_in'], 'v_q_norm_w': out['v_q_norm_w'], 'v_k_norm_w': out['v_k_norm_w'], 'v_conv_qkv_w': out['v_conv_qkv_w'], 'v_a_log': out['v_a_log'], 'v_dt_bias': out['v_dt_bias'], 'v_gdn_norm_w': out['v_gdn_norm_w'], 'v_w_pa': out['v_w_pa'], 'v_w_pd': out['v_w_pd'], 'v_w_out': out['v_w_out'], 'v_w_up': out['v_w_up'], 'v_ffn_conv_w': out['v_ffn_conv_w'], 'v_ffn_conv_b': out['v_ffn_conv_b'], 'v_w_down': out['v_w_down'], 'v_final_norm_w': out['v_final_norm_w']}


def _loss(weights, diff, rest, loss_target):
    with _jax.named_scope("forward"):
        args = {**rest, TWIN_DIFF_INPUT: diff, **{k: w.astype(_WEIGHT_DTYPES[k]) for k, w in weights.items()}}
        y = _forward(args)
    with _jax.named_scope("loss_head"):
        err = _jnp.square(y.astype(_jnp.float32) - loss_target)
        return 0.5 * _jnp.sum(_jnp.mean(err, axis=-1)) if err.ndim else 0.5 * err


def _adamw(w, g, m, v):
    m = ADAM_B1 * m + (1.0 - ADAM_B1) * g
    v = ADAM_B2 * v + (1.0 - ADAM_B2) * _jnp.square(g)
    m_hat = m / (1.0 - ADAM_B1 ** ADAM_STEP)
    v_hat = v / (1.0 - ADAM_B2 ** ADAM_STEP)
    delta = -ADAM_LR * (m_hat / (_jnp.sqrt(v_hat) + ADAM_EPS) + ADAM_WD * w)
    return delta, m, v


def reference(x, c, ctx, c_ctx, w_mod, b_mod, w_in, q_norm_w, k_norm_w, conv_qkv_w, a_log, dt_bias, gdn_norm_w, w_pa, w_pd, w_out, w_up, ffn_conv_w, ffn_conv_b, w_down, final_norm_w, loss_target, m_c_ctx, m_w_mod, m_b_mod, m_w_in, m_q_norm_w, m_k_norm_w, m_conv_qkv_w, m_a_log, m_dt_bias, m_gdn_norm_w, m_w_pa, m_w_pd, m_w_out, m_w_up, m_ffn_conv_w, m_ffn_conv_b, m_w_down, m_final_norm_w, v_c_ctx, v_w_mod, v_b_mod, v_w_in, v_q_norm_w, v_k_norm_w, v_conv_qkv_w, v_a_log, v_dt_bias, v_gdn_norm_w, v_w_pa, v_w_pd, v_w_out, v_w_up, v_ffn_conv_w, v_ffn_conv_b, v_w_down, v_final_norm_w):
    given = dict(x=x, c=c, ctx=ctx, c_ctx=c_ctx, w_mod=w_mod, b_mod=b_mod, w_in=w_in, q_norm_w=q_norm_w, k_norm_w=k_norm_w, conv_qkv_w=conv_qkv_w, a_log=a_log, dt_bias=dt_bias, gdn_norm_w=gdn_norm_w, w_pa=w_pa, w_pd=w_pd, w_out=w_out, w_up=w_up, ffn_conv_w=ffn_conv_w, ffn_conv_b=ffn_conv_b, w_down=w_down, final_norm_w=final_norm_w, loss_target=loss_target, m_c_ctx=m_c_ctx, m_w_mod=m_w_mod, m_b_mod=m_b_mod, m_w_in=m_w_in, m_q_norm_w=m_q_norm_w, m_k_norm_w=m_k_norm_w, m_conv_qkv_w=m_conv_qkv_w, m_a_log=m_a_log, m_dt_bias=m_dt_bias, m_gdn_norm_w=m_gdn_norm_w, m_w_pa=m_w_pa, m_w_pd=m_w_pd, m_w_out=m_w_out, m_w_up=m_w_up, m_ffn_conv_w=m_ffn_conv_w, m_ffn_conv_b=m_ffn_conv_b, m_w_down=m_w_down, m_final_norm_w=m_final_norm_w, v_c_ctx=v_c_ctx, v_w_mod=v_w_mod, v_b_mod=v_b_mod, v_w_in=v_w_in, v_q_norm_w=v_q_norm_w, v_k_norm_w=v_k_norm_w, v_conv_qkv_w=v_conv_qkv_w, v_a_log=v_a_log, v_dt_bias=v_dt_bias, v_gdn_norm_w=v_gdn_norm_w, v_w_pa=v_w_pa, v_w_pd=v_w_pd, v_w_out=v_w_out, v_w_up=v_w_up, v_ffn_conv_w=v_ffn_conv_w, v_ffn_conv_b=v_ffn_conv_b, v_w_down=v_w_down, v_final_norm_w=v_final_norm_w)
    weights = {n: given[n] for n in TWIN_WEIGHTS}
    shared = {n: given[n] for n in SHARED_INPUTS}
    per_example = {n: given[n] for n in ['x', 'c', 'ctx']}
    grad_fn = _jax.value_and_grad(_loss, argnums=(0, 1))

    def one_microbatch(ex, loss_target):
        ex = dict(ex)
        diff = ex.pop(TWIN_DIFF_INPUT)
        return grad_fn(weights, diff, {**shared, **ex}, loss_target)

    if N_MICROBATCH == 1:
        loss, (grad_w, grad_x) = one_microbatch(per_example, given["loss_target"])
    else:
        def body(carry, xs):
            loss_sum, grad_sum = carry
            l_k, (gw_k, gx_k) = one_microbatch(xs[0], xs[1])
            with _jax.named_scope("update"):
                return (loss_sum + l_k, _jax.tree.map(_jnp.add, grad_sum, gw_k)), gx_k

        init = (_jnp.zeros((), _jnp.float32), _jax.tree.map(_jnp.zeros_like, weights))
        (loss, grad_w), grad_x = _jax.lax.scan(body, init, (per_example, given["loss_target"]))
    with _jax.named_scope("update"):
        delta_w, new_m, new_v = {}, {}, {}
        for n in TWIN_WEIGHTS:
            delta_w[n], new_m[n], new_v[n] = _adamw(weights[n], grad_w[n], given["m_" + n], given["v_" + n])
    return (loss, grad_x, *[grad_w[n] for n in TWIN_WEIGHTS], *[delta_w[n] for n in TWIN_WEIGHTS],
            *[new_m[n] for n in TWIN_WEIGHTS], *[new_v[n] for n in TWIN_WEIGHTS])
```

```python
import functools
import math

import jax
import jax.numpy as jnp
import numpy as np
from jax import lax
from jax.experimental import pallas as pl
from jax.experimental.pallas import tpu as pltpu

F32 = jnp.float32
BF16 = jnp.bfloat16
HIGHEST = lax.Precision.HIGHEST
MESH = pl.DeviceIdType.MESH

D_MODEL = 1024
GRID_W = 64
ATTN_HEADS = 8
ATTN_KV_HEADS = 2
ATTN_GROUP = ATTN_HEADS // ATTN_KV_HEADS
HEAD_DIM = 128
ROPE_THETA = 10000.0
GDN_HEADS = 8
GDN_CHUNK = 64
D_FF = 2816
NORM_EPS = 1e-6
KV_W = ATTN_KV_HEADS * HEAD_DIM
Q_W = ATTN_HEADS * HEAD_DIM
GDN_W = GDN_HEADS * HEAD_DIM
CONV_W = 3 * GDN_W
MOD_W = 6 * D_MODEL
IN_COLS = 2 * KV_W + CONV_W + 4 * GDN_HEADS + Q_W + GDN_W + 2 * D_MODEL
IN_MAIN = IN_COLS - 4 * GDN_HEADS
SMALL_AT = 2 * KV_W + CONV_W
N_CHIPS = 4
N_DEV = 8

ADAM_LR = 0.001
ADAM_B1 = 0.9
ADAM_B2 = 0.999
ADAM_EPS = 1e-08
ADAM_WD = 0.01
ADAM_STEP = 10

VMEM_LIMIT = 48 * 1024 * 1024


def _pick(dim, prefs):
    for p in prefs:
        if p <= dim and dim % p == 0:
            return p
    return dim


_DIMS = {
    "nn": (((1,), (0,)), ((), ())),
    "nt": (((1,), (1,)), ((), ())),
    "tn": (((0,), (0,)), ((), ())),
}


def _matmul(a, b, mode, name):
    if mode == "nn":
        (m, k), (_, n) = a.shape, b.shape
    elif mode == "nt":
        (m, k), (n, _) = a.shape, b.shape
    else:
        (k, m), (_, n) = a.shape, b.shape
    tm = _pick(m, (512, 384, 256, 128))
    tn = _pick(n, (1536, 1408, 1024, 768, 512, 256, 128))
    tk = _pick(k, (1024, 1408, 768, 512, 256, 128))
    nk = k // tk
    if mode == "tn":
        a_spec = pl.BlockSpec((tk, tm), lambda i, j, l: (l, i))
    else:
        a_spec = pl.BlockSpec((tm, tk), lambda i, j, l: (i, l))
    if mode == "nt":
        b_spec = pl.BlockSpec((tn, tk), lambda i, j, l: (j, l))
    else:
        b_spec = pl.BlockSpec((tk, tn), lambda i, j, l: (l, j))
    dims = _DIMS[mode]

    def body(a_ref, b_ref, o_ref, acc_ref):
        l = pl.program_id(2)

        @pl.when(l == 0)
        def _():
            acc_ref[...] = jnp.zeros_like(acc_ref)

        acc_ref[...] += lax.dot_general(a_ref[...].astype(BF16), b_ref[...].astype(BF16), dims,
                                        preferred_element_type=F32)

        @pl.when(l == nk - 1)
        def _():
            o_ref[...] = acc_ref[...]

    return pl.pallas_call(
        body,
        name=name,
        grid=(m // tm, n // tn, nk),
        in_specs=[a_spec, b_spec],
        out_specs=pl.BlockSpec((tm, tn), lambda i, j, l: (i, j)),
        out_shape=jax.ShapeDtypeStruct((m, n), F32),
        scratch_shapes=[pltpu.VMEM((tm, tn), F32)],
        compiler_params=pltpu.CompilerParams(dimension_semantics=("parallel", "parallel", "arbitrary"),
                                             vmem_limit_bytes=VMEM_LIMIT),
    )(a, b)


@functools.partial(jax.custom_vjp, nondiff_argnums=(3,))
def pmm(a, w, wz, name):
    del wz
    return _matmul(a, w, "nn", name + "_f")


def _pmm_fwd(a, w, wz, name):
    del wz
    return _matmul(a, w, "nn", name + "_f"), (a, w)


def _pmm_bwd(name, res, g):
    a, w = res
    da = _matmul(g, w, "nt", name + "_da")
    if a.shape[0] < 128:
        pad = 128 - a.shape[0]
        at = jnp.pad(a.T, ((0, 0), (0, pad)))
        gp = jnp.pad(g, ((0, pad), (0, 0)))
        dw = _matmul(at, gp, "nn", name + "_dw")
    else:
        dw = _matmul(a, g, "tn", name + "_dw")
    return da, jnp.zeros_like(w), dw


pmm.defvjp(_pmm_fwd, _pmm_bwd)


def rowop(fn, name, rows, bcs=(), crows=(), cbcs=(), tr=256):
    rows, bcs, crows, cbcs = tuple(rows), tuple(bcs), tuple(crows), tuple(cbcs)
    n_rows = rows[0].shape[0]
    tr = _pick(n_rows, (tr, 128, 64, 32, 16, 8))
    nr, nb, ncr, ncb = len(rows), len(bcs), len(crows), len(cbcs)
    n_in = nr + nb + ncr + ncb
    grid = (n_rows // tr,)

    def blk(arr):
        return jax.ShapeDtypeStruct((tr, arr.shape[1]), arr.dtype)

    def row_spec(arr):
        return pl.BlockSpec((tr, arr.shape[1]), lambda i: (i, 0))

    def bc_spec(arr):
        return pl.BlockSpec(arr.shape, lambda i: (0, 0))

    out_blk = jax.eval_shape(fn, *[blk(r) for r in rows], *bcs, *[blk(r) for r in crows], *cbcs)
    n_out = len(out_blk)
    out_shape = tuple(jax.ShapeDtypeStruct((n_rows, o.shape[1]), o.dtype) for o in out_blk)
    in_specs = ([row_spec(r) for r in rows] + [bc_spec(b) for b in bcs]
                + [row_spec(r) for r in crows] + [bc_spec(b) for b in cbcs])

    def order(vals):
        return vals

    def fwd_call(args):
        def body(*refs):
            vals = [r[...] for r in refs[:n_in]]
            res = fn(*order(vals))
            for o_ref, r in zip(refs[n_in:], res):
                o_ref[...] = r

        return pl.pallas_call(
            body, name=name + "_f", grid=grid, in_specs=in_specs,
            out_specs=[row_spec(o) for o in out_shape], out_shape=out_shape,
            compiler_params=pltpu.CompilerParams(dimension_semantics=("parallel",), vmem_limit_bytes=VMEM_LIMIT),
        )(*args)

    def bwd_call(args, cts):
        def body(*refs):
            vals = [r[...] for r in refs[:n_in]]
            ct_refs = refs[n_in:n_in + n_out]
            d_rows = refs[n_in + n_out:n_in + n_out + nr]
            d_bcs = refs[n_in + n_out + nr:]
            consts = vals[nr + nb:]
            _, vjp = jax.vjp(lambda *p: fn(*p, *consts), *vals[:nr + nb])
            grads = vjp(tuple(c[...] for c in ct_refs))
            for ref, g in zip(d_rows, grads[:nr]):
                ref[...] = g

            @pl.when(pl.program_id(0) == 0)
            def _():
                for ref in d_bcs:
                    ref[...] = jnp.zeros_like(ref)

            for ref, g in zip(d_bcs, grads[nr:]):
                ref[...] += g

        d_shape = tuple(jax.ShapeDtypeStruct(r.shape, r.dtype) for r in rows + bcs)
        return pl.pallas_call(
            body, name=name + "_b", grid=grid,
            in_specs=in_specs + [row_spec(o) for o in out_shape],
            out_specs=[row_spec(r) for r in rows] + [bc_spec(b) for b in bcs], out_shape=d_shape,
            compiler_params=pltpu.CompilerParams(dimension_semantics=("arbitrary",), vmem_limit_bytes=VMEM_LIMIT),
        )(*args, *cts)

    @jax.custom_vjp
    def op(diff, const):
        return fwd_call(diff + const)

    def op_fwd(diff, const):
        return fwd_call(diff + const), (diff, const)

    def op_bwd(res, cts):
        diff, const = res
        grads = bwd_call(diff + const, tuple(cts))
        return tuple(grads), tuple(jnp.zeros_like(c) for c in const)

    op.defvjp(op_fwd, op_bwd)
    return op(rows + bcs, crows + cbcs)


def _rms(x):
    return x * lax.rsqrt(jnp.mean(x * x, axis=-1, keepdims=True) + NORM_EPS)


def _heads(x, n):
    return [x[:, h * HEAD_DIM:(h + 1) * HEAD_DIM] for h in range(n)]


_NT = (((1,), (1,)), ((), ()))
_TN = (((0,), (0,)), ((), ()))
_TQ = 256


def _attn_probs(q, k):
    s = lax.dot_general(q, k, _NT, preferred_element_type=F32) * (HEAD_DIM ** -0.5)
    p = jnp.exp(s - jnp.max(s, axis=-1, keepdims=True))
    return p / jnp.sum(p, axis=-1, keepdims=True)


def _attn_fwd_call(q, k, v):
    n, t = q.shape[0], k.shape[0]
    tq = _pick(n, (_TQ, 128))

    def body(q_ref, k_ref, v_ref, o_ref):
        p = _attn_probs(q_ref[...].astype(BF16), k_ref[...].astype(BF16))
        o_ref[...] = jnp.dot(p.astype(BF16), v_ref[...].astype(BF16), preferred_element_type=F32)

    return pl.pallas_call(
        body, name="attn_f", grid=(ATTN_HEADS, n // tq),
        in_specs=[pl.BlockSpec((tq, HEAD_DIM), lambda h, i: (i, h)),
                  pl.BlockSpec((t, HEAD_DIM), lambda h, i: (0, h // ATTN_GROUP)),
                  pl.BlockSpec((t, HEAD_DIM), lambda h, i: (0, h // ATTN_GROUP))],
        out_specs=pl.BlockSpec((tq, HEAD_DIM), lambda h, i: (i, h)),
        out_shape=jax.ShapeDtypeStruct(q.shape, F32),
        compiler_params=pltpu.CompilerParams(dimension_semantics=("parallel", "parallel"),
                                             vmem_limit_bytes=VMEM_LIMIT),
    )(q, k, v)


def _attn_bwd_call(q, k, v, do):
    n, t = q.shape[0], k.shape[0]
    tq = _pick(n, (_TQ, 128))

    def body(q_ref, k_ref, v_ref, do_ref, dq_ref, dk_ref, dv_ref):
        @pl.when((pl.program_id(1) == 0) & (pl.program_id(2) == 0))
        def _():
            dk_ref[...] = jnp.zeros_like(dk_ref)
            dv_ref[...] = jnp.zeros_like(dv_ref)

        qb, kb, vb, dob = (r[...].astype(BF16) for r in (q_ref, k_ref, v_ref, do_ref))
        p = _attn_probs(qb, kb)
        dp = lax.dot_general(dob, vb, _NT, preferred_element_type=F32)
        ds = p * (dp - jnp.sum(p * dp, axis=-1, keepdims=True)) * (HEAD_DIM ** -0.5)
        dsb = ds.astype(BF16)
        dq_ref[...] = jnp.dot(dsb, kb, preferred_element_type=F32)
        dk_ref[...] += lax.dot_general(dsb, qb, _TN, preferred_element_type=F32)
        dv_ref[...] += lax.dot_general(p.astype(BF16), dob, _TN, preferred_element_type=F32)

    q_spec = pl.BlockSpec((tq, HEAD_DIM), lambda kh, g, i: (i, kh * ATTN_GROUP + g))
    kv_spec = pl.BlockSpec((t, HEAD_DIM), lambda kh, g, i: (0, kh))
    return pl.pallas_call(
        body, name="attn_b", grid=(ATTN_KV_HEADS, ATTN_GROUP, n // tq),
        in_specs=[q_spec, kv_spec, kv_spec, q_spec],
        out_specs=[q_spec, kv_spec, kv_spec],
        out_shape=(jax.ShapeDtypeStruct(q.shape, F32), jax.ShapeDtypeStruct(k.shape, F32),
                   jax.ShapeDtypeStruct(v.shape, F32)),
        compiler_params=pltpu.CompilerParams(dimension_semantics=("parallel", "arbitrary", "arbitrary"),
                                             vmem_limit_bytes=VMEM_LIMIT),
    )(q, k, v, do)


@jax.custom_vjp
def attention(q, k, v):
    return _attn_fwd_call(q, k, v)


def _attention_fwd(q, k, v):
    return _attn_fwd_call(q, k, v), (q, k, v)


def _attention_bwd(res, do):
    return _attn_bwd_call(*res, do)


attention.defvjp(_attention_fwd, _attention_bwd)


_C = GDN_CHUNK


def _hdot(a, b):
    return jnp.dot(a, b, precision=HIGHEST, preferred_element_type=F32)


def _unit_lower_inverse(low, blockdiag):
    eye = (lax.broadcasted_iota(jnp.int32, (_C, _C), 0) == lax.broadcasted_iota(jnp.int32, (_C, _C), 1)).astype(F32)
    ld = low * blockdiag
    lo = low - ld
    l2 = _hdot(ld, ld)
    l4 = _hdot(l2, l2)
    l8 = _hdot(l4, l4)
    td = _hdot(_hdot(_hdot(eye - ld, eye + l2), eye + l4), eye + l8)
    nn = _hdot(td, lo)
    n2 = _hdot(nn, nn)
    return _hdot(_hdot(eye - nn, eye + n2), td)


def _gdn_chunk(q, k, v, b_b, be_b, e_b, kd_b, m1, dec, gl, s, blockdiag):
    kk = lax.dot_general(k, k, _NT, preferred_element_type=F32)
    t_inv = _unit_lower_inverse(m1 * kk, blockdiag)
    u = _hdot(t_inv, b_b * v)
    w = _hdot(t_inv, be_b * k)
    delta = u - jnp.dot(w, s, preferred_element_type=F32)
    p = dec * lax.dot_general(q, k, _NT, preferred_element_type=F32)
    o = jnp.dot(q * e_b, s, preferred_element_type=F32) + jnp.dot(p, delta, preferred_element_type=F32)
    s_new = gl * s + lax.dot_general(k * kd_b, delta, _TN, preferred_element_type=F32)
    return o, s_new


def _blockdiag_mask():
    r = lax.broadcasted_iota(jnp.int32, (_C, _C), 0) >> 4
    c = lax.broadcasted_iota(jnp.int32, (_C, _C), 1) >> 4
    return (r == c).astype(F32)


def _gdn_specs(nc, rev):
    def ch(c):
        return nc - 1 - c if rev else c

    tok = lambda col0: pl.BlockSpec((_C, HEAD_DIM), lambda h, c: (ch(c), col0 + h))
    per_tok = pl.BlockSpec((None, _C, HEAD_DIM), lambda h, c: (h, ch(c), 0))
    mat = pl.BlockSpec((None, None, _C, _C), lambda h, c: (h, ch(c), 0, 0))
    row = pl.BlockSpec((None, None, 1, HEAD_DIM), lambda h, c: (h, ch(c), 0, 0))
    state = pl.BlockSpec((None, None, HEAD_DIM, HEAD_DIM), lambda h, c: (h, ch(c), 0, 0))
    return tok, per_tok, mat, row, state


def _gdn_fwd_call(name, qkv, b_b, be_b, e_b, kd_b, m1, dec, gl):
    t = qkv.shape[0]
    nc = t // _C
    tok, per_tok, mat, row, state = _gdn_specs(nc, False)

    def body(q_ref, k_ref, v_ref, b_ref, be_ref, e_ref, kd_ref, m1_ref, dec_ref, gl_ref, o_ref, sall_ref, s_ref):
        @pl.when(pl.program_id(1) == 0)
        def _():
            s_ref[...] = jnp.zeros_like(s_ref)

        s = s_ref[...]
        sall_ref[...] = s
        o, s_new = _gdn_chunk(q_ref[...], k_ref[...], v_ref[...], b_ref[...], be_ref[...], e_ref[...], kd_ref[...],
                              m1_ref[...], dec_ref[...], gl_ref[...], s, _blockdiag_mask())
        o_ref[...] = o
        s_ref[...] = s_new

    return pl.pallas_call(
        body, name=name + "_f", grid=(GDN_HEADS, nc),
        in_specs=[tok(0), tok(GDN_HEADS), tok(2 * GDN_HEADS), per_tok, per_tok, per_tok, per_tok, mat, mat, row],
        out_specs=[tok(0), state],
        out_shape=(jax.ShapeDtypeStruct((t, GDN_W), F32),
                   jax.ShapeDtypeStruct((GDN_HEADS, nc, HEAD_DIM, HEAD_DIM), F32)),
        scratch_shapes=[pltpu.VMEM((HEAD_DIM, HEAD_DIM), F32)],
        compiler_params=pltpu.CompilerParams(dimension_semantics=("parallel", "arbitrary"),
                                             vmem_limit_bytes=VMEM_LIMIT),
    )(qkv, qkv, qkv, b_b, be_b, e_b, kd_b, m1, dec, gl)


def _gdn_bwd_call(name, qkv, b_b, be_b, e_b, kd_b, m1, dec, gl, sall, do):
    t = qkv.shape[0]
    nc = t // _C
    tok, per_tok, mat, row, state = _gdn_specs(nc, True)

    def body(q_ref, k_ref, v_ref, b_ref, be_ref, e_ref, kd_ref, m1_ref, dec_ref, gl_ref, sall_ref, do_ref,
             dq_ref, dk_ref, dv_ref, db_ref, dbe_ref, de_ref, dkd_ref, dm1_ref, ddec_ref, dgl_ref, ds_ref):
        @pl.when(pl.program_id(1) == 0)
        def _():
            ds_ref[...] = jnp.zeros_like(ds_ref)

        bd = _blockdiag_mask()
        ins = [r[...] for r in (q_ref, k_ref, v_ref, b_ref, be_ref, e_ref, kd_ref, m1_ref, dec_ref, gl_ref, sall_ref)]
        _, vjp = jax.vjp(lambda *p: _gdn_chunk(*p, bd), *ins)
        grads = vjp((do_ref[...], ds_ref[...]))
        for ref, g in zip((dq_ref, dk_ref, dv_ref, db_ref, dbe_ref, de_ref, dkd_ref, dm1_ref, ddec_ref, dgl_ref),
                          grads[:10]):
            ref[...] = g
        ds_ref[...] = grads[10]

    tok0 = tok(0)
    shp = lambda a: jax.ShapeDtypeStruct(a.shape, F32)
    tok_shape = jax.ShapeDtypeStruct((t, GDN_W), F32)
    return pl.pallas_call(
        body, name=name + "_b", grid=(GDN_HEADS, nc),
        in_specs=[tok(0), tok(GDN_HEADS), tok(2 * GDN_HEADS), per_tok, per_tok, per_tok, per_tok, mat, mat, row,
                  state, tok0],
        out_specs=[tok0, tok0, tok0, per_tok, per_tok, per_tok, per_tok, mat, mat, row],
        out_shape=(tok_shape, tok_shape, tok_shape, shp(b_b), shp(be_b), shp(e_b), shp(kd_b), shp(m1), shp(dec),
                   shp(gl)),
        scratch_shapes=[pltpu.VMEM((HEAD_DIM, HEAD_DIM), F32)],
        compiler_params=pltpu.CompilerParams(dimension_semantics=("parallel", "arbitrary"),
                                             vmem_limit_bytes=VMEM_LIMIT),
    )(qkv, qkv, qkv, b_b, be_b, e_b, kd_b, m1, dec, gl, sall, do)


@functools.partial(jax.custom_vjp, nondiff_argnums=(0,))
def gdn_scan(name, qkv, b_b, be_b, e_b, kd_b, m1, dec, gl):
    return _gdn_fwd_call(name, qkv, b_b, be_b, e_b, kd_b, m1, dec, gl)[0]


def _gdn_scan_fwd(name, *args):
    o, sall = _gdn_fwd_call(name, *args)
    return o, (args, sall)


def _gdn_scan_bwd(name, res, do):
    args, sall = res
    dq, dk, dv, *rest = _gdn_bwd_call(name, *args, sall, do)
    return (jnp.concatenate([dq, dk, dv], axis=1), *rest)


gdn_scan.defvjp(_gdn_scan_fwd, _gdn_scan_bwd)


def _rope_tables(n, cl):
    t = np.arange(n)
    inv_freq = (ROPE_THETA ** (-np.arange(0, HEAD_DIM // 2, 2, dtype=np.float32) / (HEAD_DIM // 2))).astype(np.float32)
    ang_r = (t // GRID_W).astype(np.float32)[:, None] * inv_freq
    ang_c = (t % GRID_W).astype(np.float32)[:, None] * inv_freq
    cos = np.concatenate([np.cos(ang_r), np.cos(ang_r), np.cos(ang_c), np.cos(ang_c)], axis=1)
    sin = np.concatenate([-np.sin(ang_r), np.sin(ang_r), -np.sin(ang_c), np.sin(ang_c)], axis=1)
    cos_all = np.concatenate([np.ones((cl, HEAD_DIM), np.float32), cos], axis=0)
    sin_all = np.concatenate([np.zeros((cl, HEAD_DIM), np.float32), sin], axis=0)
    j = np.arange(HEAD_DIM)
    src = np.where((j % 64) < 32, j + 32, j - 32)
    perm = np.zeros((HEAD_DIM, HEAD_DIM), np.float32)
    perm[src, j] = 1.0
    return (jnp.asarray(cos.astype(np.float32)), jnp.asarray(sin.astype(np.float32)),
            jnp.asarray(cos_all), jnp.asarray(sin_all), jnp.asarray(perm))


def _shift_rows(a, cl):
    z = jnp.zeros((1, a.shape[1]), a.dtype)
    parts = [a[:cl], a[cl:]] if cl else [a]
    prev = jnp.concatenate([jnp.concatenate([z, p[:-1]], axis=0) for p in parts], axis=0)
    nxt = jnp.concatenate([jnp.concatenate([p[1:], z], axis=0) for p in parts], axis=0)
    return prev, nxt


def _seq_flip(a, cl):
    return jnp.concatenate([jnp.flip(a[:cl], axis=0), jnp.flip(a[cl:], axis=0)], axis=0)


def _gdn_factors(log_a, beta):
    t = log_a.shape[0]
    nc = t // _C
    la = log_a.reshape(nc, _C, GDN_HEADS).transpose(2, 0, 1)
    be = beta.reshape(nc, _C, GDN_HEADS).transpose(2, 0, 1)
    gam = jnp.cumsum(la, axis=-1)
    idx = jnp.arange(_C)
    incl = idx[:, None] >= idx[None, :]
    strict = idx[:, None] > idx[None, :]
    dec = jnp.exp(jnp.where(incl, gam[..., :, None] - gam[..., None, :], -jnp.inf))
    m1 = jnp.where(strict, be[..., :, None] * dec, 0.0)
    e = jnp.exp(gam)
    lanes = lambda a: jnp.broadcast_to(a.reshape(GDN_HEADS, t, 1), (GDN_HEADS, t, HEAD_DIM))
    gl = jnp.broadcast_to(jnp.exp(gam[..., -1])[..., None, None], (GDN_HEADS, nc, 1, HEAD_DIM))
    return lanes(be), lanes(be * e), lanes(e), lanes(jnp.exp(gam[..., -1:] - gam)), m1, dec, gl


def local_loss(x, wz, wb, ws, c, ctx, target):
    n, cl = x.shape[0], ctx.shape[0]
    cos_q, sin_q, cos_k, sin_k, perm = _rope_tables(n, cl)

    sc_in = jnp.concatenate([jax.nn.silu(c), jax.nn.silu(ws["c_ctx"])[None, :], jnp.zeros((14, D_MODEL), F32)], axis=0)
    mod = pmm(sc_in, wb["w_mod"], wz["w_mod"], "mm_mod") + ws["b_mod"]
    sh1, sc1, g1, sh2, sc2, g2 = [mod[0:1, i * D_MODEL:(i + 1) * D_MODEL] for i in range(6)]
    csh1, csc1 = mod[1:2, 0:D_MODEL], mod[1:2, D_MODEL:2 * D_MODEL]

    def norm_mod(a, sh, sc):
        return (_rms(a) * (1.0 + sc) + sh,)

    (hx,) = rowop(norm_mod, "normmod_x", (x,), (sh1, sc1))
    (hc,) = rowop(norm_mod, "normmod_c", (ctx,), (csh1, csc1))
    h_all = jnp.concatenate([hc, hx], axis=0)
    p_main = pmm(h_all, wb["w_in_main"], wz["w_in_main"], "mm_in")
    p_small = pmm(h_all, wb["w_in_small"], wz["w_in_small"], "mm_ins")
    ak, av, qkv, aq, z, gate = jnp.split(p_main, [KV_W, 2 * KV_W, SMALL_AT, SMALL_AT + Q_W, SMALL_AT + Q_W + GDN_W],
                                         axis=1)
    db, da = p_small[:, :2 * GDN_HEADS], p_small[:, 2 * GDN_HEADS:4 * GDN_HEADS]

    def qk_prep(nh):
        def fn(a, w, cos, sin, pm):
            outs = []
            for ah in _heads(a, nh):
                y = _rms(ah) * w
                outs.append(y * cos + _hdot(y, pm) * sin)
            return (jnp.concatenate(outs, axis=1),)
        return fn

    (q_x,) = rowop(qk_prep(ATTN_HEADS), "q_prep", (aq[cl:],), (ws["q_norm_w"],), (cos_q, sin_q), (perm,))
    (k_all,) = rowop(qk_prep(ATTN_KV_HEADS), "k_prep", (ak,), (ws["k_norm_w"],), (cos_k, sin_k), (perm,))
    attn_x = attention(q_x, k_all, av)

    qkv_prev, qkv_next = _shift_rows(qkv, cl)
    cw = ws["conv_qkv_w"]

    def gdn_prep(a, ap, an, w0, w1, w2):
        s = jax.nn.silu(ap * w0 + a * w1 + an * w2)
        outs = []
        for i, sh in enumerate(_heads(s, 3 * GDN_HEADS)):
            if i < 2 * GDN_HEADS:
                sh = sh * lax.rsqrt(jnp.sum(sh * sh, axis=-1, keepdims=True) + NORM_EPS)
                if i < GDN_HEADS:
                    sh = sh * (HEAD_DIM ** -0.5)
            outs.append(sh)
        return (jnp.concatenate(outs, axis=1),)

    (qkvn,) = rowop(gdn_prep, "gdn_prep", (qkv, qkv_prev, qkv_next), (cw[0:1], cw[1:2], cw[2:3]), tr=128)
    beta = jax.nn.sigmoid(db).reshape(-1, 2, GDN_HEADS)
    log_a = -jnp.exp(ws["a_log"])[None] * jax.nn.softplus(da.reshape(-1, 2, GDN_HEADS) + ws["dt_bias"][None])
    o_fwd = gdn_scan("gdn_d0", qkvn, *_gdn_factors(log_a[:, 0], beta[:, 0]))
    o_bwd = gdn_scan("gdn_d1", _seq_flip(qkvn, cl), *_gdn_factors(_seq_flip(log_a[:, 1], cl), _seq_flip(beta[:, 1], cl)))
    o_x = o_fwd[cl:] + jnp.flip(o_bwd[cl:], axis=0)

    def gdn_out(o, zz, w):
        outs = [_rms(oh) * w * jax.nn.silu(zh) for oh, zh in zip(_heads(o, GDN_HEADS), _heads(zz, GDN_HEADS))]
        return (jnp.concatenate(outs, axis=1),)

    (gdn_x,) = rowop(gdn_out, "gdn_out", (o_x, z[cl:]), (ws["gdn_norm_w"],))

    pa = pmm(attn_x, wb["w_pa"], wz["w_pa"], "mm_pa")
    pd = pmm(gdn_x, wb["w_pd"], wz["w_pd"], "mm_pd")

    def merge(a, d, g):
        return (jax.nn.sigmoid(g[:, :D_MODEL]) * a + jax.nn.sigmoid(g[:, D_MODEL:]) * d,)

    (y,) = rowop(merge, "merge", (pa, pd, gate[cl:]))
    mo = pmm(y, wb["w_out"], wz["w_out"], "mm_out")

    def res_norm_mod(xx, m, g, sh, sc):
        x1 = xx + g * m
        return x1, _rms(x1) * (1.0 + sc) + sh

    x1, h2 = rowop(res_norm_mod, "res1", (x, mo), (g1, sh2, sc2))
    up = pmm(h2, wb["w_up"], wz["w_up"], "mm_up")
    up_prev, up_next = _shift_rows(up, 0)
    fw = ws["ffn_conv_w"]

    def ffn_act(a, ap, an, w0, w1, w2, b):
        u = ap * w0 + a * w1 + an * w2 + b
        return (jax.nn.silu(u[:, :D_FF]) * u[:, D_FF:],)

    (act,) = rowop(ffn_act, "ffn_act", (up, up_prev, up_next), (fw[0:1], fw[1:2], fw[2:3], ws["ffn_conv_b"]), tr=128)
    dn = pmm(act, wb["w_down"], wz["w_down"], "mm_down")

    def head(xx, m, g, w, tgt):
        yy = _rms(xx + g * m) * w
        err = (yy - tgt) ** 2
        return (jnp.broadcast_to(0.5 * jnp.mean(err, axis=-1, keepdims=True), (xx.shape[0], HEAD_DIM)),)

    (row_loss,) = rowop(head, "head", (x1, dn), (g2, ws["final_norm_w"][None, :]), (target,))
    return jnp.sum(row_loss[:, 0])


_HBM = pl.BlockSpec(memory_space=pltpu.HBM)


def _chip_peers():
    x, y = lax.axis_index("x"), lax.axis_index("y")
    return [(1 - x, y), (x, 1 - y), (1 - x, 1 - y)]


def all_gather_chips(pack):
    def body(p_ref, out_ref, send_sems, recv_sems, local_sem):
        x, y, c = lax.axis_index("x"), lax.axis_index("y"), lax.axis_index("c")
        mine = pltpu.make_async_copy(p_ref, out_ref.at[2 * x + y], local_sem)
        mine.start()
        sends = [pltpu.make_async_remote_copy(src_ref=p_ref, dst_ref=out_ref.at[2 * x + y], send_sem=send_sems.at[k],
                                              recv_sem=recv_sems.at[k], device_id=(px, py, c), device_id_type=MESH)
                 for k, (px, py) in enumerate(_chip_peers())]
        for cp in sends:
            cp.start()
        for k, (px, py) in enumerate(_chip_peers()):
            pltpu.make_async_remote_copy(src_ref=p_ref, dst_ref=out_ref.at[2 * px + py], send_sem=send_sems.at[k],
                                         recv_sem=recv_sems.at[k], device_id=(px, py, c),
                                         device_id_type=MESH).wait_recv()
        for cp in sends:
            cp.wait_send()
        mine.wait()

    return pl.pallas_call(
        body, name="ag_weights", in_specs=[_HBM], out_specs=_HBM,
        out_shape=jax.ShapeDtypeStruct((N_CHIPS,) + pack.shape, pack.dtype),
        scratch_shapes=[pltpu.SemaphoreType.DMA((3,)), pltpu.SemaphoreType.DMA((3,)), pltpu.SemaphoreType.DMA],
    )(pack)


def scatter_to_chips(blocks):
    def body(b_ref, out_ref, send_sems, recv_sems):
        c = lax.axis_index("c")
        sends = [pltpu.make_async_remote_copy(src_ref=b_ref.at[2 * px + py], dst_ref=out_ref.at[k],
                                              send_sem=send_sems.at[k], recv_sem=recv_sems.at[k],
                                              device_id=(px, py, c), device_id_type=MESH)
                 for k, (px, py) in enumerate(_chip_peers())]
        for cp in sends:
            cp.start()
        for cp in sends:
            cp.wait_recv()
        for cp in sends:
            cp.wait_send()

    return pl.pallas_call(
        body, name="rs_grads", in_specs=[_HBM], out_specs=_HBM,
        out_shape=jax.ShapeDtypeStruct((3,) + blocks.shape[1:], blocks.dtype),
        scratch_shapes=[pltpu.SemaphoreType.DMA((3,)), pltpu.SemaphoreType.DMA((3,))],
    )(blocks)


def sibling_swap(a):
    def body(a_ref, out_ref, send_sem, recv_sem):
        x, y, c = lax.axis_index("x"), lax.axis_index("y"), lax.axis_index("c")
        cp = pltpu.make_async_remote_copy(src_ref=a_ref, dst_ref=out_ref, send_sem=send_sem, recv_sem=recv_sem,
                                          device_id=(x, y, 1 - c), device_id_type=MESH)
        cp.start()
        cp.wait()

    return pl.pallas_call(
        body, name="sib_swap", in_specs=[_HBM], out_specs=_HBM,
        out_shape=jax.ShapeDtypeStruct(a.shape, a.dtype),
        scratch_shapes=[pltpu.SemaphoreType.DMA, pltpu.SemaphoreType.DMA],
    )(a)


def all_reduce_small(v):
    def body(v_ref, tot_ref, gath_ref, send_sems, recv_sems):
        x, y, c = lax.axis_index("x"), lax.axis_index("y"), lax.axis_index("c")
        me = 4 * x + 2 * y + c
        gath_ref[me] = v_ref[...]

        def peer(k):
            m = k + 1
            return (x ^ (m >> 2 & 1), y ^ (m >> 1 & 1), c ^ (m & 1))

        sends = [pltpu.make_async_remote_copy(src_ref=v_ref, dst_ref=gath_ref.at[me], send_sem=send_sems.at[k],
                                              recv_sem=recv_sems.at[k], device_id=peer(k), device_id_type=MESH)
                 for k in range(N_DEV - 1)]
        for cp in sends:
            cp.start()
        for k in range(N_DEV - 1):
            px, py, pc = peer(k)
            pltpu.make_async_remote_copy(src_ref=v_ref, dst_ref=gath_ref.at[4 * px + 2 * py + pc],
                                         send_sem=send_sems.at[k], recv_sem=recv_sems.at[k], device_id=peer(k),
                                         device_id_type=MESH).wait_recv()
        for cp in sends:
            cp.wait_send()
        acc = gath_ref[0]
        for d in range(1, N_DEV):
            acc = acc + gath_ref[d]
        tot_ref[...] = acc

    vm = pl.BlockSpec(memory_space=pltpu.VMEM)
    return pl.pallas_call(
        body, name="ar_small", in_specs=[vm], out_specs=[vm, vm],
        out_shape=(jax.ShapeDtypeStruct(v.shape, v.dtype), jax.ShapeDtypeStruct((N_DEV,) + v.shape, v.dtype)),
        scratch_shapes=[pltpu.SemaphoreType.DMA((N_DEV - 1,)), pltpu.SemaphoreType.DMA((N_DEV - 1,))],
    )(v)[0]


def _elementwise(fn, name, ins, n_out, out_dtype=F32):
    r, cdim = ins[0].shape
    tr = _pick(r, tuple(p for p in (488, 256, 128, 104, 64, 32, 16, 8) if p * cdim * 4 <= 2 * 1024 * 1024))
    spec = pl.BlockSpec((tr, cdim), lambda i: (i, 0))

    def body(*refs):
        res = fn(*[ref[...] for ref in refs[:len(ins)]])
        for o_ref, v in zip(refs[len(ins):], res):
            o_ref[...] = v

    return pl.pallas_call(
        body, name=name, grid=(r // tr,), in_specs=[spec] * len(ins), out_specs=[spec] * n_out,
        out_shape=tuple(jax.ShapeDtypeStruct((r, cdim), out_dtype) for _ in range(n_out)),
        compiler_params=pltpu.CompilerParams(dimension_semantics=("parallel",), vmem_limit_bytes=VMEM_LIMIT),
    )(*ins)


def _adamw(w, ga, gb, m, v, name):
    shape = w.shape
    to2 = lambda a: a.reshape(-1, shape[-1])

    def fn(w_, ga_, gb_, m_, v_):
        g = ga_ + gb_
        m_new = ADAM_B1 * m_ + (1.0 - ADAM_B1) * g
        v_new = ADAM_B2 * v_ + (1.0 - ADAM_B2) * (g * g)
        m_hat = m_new / (1.0 - ADAM_B1 ** ADAM_STEP)
        v_hat = v_new / (1.0 - ADAM_B2 ** ADAM_STEP)
        delta = -ADAM_LR * (m_hat / (jnp.sqrt(v_hat) + ADAM_EPS) + ADAM_WD * w_)
        return g, delta, m_new, v_new

    outs = _elementwise(fn, name, [to2(a) for a in (w, ga, gb, m, v)], 4)
    return tuple(o.reshape(shape) for o in outs)


_BIG = ("w_mod", "w_in", "w_pa", "w_pd", "w_out", "w_up", "w_down")
_COL_SHARDED = ("w_mod", "w_in", "w_up")
_FULL_SHAPE = {"w_mod": (D_MODEL, MOD_W), "w_in": (D_MODEL, IN_COLS), "w_pa": (Q_W, D_MODEL), "w_pd": (GDN_W, D_MODEL),
               "w_out": (D_MODEL, D_MODEL), "w_up": (D_MODEL, 2 * D_FF), "w_down": (D_FF, D_MODEL)}


def _shard_shape(name):
    r, cdim = _FULL_SHAPE[name]
    return (r, cdim // N_CHIPS) if name in _COL_SHARDED else (r // N_CHIPS, cdim)


_BIG_ELEMS = sum(int(np.prod(_shard_shape(nm))) for nm in _BIG)
_BIG_ROWS = _BIG_ELEMS // D_MODEL
_CONV_ELEMS = 2 * (3 * CONV_W // N_CHIPS + 3 * 2 * D_FF // N_CHIPS)
_CONV_ROWS = 24


def _blocks_of_full(name, full):
    r, cdim = _FULL_SHAPE[name]
    if name in _COL_SHARDED:
        return full.reshape(r, N_CHIPS, cdim // N_CHIPS).transpose(1, 0, 2).reshape(N_CHIPS, -1)
    return full.reshape(N_CHIPS, -1)


def _full_of_blocks(name, blocks):
    r, cdim = _FULL_SHAPE[name]
    if name in _COL_SHARDED:
        return blocks.reshape(N_CHIPS, r, cdim // N_CHIPS).transpose(1, 0, 2).reshape(r, cdim)
    return blocks.reshape(r, cdim)


def _split_big(flat):
    out, off = {}, 0
    for nm in _BIG:
        size = int(np.prod(_shard_shape(nm)))
        out[nm] = flat[..., off:off + size]
        off += size
    return out


def _w_in_regroup(w_in_full):
    main = jnp.concatenate([w_in_full[:, :SMALL_AT], w_in_full[:, SMALL_AT + 4 * GDN_HEADS:]], axis=1)
    small = jnp.pad(w_in_full[:, SMALL_AT:SMALL_AT + 4 * GDN_HEADS], ((0, 0), (0, HEAD_DIM - 4 * GDN_HEADS)))
    return main, small


def _w_in_ungroup(main, small):
    return jnp.concatenate([main[:, :SMALL_AT], small[:, :4 * GDN_HEADS], main[:, SMALL_AT:]], axis=1)


_SMALL = ("c_ctx", "b_mod", "q_norm_w", "k_norm_w", "conv_qkv_w", "a_log", "dt_bias", "gdn_norm_w", "ffn_conv_w",
          "ffn_conv_b", "final_norm_w")


def _pack_small(tree, rows):
    flat = jnp.concatenate([tree[nm].reshape(-1) for nm in _SMALL])
    return jnp.pad(flat, (0, rows * 128 - flat.shape[0])).reshape(rows, 128)


def _unpack_small(packed, like):
    flat, out, off = packed.reshape(-1), {}, 0
    for nm in _SMALL:
        size = int(np.prod(like[nm].shape))
        out[nm] = flat[off:off + size].reshape(like[nm].shape)
        off += size
    return out


def kernel(x, c, ctx, c_ctx, w_mod, b_mod, w_in, q_norm_w, k_norm_w, conv_qkv_w, a_log, dt_bias, gdn_norm_w, w_pa, w_pd, w_out, w_up, ffn_conv_w, ffn_conv_b, w_down, final_norm_w, loss_target, m_c_ctx, m_w_mod, m_b_mod, m_w_in, m_q_norm_w, m_k_norm_w, m_conv_qkv_w, m_a_log, m_dt_bias, m_gdn_norm_w, m_w_pa, m_w_pd, m_w_out, m_w_up, m_ffn_conv_w, m_ffn_conv_b, m_w_down, m_final_norm_w, v_c_ctx, v_w_mod, v_b_mod, v_w_in, v_q_norm_w, v_k_norm_w, v_conv_qkv_w, v_a_log, v_dt_bias, v_gdn_norm_w, v_w_pa, v_w_pd, v_w_out, v_w_up, v_ffn_conv_w, v_ffn_conv_b, v_w_down, v_final_norm_w):
    names = ("c_ctx", "w_mod", "b_mod", "w_in", "q_norm_w", "k_norm_w", "conv_qkv_w", "a_log", "dt_bias", "gdn_norm_w",
             "w_pa", "w_pd", "w_out", "w_up", "ffn_conv_w", "ffn_conv_b", "w_down", "final_norm_w")
    w_sh = dict(c_ctx=c_ctx, w_mod=w_mod, b_mod=b_mod, w_in=w_in, q_norm_w=q_norm_w, k_norm_w=k_norm_w,
                conv_qkv_w=conv_qkv_w, a_log=a_log, dt_bias=dt_bias, gdn_norm_w=gdn_norm_w, w_pa=w_pa, w_pd=w_pd,
                w_out=w_out, w_up=w_up, ffn_conv_w=ffn_conv_w, ffn_conv_b=ffn_conv_b, w_down=w_down,
                final_norm_w=final_norm_w)
    m_sh = dict(c_ctx=m_c_ctx, w_mod=m_w_mod, b_mod=m_b_mod, w_in=m_w_in, q_norm_w=m_q_norm_w, k_norm_w=m_k_norm_w,
                conv_qkv_w=m_conv_qkv_w, a_log=m_a_log, dt_bias=m_dt_bias, gdn_norm_w=m_gdn_norm_w, w_pa=m_w_pa,
                w_pd=m_w_pd, w_out=m_w_out, w_up=m_w_up, ffn_conv_w=m_ffn_conv_w, ffn_conv_b=m_ffn_conv_b,
                w_down=m_w_down, final_norm_w=m_final_norm_w)
    v_sh = dict(c_ctx=v_c_ctx, w_mod=v_w_mod, b_mod=v_b_mod, w_in=v_w_in, q_norm_w=v_q_norm_w, k_norm_w=v_k_norm_w,
                conv_qkv_w=v_conv_qkv_w, a_log=v_a_log, dt_bias=v_dt_bias, gdn_norm_w=v_gdn_norm_w, w_pa=v_w_pa,
                w_pd=v_w_pd, w_out=v_w_out, w_up=v_w_up, ffn_conv_w=v_ffn_conv_w, ffn_conv_b=v_ffn_conv_b,
                w_down=v_w_down, final_norm_w=v_final_norm_w)
    chip = 2 * lax.axis_index("x") + lax.axis_index("y")

    conv_bits = jnp.concatenate([lax.bitcast_convert_type(w_sh[nm][0], BF16).reshape(-1)
                                 for nm in ("conv_qkv_w", "ffn_conv_w")])
    pack = jnp.concatenate([w_sh[nm][0].astype(BF16).reshape(-1) for nm in _BIG]
                           + [conv_bits, jnp.zeros((_CONV_ROWS * D_MODEL - _CONV_ELEMS,), BF16)])
    gathered = all_gather_chips(pack.reshape(_BIG_ROWS + _CONV_ROWS, D_MODEL)).reshape(N_CHIPS, -1)
    big_blocks = _split_big(gathered[:, :_BIG_ELEMS])
    wb = {nm: _full_of_blocks(nm, big_blocks[nm]) for nm in _BIG}
    wb["w_in_main"], wb["w_in_small"] = _w_in_regroup(wb.pop("w_in"))
    conv_all = gathered[:, _BIG_ELEMS:_BIG_ELEMS + _CONV_ELEMS]
    n_cq = 2 * 3 * CONV_W // N_CHIPS
    unbits = lambda a, w: lax.bitcast_convert_type(a.reshape(N_CHIPS, 3, w // N_CHIPS, 2), F32).transpose(1, 0, 2).reshape(3, w)
    ws = dict(c_ctx=c_ctx, b_mod=b_mod, q_norm_w=q_norm_w, k_norm_w=k_norm_w, a_log=a_log[0], dt_bias=dt_bias[0],
              gdn_norm_w=gdn_norm_w, ffn_conv_b=ffn_conv_b, final_norm_w=final_norm_w,
              conv_qkv_w=unbits(conv_all[:, :n_cq], CONV_W), ffn_conv_w=unbits(conv_all[:, n_cq:], 2 * D_FF))
    wz = {nm: jnp.zeros(a.shape, F32) for nm, a in wb.items()}

    loss_local, (gx, gz, gs) = jax.value_and_grad(local_loss, argnums=(0, 1, 3))(
        x[0], wz, wb, ws, c, ctx[0], loss_target[0])
    loss = lax.psum(loss_local, ("x", "y", "c"))

    gz["w_in"] = _w_in_ungroup(gz.pop("w_in_main"), gz.pop("w_in_small"))
    g_blocks = jnp.concatenate([_blocks_of_full(nm, gz[nm]) for nm in _BIG], axis=1)
    g_own = lax.dynamic_index_in_dim(g_blocks, chip, axis=0, keepdims=False).reshape(_BIG_ROWS, D_MODEL)
    got = scatter_to_chips(g_blocks.astype(BF16).reshape(N_CHIPS, _BIG_ROWS, D_MODEL))
    (partial,) = _elementwise(lambda a, r0, r1, r2: (a + r0.astype(F32) + r1.astype(F32) + r2.astype(F32),),
                              "rs_sum", [g_own, got[0], got[1], got[2]], 1)
    other = sibling_swap(partial)
    part_a, part_b = _split_big(partial.reshape(-1)), _split_big(other.reshape(-1))

    gs["a_log"], gs["dt_bias"] = gs["a_log"][None], gs["dt_bias"][None]
    like = {nm: gs[nm] for nm in _SMALL}
    small_rows = -(-sum(int(np.prod(like[nm].shape)) for nm in _SMALL) // 1024) * 8
    g_small = _unpack_small(all_reduce_small(_pack_small(gs, small_rows)), like)
    for nm, width in (("conv_qkv_w", CONV_W), ("ffn_conv_w", 2 * D_FF)):
        g_small[nm] = lax.dynamic_slice_in_dim(g_small[nm], chip * (width // N_CHIPS), width // N_CHIPS, axis=1)[None]

    grads, deltas, new_m, new_v = {}, {}, {}, {}
    for nm in _BIG:
        shp = _shard_shape(nm)
        grads[nm], deltas[nm], new_m[nm], new_v[nm] = (
            o[None] for o in _adamw(w_sh[nm][0], part_a[nm].reshape(shp), part_b[nm].reshape(shp), m_sh[nm][0],
                                    v_sh[nm][0], "adamw_" + nm))
    shard_like = {nm: w_sh[nm] for nm in _SMALL}
    rows_l = -(-sum(int(np.prod(shard_like[nm].shape)) for nm in _SMALL) // 1024) * 8
    g_l = _pack_small({nm: g_small[nm].reshape(w_sh[nm].shape) for nm in _SMALL}, rows_l)
    outs = _adamw(_pack_small(w_sh, rows_l), g_l, jnp.zeros_like(g_l), _pack_small(m_sh, rows_l),
                  _pack_small(v_sh, rows_l), "adamw_small")
    for tree, packed in zip((grads, deltas, new_m, new_v), outs):
        tree.update(_unpack_small(packed, shard_like))

    return (loss, gx[None], *[grads[nm] for nm in names], *[deltas[nm] for nm in names],
            *[new_m[nm] for nm in names], *[new_v[nm] for nm in names])
```

```python
import functools
import math

import jax
import jax.numpy as jnp
import numpy as np
from jax import lax
from jax.experimental import pallas as pl
from jax.experimental.pallas import tpu as pltpu

F32 = jnp.float32
BF16 = jnp.bfloat16
HIGHEST = lax.Precision.HIGHEST
MESH = pl.DeviceIdType.MESH

D_MODEL = 1024
GRID_W = 64
ATTN_HEADS = 8
ATTN_KV_HEADS = 2
ATTN_GROUP = ATTN_HEADS // ATTN_KV_HEADS
HEAD_DIM = 128
ROPE_THETA = 10000.0
GDN_HEADS = 8
GDN_CHUNK = 64
D_FF = 2816
NORM_EPS = 1e-6
KV_W = ATTN_KV_HEADS * HEAD_DIM
Q_W = ATTN_HEADS * HEAD_DIM
GDN_W = GDN_HEADS * HEAD_DIM
CONV_W = 3 * GDN_W
MOD_W = 6 * D_MODEL
IN_COLS = 2 * KV_W + CONV_W + 4 * GDN_HEADS + Q_W + GDN_W + 2 * D_MODEL
IN_MAIN = IN_COLS - 4 * GDN_HEADS
SMALL_AT = 2 * KV_W + CONV_W
N_CHIPS = 4
N_DEV = 8

ADAM_LR = 0.001
ADAM_B1 = 0.9
ADAM_B2 = 0.999
ADAM_EPS = 1e-08
ADAM_WD = 0.01
ADAM_STEP = 10

VMEM_LIMIT = 48 * 1024 * 1024


def _pick(dim, prefs):
    for p in prefs:
        if p <= dim and dim % p == 0:
            return p
    return dim


_DIMS = {
    "nn": (((1,), (0,)), ((), ())),
    "nt": (((1,), (1,)), ((), ())),
    "tn": (((0,), (0,)), ((), ())),
}


def _matmul(a, b, mode, name):
    if mode == "nn":
        (m, k), (_, n) = a.shape, b.shape
    elif mode == "nt":
        (m, k), (n, _) = a.shape, b.shape
    else:
        (k, m), (_, n) = a.shape, b.shape
    tm = _pick(m, (512, 384, 256, 128))
    tn = _pick(n, (1536, 1408, 1024, 768, 512, 256, 128))
    tk = _pick(k, (1024, 1408, 768, 512, 256, 128))
    nk = k // tk
    if mode == "tn":
        a_spec = pl.BlockSpec((tk, tm), lambda i, j, l: (l, i))
    else:
        a_spec = pl.BlockSpec((tm, tk), lambda i, j, l: (i, l))
    if mode == "nt":
        b_spec = pl.BlockSpec((tn, tk), lambda i, j, l: (j, l))
    else:
        b_spec = pl.BlockSpec((tk, tn), lambda i, j, l: (l, j))
    dims = _DIMS[mode]

    def body(a_ref, b_ref, o_ref, acc_ref):
        l = pl.program_id(2)

        @pl.when(l == 0)
        def _():
            acc_ref[...] = jnp.zeros_like(acc_ref)

        acc_ref[...] += lax.dot_general(a_ref[...].astype(BF16), b_ref[...].astype(BF16), dims,
                                        preferred_element_type=F32)

        @pl.when(l == nk - 1)
        def _():
            o_ref[...] = acc_ref[...]

    return pl.pallas_call(
        body,
        name=name,
        grid=(m // tm, n // tn, nk),
        in_specs=[a_spec, b_spec],
        out_specs=pl.BlockSpec((tm, tn), lambda i, j, l: (i, j)),
        out_shape=jax.ShapeDtypeStruct((m, n), F32),
        scratch_shapes=[pltpu.VMEM((tm, tn), F32)],
        compiler_params=pltpu.CompilerParams(dimension_semantics=("parallel", "parallel", "arbitrary"),
                                             vmem_limit_bytes=VMEM_LIMIT),
    )(a, b)


@functools.partial(jax.custom_vjp, nondiff_argnums=(3,))
def pmm(a, w, wz, name):
    del wz
    return _matmul(a, w, "nn", name + "_f")


def _pmm_fwd(a, w, wz, name):
    del wz
    return _matmul(a, w, "nn", name + "_f"), (a, w)


def _pmm_bwd(name, res, g):
    a, w = res
    da = _matmul(g, w, "nt", name + "_da")
    if a.shape[0] < 128:
        pad = 128 - a.shape[0]
        at = jnp.pad(a.T, ((0, 0), (0, pad)))
        gp = jnp.pad(g, ((0, pad), (0, 0)))
        dw = _matmul(at, gp, "nn", name + "_dw")
    else:
        dw = _matmul(a, g, "tn", name + "_dw")
    return da, jnp.zeros_like(w), dw


pmm.defvjp(_pmm_fwd, _pmm_bwd)


@functools.partial(jax.custom_vjp, nondiff_argnums=(3,))
def pmm_t(a, wt, wtz, name):
    del wtz
    return _matmul(a, wt, "nt", name + "_f")


def _pmm_t_fwd(a, wt, wtz, name):
    del wtz
    return _matmul(a, wt, "nt", name + "_f"), (a, wt)


def _pmm_t_bwd(name, res, g):
    a, wt = res
    return _matmul(g, wt, "nn", name + "_da"), jnp.zeros_like(wt), _matmul(g, a, "tn", name + "_dw")


pmm_t.defvjp(_pmm_t_fwd, _pmm_t_bwd)


def rowop(fn, name, rows, bcs=(), crows=(), cbcs=(), tr=256):
    rows, bcs, crows, cbcs = tuple(rows), tuple(bcs), tuple(crows), tuple(cbcs)
    n_rows = rows[0].shape[0]
    tr = _pick(n_rows, (tr, 128, 64, 32, 16, 8))
    nr, nb, ncr, ncb = len(rows), len(bcs), len(crows), len(cbcs)
    n_in = nr + nb + ncr + ncb
    grid = (n_rows // tr,)

    def blk(arr):
        return jax.ShapeDtypeStruct((tr, arr.shape[1]), arr.dtype)

    def row_spec(arr):
        return pl.BlockSpec((tr, arr.shape[1]), lambda i: (i, 0))

    def bc_spec(arr):
        return pl.BlockSpec(arr.shape, lambda i: (0, 0))

    out_blk = jax.eval_shape(fn, *[blk(r) for r in rows], *bcs, *[blk(r) for r in crows], *cbcs)
    n_out = len(out_blk)
    out_shape = tuple(jax.ShapeDtypeStruct((n_rows, o.shape[1]), o.dtype) for o in out_blk)
    in_specs = ([row_spec(r) for r in rows] + [bc_spec(b) for b in bcs]
                + [row_spec(r) for r in crows] + [bc_spec(b) for b in cbcs])

    def order(vals):
        return vals

    def fwd_call(args):
        def body(*refs):
            vals = [r[...] for r in refs[:n_in]]
            res = fn(*order(vals))
            for o_ref, r in zip(refs[n_in:], res):
                o_ref[...] = r

        return pl.pallas_call(
            body, name=name + "_f", grid=grid, in_specs=in_specs,
            out_specs=[row_spec(o) for o in out_shape], out_shape=out_shape,
            compiler_params=pltpu.CompilerParams(dimension_semantics=("parallel",), vmem_limit_bytes=VMEM_LIMIT),
        )(*args)

    def bwd_call(args, cts):
        def body(*refs):
            vals = [r[...] for r in refs[:n_in]]
            ct_refs = refs[n_in:n_in + n_out]
            d_rows = refs[n_in + n_out:n_in + n_out + nr]
            d_bcs = refs[n_in + n_out + nr:]
            consts = vals[nr + nb:]
            _, vjp = jax.vjp(lambda *p: fn(*p, *consts), *vals[:nr + nb])
            grads = vjp(tuple(c[...] for c in ct_refs))
            for ref, g in zip(d_rows, grads[:nr]):
                ref[...] = g

            @pl.when(pl.program_id(0) == 0)
            def _():
                for ref in d_bcs:
                    ref[...] = jnp.zeros_like(ref)

            for ref, g in zip(d_bcs, grads[nr:]):
                ref[...] += g

        d_shape = tuple(jax.ShapeDtypeStruct(r.shape, r.dtype) for r in rows + bcs)
        return pl.pallas_call(
            body, name=name + "_b", grid=grid,
            in_specs=in_specs + [row_spec(o) for o in out_shape],
            out_specs=[row_spec(r) for r in rows] + [bc_spec(b) for b in bcs], out_shape=d_shape,
            compiler_params=pltpu.CompilerParams(dimension_semantics=("arbitrary",), vmem_limit_bytes=VMEM_LIMIT),
        )(*args, *cts)

    @jax.custom_vjp
    def op(diff, const):
        return fwd_call(diff + const)

    def op_fwd(diff, const):
        return fwd_call(diff + const), (diff, const)

    def op_bwd(res, cts):
        diff, const = res
        grads = bwd_call(diff + const, tuple(cts))
        return tuple(grads), tuple(jnp.zeros_like(c) for c in const)

    op.defvjp(op_fwd, op_bwd)
    return op(rows + bcs, crows + cbcs)


def _rms(x):
    return x * lax.rsqrt(jnp.mean(x * x, axis=-1, keepdims=True) + NORM_EPS)


def _heads(x, n):
    return [x[:, h * HEAD_DIM:(h + 1) * HEAD_DIM] for h in range(n)]


_NT = (((1,), (1,)), ((), ()))
_TN = (((0,), (0,)), ((), ()))
_TQ = 256


def _attn_probs(q, k):
    s = lax.dot_general(q, k, _NT, preferred_element_type=F32) * (HEAD_DIM ** -0.5)
    p = jnp.exp(s - jnp.max(s, axis=-1, keepdims=True))
    return p / jnp.sum(p, axis=-1, keepdims=True)


def _attn_fwd_call(q, k, v):
    n, t = q.shape[0], k.shape[0]
    tq = _pick(n, (_TQ, 128))

    def body(q_ref, k_ref, v_ref, o_ref):
        p = _attn_probs(q_ref[...].astype(BF16), k_ref[...].astype(BF16))
        o_ref[...] = jnp.dot(p.astype(BF16), v_ref[...].astype(BF16), preferred_element_type=F32)

    return pl.pallas_call(
        body, name="attn_f", grid=(ATTN_HEADS, n // tq),
        in_specs=[pl.BlockSpec((tq, HEAD_DIM), lambda h, i: (i, h)),
                  pl.BlockSpec((t, HEAD_DIM), lambda h, i: (0, h // ATTN_GROUP)),
                  pl.BlockSpec((t, HEAD_DIM), lambda h, i: (0, h // ATTN_GROUP))],
        out_specs=pl.BlockSpec((tq, HEAD_DIM), lambda h, i: (i, h)),
        out_shape=jax.ShapeDtypeStruct(q.shape, F32),
        compiler_params=pltpu.CompilerParams(dimension_semantics=("parallel", "parallel"),
                                             vmem_limit_bytes=VMEM_LIMIT),
    )(q, k, v)


def _attn_bwd_call(q, k, v, do):
    n, t = q.shape[0], k.shape[0]
    tq = _pick(n, (_TQ, 128))

    def body(q_ref, k_ref, v_ref, do_ref, dq_ref, dk_ref, dv_ref):
        @pl.when((pl.program_id(1) == 0) & (pl.program_id(2) == 0))
        def _():
            dk_ref[...] = jnp.zeros_like(dk_ref)
            dv_ref[...] = jnp.zeros_like(dv_ref)

        qb, kb, vb, dob = (r[...].astype(BF16) for r in (q_ref, k_ref, v_ref, do_ref))
        p = _attn_probs(qb, kb)
        dp = lax.dot_general(dob, vb, _NT, preferred_element_type=F32)
        ds = p * (dp - jnp.sum(p * dp, axis=-1, keepdims=True)) * (HEAD_DIM ** -0.5)
        dsb = ds.astype(BF16)
        dq_ref[...] = jnp.dot(dsb, kb, preferred_element_type=F32)
        dk_ref[...] += lax.dot_general(dsb, qb, _TN, preferred_element_type=F32)
        dv_ref[...] += lax.dot_general(p.astype(BF16), dob, _TN, preferred_element_type=F32)

    q_spec = pl.BlockSpec((tq, HEAD_DIM), lambda kh, g, i: (i, kh * ATTN_GROUP + g))
    kv_spec = pl.BlockSpec((t, HEAD_DIM), lambda kh, g, i: (0, kh))
    return pl.pallas_call(
        body, name="attn_b", grid=(ATTN_KV_HEADS, ATTN_GROUP, n // tq),
        in_specs=[q_spec, kv_spec, kv_spec, q_spec],
        out_specs=[q_spec, kv_spec, kv_spec],
        out_shape=(jax.ShapeDtypeStruct(q.shape, F32), jax.ShapeDtypeStruct(k.shape, F32),
                   jax.ShapeDtypeStruct(v.shape, F32)),
        compiler_params=pltpu.CompilerParams(dimension_semantics=("parallel", "arbitrary", "arbitrary"),
                                             vmem_limit_bytes=VMEM_LIMIT),
    )(q, k, v, do)


@jax.custom_vjp
def attention(q, k, v):
    return _attn_fwd_call(q, k, v)


def _attention_fwd(q, k, v):
    return _attn_fwd_call(q, k, v), (q, k, v)


def _attention_bwd(res, do):
    return _attn_bwd_call(*res, do)


attention.defvjp(_attention_fwd, _attention_bwd)


_C = GDN_CHUNK


def _hdot(a, b):
    return jnp.dot(a, b, precision=HIGHEST, preferred_element_type=F32)


def _unit_lower_inverse(low, blockdiag):
    eye = (lax.broadcasted_iota(jnp.int32, (_C, _C), 0) == lax.broadcasted_iota(jnp.int32, (_C, _C), 1)).astype(F32)
    ld = low * blockdiag
    lo = low - ld
    l2 = _hdot(ld, ld)
    l4 = _hdot(l2, l2)
    l8 = _hdot(l4, l4)
    td = _hdot(_hdot(_hdot(eye - ld, eye + l2), eye + l4), eye + l8)
    nn = _hdot(td, lo)
    n2 = _hdot(nn, nn)
    return _hdot(_hdot(eye - nn, eye + n2), td)


def _gdn_chunk(q, k, v, b_b, be_b, e_b, kd_b, m1, dec, gl, s, blockdiag):
    kk = lax.dot_general(k, k, _NT, preferred_element_type=F32)
    t_inv = _unit_lower_inverse(m1 * kk, blockdiag)
    u = _hdot(t_inv, b_b * v)
    w = _hdot(t_inv, be_b * k)
    delta = u - jnp.dot(w, s, preferred_element_type=F32)
    p = dec * lax.dot_general(q, k, _NT, preferred_element_type=F32)
    o = jnp.dot(q * e_b, s, preferred_element_type=F32) + jnp.dot(p, delta, preferred_element_type=F32)
    s_new = gl * s + lax.dot_general(k * kd_b, delta, _TN, preferred_element_type=F32)
    return o, s_new


def _blockdiag_mask():
    r = lax.broadcasted_iota(jnp.int32, (_C, _C), 0) >> 4
    c = lax.broadcasted_iota(jnp.int32, (_C, _C), 1) >> 4
    return (r == c).astype(F32)


def _gdn_specs(nc, ncc, reverse, backward):
    def ch(s):
        s = nc - 1 - s if backward else s
        return jnp.where(s < ncc, ncc - 1 - s, nc + ncc - 1 - s) if reverse else s

    tok = pl.BlockSpec((_C, 3 * GDN_W), lambda s: (ch(s), 0))
    out = pl.BlockSpec((_C, GDN_W), lambda s: (ch(s), 0))
    per_tok = pl.BlockSpec((GDN_HEADS, _C, HEAD_DIM), lambda s: (0, ch(s), 0))
    mat = pl.BlockSpec((GDN_HEADS, None, _C, _C), lambda s: (0, ch(s), 0, 0))
    row = pl.BlockSpec((GDN_HEADS, None, 1, HEAD_DIM), lambda s: (0, ch(s), 0, 0))
    state = pl.BlockSpec((GDN_HEADS, None, HEAD_DIM, HEAD_DIM), lambda s: (0, ch(s), 0, 0))
    return tok, out, per_tok, mat, row, state


def _head_cols(h, part):
    return slice((part * GDN_HEADS + h) * HEAD_DIM, (part * GDN_HEADS + h + 1) * HEAD_DIM)


def _gdn_fwd_call(name, reverse, ncc, qkv, b_b, be_b, e_b, kd_b, m1, dec, gl):
    t = qkv.shape[0]
    nc = t // _C
    tok, out, per_tok, mat, row, state = _gdn_specs(nc, ncc, reverse, False)

    def body(qkv_ref, b_ref, be_ref, e_ref, kd_ref, m1_ref, dec_ref, gl_ref, o_ref, sall_ref, s_ref):
        @pl.when(pl.program_id(0) == 0)
        def _():
            s_ref[...] = jnp.zeros_like(s_ref)

        bd = _blockdiag_mask()
        for h in range(GDN_HEADS):
            s = s_ref[h]
            sall_ref[h] = s
            o, s_new = _gdn_chunk(qkv_ref[:, _head_cols(h, 0)], qkv_ref[:, _head_cols(h, 1)],
                                  qkv_ref[:, _head_cols(h, 2)], b_ref[h], be_ref[h], e_ref[h], kd_ref[h],
                                  m1_ref[h], dec_ref[h], gl_ref[h], s, bd)
            o_ref[:, _head_cols(h, 0)] = o
            s_ref[h] = s_new

    return pl.pallas_call(
        body, name=name + "_f", grid=(nc,),
        in_specs=[tok, per_tok, per_tok, per_tok, per_tok, mat, mat, row],
        out_specs=[out, state],
        out_shape=(jax.ShapeDtypeStruct((t, GDN_W), F32),
                   jax.ShapeDtypeStruct((GDN_HEADS, nc, HEAD_DIM, HEAD_DIM), F32)),
        scratch_shapes=[pltpu.VMEM((GDN_HEADS, HEAD_DIM, HEAD_DIM), F32)],
        compiler_params=pltpu.CompilerParams(dimension_semantics=("arbitrary",), vmem_limit_bytes=VMEM_LIMIT),
    )(qkv, b_b, be_b, e_b, kd_b, m1, dec, gl)


def _gdn_bwd_call(name, reverse, ncc, qkv, b_b, be_b, e_b, kd_b, m1, dec, gl, sall, do):
    t = qkv.shape[0]
    nc = t // _C
    tok, out, per_tok, mat, row, state = _gdn_specs(nc, ncc, reverse, True)

    def body(qkv_ref, b_ref, be_ref, e_ref, kd_ref, m1_ref, dec_ref, gl_ref, sall_ref, do_ref,
             dqkv_ref, db_ref, dbe_ref, de_ref, dkd_ref, dm1_ref, ddec_ref, dgl_ref, ds_ref):
        @pl.when(pl.program_id(0) == 0)
        def _():
            ds_ref[...] = jnp.zeros_like(ds_ref)

        bd = _blockdiag_mask()
        for h in range(GDN_HEADS):
            ins = [qkv_ref[:, _head_cols(h, 0)], qkv_ref[:, _head_cols(h, 1)], qkv_ref[:, _head_cols(h, 2)],
                   b_ref[h], be_ref[h], e_ref[h], kd_ref[h], m1_ref[h], dec_ref[h], gl_ref[h], sall_ref[h]]
            _, vjp = jax.vjp(lambda *p: _gdn_chunk(*p, bd), *ins)
            grads = vjp((do_ref[:, _head_cols(h, 0)], ds_ref[h]))
            for part in range(3):
                dqkv_ref[:, _head_cols(h, part)] = grads[part]
            for ref, g in zip((db_ref, dbe_ref, de_ref, dkd_ref, dm1_ref, ddec_ref, dgl_ref), grads[3:10]):
                ref[h] = g
            ds_ref[h] = grads[10]

    shp = lambda a: jax.ShapeDtypeStruct(a.shape, F32)
    return pl.pallas_call(
        body, name=name + "_b", grid=(nc,),
        in_specs=[tok, per_tok, per_tok, per_tok, per_tok, mat, mat, row, state, out],
        out_specs=[tok, per_tok, per_tok, per_tok, per_tok, mat, mat, row],
        out_shape=(shp(qkv), shp(b_b), shp(be_b), shp(e_b), shp(kd_b), shp(m1), shp(dec), shp(gl)),
        scratch_shapes=[pltpu.VMEM((GDN_HEADS, HEAD_DIM, HEAD_DIM), F32)],
        compiler_params=pltpu.CompilerParams(dimension_semantics=("arbitrary",), vmem_limit_bytes=VMEM_LIMIT),
    )(qkv, b_b, be_b, e_b, kd_b, m1, dec, gl, sall, do)


@functools.partial(jax.custom_vjp, nondiff_argnums=(0, 1, 2))
def gdn_scan(name, reverse, ncc, qkv, b_b, be_b, e_b, kd_b, m1, dec, gl):
    return _gdn_fwd_call(name, reverse, ncc, qkv, b_b, be_b, e_b, kd_b, m1, dec, gl)[0]


def _gdn_scan_fwd(name, reverse, ncc, *args):
    o, sall = _gdn_fwd_call(name, reverse, ncc, *args)
    return o, (args, sall)


def _gdn_scan_bwd(name, reverse, ncc, res, do):
    args, sall = res
    return _gdn_bwd_call(name, reverse, ncc, *args, sall, do)


gdn_scan.defvjp(_gdn_scan_fwd, _gdn_scan_bwd)


def _rope_tables(n, cl):
    t = np.arange(n)
    inv_freq = (ROPE_THETA ** (-np.arange(0, HEAD_DIM // 2, 2, dtype=np.float32) / (HEAD_DIM // 2))).astype(np.float32)
    ang_r = (t // GRID_W).astype(np.float32)[:, None] * inv_freq
    ang_c = (t % GRID_W).astype(np.float32)[:, None] * inv_freq
    cos = np.concatenate([np.cos(ang_r), np.cos(ang_r), np.cos(ang_c), np.cos(ang_c)], axis=1)
    sin = np.concatenate([-np.sin(ang_r), np.sin(ang_r), -np.sin(ang_c), np.sin(ang_c)], axis=1)
    cos_all = np.concatenate([np.ones((cl, HEAD_DIM), np.float32), cos], axis=0)
    sin_all = np.concatenate([np.zeros((cl, HEAD_DIM), np.float32), sin], axis=0)
    j = np.arange(HEAD_DIM)
    src = np.where((j % 64) < 32, j + 32, j - 32)
    perm = np.zeros((HEAD_DIM, HEAD_DIM), np.float32)
    perm[src, j] = 1.0
    return (jnp.asarray(cos.astype(np.float32)), jnp.asarray(sin.astype(np.float32)),
            jnp.asarray(cos_all), jnp.asarray(sin_all), jnp.asarray(perm))


def _shift_rows(a, cl):
    z = jnp.zeros((1, a.shape[1]), a.dtype)
    parts = [a[:cl], a[cl:]] if cl else [a]
    prev = jnp.concatenate([jnp.concatenate([z, p[:-1]], axis=0) for p in parts], axis=0)
    nxt = jnp.concatenate([jnp.concatenate([p[1:], z], axis=0) for p in parts], axis=0)
    return prev, nxt


def _gdn_factors(log_a, beta, reverse):
    t = log_a.shape[0]
    nc = t // _C
    la = log_a.reshape(nc, _C, GDN_HEADS).transpose(2, 0, 1)
    be = beta.reshape(nc, _C, GDN_HEADS).transpose(2, 0, 1)
    gam = lax.cumsum(la, axis=2, reverse=reverse)
    idx = jnp.arange(_C)
    incl = (idx[:, None] <= idx[None, :]) if reverse else (idx[:, None] >= idx[None, :])
    strict = (idx[:, None] < idx[None, :]) if reverse else (idx[:, None] > idx[None, :])
    dec = jnp.exp(jnp.where(incl, gam[..., :, None] - gam[..., None, :], -jnp.inf))
    m1 = jnp.where(strict, be[..., :, None] * dec, 0.0)
    e = jnp.exp(gam)
    g_last = gam[..., :1] if reverse else gam[..., -1:]
    lanes = lambda a: jnp.broadcast_to(a.reshape(GDN_HEADS, t, 1), (GDN_HEADS, t, HEAD_DIM))
    gl = jnp.broadcast_to(jnp.exp(g_last)[..., None], (GDN_HEADS, nc, 1, HEAD_DIM))
    return lanes(be), lanes(be * e), lanes(e), lanes(jnp.exp(g_last - gam)), m1, dec, gl


def local_loss(x, wz, wb, ws, c, ctx, target):
    n, cl = x.shape[0], ctx.shape[0]
    cos_q, sin_q, cos_k, sin_k, perm = _rope_tables(n, cl)

    sc_in = jnp.concatenate([jax.nn.silu(c), jax.nn.silu(ws["c_ctx"])[None, :], jnp.zeros((14, D_MODEL), F32)], axis=0)
    mod = pmm(sc_in, wb["w_mod"], wz["w_mod"], "mm_mod") + ws["b_mod"]
    sh1, sc1, g1, sh2, sc2, g2 = [mod[0:1, i * D_MODEL:(i + 1) * D_MODEL] for i in range(6)]
    csh1, csc1 = mod[1:2, 0:D_MODEL], mod[1:2, D_MODEL:2 * D_MODEL]

    def norm_mod(a, sh, sc):
        return (_rms(a) * (1.0 + sc) + sh,)

    (hx,) = rowop(norm_mod, "normmod_x", (x,), (sh1, sc1))
    (hc,) = rowop(norm_mod, "normmod_c", (ctx,), (csh1, csc1))
    h_all = jnp.concatenate([hc, hx], axis=0)
    p_main = pmm_t(h_all, wb["w_in_main"], wz["w_in_main"], "mm_in")
    p_small = pmm_t(h_all, wb["w_in_small"], wz["w_in_small"], "mm_ins")
    ak, av, qkv, aq, z, gate = jnp.split(p_main, [KV_W, 2 * KV_W, SMALL_AT, SMALL_AT + Q_W, SMALL_AT + Q_W + GDN_W],
                                         axis=1)
    db, da = p_small[:, :2 * GDN_HEADS], p_small[:, 2 * GDN_HEADS:4 * GDN_HEADS]

    def qk_prep(nh):
        def fn(a, w, cos, sin, pm):
            outs = []
            for ah in _heads(a, nh):
                y = _rms(ah) * w
                outs.append(y * cos + _hdot(y, pm) * sin)
            return (jnp.concatenate(outs, axis=1),)
        return fn

    (q_x,) = rowop(qk_prep(ATTN_HEADS), "q_prep", (aq[cl:],), (ws["q_norm_w"],), (cos_q, sin_q), (perm,))
    (k_all,) = rowop(qk_prep(ATTN_KV_HEADS), "k_prep", (ak,), (ws["k_norm_w"],), (cos_k, sin_k), (perm,))
    attn_x = attention(q_x, k_all, av)

    qkv_prev, qkv_next = _shift_rows(qkv, cl)
    cw = ws["conv_qkv_w"]

    def gdn_prep(a, ap, an, w0, w1, w2):
        s = jax.nn.silu(ap * w0 + a * w1 + an * w2)
        outs = []
        for i, sh in enumerate(_heads(s, 3 * GDN_HEADS)):
            if i < 2 * GDN_HEADS:
                sh = sh * lax.rsqrt(jnp.sum(sh * sh, axis=-1, keepdims=True) + NORM_EPS)
                if i < GDN_HEADS:
                    sh = sh * (HEAD_DIM ** -0.5)
            outs.append(sh)
        return (jnp.concatenate(outs, axis=1),)

    (qkvn,) = rowop(gdn_prep, "gdn_prep", (qkv, qkv_prev, qkv_next), (cw[0:1], cw[1:2], cw[2:3]), tr=128)
    beta = jax.nn.sigmoid(db).reshape(-1, 2, GDN_HEADS)
    log_a = -jnp.exp(ws["a_log"])[None] * jax.nn.softplus(da.reshape(-1, 2, GDN_HEADS) + ws["dt_bias"][None])
    o_fwd = gdn_scan("gdn_d0", False, cl // _C, qkvn, *_gdn_factors(log_a[:, 0], beta[:, 0], False))
    o_bwd = gdn_scan("gdn_d1", True, cl // _C, qkvn, *_gdn_factors(log_a[:, 1], beta[:, 1], True))
    o_x = o_fwd[cl:] + o_bwd[cl:]

    def gdn_out(o, zz, w):
        outs = [_rms(oh) * w * jax.nn.silu(zh) for oh, zh in zip(_heads(o, GDN_HEADS), _heads(zz, GDN_HEADS))]
        return (jnp.concatenate(outs, axis=1),)

    (gdn_x,) = rowop(gdn_out, "gdn_out", (o_x, z[cl:]), (ws["gdn_norm_w"],))

    pa = pmm(attn_x, wb["w_pa"], wz["w_pa"], "mm_pa")
    pd = pmm(gdn_x, wb["w_pd"], wz["w_pd"], "mm_pd")

    def merge(a, d, g):
        return (jax.nn.sigmoid(g[:, :D_MODEL]) * a + jax.nn.sigmoid(g[:, D_MODEL:]) * d,)

    (y,) = rowop(merge, "merge", (pa, pd, gate[cl:]))
    mo = pmm(y, wb["w_out"], wz["w_out"], "mm_out")

    def res_norm_mod(xx, m, g, sh, sc):
        x1 = xx + g * m
        return x1, _rms(x1) * (1.0 + sc) + sh

    x1, h2 = rowop(res_norm_mod, "res1", (x, mo), (g1, sh2, sc2))
    up = pmm(h2, wb["w_up"], wz["w_up"], "mm_up")
    up_prev, up_next = _shift_rows(up, 0)
    fw = ws["ffn_conv_w"]

    def ffn_act(a, ap, an, w0, w1, w2, b):
        u = ap * w0 + a * w1 + an * w2 + b
        return (jax.nn.silu(u[:, :D_FF]) * u[:, D_FF:],)

    (act,) = rowop(ffn_act, "ffn_act", (up, up_prev, up_next), (fw[0:1], fw[1:2], fw[2:3], ws["ffn_conv_b"]), tr=128)
    dn = pmm(act, wb["w_down"], wz["w_down"], "mm_down")

    def head(xx, m, g, w, tgt):
        yy = _rms(xx + g * m) * w
        err = (yy - tgt) ** 2
        return (jnp.broadcast_to(0.5 * jnp.mean(err, axis=-1, keepdims=True), (xx.shape[0], HEAD_DIM)),)

    (row_loss,) = rowop(head, "head", (x1, dn), (g2, ws["final_norm_w"][None, :]), (target,))
    return jnp.sum(row_loss[:, 0])


_HBM = pl.BlockSpec(memory_space=pltpu.HBM)


def _chip_peers():
    x, y = lax.axis_index("x"), lax.axis_index("y")
    return [(1 - x, y), (x, 1 - y), (1 - x, 1 - y)]


def all_gather_chips(pack):
    def body(p_ref, out_ref, send_sems, recv_sems, local_sem):
        x, y, c = lax.axis_index("x"), lax.axis_index("y"), lax.axis_index("c")
        mine = pltpu.make_async_copy(p_ref, out_ref.at[2 * x + y], local_sem)
        mine.start()
        sends = [pltpu.make_async_remote_copy(src_ref=p_ref, dst_ref=out_ref.at[2 * x + y], send_sem=send_sems.at[k],
                                              recv_sem=recv_sems.at[k], device_id=(px, py, c), device_id_type=MESH)
                 for k, (px, py) in enumerate(_chip_peers())]
        for cp in sends:
            cp.start()
        for k, (px, py) in enumerate(_chip_peers()):
            pltpu.make_async_remote_copy(src_ref=p_ref, dst_ref=out_ref.at[2 * px + py], send_sem=send_sems.at[k],
                                         recv_sem=recv_sems.at[k], device_id=(px, py, c),
                                         device_id_type=MESH).wait_recv()
        for cp in sends:
            cp.wait_send()
        mine.wait()

    return pl.pallas_call(
        body, name="ag_weights", in_specs=[_HBM], out_specs=_HBM,
        out_shape=jax.ShapeDtypeStruct((N_CHIPS,) + pack.shape, pack.dtype),
        scratch_shapes=[pltpu.SemaphoreType.DMA((3,)), pltpu.SemaphoreType.DMA((3,)), pltpu.SemaphoreType.DMA],
    )(pack)


def scatter_to_chips(blocks):
    def body(b_ref, out_ref, send_sems, recv_sems):
        c = lax.axis_index("c")
        sends = [pltpu.make_async_remote_copy(src_ref=b_ref.at[2 * px + py], dst_ref=out_ref.at[k],
                                              send_sem=send_sems.at[k], recv_sem=recv_sems.at[k],
                                              device_id=(px, py, c), device_id_type=MESH)
                 for k, (px, py) in enumerate(_chip_peers())]
        for cp in sends:
            cp.start()
        for cp in sends:
            cp.wait_recv()
        for cp in sends:
            cp.wait_send()

    return pl.pallas_call(
        body, name="rs_grads", in_specs=[_HBM], out_specs=_HBM,
        out_shape=jax.ShapeDtypeStruct((3,) + blocks.shape[1:], blocks.dtype),
        scratch_shapes=[pltpu.SemaphoreType.DMA((3,)), pltpu.SemaphoreType.DMA((3,))],
    )(blocks)


def sibling_swap(a):
    def body(a_ref, out_ref, send_sem, recv_sem):
        x, y, c = lax.axis_index("x"), lax.axis_index("y"), lax.axis_index("c")
        cp = pltpu.make_async_remote_copy(src_ref=a_ref, dst_ref=out_ref, send_sem=send_sem, recv_sem=recv_sem,
                                          device_id=(x, y, 1 - c), device_id_type=MESH)
        cp.start()
        cp.wait()

    return pl.pallas_call(
        body, name="sib_swap", in_specs=[_HBM], out_specs=_HBM,
        out_shape=jax.ShapeDtypeStruct(a.shape, a.dtype),
        scratch_shapes=[pltpu.SemaphoreType.DMA, pltpu.SemaphoreType.DMA],
    )(a)


def all_reduce_small(v):
    def body(v_ref, tot_ref, gath_ref, send_sems, recv_sems):
        x, y, c = lax.axis_index("x"), lax.axis_index("y"), lax.axis_index("c")
        me = 4 * x + 2 * y + c
        gath_ref[me] = v_ref[...]

        def peer(k):
            m = k + 1
            return (x ^ (m >> 2 & 1), y ^ (m >> 1 & 1), c ^ (m & 1))

        sends = [pltpu.make_async_remote_copy(src_ref=v_ref, dst_ref=gath_ref.at[me], send_sem=send_sems.at[k],
                                              recv_sem=recv_sems.at[k], device_id=peer(k), device_id_type=MESH)
                 for k in range(N_DEV - 1)]
        for cp in sends:
            cp.start()
        for k in range(N_DEV - 1):
            px, py, pc = peer(k)
            pltpu.make_async_remote_copy(src_ref=v_ref, dst_ref=gath_ref.at[4 * px + 2 * py + pc],
                                         send_sem=send_sems.at[k], recv_sem=recv_sems.at[k], device_id=peer(k),
                                         device_id_type=MESH).wait_recv()
        for cp in sends:
            cp.wait_send()
        acc = gath_ref[0]
        for d in range(1, N_DEV):
            acc = acc + gath_ref[d]
        tot_ref[...] = acc

    vm = pl.BlockSpec(memory_space=pltpu.VMEM)
    return pl.pallas_call(
        body, name="ar_small", in_specs=[vm], out_specs=[vm, vm],
        out_shape=(jax.ShapeDtypeStruct(v.shape, v.dtype), jax.ShapeDtypeStruct((N_DEV,) + v.shape, v.dtype)),
        scratch_shapes=[pltpu.SemaphoreType.DMA((N_DEV - 1,)), pltpu.SemaphoreType.DMA((N_DEV - 1,))],
    )(v)[0]


def _elementwise(fn, name, ins, n_out, out_dtype=F32):
    r, cdim = ins[0].shape
    tr = _pick(r, tuple(p for p in (488, 256, 128, 104, 64, 32, 16, 8) if p * cdim * 4 <= 2 * 1024 * 1024))
    spec = pl.BlockSpec((tr, cdim), lambda i: (i, 0))

    def body(*refs):
        res = fn(*[ref[...] for ref in refs[:len(ins)]])
        for o_ref, v in zip(refs[len(ins):], res):
            o_ref[...] = v

    return pl.pallas_call(
        body, name=name, grid=(r // tr,), in_specs=[spec] * len(ins), out_specs=[spec] * n_out,
        out_shape=tuple(jax.ShapeDtypeStruct((r, cdim), out_dtype) for _ in range(n_out)),
        compiler_params=pltpu.CompilerParams(dimension_semantics=("parallel",), vmem_limit_bytes=VMEM_LIMIT),
    )(*ins)


def _adamw(w, ga, gb, m, v, name):
    shape = w.shape
    to2 = lambda a: a.reshape(-1, shape[-1])

    def fn(w_, ga_, gb_, m_, v_):
        g = ga_ + gb_
        m_new = ADAM_B1 * m_ + (1.0 - ADAM_B1) * g
        v_new = ADAM_B2 * v_ + (1.0 - ADAM_B2) * (g * g)
        m_hat = m_new / (1.0 - ADAM_B1 ** ADAM_STEP)
        v_hat = v_new / (1.0 - ADAM_B2 ** ADAM_STEP)
        delta = -ADAM_LR * (m_hat / (jnp.sqrt(v_hat) + ADAM_EPS) + ADAM_WD * w_)
        return g, delta, m_new, v_new

    outs = _elementwise(fn, name, [to2(a) for a in (w, ga, gb, m, v)], 4)
    return tuple(o.reshape(shape) for o in outs)


_BIG = ("w_mod", "w_in", "w_pa", "w_pd", "w_out", "w_up", "w_down")
_COL_SHARDED = ("w_mod", "w_up")
_FULL_SHAPE = {"w_mod": (D_MODEL, MOD_W), "w_in": (IN_COLS, D_MODEL), "w_pa": (Q_W, D_MODEL), "w_pd": (GDN_W, D_MODEL),
               "w_out": (D_MODEL, D_MODEL), "w_up": (D_MODEL, 2 * D_FF), "w_down": (D_FF, D_MODEL)}


def _shard_shape(name):
    r, cdim = _FULL_SHAPE[name]
    return (r, cdim // N_CHIPS) if name in _COL_SHARDED else (r // N_CHIPS, cdim)


_BIG_ELEMS = sum(int(np.prod(_shard_shape(nm))) for nm in _BIG)
_BIG_ROWS = _BIG_ELEMS // D_MODEL
_CONV_ELEMS = 2 * (3 * CONV_W // N_CHIPS + 3 * 2 * D_FF // N_CHIPS)
_CONV_ROWS = 24


def _blocks_of_full(name, full):
    r, cdim = _FULL_SHAPE[name]
    if name in _COL_SHARDED:
        return full.reshape(r, N_CHIPS, cdim // N_CHIPS).transpose(1, 0, 2).reshape(N_CHIPS, -1)
    return full.reshape(N_CHIPS, -1)


def _full_of_blocks(name, blocks):
    r, cdim = _FULL_SHAPE[name]
    if name in _COL_SHARDED:
        return blocks.reshape(N_CHIPS, r, cdim // N_CHIPS).transpose(1, 0, 2).reshape(r, cdim)
    return blocks.reshape(r, cdim)


def _split_big(flat):
    out, off = {}, 0
    for nm in _BIG:
        size = int(np.prod(_shard_shape(nm)))
        out[nm] = flat[..., off:off + size]
        off += size
    return out


def _w_in_regroup(w_in_t):
    main = jnp.concatenate([w_in_t[:SMALL_AT], w_in_t[SMALL_AT + 4 * GDN_HEADS:]], axis=0)
    small = jnp.pad(w_in_t[SMALL_AT:SMALL_AT + 4 * GDN_HEADS], ((0, HEAD_DIM - 4 * GDN_HEADS), (0, 0)))
    return main, small


def _w_in_ungroup(main, small):
    return jnp.concatenate([main[:SMALL_AT], small[:4 * GDN_HEADS], main[SMALL_AT:]], axis=0)


_SMALL = ("c_ctx", "b_mod", "q_norm_w", "k_norm_w", "conv_qkv_w", "a_log", "dt_bias", "gdn_norm_w", "ffn_conv_w",
          "ffn_conv_b", "final_norm_w")


def _pack_small(tree, rows):
    flat = jnp.concatenate([tree[nm].reshape(-1) for nm in _SMALL])
    return jnp.pad(flat, (0, rows * 128 - flat.shape[0])).reshape(rows, 128)


def _unpack_small(packed, like):
    flat, out, off = packed.reshape(-1), {}, 0
    for nm in _SMALL:
        size = int(np.prod(like[nm].shape))
        out[nm] = flat[off:off + size].reshape(like[nm].shape)
        off += size
    return out


def kernel(x, c, ctx, c_ctx, w_mod, b_mod, w_in, q_norm_w, k_norm_w, conv_qkv_w, a_log, dt_bias, gdn_norm_w, w_pa, w_pd, w_out, w_up, ffn_conv_w, ffn_conv_b, w_down, final_norm_w, loss_target, m_c_ctx, m_w_mod, m_b_mod, m_w_in, m_q_norm_w, m_k_norm_w, m_conv_qkv_w, m_a_log, m_dt_bias, m_gdn_norm_w, m_w_pa, m_w_pd, m_w_out, m_w_up, m_ffn_conv_w, m_ffn_conv_b, m_w_down, m_final_norm_w, v_c_ctx, v_w_mod, v_b_mod, v_w_in, v_q_norm_w, v_k_norm_w, v_conv_qkv_w, v_a_log, v_dt_bias, v_gdn_norm_w, v_w_pa, v_w_pd, v_w_out, v_w_up, v_ffn_conv_w, v_ffn_conv_b, v_w_down, v_final_norm_w):
    names = ("c_ctx", "w_mod", "b_mod", "w_in", "q_norm_w", "k_norm_w", "conv_qkv_w", "a_log", "dt_bias", "gdn_norm_w",
             "w_pa", "w_pd", "w_out", "w_up", "ffn_conv_w", "ffn_conv_b", "w_down", "final_norm_w")
    w_sh = dict(c_ctx=c_ctx, w_mod=w_mod, b_mod=b_mod, w_in=w_in, q_norm_w=q_norm_w, k_norm_w=k_norm_w,
                conv_qkv_w=conv_qkv_w, a_log=a_log, dt_bias=dt_bias, gdn_norm_w=gdn_norm_w, w_pa=w_pa, w_pd=w_pd,
                w_out=w_out, w_up=w_up, ffn_conv_w=ffn_conv_w, ffn_conv_b=ffn_conv_b, w_down=w_down,
                final_norm_w=final_norm_w)
    m_sh = dict(c_ctx=m_c_ctx, w_mod=m_w_mod, b_mod=m_b_mod, w_in=m_w_in, q_norm_w=m_q_norm_w, k_norm_w=m_k_norm_w,
                conv_qkv_w=m_conv_qkv_w, a_log=m_a_log, dt_bias=m_dt_bias, gdn_norm_w=m_gdn_norm_w, w_pa=m_w_pa,
                w_pd=m_w_pd, w_out=m_w_out, w_up=m_w_up, ffn_conv_w=m_ffn_conv_w, ffn_conv_b=m_ffn_conv_b,
                w_down=m_w_down, final_norm_w=m_final_norm_w)
    v_sh = dict(c_ctx=v_c_ctx, w_mod=v_w_mod, b_mod=v_b_mod, w_in=v_w_in, q_norm_w=v_q_norm_w, k_norm_w=v_k_norm_w,
                conv_qkv_w=v_conv_qkv_w, a_log=v_a_log, dt_bias=v_dt_bias, gdn_norm_w=v_gdn_norm_w, w_pa=v_w_pa,
                w_pd=v_w_pd, w_out=v_w_out, w_up=v_w_up, ffn_conv_w=v_ffn_conv_w, ffn_conv_b=v_ffn_conv_b,
                w_down=v_w_down, final_norm_w=v_final_norm_w)
    chip = 2 * lax.axis_index("x") + lax.axis_index("y")

    conv_bits = jnp.concatenate([lax.bitcast_convert_type(w_sh[nm][0], BF16).reshape(-1)
                                 for nm in ("conv_qkv_w", "ffn_conv_w")])
    as_sent = lambda nm: w_sh[nm][0].astype(BF16).T if nm == "w_in" else w_sh[nm][0].astype(BF16)
    pack = jnp.concatenate([as_sent(nm).reshape(-1) for nm in _BIG]
                           + [conv_bits, jnp.zeros((_CONV_ROWS * D_MODEL - _CONV_ELEMS,), BF16)])
    gathered = all_gather_chips(pack.reshape(_BIG_ROWS + _CONV_ROWS, D_MODEL)).reshape(N_CHIPS, -1)
    big_blocks = _split_big(gathered[:, :_BIG_ELEMS])
    wb = {nm: _full_of_blocks(nm, big_blocks[nm]) for nm in _BIG}
    wb["w_in_main"], wb["w_in_small"] = _w_in_regroup(wb.pop("w_in"))
    conv_all = gathered[:, _BIG_ELEMS:_BIG_ELEMS + _CONV_ELEMS]
    n_cq = 2 * 3 * CONV_W // N_CHIPS
    unbits = lambda a, w: lax.bitcast_convert_type(a.reshape(N_CHIPS, 3, w // N_CHIPS, 2), F32).transpose(1, 0, 2).reshape(3, w)
    ws = dict(c_ctx=c_ctx, b_mod=b_mod, q_norm_w=q_norm_w, k_norm_w=k_norm_w, a_log=a_log[0], dt_bias=dt_bias[0],
              gdn_norm_w=gdn_norm_w, ffn_conv_b=ffn_conv_b, final_norm_w=final_norm_w,
              conv_qkv_w=unbits(conv_all[:, :n_cq], CONV_W), ffn_conv_w=unbits(conv_all[:, n_cq:], 2 * D_FF))
    wz = {nm: jnp.zeros(a.shape, F32) for nm, a in wb.items()}

    loss_local, (gx, gz, gs) = jax.value_and_grad(local_loss, argnums=(0, 1, 3))(
        x[0], wz, wb, ws, c, ctx[0], loss_target[0])
    loss = lax.psum(loss_local, ("x", "y", "c"))

    gz["w_in"] = _w_in_ungroup(gz.pop("w_in_main"), gz.pop("w_in_small"))
    g_blocks = jnp.concatenate([_blocks_of_full(nm, gz[nm]) for nm in _BIG], axis=1)
    g_own = lax.dynamic_index_in_dim(g_blocks, chip, axis=0, keepdims=False).reshape(_BIG_ROWS, D_MODEL)
    got = scatter_to_chips(g_blocks.astype(BF16).reshape(N_CHIPS, _BIG_ROWS, D_MODEL))
    (partial,) = _elementwise(lambda a, r0, r1, r2: (a + r0.astype(F32) + r1.astype(F32) + r2.astype(F32),),
                              "rs_sum", [g_own, got[0], got[1], got[2]], 1)
    other = sibling_swap(partial)
    part_a, part_b = _split_big(partial.reshape(-1)), _split_big(other.reshape(-1))

    gs["a_log"], gs["dt_bias"] = gs["a_log"][None], gs["dt_bias"][None]
    like = {nm: gs[nm] for nm in _SMALL}
    small_rows = -(-sum(int(np.prod(like[nm].shape)) for nm in _SMALL) // 1024) * 8
    g_small = _unpack_small(all_reduce_small(_pack_small(gs, small_rows)), like)
    for nm, width in (("conv_qkv_w", CONV_W), ("ffn_conv_w", 2 * D_FF)):
        g_small[nm] = lax.dynamic_slice_in_dim(g_small[nm], chip * (width // N_CHIPS), width // N_CHIPS, axis=1)[None]

    grads, deltas, new_m, new_v = {}, {}, {}, {}
    for nm in _BIG:
        shp = _shard_shape(nm)
        as_kept = lambda part: part.reshape(shp).T if nm == "w_in" else part.reshape(shp)
        grads[nm], deltas[nm], new_m[nm], new_v[nm] = (
            o[None] for o in _adamw(w_sh[nm][0], as_kept(part_a[nm]), as_kept(part_b[nm]), m_sh[nm][0],
                                    v_sh[nm][0], "adamw_" + nm))
    shard_like = {nm: w_sh[nm] for nm in _SMALL}
    rows_l = -(-sum(int(np.prod(shard_like[nm].shape)) for nm in _SMALL) // 1024) * 8
    g_l = _pack_small({nm: g_small[nm].reshape(w_sh[nm].shape) for nm in _SMALL}, rows_l)
    outs = _adamw(_pack_small(w_sh, rows_l), g_l, jnp.zeros_like(g_l), _pack_small(m_sh, rows_l),
                  _pack_small(v_sh, rows_l), "adamw_small")
    for tree, packed in zip((grads, deltas, new_m, new_v), outs):
        tree.update(_unpack_small(packed, shard_like))

    return (loss, gx[None], *[grads[nm] for nm in names], *[deltas[nm] for nm in names],
            *[new_m[nm] for nm in names], *[new_v[nm] for nm in names])
```

```python
import functools
import math

import jax
import jax.numpy as jnp
import numpy as np
from jax import lax
from jax.experimental import pallas as pl
from jax.experimental.pallas import tpu as pltpu

F32 = jnp.float32
BF16 = jnp.bfloat16
HIGHEST = lax.Precision.HIGHEST
MESH = pl.DeviceIdType.MESH

D_MODEL = 1024
GRID_W = 64
ATTN_HEADS = 8
ATTN_KV_HEADS = 2
ATTN_GROUP = ATTN_HEADS // ATTN_KV_HEADS
HEAD_DIM = 128
ROPE_THETA = 10000.0
GDN_HEADS = 8
GDN_CHUNK = 64
D_FF = 2816
NORM_EPS = 1e-6
KV_W = ATTN_KV_HEADS * HEAD_DIM
Q_W = ATTN_HEADS * HEAD_DIM
GDN_W = GDN_HEADS * HEAD_DIM
CONV_W = 3 * GDN_W
MOD_W = 6 * D_MODEL
IN_COLS = 2 * KV_W + CONV_W + 4 * GDN_HEADS + Q_W + GDN_W + 2 * D_MODEL
IN_MAIN = IN_COLS - 4 * GDN_HEADS
SMALL_AT = 2 * KV_W + CONV_W
N_CHIPS = 4
N_DEV = 8

ADAM_LR = 0.001
ADAM_B1 = 0.9
ADAM_B2 = 0.999
ADAM_EPS = 1e-08
ADAM_WD = 0.01
ADAM_STEP = 10

VMEM_LIMIT = 48 * 1024 * 1024


def _pick(dim, prefs):
    for p in prefs:
        if p <= dim and dim % p == 0:
            return p
    return dim


_DIMS = {
    "nn": (((1,), (0,)), ((), ())),
    "nt": (((1,), (1,)), ((), ())),
    "tn": (((0,), (0,)), ((), ())),
}


def _matmul(a, b, mode, name):
    if mode == "nn":
        (m, k), (_, n) = a.shape, b.shape
    elif mode == "nt":
        (m, k), (n, _) = a.shape, b.shape
    else:
        (k, m), (_, n) = a.shape, b.shape
    tm = _pick(m, (512, 384, 256, 128))
    tn = _pick(n, (1536, 1408, 1024, 768, 512, 256, 128))
    tk = _pick(k, (1024, 1408, 768, 512, 256, 128))
    nk = k // tk
    if mode == "tn":
        a_spec = pl.BlockSpec((tk, tm), lambda i, j, l: (l, i))
    else:
        a_spec = pl.BlockSpec((tm, tk), lambda i, j, l: (i, l))
    if mode == "nt":
        b_spec = pl.BlockSpec((tn, tk), lambda i, j, l: (j, l))
    else:
        b_spec = pl.BlockSpec((tk, tn), lambda i, j, l: (l, j))
    dims = _DIMS[mode]

    def body(a_ref, b_ref, o_ref, acc_ref):
        l = pl.program_id(2)

        @pl.when(l == 0)
        def _():
            acc_ref[...] = jnp.zeros_like(acc_ref)

        acc_ref[...] += lax.dot_general(a_ref[...].astype(BF16), b_ref[...].astype(BF16), dims,
                                        preferred_element_type=F32)

        @pl.when(l == nk - 1)
        def _():
            o_ref[...] = acc_ref[...]

    return pl.pallas_call(
        body,
        name=name,
        grid=(m // tm, n // tn, nk),
        in_specs=[a_spec, b_spec],
        out_specs=pl.BlockSpec((tm, tn), lambda i, j, l: (i, j)),
        out_shape=jax.ShapeDtypeStruct((m, n), F32),
        scratch_shapes=[pltpu.VMEM((tm, tn), F32)],
        compiler_params=pltpu.CompilerParams(dimension_semantics=("parallel", "parallel", "arbitrary"),
                                             vmem_limit_bytes=VMEM_LIMIT),
    )(a, b)


@functools.partial(jax.custom_vjp, nondiff_argnums=(3,))
def pmm(a, w, wz, name):
    del wz
    return _matmul(a, w, "nn", name + "_f")


def _pmm_fwd(a, w, wz, name):
    del wz
    return _matmul(a, w, "nn", name + "_f"), (a, w)


def _pmm_bwd(name, res, g):
    a, w = res
    da = _matmul(g, w, "nt", name + "_da")
    if a.shape[0] < 128:
        pad = 128 - a.shape[0]
        at = jnp.pad(a.T, ((0, 0), (0, pad)))
        gp = jnp.pad(g, ((0, pad), (0, 0)))
        dw = _matmul(at, gp, "nn", name + "_dw")
    else:
        dw = _matmul(a, g, "tn", name + "_dw")
    return da, jnp.zeros_like(w), dw


pmm.defvjp(_pmm_fwd, _pmm_bwd)


@functools.partial(jax.custom_vjp, nondiff_argnums=(3,))
def pmm_t(a, wt, wtz, name):
    del wtz
    return _matmul(a, wt, "nt", name + "_f")


def _pmm_t_fwd(a, wt, wtz, name):
    del wtz
    return _matmul(a, wt, "nt", name + "_f"), (a, wt)


def _pmm_t_bwd(name, res, g):
    a, wt = res
    return _matmul(g, wt, "nn", name + "_da"), jnp.zeros_like(wt), _matmul(g, a, "tn", name + "_dw")


pmm_t.defvjp(_pmm_t_fwd, _pmm_t_bwd)


def rowop(fn, name, rows, bcs=(), crows=(), cbcs=(), tr=256):
    rows, bcs, crows, cbcs = tuple(rows), tuple(bcs), tuple(crows), tuple(cbcs)
    n_rows = rows[0].shape[0]
    tr = _pick(n_rows, (tr, 128, 64, 32, 16, 8))
    nr, nb, ncr, ncb = len(rows), len(bcs), len(crows), len(cbcs)
    n_in = nr + nb + ncr + ncb
    grid = (n_rows // tr,)

    def blk(arr):
        return jax.ShapeDtypeStruct((tr, arr.shape[1]), arr.dtype)

    def row_spec(arr):
        return pl.BlockSpec((tr, arr.shape[1]), lambda i: (i, 0))

    def bc_spec(arr):
        return pl.BlockSpec(arr.shape, lambda i: (0, 0))

    out_blk = jax.eval_shape(fn, *[blk(r) for r in rows], *bcs, *[blk(r) for r in crows], *cbcs)
    n_out = len(out_blk)
    out_shape = tuple(jax.ShapeDtypeStruct((n_rows, o.shape[1]), o.dtype) for o in out_blk)
    in_specs = ([row_spec(r) for r in rows] + [bc_spec(b) for b in bcs]
                + [row_spec(r) for r in crows] + [bc_spec(b) for b in cbcs])

    def order(vals):
        return vals

    def fwd_call(args):
        def body(*refs):
            vals = [r[...] for r in refs[:n_in]]
            res = fn(*order(vals))
            for o_ref, r in zip(refs[n_in:], res):
                o_ref[...] = r

        return pl.pallas_call(
            body, name=name + "_f", grid=grid, in_specs=in_specs,
            out_specs=[row_spec(o) for o in out_shape], out_shape=out_shape,
            compiler_params=pltpu.CompilerParams(dimension_semantics=("parallel",), vmem_limit_bytes=VMEM_LIMIT),
        )(*args)

    def bwd_call(args, cts):
        def body(*refs):
            vals = [r[...] for r in refs[:n_in]]
            ct_refs = refs[n_in:n_in + n_out]
            d_rows = refs[n_in + n_out:n_in + n_out + nr]
            d_bcs = refs[n_in + n_out + nr:]
            consts = vals[nr + nb:]
            _, vjp = jax.vjp(lambda *p: fn(*p, *consts), *vals[:nr + nb])
            grads = vjp(tuple(c[...] for c in ct_refs))
            for ref, g in zip(d_rows, grads[:nr]):
                ref[...] = g

            @pl.when(pl.program_id(0) == 0)
            def _():
                for ref in d_bcs:
                    ref[...] = jnp.zeros_like(ref)

            for ref, g in zip(d_bcs, grads[nr:]):
                ref[...] += g

        d_shape = tuple(jax.ShapeDtypeStruct(r.shape, r.dtype) for r in rows + bcs)
        return pl.pallas_call(
            body, name=name + "_b", grid=grid,
            in_specs=in_specs + [row_spec(o) for o in out_shape],
            out_specs=[row_spec(r) for r in rows] + [bc_spec(b) for b in bcs], out_shape=d_shape,
            compiler_params=pltpu.CompilerParams(dimension_semantics=("arbitrary",), vmem_limit_bytes=VMEM_LIMIT),
        )(*args, *cts)

    @jax.custom_vjp
    def op(diff, const):
        return fwd_call(diff + const)

    def op_fwd(diff, const):
        return fwd_call(diff + const), (diff, const)

    def op_bwd(res, cts):
        diff, const = res
        grads = bwd_call(diff + const, tuple(cts))
        return tuple(grads), tuple(jnp.zeros_like(c) for c in const)

    op.defvjp(op_fwd, op_bwd)
    return op(rows + bcs, crows + cbcs)


def _rms(x):
    return x * lax.rsqrt(jnp.mean(x * x, axis=-1, keepdims=True) + NORM_EPS)


def _heads(x, n):
    return [x[:, h * HEAD_DIM:(h + 1) * HEAD_DIM] for h in range(n)]


_NT = (((1,), (1,)), ((), ()))
_TN = (((0,), (0,)), ((), ()))
_TQ = 256


def _attn_probs(q, k):
    s = lax.dot_general(q, k, _NT, preferred_element_type=F32) * (HEAD_DIM ** -0.5)
    p = jnp.exp(s - jnp.max(s, axis=-1, keepdims=True))
    return p / jnp.sum(p, axis=-1, keepdims=True)


def _attn_fwd_call(q, k, v):
    n, t = q.shape[0], k.shape[0]
    tq = _pick(n, (_TQ, 128))

    def body(q_ref, k_ref, v_ref, o_ref):
        p = _attn_probs(q_ref[...].astype(BF16), k_ref[...].astype(BF16))
        o_ref[...] = jnp.dot(p.astype(BF16), v_ref[...].astype(BF16), preferred_element_type=F32)

    return pl.pallas_call(
        body, name="attn_f", grid=(ATTN_HEADS, n // tq),
        in_specs=[pl.BlockSpec((tq, HEAD_DIM), lambda h, i: (i, h)),
                  pl.BlockSpec((t, HEAD_DIM), lambda h, i: (0, h // ATTN_GROUP)),
                  pl.BlockSpec((t, HEAD_DIM), lambda h, i: (0, h // ATTN_GROUP))],
        out_specs=pl.BlockSpec((tq, HEAD_DIM), lambda h, i: (i, h)),
        out_shape=jax.ShapeDtypeStruct(q.shape, F32),
        compiler_params=pltpu.CompilerParams(dimension_semantics=("parallel", "parallel"),
                                             vmem_limit_bytes=VMEM_LIMIT),
    )(q, k, v)


def _attn_bwd_call(q, k, v, do):
    n, t = q.shape[0], k.shape[0]
    tq = _pick(n, (_TQ, 128))

    def body(q_ref, k_ref, v_ref, do_ref, dq_ref, dk_ref, dv_ref):
        @pl.when((pl.program_id(1) == 0) & (pl.program_id(2) == 0))
        def _():
            dk_ref[...] = jnp.zeros_like(dk_ref)
            dv_ref[...] = jnp.zeros_like(dv_ref)

        qb, kb, vb, dob = (r[...].astype(BF16) for r in (q_ref, k_ref, v_ref, do_ref))
        p = _attn_probs(qb, kb)
        dp = lax.dot_general(dob, vb, _NT, preferred_element_type=F32)
        ds = p * (dp - jnp.sum(p * dp, axis=-1, keepdims=True)) * (HEAD_DIM ** -0.5)
        dsb = ds.astype(BF16)
        dq_ref[...] = jnp.dot(dsb, kb, preferred_element_type=F32)
        dk_ref[...] += lax.dot_general(dsb, qb, _TN, preferred_element_type=F32)
        dv_ref[...] += lax.dot_general(p.astype(BF16), dob, _TN, preferred_element_type=F32)

    q_spec = pl.BlockSpec((tq, HEAD_DIM), lambda kh, g, i: (i, kh * ATTN_GROUP + g))
    kv_spec = pl.BlockSpec((t, HEAD_DIM), lambda kh, g, i: (0, kh))
    return pl.pallas_call(
        body, name="attn_b", grid=(ATTN_KV_HEADS, ATTN_GROUP, n // tq),
        in_specs=[q_spec, kv_spec, kv_spec, q_spec],
        out_specs=[q_spec, kv_spec, kv_spec],
        out_shape=(jax.ShapeDtypeStruct(q.shape, F32), jax.ShapeDtypeStruct(k.shape, F32),
                   jax.ShapeDtypeStruct(v.shape, F32)),
        compiler_params=pltpu.CompilerParams(dimension_semantics=("parallel", "arbitrary", "arbitrary"),
                                             vmem_limit_bytes=VMEM_LIMIT),
    )(q, k, v, do)


@jax.custom_vjp
def attention(q, k, v):
    return _attn_fwd_call(q, k, v)


def _attention_fwd(q, k, v):
    return _attn_fwd_call(q, k, v), (q, k, v)


def _attention_bwd(res, do):
    return _attn_bwd_call(*res, do)


attention.defvjp(_attention_fwd, _attention_bwd)


_C = GDN_CHUNK


def _hdot(a, b):
    return jnp.dot(a, b, precision=HIGHEST, preferred_element_type=F32)


def _each(fn, *lists):
    return [fn(*args) for args in zip(*lists)]


def _unit_lower_inverse(low, blockdiag):
    eye = (lax.broadcasted_iota(jnp.int32, (_C, _C), 0) == lax.broadcasted_iota(jnp.int32, (_C, _C), 1)).astype(F32)
    ld = _each(lambda a: a * blockdiag, low)
    lo = _each(lambda a, d: a - d, low, ld)
    l2 = _each(_hdot, ld, ld)
    l4 = _each(_hdot, l2, l2)
    l8 = _each(_hdot, l4, l4)
    td = _each(lambda d, a2: _hdot(eye - d, eye + a2), ld, l2)
    td = _each(lambda t, a4: _hdot(t, eye + a4), td, l4)
    td = _each(lambda t, a8: _hdot(t, eye + a8), td, l8)
    nn = _each(_hdot, td, lo)
    n2 = _each(_hdot, nn, nn)
    out = _each(lambda n, m2: _hdot(eye - n, eye + m2), nn, n2)
    return _each(_hdot, out, td)


def _gdn_chunks(heads, blockdiag):
    q, k, v, b_b, be_b, e_b, kd_b, m1, dec, gl, s = (list(col) for col in zip(*heads))
    f32dot = lambda a, b: jnp.dot(a, b, preferred_element_type=F32)
    nt = lambda a, b: lax.dot_general(a, b, _NT, preferred_element_type=F32)
    kk = _each(nt, k, k)
    t_inv = _unit_lower_inverse(_each(lambda m, a: m * a, m1, kk), blockdiag)
    u = _each(lambda t, b, x: _hdot(t, b * x), t_inv, b_b, v)
    w = _each(lambda t, b, x: _hdot(t, b * x), t_inv, be_b, k)
    delta = _each(lambda uu, ww, ss: uu - f32dot(ww, ss), u, w, s)
    p = _each(lambda d, qq, kx: d * nt(qq, kx), dec, q, k)
    o = _each(lambda qq, e, ss, pp, dd: f32dot(qq * e, ss) + f32dot(pp, dd), q, e_b, s, p, delta)
    s_new = _each(lambda g, ss, kx, kd, dd: g * ss + lax.dot_general(kx * kd, dd, _TN, preferred_element_type=F32),
                  gl, s, k, kd_b, delta)
    return o, s_new


def _blockdiag_mask():
    r = lax.broadcasted_iota(jnp.int32, (_C, _C), 0) >> 4
    c = lax.broadcasted_iota(jnp.int32, (_C, _C), 1) >> 4
    return (r == c).astype(F32)


def _gdn_specs(nc, ncc, reverse, backward):
    def ch(s):
        s = nc - 1 - s if backward else s
        return jnp.where(s < ncc, ncc - 1 - s, nc + ncc - 1 - s) if reverse else s

    tok = pl.BlockSpec((_C, 3 * GDN_W), lambda s: (ch(s), 0))
    out = pl.BlockSpec((_C, GDN_W), lambda s: (ch(s), 0))
    per_tok = pl.BlockSpec((GDN_HEADS, _C, HEAD_DIM), lambda s: (0, ch(s), 0))
    mat = pl.BlockSpec((GDN_HEADS, None, _C, _C), lambda s: (0, ch(s), 0, 0))
    row = pl.BlockSpec((GDN_HEADS, None, 1, HEAD_DIM), lambda s: (0, ch(s), 0, 0))
    state = pl.BlockSpec((GDN_HEADS, None, HEAD_DIM, HEAD_DIM), lambda s: (0, ch(s), 0, 0))
    return tok, out, per_tok, mat, row, state


def _head_cols(h, part):
    return slice((part * GDN_HEADS + h) * HEAD_DIM, (part * GDN_HEADS + h + 1) * HEAD_DIM)


def _gdn_fwd_call(name, reverse, ncc, qkv, b_b, be_b, e_b, kd_b, m1, dec, gl):
    t = qkv.shape[0]
    nc = t // _C
    tok, out, per_tok, mat, row, state = _gdn_specs(nc, ncc, reverse, False)

    def body(qkv_ref, b_ref, be_ref, e_ref, kd_ref, m1_ref, dec_ref, gl_ref, o_ref, sall_ref, s_ref):
        @pl.when(pl.program_id(0) == 0)
        def _():
            s_ref[...] = jnp.zeros_like(s_ref)

        bd = _blockdiag_mask()
        ins = [[qkv_ref[:, _head_cols(h, 0)], qkv_ref[:, _head_cols(h, 1)], qkv_ref[:, _head_cols(h, 2)],
                b_ref[h], be_ref[h], e_ref[h], kd_ref[h], m1_ref[h], dec_ref[h], gl_ref[h], s_ref[h]]
               for h in range(GDN_HEADS)]
        o, s_new = _gdn_chunks(ins, bd)
        for h in range(GDN_HEADS):
            sall_ref[h] = ins[h][10]
            o_ref[:, _head_cols(h, 0)] = o[h]
            s_ref[h] = s_new[h]

    return pl.pallas_call(
        body, name=name + "_f", grid=(nc,),
        in_specs=[tok, per_tok, per_tok, per_tok, per_tok, mat, mat, row],
        out_specs=[out, state],
        out_shape=(jax.ShapeDtypeStruct((t, GDN_W), F32),
                   jax.ShapeDtypeStruct((GDN_HEADS, nc, HEAD_DIM, HEAD_DIM), F32)),
        scratch_shapes=[pltpu.VMEM((GDN_HEADS, HEAD_DIM, HEAD_DIM), F32)],
        compiler_params=pltpu.CompilerParams(dimension_semantics=("arbitrary",), vmem_limit_bytes=VMEM_LIMIT),
    )(qkv, b_b, be_b, e_b, kd_b, m1, dec, gl)


def _gdn_bwd_call(name, reverse, ncc, qkv, b_b, be_b, e_b, kd_b, m1, dec, gl, sall, do):
    t = qkv.shape[0]
    nc = t // _C
    tok, out, per_tok, mat, row, state = _gdn_specs(nc, ncc, reverse, True)

    def body(qkv_ref, b_ref, be_ref, e_ref, kd_ref, m1_ref, dec_ref, gl_ref, sall_ref, do_ref,
             dqkv_ref, db_ref, dbe_ref, de_ref, dkd_ref, dm1_ref, ddec_ref, dgl_ref, ds_ref):
        @pl.when(pl.program_id(0) == 0)
        def _():
            ds_ref[...] = jnp.zeros_like(ds_ref)

        bd = _blockdiag_mask()
        ins = [[qkv_ref[:, _head_cols(h, 0)], qkv_ref[:, _head_cols(h, 1)], qkv_ref[:, _head_cols(h, 2)],
                b_ref[h], be_ref[h], e_ref[h], kd_ref[h], m1_ref[h], dec_ref[h], gl_ref[h], sall_ref[h]]
               for h in range(GDN_HEADS)]
        _, vjp = jax.vjp(lambda hs: _gdn_chunks(hs, bd), ins)
        (all_grads,) = vjp(([do_ref[:, _head_cols(h, 0)] for h in range(GDN_HEADS)],
                            [ds_ref[h] for h in range(GDN_HEADS)]))
        for h, grads in enumerate(all_grads):
            for part in range(3):
                dqkv_ref[:, _head_cols(h, part)] = grads[part]
            for ref, g in zip((db_ref, dbe_ref, de_ref, dkd_ref, dm1_ref, ddec_ref, dgl_ref), grads[3:10]):
                ref[h] = g
            ds_ref[h] = grads[10]

    shp = lambda a: jax.ShapeDtypeStruct(a.shape, F32)
    return pl.pallas_call(
        body, name=name + "_b", grid=(nc,),
        in_specs=[tok, per_tok, per_tok, per_tok, per_tok, mat, mat, row, state, out],
        out_specs=[tok, per_tok, per_tok, per_tok, per_tok, mat, mat, row],
        out_shape=(shp(qkv), shp(b_b), shp(be_b), shp(e_b), shp(kd_b), shp(m1), shp(dec), shp(gl)),
        scratch_shapes=[pltpu.VMEM((GDN_HEADS, HEAD_DIM, HEAD_DIM), F32)],
        compiler_params=pltpu.CompilerParams(dimension_semantics=("arbitrary",), vmem_limit_bytes=VMEM_LIMIT),
    )(qkv, b_b, be_b, e_b, kd_b, m1, dec, gl, sall, do)


@functools.partial(jax.custom_vjp, nondiff_argnums=(0, 1, 2))
def gdn_scan(name, reverse, ncc, qkv, b_b, be_b, e_b, kd_b, m1, dec, gl):
    return _gdn_fwd_call(name, reverse, ncc, qkv, b_b, be_b, e_b, kd_b, m1, dec, gl)[0]


def _gdn_scan_fwd(name, reverse, ncc, *args):
    o, sall = _gdn_fwd_call(name, reverse, ncc, *args)
    return o, (args, sall)


def _gdn_scan_bwd(name, reverse, ncc, res, do):
    args, sall = res
    return _gdn_bwd_call(name, reverse, ncc, *args, sall, do)


gdn_scan.defvjp(_gdn_scan_fwd, _gdn_scan_bwd)


def _rope_tables(n, cl):
    t = np.arange(n)
    inv_freq = (ROPE_THETA ** (-np.arange(0, HEAD_DIM // 2, 2, dtype=np.float32) / (HEAD_DIM // 2))).astype(np.float32)
    ang_r = (t // GRID_W).astype(np.float32)[:, None] * inv_freq
    ang_c = (t % GRID_W).astype(np.float32)[:, None] * inv_freq
    cos = np.concatenate([np.cos(ang_r), np.cos(ang_r), np.cos(ang_c), np.cos(ang_c)], axis=1)
    sin = np.concatenate([-np.sin(ang_r), np.sin(ang_r), -np.sin(ang_c), np.sin(ang_c)], axis=1)
    cos_all = np.concatenate([np.ones((cl, HEAD_DIM), np.float32), cos], axis=0)
    sin_all = np.concatenate([np.zeros((cl, HEAD_DIM), np.float32), sin], axis=0)
    j = np.arange(HEAD_DIM)
    src = np.where((j % 64) < 32, j + 32, j - 32)
    perm = np.zeros((HEAD_DIM, HEAD_DIM), np.float32)
    perm[src, j] = 1.0
    return (jnp.asarray(cos.astype(np.float32)), jnp.asarray(sin.astype(np.float32)),
            jnp.asarray(cos_all), jnp.asarray(sin_all), jnp.asarray(perm))


def _shift_rows(a, cl):
    z = jnp.zeros((1, a.shape[1]), a.dtype)
    parts = [a[:cl], a[cl:]] if cl else [a]
    prev = jnp.concatenate([jnp.concatenate([z, p[:-1]], axis=0) for p in parts], axis=0)
    nxt = jnp.concatenate([jnp.concatenate([p[1:], z], axis=0) for p in parts], axis=0)
    return prev, nxt


def _gdn_factors(log_a, beta, reverse):
    t = log_a.shape[0]
    nc = t // _C
    la = log_a.reshape(nc, _C, GDN_HEADS).transpose(2, 0, 1)
    be = beta.reshape(nc, _C, GDN_HEADS).transpose(2, 0, 1)
    gam = lax.cumsum(la, axis=2, reverse=reverse)
    idx = jnp.arange(_C)
    incl = (idx[:, None] <= idx[None, :]) if reverse else (idx[:, None] >= idx[None, :])
    strict = (idx[:, None] < idx[None, :]) if reverse else (idx[:, None] > idx[None, :])
    dec = jnp.exp(jnp.where(incl, gam[..., :, None] - gam[..., None, :], -jnp.inf))
    m1 = jnp.where(strict, be[..., :, None] * dec, 0.0)
    e = jnp.exp(gam)
    g_last = gam[..., :1] if reverse else gam[..., -1:]
    lanes = lambda a: jnp.broadcast_to(a.reshape(GDN_HEADS, t, 1), (GDN_HEADS, t, HEAD_DIM))
    gl = jnp.broadcast_to(jnp.exp(g_last)[..., None], (GDN_HEADS, nc, 1, HEAD_DIM))
    return lanes(be), lanes(be * e), lanes(e), lanes(jnp.exp(g_last - gam)), m1, dec, gl


def local_loss(x, wz, wb, ws, c, ctx, target):
    n, cl = x.shape[0], ctx.shape[0]
    cos_q, sin_q, cos_k, sin_k, perm = _rope_tables(n, cl)

    sc_in = jnp.concatenate([jax.nn.silu(c), jax.nn.silu(ws["c_ctx"])[None, :], jnp.zeros((14, D_MODEL), F32)], axis=0)
    mod = pmm(sc_in, wb["w_mod"], wz["w_mod"], "mm_mod") + ws["b_mod"]
    sh1, sc1, g1, sh2, sc2, g2 = [mod[0:1, i * D_MODEL:(i + 1) * D_MODEL] for i in range(6)]
    csh1, csc1 = mod[1:2, 0:D_MODEL], mod[1:2, D_MODEL:2 * D_MODEL]

    def norm_mod(a, sh, sc):
        return (_rms(a) * (1.0 + sc) + sh,)

    (hx,) = rowop(norm_mod, "normmod_x", (x,), (sh1, sc1))
    (hc,) = rowop(norm_mod, "normmod_c", (ctx,), (csh1, csc1))
    h_all = jnp.concatenate([hc, hx], axis=0)
    p_main = pmm_t(h_all, wb["w_in_main"], wz["w_in_main"], "mm_in")
    p_small = pmm_t(h_all, wb["w_in_small"], wz["w_in_small"], "mm_ins")
    ak, av, qkv, aq, z, gate = jnp.split(p_main, [KV_W, 2 * KV_W, SMALL_AT, SMALL_AT + Q_W, SMALL_AT + Q_W + GDN_W],
                                         axis=1)
    db, da = p_small[:, :2 * GDN_HEADS], p_small[:, 2 * GDN_HEADS:4 * GDN_HEADS]

    def qk_prep(nh):
        def fn(a, w, cos, sin, pm):
            outs = []
            for ah in _heads(a, nh):
                y = _rms(ah) * w
                outs.append(y * cos + _hdot(y, pm) * sin)
            return (jnp.concatenate(outs, axis=1),)
        return fn

    (q_x,) = rowop(qk_prep(ATTN_HEADS), "q_prep", (aq[cl:],), (ws["q_norm_w"],), (cos_q, sin_q), (perm,))
    (k_all,) = rowop(qk_prep(ATTN_KV_HEADS), "k_prep", (ak,), (ws["k_norm_w"],), (cos_k, sin_k), (perm,))
    attn_x = attention(q_x, k_all, av)

    qkv_prev, qkv_next = _shift_rows(qkv, cl)
    cw = ws["conv_qkv_w"]

    def gdn_prep(a, ap, an, w0, w1, w2):
        s = jax.nn.silu(ap * w0 + a * w1 + an * w2)
        outs = []
        for i, sh in enumerate(_heads(s, 3 * GDN_HEADS)):
            if i < 2 * GDN_HEADS:
                sh = sh * lax.rsqrt(jnp.sum(sh * sh, axis=-1, keepdims=True) + NORM_EPS)
                if i < GDN_HEADS:
                    sh = sh * (HEAD_DIM ** -0.5)
            outs.append(sh)
        return (jnp.concatenate(outs, axis=1),)

    (qkvn,) = rowop(gdn_prep, "gdn_prep", (qkv, qkv_prev, qkv_next), (cw[0:1], cw[1:2], cw[2:3]), tr=128)
    beta = jax.nn.sigmoid(db).reshape(-1, 2, GDN_HEADS)
    log_a = -jnp.exp(ws["a_log"])[None] * jax.nn.softplus(da.reshape(-1, 2, GDN_HEADS) + ws["dt_bias"][None])
    o_fwd = gdn_scan("gdn_d0", False, cl // _C, qkvn, *_gdn_factors(log_a[:, 0], beta[:, 0], False))
    o_bwd = gdn_scan("gdn_d1", True, cl // _C, qkvn, *_gdn_factors(log_a[:, 1], beta[:, 1], True))
    o_x = o_fwd[cl:] + o_bwd[cl:]

    def gdn_out(o, zz, w):
        outs = [_rms(oh) * w * jax.nn.silu(zh) for oh, zh in zip(_heads(o, GDN_HEADS), _heads(zz, GDN_HEADS))]
        return (jnp.concatenate(outs, axis=1),)

    (gdn_x,) = rowop(gdn_out, "gdn_out", (o_x, z[cl:]), (ws["gdn_norm_w"],))

    pa = pmm(attn_x, wb["w_pa"], wz["w_pa"], "mm_pa")
    pd = pmm(gdn_x, wb["w_pd"], wz["w_pd"], "mm_pd")

    def merge(a, d, g):
        return (jax.nn.sigmoid(g[:, :D_MODEL]) * a + jax.nn.sigmoid(g[:, D_MODEL:]) * d,)

    (y,) = rowop(merge, "merge", (pa, pd, gate[cl:]))
    mo = pmm(y, wb["w_out"], wz["w_out"], "mm_out")

    def res_norm_mod(xx, m, g, sh, sc):
        x1 = xx + g * m
        return x1, _rms(x1) * (1.0 + sc) + sh

    x1, h2 = rowop(res_norm_mod, "res1", (x, mo), (g1, sh2, sc2))
    up = pmm(h2, wb["w_up"], wz["w_up"], "mm_up")
    up_prev, up_next = _shift_rows(up, 0)
    fw = ws["ffn_conv_w"]

    def ffn_act(a, ap, an, w0, w1, w2, b):
        u = ap * w0 + a * w1 + an * w2 + b
        return (jax.nn.silu(u[:, :D_FF]) * u[:, D_FF:],)

    (act,) = rowop(ffn_act, "ffn_act", (up, up_prev, up_next), (fw[0:1], fw[1:2], fw[2:3], ws["ffn_conv_b"]), tr=128)
    dn = pmm(act, wb["w_down"], wz["w_down"], "mm_down")

    def head(xx, m, g, w, tgt):
        yy = _rms(xx + g * m) * w
        err = (yy - tgt) ** 2
        return (jnp.broadcast_to(0.5 * jnp.mean(err, axis=-1, keepdims=True), (xx.shape[0], HEAD_DIM)),)

    (row_loss,) = rowop(head, "head", (x1, dn), (g2, ws["final_norm_w"][None, :]), (target,))
    return jnp.sum(row_loss[:, 0])


_HBM = pl.BlockSpec(memory_space=pltpu.HBM)


def _chip_peers():
    x, y = lax.axis_index("x"), lax.axis_index("y")
    return [(1 - x, y), (x, 1 - y), (1 - x, 1 - y)]


def all_gather_chips(pack):
    def body(p_ref, out_ref, send_sems, recv_sems, local_sem):
        x, y, c = lax.axis_index("x"), lax.axis_index("y"), lax.axis_index("c")
        mine = pltpu.make_async_copy(p_ref, out_ref.at[2 * x + y], local_sem)
        mine.start()
        sends = [pltpu.make_async_remote_copy(src_ref=p_ref, dst_ref=out_ref.at[2 * x + y], send_sem=send_sems.at[k],
                                              recv_sem=recv_sems.at[k], device_id=(px, py, c), device_id_type=MESH)
                 for k, (px, py) in enumerate(_chip_peers())]
        for cp in sends:
            cp.start()
        for k, (px, py) in enumerate(_chip_peers()):
            pltpu.make_async_remote_copy(src_ref=p_ref, dst_ref=out_ref.at[2 * px + py], send_sem=send_sems.at[k],
                                         recv_sem=recv_sems.at[k], device_id=(px, py, c),
                                         device_id_type=MESH).wait_recv()
        for cp in sends:
            cp.wait_send()
        mine.wait()

    return pl.pallas_call(
        body, name="ag_weights", in_specs=[_HBM], out_specs=_HBM,
        out_shape=jax.ShapeDtypeStruct((N_CHIPS,) + pack.shape, pack.dtype),
        scratch_shapes=[pltpu.SemaphoreType.DMA((3,)), pltpu.SemaphoreType.DMA((3,)), pltpu.SemaphoreType.DMA],
    )(pack)


def scatter_to_chips(blocks):
    def body(b_ref, out_ref, send_sems, recv_sems):
        c = lax.axis_index("c")
        sends = [pltpu.make_async_remote_copy(src_ref=b_ref.at[2 * px + py], dst_ref=out_ref.at[k],
                                              send_sem=send_sems.at[k], recv_sem=recv_sems.at[k],
                                              device_id=(px, py, c), device_id_type=MESH)
                 for k, (px, py) in enumerate(_chip_peers())]
        for cp in sends:
            cp.start()
        for cp in sends:
            cp.wait_recv()
        for cp in sends:
            cp.wait_send()

    return pl.pallas_call(
        body, name="rs_grads", in_specs=[_HBM], out_specs=_HBM,
        out_shape=jax.ShapeDtypeStruct((3,) + blocks.shape[1:], blocks.dtype),
        scratch_shapes=[pltpu.SemaphoreType.DMA((3,)), pltpu.SemaphoreType.DMA((3,))],
    )(blocks)


def sibling_swap(a):
    def body(a_ref, out_ref, send_sem, recv_sem):
        x, y, c = lax.axis_index("x"), lax.axis_index("y"), lax.axis_index("c")
        cp = pltpu.make_async_remote_copy(src_ref=a_ref, dst_ref=out_ref, send_sem=send_sem, recv_sem=recv_sem,
                                          device_id=(x, y, 1 - c), device_id_type=MESH)
        cp.start()
        cp.wait()

    return pl.pallas_call(
        body, name="sib_swap", in_specs=[_HBM], out_specs=_HBM,
        out_shape=jax.ShapeDtypeStruct(a.shape, a.dtype),
        scratch_shapes=[pltpu.SemaphoreType.DMA, pltpu.SemaphoreType.DMA],
    )(a)


def all_reduce_small(v):
    def body(v_ref, tot_ref, gath_ref, send_sems, recv_sems):
        x, y, c = lax.axis_index("x"), lax.axis_index("y"), lax.axis_index("c")
        me = 4 * x + 2 * y + c
        gath_ref[me] = v_ref[...]

        def peer(k):
            m = k + 1
            return (x ^ (m >> 2 & 1), y ^ (m >> 1 & 1), c ^ (m & 1))

        sends = [pltpu.make_async_remote_copy(src_ref=v_ref, dst_ref=gath_ref.at[me], send_sem=send_sems.at[k],
                                              recv_sem=recv_sems.at[k], device_id=peer(k), device_id_type=MESH)
                 for k in range(N_DEV - 1)]
        for cp in sends:
            cp.start()
        for k in range(N_DEV - 1):
            px, py, pc = peer(k)
            pltpu.make_async_remote_copy(src_ref=v_ref, dst_ref=gath_ref.at[4 * px + 2 * py + pc],
                                         send_sem=send_sems.at[k], recv_sem=recv_sems.at[k], device_id=peer(k),
                                         device_id_type=MESH).wait_recv()
        for cp in sends:
            cp.wait_send()
        acc = gath_ref[0]
        for d in range(1, N_DEV):
            acc = acc + gath_ref[d]
        tot_ref[...] = acc

    vm = pl.BlockSpec(memory_space=pltpu.VMEM)
    return pl.pallas_call(
        body, name="ar_small", in_specs=[vm], out_specs=[vm, vm],
        out_shape=(jax.ShapeDtypeStruct(v.shape, v.dtype), jax.ShapeDtypeStruct((N_DEV,) + v.shape, v.dtype)),
        scratch_shapes=[pltpu.SemaphoreType.DMA((N_DEV - 1,)), pltpu.SemaphoreType.DMA((N_DEV - 1,))],
    )(v)[0]


def _elementwise(fn, name, ins, n_out, out_dtype=F32):
    r, cdim = ins[0].shape
    tr = _pick(r, tuple(p for p in (488, 256, 128, 104, 64, 32, 16, 8) if p * cdim * 4 <= 2 * 1024 * 1024))
    spec = pl.BlockSpec((tr, cdim), lambda i: (i, 0))

    def body(*refs):
        res = fn(*[ref[...] for ref in refs[:len(ins)]])
        for o_ref, v in zip(refs[len(ins):], res):
            o_ref[...] = v

    return pl.pallas_call(
        body, name=name, grid=(r // tr,), in_specs=[spec] * len(ins), out_specs=[spec] * n_out,
        out_shape=tuple(jax.ShapeDtypeStruct((r, cdim), out_dtype) for _ in range(n_out)),
        compiler_params=pltpu.CompilerParams(dimension_semantics=("parallel",), vmem_limit_bytes=VMEM_LIMIT),
    )(*ins)


def _adamw(w, ga, gb, m, v, name):
    shape = w.shape
    to2 = lambda a: a.reshape(-1, shape[-1])

    def fn(w_, ga_, gb_, m_, v_):
        g = ga_ + gb_
        m_new = ADAM_B1 * m_ + (1.0 - ADAM_B1) * g
        v_new = ADAM_B2 * v_ + (1.0 - ADAM_B2) * (g * g)
        m_hat = m_new / (1.0 - ADAM_B1 ** ADAM_STEP)
        v_hat = v_new / (1.0 - ADAM_B2 ** ADAM_STEP)
        delta = -ADAM_LR * (m_hat / (jnp.sqrt(v_hat) + ADAM_EPS) + ADAM_WD * w_)
        return g, delta, m_new, v_new

    outs = _elementwise(fn, name, [to2(a) for a in (w, ga, gb, m, v)], 4)
    return tuple(o.reshape(shape) for o in outs)


_BIG = ("w_mod", "w_in", "w_pa", "w_pd", "w_out", "w_up", "w_down")
_COL_SHARDED = ("w_mod", "w_up")
_FULL_SHAPE = {"w_mod": (D_MODEL, MOD_W), "w_in": (IN_COLS, D_MODEL), "w_pa": (Q_W, D_MODEL), "w_pd": (GDN_W, D_MODEL),
               "w_out": (D_MODEL, D_MODEL), "w_up": (D_MODEL, 2 * D_FF), "w_down": (D_FF, D_MODEL)}


def _shard_shape(name):
    r, cdim = _FULL_SHAPE[name]
    return (r, cdim // N_CHIPS) if name in _COL_SHARDED else (r // N_CHIPS, cdim)


_BIG_ELEMS = sum(int(np.prod(_shard_shape(nm))) for nm in _BIG)
_BIG_ROWS = _BIG_ELEMS // D_MODEL
_CONV_ELEMS = 2 * (3 * CONV_W // N_CHIPS + 3 * 2 * D_FF // N_CHIPS)
_CONV_ROWS = 24


def _blocks_of_full(name, full):
    r, cdim = _FULL_SHAPE[name]
    if name in _COL_SHARDED:
        return full.reshape(r, N_CHIPS, cdim // N_CHIPS).transpose(1, 0, 2).reshape(N_CHIPS, -1)
    return full.reshape(N_CHIPS, -1)


def _full_of_blocks(name, blocks):
    r, cdim = _FULL_SHAPE[name]
    if name in _COL_SHARDED:
        return blocks.reshape(N_CHIPS, r, cdim // N_CHIPS).transpose(1, 0, 2).reshape(r, cdim)
    return blocks.reshape(r, cdim)


def _split_big(flat):
    out, off = {}, 0
    for nm in _BIG:
        size = int(np.prod(_shard_shape(nm)))
        out[nm] = flat[..., off:off + size]
        off += size
    return out


def _w_in_regroup(w_in_t):
    main = jnp.concatenate([w_in_t[:SMALL_AT], w_in_t[SMALL_AT + 4 * GDN_HEADS:]], axis=0)
    small = jnp.pad(w_in_t[SMALL_AT:SMALL_AT + 4 * GDN_HEADS], ((0, HEAD_DIM - 4 * GDN_HEADS), (0, 0)))
    return main, small


def _w_in_ungroup(main, small):
    return jnp.concatenate([main[:SMALL_AT], small[:4 * GDN_HEADS], main[SMALL_AT:]], axis=0)


_SMALL = ("c_ctx", "b_mod", "q_norm_w", "k_norm_w", "conv_qkv_w", "a_log", "dt_bias", "gdn_norm_w", "ffn_conv_w",
          "ffn_conv_b", "final_norm_w")


def _pack_small(tree, rows):
    flat = jnp.concatenate([tree[nm].reshape(-1) for nm in _SMALL])
    return jnp.pad(flat, (0, rows * 128 - flat.shape[0])).reshape(rows, 128)


def _unpack_small(packed, like):
    flat, out, off = packed.reshape(-1), {}, 0
    for nm in _SMALL:
        size = int(np.prod(like[nm].shape))
        out[nm] = flat[off:off + size].reshape(like[nm].shape)
        off += size
    return out


def kernel(x, c, ctx, c_ctx, w_mod, b_mod, w_in, q_norm_w, k_norm_w, conv_qkv_w, a_log, dt_bias, gdn_norm_w, w_pa, w_pd, w_out, w_up, ffn_conv_w, ffn_conv_b, w_down, final_norm_w, loss_target, m_c_ctx, m_w_mod, m_b_mod, m_w_in, m_q_norm_w, m_k_norm_w, m_conv_qkv_w, m_a_log, m_dt_bias, m_gdn_norm_w, m_w_pa, m_w_pd, m_w_out, m_w_up, m_ffn_conv_w, m_ffn_conv_b, m_w_down, m_final_norm_w, v_c_ctx, v_w_mod, v_b_mod, v_w_in, v_q_norm_w, v_k_norm_w, v_conv_qkv_w, v_a_log, v_dt_bias, v_gdn_norm_w, v_w_pa, v_w_pd, v_w_out, v_w_up, v_ffn_conv_w, v_ffn_conv_b, v_w_down, v_final_norm_w):
    names = ("c_ctx", "w_mod", "b_mod", "w_in", "q_norm_w", "k_norm_w", "conv_qkv_w", "a_log", "dt_bias", "gdn_norm_w",
             "w_pa", "w_pd", "w_out", "w_up", "ffn_conv_w", "ffn_conv_b", "w_down", "final_norm_w")
    w_sh = dict(c_ctx=c_ctx, w_mod=w_mod, b_mod=b_mod, w_in=w_in, q_norm_w=q_norm_w, k_norm_w=k_norm_w,
                conv_qkv_w=conv_qkv_w, a_log=a_log, dt_bias=dt_bias, gdn_norm_w=gdn_norm_w, w_pa=w_pa, w_pd=w_pd,
                w_out=w_out, w_up=w_up, ffn_conv_w=ffn_conv_w, ffn_conv_b=ffn_conv_b, w_down=w_down,
                final_norm_w=final_norm_w)
    m_sh = dict(c_ctx=m_c_ctx, w_mod=m_w_mod, b_mod=m_b_mod, w_in=m_w_in, q_norm_w=m_q_norm_w, k_norm_w=m_k_norm_w,
                conv_qkv_w=m_conv_qkv_w, a_log=m_a_log, dt_bias=m_dt_bias, gdn_norm_w=m_gdn_norm_w, w_pa=m_w_pa,
                w_pd=m_w_pd, w_out=m_w_out, w_up=m_w_up, ffn_conv_w=m_ffn_conv_w, ffn_conv_b=m_ffn_conv_b,
                w_down=m_w_down, final_norm_w=m_final_norm_w)
    v_sh = dict(c_ctx=v_c_ctx, w_mod=v_w_mod, b_mod=v_b_mod, w_in=v_w_in, q_norm_w=v_q_norm_w, k_norm_w=v_k_norm_w,
                conv_qkv_w=v_conv_qkv_w, a_log=v_a_log, dt_bias=v_dt_bias, gdn_norm_w=v_gdn_norm_w, w_pa=v_w_pa,
                w_pd=v_w_pd, w_out=v_w_out, w_up=v_w_up, ffn_conv_w=v_ffn_conv_w, ffn_conv_b=v_ffn_conv_b,
                w_down=v_w_down, final_norm_w=v_final_norm_w)
    chip = 2 * lax.axis_index("x") + lax.axis_index("y")

    conv_bits = jnp.concatenate([lax.bitcast_convert_type(w_sh[nm][0], BF16).reshape(-1)
                                 for nm in ("conv_qkv_w", "ffn_conv_w")])
    as_sent = lambda nm: w_sh[nm][0].astype(BF16).T if nm == "w_in" else w_sh[nm][0].astype(BF16)
    pack = jnp.concatenate([as_sent(nm).reshape(-1) for nm in _BIG]
                           + [conv_bits, jnp.zeros((_CONV_ROWS * D_MODEL - _CONV_ELEMS,), BF16)])
    gathered = all_gather_chips(pack.reshape(_BIG_ROWS + _CONV_ROWS, D_MODEL)).reshape(N_CHIPS, -1)
    big_blocks = _split_big(gathered[:, :_BIG_ELEMS])
    wb = {nm: _full_of_blocks(nm, big_blocks[nm]) for nm in _BIG}
    wb["w_in_main"], wb["w_in_small"] = _w_in_regroup(wb.pop("w_in"))
    conv_all = gathered[:, _BIG_ELEMS:_BIG_ELEMS + _CONV_ELEMS]
    n_cq = 2 * 3 * CONV_W // N_CHIPS
    unbits = lambda a, w: lax.bitcast_convert_type(a.reshape(N_CHIPS, 3, w // N_CHIPS, 2), F32).transpose(1, 0, 2).reshape(3, w)
    ws = dict(c_ctx=c_ctx, b_mod=b_mod, q_norm_w=q_norm_w, k_norm_w=k_norm_w, a_log=a_log[0], dt_bias=dt_bias[0],
              gdn_norm_w=gdn_norm_w, ffn_conv_b=ffn_conv_b, final_norm_w=final_norm_w,
              conv_qkv_w=unbits(conv_all[:, :n_cq], CONV_W), ffn_conv_w=unbits(conv_all[:, n_cq:], 2 * D_FF))
    wz = {nm: jnp.zeros(a.shape, F32) for nm, a in wb.items()}

    loss_local, (gx, gz, gs) = jax.value_and_grad(local_loss, argnums=(0, 1, 3))(
        x[0], wz, wb, ws, c, ctx[0], loss_target[0])
    loss = lax.psum(loss_local, ("x", "y", "c"))

    gz["w_in"] = _w_in_ungroup(gz.pop("w_in_main"), gz.pop("w_in_small"))
    g_blocks = jnp.concatenate([_blocks_of_full(nm, gz[nm]) for nm in _BIG], axis=1)
    g_own = lax.dynamic_index_in_dim(g_blocks, chip, axis=0, keepdims=False).reshape(_BIG_ROWS, D_MODEL)
    got = scatter_to_chips(g_blocks.astype(BF16).reshape(N_CHIPS, _BIG_ROWS, D_MODEL))
    (partial,) = _elementwise(lambda a, r0, r1, r2: (a + r0.astype(F32) + r1.astype(F32) + r2.astype(F32),),
                              "rs_sum", [g_own, got[0], got[1], got[2]], 1)
    other = sibling_swap(partial)
    part_a, part_b = _split_big(partial.reshape(-1)), _split_big(other.reshape(-1))

    gs["a_log"], gs["dt_bias"] = gs["a_log"][None], gs["dt_bias"][None]
    like = {nm: gs[nm] for nm in _SMALL}
    small_rows = -(-sum(int(np.prod(like[nm].shape)) for nm in _SMALL) // 1024) * 8
    g_small = _unpack_small(all_reduce_small(_pack_small(gs, small_rows)), like)
    for nm, width in (("conv_qkv_w", CONV_W), ("ffn_conv_w", 2 * D_FF)):
        g_small[nm] = lax.dynamic_slice_in_dim(g_small[nm], chip * (width // N_CHIPS), width // N_CHIPS, axis=1)[None]

    grads, deltas, new_m, new_v = {}, {}, {}, {}
    for nm in _BIG:
        shp = _shard_shape(nm)
        as_kept = lambda part: part.reshape(shp).T if nm == "w_in" else part.reshape(shp)
        grads[nm], deltas[nm], new_m[nm], new_v[nm] = (
            o[None] for o in _adamw(w_sh[nm][0], as_kept(part_a[nm]), as_kept(part_b[nm]), m_sh[nm][0],
                                    v_sh[nm][0], "adamw_" + nm))
    shard_like = {nm: w_sh[nm] for nm in _SMALL}
    rows_l = -(-sum(int(np.prod(shard_like[nm].shape)) for nm in _SMALL) // 1024) * 8
    g_l = _pack_small({nm: g_small[nm].reshape(w_sh[nm].shape) for nm in _SMALL}, rows_l)
    outs = _adamw(_pack_small(w_sh, rows_l), g_l, jnp.zeros_like(g_l), _pack_small(m_sh, rows_l),
                  _pack_small(v_sh, rows_l), "adamw_small")
    for tree, packed in zip((grads, deltas, new_m, new_v), outs):
        tree.update(_unpack_small(packed, shard_like))

    return (loss, gx[None], *[grads[nm] for nm in names], *[deltas[nm] for nm in names],
            *[new_m[nm] for nm in names], *[new_v[nm] for nm in names])
```

```python
import functools
import math

import jax
import jax.numpy as jnp
import numpy as np
from jax import lax
from jax.experimental import pallas as pl
from jax.experimental.pallas import tpu as pltpu

F32 = jnp.float32
BF16 = jnp.bfloat16
HIGHEST = lax.Precision.HIGHEST
MESH = pl.DeviceIdType.MESH

D_MODEL = 1024
GRID_W = 64
ATTN_HEADS = 8
ATTN_KV_HEADS = 2
ATTN_GROUP = ATTN_HEADS // ATTN_KV_HEADS
HEAD_DIM = 128
ROPE_THETA = 10000.0
GDN_HEADS = 8
GDN_CHUNK = 64
D_FF = 2816
NORM_EPS = 1e-6
KV_W = ATTN_KV_HEADS * HEAD_DIM
Q_W = ATTN_HEADS * HEAD_DIM
GDN_W = GDN_HEADS * HEAD_DIM
CONV_W = 3 * GDN_W
MOD_W = 6 * D_MODEL
IN_COLS = 2 * KV_W + CONV_W + 4 * GDN_HEADS + Q_W + GDN_W + 2 * D_MODEL
IN_MAIN = IN_COLS - 4 * GDN_HEADS
SMALL_AT = 2 * KV_W + CONV_W
N_CHIPS = 4
N_DEV = 8

ADAM_LR = 0.001
ADAM_B1 = 0.9
ADAM_B2 = 0.999
ADAM_EPS = 1e-08
ADAM_WD = 0.01
ADAM_STEP = 10

VMEM_LIMIT = 48 * 1024 * 1024


def _pick(dim, prefs):
    for p in prefs:
        if p <= dim and dim % p == 0:
            return p
    return dim


_DIMS = {
    "nn": (((1,), (0,)), ((), ())),
    "nt": (((1,), (1,)), ((), ())),
    "tn": (((0,), (0,)), ((), ())),
}


def _matmul(a, b, mode, name):
    if mode == "nn":
        (m, k), (_, n) = a.shape, b.shape
    elif mode == "nt":
        (m, k), (n, _) = a.shape, b.shape
    else:
        (k, m), (_, n) = a.shape, b.shape
    tm = _pick(m, (512, 384, 256, 128))
    tn = _pick(n, (1536, 1408, 1024, 768, 512, 256, 128))
    tk = _pick(k, (1024, 1408, 768, 512, 256, 128))
    nk = k // tk
    if mode == "tn":
        a_spec = pl.BlockSpec((tk, tm), lambda i, j, l: (l, i))
    else:
        a_spec = pl.BlockSpec((tm, tk), lambda i, j, l: (i, l))
    if mode == "nt":
        b_spec = pl.BlockSpec((tn, tk), lambda i, j, l: (j, l))
    else:
        b_spec = pl.BlockSpec((tk, tn), lambda i, j, l: (l, j))
    dims = _DIMS[mode]

    def body(a_ref, b_ref, o_ref, acc_ref):
        l = pl.program_id(2)

        @pl.when(l == 0)
        def _():
            acc_ref[...] = jnp.zeros_like(acc_ref)

        acc_ref[...] += lax.dot_general(a_ref[...].astype(BF16), b_ref[...].astype(BF16), dims,
                                        preferred_element_type=F32)

        @pl.when(l == nk - 1)
        def _():
            o_ref[...] = acc_ref[...]

    return pl.pallas_call(
        body,
        name=name,
        grid=(m // tm, n // tn, nk),
        in_specs=[a_spec, b_spec],
        out_specs=pl.BlockSpec((tm, tn), lambda i, j, l: (i, j)),
        out_shape=jax.ShapeDtypeStruct((m, n), F32),
        scratch_shapes=[pltpu.VMEM((tm, tn), F32)],
        compiler_params=pltpu.CompilerParams(dimension_semantics=("parallel", "parallel", "arbitrary"),
                                             vmem_limit_bytes=VMEM_LIMIT),
    )(a, b)


@functools.partial(jax.custom_vjp, nondiff_argnums=(3,))
def pmm(a, w, wz, name):
    del wz
    return _matmul(a, w, "nn", name + "_f")


def _pmm_fwd(a, w, wz, name):
    del wz
    return _matmul(a, w, "nn", name + "_f"), (a, w)


def _pmm_bwd(name, res, g):
    a, w = res
    da = _matmul(g, w, "nt", name + "_da")
    if a.shape[0] < 128:
        pad = 128 - a.shape[0]
        at = jnp.pad(a.T, ((0, 0), (0, pad)))
        gp = jnp.pad(g, ((0, pad), (0, 0)))
        dw = _matmul(at, gp, "nn", name + "_dw")
    else:
        dw = _matmul(a, g, "tn", name + "_dw")
    return da, jnp.zeros_like(w), dw


pmm.defvjp(_pmm_fwd, _pmm_bwd)


@functools.partial(jax.custom_vjp, nondiff_argnums=(3,))
def pmm_t(a, wt, wtz, name):
    del wtz
    return _matmul(a, wt, "nt", name + "_f")


def _pmm_t_fwd(a, wt, wtz, name):
    del wtz
    return _matmul(a, wt, "nt", name + "_f"), (a, wt)


def _pmm_t_bwd(name, res, g):
    a, wt = res
    return _matmul(g, wt, "nn", name + "_da"), jnp.zeros_like(wt), _matmul(g, a, "tn", name + "_dw")


pmm_t.defvjp(_pmm_t_fwd, _pmm_t_bwd)


def rowop(fn, name, rows, bcs=(), crows=(), cbcs=(), tr=256):
    rows, bcs, crows, cbcs = tuple(rows), tuple(bcs), tuple(crows), tuple(cbcs)
    n_rows = rows[0].shape[0]
    tr = _pick(n_rows, (tr, 128, 64, 32, 16, 8))
    nr, nb, ncr, ncb = len(rows), len(bcs), len(crows), len(cbcs)
    n_in = nr + nb + ncr + ncb
    grid = (n_rows // tr,)

    def blk(arr):
        return jax.ShapeDtypeStruct((tr, arr.shape[1]), arr.dtype)

    def row_spec(arr):
        return pl.BlockSpec((tr, arr.shape[1]), lambda i: (i, 0))

    def bc_spec(arr):
        return pl.BlockSpec(arr.shape, lambda i: (0, 0))

    out_blk = jax.eval_shape(fn, *[blk(r) for r in rows], *bcs, *[blk(r) for r in crows], *cbcs)
    n_out = len(out_blk)
    out_shape = tuple(jax.ShapeDtypeStruct((n_rows, o.shape[1]), o.dtype) for o in out_blk)
    in_specs = ([row_spec(r) for r in rows] + [bc_spec(b) for b in bcs]
                + [row_spec(r) for r in crows] + [bc_spec(b) for b in cbcs])

    def order(vals):
        return vals

    def fwd_call(args):
        def body(*refs):
            vals = [r[...] for r in refs[:n_in]]
            res = fn(*order(vals))
            for o_ref, r in zip(refs[n_in:], res):
                o_ref[...] = r

        return pl.pallas_call(
            body, name=name + "_f", grid=grid, in_specs=in_specs,
            out_specs=[row_spec(o) for o in out_shape], out_shape=out_shape,
            compiler_params=pltpu.CompilerParams(dimension_semantics=("parallel",), vmem_limit_bytes=VMEM_LIMIT),
        )(*args)

    def bwd_call(args, cts):
        def body(*refs):
            vals = [r[...] for r in refs[:n_in]]
            ct_refs = refs[n_in:n_in + n_out]
            d_rows = refs[n_in + n_out:n_in + n_out + nr]
            d_bcs = refs[n_in + n_out + nr:]
            consts = vals[nr + nb:]
            _, vjp = jax.vjp(lambda *p: fn(*p, *consts), *vals[:nr + nb])
            grads = vjp(tuple(c[...] for c in ct_refs))
            for ref, g in zip(d_rows, grads[:nr]):
                ref[...] = g

            @pl.when(pl.program_id(0) == 0)
            def _():
                for ref in d_bcs:
                    ref[...] = jnp.zeros_like(ref)

            for ref, g in zip(d_bcs, grads[nr:]):
                ref[...] += g

        d_shape = tuple(jax.ShapeDtypeStruct(r.shape, r.dtype) for r in rows + bcs)
        return pl.pallas_call(
            body, name=name + "_b", grid=grid,
            in_specs=in_specs + [row_spec(o) for o in out_shape],
            out_specs=[row_spec(r) for r in rows] + [bc_spec(b) for b in bcs], out_shape=d_shape,
            compiler_params=pltpu.CompilerParams(dimension_semantics=("arbitrary",), vmem_limit_bytes=VMEM_LIMIT),
        )(*args, *cts)

    @jax.custom_vjp
    def op(diff, const):
        return fwd_call(diff + const)

    def op_fwd(diff, const):
        return fwd_call(diff + const), (diff, const)

    def op_bwd(res, cts):
        diff, const = res
        grads = bwd_call(diff + const, tuple(cts))
        return tuple(grads), tuple(jnp.zeros_like(c) for c in const)

    op.defvjp(op_fwd, op_bwd)
    return op(rows + bcs, crows + cbcs)


def _rms(x):
    return x * lax.rsqrt(jnp.mean(x * x, axis=-1, keepdims=True) + NORM_EPS)


def _heads(x, n):
    return [x[:, h * HEAD_DIM:(h + 1) * HEAD_DIM] for h in range(n)]


_NT = (((1,), (1,)), ((), ()))
_TN = (((0,), (0,)), ((), ()))
_TQ = 256


def _attn_probs(q, k):
    s = lax.dot_general(q, k, _NT, preferred_element_type=F32) * (HEAD_DIM ** -0.5)
    p = jnp.exp(s - jnp.max(s, axis=-1, keepdims=True))
    return p / jnp.sum(p, axis=-1, keepdims=True)


def _attn_fwd_call(q, k, v):
    n, t = q.shape[0], k.shape[0]
    tq = _pick(n, (_TQ, 128))

    def body(q_ref, k_ref, v_ref, o_ref):
        p = _attn_probs(q_ref[...].astype(BF16), k_ref[...].astype(BF16))
        o_ref[...] = jnp.dot(p.astype(BF16), v_ref[...].astype(BF16), preferred_element_type=F32)

    return pl.pallas_call(
        body, name="attn_f", grid=(ATTN_HEADS, n // tq),
        in_specs=[pl.BlockSpec((tq, HEAD_DIM), lambda h, i: (i, h)),
                  pl.BlockSpec((t, HEAD_DIM), lambda h, i: (0, h // ATTN_GROUP)),
                  pl.BlockSpec((t, HEAD_DIM), lambda h, i: (0, h // ATTN_GROUP))],
        out_specs=pl.BlockSpec((tq, HEAD_DIM), lambda h, i: (i, h)),
        out_shape=jax.ShapeDtypeStruct(q.shape, F32),
        compiler_params=pltpu.CompilerParams(dimension_semantics=("parallel", "parallel"),
                                             vmem_limit_bytes=VMEM_LIMIT),
    )(q, k, v)


def _attn_bwd_call(q, k, v, do):
    n, t = q.shape[0], k.shape[0]
    tq = _pick(n, (_TQ, 128))

    def body(q_ref, k_ref, v_ref, do_ref, dq_ref, dk_ref, dv_ref):
        @pl.when((pl.program_id(1) == 0) & (pl.program_id(2) == 0))
        def _():
            dk_ref[...] = jnp.zeros_like(dk_ref)
            dv_ref[...] = jnp.zeros_like(dv_ref)

        qb, kb, vb, dob = (r[...].astype(BF16) for r in (q_ref, k_ref, v_ref, do_ref))
        p = _attn_probs(qb, kb)
        dp = lax.dot_general(dob, vb, _NT, preferred_element_type=F32)
        ds = p * (dp - jnp.sum(p * dp, axis=-1, keepdims=True)) * (HEAD_DIM ** -0.5)
        dsb = ds.astype(BF16)
        dq_ref[...] = jnp.dot(dsb, kb, preferred_element_type=F32)
        dk_ref[...] += lax.dot_general(dsb, qb, _TN, preferred_element_type=F32)
        dv_ref[...] += lax.dot_general(p.astype(BF16), dob, _TN, preferred_element_type=F32)

    q_spec = pl.BlockSpec((tq, HEAD_DIM), lambda kh, g, i: (i, kh * ATTN_GROUP + g))
    kv_spec = pl.BlockSpec((t, HEAD_DIM), lambda kh, g, i: (0, kh))
    return pl.pallas_call(
        body, name="attn_b", grid=(ATTN_KV_HEADS, ATTN_GROUP, n // tq),
        in_specs=[q_spec, kv_spec, kv_spec, q_spec],
        out_specs=[q_spec, kv_spec, kv_spec],
        out_shape=(jax.ShapeDtypeStruct(q.shape, F32), jax.ShapeDtypeStruct(k.shape, F32),
                   jax.ShapeDtypeStruct(v.shape, F32)),
        compiler_params=pltpu.CompilerParams(dimension_semantics=("parallel", "arbitrary", "arbitrary"),
                                             vmem_limit_bytes=VMEM_LIMIT),
    )(q, k, v, do)


@jax.custom_vjp
def attention(q, k, v):
    return _attn_fwd_call(q, k, v)


def _attention_fwd(q, k, v):
    return _attn_fwd_call(q, k, v), (q, k, v)


def _attention_bwd(res, do):
    return _attn_bwd_call(*res, do)


attention.defvjp(_attention_fwd, _attention_bwd)


_C = GDN_CHUNK


def _hdot(a, b):
    return jnp.dot(a, b, precision=HIGHEST, preferred_element_type=F32)


def _each(fn, *lists):
    return [fn(*args) for args in zip(*lists)]


def _unit_lower_inverse(low, blockdiag):
    eye = (lax.broadcasted_iota(jnp.int32, (_C, _C), 0) == lax.broadcasted_iota(jnp.int32, (_C, _C), 1)).astype(F32)
    ld = _each(lambda a: a * blockdiag, low)
    lo = _each(lambda a, d: a - d, low, ld)
    l2 = _each(_hdot, ld, ld)
    l4 = _each(_hdot, l2, l2)
    l8 = _each(_hdot, l4, l4)
    td = _each(lambda d, a2: _hdot(eye - d, eye + a2), ld, l2)
    td = _each(lambda t, a4: _hdot(t, eye + a4), td, l4)
    td = _each(lambda t, a8: _hdot(t, eye + a8), td, l8)
    nn = _each(_hdot, td, lo)
    n2 = _each(_hdot, nn, nn)
    out = _each(lambda n, m2: _hdot(eye - n, eye + m2), nn, n2)
    return _each(_hdot, out, td)


def _gdn_chunks(heads, blockdiag):
    q, k, v, b_b, be_b, e_b, kd_b, m1, dec, gl, s = (list(col) for col in zip(*heads))
    f32dot = lambda a, b: jnp.dot(a, b, preferred_element_type=F32)
    nt = lambda a, b: lax.dot_general(a, b, _NT, preferred_element_type=F32)
    kk = _each(nt, k, k)
    t_inv = _unit_lower_inverse(_each(lambda m, a: m * a, m1, kk), blockdiag)
    u = _each(lambda t, b, x: _hdot(t, b * x), t_inv, b_b, v)
    w = _each(lambda t, b, x: _hdot(t, b * x), t_inv, be_b, k)
    delta = _each(lambda uu, ww, ss: uu - f32dot(ww, ss), u, w, s)
    p = _each(lambda d, qq, kx: d * nt(qq, kx), dec, q, k)
    o = _each(lambda qq, e, ss, pp, dd: f32dot(qq * e, ss) + f32dot(pp, dd), q, e_b, s, p, delta)
    s_new = _each(lambda g, ss, kx, kd, dd: g * ss + lax.dot_general(kx * kd, dd, _TN, preferred_element_type=F32),
                  gl, s, k, kd_b, delta)
    return o, s_new


def _blockdiag_mask():
    r = lax.broadcasted_iota(jnp.int32, (_C, _C), 0) >> 4
    c = lax.broadcasted_iota(jnp.int32, (_C, _C), 1) >> 4
    return (r == c).astype(F32)


def _gdn_specs(nc, ncc, reverse, backward):
    def ch(s):
        s = nc - 1 - s if backward else s
        return jnp.where(s < ncc, ncc - 1 - s, nc + ncc - 1 - s) if reverse else s

    tok = pl.BlockSpec((_C, 3 * GDN_W), lambda s: (ch(s), 0))
    out = pl.BlockSpec((_C, GDN_W), lambda s: (ch(s), 0))
    per_tok = pl.BlockSpec((GDN_HEADS, _C, HEAD_DIM), lambda s: (0, ch(s), 0))
    mat = pl.BlockSpec((GDN_HEADS, None, _C, _C), lambda s: (0, ch(s), 0, 0))
    row = pl.BlockSpec((GDN_HEADS, None, 1, HEAD_DIM), lambda s: (0, ch(s), 0, 0))
    state = pl.BlockSpec((GDN_HEADS, None, HEAD_DIM, HEAD_DIM), lambda s: (0, ch(s), 0, 0))
    return tok, out, per_tok, mat, row, state


def _head_cols(h, part):
    return slice((part * GDN_HEADS + h) * HEAD_DIM, (part * GDN_HEADS + h + 1) * HEAD_DIM)


def _gdn_fwd_call(name, reverse, ncc, qkv, b_b, be_b, e_b, kd_b, m1, dec, gl):
    t = qkv.shape[0]
    nc = t // _C
    tok, out, per_tok, mat, row, state = _gdn_specs(nc, ncc, reverse, False)

    def body(qkv_ref, b_ref, be_ref, e_ref, kd_ref, m1_ref, dec_ref, gl_ref, o_ref, sall_ref, s_ref):
        @pl.when(pl.program_id(0) == 0)
        def _():
            s_ref[...] = jnp.zeros_like(s_ref)

        bd = _blockdiag_mask()
        ins = [[qkv_ref[:, _head_cols(h, 0)], qkv_ref[:, _head_cols(h, 1)], qkv_ref[:, _head_cols(h, 2)],
                b_ref[h], be_ref[h], e_ref[h], kd_ref[h], m1_ref[h], dec_ref[h], gl_ref[h], s_ref[h]]
               for h in range(GDN_HEADS)]
        o, s_new = _gdn_chunks(ins, bd)
        for h in range(GDN_HEADS):
            sall_ref[h] = ins[h][10]
            o_ref[:, _head_cols(h, 0)] = o[h]
            s_ref[h] = s_new[h]

    return pl.pallas_call(
        body, name=name + "_f", grid=(nc,),
        in_specs=[tok, per_tok, per_tok, per_tok, per_tok, mat, mat, row],
        out_specs=[out, state],
        out_shape=(jax.ShapeDtypeStruct((t, GDN_W), F32),
                   jax.ShapeDtypeStruct((GDN_HEADS, nc, HEAD_DIM, HEAD_DIM), F32)),
        scratch_shapes=[pltpu.VMEM((GDN_HEADS, HEAD_DIM, HEAD_DIM), F32)],
        compiler_params=pltpu.CompilerParams(dimension_semantics=("arbitrary",), vmem_limit_bytes=VMEM_LIMIT),
    )(qkv, b_b, be_b, e_b, kd_b, m1, dec, gl)


def _gdn_bwd_call(name, reverse, ncc, qkv, b_b, be_b, e_b, kd_b, m1, dec, gl, sall, do):
    t = qkv.shape[0]
    nc = t // _C
    tok, out, per_tok, mat, row, state = _gdn_specs(nc, ncc, reverse, True)

    def body(qkv_ref, b_ref, be_ref, e_ref, kd_ref, m1_ref, dec_ref, gl_ref, sall_ref, do_ref,
             dqkv_ref, db_ref, dbe_ref, de_ref, dkd_ref, dm1_ref, ddec_ref, dgl_ref, ds_ref):
        @pl.when(pl.program_id(0) == 0)
        def _():
            ds_ref[...] = jnp.zeros_like(ds_ref)

        bd = _blockdiag_mask()
        ins = [[qkv_ref[:, _head_cols(h, 0)], qkv_ref[:, _head_cols(h, 1)], qkv_ref[:, _head_cols(h, 2)],
                b_ref[h], be_ref[h], e_ref[h], kd_ref[h], m1_ref[h], dec_ref[h], gl_ref[h], sall_ref[h]]
               for h in range(GDN_HEADS)]
        _, vjp = jax.vjp(lambda hs: _gdn_chunks(hs, bd), ins)
        (all_grads,) = vjp(([do_ref[:, _head_cols(h, 0)] for h in range(GDN_HEADS)],
                            [ds_ref[h] for h in range(GDN_HEADS)]))
        for h, grads in enumerate(all_grads):
            for part in range(3):
                dqkv_ref[:, _head_cols(h, part)] = grads[part]
            for ref, g in zip((db_ref, dbe_ref, de_ref, dkd_ref, dm1_ref, ddec_ref, dgl_ref), grads[3:10]):
                ref[h] = g
            ds_ref[h] = grads[10]

    shp = lambda a: jax.ShapeDtypeStruct(a.shape, F32)
    return pl.pallas_call(
        body, name=name + "_b", grid=(nc,),
        in_specs=[tok, per_tok, per_tok, per_tok, per_tok, mat, mat, row, state, out],
        out_specs=[tok, per_tok, per_tok, per_tok, per_tok, mat, mat, row],
        out_shape=(shp(qkv), shp(b_b), shp(be_b), shp(e_b), shp(kd_b), shp(m1), shp(dec), shp(gl)),
        scratch_shapes=[pltpu.VMEM((GDN_HEADS, HEAD_DIM, HEAD_DIM), F32)],
        compiler_params=pltpu.CompilerParams(dimension_semantics=("arbitrary",), vmem_limit_bytes=VMEM_LIMIT),
    )(qkv, b_b, be_b, e_b, kd_b, m1, dec, gl, sall, do)


@functools.partial(jax.custom_vjp, nondiff_argnums=(0, 1, 2))
def gdn_scan(name, reverse, ncc, qkv, b_b, be_b, e_b, kd_b, m1, dec, gl):
    return _gdn_fwd_call(name, reverse, ncc, qkv, b_b, be_b, e_b, kd_b, m1, dec, gl)[0]


def _gdn_scan_fwd(name, reverse, ncc, *args):
    o, sall = _gdn_fwd_call(name, reverse, ncc, *args)
    return o, (args, sall)


def _gdn_scan_bwd(name, reverse, ncc, res, do):
    args, sall = res
    return _gdn_bwd_call(name, reverse, ncc, *args, sall, do)


gdn_scan.defvjp(_gdn_scan_fwd, _gdn_scan_bwd)


def _rope_tables(n, cl):
    t = np.arange(n)
    inv_freq = (ROPE_THETA ** (-np.arange(0, HEAD_DIM // 2, 2, dtype=np.float32) / (HEAD_DIM // 2))).astype(np.float32)
    ang_r = (t // GRID_W).astype(np.float32)[:, None] * inv_freq
    ang_c = (t % GRID_W).astype(np.float32)[:, None] * inv_freq
    cos = np.concatenate([np.cos(ang_r), np.cos(ang_r), np.cos(ang_c), np.cos(ang_c)], axis=1)
    sin = np.concatenate([-np.sin(ang_r), np.sin(ang_r), -np.sin(ang_c), np.sin(ang_c)], axis=1)
    cos_all = np.concatenate([np.ones((cl, HEAD_DIM), np.float32), cos], axis=0)
    sin_all = np.concatenate([np.zeros((cl, HEAD_DIM), np.float32), sin], axis=0)
    j = np.arange(HEAD_DIM)
    src = np.where((j % 64) < 32, j + 32, j - 32)
    perm = np.zeros((HEAD_DIM, HEAD_DIM), np.float32)
    perm[src, j] = 1.0
    return (jnp.asarray(cos.astype(np.float32)), jnp.asarray(sin.astype(np.float32)),
            jnp.asarray(cos_all), jnp.asarray(sin_all), jnp.asarray(perm))


def _shift_rows(a, cl):
    z = jnp.zeros((1, a.shape[1]), a.dtype)
    parts = [a[:cl], a[cl:]] if cl else [a]
    prev = jnp.concatenate([jnp.concatenate([z, p[:-1]], axis=0) for p in parts], axis=0)
    nxt = jnp.concatenate([jnp.concatenate([p[1:], z], axis=0) for p in parts], axis=0)
    return prev, nxt


def _gdn_factors(log_a, beta, reverse):
    t = log_a.shape[0]
    nc = t // _C
    la = log_a.reshape(nc, _C, GDN_HEADS).transpose(2, 0, 1)
    be = beta.reshape(nc, _C, GDN_HEADS).transpose(2, 0, 1)
    gam = lax.cumsum(la, axis=2, reverse=reverse)
    idx = jnp.arange(_C)
    incl = (idx[:, None] <= idx[None, :]) if reverse else (idx[:, None] >= idx[None, :])
    strict = (idx[:, None] < idx[None, :]) if reverse else (idx[:, None] > idx[None, :])
    dec = jnp.exp(jnp.where(incl, gam[..., :, None] - gam[..., None, :], -jnp.inf))
    m1 = jnp.where(strict, be[..., :, None] * dec, 0.0)
    e = jnp.exp(gam)
    g_last = gam[..., :1] if reverse else gam[..., -1:]
    lanes = lambda a: jnp.broadcast_to(a.reshape(GDN_HEADS, t, 1), (GDN_HEADS, t, HEAD_DIM))
    gl = jnp.broadcast_to(jnp.exp(g_last)[..., None], (GDN_HEADS, nc, 1, HEAD_DIM))
    return lanes(be), lanes(be * e), lanes(e), lanes(jnp.exp(g_last - gam)), m1, dec, gl


def local_loss(x, wz, wb, ws, c, ctx, target):
    n, cl = x.shape[0], ctx.shape[0]
    cos_q, sin_q, cos_k, sin_k, perm = _rope_tables(n, cl)

    sc_in = jnp.concatenate([jax.nn.silu(c), jax.nn.silu(ws["c_ctx"])[None, :], jnp.zeros((14, D_MODEL), F32)], axis=0)
    mod = pmm(sc_in, wb["w_mod"], wz["w_mod"], "mm_mod") + ws["b_mod"]
    sh1, sc1, g1, sh2, sc2, g2 = [mod[0:1, i * D_MODEL:(i + 1) * D_MODEL] for i in range(6)]
    csh1, csc1 = mod[1:2, 0:D_MODEL], mod[1:2, D_MODEL:2 * D_MODEL]

    def norm_mod(a, sh, sc):
        return (_rms(a) * (1.0 + sc) + sh,)

    (hx,) = rowop(norm_mod, "normmod_x", (x,), (sh1, sc1))
    (hc,) = rowop(norm_mod, "normmod_c", (ctx,), (csh1, csc1))
    h_all = jnp.concatenate([hc, hx], axis=0)
    p_main = pmm_t(h_all, wb["w_in_main"], wz["w_in_main"], "mm_in")
    p_small = pmm_t(h_all, wb["w_in_small"], wz["w_in_small"], "mm_ins")
    ak, av, qkv, aq, z, gate = jnp.split(p_main, [KV_W, 2 * KV_W, SMALL_AT, SMALL_AT + Q_W, SMALL_AT + Q_W + GDN_W],
                                         axis=1)
    db, da = p_small[:, :2 * GDN_HEADS], p_small[:, 2 * GDN_HEADS:4 * GDN_HEADS]

    def qk_prep(nh):
        def fn(a, w, cos, sin, pm):
            outs = []
            for ah in _heads(a, nh):
                y = _rms(ah) * w
                outs.append(y * cos + _hdot(y, pm) * sin)
            return (jnp.concatenate(outs, axis=1),)
        return fn

    (q_x,) = rowop(qk_prep(ATTN_HEADS), "q_prep", (aq[cl:],), (ws["q_norm_w"],), (cos_q, sin_q), (perm,))
    (k_all,) = rowop(qk_prep(ATTN_KV_HEADS), "k_prep", (ak,), (ws["k_norm_w"],), (cos_k, sin_k), (perm,))
    attn_x = attention(q_x, k_all, av)

    qkv_prev, qkv_next = _shift_rows(qkv, cl)
    cw = ws["conv_qkv_w"]

    def gdn_prep(a, ap, an, w0, w1, w2):
        s = jax.nn.silu(ap * w0 + a * w1 + an * w2)
        outs = []
        for i, sh in enumerate(_heads(s, 3 * GDN_HEADS)):
            if i < 2 * GDN_HEADS:
                sh = sh * lax.rsqrt(jnp.sum(sh * sh, axis=-1, keepdims=True) + NORM_EPS)
                if i < GDN_HEADS:
                    sh = sh * (HEAD_DIM ** -0.5)
            outs.append(sh)
        return (jnp.concatenate(outs, axis=1),)

    (qkvn,) = rowop(gdn_prep, "gdn_prep", (qkv, qkv_prev, qkv_next), (cw[0:1], cw[1:2], cw[2:3]), tr=128)
    beta = jax.nn.sigmoid(db).reshape(-1, 2, GDN_HEADS)
    log_a = -jnp.exp(ws["a_log"])[None] * jax.nn.softplus(da.reshape(-1, 2, GDN_HEADS) + ws["dt_bias"][None])
    o_fwd = gdn_scan("gdn_d0", False, cl // _C, qkvn, *_gdn_factors(log_a[:, 0], beta[:, 0], False))
    o_bwd = gdn_scan("gdn_d1", True, cl // _C, qkvn, *_gdn_factors(log_a[:, 1], beta[:, 1], True))
    o_x = o_fwd[cl:] + o_bwd[cl:]

    def gdn_out(o, zz, w):
        outs = [_rms(oh) * w * jax.nn.silu(zh) for oh, zh in zip(_heads(o, GDN_HEADS), _heads(zz, GDN_HEADS))]
        return (jnp.concatenate(outs, axis=1),)

    (gdn_x,) = rowop(gdn_out, "gdn_out", (o_x, z[cl:]), (ws["gdn_norm_w"],))

    pa = pmm(attn_x, wb["w_pa"], wz["w_pa"], "mm_pa")
    pd = pmm(gdn_x, wb["w_pd"], wz["w_pd"], "mm_pd")

    def merge(a, d, g):
        return (jax.nn.sigmoid(g[:, :D_MODEL]) * a + jax.nn.sigmoid(g[:, D_MODEL:]) * d,)

    (y,) = rowop(merge, "merge", (pa, pd, gate[cl:]))
    mo = pmm(y, wb["w_out"], wz["w_out"], "mm_out")

    def res_norm_mod(xx, m, g, sh, sc):
        x1 = xx + g * m
        return x1, _rms(x1) * (1.0 + sc) + sh

    x1, h2 = rowop(res_norm_mod, "res1", (x, mo), (g1, sh2, sc2))
    up = pmm(h2, wb["w_up"], wz["w_up"], "mm_up")
    up_prev, up_next = _shift_rows(up, 0)
    fw = ws["ffn_conv_w"]

    def ffn_act(a, ap, an, w0, w1, w2, b):
        u = ap * w0 + a * w1 + an * w2 + b
        return (jax.nn.silu(u[:, :D_FF]) * u[:, D_FF:],)

    (act,) = rowop(ffn_act, "ffn_act", (up, up_prev, up_next), (fw[0:1], fw[1:2], fw[2:3], ws["ffn_conv_b"]), tr=128)
    dn = pmm(act, wb["w_down"], wz["w_down"], "mm_down")

    def head(xx, m, g, w, tgt):
        yy = _rms(xx + g * m) * w
        err = (yy - tgt) ** 2
        return (jnp.broadcast_to(0.5 * jnp.mean(err, axis=-1, keepdims=True), (xx.shape[0], HEAD_DIM)),)

    (row_loss,) = rowop(head, "head", (x1, dn), (g2, ws["final_norm_w"][None, :]), (target,))
    return jnp.sum(row_loss[:, 0])


_HBM = pl.BlockSpec(memory_space=pltpu.HBM)


def _chip_peers():
    x, y = lax.axis_index("x"), lax.axis_index("y")
    return [(1 - x, y), (x, 1 - y), (1 - x, 1 - y)]


_SPLIT_COLS = ("w_in",)


def _half_of(view, nm, idx, lead=0):
    r, cdim = view.shape[-2:]
    pre = (slice(None),) * lead
    if nm in _SPLIT_COLS:
        return view.at[pre + (slice(None), pl.ds(pl.multiple_of(idx * (cdim // 2), 128), cdim // 2))]
    return view.at[pre + (pl.ds(pl.multiple_of(idx * (r // 2), 16), r // 2), slice(None))]


def _remote(src, dst, send_sem, recv_sem, dev):
    return pltpu.make_async_remote_copy(src_ref=src, dst_ref=dst, send_sem=send_sem, recv_sem=recv_sem,
                                        device_id=dev, device_id_type=MESH)


def _hbm_call(body, name, ins, out_shape, n_sems):
    names = tuple(ins)
    return dict(zip(names, pl.pallas_call(
        body, name=name, in_specs=[_HBM] * len(names), out_specs=[_HBM] * len(names),
        out_shape=[out_shape(nm, ins[nm]) for nm in names],
        scratch_shapes=[pltpu.SemaphoreType.DMA((k,)) for k in n_sems],
    )(*[ins[nm] for nm in names])))


def all_gather_chips(shards):
    names = tuple(shards)
    n = len(names)

    def body(*refs):
        ins, outs = dict(zip(names, refs[:n])), dict(zip(names, refs[n:2 * n]))
        ici_send, ici_recv, d2d_send, d2d_recv, local_sems = refs[2 * n:]
        x, y, c = lax.axis_index("x"), lax.axis_index("y"), lax.axis_index("c")
        me, sib = 2 * x + y, (x, y, 1 - c)
        local = [pltpu.make_async_copy(ins[nm], outs[nm].at[me], local_sems.at[i]) for i, nm in enumerate(names)]
        for cp in local:
            cp.start()
        sends = []
        for k, (px, py) in enumerate(_chip_peers()):
            for i, nm in enumerate(names):
                cp = _remote(_half_of(ins[nm], nm, c), _half_of(outs[nm].at[me], nm, c), ici_send.at[k * n + i],
                             ici_recv.at[k * n + i], (px, py, c))
                cp.start()
                sends.append(cp)
        for k, (px, py) in enumerate(_chip_peers()):
            for i, nm in enumerate(names):
                landed = _half_of(outs[nm].at[2 * px + py], nm, c)
                _remote(landed, landed, ici_send.at[k * n + i], ici_recv.at[k * n + i], (px, py, c)).wait_recv()
                fw = _remote(landed, landed, d2d_send.at[k * n + i], d2d_recv.at[k * n + i], sib)
                fw.start()
                sends.append(fw)
        for k, (px, py) in enumerate(_chip_peers()):
            for i, nm in enumerate(names):
                other = _half_of(outs[nm].at[2 * px + py], nm, 1 - c)
                _remote(other, other, d2d_send.at[k * n + i], d2d_recv.at[k * n + i], sib).wait_recv()
        for cp in sends:
            cp.wait_send()
        for cp in local:
            cp.wait()

    return _hbm_call(body, "ag_weights", shards, lambda nm, a: jax.ShapeDtypeStruct((N_CHIPS,) + a.shape, a.dtype),
                     (3 * n, 3 * n, 3 * n, 3 * n, n))


def sibling_halves(blocks):
    names = tuple(blocks)

    def body(*refs):
        n = len(names)
        ins, outs = dict(zip(names, refs[:n])), dict(zip(names, refs[n:2 * n]))
        send_sems, recv_sems = refs[2 * n:]
        x, y, c = lax.axis_index("x"), lax.axis_index("y"), lax.axis_index("c")
        cps = [_remote(_half_of(ins[nm], nm, 1 - c, lead=1), outs[nm], send_sems.at[i], recv_sems.at[i], (x, y, 1 - c))
               for i, nm in enumerate(names)]
        for cp in cps:
            cp.start()
        for cp in cps:
            cp.wait()

    def half_shape(nm, a):
        r, cdim = a.shape[-2:]
        return jax.ShapeDtypeStruct((N_CHIPS, r, cdim // 2) if nm in _SPLIT_COLS else (N_CHIPS, r // 2, cdim), a.dtype)

    return _hbm_call(body, "rs_sibling", blocks, half_shape, (len(names), len(names)))


def scatter_halves(blocks):
    names = tuple(blocks)
    n = len(names)

    def body(*refs):
        ins, outs = dict(zip(names, refs[:n])), dict(zip(names, refs[n:2 * n]))
        send_sems, recv_sems = refs[2 * n:]
        c = lax.axis_index("c")
        cps = [_remote(ins[nm].at[2 * px + py], outs[nm].at[k], send_sems.at[k * n + i], recv_sems.at[k * n + i],
                       (px, py, c))
               for k, (px, py) in enumerate(_chip_peers()) for i, nm in enumerate(names)]
        for cp in cps:
            cp.start()
        for cp in cps:
            cp.wait_recv()
        for cp in cps:
            cp.wait_send()

    return _hbm_call(body, "rs_grads", blocks, lambda nm, a: jax.ShapeDtypeStruct((3,) + a.shape[1:], a.dtype),
                     (3 * n, 3 * n))


def sibling_assemble(halves, full_shapes):
    names = tuple(halves)

    def body(*refs):
        n = len(names)
        ins, outs = dict(zip(names, refs[:n])), dict(zip(names, refs[n:2 * n]))
        send_sems, recv_sems, local_sems = refs[2 * n:]
        x, y, c = lax.axis_index("x"), lax.axis_index("y"), lax.axis_index("c")
        local = [pltpu.make_async_copy(ins[nm], _half_of(outs[nm], nm, c), local_sems.at[i])
                 for i, nm in enumerate(names)]
        cps = [_remote(ins[nm], _half_of(outs[nm], nm, c), send_sems.at[i], recv_sems.at[i], (x, y, 1 - c))
               for i, nm in enumerate(names)]
        for cp in local + cps:
            cp.start()
        for i, nm in enumerate(names):
            _remote(ins[nm], _half_of(outs[nm], nm, 1 - c), send_sems.at[i], recv_sems.at[i], (x, y, 1 - c)).wait_recv()
        for cp in cps:
            cp.wait_send()
        for cp in local:
            cp.wait()

    return _hbm_call(body, "rs_assemble", halves, lambda nm, a: jax.ShapeDtypeStruct(full_shapes[nm], a.dtype),
                     (len(names), len(names), len(names)))


def all_reduce_small(v):
    def body(v_ref, tot_ref, gath_ref, send_sems, recv_sems):
        x, y, c = lax.axis_index("x"), lax.axis_index("y"), lax.axis_index("c")
        me = 4 * x + 2 * y + c
        gath_ref[me] = v_ref[...]

        def peer(k):
            m = k + 1
            return (x ^ (m >> 2 & 1), y ^ (m >> 1 & 1), c ^ (m & 1))

        sends = [pltpu.make_async_remote_copy(src_ref=v_ref, dst_ref=gath_ref.at[me], send_sem=send_sems.at[k],
                                              recv_sem=recv_sems.at[k], device_id=peer(k), device_id_type=MESH)
                 for k in range(N_DEV - 1)]
        for cp in sends:
            cp.start()
        for k in range(N_DEV - 1):
            px, py, pc = peer(k)
            pltpu.make_async_remote_copy(src_ref=v_ref, dst_ref=gath_ref.at[4 * px + 2 * py + pc],
                                         send_sem=send_sems.at[k], recv_sem=recv_sems.at[k], device_id=peer(k),
                                         device_id_type=MESH).wait_recv()
        for cp in sends:
            cp.wait_send()
        acc = gath_ref[0]
        for d in range(1, N_DEV):
            acc = acc + gath_ref[d]
        tot_ref[...] = acc

    vm = pl.BlockSpec(memory_space=pltpu.VMEM)
    return pl.pallas_call(
        body, name="ar_small", in_specs=[vm], out_specs=[vm, vm],
        out_shape=(jax.ShapeDtypeStruct(v.shape, v.dtype), jax.ShapeDtypeStruct((N_DEV,) + v.shape, v.dtype)),
        scratch_shapes=[pltpu.SemaphoreType.DMA((N_DEV - 1,)), pltpu.SemaphoreType.DMA((N_DEV - 1,))],
    )(v)[0]


def _elementwise(fn, name, ins, n_out, out_dtype=F32):
    r, cdim = ins[0].shape
    tr = _pick(r, tuple(p for p in (488, 256, 128, 104, 64, 32, 16, 8) if p * cdim * 4 <= 2 * 1024 * 1024))
    spec = pl.BlockSpec((tr, cdim), lambda i: (i, 0))

    def body(*refs):
        res = fn(*[ref[...] for ref in refs[:len(ins)]])
        for o_ref, v in zip(refs[len(ins):], res):
            o_ref[...] = v

    return pl.pallas_call(
        body, name=name, grid=(r // tr,), in_specs=[spec] * len(ins), out_specs=[spec] * n_out,
        out_shape=tuple(jax.ShapeDtypeStruct((r, cdim), out_dtype) for _ in range(n_out)),
        compiler_params=pltpu.CompilerParams(dimension_semantics=("parallel",), vmem_limit_bytes=VMEM_LIMIT),
    )(*ins)


def _half_block_specs(nm, shard_shape):
    r, cdim = shard_shape
    if nm in _SPLIT_COLS:
        return (None, r, cdim // 2), (lambda j, c: (j, 0, c))
    return (None, r // 2, cdim), (lambda j, c: (j, c, 0))


def _presum(nm, sel, g32, a):
    blk, at = _half_block_specs(nm, g32.shape[1:])

    def body(s_ref, g_ref, a_ref, o_ref):
        del s_ref
        o_ref[...] = (g_ref[...] + a_ref[...]).astype(BF16)

    return pl.pallas_call(
        body, name="rs_presum_" + nm,
        grid_spec=pltpu.PrefetchScalarGridSpec(
            num_scalar_prefetch=1, grid=(N_CHIPS,),
            in_specs=[pl.BlockSpec(blk, lambda j, s: at(j, s[0])), pl.BlockSpec(blk, lambda j, s: (j, 0, 0))],
            out_specs=pl.BlockSpec(blk, lambda j, s: (j, 0, 0))),
        out_shape=jax.ShapeDtypeStruct(a.shape, BF16),
        compiler_params=pltpu.CompilerParams(dimension_semantics=("parallel",), vmem_limit_bytes=VMEM_LIMIT),
    )(sel, g32, a)


def _finalsum(nm, sel, g32, a, got):
    blk, at = _half_block_specs(nm, g32.shape[1:])

    def body(s_ref, g_ref, a_ref, r_ref, o_ref):
        del s_ref
        acc = g_ref[...] + a_ref[...]
        for k in range(3):
            acc = acc + r_ref[k].astype(F32)
        o_ref[...] = acc

    return pl.pallas_call(
        body, name="rs_final_" + nm,
        grid_spec=pltpu.PrefetchScalarGridSpec(
            num_scalar_prefetch=1, grid=(1,),
            in_specs=[pl.BlockSpec(blk, lambda i, s: at(s[1], s[0])), pl.BlockSpec(blk, lambda i, s: (s[1], 0, 0)),
                      pl.BlockSpec(got.shape, lambda i, s: (0, 0, 0))],
            out_specs=pl.BlockSpec(a.shape[1:], lambda i, s: (0, 0))),
        out_shape=jax.ShapeDtypeStruct(a.shape[1:], F32),
        compiler_params=pltpu.CompilerParams(dimension_semantics=("arbitrary",), vmem_limit_bytes=VMEM_LIMIT),
    )(sel, g32, a, got)


def _adamw(w, g, m, v, name):
    shape = w.shape
    to2 = lambda a: a.reshape(-1, shape[-1])

    def fn(w_, g_, m_, v_):
        m_new = ADAM_B1 * m_ + (1.0 - ADAM_B1) * g_
        v_new = ADAM_B2 * v_ + (1.0 - ADAM_B2) * (g_ * g_)
        m_hat = m_new / (1.0 - ADAM_B1 ** ADAM_STEP)
        v_hat = v_new / (1.0 - ADAM_B2 ** ADAM_STEP)
        delta = -ADAM_LR * (m_hat / (jnp.sqrt(v_hat) + ADAM_EPS) + ADAM_WD * w_)
        return delta, m_new, v_new

    outs = _elementwise(fn, name, [to2(a) for a in (w, g, m, v)], 3)
    return tuple(o.reshape(shape) for o in outs)


_BIG = ("w_mod", "w_in", "w_pa", "w_pd", "w_out", "w_up", "w_down")
_COL_SHARDED = ("w_mod", "w_up")
_FULL_SHAPE = {"w_mod": (D_MODEL, MOD_W), "w_in": (IN_COLS, D_MODEL), "w_pa": (Q_W, D_MODEL), "w_pd": (GDN_W, D_MODEL),
               "w_out": (D_MODEL, D_MODEL), "w_up": (D_MODEL, 2 * D_FF), "w_down": (D_FF, D_MODEL)}


def _shard_shape(name):
    r, cdim = _FULL_SHAPE[name]
    return (r, cdim // N_CHIPS) if name in _COL_SHARDED else (r // N_CHIPS, cdim)


_CONV_ELEMS = 2 * (3 * CONV_W // N_CHIPS + 3 * 2 * D_FF // N_CHIPS)
_CONV_ROWS = 32


def _blocks_of_full(name, full):
    r, cdim = _FULL_SHAPE[name]
    if name in _COL_SHARDED:
        return full.reshape(r, N_CHIPS, cdim // N_CHIPS).transpose(1, 0, 2)
    return full.reshape(N_CHIPS, r // N_CHIPS, cdim)


def _full_of_blocks(name, blocks):
    r, cdim = _FULL_SHAPE[name]
    if name in _COL_SHARDED:
        return blocks.transpose(1, 0, 2).reshape(r, cdim)
    return blocks.reshape(r, cdim)


def _w_in_regroup(w_in_t):
    main = jnp.concatenate([w_in_t[:SMALL_AT], w_in_t[SMALL_AT + 4 * GDN_HEADS:]], axis=0)
    small = jnp.pad(w_in_t[SMALL_AT:SMALL_AT + 4 * GDN_HEADS], ((0, HEAD_DIM - 4 * GDN_HEADS), (0, 0)))
    return main, small


def _w_in_ungroup(main, small):
    return jnp.concatenate([main[:SMALL_AT], small[:4 * GDN_HEADS], main[SMALL_AT:]], axis=0)


_SMALL = ("c_ctx", "b_mod", "q_norm_w", "k_norm_w", "conv_qkv_w", "a_log", "dt_bias", "gdn_norm_w", "ffn_conv_w",
          "ffn_conv_b", "final_norm_w")


def _pack_small(tree, rows):
    flat = jnp.concatenate([tree[nm].reshape(-1) for nm in _SMALL])
    return jnp.pad(flat, (0, rows * 128 - flat.shape[0])).reshape(rows, 128)


def _unpack_small(packed, like):
    flat, out, off = packed.reshape(-1), {}, 0
    for nm in _SMALL:
        size = int(np.prod(like[nm].shape))
        out[nm] = flat[off:off + size].reshape(like[nm].shape)
        off += size
    return out


def kernel(x, c, ctx, c_ctx, w_mod, b_mod, w_in, q_norm_w, k_norm_w, conv_qkv_w, a_log, dt_bias, gdn_norm_w, w_pa, w_pd, w_out, w_up, ffn_conv_w, ffn_conv_b, w_down, final_norm_w, loss_target, m_c_ctx, m_w_mod, m_b_mod, m_w_in, m_q_norm_w, m_k_norm_w, m_conv_qkv_w, m_a_log, m_dt_bias, m_gdn_norm_w, m_w_pa, m_w_pd, m_w_out, m_w_up, m_ffn_conv_w, m_ffn_conv_b, m_w_down, m_final_norm_w, v_c_ctx, v_w_mod, v_b_mod, v_w_in, v_q_norm_w, v_k_norm_w, v_conv_qkv_w, v_a_log, v_dt_bias, v_gdn_norm_w, v_w_pa, v_w_pd, v_w_out, v_w_up, v_ffn_conv_w, v_ffn_conv_b, v_w_down, v_final_norm_w):
    names = ("c_ctx", "w_mod", "b_mod", "w_in", "q_norm_w", "k_norm_w", "conv_qkv_w", "a_log", "dt_bias", "gdn_norm_w",
             "w_pa", "w_pd", "w_out", "w_up", "ffn_conv_w", "ffn_conv_b", "w_down", "final_norm_w")
    w_sh = dict(c_ctx=c_ctx, w_mod=w_mod, b_mod=b_mod, w_in=w_in, q_norm_w=q_norm_w, k_norm_w=k_norm_w,
                conv_qkv_w=conv_qkv_w, a_log=a_log, dt_bias=dt_bias, gdn_norm_w=gdn_norm_w, w_pa=w_pa, w_pd=w_pd,
                w_out=w_out, w_up=w_up, ffn_conv_w=ffn_conv_w, ffn_conv_b=ffn_conv_b, w_down=w_down,
                final_norm_w=final_norm_w)
    m_sh = dict(c_ctx=m_c_ctx, w_mod=m_w_mod, b_mod=m_b_mod, w_in=m_w_in, q_norm_w=m_q_norm_w, k_norm_w=m_k_norm_w,
                conv_qkv_w=m_conv_qkv_w, a_log=m_a_log, dt_bias=m_dt_bias, gdn_norm_w=m_gdn_norm_w, w_pa=m_w_pa,
                w_pd=m_w_pd, w_out=m_w_out, w_up=m_w_up, ffn_conv_w=m_ffn_conv_w, ffn_conv_b=m_ffn_conv_b,
                w_down=m_w_down, final_norm_w=m_final_norm_w)
    v_sh = dict(c_ctx=v_c_ctx, w_mod=v_w_mod, b_mod=v_b_mod, w_in=v_w_in, q_norm_w=v_q_norm_w, k_norm_w=v_k_norm_w,
                conv_qkv_w=v_conv_qkv_w, a_log=v_a_log, dt_bias=v_dt_bias, gdn_norm_w=v_gdn_norm_w, w_pa=v_w_pa,
                w_pd=v_w_pd, w_out=v_w_out, w_up=v_w_up, ffn_conv_w=v_ffn_conv_w, ffn_conv_b=v_ffn_conv_b,
                w_down=v_w_down, final_norm_w=v_final_norm_w)
    chip = 2 * lax.axis_index("x") + lax.axis_index("y")

    conv_bits = jnp.concatenate([lax.bitcast_convert_type(w_sh[nm][0], BF16).reshape(-1)
                                 for nm in ("conv_qkv_w", "ffn_conv_w")])
    shards = {nm: w_sh[nm][0].astype(BF16).T if nm == "w_in" else w_sh[nm][0].astype(BF16) for nm in _BIG}
    shards["conv"] = jnp.pad(conv_bits, (0, _CONV_ROWS * D_MODEL - _CONV_ELEMS)).reshape(_CONV_ROWS, D_MODEL)
    gathered = all_gather_chips(shards)
    wb = {nm: _full_of_blocks(nm, gathered[nm]) for nm in _BIG}
    wb["w_in_main"], wb["w_in_small"] = _w_in_regroup(wb.pop("w_in"))
    conv_all = gathered["conv"].reshape(N_CHIPS, -1)[:, :_CONV_ELEMS]
    n_cq = 2 * 3 * CONV_W // N_CHIPS
    unbits = lambda a, w: lax.bitcast_convert_type(a.reshape(N_CHIPS, 3, w // N_CHIPS, 2), F32).transpose(1, 0, 2).reshape(3, w)
    ws = dict(c_ctx=c_ctx, b_mod=b_mod, q_norm_w=q_norm_w, k_norm_w=k_norm_w, a_log=a_log[0], dt_bias=dt_bias[0],
              gdn_norm_w=gdn_norm_w, ffn_conv_b=ffn_conv_b, final_norm_w=final_norm_w,
              conv_qkv_w=unbits(conv_all[:, :n_cq], CONV_W), ffn_conv_w=unbits(conv_all[:, n_cq:], 2 * D_FF))
    wz = {nm: jnp.zeros(a.shape, F32) for nm, a in wb.items()}

    loss_local, (gx, gz, gs) = jax.value_and_grad(local_loss, argnums=(0, 1, 3))(
        x[0], wz, wb, ws, c, ctx[0], loss_target[0])
    loss = lax.psum(loss_local, ("x", "y", "c"))

    gz["w_in"] = _w_in_ungroup(gz.pop("w_in_main"), gz.pop("w_in_small"))
    g32 = {nm: _blocks_of_full(nm, gz[nm]) for nm in _BIG}
    sel = jnp.stack([lax.axis_index("c"), chip]).astype(jnp.int32)
    theirs = sibling_halves(g32)
    got = scatter_halves({nm: _presum(nm, sel, g32[nm], theirs[nm]) for nm in _BIG})
    g_big = sibling_assemble({nm: _finalsum(nm, sel, g32[nm], theirs[nm], got[nm]) for nm in _BIG},
                             {nm: _shard_shape(nm) for nm in _BIG})

    gs["a_log"], gs["dt_bias"] = gs["a_log"][None], gs["dt_bias"][None]
    like = {nm: gs[nm] for nm in _SMALL}
    small_rows = -(-sum(int(np.prod(like[nm].shape)) for nm in _SMALL) // 1024) * 8
    g_small = _unpack_small(all_reduce_small(_pack_small(gs, small_rows)), like)
    for nm, width in (("conv_qkv_w", CONV_W), ("ffn_conv_w", 2 * D_FF)):
        g_small[nm] = lax.dynamic_slice_in_dim(g_small[nm], chip * (width // N_CHIPS), width // N_CHIPS, axis=1)[None]

    grads, deltas, new_m, new_v = {}, {}, {}, {}
    for nm in _BIG:
        g = g_big[nm].T if nm == "w_in" else g_big[nm]
        grads[nm] = g[None]
        deltas[nm], new_m[nm], new_v[nm] = (o[None] for o in _adamw(w_sh[nm][0], g, m_sh[nm][0], v_sh[nm][0],
                                                                     "adamw_" + nm))
    shard_like = {nm: w_sh[nm] for nm in _SMALL}
    rows_l = -(-sum(int(np.prod(shard_like[nm].shape)) for nm in _SMALL) // 1024) * 8
    g_l = _pack_small({nm: g_small[nm].reshape(w_sh[nm].shape) for nm in _SMALL}, rows_l)
    outs = _adamw(_pack_small(w_sh, rows_l), g_l, _pack_small(m_sh, rows_l), _pack_small(v_sh, rows_l), "adamw_small")
    grads.update(_unpack_small(g_l, shard_like))
    for tree, packed in zip((deltas, new_m, new_v), outs):
        tree.update(_unpack_small(packed, shard_like))

    return (loss, gx[None], *[grads[nm] for nm in names], *[deltas[nm] for nm in names],
            *[new_m[nm] for nm in names], *[new_v[nm] for nm in names])
```

```python
import functools
import math

import jax
import jax.numpy as jnp
import numpy as np
from jax import lax
from jax.experimental import pallas as pl
from jax.experimental.pallas import tpu as pltpu

F32 = jnp.float32
BF16 = jnp.bfloat16
HIGHEST = lax.Precision.HIGHEST
MESH = pl.DeviceIdType.MESH

D_MODEL = 1024
GRID_W = 64
ATTN_HEADS = 8
ATTN_KV_HEADS = 2
ATTN_GROUP = ATTN_HEADS // ATTN_KV_HEADS
HEAD_DIM = 128
ROPE_THETA = 10000.0
GDN_HEADS = 8
GDN_CHUNK = 64
D_FF = 2816
NORM_EPS = 1e-6
KV_W = ATTN_KV_HEADS * HEAD_DIM
Q_W = ATTN_HEADS * HEAD_DIM
GDN_W = GDN_HEADS * HEAD_DIM
CONV_W = 3 * GDN_W
MOD_W = 6 * D_MODEL
IN_COLS = 2 * KV_W + CONV_W + 4 * GDN_HEADS + Q_W + GDN_W + 2 * D_MODEL
IN_MAIN = IN_COLS - 4 * GDN_HEADS
SMALL_AT = 2 * KV_W + CONV_W
N_CHIPS = 4
N_DEV = 8

ADAM_LR = 0.001
ADAM_B1 = 0.9
ADAM_B2 = 0.999
ADAM_EPS = 1e-08
ADAM_WD = 0.01
ADAM_STEP = 10

VMEM_LIMIT = 48 * 1024 * 1024


def _pick(dim, prefs):
    for p in prefs:
        if p <= dim and dim % p == 0:
            return p
    return dim


_DIMS = {
    "nn": (((1,), (0,)), ((), ())),
    "nt": (((1,), (1,)), ((), ())),
    "tn": (((0,), (0,)), ((), ())),
}


def _matmul(a, b, mode, name):
    if mode == "nn":
        (m, k), (_, n) = a.shape, b.shape
    elif mode == "nt":
        (m, k), (n, _) = a.shape, b.shape
    else:
        (k, m), (_, n) = a.shape, b.shape
    tm = _pick(m, (512, 384, 256, 128))
    tn = _pick(n, (1536, 1408, 1024, 768, 512, 256, 128))
    tk = _pick(k, (1024, 1408, 768, 512, 256, 128))
    nk = k // tk
    if mode == "tn":
        a_spec = pl.BlockSpec((tk, tm), lambda i, j, l: (l, i))
    else:
        a_spec = pl.BlockSpec((tm, tk), lambda i, j, l: (i, l))
    if mode == "nt":
        b_spec = pl.BlockSpec((tn, tk), lambda i, j, l: (j, l))
    else:
        b_spec = pl.BlockSpec((tk, tn), lambda i, j, l: (l, j))
    dims = _DIMS[mode]

    def body(a_ref, b_ref, o_ref, acc_ref):
        l = pl.program_id(2)

        @pl.when(l == 0)
        def _():
            acc_ref[...] = jnp.zeros_like(acc_ref)

        acc_ref[...] += lax.dot_general(a_ref[...].astype(BF16), b_ref[...].astype(BF16), dims,
                                        preferred_element_type=F32)

        @pl.when(l == nk - 1)
        def _():
            o_ref[...] = acc_ref[...]

    return pl.pallas_call(
        body,
        name=name,
        grid=(m // tm, n // tn, nk),
        in_specs=[a_spec, b_spec],
        out_specs=pl.BlockSpec((tm, tn), lambda i, j, l: (i, j)),
        out_shape=jax.ShapeDtypeStruct((m, n), F32),
        scratch_shapes=[pltpu.VMEM((tm, tn), F32)],
        compiler_params=pltpu.CompilerParams(dimension_semantics=("parallel", "parallel", "arbitrary"),
                                             vmem_limit_bytes=VMEM_LIMIT),
    )(a, b)


@functools.partial(jax.custom_vjp, nondiff_argnums=(3,))
def pmm(a, w, wz, name):
    del wz
    return _matmul(a, w, "nn", name + "_f")


def _pmm_fwd(a, w, wz, name):
    del wz
    return _matmul(a, w, "nn", name + "_f"), (a, w)


def _pmm_bwd(name, res, g):
    a, w = res
    da = _matmul(g, w, "nt", name + "_da")
    if a.shape[0] < 128:
        pad = 128 - a.shape[0]
        at = jnp.pad(a.T, ((0, 0), (0, pad)))
        gp = jnp.pad(g, ((0, pad), (0, 0)))
        dw = _matmul(at, gp, "nn", name + "_dw")
    else:
        dw = _matmul(a, g, "tn", name + "_dw")
    return da, jnp.zeros_like(w), dw


pmm.defvjp(_pmm_fwd, _pmm_bwd)


@functools.partial(jax.custom_vjp, nondiff_argnums=(3,))
def pmm_t(a, wt, wtz, name):
    del wtz
    return _matmul(a, wt, "nt", name + "_f")


def _pmm_t_fwd(a, wt, wtz, name):
    del wtz
    return _matmul(a, wt, "nt", name + "_f"), (a, wt)


def _pmm_t_bwd(name, res, g):
    a, wt = res
    return _matmul(g, wt, "nn", name + "_da"), jnp.zeros_like(wt), _matmul(g, a, "tn", name + "_dw")


pmm_t.defvjp(_pmm_t_fwd, _pmm_t_bwd)


def rowop(fn, name, rows, bcs=(), crows=(), cbcs=(), tr=256):
    rows, bcs, crows, cbcs = tuple(rows), tuple(bcs), tuple(crows), tuple(cbcs)
    n_rows = rows[0].shape[0]
    tr = _pick(n_rows, (tr, 128, 64, 32, 16, 8))
    nr, nb, ncr, ncb = len(rows), len(bcs), len(crows), len(cbcs)
    n_in = nr + nb + ncr + ncb
    grid = (n_rows // tr,)

    def blk(arr):
        return jax.ShapeDtypeStruct((tr, arr.shape[1]), arr.dtype)

    def row_spec(arr):
        return pl.BlockSpec((tr, arr.shape[1]), lambda i: (i, 0))

    def bc_spec(arr):
        return pl.BlockSpec(arr.shape, lambda i: (0, 0))

    out_blk = jax.eval_shape(fn, *[blk(r) for r in rows], *bcs, *[blk(r) for r in crows], *cbcs)
    n_out = len(out_blk)
    out_shape = tuple(jax.ShapeDtypeStruct((n_rows, o.shape[1]), o.dtype) for o in out_blk)
    in_specs = ([row_spec(r) for r in rows] + [bc_spec(b) for b in bcs]
                + [row_spec(r) for r in crows] + [bc_spec(b) for b in cbcs])

    def order(vals):
        return vals

    def fwd_call(args):
        def body(*refs):
            vals = [r[...] for r in refs[:n_in]]
            res = fn(*order(vals))
            for o_ref, r in zip(refs[n_in:], res):
                o_ref[...] = r

        return pl.pallas_call(
            body, name=name + "_f", grid=grid, in_specs=in_specs,
            out_specs=[row_spec(o) for o in out_shape], out_shape=out_shape,
            compiler_params=pltpu.CompilerParams(dimension_semantics=("parallel",), vmem_limit_bytes=VMEM_LIMIT),
        )(*args)

    def bwd_call(args, cts):
        def body(*refs):
            vals = [r[...] for r in refs[:n_in]]
            ct_refs = refs[n_in:n_in + n_out]
            d_rows = refs[n_in + n_out:n_in + n_out + nr]
            d_bcs = refs[n_in + n_out + nr:]
            consts = vals[nr + nb:]
            _, vjp = jax.vjp(lambda *p: fn(*p, *consts), *vals[:nr + nb])
            grads = vjp(tuple(c[...] for c in ct_refs))
            for ref, g in zip(d_rows, grads[:nr]):
                ref[...] = g

            @pl.when(pl.program_id(0) == 0)
            def _():
                for ref in d_bcs:
                    ref[...] = jnp.zeros_like(ref)

            for ref, g in zip(d_bcs, grads[nr:]):
                ref[...] += g

        d_shape = tuple(jax.ShapeDtypeStruct(r.shape, r.dtype) for r in rows + bcs)
        return pl.pallas_call(
            body, name=name + "_b", grid=grid,
            in_specs=in_specs + [row_spec(o) for o in out_shape],
            out_specs=[row_spec(r) for r in rows] + [bc_spec(b) for b in bcs], out_shape=d_shape,
            compiler_params=pltpu.CompilerParams(dimension_semantics=("arbitrary",), vmem_limit_bytes=VMEM_LIMIT),
        )(*args, *cts)

    @jax.custom_vjp
    def op(diff, const):
        return fwd_call(diff + const)

    def op_fwd(diff, const):
        return fwd_call(diff + const), (diff, const)

    def op_bwd(res, cts):
        diff, const = res
        grads = bwd_call(diff + const, tuple(cts))
        return tuple(grads), tuple(jnp.zeros_like(c) for c in const)

    op.defvjp(op_fwd, op_bwd)
    return op(rows + bcs, crows + cbcs)


def _rms(x):
    return x * lax.rsqrt(jnp.mean(x * x, axis=-1, keepdims=True) + NORM_EPS)


def _heads(x, n):
    return [x[:, h * HEAD_DIM:(h + 1) * HEAD_DIM] for h in range(n)]


_NT = (((1,), (1,)), ((), ()))
_TN = (((0,), (0,)), ((), ()))
_TQ = 256


def _attn_probs(q, k):
    s = lax.dot_general(q, k, _NT, preferred_element_type=F32) * (HEAD_DIM ** -0.5)
    p = jnp.exp(s - jnp.max(s, axis=-1, keepdims=True))
    return p / jnp.sum(p, axis=-1, keepdims=True)


def _attn_fwd_call(q, k, v):
    n, t = q.shape[0], k.shape[0]
    tq = _pick(n, (_TQ, 128))

    def body(q_ref, k_ref, v_ref, o_ref):
        p = _attn_probs(q_ref[...].astype(BF16), k_ref[...].astype(BF16))
        o_ref[...] = jnp.dot(p.astype(BF16), v_ref[...].astype(BF16), preferred_element_type=F32)

    return pl.pallas_call(
        body, name="attn_f", grid=(ATTN_HEADS, n // tq),
        in_specs=[pl.BlockSpec((tq, HEAD_DIM), lambda h, i: (i, h)),
                  pl.BlockSpec((t, HEAD_DIM), lambda h, i: (0, h // ATTN_GROUP)),
                  pl.BlockSpec((t, HEAD_DIM), lambda h, i: (0, h // ATTN_GROUP))],
        out_specs=pl.BlockSpec((tq, HEAD_DIM), lambda h, i: (i, h)),
        out_shape=jax.ShapeDtypeStruct(q.shape, F32),
        compiler_params=pltpu.CompilerParams(dimension_semantics=("parallel", "parallel"),
                                             vmem_limit_bytes=VMEM_LIMIT),
    )(q, k, v)


def _attn_bwd_call(q, k, v, do):
    n, t = q.shape[0], k.shape[0]
    tq = _pick(n, (_TQ, 128))

    def body(q_ref, k_ref, v_ref, do_ref, dq_ref, dk_ref, dv_ref):
        @pl.when((pl.program_id(1) == 0) & (pl.program_id(2) == 0))
        def _():
            dk_ref[...] = jnp.zeros_like(dk_ref)
            dv_ref[...] = jnp.zeros_like(dv_ref)

        qb, kb, vb, dob = (r[...].astype(BF16) for r in (q_ref, k_ref, v_ref, do_ref))
        p = _attn_probs(qb, kb)
        dp = lax.dot_general(dob, vb, _NT, preferred_element_type=F32)
        ds = p * (dp - jnp.sum(p * dp, axis=-1, keepdims=True)) * (HEAD_DIM ** -0.5)
        dsb = ds.astype(BF16)
        dq_ref[...] = jnp.dot(dsb, kb, preferred_element_type=F32)
        dk_ref[...] += lax.dot_general(dsb, qb, _TN, preferred_element_type=F32)
        dv_ref[...] += lax.dot_general(p.astype(BF16), dob, _TN, preferred_element_type=F32)

    q_spec = pl.BlockSpec((tq, HEAD_DIM), lambda kh, g, i: (i, kh * ATTN_GROUP + g))
    kv_spec = pl.BlockSpec((t, HEAD_DIM), lambda kh, g, i: (0, kh))
    return pl.pallas_call(
        body, name="attn_b", grid=(ATTN_KV_HEADS, ATTN_GROUP, n // tq),
        in_specs=[q_spec, kv_spec, kv_spec, q_spec],
        out_specs=[q_spec, kv_spec, kv_spec],
        out_shape=(jax.ShapeDtypeStruct(q.shape, F32), jax.ShapeDtypeStruct(k.shape, F32),
                   jax.ShapeDtypeStruct(v.shape, F32)),
        compiler_params=pltpu.CompilerParams(dimension_semantics=("parallel", "arbitrary", "arbitrary"),
                                             vmem_limit_bytes=VMEM_LIMIT),
    )(q, k, v, do)


@jax.custom_vjp
def attention(q, k, v):
    return _attn_fwd_call(q, k, v)


def _attention_fwd(q, k, v):
    return _attn_fwd_call(q, k, v), (q, k, v)


def _attention_bwd(res, do):
    return _attn_bwd_call(*res, do)


attention.defvjp(_attention_fwd, _attention_bwd)


_C = GDN_CHUNK


def _hdot(a, b):
    return jnp.dot(a, b, precision=lax.Precision.HIGH, preferred_element_type=F32)


def _each(fn, *lists):
    return [fn(*args) for args in zip(*lists)]


def _unit_lower_inverse(low, blockdiag):
    eye = (lax.broadcasted_iota(jnp.int32, (_C, _C), 0) == lax.broadcasted_iota(jnp.int32, (_C, _C), 1)).astype(F32)
    ld = _each(lambda a: a * blockdiag, low)
    lo = _each(lambda a, d: a - d, low, ld)
    l2 = _each(_hdot, ld, ld)
    l4 = _each(_hdot, l2, l2)
    l8 = _each(_hdot, l4, l4)
    td = _each(lambda d, a2: _hdot(eye - d, eye + a2), ld, l2)
    td = _each(lambda t, a4: _hdot(t, eye + a4), td, l4)
    td = _each(lambda t, a8: _hdot(t, eye + a8), td, l8)
    nn = _each(_hdot, td, lo)
    n2 = _each(_hdot, nn, nn)
    out = _each(lambda n, m2: _hdot(eye - n, eye + m2), nn, n2)
    return _each(_hdot, out, td)


def _gdn_chunks(heads, blockdiag):
    q, k, v, b_b, be_b, e_b, kd_b, m1, dec, gl, s = (list(col) for col in zip(*heads))
    f32dot = lambda a, b: jnp.dot(a, b, preferred_element_type=F32)
    nt = lambda a, b: lax.dot_general(a, b, _NT, preferred_element_type=F32)
    kk = _each(nt, k, k)
    t_inv = _unit_lower_inverse(_each(lambda m, a: m * a, m1, kk), blockdiag)
    u = _each(lambda t, b, x: _hdot(t, b * x), t_inv, b_b, v)
    w = _each(lambda t, b, x: _hdot(t, b * x), t_inv, be_b, k)
    delta = _each(lambda uu, ww, ss: uu - f32dot(ww, ss), u, w, s)
    p = _each(lambda d, qq, kx: d * nt(qq, kx), dec, q, k)
    o = _each(lambda qq, e, ss, pp, dd: f32dot(qq * e, ss) + f32dot(pp, dd), q, e_b, s, p, delta)
    s_new = _each(lambda g, ss, kx, kd, dd: g * ss + lax.dot_general(kx * kd, dd, _TN, preferred_element_type=F32),
                  gl, s, k, kd_b, delta)
    return o, s_new


def _blockdiag_mask():
    r = lax.broadcasted_iota(jnp.int32, (_C, _C), 0) >> 4
    c = lax.broadcasted_iota(jnp.int32, (_C, _C), 1) >> 4
    return (r == c).astype(F32)


def _gdn_specs(nc, ncc, reverse, backward):
    def ch(s):
        s = nc - 1 - s if backward else s
        return jnp.where(s < ncc, ncc - 1 - s, nc + ncc - 1 - s) if reverse else s

    tok = pl.BlockSpec((_C, 3 * GDN_W), lambda s: (ch(s), 0))
    out = pl.BlockSpec((_C, GDN_W), lambda s: (ch(s), 0))
    per_tok = pl.BlockSpec((GDN_HEADS, _C, HEAD_DIM), lambda s: (0, ch(s), 0))
    mat = pl.BlockSpec((GDN_HEADS, None, _C, _C), lambda s: (0, ch(s), 0, 0))
    row = pl.BlockSpec((GDN_HEADS, None, 1, HEAD_DIM), lambda s: (0, ch(s), 0, 0))
    state = pl.BlockSpec((GDN_HEADS, None, HEAD_DIM, HEAD_DIM), lambda s: (0, ch(s), 0, 0))
    return tok, out, per_tok, mat, row, state


def _head_cols(h, part):
    return slice((part * GDN_HEADS + h) * HEAD_DIM, (part * GDN_HEADS + h + 1) * HEAD_DIM)


def _gdn_fwd_call(name, reverse, ncc, qkv, b_b, be_b, e_b, kd_b, m1, dec, gl):
    t = qkv.shape[0]
    nc = t // _C
    tok, out, per_tok, mat, row, state = _gdn_specs(nc, ncc, reverse, False)

    def body(qkv_ref, b_ref, be_ref, e_ref, kd_ref, m1_ref, dec_ref, gl_ref, o_ref, sall_ref, s_ref):
        @pl.when(pl.program_id(0) == 0)
        def _():
            s_ref[...] = jnp.zeros_like(s_ref)

        bd = _blockdiag_mask()
        ins = [[qkv_ref[:, _head_cols(h, 0)], qkv_ref[:, _head_cols(h, 1)], qkv_ref[:, _head_cols(h, 2)],
                b_ref[h], be_ref[h], e_ref[h], kd_ref[h], m1_ref[h], dec_ref[h], gl_ref[h], s_ref[h]]
               for h in range(GDN_HEADS)]
        o, s_new = _gdn_chunks(ins, bd)
        for h in range(GDN_HEADS):
            sall_ref[h] = ins[h][10]
            o_ref[:, _head_cols(h, 0)] = o[h]
            s_ref[h] = s_new[h]

    return pl.pallas_call(
        body, name=name + "_f", grid=(nc,),
        in_specs=[tok, per_tok, per_tok, per_tok, per_tok, mat, mat, row],
        out_specs=[out, state],
        out_shape=(jax.ShapeDtypeStruct((t, GDN_W), F32),
                   jax.ShapeDtypeStruct((GDN_HEADS, nc, HEAD_DIM, HEAD_DIM), F32)),
        scratch_shapes=[pltpu.VMEM((GDN_HEADS, HEAD_DIM, HEAD_DIM), F32)],
        compiler_params=pltpu.CompilerParams(dimension_semantics=("arbitrary",), vmem_limit_bytes=VMEM_LIMIT),
    )(qkv, b_b, be_b, e_b, kd_b, m1, dec, gl)


def _gdn_bwd_call(name, reverse, ncc, qkv, b_b, be_b, e_b, kd_b, m1, dec, gl, sall, do):
    t = qkv.shape[0]
    nc = t // _C
    tok, out, per_tok, mat, row, state = _gdn_specs(nc, ncc, reverse, True)

    def body(qkv_ref, b_ref, be_ref, e_ref, kd_ref, m1_ref, dec_ref, gl_ref, sall_ref, do_ref,
             dqkv_ref, db_ref, dbe_ref, de_ref, dkd_ref, dm1_ref, ddec_ref, dgl_ref, ds_ref):
        @pl.when(pl.program_id(0) == 0)
        def _():
            ds_ref[...] = jnp.zeros_like(ds_ref)

        bd = _blockdiag_mask()
        ins = [[qkv_ref[:, _head_cols(h, 0)], qkv_ref[:, _head_cols(h, 1)], qkv_ref[:, _head_cols(h, 2)],
                b_ref[h], be_ref[h], e_ref[h], kd_ref[h], m1_ref[h], dec_ref[h], gl_ref[h], sall_ref[h]]
               for h in range(GDN_HEADS)]
        _, vjp = jax.vjp(lambda hs: _gdn_chunks(hs, bd), ins)
        (all_grads,) = vjp(([do_ref[:, _head_cols(h, 0)] for h in range(GDN_HEADS)],
                            [ds_ref[h] for h in range(GDN_HEADS)]))
        for h, grads in enumerate(all_grads):
            for part in range(3):
                dqkv_ref[:, _head_cols(h, part)] = grads[part]
            for ref, g in zip((db_ref, dbe_ref, de_ref, dkd_ref, dm1_ref, ddec_ref, dgl_ref), grads[3:10]):
                ref[h] = g
            ds_ref[h] = grads[10]

    shp = lambda a: jax.ShapeDtypeStruct(a.shape, F32)
    return pl.pallas_call(
        body, name=name + "_b", grid=(nc,),
        in_specs=[tok, per_tok, per_tok, per_tok, per_tok, mat, mat, row, state, out],
        out_specs=[tok, per_tok, per_tok, per_tok, per_tok, mat, mat, row],
        out_shape=(shp(qkv), shp(b_b), shp(be_b), shp(e_b), shp(kd_b), shp(m1), shp(dec), shp(gl)),
        scratch_shapes=[pltpu.VMEM((GDN_HEADS, HEAD_DIM, HEAD_DIM), F32)],
        compiler_params=pltpu.CompilerParams(dimension_semantics=("arbitrary",), vmem_limit_bytes=VMEM_LIMIT),
    )(qkv, b_b, be_b, e_b, kd_b, m1, dec, gl, sall, do)


@functools.partial(jax.custom_vjp, nondiff_argnums=(0, 1, 2))
def gdn_scan(name, reverse, ncc, qkv, b_b, be_b, e_b, kd_b, m1, dec, gl):
    return _gdn_fwd_call(name, reverse, ncc, qkv, b_b, be_b, e_b, kd_b, m1, dec, gl)[0]


def _gdn_scan_fwd(name, reverse, ncc, *args):
    o, sall = _gdn_fwd_call(name, reverse, ncc, *args)
    return o, (args, sall)


def _gdn_scan_bwd(name, reverse, ncc, res, do):
    args, sall = res
    return _gdn_bwd_call(name, reverse, ncc, *args, sall, do)


gdn_scan.defvjp(_gdn_scan_fwd, _gdn_scan_bwd)


def _rope_tables(n, cl):
    t = np.arange(n)
    inv_freq = (ROPE_THETA ** (-np.arange(0, HEAD_DIM // 2, 2, dtype=np.float32) / (HEAD_DIM // 2))).astype(np.float32)
    ang_r = (t // GRID_W).astype(np.float32)[:, None] * inv_freq
    ang_c = (t % GRID_W).astype(np.float32)[:, None] * inv_freq
    cos = np.concatenate([np.cos(ang_r), np.cos(ang_r), np.cos(ang_c), np.cos(ang_c)], axis=1)
    sin = np.concatenate([-np.sin(ang_r), np.sin(ang_r), -np.sin(ang_c), np.sin(ang_c)], axis=1)
    cos_all = np.concatenate([np.ones((cl, HEAD_DIM), np.float32), cos], axis=0)
    sin_all = np.concatenate([np.zeros((cl, HEAD_DIM), np.float32), sin], axis=0)
    j = np.arange(HEAD_DIM)
    src = np.where((j % 64) < 32, j + 32, j - 32)
    perm = np.zeros((HEAD_DIM, HEAD_DIM), np.float32)
    perm[src, j] = 1.0
    return (jnp.asarray(cos.astype(np.float32)), jnp.asarray(sin.astype(np.float32)),
            jnp.asarray(cos_all), jnp.asarray(sin_all), jnp.asarray(perm))


def _shift_rows(a, cl):
    z = jnp.zeros((1, a.shape[1]), a.dtype)
    parts = [a[:cl], a[cl:]] if cl else [a]
    prev = jnp.concatenate([jnp.concatenate([z, p[:-1]], axis=0) for p in parts], axis=0)
    nxt = jnp.concatenate([jnp.concatenate([p[1:], z], axis=0) for p in parts], axis=0)
    return prev, nxt


def _gdn_factors(log_a, beta, reverse):
    t = log_a.shape[0]
    nc = t // _C
    la = log_a.reshape(nc, _C, GDN_HEADS).transpose(2, 0, 1)
    be = beta.reshape(nc, _C, GDN_HEADS).transpose(2, 0, 1)
    gam = lax.cumsum(la, axis=2, reverse=reverse)
    idx = jnp.arange(_C)
    incl = (idx[:, None] <= idx[None, :]) if reverse else (idx[:, None] >= idx[None, :])
    strict = (idx[:, None] < idx[None, :]) if reverse else (idx[:, None] > idx[None, :])
    dec = jnp.exp(jnp.where(incl, gam[..., :, None] - gam[..., None, :], -jnp.inf))
    m1 = jnp.where(strict, be[..., :, None] * dec, 0.0)
    e = jnp.exp(gam)
    g_last = gam[..., :1] if reverse else gam[..., -1:]
    lanes = lambda a: jnp.broadcast_to(a.reshape(GDN_HEADS, t, 1), (GDN_HEADS, t, HEAD_DIM))
    gl = jnp.broadcast_to(jnp.exp(g_last)[..., None], (GDN_HEADS, nc, 1, HEAD_DIM))
    return lanes(be), lanes(be * e), lanes(e), lanes(jnp.exp(g_last - gam)), m1, dec, gl


def local_loss(x, wz, wb, ws, c, ctx, target):
    n, cl = x.shape[0], ctx.shape[0]
    cos_q, sin_q, cos_k, sin_k, perm = _rope_tables(n, cl)

    sc_in = jnp.concatenate([jax.nn.silu(c), jax.nn.silu(ws["c_ctx"])[None, :], jnp.zeros((14, D_MODEL), F32)], axis=0)
    mod = pmm(sc_in, wb["w_mod"], wz["w_mod"], "mm_mod") + ws["b_mod"]
    sh1, sc1, g1, sh2, sc2, g2 = [mod[0:1, i * D_MODEL:(i + 1) * D_MODEL] for i in range(6)]
    csh1, csc1 = mod[1:2, 0:D_MODEL], mod[1:2, D_MODEL:2 * D_MODEL]

    def norm_mod(a, sh, sc):
        return (_rms(a) * (1.0 + sc) + sh,)

    (hx,) = rowop(norm_mod, "normmod_x", (x,), (sh1, sc1))
    (hc,) = rowop(norm_mod, "normmod_c", (ctx,), (csh1, csc1))
    h_all = jnp.concatenate([hc, hx], axis=0)
    p_main = pmm_t(h_all, wb["w_in_main"], wz["w_in_main"], "mm_in")
    p_small = pmm_t(h_all, wb["w_in_small"], wz["w_in_small"], "mm_ins")
    ak, av, qkv, aq, z, gate = jnp.split(p_main, [KV_W, 2 * KV_W, SMALL_AT, SMALL_AT + Q_W, SMALL_AT + Q_W + GDN_W],
                                         axis=1)
    db, da = p_small[:, :2 * GDN_HEADS], p_small[:, 2 * GDN_HEADS:4 * GDN_HEADS]

    def qk_prep(nh):
        def fn(a, w, cos, sin, pm):
            outs = []
            for ah in _heads(a, nh):
                y = _rms(ah) * w
                outs.append(y * cos + _hdot(y, pm) * sin)
            return (jnp.concatenate(outs, axis=1),)
        return fn

    (q_x,) = rowop(qk_prep(ATTN_HEADS), "q_prep", (aq[cl:],), (ws["q_norm_w"],), (cos_q, sin_q), (perm,))
    (k_all,) = rowop(qk_prep(ATTN_KV_HEADS), "k_prep", (ak,), (ws["k_norm_w"],), (cos_k, sin_k), (perm,))
    attn_x = attention(q_x, k_all, av)

    qkv_prev, qkv_next = _shift_rows(qkv, cl)
    cw = ws["conv_qkv_w"]

    def gdn_prep(a, ap, an, w0, w1, w2):
        s = jax.nn.silu(ap * w0 + a * w1 + an * w2)
        outs = []
        for i, sh in enumerate(_heads(s, 3 * GDN_HEADS)):
            if i < 2 * GDN_HEADS:
                sh = sh * lax.rsqrt(jnp.sum(sh * sh, axis=-1, keepdims=True) + NORM_EPS)
                if i < GDN_HEADS:
                    sh = sh * (HEAD_DIM ** -0.5)
            outs.append(sh)
        return (jnp.concatenate(outs, axis=1),)

    (qkvn,) = rowop(gdn_prep, "gdn_prep", (qkv, qkv_prev, qkv_next), (cw[0:1], cw[1:2], cw[2:3]), tr=128)
    beta = jax.nn.sigmoid(db).reshape(-1, 2, GDN_HEADS)
    log_a = -jnp.exp(ws["a_log"])[None] * jax.nn.softplus(da.reshape(-1, 2, GDN_HEADS) + ws["dt_bias"][None])
    o_fwd = gdn_scan("gdn_d0", False, cl // _C, qkvn, *_gdn_factors(log_a[:, 0], beta[:, 0], False))
    o_bwd = gdn_scan("gdn_d1", True, cl // _C, qkvn, *_gdn_factors(log_a[:, 1], beta[:, 1], True))
    o_x = o_fwd[cl:] + o_bwd[cl:]

    def gdn_out(o, zz, w):
        outs = [_rms(oh) * w * jax.nn.silu(zh) for oh, zh in zip(_heads(o, GDN_HEADS), _heads(zz, GDN_HEADS))]
        return (jnp.concatenate(outs, axis=1),)

    (gdn_x,) = rowop(gdn_out, "gdn_out", (o_x, z[cl:]), (ws["gdn_norm_w"],))

    pa = pmm(attn_x, wb["w_pa"], wz["w_pa"], "mm_pa")
    pd = pmm(gdn_x, wb["w_pd"], wz["w_pd"], "mm_pd")

    def merge(a, d, g):
        return (jax.nn.sigmoid(g[:, :D_MODEL]) * a + jax.nn.sigmoid(g[:, D_MODEL:]) * d,)

    (y,) = rowop(merge, "merge", (pa, pd, gate[cl:]))
    mo = pmm(y, wb["w_out"], wz["w_out"], "mm_out")

    def res_norm_mod(xx, m, g, sh, sc):
        x1 = xx + g * m
        return x1, _rms(x1) * (1.0 + sc) + sh

    x1, h2 = rowop(res_norm_mod, "res1", (x, mo), (g1, sh2, sc2))
    up = pmm(h2, wb["w_up"], wz["w_up"], "mm_up")
    up_prev, up_next = _shift_rows(up, 0)
    fw = ws["ffn_conv_w"]

    def ffn_act(a, ap, an, w0, w1, w2, b):
        u = ap * w0 + a * w1 + an * w2 + b
        return (jax.nn.silu(u[:, :D_FF]) * u[:, D_FF:],)

    (act,) = rowop(ffn_act, "ffn_act", (up, up_prev, up_next), (fw[0:1], fw[1:2], fw[2:3], ws["ffn_conv_b"]), tr=128)
    dn = pmm(act, wb["w_down"], wz["w_down"], "mm_down")

    def head(xx, m, g, w, tgt):
        yy = _rms(xx + g * m) * w
        err = (yy - tgt) ** 2
        return (jnp.broadcast_to(0.5 * jnp.mean(err, axis=-1, keepdims=True), (xx.shape[0], HEAD_DIM)),)

    (row_loss,) = rowop(head, "head", (x1, dn), (g2, ws["final_norm_w"][None, :]), (target,))
    return jnp.sum(row_loss[:, 0])


_HBM = pl.BlockSpec(memory_space=pltpu.HBM)


def _chip_peers():
    x, y = lax.axis_index("x"), lax.axis_index("y")
    return [(1 - x, y), (x, 1 - y), (1 - x, 1 - y)]


_SPLIT_COLS = ("w_in",)


def _half_of(view, nm, idx, lead=0):
    r, cdim = view.shape[-2:]
    pre = (slice(None),) * lead
    if nm in _SPLIT_COLS:
        return view.at[pre + (slice(None), pl.ds(pl.multiple_of(idx * (cdim // 2), 128), cdim // 2))]
    return view.at[pre + (pl.ds(pl.multiple_of(idx * (r // 2), 16), r // 2), slice(None))]


def _remote(src, dst, send_sem, recv_sem, dev):
    return pltpu.make_async_remote_copy(src_ref=src, dst_ref=dst, send_sem=send_sem, recv_sem=recv_sem,
                                        device_id=dev, device_id_type=MESH)


def _hbm_call(body, name, ins, out_shape, n_sems, in_place=False):
    names = tuple(ins)
    return dict(zip(names, pl.pallas_call(
        body, name=name, in_specs=[_HBM] * len(names), out_specs=[_HBM] * len(names),
        out_shape=[out_shape(nm, ins[nm]) for nm in names],
        scratch_shapes=[pltpu.SemaphoreType.DMA((k,)) for k in n_sems],
        input_output_aliases={i: i for i in range(len(names))} if in_place else {},
    )(*[ins[nm] for nm in names])))


def all_gather_chips(shards):
    names = tuple(shards)
    n = len(names)

    def body(*refs):
        ins, outs = dict(zip(names, refs[:n])), dict(zip(names, refs[n:2 * n]))
        ici_send, ici_recv, d2d_send, d2d_recv, own_send, own_recv = refs[2 * n:]
        x, y, c = lax.axis_index("x"), lax.axis_index("y"), lax.axis_index("c")
        me, sib = 2 * x + y, (x, y, 1 - c)
        own = [_remote(ins[nm], outs[nm].at[me], own_send.at[i], own_recv.at[i], sib) for i, nm in enumerate(names)]
        for cp in own:
            cp.start()
        sends = []
        for k, (px, py) in enumerate(_chip_peers()):
            for i, nm in enumerate(names):
                cp = _remote(_half_of(ins[nm], nm, c), _half_of(outs[nm].at[me], nm, c), ici_send.at[k * n + i],
                             ici_recv.at[k * n + i], (px, py, c))
                cp.start()
                sends.append(cp)
        for k, (px, py) in enumerate(_chip_peers()):
            for i, nm in enumerate(names):
                landed = _half_of(outs[nm].at[2 * px + py], nm, c)
                _remote(landed, landed, ici_send.at[k * n + i], ici_recv.at[k * n + i], (px, py, c)).wait_recv()
                fw = _remote(landed, landed, d2d_send.at[k * n + i], d2d_recv.at[k * n + i], sib)
                fw.start()
                sends.append(fw)
        for k, (px, py) in enumerate(_chip_peers()):
            for i, nm in enumerate(names):
                other = _half_of(outs[nm].at[2 * px + py], nm, 1 - c)
                _remote(other, other, d2d_send.at[k * n + i], d2d_recv.at[k * n + i], sib).wait_recv()
        for cp in sends:
            cp.wait_send()
        for cp in own:
            cp.wait()

    return _hbm_call(body, "ag_weights", shards, lambda nm, a: jax.ShapeDtypeStruct((N_CHIPS,) + a.shape, a.dtype),
                     (3 * n, 3 * n, 3 * n, 3 * n, n, n))


def sibling_halves(blocks):
    names = tuple(blocks)

    def body(*refs):
        n = len(names)
        ins, outs = dict(zip(names, refs[:n])), dict(zip(names, refs[n:2 * n]))
        send_sems, recv_sems = refs[2 * n:]
        x, y, c = lax.axis_index("x"), lax.axis_index("y"), lax.axis_index("c")
        cps = [_remote(_half_of(ins[nm], nm, 1 - c, lead=1), outs[nm], send_sems.at[i], recv_sems.at[i], (x, y, 1 - c))
               for i, nm in enumerate(names)]
        for cp in cps:
            cp.start()
        for cp in cps:
            cp.wait()

    def half_shape(nm, a):
        r, cdim = a.shape[-2:]
        return jax.ShapeDtypeStruct((N_CHIPS, r, cdim // 2) if nm in _SPLIT_COLS else (N_CHIPS, r // 2, cdim), a.dtype)

    return _hbm_call(body, "rs_sibling", blocks, half_shape, (len(names), len(names)))


def scatter_halves(blocks):
    names = tuple(blocks)
    n = len(names)

    def body(*refs):
        ins, outs = dict(zip(names, refs[:n])), dict(zip(names, refs[n:2 * n]))
        send_sems, recv_sems = refs[2 * n:]
        c = lax.axis_index("c")
        cps = [_remote(ins[nm].at[2 * px + py], outs[nm].at[k], send_sems.at[k * n + i], recv_sems.at[k * n + i],
                       (px, py, c))
               for k, (px, py) in enumerate(_chip_peers()) for i, nm in enumerate(names)]
        for cp in cps:
            cp.start()
        for cp in cps:
            cp.wait_recv()
        for cp in cps:
            cp.wait_send()

    return _hbm_call(body, "rs_grads", blocks, lambda nm, a: jax.ShapeDtypeStruct((3,) + a.shape[1:], a.dtype),
                     (3 * n, 3 * n))


def sibling_assemble(arrays):
    names = tuple(arrays)

    def body(*refs):
        n = len(names)
        ins, outs = dict(zip(names, refs[:n])), dict(zip(names, refs[n:2 * n]))
        send_sems, recv_sems = refs[2 * n:]
        x, y, c = lax.axis_index("x"), lax.axis_index("y"), lax.axis_index("c")
        cps = [_remote(_half_of(ins[nm], nm, c), _half_of(outs[nm], nm, c), send_sems.at[i], recv_sems.at[i],
                       (x, y, 1 - c)) for i, nm in enumerate(names)]
        for cp in cps:
            cp.start()
        for i, nm in enumerate(names):
            other = _half_of(outs[nm], nm, 1 - c)
            _remote(other, other, send_sems.at[i], recv_sems.at[i], (x, y, 1 - c)).wait_recv()
        for cp in cps:
            cp.wait_send()

    return _hbm_call(body, "rs_assemble", arrays, lambda nm, a: jax.ShapeDtypeStruct(a.shape, a.dtype),
                     (len(names), len(names)), in_place=True)


def all_reduce_small(v):
    def body(v_ref, tot_ref, gath_ref, send_sems, recv_sems):
        x, y, c = lax.axis_index("x"), lax.axis_index("y"), lax.axis_index("c")
        me = 4 * x + 2 * y + c
        gath_ref[me] = v_ref[...]

        def peer(k):
            m = k + 1
            return (x ^ (m >> 2 & 1), y ^ (m >> 1 & 1), c ^ (m & 1))

        sends = [pltpu.make_async_remote_copy(src_ref=v_ref, dst_ref=gath_ref.at[me], send_sem=send_sems.at[k],
                                              recv_sem=recv_sems.at[k], device_id=peer(k), device_id_type=MESH)
                 for k in range(N_DEV - 1)]
        for cp in sends:
            cp.start()
        for k in range(N_DEV - 1):
            px, py, pc = peer(k)
            pltpu.make_async_remote_copy(src_ref=v_ref, dst_ref=gath_ref.at[4 * px + 2 * py + pc],
                                         send_sem=send_sems.at[k], recv_sem=recv_sems.at[k], device_id=peer(k),
                                         device_id_type=MESH).wait_recv()
        for cp in sends:
            cp.wait_send()
        acc = gath_ref[0]
        for d in range(1, N_DEV):
            acc = acc + gath_ref[d]
        tot_ref[...] = acc

    vm = pl.BlockSpec(memory_space=pltpu.VMEM)
    return pl.pallas_call(
        body, name="ar_small", in_specs=[vm], out_specs=[vm, vm],
        out_shape=(jax.ShapeDtypeStruct(v.shape, v.dtype), jax.ShapeDtypeStruct((N_DEV,) + v.shape, v.dtype)),
        scratch_shapes=[pltpu.SemaphoreType.DMA((N_DEV - 1,)), pltpu.SemaphoreType.DMA((N_DEV - 1,))],
    )(v)[0]


def _elementwise(fn, name, ins, n_out, out_dtype=F32):
    r, cdim = ins[0].shape
    tr = _pick(r, tuple(p for p in (488, 256, 128, 104, 64, 32, 16, 8) if p * cdim * 4 <= 2 * 1024 * 1024))
    spec = pl.BlockSpec((tr, cdim), lambda i: (i, 0))

    def body(*refs):
        res = fn(*[ref[...] for ref in refs[:len(ins)]])
        for o_ref, v in zip(refs[len(ins):], res):
            o_ref[...] = v

    return pl.pallas_call(
        body, name=name, grid=(r // tr,), in_specs=[spec] * len(ins), out_specs=[spec] * n_out,
        out_shape=tuple(jax.ShapeDtypeStruct((r, cdim), out_dtype) for _ in range(n_out)),
        compiler_params=pltpu.CompilerParams(dimension_semantics=("parallel",), vmem_limit_bytes=VMEM_LIMIT),
    )(*ins)


def _half_block_specs(nm, shard_shape):
    r, cdim = shard_shape
    if nm in _SPLIT_COLS:
        return (None, r, cdim // 2), (lambda j, c: (j, 0, c))
    return (None, r // 2, cdim), (lambda j, c: (j, c, 0))


def _presum(nm, sel, g32, a):
    blk, at = _half_block_specs(nm, g32.shape[1:])

    def body(s_ref, g_ref, a_ref, o_ref):
        del s_ref
        o_ref[...] = (g_ref[...] + a_ref[...]).astype(BF16)

    return pl.pallas_call(
        body, name="rs_presum_" + nm,
        grid_spec=pltpu.PrefetchScalarGridSpec(
            num_scalar_prefetch=1, grid=(N_CHIPS,),
            in_specs=[pl.BlockSpec(blk, lambda j, s: at(j, s[0])), pl.BlockSpec(blk, lambda j, s: (j, 0, 0))],
            out_specs=pl.BlockSpec(blk, lambda j, s: (j, 0, 0))),
        out_shape=jax.ShapeDtypeStruct(a.shape, BF16),
        compiler_params=pltpu.CompilerParams(dimension_semantics=("parallel",), vmem_limit_bytes=VMEM_LIMIT),
    )(sel, g32, a)


def _finalsum(nm, sel, g32, a, got):
    blk, at = _half_block_specs(nm, g32.shape[1:])

    def body(s_ref, g_ref, a_ref, r_ref, o_ref):
        del s_ref
        acc = g_ref[...] + a_ref[...]
        for k in range(3):
            acc = acc + r_ref[k].astype(F32)
        o_ref[...] = acc

    return pl.pallas_call(
        body, name="rs_final_" + nm,
        grid_spec=pltpu.PrefetchScalarGridSpec(
            num_scalar_prefetch=1, grid=(1,),
            in_specs=[pl.BlockSpec(blk, lambda i, s: at(s[1], s[0])), pl.BlockSpec(blk, lambda i, s: (s[1], 0, 0)),
                      pl.BlockSpec(got.shape, lambda i, s: (0, 0, 0))],
            out_specs=pl.BlockSpec(blk[1:], lambda i, s: at(0, s[0])[1:])),
        out_shape=jax.ShapeDtypeStruct(g32.shape[1:], F32),
        compiler_params=pltpu.CompilerParams(dimension_semantics=("arbitrary",), vmem_limit_bytes=VMEM_LIMIT),
    )(sel, g32, a, got)


def _adamw(w, g, m, v, name):
    shape = w.shape
    to2 = lambda a: a.reshape(-1, shape[-1])

    def fn(w_, g_, m_, v_):
        m_new = ADAM_B1 * m_ + (1.0 - ADAM_B1) * g_
        v_new = ADAM_B2 * v_ + (1.0 - ADAM_B2) * (g_ * g_)
        m_hat = m_new / (1.0 - ADAM_B1 ** ADAM_STEP)
        v_hat = v_new / (1.0 - ADAM_B2 ** ADAM_STEP)
        delta = -ADAM_LR * (m_hat / (jnp.sqrt(v_hat) + ADAM_EPS) + ADAM_WD * w_)
        return delta, m_new, v_new

    outs = _elementwise(fn, name, [to2(a) for a in (w, g, m, v)], 3)
    return tuple(o.reshape(shape) for o in outs)


_BIG = ("w_mod", "w_in", "w_pa", "w_pd", "w_out", "w_up", "w_down")
_COL_SHARDED = ("w_mod", "w_up")
_FULL_SHAPE = {"w_mod": (D_MODEL, MOD_W), "w_in": (IN_COLS, D_MODEL), "w_pa": (Q_W, D_MODEL), "w_pd": (GDN_W, D_MODEL),
               "w_out": (D_MODEL, D_MODEL), "w_up": (D_MODEL, 2 * D_FF), "w_down": (D_FF, D_MODEL)}


def _shard_shape(name):
    r, cdim = _FULL_SHAPE[name]
    return (r, cdim // N_CHIPS) if name in _COL_SHARDED else (r // N_CHIPS, cdim)


_CONV_ELEMS = 2 * (3 * CONV_W // N_CHIPS + 3 * 2 * D_FF // N_CHIPS)
_CONV_ROWS = 32


def _blocks_of_full(name, full):
    r, cdim = _FULL_SHAPE[name]
    if name in _COL_SHARDED:
        return full.reshape(r, N_CHIPS, cdim // N_CHIPS).transpose(1, 0, 2)
    return full.reshape(N_CHIPS, r // N_CHIPS, cdim)


def _full_of_blocks(name, blocks):
    r, cdim = _FULL_SHAPE[name]
    if name in _COL_SHARDED:
        return blocks.transpose(1, 0, 2).reshape(r, cdim)
    return blocks.reshape(r, cdim)


def _w_in_regroup(w_in_t):
    main = jnp.concatenate([w_in_t[:SMALL_AT], w_in_t[SMALL_AT + 4 * GDN_HEADS:]], axis=0)
    small = jnp.pad(w_in_t[SMALL_AT:SMALL_AT + 4 * GDN_HEADS], ((0, HEAD_DIM - 4 * GDN_HEADS), (0, 0)))
    return main, small


def _w_in_ungroup(main, small):
    return jnp.concatenate([main[:SMALL_AT], small[:4 * GDN_HEADS], main[SMALL_AT:]], axis=0)


_SMALL = ("c_ctx", "b_mod", "q_norm_w", "k_norm_w", "conv_qkv_w", "a_log", "dt_bias", "gdn_norm_w", "ffn_conv_w",
          "ffn_conv_b", "final_norm_w")


def _pack_small(tree, rows):
    flat = jnp.concatenate([tree[nm].reshape(-1) for nm in _SMALL])
    return jnp.pad(flat, (0, rows * 128 - flat.shape[0])).reshape(rows, 128)


def _unpack_small(packed, like):
    flat, out, off = packed.reshape(-1), {}, 0
    for nm in _SMALL:
        size = int(np.prod(like[nm].shape))
        out[nm] = flat[off:off + size].reshape(like[nm].shape)
        off += size
    return out


def kernel(x, c, ctx, c_ctx, w_mod, b_mod, w_in, q_norm_w, k_norm_w, conv_qkv_w, a_log, dt_bias, gdn_norm_w, w_pa, w_pd, w_out, w_up, ffn_conv_w, ffn_conv_b, w_down, final_norm_w, loss_target, m_c_ctx, m_w_mod, m_b_mod, m_w_in, m_q_norm_w, m_k_norm_w, m_conv_qkv_w, m_a_log, m_dt_bias, m_gdn_norm_w, m_w_pa, m_w_pd, m_w_out, m_w_up, m_ffn_conv_w, m_ffn_conv_b, m_w_down, m_final_norm_w, v_c_ctx, v_w_mod, v_b_mod, v_w_in, v_q_norm_w, v_k_norm_w, v_conv_qkv_w, v_a_log, v_dt_bias, v_gdn_norm_w, v_w_pa, v_w_pd, v_w_out, v_w_up, v_ffn_conv_w, v_ffn_conv_b, v_w_down, v_final_norm_w):
    names = ("c_ctx", "w_mod", "b_mod", "w_in", "q_norm_w", "k_norm_w", "conv_qkv_w", "a_log", "dt_bias", "gdn_norm_w",
             "w_pa", "w_pd", "w_out", "w_up", "ffn_conv_w", "ffn_conv_b", "w_down", "final_norm_w")
    w_sh = dict(c_ctx=c_ctx, w_mod=w_mod, b_mod=b_mod, w_in=w_in, q_norm_w=q_norm_w, k_norm_w=k_norm_w,
                conv_qkv_w=conv_qkv_w, a_log=a_log, dt_bias=dt_bias, gdn_norm_w=gdn_norm_w, w_pa=w_pa, w_pd=w_pd,
                w_out=w_out, w_up=w_up, ffn_conv_w=ffn_conv_w, ffn_conv_b=ffn_conv_b, w_down=w_down,
                final_norm_w=final_norm_w)
    m_sh = dict(c_ctx=m_c_ctx, w_mod=m_w_mod, b_mod=m_b_mod, w_in=m_w_in, q_norm_w=m_q_norm_w, k_norm_w=m_k_norm_w,
                conv_qkv_w=m_conv_qkv_w, a_log=m_a_log, dt_bias=m_dt_bias, gdn_norm_w=m_gdn_norm_w, w_pa=m_w_pa,
                w_pd=m_w_pd, w_out=m_w_out, w_up=m_w_up, ffn_conv_w=m_ffn_conv_w, ffn_conv_b=m_ffn_conv_b,
                w_down=m_w_down, final_norm_w=m_final_norm_w)
    v_sh = dict(c_ctx=v_c_ctx, w_mod=v_w_mod, b_mod=v_b_mod, w_in=v_w_in, q_norm_w=v_q_norm_w, k_norm_w=v_k_norm_w,
                conv_qkv_w=v_conv_qkv_w, a_log=v_a_log, dt_bias=v_dt_bias, gdn_norm_w=v_gdn_norm_w, w_pa=v_w_pa,
                w_pd=v_w_pd, w_out=v_w_out, w_up=v_w_up, ffn_conv_w=v_ffn_conv_w, ffn_conv_b=v_ffn_conv_b,
                w_down=v_w_down, final_norm_w=v_final_norm_w)
    chip = 2 * lax.axis_index("x") + lax.axis_index("y")

    conv_bits = jnp.concatenate([lax.bitcast_convert_type(w_sh[nm][0], BF16).reshape(-1)
                                 for nm in ("conv_qkv_w", "ffn_conv_w")])
    shards = {nm: w_sh[nm][0].astype(BF16).T if nm == "w_in" else w_sh[nm][0].astype(BF16) for nm in _BIG}
    shards["conv"] = jnp.pad(conv_bits, (0, _CONV_ROWS * D_MODEL - _CONV_ELEMS)).reshape(_CONV_ROWS, D_MODEL)
    gathered = all_gather_chips(shards)
    wb = {nm: _full_of_blocks(nm, gathered[nm]) for nm in _BIG}
    wb["w_in_main"], wb["w_in_small"] = _w_in_regroup(wb.pop("w_in"))
    conv_all = gathered["conv"].reshape(N_CHIPS, -1)[:, :_CONV_ELEMS]
    n_cq = 2 * 3 * CONV_W // N_CHIPS
    unbits = lambda a, w: lax.bitcast_convert_type(a.reshape(N_CHIPS, 3, w // N_CHIPS, 2), F32).transpose(1, 0, 2).reshape(3, w)
    ws = dict(c_ctx=c_ctx, b_mod=b_mod, q_norm_w=q_norm_w, k_norm_w=k_norm_w, a_log=a_log[0], dt_bias=dt_bias[0],
              gdn_norm_w=gdn_norm_w, ffn_conv_b=ffn_conv_b, final_norm_w=final_norm_w,
              conv_qkv_w=unbits(conv_all[:, :n_cq], CONV_W), ffn_conv_w=unbits(conv_all[:, n_cq:], 2 * D_FF))
    wz = {nm: jnp.zeros(a.shape, F32) for nm, a in wb.items()}

    loss_local, (gx, gz, gs) = jax.value_and_grad(local_loss, argnums=(0, 1, 3))(
        x[0], wz, wb, ws, c, ctx[0], loss_target[0])
    loss = lax.psum(loss_local, ("x", "y", "c"))

    gz["w_in"] = _w_in_ungroup(gz.pop("w_in_main"), gz.pop("w_in_small"))
    g32 = {nm: _blocks_of_full(nm, gz[nm]) for nm in _BIG}
    sel = jnp.stack([lax.axis_index("c"), chip]).astype(jnp.int32)
    theirs = sibling_halves(g32)
    got = scatter_halves({nm: _presum(nm, sel, g32[nm], theirs[nm]) for nm in _BIG})
    g_big = sibling_assemble({nm: _finalsum(nm, sel, g32[nm], theirs[nm], got[nm]) for nm in _BIG})

    gs["a_log"], gs["dt_bias"] = gs["a_log"][None], gs["dt_bias"][None]
    like = {nm: gs[nm] for nm in _SMALL}
    small_rows = -(-sum(int(np.prod(like[nm].shape)) for nm in _SMALL) // 1024) * 8
    g_small = _unpack_small(all_reduce_small(_pack_small(gs, small_rows)), like)
    for nm, width in (("conv_qkv_w", CONV_W), ("ffn_conv_w", 2 * D_FF)):
        g_small[nm] = lax.dynamic_slice_in_dim(g_small[nm], chip * (width // N_CHIPS), width // N_CHIPS, axis=1)[None]

    grads, deltas, new_m, new_v = {}, {}, {}, {}
    for nm in _BIG:
        g = g_big[nm].T if nm == "w_in" else g_big[nm]
        grads[nm] = g[None]
        deltas[nm], new_m[nm], new_v[nm] = (o[None] for o in _adamw(w_sh[nm][0], g, m_sh[nm][0], v_sh[nm][0],
                                                                     "adamw_" + nm))
    shard_like = {nm: w_sh[nm] for nm in _SMALL}
    rows_l = -(-sum(int(np.prod(shard_like[nm].shape)) for nm in _SMALL) // 1024) * 8
    g_l = _pack_small({nm: g_small[nm].reshape(w_sh[nm].shape) for nm in _SMALL}, rows_l)
    outs = _adamw(_pack_small(w_sh, rows_l), g_l, _pack_small(m_sh, rows_l), _pack_small(v_sh, rows_l), "adamw_small")
    grads.update(_unpack_small(g_l, shard_like))
    for tree, packed in zip((deltas, new_m, new_v), outs):
        tree.update(_unpack_small(packed, shard_like))

    return (loss, gx[None], *[grads[nm] for nm in names], *[deltas[nm] for nm in names],
            *[new_m[nm] for nm in names], *[new_v[nm] for nm in names])
```

```python
import functools
import math

import jax
import jax.numpy as jnp
import numpy as np
from jax import lax
from jax.experimental import pallas as pl
from jax.experimental.pallas import tpu as pltpu

F32 = jnp.float32
BF16 = jnp.bfloat16
HIGHEST = lax.Precision.HIGHEST
MESH = pl.DeviceIdType.MESH

D_MODEL = 1024
GRID_W = 64
ATTN_HEADS = 8
ATTN_KV_HEADS = 2
ATTN_GROUP = ATTN_HEADS // ATTN_KV_HEADS
HEAD_DIM = 128
ROPE_THETA = 10000.0
GDN_HEADS = 8
GDN_CHUNK = 64
D_FF = 2816
NORM_EPS = 1e-6
KV_W = ATTN_KV_HEADS * HEAD_DIM
Q_W = ATTN_HEADS * HEAD_DIM
GDN_W = GDN_HEADS * HEAD_DIM
CONV_W = 3 * GDN_W
MOD_W = 6 * D_MODEL
IN_COLS = 2 * KV_W + CONV_W + 4 * GDN_HEADS + Q_W + GDN_W + 2 * D_MODEL
IN_MAIN = IN_COLS - 4 * GDN_HEADS
SMALL_AT = 2 * KV_W + CONV_W
N_CHIPS = 4
N_DEV = 8

ADAM_LR = 0.001
ADAM_B1 = 0.9
ADAM_B2 = 0.999
ADAM_EPS = 1e-08
ADAM_WD = 0.01
ADAM_STEP = 10

VMEM_LIMIT = 48 * 1024 * 1024


def _pick(dim, prefs):
    for p in prefs:
        if p <= dim and dim % p == 0:
            return p
    return dim


_DIMS = {
    "nn": (((1,), (0,)), ((), ())),
    "nt": (((1,), (1,)), ((), ())),
    "tn": (((0,), (0,)), ((), ())),
}


def _matmul(a, b, mode, name):
    if mode == "nn":
        (m, k), (_, n) = a.shape, b.shape
    elif mode == "nt":
        (m, k), (n, _) = a.shape, b.shape
    else:
        (k, m), (_, n) = a.shape, b.shape
    tm = _pick(m, (512, 384, 256, 128))
    tn = _pick(n, (1536, 1408, 1024, 768, 512, 256, 128))
    tk = _pick(k, (1024, 1408, 768, 512, 256, 128))
    nk = k // tk
    if mode == "tn":
        a_spec = pl.BlockSpec((tk, tm), lambda i, j, l: (l, i))
    else:
        a_spec = pl.BlockSpec((tm, tk), lambda i, j, l: (i, l))
    if mode == "nt":
        b_spec = pl.BlockSpec((tn, tk), lambda i, j, l: (j, l))
    else:
        b_spec = pl.BlockSpec((tk, tn), lambda i, j, l: (l, j))
    dims = _DIMS[mode]

    def body(a_ref, b_ref, o_ref, acc_ref):
        l = pl.program_id(2)

        @pl.when(l == 0)
        def _():
            acc_ref[...] = jnp.zeros_like(acc_ref)

        acc_ref[...] += lax.dot_general(a_ref[...].astype(BF16), b_ref[...].astype(BF16), dims,
                                        preferred_element_type=F32)

        @pl.when(l == nk - 1)
        def _():
            o_ref[...] = acc_ref[...]

    return pl.pallas_call(
        body,
        name=name,
        grid=(m // tm, n // tn, nk),
        in_specs=[a_spec, b_spec],
        out_specs=pl.BlockSpec((tm, tn), lambda i, j, l: (i, j)),
        out_shape=jax.ShapeDtypeStruct((m, n), F32),
        scratch_shapes=[pltpu.VMEM((tm, tn), F32)],
        compiler_params=pltpu.CompilerParams(dimension_semantics=("parallel", "parallel", "arbitrary"),
                                             vmem_limit_bytes=VMEM_LIMIT),
    )(a, b)


@functools.partial(jax.custom_vjp, nondiff_argnums=(3,))
def pmm(a, w, wz, name):
    del wz
    return _matmul(a, w, "nn", name + "_f")


def _pmm_fwd(a, w, wz, name):
    del wz
    return _matmul(a, w, "nn", name + "_f"), (a, w)


def _pmm_bwd(name, res, g):
    a, w = res
    da = _matmul(g, w, "nt", name + "_da")
    if a.shape[0] < 128:
        pad = 128 - a.shape[0]
        at = jnp.pad(a.T, ((0, 0), (0, pad)))
        gp = jnp.pad(g, ((0, pad), (0, 0)))
        dw = _matmul(at, gp, "nn", name + "_dw")
    else:
        dw = _matmul(a, g, "tn", name + "_dw")
    return da, jnp.zeros_like(w), dw


pmm.defvjp(_pmm_fwd, _pmm_bwd)


@functools.partial(jax.custom_vjp, nondiff_argnums=(3,))
def pmm_t(a, wt, wtz, name):
    del wtz
    return _matmul(a, wt, "nt", name + "_f")


def _pmm_t_fwd(a, wt, wtz, name):
    del wtz
    return _matmul(a, wt, "nt", name + "_f"), (a, wt)


def _pmm_t_bwd(name, res, g):
    a, wt = res
    return _matmul(g, wt, "nn", name + "_da"), jnp.zeros_like(wt), _matmul(g, a, "tn", name + "_dw")


pmm_t.defvjp(_pmm_t_fwd, _pmm_t_bwd)


def rowop(fn, name, rows, bcs=(), crows=(), cbcs=(), tr=256):
    rows, bcs, crows, cbcs = tuple(rows), tuple(bcs), tuple(crows), tuple(cbcs)
    n_rows = rows[0].shape[0]
    tr = _pick(n_rows, (tr, 128, 64, 32, 16, 8))
    nr, nb, ncr, ncb = len(rows), len(bcs), len(crows), len(cbcs)
    n_in = nr + nb + ncr + ncb
    grid = (n_rows // tr,)

    def blk(arr):
        return jax.ShapeDtypeStruct((tr, arr.shape[1]), arr.dtype)

    def row_spec(arr):
        return pl.BlockSpec((tr, arr.shape[1]), lambda i: (i, 0))

    def bc_spec(arr):
        return pl.BlockSpec(arr.shape, lambda i: (0, 0))

    out_blk = jax.eval_shape(fn, *[blk(r) for r in rows], *bcs, *[blk(r) for r in crows], *cbcs)
    n_out = len(out_blk)
    out_shape = tuple(jax.ShapeDtypeStruct((n_rows, o.shape[1]), o.dtype) for o in out_blk)
    in_specs = ([row_spec(r) for r in rows] + [bc_spec(b) for b in bcs]
                + [row_spec(r) for r in crows] + [bc_spec(b) for b in cbcs])

    def order(vals):
        return vals

    def fwd_call(args):
        def body(*refs):
            vals = [r[...] for r in refs[:n_in]]
            res = fn(*order(vals))
            for o_ref, r in zip(refs[n_in:], res):
                o_ref[...] = r

        return pl.pallas_call(
            body, name=name + "_f", grid=grid, in_specs=in_specs,
            out_specs=[row_spec(o) for o in out_shape], out_shape=out_shape,
            compiler_params=pltpu.CompilerParams(dimension_semantics=("parallel",), vmem_limit_bytes=VMEM_LIMIT),
        )(*args)

    def bwd_call(args, cts):
        def body(*refs):
            vals = [r[...] for r in refs[:n_in]]
            ct_refs = refs[n_in:n_in + n_out]
            d_rows = refs[n_in + n_out:n_in + n_out + nr]
            d_bcs = refs[n_in + n_out + nr:]
            consts = vals[nr + nb:]
            _, vjp = jax.vjp(lambda *p: fn(*p, *consts), *vals[:nr + nb])
            grads = vjp(tuple(c[...] for c in ct_refs))
            for ref, g in zip(d_rows, grads[:nr]):
                ref[...] = g

            @pl.when(pl.program_id(0) == 0)
            def _():
                for ref in d_bcs:
                    ref[...] = jnp.zeros_like(ref)

            for ref, g in zip(d_bcs, grads[nr:]):
                ref[...] += g

        d_shape = tuple(jax.ShapeDtypeStruct(r.shape, r.dtype) for r in rows + bcs)
        return pl.pallas_call(
            body, name=name + "_b", grid=grid,
            in_specs=in_specs + [row_spec(o) for o in out_shape],
            out_specs=[row_spec(r) for r in rows] + [bc_spec(b) for b in bcs], out_shape=d_shape,
            compiler_params=pltpu.CompilerParams(dimension_semantics=("arbitrary",), vmem_limit_bytes=VMEM_LIMIT),
        )(*args, *cts)

    @jax.custom_vjp
    def op(diff, const):
        return fwd_call(diff + const)

    def op_fwd(diff, const):
        return fwd_call(diff + const), (diff, const)

    def op_bwd(res, cts):
        diff, const = res
        grads = bwd_call(diff + const, tuple(cts))
        return tuple(grads), tuple(jnp.zeros_like(c) for c in const)

    op.defvjp(op_fwd, op_bwd)
    return op(rows + bcs, crows + cbcs)


def colop(fn, name, arrays, uses, n_const, nblk, cw=128):
    arrays = tuple(arrays)
    n_diff = len(arrays) - n_const
    nd = sum(1 for u in uses if u[0] < n_diff)
    assert all(u[0] < n_diff for u in uses[:nd]) and all(u[0] >= n_diff for u in uses[nd:])

    def spec(u):
        return pl.BlockSpec((arrays[u[0]].shape[0], cw), lambda j, off=u[1]: (0, off + j))

    def out_spec(rows):
        return pl.BlockSpec((rows, cw), lambda j: (0, j))

    out_blk = jax.eval_shape(fn, *[jax.ShapeDtypeStruct((arrays[u[0]].shape[0], cw), arrays[u[0]].dtype)
                                   for u in uses])
    out_shape = tuple(jax.ShapeDtypeStruct((o.shape[0], nblk * cw), o.dtype) for o in out_blk)
    params = pltpu.CompilerParams(dimension_semantics=("parallel",), vmem_limit_bytes=VMEM_LIMIT)

    def fwd_call(arrs):
        def body(*refs):
            res = fn(*[r[...] for r in refs[:len(uses)]])
            for o_ref, r in zip(refs[len(uses):], res):
                o_ref[...] = r

        return pl.pallas_call(
            body, name=name + "_f", grid=(nblk,), in_specs=[spec(u) for u in uses],
            out_specs=[out_spec(o.shape[0]) for o in out_shape], out_shape=out_shape, compiler_params=params,
        )(*[arrs[u[0]] for u in uses])

    def bwd_call(arrs, cts):
        def body(*refs):
            vals = [r[...] for r in refs[:len(uses)]]
            ct_refs = refs[len(uses):len(uses) + len(out_shape)]
            _, vjp = jax.vjp(lambda *p: fn(*p, *vals[nd:]), *vals[:nd])
            for ref, g in zip(refs[len(uses) + len(out_shape):], vjp(tuple(c[...] for c in ct_refs))):
                ref[...] = g

        d_shape = tuple(jax.ShapeDtypeStruct((arrays[u[0]].shape[0], nblk * cw), F32) for u in uses[:nd])
        return pl.pallas_call(
            body, name=name + "_b", grid=(nblk,),
            in_specs=[spec(u) for u in uses] + [out_spec(o.shape[0]) for o in out_shape],
            out_specs=[out_spec(s.shape[0]) for s in d_shape], out_shape=d_shape, compiler_params=params,
        )(*[arrs[u[0]] for u in uses], *cts)

    @jax.custom_vjp
    def op(diff, const):
        return fwd_call(diff + const)

    def op_fwd(diff, const):
        return fwd_call(diff + const), (diff, const)

    def op_bwd(res, cts):
        diff, const = res
        d_uses = bwd_call(diff + const, tuple(cts))
        grads = []
        for i in range(n_diff):
            parts = sorted([(u[1], k) for k, u in enumerate(uses[:nd]) if u[0] == i])
            grads.append(d_uses[parts[0][1]] if len(parts) == 1
                         else jnp.concatenate([d_uses[k] for _, k in parts], axis=1))
        return tuple(grads), tuple(jnp.zeros_like(c) for c in const)

    op.defvjp(op_fwd, op_bwd)
    return op(arrays[:n_diff], arrays[n_diff:])


@functools.partial(jax.custom_vjp, nondiff_argnums=(1,))
def _roll_rows(x, k):
    return pltpu.roll(x, k % x.shape[0], 0)


def _roll_rows_fwd(x, k):
    return _roll_rows(x, k), None


def _roll_rows_bwd(k, _, g):
    return (_roll_rows(g, -k),)


_roll_rows.defvjp(_roll_rows_fwd, _roll_rows_bwd)


def _conv3(x, w0, w1, w2, starts):
    rows = lax.broadcasted_iota(jnp.int32, x.shape, 0)
    ends = tuple(s - 1 for s in starts[1:]) + (x.shape[0] - 1,)
    first = functools.reduce(jnp.logical_or, [rows == s for s in starts])
    last = functools.reduce(jnp.logical_or, [rows == e for e in ends])
    prev = jnp.where(first, 0.0, _roll_rows(x, 1))
    nxt = jnp.where(last, 0.0, _roll_rows(x, -1))
    return prev * w0 + x * w1 + nxt * w2


def _rms(x):
    return x * lax.rsqrt(jnp.mean(x * x, axis=-1, keepdims=True) + NORM_EPS)


def _heads(x, n):
    return [x[:, h * HEAD_DIM:(h + 1) * HEAD_DIM] for h in range(n)]


_NT = (((1,), (1,)), ((), ()))
_TN = (((0,), (0,)), ((), ()))
_TQ = 256


def _attn_probs(q, k):
    s = lax.dot_general(q, k, _NT, preferred_element_type=F32) * (HEAD_DIM ** -0.5)
    p = jnp.exp(s - jnp.max(s, axis=-1, keepdims=True))
    return p / jnp.sum(p, axis=-1, keepdims=True)


def _attn_fwd_call(q, k, v):
    n, t = q.shape[0], k.shape[0]
    tq = _pick(n, (_TQ, 128))

    def body(q_ref, k_ref, v_ref, o_ref):
        p = _attn_probs(q_ref[...].astype(BF16), k_ref[...].astype(BF16))
        o_ref[...] = jnp.dot(p.astype(BF16), v_ref[...].astype(BF16), preferred_element_type=F32)

    return pl.pallas_call(
        body, name="attn_f", grid=(ATTN_HEADS, n // tq),
        in_specs=[pl.BlockSpec((tq, HEAD_DIM), lambda h, i: (i, h)),
                  pl.BlockSpec((t, HEAD_DIM), lambda h, i: (0, h // ATTN_GROUP)),
                  pl.BlockSpec((t, HEAD_DIM), lambda h, i: (0, h // ATTN_GROUP))],
        out_specs=pl.BlockSpec((tq, HEAD_DIM), lambda h, i: (i, h)),
        out_shape=jax.ShapeDtypeStruct(q.shape, F32),
        compiler_params=pltpu.CompilerParams(dimension_semantics=("parallel", "parallel"),
                                             vmem_limit_bytes=VMEM_LIMIT),
    )(q, k, v)


def _attn_bwd_call(q, k, v, do):
    n, t = q.shape[0], k.shape[0]
    tq = _pick(n, (_TQ, 128))

    def body(q_ref, k_ref, v_ref, do_ref, dq_ref, dk_ref, dv_ref):
        @pl.when((pl.program_id(1) == 0) & (pl.program_id(2) == 0))
        def _():
            dk_ref[...] = jnp.zeros_like(dk_ref)
            dv_ref[...] = jnp.zeros_like(dv_ref)

        qb, kb, vb, dob = (r[...].astype(BF16) for r in (q_ref, k_ref, v_ref, do_ref))
        p = _attn_probs(qb, kb)
        dp = lax.dot_general(dob, vb, _NT, preferred_element_type=F32)
        ds = p * (dp - jnp.sum(p * dp, axis=-1, keepdims=True)) * (HEAD_DIM ** -0.5)
        dsb = ds.astype(BF16)
        dq_ref[...] = jnp.dot(dsb, kb, preferred_element_type=F32)
        dk_ref[...] += lax.dot_general(dsb, qb, _TN, preferred_element_type=F32)
        dv_ref[...] += lax.dot_general(p.astype(BF16), dob, _TN, preferred_element_type=F32)

    q_spec = pl.BlockSpec((tq, HEAD_DIM), lambda kh, g, i: (i, kh * ATTN_GROUP + g))
    kv_spec = pl.BlockSpec((t, HEAD_DIM), lambda kh, g, i: (0, kh))
    return pl.pallas_call(
        body, name="attn_b", grid=(ATTN_KV_HEADS, ATTN_GROUP, n // tq),
        in_specs=[q_spec, kv_spec, kv_spec, q_spec],
        out_specs=[q_spec, kv_spec, kv_spec],
        out_shape=(jax.ShapeDtypeStruct(q.shape, F32), jax.ShapeDtypeStruct(k.shape, F32),
                   jax.ShapeDtypeStruct(v.shape, F32)),
        compiler_params=pltpu.CompilerParams(dimension_semantics=("parallel", "arbitrary", "arbitrary"),
                                             vmem_limit_bytes=VMEM_LIMIT),
    )(q, k, v, do)


@jax.custom_vjp
def attention(q, k, v):
    return _attn_fwd_call(q, k, v)


def _attention_fwd(q, k, v):
    return _attn_fwd_call(q, k, v), (q, k, v)


def _attention_bwd(res, do):
    return _attn_bwd_call(*res, do)


attention.defvjp(_attention_fwd, _attention_bwd)


_C = GDN_CHUNK


def _hdot(a, b):
    return jnp.dot(a, b, precision=lax.Precision.HIGH, preferred_element_type=F32)


def _each(fn, *lists):
    return [fn(*args) for args in zip(*lists)]


def _unit_lower_inverse(low, blockdiag):
    eye = (lax.broadcasted_iota(jnp.int32, (_C, _C), 0) == lax.broadcasted_iota(jnp.int32, (_C, _C), 1)).astype(F32)
    ld = _each(lambda a: a * blockdiag, low)
    lo = _each(lambda a, d: a - d, low, ld)
    l2 = _each(_hdot, ld, ld)
    l4 = _each(_hdot, l2, l2)
    l8 = _each(_hdot, l4, l4)
    td = _each(lambda d, a2: _hdot(eye - d, eye + a2), ld, l2)
    td = _each(lambda t, a4: _hdot(t, eye + a4), td, l4)
    td = _each(lambda t, a8: _hdot(t, eye + a8), td, l8)
    nn = _each(_hdot, td, lo)
    n2 = _each(_hdot, nn, nn)
    out = _each(lambda n, m2: _hdot(eye - n, eye + m2), nn, n2)
    return _each(_hdot, out, td)


def _gdn_chunks(heads, blockdiag):
    q, k, v, b_b, be_b, e_b, kd_b, m1, dec, gl, s = (list(col) for col in zip(*heads))
    f32dot = lambda a, b: jnp.dot(a, b, preferred_element_type=F32)
    nt = lambda a, b: lax.dot_general(a, b, _NT, preferred_element_type=F32)
    kk = _each(nt, k, k)
    t_inv = _unit_lower_inverse(_each(lambda m, a: m * a, m1, kk), blockdiag)
    u = _each(lambda t, b, x: _hdot(t, b * x), t_inv, b_b, v)
    w = _each(lambda t, b, x: _hdot(t, b * x), t_inv, be_b, k)
    delta = _each(lambda uu, ww, ss: uu - f32dot(ww, ss), u, w, s)
    p = _each(lambda d, qq, kx: d * nt(qq, kx), dec, q, k)
    o = _each(lambda qq, e, ss, pp, dd: f32dot(qq * e, ss) + f32dot(pp, dd), q, e_b, s, p, delta)
    s_new = _each(lambda g, ss, kx, kd, dd: g * ss + lax.dot_general(kx * kd, dd, _TN, preferred_element_type=F32),
                  gl, s, k, kd_b, delta)
    return o, s_new


def _blockdiag_mask():
    r = lax.broadcasted_iota(jnp.int32, (_C, _C), 0) >> 4
    c = lax.broadcasted_iota(jnp.int32, (_C, _C), 1) >> 4
    return (r == c).astype(F32)


def _gdn_specs(nc, ncc, reverse, backward):
    def ch(s):
        s = nc - 1 - s if backward else s
        return jnp.where(s < ncc, ncc - 1 - s, nc + ncc - 1 - s) if reverse else s

    tok = pl.BlockSpec((_C, 3 * GDN_W), lambda s: (ch(s), 0))
    out = pl.BlockSpec((_C, GDN_W), lambda s: (ch(s), 0))
    per_tok = pl.BlockSpec((GDN_HEADS, _C, HEAD_DIM), lambda s: (0, ch(s), 0))
    mat = pl.BlockSpec((GDN_HEADS, None, _C, _C), lambda s: (0, ch(s), 0, 0))
    row = pl.BlockSpec((GDN_HEADS, None, 1, HEAD_DIM), lambda s: (0, ch(s), 0, 0))
    state = pl.BlockSpec((GDN_HEADS, None, HEAD_DIM, HEAD_DIM), lambda s: (0, ch(s), 0, 0))
    return tok, out, per_tok, mat, row, state


def _head_cols(h, part):
    return slice((part * GDN_HEADS + h) * HEAD_DIM, (part * GDN_HEADS + h + 1) * HEAD_DIM)


def _gdn_fwd_call(name, reverse, ncc, qkv, b_b, be_b, e_b, kd_b, m1, dec, gl):
    t = qkv.shape[0]
    nc = t // _C
    tok, out, per_tok, mat, row, state = _gdn_specs(nc, ncc, reverse, False)

    def body(qkv_ref, b_ref, be_ref, e_ref, kd_ref, m1_ref, dec_ref, gl_ref, o_ref, sall_ref, s_ref):
        @pl.when(pl.program_id(0) == 0)
        def _():
            s_ref[...] = jnp.zeros_like(s_ref)

        bd = _blockdiag_mask()
        ins = [[qkv_ref[:, _head_cols(h, 0)], qkv_ref[:, _head_cols(h, 1)], qkv_ref[:, _head_cols(h, 2)],
                b_ref[h], be_ref[h], e_ref[h], kd_ref[h], m1_ref[h], dec_ref[h], gl_ref[h], s_ref[h]]
               for h in range(GDN_HEADS)]
        o, s_new = _gdn_chunks(ins, bd)
        for h in range(GDN_HEADS):
            sall_ref[h] = ins[h][10]
            o_ref[:, _head_cols(h, 0)] = o[h]
            s_ref[h] = s_new[h]

    return pl.pallas_call(
        body, name=name + "_f", grid=(nc,),
        in_specs=[tok, per_tok, per_tok, per_tok, per_tok, mat, mat, row],
        out_specs=[out, state],
        out_shape=(jax.ShapeDtypeStruct((t, GDN_W), F32),
                   jax.ShapeDtypeStruct((GDN_HEADS, nc, HEAD_DIM, HEAD_DIM), F32)),
        scratch_shapes=[pltpu.VMEM((GDN_HEADS, HEAD_DIM, HEAD_DIM), F32)],
        compiler_params=pltpu.CompilerParams(dimension_semantics=("arbitrary",), vmem_limit_bytes=VMEM_LIMIT),
    )(qkv, b_b, be_b, e_b, kd_b, m1, dec, gl)


def _gdn_bwd_call(name, reverse, ncc, qkv, b_b, be_b, e_b, kd_b, m1, dec, gl, sall, do):
    t = qkv.shape[0]
    nc = t // _C
    tok, out, per_tok, mat, row, state = _gdn_specs(nc, ncc, reverse, True)

    def body(qkv_ref, b_ref, be_ref, e_ref, kd_ref, m1_ref, dec_ref, gl_ref, sall_ref, do_ref,
             dqkv_ref, db_ref, dbe_ref, de_ref, dkd_ref, dm1_ref, ddec_ref, dgl_ref, ds_ref):
        @pl.when(pl.program_id(0) == 0)
        def _():
            ds_ref[...] = jnp.zeros_like(ds_ref)

        bd = _blockdiag_mask()
        ins = [[qkv_ref[:, _head_cols(h, 0)], qkv_ref[:, _head_cols(h, 1)], qkv_ref[:, _head_cols(h, 2)],
                b_ref[h], be_ref[h], e_ref[h], kd_ref[h], m1_ref[h], dec_ref[h], gl_ref[h], sall_ref[h]]
               for h in range(GDN_HEADS)]
        _, vjp = jax.vjp(lambda hs: _gdn_chunks(hs, bd), ins)
        (all_grads,) = vjp(([do_ref[:, _head_cols(h, 0)] for h in range(GDN_HEADS)],
                            [ds_ref[h] for h in range(GDN_HEADS)]))
        for h, grads in enumerate(all_grads):
            for part in range(3):
                dqkv_ref[:, _head_cols(h, part)] = grads[part]
            for ref, g in zip((db_ref, dbe_ref, de_ref, dkd_ref, dm1_ref, ddec_ref, dgl_ref), grads[3:10]):
                ref[h] = g
            ds_ref[h] = grads[10]

    shp = lambda a: jax.ShapeDtypeStruct(a.shape, F32)
    return pl.pallas_call(
        body, name=name + "_b", grid=(nc,),
        in_specs=[tok, per_tok, per_tok, per_tok, per_tok, mat, mat, row, state, out],
        out_specs=[tok, per_tok, per_tok, per_tok, per_tok, mat, mat, row],
        out_shape=(shp(qkv), shp(b_b), shp(be_b), shp(e_b), shp(kd_b), shp(m1), shp(dec), shp(gl)),
        scratch_shapes=[pltpu.VMEM((GDN_HEADS, HEAD_DIM, HEAD_DIM), F32)],
        compiler_params=pltpu.CompilerParams(dimension_semantics=("arbitrary",), vmem_limit_bytes=VMEM_LIMIT),
    )(qkv, b_b, be_b, e_b, kd_b, m1, dec, gl, sall, do)


@functools.partial(jax.custom_vjp, nondiff_argnums=(0, 1, 2))
def gdn_scan(name, reverse, ncc, qkv, b_b, be_b, e_b, kd_b, m1, dec, gl):
    return _gdn_fwd_call(name, reverse, ncc, qkv, b_b, be_b, e_b, kd_b, m1, dec, gl)[0]


def _gdn_scan_fwd(name, reverse, ncc, *args):
    o, sall = _gdn_fwd_call(name, reverse, ncc, *args)
    return o, (args, sall)


def _gdn_scan_bwd(name, reverse, ncc, res, do):
    args, sall = res
    return _gdn_bwd_call(name, reverse, ncc, *args, sall, do)


gdn_scan.defvjp(_gdn_scan_fwd, _gdn_scan_bwd)


def _rope_tables(n, cl):
    t = np.arange(n)
    inv_freq = (ROPE_THETA ** (-np.arange(0, HEAD_DIM // 2, 2, dtype=np.float32) / (HEAD_DIM // 2))).astype(np.float32)
    ang_r = (t // GRID_W).astype(np.float32)[:, None] * inv_freq
    ang_c = (t % GRID_W).astype(np.float32)[:, None] * inv_freq
    cos = np.concatenate([np.cos(ang_r), np.cos(ang_r), np.cos(ang_c), np.cos(ang_c)], axis=1)
    sin = np.concatenate([-np.sin(ang_r), np.sin(ang_r), -np.sin(ang_c), np.sin(ang_c)], axis=1)
    cos_all = np.concatenate([np.ones((cl, HEAD_DIM), np.float32), cos], axis=0)
    sin_all = np.concatenate([np.zeros((cl, HEAD_DIM), np.float32), sin], axis=0)
    j = np.arange(HEAD_DIM)
    src = np.where((j % 64) < 32, j + 32, j - 32)
    perm = np.zeros((HEAD_DIM, HEAD_DIM), np.float32)
    perm[src, j] = 1.0
    return (jnp.asarray(cos.astype(np.float32)), jnp.asarray(sin.astype(np.float32)),
            jnp.asarray(cos_all), jnp.asarray(sin_all), jnp.asarray(perm))


def _shift_rows(a, cl):
    z = jnp.zeros((1, a.shape[1]), a.dtype)
    parts = [a[:cl], a[cl:]] if cl else [a]
    prev = jnp.concatenate([jnp.concatenate([z, p[:-1]], axis=0) for p in parts], axis=0)
    nxt = jnp.concatenate([jnp.concatenate([p[1:], z], axis=0) for p in parts], axis=0)
    return prev, nxt


def _gdn_factors(log_a, beta, reverse):
    t = log_a.shape[0]
    nc = t // _C
    la = log_a.reshape(nc, _C, GDN_HEADS).transpose(2, 0, 1)
    be = beta.reshape(nc, _C, GDN_HEADS).transpose(2, 0, 1)
    gam = lax.cumsum(la, axis=2, reverse=reverse)
    idx = jnp.arange(_C)
    incl = (idx[:, None] <= idx[None, :]) if reverse else (idx[:, None] >= idx[None, :])
    strict = (idx[:, None] < idx[None, :]) if reverse else (idx[:, None] > idx[None, :])
    dec = jnp.exp(jnp.where(incl, gam[..., :, None] - gam[..., None, :], -jnp.inf))
    m1 = jnp.where(strict, be[..., :, None] * dec, 0.0)
    e = jnp.exp(gam)
    g_last = gam[..., :1] if reverse else gam[..., -1:]
    lanes = lambda a: jnp.broadcast_to(a.reshape(GDN_HEADS, t, 1), (GDN_HEADS, t, HEAD_DIM))
    gl = jnp.broadcast_to(jnp.exp(g_last)[..., None], (GDN_HEADS, nc, 1, HEAD_DIM))
    return lanes(be), lanes(be * e), lanes(e), lanes(jnp.exp(g_last - gam)), m1, dec, gl


def local_loss(x, wz, wb, ws, c, ctx, target):
    n, cl = x.shape[0], ctx.shape[0]
    cos_q, sin_q, cos_k, sin_k, perm = _rope_tables(n, cl)

    sc_in = jnp.concatenate([jax.nn.silu(c), jax.nn.silu(ws["c_ctx"])[None, :], jnp.zeros((14, D_MODEL), F32)], axis=0)
    mod = pmm(sc_in, wb["w_mod"], wz["w_mod"], "mm_mod") + ws["b_mod"]
    sh1, sc1, g1, sh2, sc2, g2 = [mod[0:1, i * D_MODEL:(i + 1) * D_MODEL] for i in range(6)]
    csh1, csc1 = mod[1:2, 0:D_MODEL], mod[1:2, D_MODEL:2 * D_MODEL]

    def norm_mod(a, sh, sc):
        return (_rms(a) * (1.0 + sc) + sh,)

    (hx,) = rowop(norm_mod, "normmod_x", (x,), (sh1, sc1))
    (hc,) = rowop(norm_mod, "normmod_c", (ctx,), (csh1, csc1))
    h_all = jnp.concatenate([hc, hx], axis=0)
    p_main = pmm_t(h_all, wb["w_in_main"], wz["w_in_main"], "mm_in")
    p_small = pmm_t(h_all, wb["w_in_small"], wz["w_in_small"], "mm_ins")
    ak, av, qkv, aq, z, gate = jnp.split(p_main, [KV_W, 2 * KV_W, SMALL_AT, SMALL_AT + Q_W, SMALL_AT + Q_W + GDN_W],
                                         axis=1)
    db, da = p_small[:, :2 * GDN_HEADS], p_small[:, 2 * GDN_HEADS:4 * GDN_HEADS]

    def qk_prep(nh):
        def fn(a, w, cos, sin, pm):
            outs = []
            for ah in _heads(a, nh):
                y = _rms(ah) * w
                outs.append(y * cos + _hdot(y, pm) * sin)
            return (jnp.concatenate(outs, axis=1),)
        return fn

    (q_x,) = rowop(qk_prep(ATTN_HEADS), "q_prep", (aq[cl:],), (ws["q_norm_w"],), (cos_q, sin_q), (perm,))
    (k_all,) = rowop(qk_prep(ATTN_KV_HEADS), "k_prep", (ak,), (ws["k_norm_w"],), (cos_k, sin_k), (perm,))
    attn_x = attention(q_x, k_all, av)

    cw = ws["conv_qkv_w"]
    normed = jnp.asarray(np.repeat([1.0, 1.0, 0.0], GDN_W)[None, :], F32)
    scale = jnp.asarray(np.repeat([HEAD_DIM ** -0.5, 1.0, 1.0], GDN_W)[None, :], F32)

    def gdn_prep(a, w0, w1, w2, nf, sc):
        s = jax.nn.silu(_conv3(a, w0, w1, w2, (0, cl)))
        inv = lax.rsqrt(jnp.sum(s * s, axis=-1, keepdims=True) + NORM_EPS)
        return (s * jnp.where(nf > 0.0, inv * sc, 1.0),)

    (qkvn,) = colop(gdn_prep, "gdn_prep", (qkv, cw[0:1], cw[1:2], cw[2:3], normed, scale),
                    [(i, 0) for i in range(6)], 2, 3 * GDN_HEADS)
    beta = jax.nn.sigmoid(db).reshape(-1, 2, GDN_HEADS)
    log_a = -jnp.exp(ws["a_log"])[None] * jax.nn.softplus(da.reshape(-1, 2, GDN_HEADS) + ws["dt_bias"][None])
    o_fwd = gdn_scan("gdn_d0", False, cl // _C, qkvn, *_gdn_factors(log_a[:, 0], beta[:, 0], False))
    o_bwd = gdn_scan("gdn_d1", True, cl // _C, qkvn, *_gdn_factors(log_a[:, 1], beta[:, 1], True))
    o_x = o_fwd[cl:] + o_bwd[cl:]

    def gdn_out(o, zz, w):
        outs = [_rms(oh) * w * jax.nn.silu(zh) for oh, zh in zip(_heads(o, GDN_HEADS), _heads(zz, GDN_HEADS))]
        return (jnp.concatenate(outs, axis=1),)

    (gdn_x,) = rowop(gdn_out, "gdn_out", (o_x, z[cl:]), (ws["gdn_norm_w"],))

    pa = pmm(attn_x, wb["w_pa"], wz["w_pa"], "mm_pa")
    pd = pmm(gdn_x, wb["w_pd"], wz["w_pd"], "mm_pd")

    def merge(a, d, g):
        return (jax.nn.sigmoid(g[:, :D_MODEL]) * a + jax.nn.sigmoid(g[:, D_MODEL:]) * d,)

    (y,) = rowop(merge, "merge", (pa, pd, gate[cl:]))
    mo = pmm(y, wb["w_out"], wz["w_out"], "mm_out")

    def res_norm_mod(xx, m, g, sh, sc):
        x1 = xx + g * m
        return x1, _rms(x1) * (1.0 + sc) + sh

    x1, h2 = rowop(res_norm_mod, "res1", (x, mo), (g1, sh2, sc2))
    up = pmm(h2, wb["w_up"], wz["w_up"], "mm_up")
    fw = ws["ffn_conv_w"]

    def ffn_act(ug, uv, w0g, w0v, w1g, w1v, w2g, w2v, bg, bv):
        g = _conv3(ug, w0g, w1g, w2g, (0,)) + bg
        v = _conv3(uv, w0v, w1v, w2v, (0,)) + bv
        return (jax.nn.silu(g) * v,)

    half = D_FF // HEAD_DIM
    (act,) = colop(ffn_act, "ffn_act", (up, fw[0:1], fw[1:2], fw[2:3], ws["ffn_conv_b"]),
                   [(i, off) for i in range(5) for off in (0, half)], 0, half)
    dn = pmm(act, wb["w_down"], wz["w_down"], "mm_down")

    def head(xx, m, g, w, tgt):
        yy = _rms(xx + g * m) * w
        err = (yy - tgt) ** 2
        return (jnp.broadcast_to(0.5 * jnp.mean(err, axis=-1, keepdims=True), (xx.shape[0], HEAD_DIM)),)

    (row_loss,) = rowop(head, "head", (x1, dn), (g2, ws["final_norm_w"][None, :]), (target,))
    return jnp.sum(row_loss[:, 0])


_HBM = pl.BlockSpec(memory_space=pltpu.HBM)


def _chip_peers():
    x, y = lax.axis_index("x"), lax.axis_index("y")
    return [(1 - x, y), (x, 1 - y), (1 - x, 1 - y)]


_SPLIT_COLS = ("w_in",)


def _half_of(view, nm, idx, lead=0):
    r, cdim = view.shape[-2:]
    pre = (slice(None),) * lead
    if nm in _SPLIT_COLS:
        return view.at[pre + (slice(None), pl.ds(pl.multiple_of(idx * (cdim // 2), 128), cdim // 2))]
    return view.at[pre + (pl.ds(pl.multiple_of(idx * (r // 2), 16), r // 2), slice(None))]


def _remote(src, dst, send_sem, recv_sem, dev):
    return pltpu.make_async_remote_copy(src_ref=src, dst_ref=dst, send_sem=send_sem, recv_sem=recv_sem,
                                        device_id=dev, device_id_type=MESH)


def _hbm_call(body, name, ins, out_shape, n_sems, in_place=False):
    names = tuple(ins)
    return dict(zip(names, pl.pallas_call(
        body, name=name, in_specs=[_HBM] * len(names), out_specs=[_HBM] * len(names),
        out_shape=[out_shape(nm, ins[nm]) for nm in names],
        scratch_shapes=[pltpu.SemaphoreType.DMA((k,)) for k in n_sems],
        input_output_aliases={i: i for i in range(len(names))} if in_place else {},
    )(*[ins[nm] for nm in names])))


def all_gather_chips(shards):
    names = tuple(shards)
    n = len(names)

    def body(*refs):
        ins, outs = dict(zip(names, refs[:n])), dict(zip(names, refs[n:2 * n]))
        ici_send, ici_recv, d2d_send, d2d_recv, own_send, own_recv = refs[2 * n:]
        x, y, c = lax.axis_index("x"), lax.axis_index("y"), lax.axis_index("c")
        me, sib = 2 * x + y, (x, y, 1 - c)
        own = [_remote(ins[nm], outs[nm].at[me], own_send.at[i], own_recv.at[i], sib) for i, nm in enumerate(names)]
        for cp in own:
            cp.start()
        sends = []
        for k, (px, py) in enumerate(_chip_peers()):
            for i, nm in enumerate(names):
                cp = _remote(_half_of(ins[nm], nm, c), _half_of(outs[nm].at[me], nm, c), ici_send.at[k * n + i],
                             ici_recv.at[k * n + i], (px, py, c))
                cp.start()
                sends.append(cp)
        for k, (px, py) in enumerate(_chip_peers()):
            for i, nm in enumerate(names):
                landed = _half_of(outs[nm].at[2 * px + py], nm, c)
                _remote(landed, landed, ici_send.at[k * n + i], ici_recv.at[k * n + i], (px, py, c)).wait_recv()
                fw = _remote(landed, landed, d2d_send.at[k * n + i], d2d_recv.at[k * n + i], sib)
                fw.start()
                sends.append(fw)
        for k, (px, py) in enumerate(_chip_peers()):
            for i, nm in enumerate(names):
                other = _half_of(outs[nm].at[2 * px + py], nm, 1 - c)
                _remote(other, other, d2d_send.at[k * n + i], d2d_recv.at[k * n + i], sib).wait_recv()
        for cp in sends:
            cp.wait_send()
        for cp in own:
            cp.wait()

    return _hbm_call(body, "ag_weights", shards, lambda nm, a: jax.ShapeDtypeStruct((N_CHIPS,) + a.shape, a.dtype),
                     (3 * n, 3 * n, 3 * n, 3 * n, n, n))


def sibling_halves(blocks):
    names = tuple(blocks)

    def body(*refs):
        n = len(names)
        ins, outs = dict(zip(names, refs[:n])), dict(zip(names, refs[n:2 * n]))
        send_sems, recv_sems = refs[2 * n:]
        x, y, c = lax.axis_index("x"), lax.axis_index("y"), lax.axis_index("c")
        cps = [_remote(_half_of(ins[nm], nm, 1 - c, lead=1), outs[nm], send_sems.at[i], recv_sems.at[i], (x, y, 1 - c))
               for i, nm in enumerate(names)]
        for cp in cps:
            cp.start()
        for cp in cps:
            cp.wait()

    def half_shape(nm, a):
        r, cdim = a.shape[-2:]
        return jax.ShapeDtypeStruct((N_CHIPS, r, cdim // 2) if nm in _SPLIT_COLS else (N_CHIPS, r // 2, cdim), a.dtype)

    return _hbm_call(body, "rs_sibling", blocks, half_shape, (len(names), len(names)))


def scatter_halves(blocks):
    names = tuple(blocks)
    n = len(names)

    def body(*refs):
        ins, outs = dict(zip(names, refs[:n])), dict(zip(names, refs[n:2 * n]))
        send_sems, recv_sems = refs[2 * n:]
        c = lax.axis_index("c")
        cps = [_remote(ins[nm].at[2 * px + py], outs[nm].at[k], send_sems.at[k * n + i], recv_sems.at[k * n + i],
                       (px, py, c))
               for k, (px, py) in enumerate(_chip_peers()) for i, nm in enumerate(names)]
        for cp in cps:
            cp.start()
        for cp in cps:
            cp.wait_recv()
        for cp in cps:
            cp.wait_send()

    return _hbm_call(body, "rs_grads", blocks, lambda nm, a: jax.ShapeDtypeStruct((3,) + a.shape[1:], a.dtype),
                     (3 * n, 3 * n))


def sibling_assemble(arrays):
    names = tuple(arrays)

    def body(*refs):
        n = len(names)
        ins, outs = dict(zip(names, refs[:n])), dict(zip(names, refs[n:2 * n]))
        send_sems, recv_sems = refs[2 * n:]
        x, y, c = lax.axis_index("x"), lax.axis_index("y"), lax.axis_index("c")
        cps = [_remote(_half_of(ins[nm], nm, c), _half_of(outs[nm], nm, c), send_sems.at[i], recv_sems.at[i],
                       (x, y, 1 - c)) for i, nm in enumerate(names)]
        for cp in cps:
            cp.start()
        for i, nm in enumerate(names):
            other = _half_of(outs[nm], nm, 1 - c)
            _remote(other, other, send_sems.at[i], recv_sems.at[i], (x, y, 1 - c)).wait_recv()
        for cp in cps:
            cp.wait_send()

    return _hbm_call(body, "rs_assemble", arrays, lambda nm, a: jax.ShapeDtypeStruct(a.shape, a.dtype),
                     (len(names), len(names)), in_place=True)


def all_reduce_small(v):
    def body(v_ref, tot_ref, gath_ref, send_sems, recv_sems):
        x, y, c = lax.axis_index("x"), lax.axis_index("y"), lax.axis_index("c")
        me = 4 * x + 2 * y + c
        gath_ref[me] = v_ref[...]

        def peer(k):
            m = k + 1
            return (x ^ (m >> 2 & 1), y ^ (m >> 1 & 1), c ^ (m & 1))

        sends = [pltpu.make_async_remote_copy(src_ref=v_ref, dst_ref=gath_ref.at[me], send_sem=send_sems.at[k],
                                              recv_sem=recv_sems.at[k], device_id=peer(k), device_id_type=MESH)
                 for k in range(N_DEV - 1)]
        for cp in sends:
            cp.start()
        for k in range(N_DEV - 1):
            px, py, pc = peer(k)
            pltpu.make_async_remote_copy(src_ref=v_ref, dst_ref=gath_ref.at[4 * px + 2 * py + pc],
                                         send_sem=send_sems.at[k], recv_sem=recv_sems.at[k], device_id=peer(k),
                                         device_id_type=MESH).wait_recv()
        for cp in sends:
            cp.wait_send()
        acc = gath_ref[0]
        for d in range(1, N_DEV):
            acc = acc + gath_ref[d]
        tot_ref[...] = acc

    vm = pl.BlockSpec(memory_space=pltpu.VMEM)
    return pl.pallas_call(
        body, name="ar_small", in_specs=[vm], out_specs=[vm, vm],
        out_shape=(jax.ShapeDtypeStruct(v.shape, v.dtype), jax.ShapeDtypeStruct((N_DEV,) + v.shape, v.dtype)),
        scratch_shapes=[pltpu.SemaphoreType.DMA((N_DEV - 1,)), pltpu.SemaphoreType.DMA((N_DEV - 1,))],
    )(v)[0]


def _elementwise(fn, name, ins, n_out, out_dtype=F32):
    r, cdim = ins[0].shape
    tr = _pick(r, tuple(p for p in (488, 256, 128, 104, 64, 32, 16, 8) if p * cdim * 4 <= 2 * 1024 * 1024))
    spec = pl.BlockSpec((tr, cdim), lambda i: (i, 0))

    def body(*refs):
        res = fn(*[ref[...] for ref in refs[:len(ins)]])
        for o_ref, v in zip(refs[len(ins):], res):
            o_ref[...] = v

    return pl.pallas_call(
        body, name=name, grid=(r // tr,), in_specs=[spec] * len(ins), out_specs=[spec] * n_out,
        out_shape=tuple(jax.ShapeDtypeStruct((r, cdim), out_dtype) for _ in range(n_out)),
        compiler_params=pltpu.CompilerParams(dimension_semantics=("parallel",), vmem_limit_bytes=VMEM_LIMIT),
    )(*ins)


def _half_block_specs(nm, shard_shape):
    r, cdim = shard_shape
    if nm in _SPLIT_COLS:
        return (None, r, cdim // 2), (lambda j, c: (j, 0, c))
    return (None, r // 2, cdim), (lambda j, c: (j, c, 0))


def _presum(nm, sel, g32, a):
    blk, at = _half_block_specs(nm, g32.shape[1:])

    def body(s_ref, g_ref, a_ref, o_ref):
        del s_ref
        o_ref[...] = (g_ref[...] + a_ref[...]).astype(BF16)

    return pl.pallas_call(
        body, name="rs_presum_" + nm,
        grid_spec=pltpu.PrefetchScalarGridSpec(
            num_scalar_prefetch=1, grid=(N_CHIPS,),
            in_specs=[pl.BlockSpec(blk, lambda j, s: at(j, s[0])), pl.BlockSpec(blk, lambda j, s: (j, 0, 0))],
            out_specs=pl.BlockSpec(blk, lambda j, s: (j, 0, 0))),
        out_shape=jax.ShapeDtypeStruct(a.shape, BF16),
        compiler_params=pltpu.CompilerParams(dimension_semantics=("parallel",), vmem_limit_bytes=VMEM_LIMIT),
    )(sel, g32, a)


def _finalsum(nm, sel, g32, a, got):
    blk, at = _half_block_specs(nm, g32.shape[1:])

    def body(s_ref, g_ref, a_ref, r_ref, o_ref):
        del s_ref
        acc = g_ref[...] + a_ref[...]
        for k in range(3):
            acc = acc + r_ref[k].astype(F32)
        o_ref[...] = acc

    return pl.pallas_call(
        body, name="rs_final_" + nm,
        grid_spec=pltpu.PrefetchScalarGridSpec(
            num_scalar_prefetch=1, grid=(1,),
            in_specs=[pl.BlockSpec(blk, lambda i, s: at(s[1], s[0])), pl.BlockSpec(blk, lambda i, s: (s[1], 0, 0)),
                      pl.BlockSpec(got.shape, lambda i, s: (0, 0, 0))],
            out_specs=pl.BlockSpec(blk[1:], lambda i, s: at(0, s[0])[1:])),
        out_shape=jax.ShapeDtypeStruct(g32.shape[1:], F32),
        compiler_params=pltpu.CompilerParams(dimension_semantics=("arbitrary",), vmem_limit_bytes=VMEM_LIMIT),
    )(sel, g32, a, got)


def _adamw(w, g, m, v, name):
    shape = w.shape
    to2 = lambda a: a.reshape(-1, shape[-1])

    def fn(w_, g_, m_, v_):
        m_new = ADAM_B1 * m_ + (1.0 - ADAM_B1) * g_
        v_new = ADAM_B2 * v_ + (1.0 - ADAM_B2) * (g_ * g_)
        m_hat = m_new / (1.0 - ADAM_B1 ** ADAM_STEP)
        v_hat = v_new / (1.0 - ADAM_B2 ** ADAM_STEP)
        delta = -ADAM_LR * (m_hat / (jnp.sqrt(v_hat) + ADAM_EPS) + ADAM_WD * w_)
        return delta, m_new, v_new

    outs = _elementwise(fn, name, [to2(a) for a in (w, g, m, v)], 3)
    return tuple(o.reshape(shape) for o in outs)


_BIG = ("w_mod", "w_in", "w_pa", "w_pd", "w_out", "w_up", "w_down")
_COL_SHARDED = ("w_mod", "w_up")
_FULL_SHAPE = {"w_mod": (D_MODEL, MOD_W), "w_in": (IN_COLS, D_MODEL), "w_pa": (Q_W, D_MODEL), "w_pd": (GDN_W, D_MODEL),
               "w_out": (D_MODEL, D_MODEL), "w_up": (D_MODEL, 2 * D_FF), "w_down": (D_FF, D_MODEL)}


def _shard_shape(name):
    r, cdim = _FULL_SHAPE[name]
    return (r, cdim // N_CHIPS) if name in _COL_SHARDED else (r // N_CHIPS, cdim)


_CONV_ELEMS = 2 * (3 * CONV_W // N_CHIPS + 3 * 2 * D_FF // N_CHIPS)
_CONV_ROWS = 32


def _blocks_of_full(name, full):
    r, cdim = _FULL_SHAPE[name]
    if name in _COL_SHARDED:
        return full.reshape(r, N_CHIPS, cdim // N_CHIPS).transpose(1, 0, 2)
    return full.reshape(N_CHIPS, r // N_CHIPS, cdim)


def _full_of_blocks(name, blocks):
    r, cdim = _FULL_SHAPE[name]
    if name in _COL_SHARDED:
        return blocks.transpose(1, 0, 2).reshape(r, cdim)
    return blocks.reshape(r, cdim)


def _w_in_regroup(w_in_t):
    main = jnp.concatenate([w_in_t[:SMALL_AT], w_in_t[SMALL_AT + 4 * GDN_HEADS:]], axis=0)
    small = jnp.pad(w_in_t[SMALL_AT:SMALL_AT + 4 * GDN_HEADS], ((0, HEAD_DIM - 4 * GDN_HEADS), (0, 0)))
    return main, small


def _w_in_ungroup(main, small):
    return jnp.concatenate([main[:SMALL_AT], small[:4 * GDN_HEADS], main[SMALL_AT:]], axis=0)


_SMALL = ("c_ctx", "b_mod", "q_norm_w", "k_norm_w", "conv_qkv_w", "a_log", "dt_bias", "gdn_norm_w", "ffn_conv_w",
          "ffn_conv_b", "final_norm_w")


def _pack_small(tree, rows):
    flat = jnp.concatenate([tree[nm].reshape(-1) for nm in _SMALL])
    return jnp.pad(flat, (0, rows * 128 - flat.shape[0])).reshape(rows, 128)


def _unpack_small(packed, like):
    flat, out, off = packed.reshape(-1), {}, 0
    for nm in _SMALL:
        size = int(np.prod(like[nm].shape))
        out[nm] = flat[off:off + size].reshape(like[nm].shape)
        off += size
    return out


def kernel(x, c, ctx, c_ctx, w_mod, b_mod, w_in, q_norm_w, k_norm_w, conv_qkv_w, a_log, dt_bias, gdn_norm_w, w_pa, w_pd, w_out, w_up, ffn_conv_w, ffn_conv_b, w_down, final_norm_w, loss_target, m_c_ctx, m_w_mod, m_b_mod, m_w_in, m_q_norm_w, m_k_norm_w, m_conv_qkv_w, m_a_log, m_dt_bias, m_gdn_norm_w, m_w_pa, m_w_pd, m_w_out, m_w_up, m_ffn_conv_w, m_ffn_conv_b, m_w_down, m_final_norm_w, v_c_ctx, v_w_mod, v_b_mod, v_w_in, v_q_norm_w, v_k_norm_w, v_conv_qkv_w, v_a_log, v_dt_bias, v_gdn_norm_w, v_w_pa, v_w_pd, v_w_out, v_w_up, v_ffn_conv_w, v_ffn_conv_b, v_w_down, v_final_norm_w):
    names = ("c_ctx", "w_mod", "b_mod", "w_in", "q_norm_w", "k_norm_w", "conv_qkv_w", "a_log", "dt_bias", "gdn_norm_w",
             "w_pa", "w_pd", "w_out", "w_up", "ffn_conv_w", "ffn_conv_b", "w_down", "final_norm_w")
    w_sh = dict(c_ctx=c_ctx, w_mod=w_mod, b_mod=b_mod, w_in=w_in, q_norm_w=q_norm_w, k_norm_w=k_norm_w,
                conv_qkv_w=conv_qkv_w, a_log=a_log, dt_bias=dt_bias, gdn_norm_w=gdn_norm_w, w_pa=w_pa, w_pd=w_pd,
                w_out=w_out, w_up=w_up, ffn_conv_w=ffn_conv_w, ffn_conv_b=ffn_conv_b, w_down=w_down,
                final_norm_w=final_norm_w)
    m_sh = dict(c_ctx=m_c_ctx, w_mod=m_w_mod, b_mod=m_b_mod, w_in=m_w_in, q_norm_w=m_q_norm_w, k_norm_w=m_k_norm_w,
                conv_qkv_w=m_conv_qkv_w, a_log=m_a_log, dt_bias=m_dt_bias, gdn_norm_w=m_gdn_norm_w, w_pa=m_w_pa,
                w_pd=m_w_pd, w_out=m_w_out, w_up=m_w_up, ffn_conv_w=m_ffn_conv_w, ffn_conv_b=m_ffn_conv_b,
                w_down=m_w_down, final_norm_w=m_final_norm_w)
    v_sh = dict(c_ctx=v_c_ctx, w_mod=v_w_mod, b_mod=v_b_mod, w_in=v_w_in, q_norm_w=v_q_norm_w, k_norm_w=v_k_norm_w,
                conv_qkv_w=v_conv_qkv_w, a_log=v_a_log, dt_bias=v_dt_bias, gdn_norm_w=v_gdn_norm_w, w_pa=v_w_pa,
                w_pd=v_w_pd, w_out=v_w_out, w_up=v_w_up, ffn_conv_w=v_ffn_conv_w, ffn_conv_b=v_ffn_conv_b,
                w_down=v_w_down, final_norm_w=v_final_norm_w)
    chip = 2 * lax.axis_index("x") + lax.axis_index("y")

    conv_bits = jnp.concatenate([lax.bitcast_convert_type(w_sh[nm][0], BF16).reshape(-1)
                                 for nm in ("conv_qkv_w", "ffn_conv_w")])
    shards = {nm: w_sh[nm][0].astype(BF16).T if nm == "w_in" else w_sh[nm][0].astype(BF16) for nm in _BIG}
    shards["conv"] = jnp.pad(conv_bits, (0, _CONV_ROWS * D_MODEL - _CONV_ELEMS)).reshape(_CONV_ROWS, D_MODEL)
    gathered = all_gather_chips(shards)
    wb = {nm: _full_of_blocks(nm, gathered[nm]) for nm in _BIG}
    wb["w_in_main"], wb["w_in_small"] = _w_in_regroup(wb.pop("w_in"))
    conv_all = gathered["conv"].reshape(N_CHIPS, -1)[:, :_CONV_ELEMS]
    n_cq = 2 * 3 * CONV_W // N_CHIPS
    unbits = lambda a, w: lax.bitcast_convert_type(a.reshape(N_CHIPS, 3, w // N_CHIPS, 2), F32).transpose(1, 0, 2).reshape(3, w)
    ws = dict(c_ctx=c_ctx, b_mod=b_mod, q_norm_w=q_norm_w, k_norm_w=k_norm_w, a_log=a_log[0], dt_bias=dt_bias[0],
              gdn_norm_w=gdn_norm_w, ffn_conv_b=ffn_conv_b, final_norm_w=final_norm_w,
              conv_qkv_w=unbits(conv_all[:, :n_cq], CONV_W), ffn_conv_w=unbits(conv_all[:, n_cq:], 2 * D_FF))
    wz = {nm: jnp.zeros(a.shape, F32) for nm, a in wb.items()}

    loss_local, (gx, gz, gs) = jax.value_and_grad(local_loss, argnums=(0, 1, 3))(
        x[0], wz, wb, ws, c, ctx[0], loss_target[0])
    loss = lax.psum(loss_local, ("x", "y", "c"))

    gz["w_in"] = _w_in_ungroup(gz.pop("w_in_main"), gz.pop("w_in_small"))
    g32 = {nm: _blocks_of_full(nm, gz[nm]) for nm in _BIG}
    sel = jnp.stack([lax.axis_index("c"), chip]).astype(jnp.int32)
    theirs = sibling_halves(g32)
    got = scatter_halves({nm: _presum(nm, sel, g32[nm], theirs[nm]) for nm in _BIG})
    g_big = sibling_assemble({nm: _finalsum(nm, sel, g32[nm], theirs[nm], got[nm]) for nm in _BIG})

    gs["a_log"], gs["dt_bias"] = gs["a_log"][None], gs["dt_bias"][None]
    like = {nm: gs[nm] for nm in _SMALL}
    small_rows = -(-sum(int(np.prod(like[nm].shape)) for nm in _SMALL) // 1024) * 8
    g_small = _unpack_small(all_reduce_small(_pack_small(gs, small_rows)), like)
    for nm, width in (("conv_qkv_w", CONV_W), ("ffn_conv_w", 2 * D_FF)):
        g_small[nm] = lax.dynamic_slice_in_dim(g_small[nm], chip * (width // N_CHIPS), width // N_CHIPS, axis=1)[None]

    grads, deltas, new_m, new_v = {}, {}, {}, {}
    for nm in _BIG:
        g = g_big[nm].T if nm == "w_in" else g_big[nm]
        grads[nm] = g[None]
        deltas[nm], new_m[nm], new_v[nm] = (o[None] for o in _adamw(w_sh[nm][0], g, m_sh[nm][0], v_sh[nm][0],
                                                                     "adamw_" + nm))
    shard_like = {nm: w_sh[nm] for nm in _SMALL}
    rows_l = -(-sum(int(np.prod(shard_like[nm].shape)) for nm in _SMALL) // 1024) * 8
    g_l = _pack_small({nm: g_small[nm].reshape(w_sh[nm].shape) for nm in _SMALL}, rows_l)
    outs = _adamw(_pack_small(w_sh, rows_l), g_l, _pack_small(m_sh, rows_l), _pack_small(v_sh, rows_l), "adamw_small")
    grads.update(_unpack_small(g_l, shard_like))
    for tree, packed in zip((deltas, new_m, new_v), outs):
        tree.update(_unpack_small(packed, shard_like))

    return (loss, gx[None], *[grads[nm] for nm in names], *[deltas[nm] for nm in names],
            *[new_m[nm] for nm in names], *[new_v[nm] for nm in names])
```

```python
import functools
import math

import jax
import jax.numpy as jnp
import numpy as np
from jax import lax
from jax.experimental import pallas as pl
from jax.experimental.pallas import tpu as pltpu

F32 = jnp.float32
BF16 = jnp.bfloat16
HIGHEST = lax.Precision.HIGHEST
MESH = pl.DeviceIdType.MESH

D_MODEL = 1024
GRID_W = 64
ATTN_HEADS = 8
ATTN_KV_HEADS = 2
ATTN_GROUP = ATTN_HEADS // ATTN_KV_HEADS
HEAD_DIM = 128
ROPE_THETA = 10000.0
GDN_HEADS = 8
GDN_CHUNK = 64
D_FF = 2816
NORM_EPS = 1e-6
KV_W = ATTN_KV_HEADS * HEAD_DIM
Q_W = ATTN_HEADS * HEAD_DIM
GDN_W = GDN_HEADS * HEAD_DIM
CONV_W = 3 * GDN_W
MOD_W = 6 * D_MODEL
IN_COLS = 2 * KV_W + CONV_W + 4 * GDN_HEADS + Q_W + GDN_W + 2 * D_MODEL
IN_MAIN = IN_COLS - 4 * GDN_HEADS
SMALL_AT = 2 * KV_W + CONV_W
N_CHIPS = 4
N_DEV = 8

ADAM_LR = 0.001
ADAM_B1 = 0.9
ADAM_B2 = 0.999
ADAM_EPS = 1e-08
ADAM_WD = 0.01
ADAM_STEP = 10

VMEM_LIMIT = 48 * 1024 * 1024
MATMUL_VMEM_BUDGET = 32 * 1024 * 1024


def _pick(dim, prefs):
    for p in prefs:
        if p <= dim and dim % p == 0:
            return p
    return dim


_DIMS = {
    "nn": (((1,), (0,)), ((), ())),
    "nt": (((1,), (1,)), ((), ())),
    "tn": (((0,), (0,)), ((), ())),
}


def _matmul(a, b, mode, name):
    if mode == "nn":
        (m, k), (_, n) = a.shape, b.shape
    elif mode == "nt":
        (m, k), (n, _) = a.shape, b.shape
    else:
        (k, m), (_, n) = a.shape, b.shape
    tm = _pick(m, (512, 384, 256, 128))
    tn = _pick(n, (1536, 1408, 1024, 768, 512, 256, 128))
    fits = lambda t: 2 * (tm * t * a.dtype.itemsize + t * tn * b.dtype.itemsize + tm * tn * 4) <= MATMUL_VMEM_BUDGET
    tk = _pick(k, tuple(t for t in (3840, 2816, 2560, 2304, 2048, 1920, 1408, 1152, 1024, 768, 512, 256, 128)
                        if fits(t)))
    nk = k // tk
    if mode == "tn":
        a_spec = pl.BlockSpec((tk, tm), lambda i, j, l: (l, i))
    else:
        a_spec = pl.BlockSpec((tm, tk), lambda i, j, l: (i, l))
    if mode == "nt":
        b_spec = pl.BlockSpec((tn, tk), lambda i, j, l: (j, l))
    else:
        b_spec = pl.BlockSpec((tk, tn), lambda i, j, l: (l, j))
    dims = _DIMS[mode]

    def body(a_ref, b_ref, o_ref):
        part = lax.dot_general(a_ref[...].astype(BF16), b_ref[...].astype(BF16), dims, preferred_element_type=F32)
        if nk == 1:
            o_ref[...] = part
        else:
            l = pl.program_id(2)

            @pl.when(l == 0)
            def _():
                o_ref[...] = part

            @pl.when(l > 0)
            def _():
                o_ref[...] += part

    return pl.pallas_call(
        body,
        name=name,
        grid=(m // tm, n // tn, nk),
        in_specs=[a_spec, b_spec],
        out_specs=pl.BlockSpec((tm, tn), lambda i, j, l: (i, j)),
        out_shape=jax.ShapeDtypeStruct((m, n), F32),
        compiler_params=pltpu.CompilerParams(dimension_semantics=("parallel", "parallel", "arbitrary"),
                                             vmem_limit_bytes=VMEM_LIMIT),
    )(a, b)


@functools.partial(jax.custom_vjp, nondiff_argnums=(3,))
def pmm(a, w, wz, name):
    del wz
    return _matmul(a, w, "nn", name + "_f")


def _pmm_fwd(a, w, wz, name):
    del wz
    return _matmul(a, w, "nn", name + "_f"), (a, w)


def _pmm_bwd(name, res, g):
    a, w = res
    da = _matmul(g, w, "nt", name + "_da")
    if a.shape[0] < 128:
        pad = 128 - a.shape[0]
        at = jnp.pad(a.T, ((0, 0), (0, pad)))
        gp = jnp.pad(g, ((0, pad), (0, 0)))
        dw = _matmul(at, gp, "nn", name + "_dw")
    else:
        dw = _matmul(a, g, "tn", name + "_dw")
    return da, jnp.zeros_like(w), dw


pmm.defvjp(_pmm_fwd, _pmm_bwd)


@functools.partial(jax.custom_vjp, nondiff_argnums=(3,))
def pmm_t(a, wt, wtz, name):
    del wtz
    return _matmul(a, wt, "nt", name + "_f")


def _pmm_t_fwd(a, wt, wtz, name):
    del wtz
    return _matmul(a, wt, "nt", name + "_f"), (a, wt)


def _pmm_t_bwd(name, res, g):
    a, wt = res
    return _matmul(g, wt, "nn", name + "_da"), jnp.zeros_like(wt), _matmul(g, a, "tn", name + "_dw")


pmm_t.defvjp(_pmm_t_fwd, _pmm_t_bwd)


def rowop(fn, name, rows, bcs=(), crows=(), cbcs=(), tr=256):
    rows, bcs, crows, cbcs = tuple(rows), tuple(bcs), tuple(crows), tuple(cbcs)
    n_rows = rows[0].shape[0]
    tr = _pick(n_rows, (tr, 128, 64, 32, 16, 8))
    nr, nb, ncr, ncb = len(rows), len(bcs), len(crows), len(cbcs)
    n_in = nr + nb + ncr + ncb
    grid = (n_rows // tr,)

    def blk(arr):
        return jax.ShapeDtypeStruct((tr, arr.shape[1]), arr.dtype)

    def row_spec(arr):
        return pl.BlockSpec((tr, arr.shape[1]), lambda i: (i, 0))

    def bc_spec(arr):
        return pl.BlockSpec(arr.shape, lambda i: (0, 0))

    out_blk = jax.eval_shape(fn, *[blk(r) for r in rows], *bcs, *[blk(r) for r in crows], *cbcs)
    n_out = len(out_blk)
    out_shape = tuple(jax.ShapeDtypeStruct((n_rows, o.shape[1]), o.dtype) for o in out_blk)
    in_specs = ([row_spec(r) for r in rows] + [bc_spec(b) for b in bcs]
                + [row_spec(r) for r in crows] + [bc_spec(b) for b in cbcs])

    def order(vals):
        return vals

    def fwd_call(args):
        def body(*refs):
            vals = [r[...] for r in refs[:n_in]]
            res = fn(*order(vals))
            for o_ref, r in zip(refs[n_in:], res):
                o_ref[...] = r

        return pl.pallas_call(
            body, name=name + "_f", grid=grid, in_specs=in_specs,
            out_specs=[row_spec(o) for o in out_shape], out_shape=out_shape,
            compiler_params=pltpu.CompilerParams(dimension_semantics=("parallel",), vmem_limit_bytes=VMEM_LIMIT),
        )(*args)

    def bwd_call(args, cts):
        def body(*refs):
            vals = [r[...] for r in refs[:n_in]]
            ct_refs = refs[n_in:n_in + n_out]
            d_rows = refs[n_in + n_out:n_in + n_out + nr]
            d_bcs = refs[n_in + n_out + nr:]
            consts = vals[nr + nb:]
            _, vjp = jax.vjp(lambda *p: fn(*p, *consts), *vals[:nr + nb])
            grads = vjp(tuple(c[...] for c in ct_refs))
            for ref, g in zip(d_rows, grads[:nr]):
                ref[...] = g

            @pl.when(pl.program_id(0) == 0)
            def _():
                for ref in d_bcs:
                    ref[...] = jnp.zeros_like(ref)

            for ref, g in zip(d_bcs, grads[nr:]):
                ref[...] += g

        d_shape = tuple(jax.ShapeDtypeStruct(r.shape, r.dtype) for r in rows + bcs)
        return pl.pallas_call(
            body, name=name + "_b", grid=grid,
            in_specs=in_specs + [row_spec(o) for o in out_shape],
            out_specs=[row_spec(r) for r in rows] + [bc_spec(b) for b in bcs], out_shape=d_shape,
            compiler_params=pltpu.CompilerParams(dimension_semantics=("arbitrary",), vmem_limit_bytes=VMEM_LIMIT),
        )(*args, *cts)

    @jax.custom_vjp
    def op(diff, const):
        return fwd_call(diff + const)

    def op_fwd(diff, const):
        return fwd_call(diff + const), (diff, const)

    def op_bwd(res, cts):
        diff, const = res
        grads = bwd_call(diff + const, tuple(cts))
        return tuple(grads), tuple(jnp.zeros_like(c) for c in const)

    op.defvjp(op_fwd, op_bwd)
    return op(rows + bcs, crows + cbcs)


def colop(fn, name, arrays, uses, n_const, nblk, cw=128):
    arrays = tuple(arrays)
    n_diff = len(arrays) - n_const
    nd = sum(1 for u in uses if u[0] < n_diff)
    assert all(u[0] < n_diff for u in uses[:nd]) and all(u[0] >= n_diff for u in uses[nd:])

    def spec(u):
        return pl.BlockSpec((arrays[u[0]].shape[0], cw), lambda j, off=u[1]: (0, off + j))

    def out_spec(rows):
        return pl.BlockSpec((rows, cw), lambda j: (0, j))

    out_blk = jax.eval_shape(fn, *[jax.ShapeDtypeStruct((arrays[u[0]].shape[0], cw), arrays[u[0]].dtype)
                                   for u in uses])
    out_shape = tuple(jax.ShapeDtypeStruct((o.shape[0], nblk * cw), o.dtype) for o in out_blk)
    params = pltpu.CompilerParams(dimension_semantics=("parallel",), vmem_limit_bytes=VMEM_LIMIT)

    def fwd_call(arrs):
        def body(*refs):
            res = fn(*[r[...] for r in refs[:len(uses)]])
            for o_ref, r in zip(refs[len(uses):], res):
                o_ref[...] = r

        return pl.pallas_call(
            body, name=name + "_f", grid=(nblk,), in_specs=[spec(u) for u in uses],
            out_specs=[out_spec(o.shape[0]) for o in out_shape], out_shape=out_shape, compiler_params=params,
        )(*[arrs[u[0]] for u in uses])

    def bwd_call(arrs, cts):
        def body(*refs):
            vals = [r[...] for r in refs[:len(uses)]]
            ct_refs = refs[len(uses):len(uses) + len(out_shape)]
            _, vjp = jax.vjp(lambda *p: fn(*p, *vals[nd:]), *vals[:nd])
            for ref, g in zip(refs[len(uses) + len(out_shape):], vjp(tuple(c[...] for c in ct_refs))):
                ref[...] = g

        d_shape = tuple(jax.ShapeDtypeStruct((arrays[u[0]].shape[0], nblk * cw), F32) for u in uses[:nd])
        return pl.pallas_call(
            body, name=name + "_b", grid=(nblk,),
            in_specs=[spec(u) for u in uses] + [out_spec(o.shape[0]) for o in out_shape],
            out_specs=[out_spec(s.shape[0]) for s in d_shape], out_shape=d_shape, compiler_params=params,
        )(*[arrs[u[0]] for u in uses], *cts)

    @jax.custom_vjp
    def op(diff, const):
        return fwd_call(diff + const)

    def op_fwd(diff, const):
        return fwd_call(diff + const), (diff, const)

    def op_bwd(res, cts):
        diff, const = res
        d_uses = bwd_call(diff + const, tuple(cts))
        grads = []
        for i in range(n_diff):
            parts = sorted([(u[1], k) for k, u in enumerate(uses[:nd]) if u[0] == i])
            grads.append(d_uses[parts[0][1]] if len(parts) == 1
                         else jnp.concatenate([d_uses[k] for _, k in parts], axis=1))
        return tuple(grads), tuple(jnp.zeros_like(c) for c in const)

    op.defvjp(op_fwd, op_bwd)
    return op(arrays[:n_diff], arrays[n_diff:])


@functools.partial(jax.custom_vjp, nondiff_argnums=(1,))
def _roll_rows(x, k):
    return pltpu.roll(x, k % x.shape[0], 0)


def _roll_rows_fwd(x, k):
    return _roll_rows(x, k), None


def _roll_rows_bwd(k, _, g):
    return (_roll_rows(g, -k),)


_roll_rows.defvjp(_roll_rows_fwd, _roll_rows_bwd)


def _conv3(x, w0, w1, w2, starts):
    rows = lax.broadcasted_iota(jnp.int32, x.shape, 0)
    ends = tuple(s - 1 for s in starts[1:]) + (x.shape[0] - 1,)
    first = functools.reduce(jnp.logical_or, [rows == s for s in starts])
    last = functools.reduce(jnp.logical_or, [rows == e for e in ends])
    prev = jnp.where(first, 0.0, _roll_rows(x, 1))
    nxt = jnp.where(last, 0.0, _roll_rows(x, -1))
    return prev * w0 + x * w1 + nxt * w2


def _rms(x):
    return x * lax.rsqrt(jnp.mean(x * x, axis=-1, keepdims=True) + NORM_EPS)


def _heads(x, n):
    return [x[:, h * HEAD_DIM:(h + 1) * HEAD_DIM] for h in range(n)]


_NT = (((1,), (1,)), ((), ()))
_TN = (((0,), (0,)), ((), ()))
_TQ = 256


def _attn_probs(q, k):
    s = lax.dot_general(q, k, _NT, preferred_element_type=F32) * (HEAD_DIM ** -0.5)
    p = jnp.exp(s - jnp.max(s, axis=-1, keepdims=True))
    return p / jnp.sum(p, axis=-1, keepdims=True)


def _attn_fwd_call(q, k, v):
    n, t = q.shape[0], k.shape[0]
    tq = _pick(n, (_TQ, 128))

    def body(q_ref, k_ref, v_ref, o_ref):
        p = _attn_probs(q_ref[...].astype(BF16), k_ref[...].astype(BF16))
        o_ref[...] = jnp.dot(p.astype(BF16), v_ref[...].astype(BF16), preferred_element_type=F32)

    return pl.pallas_call(
        body, name="attn_f", grid=(ATTN_HEADS, n // tq),
        in_specs=[pl.BlockSpec((tq, HEAD_DIM), lambda h, i: (i, h)),
                  pl.BlockSpec((t, HEAD_DIM), lambda h, i: (0, h // ATTN_GROUP)),
                  pl.BlockSpec((t, HEAD_DIM), lambda h, i: (0, h // ATTN_GROUP))],
        out_specs=pl.BlockSpec((tq, HEAD_DIM), lambda h, i: (i, h)),
        out_shape=jax.ShapeDtypeStruct(q.shape, F32),
        compiler_params=pltpu.CompilerParams(dimension_semantics=("parallel", "parallel"),
                                             vmem_limit_bytes=VMEM_LIMIT),
    )(q, k, v)


def _attn_bwd_call(q, k, v, do):
    n, t = q.shape[0], k.shape[0]
    tq = _pick(n, (_TQ, 128))

    def body(q_ref, k_ref, v_ref, do_ref, dq_ref, dk_ref, dv_ref):
        @pl.when((pl.program_id(1) == 0) & (pl.program_id(2) == 0))
        def _():
            dk_ref[...] = jnp.zeros_like(dk_ref)
            dv_ref[...] = jnp.zeros_like(dv_ref)

        qb, kb, vb, dob = (r[...].astype(BF16) for r in (q_ref, k_ref, v_ref, do_ref))
        p = _attn_probs(qb, kb)
        dp = lax.dot_general(dob, vb, _NT, preferred_element_type=F32)
        ds = p * (dp - jnp.sum(p * dp, axis=-1, keepdims=True)) * (HEAD_DIM ** -0.5)
        dsb = ds.astype(BF16)
        dq_ref[...] = jnp.dot(dsb, kb, preferred_element_type=F32)
        dk_ref[...] += lax.dot_general(dsb, qb, _TN, preferred_element_type=F32)
        dv_ref[...] += lax.dot_general(p.astype(BF16), dob, _TN, preferred_element_type=F32)

    q_spec = pl.BlockSpec((tq, HEAD_DIM), lambda kh, g, i: (i, kh * ATTN_GROUP + g))
    kv_spec = pl.BlockSpec((t, HEAD_DIM), lambda kh, g, i: (0, kh))
    return pl.pallas_call(
        body, name="attn_b", grid=(ATTN_KV_HEADS, ATTN_GROUP, n // tq),
        in_specs=[q_spec, kv_spec, kv_spec, q_spec],
        out_specs=[q_spec, kv_spec, kv_spec],
        out_shape=(jax.ShapeDtypeStruct(q.shape, F32), jax.ShapeDtypeStruct(k.shape, F32),
                   jax.ShapeDtypeStruct(v.shape, F32)),
        compiler_params=pltpu.CompilerParams(dimension_semantics=("parallel", "arbitrary", "arbitrary"),
                                             vmem_limit_bytes=VMEM_LIMIT),
    )(q, k, v, do)


@jax.custom_vjp
def attention(q, k, v):
    return _attn_fwd_call(q, k, v)


def _attention_fwd(q, k, v):
    return _attn_fwd_call(q, k, v), (q, k, v)


def _attention_bwd(res, do):
    return _attn_bwd_call(*res, do)


attention.defvjp(_attention_fwd, _attention_bwd)


_C = GDN_CHUNK


def _pdot(a, b):
    return jnp.dot(a, b, precision=lax.Precision.HIGH, preferred_element_type=F32)


@jax.custom_vjp
def _hdot(a, b):
    return jnp.dot(a.astype(BF16), b.astype(BF16), preferred_element_type=F32)


def _hdot_fwd(a, b):
    return _hdot(a, b), (a, b)


def _hdot_bwd(res, g):
    a, b = res
    gb = g.astype(BF16)
    return (lax.dot_general(gb, b.astype(BF16), _NT, preferred_element_type=F32),
            lax.dot_general(a.astype(BF16), gb, _TN, preferred_element_type=F32))


_hdot.defvjp(_hdot_fwd, _hdot_bwd)


def _each(fn, *lists):
    return [fn(*args) for args in zip(*lists)]


def _unit_lower_inverse(low, blockdiag):
    eye = (lax.broadcasted_iota(jnp.int32, (_C, _C), 0) == lax.broadcasted_iota(jnp.int32, (_C, _C), 1)).astype(F32)
    ld = _each(lambda a: a * blockdiag, low)
    lo = _each(lambda a, d: a - d, low, ld)
    l2 = _each(_hdot, ld, ld)
    l4 = _each(_hdot, l2, l2)
    l8 = _each(_hdot, l4, l4)
    td = _each(lambda d, a2: _hdot(eye - d, eye + a2), ld, l2)
    td = _each(lambda t, a4: _hdot(t, eye + a4), td, l4)
    td = _each(lambda t, a8: _hdot(t, eye + a8), td, l8)
    nn = _each(_hdot, td, lo)
    n2 = _each(_hdot, nn, nn)
    out = _each(lambda n, m2: _hdot(eye - n, eye + m2), nn, n2)
    return _each(_hdot, out, td)


def _gdn_chunks(heads, blockdiag):
    q, k, v, b_b, be_b, e_b, kd_b, m1, dec, gl, s = (list(col) for col in zip(*heads))
    f32dot = lambda a, b: jnp.dot(a, b, preferred_element_type=F32)
    nt = lambda a, b: lax.dot_general(a, b, _NT, preferred_element_type=F32)
    kk = _each(nt, k, k)
    t_inv = _unit_lower_inverse(_each(lambda m, a: m * a, m1, kk), blockdiag)
    u = _each(lambda t, b, x: _hdot(t, b * x), t_inv, b_b, v)
    w = _each(lambda t, b, x: _hdot(t, b * x), t_inv, be_b, k)
    delta = _each(lambda uu, ww, ss: uu - f32dot(ww, ss), u, w, s)
    p = _each(lambda d, qq, kx: d * nt(qq, kx), dec, q, k)
    o = _each(lambda qq, e, ss, pp, dd: f32dot(qq * e, ss) + f32dot(pp, dd), q, e_b, s, p, delta)
    s_new = _each(lambda g, ss, kx, kd, dd: g * ss + lax.dot_general(kx * kd, dd, _TN, preferred_element_type=F32),
                  gl, s, k, kd_b, delta)
    return o, s_new


def _blockdiag_mask():
    r = lax.broadcasted_iota(jnp.int32, (_C, _C), 0) >> 4
    c = lax.broadcasted_iota(jnp.int32, (_C, _C), 1) >> 4
    return (r == c).astype(F32)


def _gdn_specs(nc, ncc, reverse, backward):
    def ch(s):
        s = nc - 1 - s if backward else s
        return jnp.where(s < ncc, ncc - 1 - s, nc + ncc - 1 - s) if reverse else s

    tok = pl.BlockSpec((_C, 3 * GDN_W), lambda s: (ch(s), 0))
    out = pl.BlockSpec((_C, GDN_W), lambda s: (ch(s), 0))
    per_tok = pl.BlockSpec((GDN_HEADS, _C, HEAD_DIM), lambda s: (0, ch(s), 0))
    mat = pl.BlockSpec((GDN_HEADS, None, _C, _C), lambda s: (0, ch(s), 0, 0))
    row = pl.BlockSpec((GDN_HEADS, None, 1, HEAD_DIM), lambda s: (0, ch(s), 0, 0))
    state = pl.BlockSpec((GDN_HEADS, None, HEAD_DIM, HEAD_DIM), lambda s: (0, ch(s), 0, 0))
    return tok, out, per_tok, mat, row, state


def _head_cols(h, part):
    return slice((part * GDN_HEADS + h) * HEAD_DIM, (part * GDN_HEADS + h + 1) * HEAD_DIM)


def _gdn_fwd_call(name, reverse, ncc, qkv, b_b, be_b, e_b, kd_b, m1, dec, gl):
    t = qkv.shape[0]
    nc = t // _C
    tok, out, per_tok, mat, row, state = _gdn_specs(nc, ncc, reverse, False)

    def body(qkv_ref, b_ref, be_ref, e_ref, kd_ref, m1_ref, dec_ref, gl_ref, o_ref, sall_ref, s_ref):
        @pl.when(pl.program_id(0) == 0)
        def _():
            s_ref[...] = jnp.zeros_like(s_ref)

        bd = _blockdiag_mask()
        ins = [[qkv_ref[:, _head_cols(h, 0)], qkv_ref[:, _head_cols(h, 1)], qkv_ref[:, _head_cols(h, 2)],
                b_ref[h], be_ref[h], e_ref[h], kd_ref[h], m1_ref[h], dec_ref[h], gl_ref[h], s_ref[h]]
               for h in range(GDN_HEADS)]
        o, s_new = _gdn_chunks(ins, bd)
        for h in range(GDN_HEADS):
            sall_ref[h] = ins[h][10]
            o_ref[:, _head_cols(h, 0)] = o[h]
            s_ref[h] = s_new[h]

    return pl.pallas_call(
        body, name=name + "_f", grid=(nc,),
        in_specs=[tok, per_tok, per_tok, per_tok, per_tok, mat, mat, row],
        out_specs=[out, state],
        out_shape=(jax.ShapeDtypeStruct((t, GDN_W), F32),
                   jax.ShapeDtypeStruct((GDN_HEADS, nc, HEAD_DIM, HEAD_DIM), F32)),
        scratch_shapes=[pltpu.VMEM((GDN_HEADS, HEAD_DIM, HEAD_DIM), F32)],
        compiler_params=pltpu.CompilerParams(dimension_semantics=("arbitrary",), vmem_limit_bytes=VMEM_LIMIT),
    )(qkv, b_b, be_b, e_b, kd_b, m1, dec, gl)


def _gdn_bwd_call(name, reverse, ncc, qkv, b_b, be_b, e_b, kd_b, m1, dec, gl, sall, do):
    t = qkv.shape[0]
    nc = t // _C
    tok, out, per_tok, mat, row, state = _gdn_specs(nc, ncc, reverse, True)

    def body(qkv_ref, b_ref, be_ref, e_ref, kd_ref, m1_ref, dec_ref, gl_ref, sall_ref, do_ref,
             dqkv_ref, db_ref, dbe_ref, de_ref, dkd_ref, dm1_ref, ddec_ref, dgl_ref, ds_ref):
        @pl.when(pl.program_id(0) == 0)
        def _():
            ds_ref[...] = jnp.zeros_like(ds_ref)

        bd = _blockdiag_mask()
        ins = [[qkv_ref[:, _head_cols(h, 0)], qkv_ref[:, _head_cols(h, 1)], qkv_ref[:, _head_cols(h, 2)],
                b_ref[h], be_ref[h], e_ref[h], kd_ref[h], m1_ref[h], dec_ref[h], gl_ref[h], sall_ref[h]]
               for h in range(GDN_HEADS)]
        _, vjp = jax.vjp(lambda hs: _gdn_chunks(hs, bd), ins)
        (all_grads,) = vjp(([do_ref[:, _head_cols(h, 0)] for h in range(GDN_HEADS)],
                            [ds_ref[h] for h in range(GDN_HEADS)]))
        for h, grads in enumerate(all_grads):
            for part in range(3):
                dqkv_ref[:, _head_cols(h, part)] = grads[part]
            for ref, g in zip((db_ref, dbe_ref, de_ref, dkd_ref, dm1_ref, ddec_ref, dgl_ref), grads[3:10]):
                ref[h] = g
            ds_ref[h] = grads[10]

    shp = lambda a: jax.ShapeDtypeStruct(a.shape, F32)
    return pl.pallas_call(
        body, name=name + "_b", grid=(nc,),
        in_specs=[tok, per_tok, per_tok, per_tok, per_tok, mat, mat, row, state, out],
        out_specs=[tok, per_tok, per_tok, per_tok, per_tok, mat, mat, row],
        out_shape=(shp(qkv), shp(b_b), shp(be_b), shp(e_b), shp(kd_b), shp(m1), shp(dec), shp(gl)),
        scratch_shapes=[pltpu.VMEM((GDN_HEADS, HEAD_DIM, HEAD_DIM), F32)],
        compiler_params=pltpu.CompilerParams(dimension_semantics=("arbitrary",), vmem_limit_bytes=VMEM_LIMIT),
    )(qkv, b_b, be_b, e_b, kd_b, m1, dec, gl, sall, do)


@functools.partial(jax.custom_vjp, nondiff_argnums=(0, 1, 2))
def gdn_scan(name, reverse, ncc, qkv, b_b, be_b, e_b, kd_b, m1, dec, gl):
    return _gdn_fwd_call(name, reverse, ncc, qkv, b_b, be_b, e_b, kd_b, m1, dec, gl)[0]


def _gdn_scan_fwd(name, reverse, ncc, *args):
    o, sall = _gdn_fwd_call(name, reverse, ncc, *args)
    return o, (args, sall)


def _gdn_scan_bwd(name, reverse, ncc, res, do):
    args, sall = res
    return _gdn_bwd_call(name, reverse, ncc, *args, sall, do)


gdn_scan.defvjp(_gdn_scan_fwd, _gdn_scan_bwd)


def _rope_tables(n, cl):
    t = np.arange(n)
    inv_freq = (ROPE_THETA ** (-np.arange(0, HEAD_DIM // 2, 2, dtype=np.float32) / (HEAD_DIM // 2))).astype(np.float32)
    ang_r = (t // GRID_W).astype(np.float32)[:, None] * inv_freq
    ang_c = (t % GRID_W).astype(np.float32)[:, None] * inv_freq
    cos = np.concatenate([np.cos(ang_r), np.cos(ang_r), np.cos(ang_c), np.cos(ang_c)], axis=1)
    sin = np.concatenate([-np.sin(ang_r), np.sin(ang_r), -np.sin(ang_c), np.sin(ang_c)], axis=1)
    cos_all = np.concatenate([np.ones((cl, HEAD_DIM), np.float32), cos], axis=0)
    sin_all = np.concatenate([np.zeros((cl, HEAD_DIM), np.float32), sin], axis=0)
    j = np.arange(HEAD_DIM)
    src = np.where((j % 64) < 32, j + 32, j - 32)
    perm = np.zeros((HEAD_DIM, HEAD_DIM), np.float32)
    perm[src, j] = 1.0
    return (jnp.asarray(cos.astype(np.float32)), jnp.asarray(sin.astype(np.float32)),
            jnp.asarray(cos_all), jnp.asarray(sin_all), jnp.asarray(perm))


def _gdn_factors(log_a, beta, reverse):
    t = log_a.shape[0]
    nc = t // _C
    la = log_a.reshape(nc, _C, GDN_HEADS).transpose(2, 0, 1)
    be = beta.reshape(nc, _C, GDN_HEADS).transpose(2, 0, 1)
    gam = lax.cumsum(la, axis=2, reverse=reverse)
    idx = jnp.arange(_C)
    incl = (idx[:, None] <= idx[None, :]) if reverse else (idx[:, None] >= idx[None, :])
    strict = (idx[:, None] < idx[None, :]) if reverse else (idx[:, None] > idx[None, :])
    dec = jnp.exp(jnp.where(incl, gam[..., :, None] - gam[..., None, :], -jnp.inf))
    m1 = jnp.where(strict, be[..., :, None] * dec, 0.0)
    e = jnp.exp(gam)
    g_last = gam[..., :1] if reverse else gam[..., -1:]
    lanes = lambda a: jnp.broadcast_to(a.reshape(GDN_HEADS, t, 1), (GDN_HEADS, t, HEAD_DIM))
    gl = jnp.broadcast_to(jnp.exp(g_last)[..., None], (GDN_HEADS, nc, 1, HEAD_DIM))
    return lanes(be), lanes(be * e), lanes(e), lanes(jnp.exp(g_last - gam)), m1, dec, gl


def local_loss(x, wz, wb, ws, c, ctx, target):
    n, cl = x.shape[0], ctx.shape[0]
    cos_q, sin_q, cos_k, sin_k, perm = _rope_tables(n, cl)

    sc_in = jnp.concatenate([jax.nn.silu(c), jax.nn.silu(ws["c_ctx"])[None, :], jnp.zeros((14, D_MODEL), F32)], axis=0)
    mod = pmm(sc_in, wb["w_mod"], wz["w_mod"], "mm_mod") + ws["b_mod"]
    sh1, sc1, g1, sh2, sc2, g2 = [mod[0:1, i * D_MODEL:(i + 1) * D_MODEL] for i in range(6)]
    csh1, csc1 = mod[1:2, 0:D_MODEL], mod[1:2, D_MODEL:2 * D_MODEL]

    def norm_mod(a, sh, sc):
        return (_rms(a) * (1.0 + sc) + sh,)

    (hx,) = rowop(norm_mod, "normmod_x", (x,), (sh1, sc1))
    (hc,) = rowop(norm_mod, "normmod_c", (ctx,), (csh1, csc1))
    h_all = jnp.concatenate([hc, hx], axis=0)
    p_main = pmm_t(h_all, wb["w_in_main"], wz["w_in_main"], "mm_in")
    p_small = pmm_t(h_all, wb["w_in_small"], wz["w_in_small"], "mm_ins")
    ak, av, qkv, aq, z, gate = jnp.split(p_main, [KV_W, 2 * KV_W, SMALL_AT, SMALL_AT + Q_W, SMALL_AT + Q_W + GDN_W],
                                         axis=1)
    db, da = p_small[:, :2 * GDN_HEADS], p_small[:, 2 * GDN_HEADS:4 * GDN_HEADS]

    def qk_prep(nh):
        def fn(a, w, cos, sin, pm):
            outs = []
            for ah in _heads(a, nh):
                y = _rms(ah) * w
                outs.append(y * cos + _pdot(y, pm) * sin)
            return (jnp.concatenate(outs, axis=1),)
        return fn

    (q_x,) = rowop(qk_prep(ATTN_HEADS), "q_prep", (aq[cl:],), (ws["q_norm_w"],), (cos_q, sin_q), (perm,))
    (k_all,) = rowop(qk_prep(ATTN_KV_HEADS), "k_prep", (ak,), (ws["k_norm_w"],), (cos_k, sin_k), (perm,))
    attn_x = attention(q_x, k_all, av)

    cw = ws["conv_qkv_w"]
    normed = jnp.asarray(np.repeat([1.0, 1.0, 0.0], GDN_W)[None, :], F32)
    scale = jnp.asarray(np.repeat([HEAD_DIM ** -0.5, 1.0, 1.0], GDN_W)[None, :], F32)

    def gdn_prep(a, w0, w1, w2, nf, sc):
        s = jax.nn.silu(_conv3(a, w0, w1, w2, (0, cl)))
        inv = lax.rsqrt(jnp.sum(s * s, axis=-1, keepdims=True) + NORM_EPS)
        return (s * jnp.where(nf > 0.0, inv * sc, 1.0),)

    (qkvn,) = colop(gdn_prep, "gdn_prep", (qkv, cw[0:1], cw[1:2], cw[2:3], normed, scale),
                    [(i, 0) for i in range(6)], 2, 3 * GDN_HEADS)
    beta = jax.nn.sigmoid(db).reshape(-1, 2, GDN_HEADS)
    log_a = -jnp.exp(ws["a_log"])[None] * jax.nn.softplus(da.reshape(-1, 2, GDN_HEADS) + ws["dt_bias"][None])
    o_fwd = gdn_scan("gdn_d0", False, cl // _C, qkvn, *_gdn_factors(log_a[:, 0], beta[:, 0], False))
    o_bwd = gdn_scan("gdn_d1", True, cl // _C, qkvn, *_gdn_factors(log_a[:, 1], beta[:, 1], True))
    o_x = o_fwd[cl:] + o_bwd[cl:]

    def gdn_out(o, zz, w):
        outs = [_rms(oh) * w * jax.nn.silu(zh) for oh, zh in zip(_heads(o, GDN_HEADS), _heads(zz, GDN_HEADS))]
        return (jnp.concatenate(outs, axis=1),)

    (gdn_x,) = rowop(gdn_out, "gdn_out", (o_x, z[cl:]), (ws["gdn_norm_w"],))

    pa = pmm(attn_x, wb["w_pa"], wz["w_pa"], "mm_pa")
    pd = pmm(gdn_x, wb["w_pd"], wz["w_pd"], "mm_pd")

    def merge(a, d, g):
        return (jax.nn.sigmoid(g[:, :D_MODEL]) * a + jax.nn.sigmoid(g[:, D_MODEL:]) * d,)

    (y,) = rowop(merge, "merge", (pa, pd, gate[cl:]))
    mo = pmm(y, wb["w_out"], wz["w_out"], "mm_out")

    def res_norm_mod(xx, m, g, sh, sc):
        x1 = xx + g * m
        return x1, _rms(x1) * (1.0 + sc) + sh

    x1, h2 = rowop(res_norm_mod, "res1", (x, mo), (g1, sh2, sc2))
    up = pmm(h2, wb["w_up"], wz["w_up"], "mm_up")
    fw = ws["ffn_conv_w"]

    def ffn_act(ug, uv, w0g, w0v, w1g, w1v, w2g, w2v, bg, bv):
        g = _conv3(ug, w0g, w1g, w2g, (0,)) + bg
        v = _conv3(uv, w0v, w1v, w2v, (0,)) + bv
        return (jax.nn.silu(g) * v,)

    half = D_FF // HEAD_DIM
    (act,) = colop(ffn_act, "ffn_act", (up, fw[0:1], fw[1:2], fw[2:3], ws["ffn_conv_b"]),
                   [(i, off) for i in range(5) for off in (0, half)], 0, half)
    dn = pmm(act, wb["w_down"], wz["w_down"], "mm_down")

    def head(xx, m, g, w, tgt):
        yy = _rms(xx + g * m) * w
        err = (yy - tgt) ** 2
        return (jnp.broadcast_to(0.5 * jnp.mean(err, axis=-1, keepdims=True), (xx.shape[0], HEAD_DIM)),)

    (row_loss,) = rowop(head, "head", (x1, dn), (g2, ws["final_norm_w"][None, :]), (target,))
    return jnp.sum(row_loss[:, 0])


_HBM = pl.BlockSpec(memory_space=pltpu.HBM)


def _chip_peers():
    x, y = lax.axis_index("x"), lax.axis_index("y")
    return [(1 - x, y), (x, 1 - y), (1 - x, 1 - y)]


_SPLIT_COLS = ("w_in",)


def _half_of(view, nm, idx, lead=0):
    r, cdim = view.shape[-2:]
    pre = (slice(None),) * lead
    if nm in _SPLIT_COLS:
        return view.at[pre + (slice(None), pl.ds(pl.multiple_of(idx * (cdim // 2), 128), cdim // 2))]
    return view.at[pre + (pl.ds(pl.multiple_of(idx * (r // 2), 16), r // 2), slice(None))]


def _remote(src, dst, send_sem, recv_sem, dev):
    return pltpu.make_async_remote_copy(src_ref=src, dst_ref=dst, send_sem=send_sem, recv_sem=recv_sem,
                                        device_id=dev, device_id_type=MESH)


def _hbm_call(body, name, ins, out_shape, n_sems, in_place=False):
    names = tuple(ins)
    return dict(zip(names, pl.pallas_call(
        body, name=name, in_specs=[_HBM] * len(names), out_specs=[_HBM] * len(names),
        out_shape=[out_shape(nm, ins[nm]) for nm in names],
        scratch_shapes=[pltpu.SemaphoreType.DMA((k,)) for k in n_sems],
        input_output_aliases={i: i for i in range(len(names))} if in_place else {},
    )(*[ins[nm] for nm in names])))


def all_gather_chips(shards):
    names = tuple(shards)
    n = len(names)

    def body(*refs):
        ins, outs = dict(zip(names, refs[:n])), dict(zip(names, refs[n:2 * n]))
        ici_send, ici_recv, d2d_send, d2d_recv, own_send, own_recv = refs[2 * n:]
        x, y, c = lax.axis_index("x"), lax.axis_index("y"), lax.axis_index("c")
        me, sib = 2 * x + y, (x, y, 1 - c)
        own = [_remote(ins[nm], outs[nm].at[me], own_send.at[i], own_recv.at[i], sib) for i, nm in enumerate(names)]
        for cp in own:
            cp.start()
        sends = []
        for k, (px, py) in enumerate(_chip_peers()):
            for i, nm in enumerate(names):
                cp = _remote(_half_of(ins[nm], nm, c), _half_of(outs[nm].at[me], nm, c), ici_send.at[k * n + i],
                             ici_recv.at[k * n + i], (px, py, c))
                cp.start()
                sends.append(cp)
        for k, (px, py) in enumerate(_chip_peers()):
            for i, nm in enumerate(names):
                landed = _half_of(outs[nm].at[2 * px + py], nm, c)
                _remote(landed, landed, ici_send.at[k * n + i], ici_recv.at[k * n + i], (px, py, c)).wait_recv()
                fw = _remote(landed, landed, d2d_send.at[k * n + i], d2d_recv.at[k * n + i], sib)
                fw.start()
                sends.append(fw)
        for k, (px, py) in enumerate(_chip_peers()):
            for i, nm in enumerate(names):
                other = _half_of(outs[nm].at[2 * px + py], nm, 1 - c)
                _remote(other, other, d2d_send.at[k * n + i], d2d_recv.at[k * n + i], sib).wait_recv()
        for cp in sends:
            cp.wait_send()
        for cp in own:
            cp.wait()

    return _hbm_call(body, "ag_weights", shards, lambda nm, a: jax.ShapeDtypeStruct((N_CHIPS,) + a.shape, a.dtype),
                     (3 * n, 3 * n, 3 * n, 3 * n, n, n))


def sibling_halves(blocks):
    names = tuple(blocks)

    def body(*refs):
        n = len(names)
        ins, outs = dict(zip(names, refs[:n])), dict(zip(names, refs[n:2 * n]))
        send_sems, recv_sems = refs[2 * n:]
        x, y, c = lax.axis_index("x"), lax.axis_index("y"), lax.axis_index("c")
        cps = [_remote(_half_of(ins[nm], nm, 1 - c, lead=1), outs[nm], send_sems.at[i], recv_sems.at[i], (x, y, 1 - c))
               for i, nm in enumerate(names)]
        for cp in cps:
            cp.start()
        for cp in cps:
            cp.wait()

    def half_shape(nm, a):
        r, cdim = a.shape[-2:]
        return jax.ShapeDtypeStruct((N_CHIPS, r, cdim // 2) if nm in _SPLIT_COLS else (N_CHIPS, r // 2, cdim), a.dtype)

    return _hbm_call(body, "rs_sibling", blocks, half_shape, (len(names), len(names)))


def scatter_halves(blocks):
    names = tuple(blocks)
    n = len(names)

    def body(*refs):
        ins, outs = dict(zip(names, refs[:n])), dict(zip(names, refs[n:2 * n]))
        send_sems, recv_sems = refs[2 * n:]
        c = lax.axis_index("c")
        cps = [_remote(ins[nm].at[2 * px + py], outs[nm].at[k], send_sems.at[k * n + i], recv_sems.at[k * n + i],
                       (px, py, c))
               for k, (px, py) in enumerate(_chip_peers()) for i, nm in enumerate(names)]
        for cp in cps:
            cp.start()
        for cp in cps:
            cp.wait_recv()
        for cp in cps:
            cp.wait_send()

    return _hbm_call(body, "rs_grads", blocks, lambda nm, a: jax.ShapeDtypeStruct((3,) + a.shape[1:], a.dtype),
                     (3 * n, 3 * n))


def sibling_assemble(arrays):
    names = tuple(arrays)

    def body(*refs):
        n = len(names)
        ins, outs = dict(zip(names, refs[:n])), dict(zip(names, refs[n:2 * n]))
        send_sems, recv_sems = refs[2 * n:]
        x, y, c = lax.axis_index("x"), lax.axis_index("y"), lax.axis_index("c")
        cps = [_remote(_half_of(ins[nm], nm, c), _half_of(outs[nm], nm, c), send_sems.at[i], recv_sems.at[i],
                       (x, y, 1 - c)) for i, nm in enumerate(names)]
        for cp in cps:
            cp.start()
        for i, nm in enumerate(names):
            other = _half_of(outs[nm], nm, 1 - c)
            _remote(other, other, send_sems.at[i], recv_sems.at[i], (x, y, 1 - c)).wait_recv()
        for cp in cps:
            cp.wait_send()

    return _hbm_call(body, "rs_assemble", arrays, lambda nm, a: jax.ShapeDtypeStruct(a.shape, a.dtype),
                     (len(names), len(names)), in_place=True)


def all_reduce_small(v):
    def body(v_ref, tot_ref, gath_ref, send_sems, recv_sems):
        x, y, c = lax.axis_index("x"), lax.axis_index("y"), lax.axis_index("c")
        me = 4 * x + 2 * y + c
        gath_ref[me] = v_ref[...]

        def peer(k):
            m = k + 1
            return (x ^ (m >> 2 & 1), y ^ (m >> 1 & 1), c ^ (m & 1))

        sends = [pltpu.make_async_remote_copy(src_ref=v_ref, dst_ref=gath_ref.at[me], send_sem=send_sems.at[k],
                                              recv_sem=recv_sems.at[k], device_id=peer(k), device_id_type=MESH)
                 for k in range(N_DEV - 1)]
        for cp in sends:
            cp.start()
        for k in range(N_DEV - 1):
            px, py, pc = peer(k)
            pltpu.make_async_remote_copy(src_ref=v_ref, dst_ref=gath_ref.at[4 * px + 2 * py + pc],
                                         send_sem=send_sems.at[k], recv_sem=recv_sems.at[k], device_id=peer(k),
                                         device_id_type=MESH).wait_recv()
        for cp in sends:
            cp.wait_send()
        acc = gath_ref[0]
        for d in range(1, N_DEV):
            acc = acc + gath_ref[d]
        tot_ref[...] = acc

    vm = pl.BlockSpec(memory_space=pltpu.VMEM)
    return pl.pallas_call(
        body, name="ar_small", in_specs=[vm], out_specs=[vm, vm],
        out_shape=(jax.ShapeDtypeStruct(v.shape, v.dtype), jax.ShapeDtypeStruct((N_DEV,) + v.shape, v.dtype)),
        scratch_shapes=[pltpu.SemaphoreType.DMA((N_DEV - 1,)), pltpu.SemaphoreType.DMA((N_DEV - 1,))],
    )(v)[0]


def _elementwise(fn, name, ins, n_out, out_dtype=F32):
    r, cdim = ins[0].shape
    tr = _pick(r, tuple(p for p in (488, 256, 128, 104, 64, 32, 16, 8) if p * cdim * 4 <= 2 * 1024 * 1024))
    spec = pl.BlockSpec((tr, cdim), lambda i: (i, 0))

    def body(*refs):
        res = fn(*[ref[...] for ref in refs[:len(ins)]])
        for o_ref, v in zip(refs[len(ins):], res):
            o_ref[...] = v

    return pl.pallas_call(
        body, name=name, grid=(r // tr,), in_specs=[spec] * len(ins), out_specs=[spec] * n_out,
        out_shape=tuple(jax.ShapeDtypeStruct((r, cdim), out_dtype) for _ in range(n_out)),
        compiler_params=pltpu.CompilerParams(dimension_semantics=("parallel",), vmem_limit_bytes=VMEM_LIMIT),
    )(*ins)


def _half_block_specs(nm, shard_shape):
    r, cdim = shard_shape
    if nm in _SPLIT_COLS:
        return (None, r, cdim // 2), (lambda j, c: (j, 0, c))
    return (None, r // 2, cdim), (lambda j, c: (j, c, 0))


def _presum(nm, sel, g32, a):
    blk, at = _half_block_specs(nm, g32.shape[1:])

    def body(s_ref, g_ref, a_ref, o_ref):
        del s_ref
        o_ref[...] = (g_ref[...] + a_ref[...]).astype(BF16)

    return pl.pallas_call(
        body, name="rs_presum_" + nm,
        grid_spec=pltpu.PrefetchScalarGridSpec(
            num_scalar_prefetch=1, grid=(N_CHIPS,),
            in_specs=[pl.BlockSpec(blk, lambda j, s: at(j, s[0])), pl.BlockSpec(blk, lambda j, s: (j, 0, 0))],
            out_specs=pl.BlockSpec(blk, lambda j, s: (j, 0, 0))),
        out_shape=jax.ShapeDtypeStruct(a.shape, BF16),
        compiler_params=pltpu.CompilerParams(dimension_semantics=("parallel",), vmem_limit_bytes=VMEM_LIMIT),
    )(sel, g32, a)


def _finalsum(nm, sel, g32, a, got):
    blk, at = _half_block_specs(nm, g32.shape[1:])

    def body(s_ref, g_ref, a_ref, r_ref, o_ref):
        del s_ref
        acc = g_ref[...] + a_ref[...]
        for k in range(3):
            acc = acc + r_ref[k].astype(F32)
        o_ref[...] = acc

    return pl.pallas_call(
        body, name="rs_final_" + nm,
        grid_spec=pltpu.PrefetchScalarGridSpec(
            num_scalar_prefetch=1, grid=(1,),
            in_specs=[pl.BlockSpec(blk, lambda i, s: at(s[1], s[0])), pl.BlockSpec(blk, lambda i, s: (s[1], 0, 0)),
                      pl.BlockSpec(got.shape, lambda i, s: (0, 0, 0))],
            out_specs=pl.BlockSpec(blk[1:], lambda i, s: at(0, s[0])[1:])),
        out_shape=jax.ShapeDtypeStruct(g32.shape[1:], F32),
        compiler_params=pltpu.CompilerParams(dimension_semantics=("arbitrary",), vmem_limit_bytes=VMEM_LIMIT),
    )(sel, g32, a, got)


def _adamw(w, g, m, v, name):
    shape = w.shape
    to2 = lambda a: a.reshape(-1, shape[-1])

    def fn(w_, g_, m_, v_):
        m_new = ADAM_B1 * m_ + (1.0 - ADAM_B1) * g_
        v_new = ADAM_B2 * v_ + (1.0 - ADAM_B2) * (g_ * g_)
        m_hat = m_new / (1.0 - ADAM_B1 ** ADAM_STEP)
        v_hat = v_new / (1.0 - ADAM_B2 ** ADAM_STEP)
        delta = -ADAM_LR * (m_hat / (jnp.sqrt(v_hat) + ADAM_EPS) + ADAM_WD * w_)
        return delta, m_new, v_new

    outs = _elementwise(fn, name, [to2(a) for a in (w, g, m, v)], 3)
    return tuple(o.reshape(shape) for o in outs)


_BIG = ("w_mod", "w_in", "w_pa", "w_pd", "w_out", "w_up", "w_down")
_COL_SHARDED = ("w_mod", "w_up")
_FULL_SHAPE = {"w_mod": (D_MODEL, MOD_W), "w_in": (IN_COLS, D_MODEL), "w_pa": (Q_W, D_MODEL), "w_pd": (GDN_W, D_MODEL),
               "w_out": (D_MODEL, D_MODEL), "w_up": (D_MODEL, 2 * D_FF), "w_down": (D_FF, D_MODEL)}


def _shard_shape(name):
    r, cdim = _FULL_SHAPE[name]
    return (r, cdim // N_CHIPS) if name in _COL_SHARDED else (r // N_CHIPS, cdim)


_CONV_ELEMS = 2 * (3 * CONV_W // N_CHIPS + 3 * 2 * D_FF // N_CHIPS)
_CONV_ROWS = 32


def _blocks_of_full(name, full):
    r, cdim = _FULL_SHAPE[name]
    if name in _COL_SHARDED:
        return full.reshape(r, N_CHIPS, cdim // N_CHIPS).transpose(1, 0, 2)
    return full.reshape(N_CHIPS, r // N_CHIPS, cdim)


def _full_of_blocks(name, blocks):
    r, cdim = _FULL_SHAPE[name]
    if name in _COL_SHARDED:
        return blocks.transpose(1, 0, 2).reshape(r, cdim)
    return blocks.reshape(r, cdim)


def _w_in_regroup(w_in_t):
    main = jnp.concatenate([w_in_t[:SMALL_AT], w_in_t[SMALL_AT + 4 * GDN_HEADS:]], axis=0)
    small = jnp.pad(w_in_t[SMALL_AT:SMALL_AT + 4 * GDN_HEADS], ((0, HEAD_DIM - 4 * GDN_HEADS), (0, 0)))
    return main, small


def _w_in_ungroup(main, small):
    return jnp.concatenate([main[:SMALL_AT], small[:4 * GDN_HEADS], main[SMALL_AT:]], axis=0)


_SMALL = ("c_ctx", "b_mod", "q_norm_w", "k_norm_w", "conv_qkv_w", "a_log", "dt_bias", "gdn_norm_w", "ffn_conv_w",
          "ffn_conv_b", "final_norm_w")


def _pack_small(tree, rows):
    flat = jnp.concatenate([tree[nm].reshape(-1) for nm in _SMALL])
    return jnp.pad(flat, (0, rows * 128 - flat.shape[0])).reshape(rows, 128)


def _unpack_small(packed, like):
    flat, out, off = packed.reshape(-1), {}, 0
    for nm in _SMALL:
        size = int(np.prod(like[nm].shape))
        out[nm] = flat[off:off + size].reshape(like[nm].shape)
        off += size
    return out


def kernel(x, c, ctx, c_ctx, w_mod, b_mod, w_in, q_norm_w, k_norm_w, conv_qkv_w, a_log, dt_bias, gdn_norm_w, w_pa, w_pd, w_out, w_up, ffn_conv_w, ffn_conv_b, w_down, final_norm_w, loss_target, m_c_ctx, m_w_mod, m_b_mod, m_w_in, m_q_norm_w, m_k_norm_w, m_conv_qkv_w, m_a_log, m_dt_bias, m_gdn_norm_w, m_w_pa, m_w_pd, m_w_out, m_w_up, m_ffn_conv_w, m_ffn_conv_b, m_w_down, m_final_norm_w, v_c_ctx, v_w_mod, v_b_mod, v_w_in, v_q_norm_w, v_k_norm_w, v_conv_qkv_w, v_a_log, v_dt_bias, v_gdn_norm_w, v_w_pa, v_w_pd, v_w_out, v_w_up, v_ffn_conv_w, v_ffn_conv_b, v_w_down, v_final_norm_w):
    names = ("c_ctx", "w_mod", "b_mod", "w_in", "q_norm_w", "k_norm_w", "conv_qkv_w", "a_log", "dt_bias", "gdn_norm_w",
             "w_pa", "w_pd", "w_out", "w_up", "ffn_conv_w", "ffn_conv_b", "w_down", "final_norm_w")
    w_sh = dict(c_ctx=c_ctx, w_mod=w_mod, b_mod=b_mod, w_in=w_in, q_norm_w=q_norm_w, k_norm_w=k_norm_w,
                conv_qkv_w=conv_qkv_w, a_log=a_log, dt_bias=dt_bias, gdn_norm_w=gdn_norm_w, w_pa=w_pa, w_pd=w_pd,
                w_out=w_out, w_up=w_up, ffn_conv_w=ffn_conv_w, ffn_conv_b=ffn_conv_b, w_down=w_down,
                final_norm_w=final_norm_w)
    m_sh = dict(c_ctx=m_c_ctx, w_mod=m_w_mod, b_mod=m_b_mod, w_in=m_w_in, q_norm_w=m_q_norm_w, k_norm_w=m_k_norm_w,
                conv_qkv_w=m_conv_qkv_w, a_log=m_a_log, dt_bias=m_dt_bias, gdn_norm_w=m_gdn_norm_w, w_pa=m_w_pa,
                w_pd=m_w_pd, w_out=m_w_out, w_up=m_w_up, ffn_conv_w=m_ffn_conv_w, ffn_conv_b=m_ffn_conv_b,
                w_down=m_w_down, final_norm_w=m_final_norm_w)
    v_sh = dict(c_ctx=v_c_ctx, w_mod=v_w_mod, b_mod=v_b_mod, w_in=v_w_in, q_norm_w=v_q_norm_w, k_norm_w=v_k_norm_w,
                conv_qkv_w=v_conv_qkv_w, a_log=v_a_log, dt_bias=v_dt_bias, gdn_norm_w=v_gdn_norm_w, w_pa=v_w_pa,
                w_pd=v_w_pd, w_out=v_w_out, w_up=v_w_up, ffn_conv_w=v_ffn_conv_w, ffn_conv_b=v_ffn_conv_b,
                w_down=v_w_down, final_norm_w=v_final_norm_w)
    chip = 2 * lax.axis_index("x") + lax.axis_index("y")

    conv_bits = jnp.concatenate([lax.bitcast_convert_type(w_sh[nm][0], BF16).reshape(-1)
                                 for nm in ("conv_qkv_w", "ffn_conv_w")])
    shards = {nm: w_sh[nm][0].astype(BF16).T if nm == "w_in" else w_sh[nm][0].astype(BF16) for nm in _BIG}
    shards["conv"] = jnp.pad(conv_bits, (0, _CONV_ROWS * D_MODEL - _CONV_ELEMS)).reshape(_CONV_ROWS, D_MODEL)
    gathered = all_gather_chips(shards)
    wb = {nm: _full_of_blocks(nm, gathered[nm]) for nm in _BIG}
    wb["w_in_main"], wb["w_in_small"] = _w_in_regroup(wb.pop("w_in"))
    conv_all = gathered["conv"].reshape(N_CHIPS, -1)[:, :_CONV_ELEMS]
    n_cq = 2 * 3 * CONV_W // N_CHIPS
    unbits = lambda a, w: lax.bitcast_convert_type(a.reshape(N_CHIPS, 3, w // N_CHIPS, 2), F32).transpose(1, 0, 2).reshape(3, w)
    ws = dict(c_ctx=c_ctx, b_mod=b_mod, q_norm_w=q_norm_w, k_norm_w=k_norm_w, a_log=a_log[0], dt_bias=dt_bias[0],
              gdn_norm_w=gdn_norm_w, ffn_conv_b=ffn_conv_b, final_norm_w=final_norm_w,
              conv_qkv_w=unbits(conv_all[:, :n_cq], CONV_W), ffn_conv_w=unbits(conv_all[:, n_cq:], 2 * D_FF))
    wz = {nm: jnp.zeros(a.shape, F32) for nm, a in wb.items()}

    loss_local, (gx, gz, gs) = jax.value_and_grad(local_loss, argnums=(0, 1, 3))(
        x[0], wz, wb, ws, c, ctx[0], loss_target[0])
    loss = lax.psum(loss_local, ("x", "y", "c"))

    gz["w_in"] = _w_in_ungroup(gz.pop("w_in_main"), gz.pop("w_in_small"))
    g32 = {nm: _blocks_of_full(nm, gz[nm]) for nm in _BIG}
    sel = jnp.stack([lax.axis_index("c"), chip]).astype(jnp.int32)
    theirs = sibling_halves(g32)
    got = scatter_halves({nm: _presum(nm, sel, g32[nm], theirs[nm]) for nm in _BIG})
    g_big = sibling_assemble({nm: _finalsum(nm, sel, g32[nm], theirs[nm], got[nm]) for nm in _BIG})

    gs["a_log"], gs["dt_bias"] = gs["a_log"][None], gs["dt_bias"][None]
    like = {nm: gs[nm] for nm in _SMALL}
    small_rows = -(-sum(int(np.prod(like[nm].shape)) for nm in _SMALL) // 1024) * 8
    g_small = _unpack_small(all_reduce_small(_pack_small(gs, small_rows)), like)
    for nm, width in (("conv_qkv_w", CONV_W), ("ffn_conv_w", 2 * D_FF)):
        g_small[nm] = lax.dynamic_slice_in_dim(g_small[nm], chip * (width // N_CHIPS), width // N_CHIPS, axis=1)[None]

    grads, deltas, new_m, new_v = {}, {}, {}, {}
    for nm in _BIG:
        g = g_big[nm].T if nm == "w_in" else g_big[nm]
        grads[nm] = g[None]
        deltas[nm], new_m[nm], new_v[nm] = (o[None] for o in _adamw(w_sh[nm][0], g, m_sh[nm][0], v_sh[nm][0],
                                                                     "adamw_" + nm))
    shard_like = {nm: w_sh[nm] for nm in _SMALL}
    rows_l = -(-sum(int(np.prod(shard_like[nm].shape)) for nm in _SMALL) // 1024) * 8
    g_l = _pack_small({nm: g_small[nm].reshape(w_sh[nm].shape) for nm in _SMALL}, rows_l)
    outs = _adamw(_pack_small(w_sh, rows_l), g_l, _pack_small(m_sh, rows_l), _pack_small(v_sh, rows_l), "adamw_small")
    grads.update(_unpack_small(g_l, shard_like))
    for tree, packed in zip((deltas, new_m, new_v), outs):
        tree.update(_unpack_small(packed, shard_like))

    return (loss, gx[None], *[grads[nm] for nm in names], *[deltas[nm] for nm in names],
            *[new_m[nm] for nm in names], *[new_v[nm] for nm in names])
```

```python
import functools
import math

import jax
import jax.numpy as jnp
import numpy as np
from jax import lax
from jax.experimental import pallas as pl
from jax.experimental.pallas import tpu as pltpu

F32 = jnp.float32
BF16 = jnp.bfloat16
HIGHEST = lax.Precision.HIGHEST
MESH = pl.DeviceIdType.MESH

D_MODEL = 1024
GRID_W = 64
ATTN_HEADS = 8
ATTN_KV_HEADS = 2
ATTN_GROUP = ATTN_HEADS // ATTN_KV_HEADS
HEAD_DIM = 128
ROPE_THETA = 10000.0
GDN_HEADS = 8
GDN_CHUNK = 64
D_FF = 2816
NORM_EPS = 1e-6
KV_W = ATTN_KV_HEADS * HEAD_DIM
Q_W = ATTN_HEADS * HEAD_DIM
GDN_W = GDN_HEADS * HEAD_DIM
CONV_W = 3 * GDN_W
MOD_W = 6 * D_MODEL
IN_COLS = 2 * KV_W + CONV_W + 4 * GDN_HEADS + Q_W + GDN_W + 2 * D_MODEL
IN_MAIN = IN_COLS - 4 * GDN_HEADS
SMALL_AT = 2 * KV_W + CONV_W
N_CHIPS = 4
N_DEV = 8

ADAM_LR = 0.001
ADAM_B1 = 0.9
ADAM_B2 = 0.999
ADAM_EPS = 1e-08
ADAM_WD = 0.01
ADAM_STEP = 10

VMEM_LIMIT = 48 * 1024 * 1024
MATMUL_VMEM_BUDGET = 32 * 1024 * 1024


def _pick(dim, prefs):
    for p in prefs:
        if p <= dim and dim % p == 0:
            return p
    return dim


_DIMS = {
    "nn": (((1,), (0,)), ((), ())),
    "nt": (((1,), (1,)), ((), ())),
    "tn": (((0,), (0,)), ((), ())),
}


def _matmul(a, b, mode, name):
    if mode == "nn":
        (m, k), (_, n) = a.shape, b.shape
    elif mode == "nt":
        (m, k), (n, _) = a.shape, b.shape
    else:
        (k, m), (_, n) = a.shape, b.shape
    tm = _pick(m, (512, 384, 256, 128))
    tn = _pick(n, (1536, 1408, 1024, 768, 512, 256, 128))
    fits = lambda t: 2 * (tm * t * a.dtype.itemsize + t * tn * b.dtype.itemsize + tm * tn * 4) <= MATMUL_VMEM_BUDGET
    tk = _pick(k, tuple(t for t in (3840, 2816, 2560, 2304, 2048, 1920, 1408, 1152, 1024, 768, 512, 256, 128)
                        if fits(t)))
    nk = k // tk
    if mode == "tn":
        a_spec = pl.BlockSpec((tk, tm), lambda i, j, l: (l, i))
    else:
        a_spec = pl.BlockSpec((tm, tk), lambda i, j, l: (i, l))
    if mode == "nt":
        b_spec = pl.BlockSpec((tn, tk), lambda i, j, l: (j, l))
    else:
        b_spec = pl.BlockSpec((tk, tn), lambda i, j, l: (l, j))
    dims = _DIMS[mode]

    def body(a_ref, b_ref, o_ref):
        part = lax.dot_general(a_ref[...].astype(BF16), b_ref[...].astype(BF16), dims, preferred_element_type=F32)
        if nk == 1:
            o_ref[...] = part
        else:
            l = pl.program_id(2)

            @pl.when(l == 0)
            def _():
                o_ref[...] = part

            @pl.when(l > 0)
            def _():
                o_ref[...] += part

    return pl.pallas_call(
        body,
        name=name,
        grid=(m // tm, n // tn, nk),
        in_specs=[a_spec, b_spec],
        out_specs=pl.BlockSpec((tm, tn), lambda i, j, l: (i, j)),
        out_shape=jax.ShapeDtypeStruct((m, n), F32),
        compiler_params=pltpu.CompilerParams(dimension_semantics=("parallel", "parallel", "arbitrary"),
                                             vmem_limit_bytes=VMEM_LIMIT),
    )(a, b)


@functools.partial(jax.custom_vjp, nondiff_argnums=(3,))
def pmm(a, w, wz, name):
    del wz
    return _matmul(a, w, "nn", name + "_f")


def _pmm_fwd(a, w, wz, name):
    del wz
    return _matmul(a, w, "nn", name + "_f"), (a, w)


def _pmm_bwd(name, res, g):
    a, w = res
    da = _matmul(g, w, "nt", name + "_da")
    if a.shape[0] < 128:
        pad = 128 - a.shape[0]
        at = jnp.pad(a.T, ((0, 0), (0, pad)))
        gp = jnp.pad(g, ((0, pad), (0, 0)))
        dw = _matmul(at, gp, "nn", name + "_dw")
    else:
        dw = _matmul(a, g, "tn", name + "_dw")
    return da, jnp.zeros_like(w), dw


pmm.defvjp(_pmm_fwd, _pmm_bwd)


@functools.partial(jax.custom_vjp, nondiff_argnums=(3,))
def pmm_t(a, wt, wtz, name):
    del wtz
    return _matmul(a, wt, "nt", name + "_f")


def _pmm_t_fwd(a, wt, wtz, name):
    del wtz
    return _matmul(a, wt, "nt", name + "_f"), (a, wt)


def _pmm_t_bwd(name, res, g):
    a, wt = res
    return _matmul(g, wt, "nn", name + "_da"), jnp.zeros_like(wt), _matmul(g, a, "tn", name + "_dw")


pmm_t.defvjp(_pmm_t_fwd, _pmm_t_bwd)


def rowop(fn, name, rows, bcs=(), crows=(), cbcs=(), tr=256):
    rows, bcs, crows, cbcs = tuple(rows), tuple(bcs), tuple(crows), tuple(cbcs)
    n_rows = rows[0].shape[0]
    tr = _pick(n_rows, (tr, 128, 64, 32, 16, 8))
    nr, nb, ncr, ncb = len(rows), len(bcs), len(crows), len(cbcs)
    n_in = nr + nb + ncr + ncb
    grid = (n_rows // tr,)

    def blk(arr):
        return jax.ShapeDtypeStruct((tr, arr.shape[1]), arr.dtype)

    def row_spec(arr):
        return pl.BlockSpec((tr, arr.shape[1]), lambda i: (i, 0))

    def bc_spec(arr):
        return pl.BlockSpec(arr.shape, lambda i: (0, 0))

    out_blk = jax.eval_shape(fn, *[blk(r) for r in rows], *bcs, *[blk(r) for r in crows], *cbcs)
    n_out = len(out_blk)
    out_shape = tuple(jax.ShapeDtypeStruct((n_rows, o.shape[1]), o.dtype) for o in out_blk)
    in_specs = ([row_spec(r) for r in rows] + [bc_spec(b) for b in bcs]
                + [row_spec(r) for r in crows] + [bc_spec(b) for b in cbcs])

    def order(vals):
        return vals

    def fwd_call(args):
        def body(*refs):
            vals = [r[...] for r in refs[:n_in]]
            res = fn(*order(vals))
            for o_ref, r in zip(refs[n_in:], res):
                o_ref[...] = r

        return pl.pallas_call(
            body, name=name + "_f", grid=grid, in_specs=in_specs,
            out_specs=[row_spec(o) for o in out_shape], out_shape=out_shape,
            compiler_params=pltpu.CompilerParams(dimension_semantics=("parallel",), vmem_limit_bytes=VMEM_LIMIT),
        )(*args)

    def bwd_call(args, cts):
        def body(*refs):
            vals = [r[...] for r in refs[:n_in]]
            ct_refs = refs[n_in:n_in + n_out]
            d_rows = refs[n_in + n_out:n_in + n_out + nr]
            d_bcs = refs[n_in + n_out + nr:]
            consts = vals[nr + nb:]
            _, vjp = jax.vjp(lambda *p: fn(*p, *consts), *vals[:nr + nb])
            grads = vjp(tuple(c[...] for c in ct_refs))
            for ref, g in zip(d_rows, grads[:nr]):
                ref[...] = g

            @pl.when(pl.program_id(0) == 0)
            def _():
                for ref in d_bcs:
                    ref[...] = jnp.zeros_like(ref)

            for ref, g in zip(d_bcs, grads[nr:]):
                ref[...] += g

        d_shape = tuple(jax.ShapeDtypeStruct(r.shape, r.dtype) for r in rows + bcs)
        return pl.pallas_call(
            body, name=name + "_b", grid=grid,
            in_specs=in_specs + [row_spec(o) for o in out_shape],
            out_specs=[row_spec(r) for r in rows] + [bc_spec(b) for b in bcs], out_shape=d_shape,
            compiler_params=pltpu.CompilerParams(dimension_semantics=("arbitrary",), vmem_limit_bytes=VMEM_LIMIT),
        )(*args, *cts)

    @jax.custom_vjp
    def op(diff, const):
        return fwd_call(diff + const)

    def op_fwd(diff, const):
        return fwd_call(diff + const), (diff, const)

    def op_bwd(res, cts):
        diff, const = res
        grads = bwd_call(diff + const, tuple(cts))
        return tuple(grads), tuple(jnp.zeros_like(c) for c in const)

    op.defvjp(op_fwd, op_bwd)
    return op(rows + bcs, crows + cbcs)


def colop(fn, name, arrays, uses, n_const, nblk, cw=128):
    arrays = tuple(arrays)
    n_diff = len(arrays) - n_const
    nd = sum(1 for u in uses if u[0] < n_diff)
    assert all(u[0] < n_diff for u in uses[:nd]) and all(u[0] >= n_diff for u in uses[nd:])

    def spec(u):
        return pl.BlockSpec((arrays[u[0]].shape[0], cw), lambda j, off=u[1]: (0, off + j))

    def out_spec(rows):
        return pl.BlockSpec((rows, cw), lambda j: (0, j))

    out_blk = jax.eval_shape(fn, *[jax.ShapeDtypeStruct((arrays[u[0]].shape[0], cw), arrays[u[0]].dtype)
                                   for u in uses])
    out_shape = tuple(jax.ShapeDtypeStruct((o.shape[0], nblk * cw), o.dtype) for o in out_blk)
    params = pltpu.CompilerParams(dimension_semantics=("parallel",), vmem_limit_bytes=VMEM_LIMIT)

    def fwd_call(arrs):
        def body(*refs):
            res = fn(*[r[...] for r in refs[:len(uses)]])
            for o_ref, r in zip(refs[len(uses):], res):
                o_ref[...] = r

        return pl.pallas_call(
            body, name=name + "_f", grid=(nblk,), in_specs=[spec(u) for u in uses],
            out_specs=[out_spec(o.shape[0]) for o in out_shape], out_shape=out_shape, compiler_params=params,
        )(*[arrs[u[0]] for u in uses])

    def bwd_call(arrs, cts):
        def body(*refs):
            vals = [r[...] for r in refs[:len(uses)]]
            ct_refs = refs[len(uses):len(uses) + len(out_shape)]
            _, vjp = jax.vjp(lambda *p: fn(*p, *vals[nd:]), *vals[:nd])
            for ref, g in zip(refs[len(uses) + len(out_shape):], vjp(tuple(c[...] for c in ct_refs))):
                ref[...] = g

        d_shape = tuple(jax.ShapeDtypeStruct((arrays[u[0]].shape[0], nblk * cw), F32) for u in uses[:nd])
        return pl.pallas_call(
            body, name=name + "_b", grid=(nblk,),
            in_specs=[spec(u) for u in uses] + [out_spec(o.shape[0]) for o in out_shape],
            out_specs=[out_spec(s.shape[0]) for s in d_shape], out_shape=d_shape, compiler_params=params,
        )(*[arrs[u[0]] for u in uses], *cts)

    @jax.custom_vjp
    def op(diff, const):
        return fwd_call(diff + const)

    def op_fwd(diff, const):
        return fwd_call(diff + const), (diff, const)

    def op_bwd(res, cts):
        diff, const = res
        d_uses = bwd_call(diff + const, tuple(cts))
        grads = []
        for i in range(n_diff):
            parts = sorted([(u[1], k) for k, u in enumerate(uses[:nd]) if u[0] == i])
            grads.append(d_uses[parts[0][1]] if len(parts) == 1
                         else jnp.concatenate([d_uses[k] for _, k in parts], axis=1))
        return tuple(grads), tuple(jnp.zeros_like(c) for c in const)

    op.defvjp(op_fwd, op_bwd)
    return op(arrays[:n_diff], arrays[n_diff:])


@functools.partial(jax.custom_vjp, nondiff_argnums=(1,))
def _roll_rows(x, k):
    return pltpu.roll(x, k % x.shape[0], 0)


def _roll_rows_fwd(x, k):
    return _roll_rows(x, k), None


def _roll_rows_bwd(k, _, g):
    return (_roll_rows(g, -k),)


_roll_rows.defvjp(_roll_rows_fwd, _roll_rows_bwd)


def _conv3(x, w0, w1, w2, starts):
    rows = lax.broadcasted_iota(jnp.int32, x.shape, 0)
    ends = tuple(s - 1 for s in starts[1:]) + (x.shape[0] - 1,)
    first = functools.reduce(jnp.logical_or, [rows == s for s in starts])
    last = functools.reduce(jnp.logical_or, [rows == e for e in ends])
    prev = jnp.where(first, 0.0, _roll_rows(x, 1))
    nxt = jnp.where(last, 0.0, _roll_rows(x, -1))
    return prev * w0 + x * w1 + nxt * w2


def _rms(x):
    return x * lax.rsqrt(jnp.mean(x * x, axis=-1, keepdims=True) + NORM_EPS)


def _heads(x, n):
    return [x[:, h * HEAD_DIM:(h + 1) * HEAD_DIM] for h in range(n)]


_NT = (((1,), (1,)), ((), ()))
_TN = (((0,), (0,)), ((), ()))
_TQ = 256


def _attn_probs(q, k):
    s = lax.dot_general(q, k, _NT, preferred_element_type=F32) * (HEAD_DIM ** -0.5)
    p = jnp.exp(s - jnp.max(s, axis=-1, keepdims=True))
    return p * (1.0 / jnp.sum(p, axis=-1, keepdims=True))


def _attn_fwd_call(q, k, v):
    n, t = q.shape[0], k.shape[0]
    tq = _pick(n, (_TQ, 128))

    def body(q_ref, k_ref, v_ref, o_ref):
        p = _attn_probs(q_ref[...].astype(BF16), k_ref[...].astype(BF16))
        o_ref[...] = jnp.dot(p.astype(BF16), v_ref[...].astype(BF16), preferred_element_type=F32)

    return pl.pallas_call(
        body, name="attn_f", grid=(ATTN_HEADS, n // tq),
        in_specs=[pl.BlockSpec((tq, HEAD_DIM), lambda h, i: (i, h)),
                  pl.BlockSpec((t, HEAD_DIM), lambda h, i: (0, h // ATTN_GROUP)),
                  pl.BlockSpec((t, HEAD_DIM), lambda h, i: (0, h // ATTN_GROUP))],
        out_specs=pl.BlockSpec((tq, HEAD_DIM), lambda h, i: (i, h)),
        out_shape=jax.ShapeDtypeStruct(q.shape, F32),
        compiler_params=pltpu.CompilerParams(dimension_semantics=("parallel", "parallel"),
                                             vmem_limit_bytes=VMEM_LIMIT),
    )(q, k, v)


def _attn_bwd_call(q, k, v, do):
    n, t = q.shape[0], k.shape[0]
    tq = _pick(n, (_TQ, 128))

    def body(q_ref, k_ref, v_ref, do_ref, dq_ref, dk_ref, dv_ref):
        @pl.when((pl.program_id(1) == 0) & (pl.program_id(2) == 0))
        def _():
            dk_ref[...] = jnp.zeros_like(dk_ref)
            dv_ref[...] = jnp.zeros_like(dv_ref)

        qb, kb, vb, dob = (r[...].astype(BF16) for r in (q_ref, k_ref, v_ref, do_ref))
        p = _attn_probs(qb, kb)
        dp = lax.dot_general(dob, vb, _NT, preferred_element_type=F32)
        ds = p * (dp - jnp.sum(p * dp, axis=-1, keepdims=True)) * (HEAD_DIM ** -0.5)
        dsb = ds.astype(BF16)
        dq_ref[...] = jnp.dot(dsb, kb, preferred_element_type=F32)
        dk_ref[...] += lax.dot_general(dsb, qb, _TN, preferred_element_type=F32)
        dv_ref[...] += lax.dot_general(p.astype(BF16), dob, _TN, preferred_element_type=F32)

    q_spec = pl.BlockSpec((tq, HEAD_DIM), lambda kh, g, i: (i, kh * ATTN_GROUP + g))
    kv_spec = pl.BlockSpec((t, HEAD_DIM), lambda kh, g, i: (0, kh))
    return pl.pallas_call(
        body, name="attn_b", grid=(ATTN_KV_HEADS, ATTN_GROUP, n // tq),
        in_specs=[q_spec, kv_spec, kv_spec, q_spec],
        out_specs=[q_spec, kv_spec, kv_spec],
        out_shape=(jax.ShapeDtypeStruct(q.shape, F32), jax.ShapeDtypeStruct(k.shape, F32),
                   jax.ShapeDtypeStruct(v.shape, F32)),
        compiler_params=pltpu.CompilerParams(dimension_semantics=("parallel", "arbitrary", "arbitrary"),
                                             vmem_limit_bytes=VMEM_LIMIT),
    )(q, k, v, do)


@jax.custom_vjp
def attention(q, k, v):
    return _attn_fwd_call(q, k, v)


def _attention_fwd(q, k, v):
    return _attn_fwd_call(q, k, v), (q, k, v)


def _attention_bwd(res, do):
    return _attn_bwd_call(*res, do)


attention.defvjp(_attention_fwd, _attention_bwd)


_C = GDN_CHUNK


def _pdot(a, b):
    return jnp.dot(a, b, precision=lax.Precision.HIGH, preferred_element_type=F32)


@jax.custom_vjp
def _hdot(a, b):
    return jnp.dot(a.astype(BF16), b.astype(BF16), preferred_element_type=F32)


def _hdot_fwd(a, b):
    return _hdot(a, b), (a, b)


def _hdot_bwd(res, g):
    a, b = res
    gb = g.astype(BF16)
    return (lax.dot_general(gb, b.astype(BF16), _NT, preferred_element_type=F32),
            lax.dot_general(a.astype(BF16), gb, _TN, preferred_element_type=F32))


_hdot.defvjp(_hdot_fwd, _hdot_bwd)


def _each(fn, *lists):
    return [fn(*args) for args in zip(*lists)]


def _unit_lower_inverse(low, blockdiag):
    eye = (lax.broadcasted_iota(jnp.int32, (_C, _C), 0) == lax.broadcasted_iota(jnp.int32, (_C, _C), 1)).astype(F32)
    ld = _each(lambda a: a * blockdiag, low)
    lo = _each(lambda a, d: a - d, low, ld)
    l2 = _each(_hdot, ld, ld)
    l4 = _each(_hdot, l2, l2)
    l8 = _each(_hdot, l4, l4)
    td = _each(lambda d, a2: _hdot(eye - d, eye + a2), ld, l2)
    td = _each(lambda t, a4: _hdot(t, eye + a4), td, l4)
    td = _each(lambda t, a8: _hdot(t, eye + a8), td, l8)
    nn = _each(_hdot, td, lo)
    n2 = _each(_hdot, nn, nn)
    out = _each(lambda n, m2: _hdot(eye - n, eye + m2), nn, n2)
    return _each(_hdot, out, td)


def _gdn_chunks(heads, blockdiag):
    q, k, v, b_b, be_b, e_b, kd_b, m1, dec, gl, s = (list(col) for col in zip(*heads))
    f32dot = lambda a, b: jnp.dot(a, b, preferred_element_type=F32)
    nt = lambda a, b: lax.dot_general(a, b, _NT, preferred_element_type=F32)
    kk = _each(nt, k, k)
    t_inv = _unit_lower_inverse(_each(lambda m, a: m * a, m1, kk), blockdiag)
    u = _each(lambda t, b, x: _hdot(t, b * x), t_inv, b_b, v)
    w = _each(lambda t, b, x: _hdot(t, b * x), t_inv, be_b, k)
    delta = _each(lambda uu, ww, ss: uu - f32dot(ww, ss), u, w, s)
    p = _each(lambda d, qq, kx: d * nt(qq, kx), dec, q, k)
    o = _each(lambda qq, e, ss, pp, dd: f32dot(qq * e, ss) + f32dot(pp, dd), q, e_b, s, p, delta)
    s_new = _each(lambda g, ss, kx, kd, dd: g * ss + lax.dot_general(kx * kd, dd, _TN, preferred_element_type=F32),
                  gl, s, k, kd_b, delta)
    return o, s_new


def _blockdiag_mask():
    r = lax.broadcasted_iota(jnp.int32, (_C, _C), 0) >> 4
    c = lax.broadcasted_iota(jnp.int32, (_C, _C), 1) >> 4
    return (r == c).astype(F32)


def _gdn_specs(nc, ncc, reverse, backward):
    def ch(s):
        s = nc - 1 - s if backward else s
        return jnp.where(s < ncc, ncc - 1 - s, nc + ncc - 1 - s) if reverse else s

    tok = pl.BlockSpec((_C, 3 * GDN_W), lambda s: (ch(s), 0))
    out = pl.BlockSpec((_C, GDN_W), lambda s: (ch(s), 0))
    per_tok = pl.BlockSpec((GDN_HEADS, _C, HEAD_DIM), lambda s: (0, ch(s), 0))
    mat = pl.BlockSpec((GDN_HEADS, None, _C, _C), lambda s: (0, ch(s), 0, 0))
    row = pl.BlockSpec((GDN_HEADS, None, 1, HEAD_DIM), lambda s: (0, ch(s), 0, 0))
    state = pl.BlockSpec((GDN_HEADS, None, HEAD_DIM, HEAD_DIM), lambda s: (0, ch(s), 0, 0))
    return tok, out, per_tok, mat, row, state


def _head_cols(h, part):
    return slice((part * GDN_HEADS + h) * HEAD_DIM, (part * GDN_HEADS + h + 1) * HEAD_DIM)


def _gdn_heads(qkv_ref, factor_refs, state_ref, first):
    return [[qkv_ref[:, _head_cols(h, 0)], qkv_ref[:, _head_cols(h, 1)], qkv_ref[:, _head_cols(h, 2)]]
            + [r[h] for r in factor_refs] + [state_ref[first + h]] for h in range(GDN_HEADS)]


def _gdn_fwd_call(ncc, qkv, factors):
    t = qkv.shape[0]
    nc, nd = t // _C, len(factors)
    specs = [_gdn_specs(nc, ncc, d == 1, False) for d in range(nd)]

    def body(*refs):
        ins_d = [refs[8 * d:8 * d + 8] for d in range(nd)]
        outs_d = [refs[8 * nd + 2 * d:8 * nd + 2 * d + 2] for d in range(nd)]
        s_ref = refs[-1]

        @pl.when(pl.program_id(0) == 0)
        def _():
            s_ref[...] = jnp.zeros_like(s_ref)

        heads = sum([_gdn_heads(r[0], r[1:], s_ref, GDN_HEADS * d) for d, r in enumerate(ins_d)], [])
        o, s_new = _gdn_chunks(heads, _blockdiag_mask())
        for d, (o_ref, sall_ref) in enumerate(outs_d):
            for h in range(GDN_HEADS):
                i = GDN_HEADS * d + h
                sall_ref[h] = heads[i][10]
                o_ref[:, _head_cols(h, 0)] = o[i]
                s_ref[i] = s_new[i]

    in_specs, out_specs, out_shape, operands = [], [], [], []
    for (tok, out, per_tok, mat, row, state), f in zip(specs, factors):
        in_specs += [tok, per_tok, per_tok, per_tok, per_tok, mat, mat, row]
        out_specs += [out, state]
        out_shape += [jax.ShapeDtypeStruct((t, GDN_W), F32),
                      jax.ShapeDtypeStruct((GDN_HEADS, nc, HEAD_DIM, HEAD_DIM), F32)]
        operands += [qkv, *f]
    res = pl.pallas_call(
        body, name="gdn_f", grid=(nc,), in_specs=in_specs, out_specs=out_specs, out_shape=out_shape,
        scratch_shapes=[pltpu.VMEM((nd * GDN_HEADS, HEAD_DIM, HEAD_DIM), F32)],
        compiler_params=pltpu.CompilerParams(dimension_semantics=("arbitrary",), vmem_limit_bytes=VMEM_LIMIT),
    )(*operands)
    return [(res[2 * d], res[2 * d + 1]) for d in range(nd)]


def _gdn_bwd_call(ncc, qkv, factors, salls, dos):
    t = qkv.shape[0]
    nc, nd = t // _C, len(factors)
    specs = [_gdn_specs(nc, ncc, d == 1, True) for d in range(nd)]

    def body(*refs):
        ins_d = [refs[10 * d:10 * d + 10] for d in range(nd)]
        outs_d = [refs[10 * nd + 8 * d:10 * nd + 8 * d + 8] for d in range(nd)]
        ds_ref = refs[-1]

        @pl.when(pl.program_id(0) == 0)
        def _():
            ds_ref[...] = jnp.zeros_like(ds_ref)

        bd = _blockdiag_mask()
        heads = sum([_gdn_heads(r[0], r[1:8], r[8], 0) for r in ins_d], [])
        _, vjp = jax.vjp(lambda hs: _gdn_chunks(hs, bd), heads)
        (all_grads,) = vjp(([r[9][:, _head_cols(h, 0)] for r in ins_d for h in range(GDN_HEADS)],
                            [ds_ref[i] for i in range(nd * GDN_HEADS)]))
        for d, out_refs in enumerate(outs_d):
            for h in range(GDN_HEADS):
                grads = all_grads[GDN_HEADS * d + h]
                for part in range(3):
                    out_refs[0][:, _head_cols(h, part)] = grads[part]
                for ref, g in zip(out_refs[1:], grads[3:10]):
                    ref[h] = g
                ds_ref[GDN_HEADS * d + h] = grads[10]

    shp = lambda a: jax.ShapeDtypeStruct(a.shape, F32)
    in_specs, out_specs, out_shape, operands = [], [], [], []
    for (tok, out, per_tok, mat, row, state), f, sall, do in zip(specs, factors, salls, dos):
        in_specs += [tok, per_tok, per_tok, per_tok, per_tok, mat, mat, row, state, out]
        out_specs += [tok, per_tok, per_tok, per_tok, per_tok, mat, mat, row]
        out_shape += [shp(qkv)] + [shp(a) for a in f]
        operands += [qkv, *f, sall, do]
    res = pl.pallas_call(
        body, name="gdn_b", grid=(nc,), in_specs=in_specs, out_specs=out_specs, out_shape=out_shape,
        scratch_shapes=[pltpu.VMEM((nd * GDN_HEADS, HEAD_DIM, HEAD_DIM), F32)],
        compiler_params=pltpu.CompilerParams(dimension_semantics=("arbitrary",), vmem_limit_bytes=VMEM_LIMIT),
    )(*operands)
    return [res[8 * d:8 * d + 8] for d in range(nd)]


@functools.partial(jax.custom_vjp, nondiff_argnums=(0,))
def gdn_scan(ncc, qkv, f_fwd, f_rev):
    (o0, _), (o1, _) = _gdn_fwd_call(ncc, qkv, [f_fwd, f_rev])
    return o0, o1


def _gdn_scan_fwd(ncc, qkv, f_fwd, f_rev):
    (o0, s0), (o1, s1) = _gdn_fwd_call(ncc, qkv, [f_fwd, f_rev])
    return (o0, o1), (qkv, f_fwd, f_rev, s0, s1)


def _gdn_scan_bwd(ncc, res, dos):
    qkv, f_fwd, f_rev, s0, s1 = res
    g0, g1 = _gdn_bwd_call(ncc, qkv, [f_fwd, f_rev], [s0, s1], list(dos))
    return g0[0] + g1[0], tuple(g0[1:]), tuple(g1[1:])


gdn_scan.defvjp(_gdn_scan_fwd, _gdn_scan_bwd)


def _rope_tables(n, cl):
    t = np.arange(n)
    inv_freq = (ROPE_THETA ** (-np.arange(0, HEAD_DIM // 2, 2, dtype=np.float32) / (HEAD_DIM // 2))).astype(np.float32)
    ang_r = (t // GRID_W).astype(np.float32)[:, None] * inv_freq
    ang_c = (t % GRID_W).astype(np.float32)[:, None] * inv_freq
    cos = np.concatenate([np.cos(ang_r), np.cos(ang_r), np.cos(ang_c), np.cos(ang_c)], axis=1)
    sin = np.concatenate([-np.sin(ang_r), np.sin(ang_r), -np.sin(ang_c), np.sin(ang_c)], axis=1)
    cos_all = np.concatenate([np.ones((cl, HEAD_DIM), np.float32), cos], axis=0)
    sin_all = np.concatenate([np.zeros((cl, HEAD_DIM), np.float32), sin], axis=0)
    j = np.arange(HEAD_DIM)
    src = np.where((j % 64) < 32, j + 32, j - 32)
    perm = np.zeros((HEAD_DIM, HEAD_DIM), np.float32)
    perm[src, j] = 1.0
    return (jnp.asarray(cos.astype(np.float32)), jnp.asarray(sin.astype(np.float32)),
            jnp.asarray(cos_all), jnp.asarray(sin_all), jnp.asarray(perm))


def _gdn_factors(log_a, beta, reverse):
    t = log_a.shape[0]
    nc = t // _C
    la = log_a.reshape(nc, _C, GDN_HEADS).transpose(2, 0, 1)
    be = beta.reshape(nc, _C, GDN_HEADS).transpose(2, 0, 1)
    gam = lax.cumsum(la, axis=2, reverse=reverse)
    idx = jnp.arange(_C)
    incl = (idx[:, None] <= idx[None, :]) if reverse else (idx[:, None] >= idx[None, :])
    strict = (idx[:, None] < idx[None, :]) if reverse else (idx[:, None] > idx[None, :])
    dec = jnp.exp(jnp.where(incl, gam[..., :, None] - gam[..., None, :], -jnp.inf))
    m1 = jnp.where(strict, be[..., :, None] * dec, 0.0)
    e = jnp.exp(gam)
    g_last = gam[..., :1] if reverse else gam[..., -1:]
    lanes = lambda a: jnp.broadcast_to(a.reshape(GDN_HEADS, t, 1), (GDN_HEADS, t, HEAD_DIM))
    gl = jnp.broadcast_to(jnp.exp(g_last)[..., None], (GDN_HEADS, nc, 1, HEAD_DIM))
    return lanes(be), lanes(be * e), lanes(e), lanes(jnp.exp(g_last - gam)), m1, dec, gl


def local_loss(x, wz, wb, ws, c, ctx, target):
    n, cl = x.shape[0], ctx.shape[0]
    cos_q, sin_q, cos_k, sin_k, perm = _rope_tables(n, cl)

    sc_in = jnp.concatenate([jax.nn.silu(c), jax.nn.silu(ws["c_ctx"])[None, :], jnp.zeros((14, D_MODEL), F32)], axis=0)
    mod = pmm(sc_in, wb["w_mod"], wz["w_mod"], "mm_mod") + ws["b_mod"]
    sh1, sc1, g1, sh2, sc2, g2 = [mod[0:1, i * D_MODEL:(i + 1) * D_MODEL] for i in range(6)]
    csh1, csc1 = mod[1:2, 0:D_MODEL], mod[1:2, D_MODEL:2 * D_MODEL]

    def norm_mod(a, sh, sc):
        return (_rms(a) * (1.0 + sc) + sh,)

    (hx,) = rowop(norm_mod, "normmod_x", (x,), (sh1, sc1))
    (hc,) = rowop(norm_mod, "normmod_c", (ctx,), (csh1, csc1))
    h_all = jnp.concatenate([hc, hx], axis=0)
    p_main = pmm_t(h_all, wb["w_in_main"], wz["w_in_main"], "mm_in")
    p_small = pmm_t(h_all, wb["w_in_small"], wz["w_in_small"], "mm_ins")
    ak, av, qkv, aq, z, gate = jnp.split(p_main, [KV_W, 2 * KV_W, SMALL_AT, SMALL_AT + Q_W, SMALL_AT + Q_W + GDN_W],
                                         axis=1)
    db, da = p_small[:, :2 * GDN_HEADS], p_small[:, 2 * GDN_HEADS:4 * GDN_HEADS]

    def qk_prep(nh):
        def fn(a, w, cos, sin, pm):
            outs = []
            for ah in _heads(a, nh):
                y = _rms(ah) * w
                outs.append(y * cos + _pdot(y, pm) * sin)
            return (jnp.concatenate(outs, axis=1),)
        return fn

    (q_x,) = rowop(qk_prep(ATTN_HEADS), "q_prep", (aq[cl:],), (ws["q_norm_w"],), (cos_q, sin_q), (perm,))
    (k_all,) = rowop(qk_prep(ATTN_KV_HEADS), "k_prep", (ak,), (ws["k_norm_w"],), (cos_k, sin_k), (perm,))
    attn_x = attention(q_x, k_all, av)

    cw = ws["conv_qkv_w"]
    normed = jnp.asarray(np.repeat([1.0, 1.0, 0.0], GDN_W)[None, :], F32)
    scale = jnp.asarray(np.repeat([HEAD_DIM ** -0.5, 1.0, 1.0], GDN_W)[None, :], F32)

    def gdn_prep(a, w0, w1, w2, nf, sc):
        s = jax.nn.silu(_conv3(a, w0, w1, w2, (0, cl)))
        inv = lax.rsqrt(jnp.sum(s * s, axis=-1, keepdims=True) + NORM_EPS)
        return (s * jnp.where(nf > 0.0, inv * sc, 1.0),)

    (qkvn,) = colop(gdn_prep, "gdn_prep", (qkv, cw[0:1], cw[1:2], cw[2:3], normed, scale),
                    [(i, 0) for i in range(6)], 2, 3 * GDN_HEADS)
    beta = jax.nn.sigmoid(db).reshape(-1, 2, GDN_HEADS)
    log_a = -jnp.exp(ws["a_log"])[None] * jax.nn.softplus(da.reshape(-1, 2, GDN_HEADS) + ws["dt_bias"][None])
    o_fwd, o_rev = gdn_scan(cl // _C, qkvn, _gdn_factors(log_a[:, 0], beta[:, 0], False),
                            _gdn_factors(log_a[:, 1], beta[:, 1], True))
    o_x = o_fwd[cl:] + o_rev[cl:]

    def gdn_out(o, zz, w):
        outs = [_rms(oh) * w * jax.nn.silu(zh) for oh, zh in zip(_heads(o, GDN_HEADS), _heads(zz, GDN_HEADS))]
        return (jnp.concatenate(outs, axis=1),)

    (gdn_x,) = rowop(gdn_out, "gdn_out", (o_x, z[cl:]), (ws["gdn_norm_w"],))

    pa = pmm(attn_x, wb["w_pa"], wz["w_pa"], "mm_pa")
    pd = pmm(gdn_x, wb["w_pd"], wz["w_pd"], "mm_pd")

    def merge(a, d, g):
        return (jax.nn.sigmoid(g[:, :D_MODEL]) * a + jax.nn.sigmoid(g[:, D_MODEL:]) * d,)

    (y,) = rowop(merge, "merge", (pa, pd, gate[cl:]))
    mo = pmm(y, wb["w_out"], wz["w_out"], "mm_out")

    def res_norm_mod(xx, m, g, sh, sc):
        x1 = xx + g * m
        return x1, _rms(x1) * (1.0 + sc) + sh

    x1, h2 = rowop(res_norm_mod, "res1", (x, mo), (g1, sh2, sc2))
    up = pmm(h2, wb["w_up"], wz["w_up"], "mm_up")
    fw = ws["ffn_conv_w"]

    def ffn_act(ug, uv, w0g, w0v, w1g, w1v, w2g, w2v, bg, bv):
        g = _conv3(ug, w0g, w1g, w2g, (0,)) + bg
        v = _conv3(uv, w0v, w1v, w2v, (0,)) + bv
        return (jax.nn.silu(g) * v,)

    half = D_FF // HEAD_DIM
    (act,) = colop(ffn_act, "ffn_act", (up, fw[0:1], fw[1:2], fw[2:3], ws["ffn_conv_b"]),
                   [(i, off) for i in range(5) for off in (0, half)], 0, half)
    dn = pmm(act, wb["w_down"], wz["w_down"], "mm_down")

    def head(xx, m, g, w, tgt):
        yy = _rms(xx + g * m) * w
        err = (yy - tgt) ** 2
        return (jnp.broadcast_to(0.5 * jnp.mean(err, axis=-1, keepdims=True), (xx.shape[0], HEAD_DIM)),)

    (row_loss,) = rowop(head, "head", (x1, dn), (g2, ws["final_norm_w"][None, :]), (target,))
    return jnp.sum(row_loss[:, 0])


_HBM = pl.BlockSpec(memory_space=pltpu.HBM)


def _chip_peers():
    x, y = lax.axis_index("x"), lax.axis_index("y")
    return [(1 - x, y), (x, 1 - y), (1 - x, 1 - y)]


_SPLIT_COLS = ("w_in",)


def _half_of(view, nm, idx, lead=0):
    r, cdim = view.shape[-2:]
    pre = (slice(None),) * lead
    if nm in _SPLIT_COLS:
        return view.at[pre + (slice(None), pl.ds(pl.multiple_of(idx * (cdim // 2), 128), cdim // 2))]
    return view.at[pre + (pl.ds(pl.multiple_of(idx * (r // 2), 16), r // 2), slice(None))]


def _remote(src, dst, send_sem, recv_sem, dev):
    return pltpu.make_async_remote_copy(src_ref=src, dst_ref=dst, send_sem=send_sem, recv_sem=recv_sem,
                                        device_id=dev, device_id_type=MESH)


def _hbm_call(body, name, ins, out_shape, n_sems, in_place=False):
    names = tuple(ins)
    return dict(zip(names, pl.pallas_call(
        body, name=name, in_specs=[_HBM] * len(names), out_specs=[_HBM] * len(names),
        out_shape=[out_shape(nm, ins[nm]) for nm in names],
        scratch_shapes=[pltpu.SemaphoreType.DMA((k,)) for k in n_sems],
        input_output_aliases={i: i for i in range(len(names))} if in_place else {},
    )(*[ins[nm] for nm in names])))


def all_gather_chips(shards):
    names = tuple(shards)
    n = len(names)

    def body(*refs):
        ins, outs = dict(zip(names, refs[:n])), dict(zip(names, refs[n:2 * n]))
        ici_send, ici_recv, d2d_send, d2d_recv, own_send, own_recv = refs[2 * n:]
        x, y, c = lax.axis_index("x"), lax.axis_index("y"), lax.axis_index("c")
        me, sib = 2 * x + y, (x, y, 1 - c)
        own = [_remote(ins[nm], outs[nm].at[me], own_send.at[i], own_recv.at[i], sib) for i, nm in enumerate(names)]
        for cp in own:
            cp.start()
        sends = []
        for k, (px, py) in enumerate(_chip_peers()):
            for i, nm in enumerate(names):
                cp = _remote(_half_of(ins[nm], nm, c), _half_of(outs[nm].at[me], nm, c), ici_send.at[k * n + i],
                             ici_recv.at[k * n + i], (px, py, c))
                cp.start()
                sends.append(cp)
        for k, (px, py) in enumerate(_chip_peers()):
            for i, nm in enumerate(names):
                landed = _half_of(outs[nm].at[2 * px + py], nm, c)
                _remote(landed, landed, ici_send.at[k * n + i], ici_recv.at[k * n + i], (px, py, c)).wait_recv()
                fw = _remote(landed, landed, d2d_send.at[k * n + i], d2d_recv.at[k * n + i], sib)
                fw.start()
                sends.append(fw)
        for k, (px, py) in enumerate(_chip_peers()):
            for i, nm in enumerate(names):
                other = _half_of(outs[nm].at[2 * px + py], nm, 1 - c)
                _remote(other, other, d2d_send.at[k * n + i], d2d_recv.at[k * n + i], sib).wait_recv()
        for cp in sends:
            cp.wait_send()
        for cp in own:
            cp.wait()

    return _hbm_call(body, "ag_weights", shards, lambda nm, a: jax.ShapeDtypeStruct((N_CHIPS,) + a.shape, a.dtype),
                     (3 * n, 3 * n, 3 * n, 3 * n, n, n))


def sibling_halves(blocks):
    names = tuple(blocks)

    def body(*refs):
        n = len(names)
        ins, outs = dict(zip(names, refs[:n])), dict(zip(names, refs[n:2 * n]))
        send_sems, recv_sems = refs[2 * n:]
        x, y, c = lax.axis_index("x"), lax.axis_index("y"), lax.axis_index("c")
        cps = [_remote(_half_of(ins[nm], nm, 1 - c, lead=1), outs[nm], send_sems.at[i], recv_sems.at[i], (x, y, 1 - c))
               for i, nm in enumerate(names)]
        for cp in cps:
            cp.start()
        for cp in cps:
            cp.wait()

    def half_shape(nm, a):
        r, cdim = a.shape[-2:]
        return jax.ShapeDtypeStruct((N_CHIPS, r, cdim // 2) if nm in _SPLIT_COLS else (N_CHIPS, r // 2, cdim), a.dtype)

    return _hbm_call(body, "rs_sibling", blocks, half_shape, (len(names), len(names)))


def scatter_halves(blocks):
    names = tuple(blocks)
    n = len(names)

    def body(*refs):
        ins, outs = dict(zip(names, refs[:n])), dict(zip(names, refs[n:2 * n]))
        send_sems, recv_sems = refs[2 * n:]
        c = lax.axis_index("c")
        cps = [_remote(ins[nm].at[2 * px + py], outs[nm].at[k], send_sems.at[k * n + i], recv_sems.at[k * n + i],
                       (px, py, c))
               for k, (px, py) in enumerate(_chip_peers()) for i, nm in enumerate(names)]
        for cp in cps:
            cp.start()
        for cp in cps:
            cp.wait_recv()
        for cp in cps:
            cp.wait_send()

    return _hbm_call(body, "rs_grads", blocks, lambda nm, a: jax.ShapeDtypeStruct((3,) + a.shape[1:], a.dtype),
                     (3 * n, 3 * n))


def sibling_assemble(arrays):
    names = tuple(arrays)

    def body(*refs):
        n = len(names)
        ins, outs = dict(zip(names, refs[:n])), dict(zip(names, refs[n:2 * n]))
        send_sems, recv_sems = refs[2 * n:]
        x, y, c = lax.axis_index("x"), lax.axis_index("y"), lax.axis_index("c")
        cps = [_remote(_half_of(ins[nm], nm, c), _half_of(outs[nm], nm, c), send_sems.at[i], recv_sems.at[i],
                       (x, y, 1 - c)) for i, nm in enumerate(names)]
        for cp in cps:
            cp.start()
        for i, nm in enumerate(names):
            other = _half_of(outs[nm], nm, 1 - c)
            _remote(other, other, send_sems.at[i], recv_sems.at[i], (x, y, 1 - c)).wait_recv()
        for cp in cps:
            cp.wait_send()

    return _hbm_call(body, "rs_assemble", arrays, lambda nm, a: jax.ShapeDtypeStruct(a.shape, a.dtype),
                     (len(names), len(names)), in_place=True)


def all_reduce_small(v):
    def body(v_ref, tot_ref, gath_ref, send_sems, recv_sems):
        x, y, c = lax.axis_index("x"), lax.axis_index("y"), lax.axis_index("c")
        me = 4 * x + 2 * y + c
        gath_ref[me] = v_ref[...]

        def peer(k):
            m = k + 1
            return (x ^ (m >> 2 & 1), y ^ (m >> 1 & 1), c ^ (m & 1))

        sends = [pltpu.make_async_remote_copy(src_ref=v_ref, dst_ref=gath_ref.at[me], send_sem=send_sems.at[k],
                                              recv_sem=recv_sems.at[k], device_id=peer(k), device_id_type=MESH)
                 for k in range(N_DEV - 1)]
        for cp in sends:
            cp.start()
        for k in range(N_DEV - 1):
            px, py, pc = peer(k)
            pltpu.make_async_remote_copy(src_ref=v_ref, dst_ref=gath_ref.at[4 * px + 2 * py + pc],
                                         send_sem=send_sems.at[k], recv_sem=recv_sems.at[k], device_id=peer(k),
                                         device_id_type=MESH).wait_recv()
        for cp in sends:
            cp.wait_send()
        acc = gath_ref[0]
        for d in range(1, N_DEV):
            acc = acc + gath_ref[d]
        tot_ref[...] = acc

    vm = pl.BlockSpec(memory_space=pltpu.VMEM)
    return pl.pallas_call(
        body, name="ar_small", in_specs=[vm], out_specs=[vm, vm],
        out_shape=(jax.ShapeDtypeStruct(v.shape, v.dtype), jax.ShapeDtypeStruct((N_DEV,) + v.shape, v.dtype)),
        scratch_shapes=[pltpu.SemaphoreType.DMA((N_DEV - 1,)), pltpu.SemaphoreType.DMA((N_DEV - 1,))],
    )(v)[0]


def _elementwise(fn, name, ins, n_out, out_dtype=F32):
    r, cdim = ins[0].shape
    tr = _pick(r, tuple(p for p in (488, 256, 128, 104, 64, 32, 16, 8) if p * cdim * 4 <= 2 * 1024 * 1024))
    spec = pl.BlockSpec((tr, cdim), lambda i: (i, 0))

    def body(*refs):
        res = fn(*[ref[...] for ref in refs[:len(ins)]])
        for o_ref, v in zip(refs[len(ins):], res):
            o_ref[...] = v

    return pl.pallas_call(
        body, name=name, grid=(r // tr,), in_specs=[spec] * len(ins), out_specs=[spec] * n_out,
        out_shape=tuple(jax.ShapeDtypeStruct((r, cdim), out_dtype) for _ in range(n_out)),
        compiler_params=pltpu.CompilerParams(dimension_semantics=("parallel",), vmem_limit_bytes=VMEM_LIMIT),
    )(*ins)


def _half_block_specs(nm, shard_shape):
    r, cdim = shard_shape
    if nm in _SPLIT_COLS:
        return (None, r, cdim // 2), (lambda j, c: (j, 0, c))
    return (None, r // 2, cdim), (lambda j, c: (j, c, 0))


def _presum(nm, sel, g32, a):
    blk, at = _half_block_specs(nm, g32.shape[1:])

    def body(s_ref, g_ref, a_ref, o_ref):
        del s_ref
        o_ref[...] = (g_ref[...] + a_ref[...]).astype(BF16)

    return pl.pallas_call(
        body, name="rs_presum_" + nm,
        grid_spec=pltpu.PrefetchScalarGridSpec(
            num_scalar_prefetch=1, grid=(N_CHIPS,),
            in_specs=[pl.BlockSpec(blk, lambda j, s: at(j, s[0])), pl.BlockSpec(blk, lambda j, s: (j, 0, 0))],
            out_specs=pl.BlockSpec(blk, lambda j, s: (j, 0, 0))),
        out_shape=jax.ShapeDtypeStruct(a.shape, BF16),
        compiler_params=pltpu.CompilerParams(dimension_semantics=("parallel",), vmem_limit_bytes=VMEM_LIMIT),
    )(sel, g32, a)


def _finalsum(nm, sel, g32, a, got):
    blk, at = _half_block_specs(nm, g32.shape[1:])

    def body(s_ref, g_ref, a_ref, r_ref, o_ref):
        del s_ref
        acc = g_ref[...] + a_ref[...]
        for k in range(3):
            acc = acc + r_ref[k].astype(F32)
        o_ref[...] = acc

    return pl.pallas_call(
        body, name="rs_final_" + nm,
        grid_spec=pltpu.PrefetchScalarGridSpec(
            num_scalar_prefetch=1, grid=(1,),
            in_specs=[pl.BlockSpec(blk, lambda i, s: at(s[1], s[0])), pl.BlockSpec(blk, lambda i, s: (s[1], 0, 0)),
                      pl.BlockSpec(got.shape, lambda i, s: (0, 0, 0))],
            out_specs=pl.BlockSpec(blk[1:], lambda i, s: at(0, s[0])[1:])),
        out_shape=jax.ShapeDtypeStruct(g32.shape[1:], F32),
        compiler_params=pltpu.CompilerParams(dimension_semantics=("arbitrary",), vmem_limit_bytes=VMEM_LIMIT),
    )(sel, g32, a, got)


def _adamw(w, g, m, v, name):
    shape = w.shape
    to2 = lambda a: a.reshape(-1, shape[-1])

    def fn(w_, g_, m_, v_):
        m_new = ADAM_B1 * m_ + (1.0 - ADAM_B1) * g_
        v_new = ADAM_B2 * v_ + (1.0 - ADAM_B2) * (g_ * g_)
        m_hat = m_new / (1.0 - ADAM_B1 ** ADAM_STEP)
        v_hat = v_new / (1.0 - ADAM_B2 ** ADAM_STEP)
        delta = -ADAM_LR * (m_hat / (jnp.sqrt(v_hat) + ADAM_EPS) + ADAM_WD * w_)
        return delta, m_new, v_new

    outs = _elementwise(fn, name, [to2(a) for a in (w, g, m, v)], 3)
    return tuple(o.reshape(shape) for o in outs)


_BIG = ("w_mod", "w_in", "w_pa", "w_pd", "w_out", "w_up", "w_down")
_COL_SHARDED = ("w_mod", "w_up")
_FULL_SHAPE = {"w_mod": (D_MODEL, MOD_W), "w_in": (IN_COLS, D_MODEL), "w_pa": (Q_W, D_MODEL), "w_pd": (GDN_W, D_MODEL),
               "w_out": (D_MODEL, D_MODEL), "w_up": (D_MODEL, 2 * D_FF), "w_down": (D_FF, D_MODEL)}


def _shard_shape(name):
    r, cdim = _FULL_SHAPE[name]
    return (r, cdim // N_CHIPS) if name in _COL_SHARDED else (r // N_CHIPS, cdim)


_CONV_ELEMS = 2 * (3 * CONV_W // N_CHIPS + 3 * 2 * D_FF // N_CHIPS)
_CONV_ROWS = 32


def _blocks_of_full(name, full):
    r, cdim = _FULL_SHAPE[name]
    if name in _COL_SHARDED:
        return full.reshape(r, N_CHIPS, cdim // N_CHIPS).transpose(1, 0, 2)
    return full.reshape(N_CHIPS, r // N_CHIPS, cdim)


def _full_of_blocks(name, blocks):
    r, cdim = _FULL_SHAPE[name]
    if name in _COL_SHARDED:
        return blocks.transpose(1, 0, 2).reshape(r, cdim)
    return blocks.reshape(r, cdim)


def _w_in_regroup(w_in_t):
    main = jnp.concatenate([w_in_t[:SMALL_AT], w_in_t[SMALL_AT + 4 * GDN_HEADS:]], axis=0)
    small = jnp.pad(w_in_t[SMALL_AT:SMALL_AT + 4 * GDN_HEADS], ((0, HEAD_DIM - 4 * GDN_HEADS), (0, 0)))
    return main, small


def _w_in_ungroup(main, small):
    return jnp.concatenate([main[:SMALL_AT], small[:4 * GDN_HEADS], main[SMALL_AT:]], axis=0)


_SMALL = ("c_ctx", "b_mod", "q_norm_w", "k_norm_w", "conv_qkv_w", "a_log", "dt_bias", "gdn_norm_w", "ffn_conv_w",
          "ffn_conv_b", "final_norm_w")


def _pack_small(tree, rows):
    flat = jnp.concatenate([tree[nm].reshape(-1) for nm in _SMALL])
    return jnp.pad(flat, (0, rows * 128 - flat.shape[0])).reshape(rows, 128)


def _unpack_small(packed, like):
    flat, out, off = packed.reshape(-1), {}, 0
    for nm in _SMALL:
        size = int(np.prod(like[nm].shape))
        out[nm] = flat[off:off + size].reshape(like[nm].shape)
        off += size
    return out


def kernel(x, c, ctx, c_ctx, w_mod, b_mod, w_in, q_norm_w, k_norm_w, conv_qkv_w, a_log, dt_bias, gdn_norm_w, w_pa, w_pd, w_out, w_up, ffn_conv_w, ffn_conv_b, w_down, final_norm_w, loss_target, m_c_ctx, m_w_mod, m_b_mod, m_w_in, m_q_norm_w, m_k_norm_w, m_conv_qkv_w, m_a_log, m_dt_bias, m_gdn_norm_w, m_w_pa, m_w_pd, m_w_out, m_w_up, m_ffn_conv_w, m_ffn_conv_b, m_w_down, m_final_norm_w, v_c_ctx, v_w_mod, v_b_mod, v_w_in, v_q_norm_w, v_k_norm_w, v_conv_qkv_w, v_a_log, v_dt_bias, v_gdn_norm_w, v_w_pa, v_w_pd, v_w_out, v_w_up, v_ffn_conv_w, v_ffn_conv_b, v_w_down, v_final_norm_w):
    names = ("c_ctx", "w_mod", "b_mod", "w_in", "q_norm_w", "k_norm_w", "conv_qkv_w", "a_log", "dt_bias", "gdn_norm_w",
             "w_pa", "w_pd", "w_out", "w_up", "ffn_conv_w", "ffn_conv_b", "w_down", "final_norm_w")
    w_sh = dict(c_ctx=c_ctx, w_mod=w_mod, b_mod=b_mod, w_in=w_in, q_norm_w=q_norm_w, k_norm_w=k_norm_w,
                conv_qkv_w=conv_qkv_w, a_log=a_log, dt_bias=dt_bias, gdn_norm_w=gdn_norm_w, w_pa=w_pa, w_pd=w_pd,
                w_out=w_out, w_up=w_up, ffn_conv_w=ffn_conv_w, ffn_conv_b=ffn_conv_b, w_down=w_down,
                final_norm_w=final_norm_w)
    m_sh = dict(c_ctx=m_c_ctx, w_mod=m_w_mod, b_mod=m_b_mod, w_in=m_w_in, q_norm_w=m_q_norm_w, k_norm_w=m_k_norm_w,
                conv_qkv_w=m_conv_qkv_w, a_log=m_a_log, dt_bias=m_dt_bias, gdn_norm_w=m_gdn_norm_w, w_pa=m_w_pa,
                w_pd=m_w_pd, w_out=m_w_out, w_up=m_w_up, ffn_conv_w=m_ffn_conv_w, ffn_conv_b=m_ffn_conv_b,
                w_down=m_w_down, final_norm_w=m_final_norm_w)
    v_sh = dict(c_ctx=v_c_ctx, w_mod=v_w_mod, b_mod=v_b_mod, w_in=v_w_in, q_norm_w=v_q_norm_w, k_norm_w=v_k_norm_w,
                conv_qkv_w=v_conv_qkv_w, a_log=v_a_log, dt_bias=v_dt_bias, gdn_norm_w=v_gdn_norm_w, w_pa=v_w_pa,
                w_pd=v_w_pd, w_out=v_w_out, w_up=v_w_up, ffn_conv_w=v_ffn_conv_w, ffn_conv_b=v_ffn_conv_b,
                w_down=v_w_down, final_norm_w=v_final_norm_w)
    chip = 2 * lax.axis_index("x") + lax.axis_index("y")

    conv_bits = jnp.concatenate([lax.bitcast_convert_type(w_sh[nm][0], BF16).reshape(-1)
                                 for nm in ("conv_qkv_w", "ffn_conv_w")])
    shards = {nm: w_sh[nm][0].astype(BF16).T if nm == "w_in" else w_sh[nm][0].astype(BF16) for nm in _BIG}
    shards["conv"] = jnp.pad(conv_bits, (0, _CONV_ROWS * D_MODEL - _CONV_ELEMS)).reshape(_CONV_ROWS, D_MODEL)
    gathered = all_gather_chips(shards)
    wb = {nm: _full_of_blocks(nm, gathered[nm]) for nm in _BIG}
    wb["w_in_main"], wb["w_in_small"] = _w_in_regroup(wb.pop("w_in"))
    conv_all = gathered["conv"].reshape(N_CHIPS, -1)[:, :_CONV_ELEMS]
    n_cq = 2 * 3 * CONV_W // N_CHIPS
    unbits = lambda a, w: lax.bitcast_convert_type(a.reshape(N_CHIPS, 3, w // N_CHIPS, 2), F32).transpose(1, 0, 2).reshape(3, w)
    ws = dict(c_ctx=c_ctx, b_mod=b_mod, q_norm_w=q_norm_w, k_norm_w=k_norm_w, a_log=a_log[0], dt_bias=dt_bias[0],
              gdn_norm_w=gdn_norm_w, ffn_conv_b=ffn_conv_b, final_norm_w=final_norm_w,
              conv_qkv_w=unbits(conv_all[:, :n_cq], CONV_W), ffn_conv_w=unbits(conv_all[:, n_cq:], 2 * D_FF))
    wz = {nm: jnp.zeros(a.shape, F32) for nm, a in wb.items()}

    loss_local, (gx, gz, gs) = jax.value_and_grad(local_loss, argnums=(0, 1, 3))(
        x[0], wz, wb, ws, c, ctx[0], loss_target[0])
    loss = lax.psum(loss_local, ("x", "y", "c"))

    gz["w_in"] = _w_in_ungroup(gz.pop("w_in_main"), gz.pop("w_in_small"))
    g32 = {nm: _blocks_of_full(nm, gz[nm]) for nm in _BIG}
    sel = jnp.stack([lax.axis_index("c"), chip]).astype(jnp.int32)
    theirs = sibling_halves(g32)
    got = scatter_halves({nm: _presum(nm, sel, g32[nm], theirs[nm]) for nm in _BIG})
    g_big = sibling_assemble({nm: _finalsum(nm, sel, g32[nm], theirs[nm], got[nm]) for nm in _BIG})

    gs["a_log"], gs["dt_bias"] = gs["a_log"][None], gs["dt_bias"][None]
    like = {nm: gs[nm] for nm in _SMALL}
    small_rows = -(-sum(int(np.prod(like[nm].shape)) for nm in _SMALL) // 1024) * 8
    g_small = _unpack_small(all_reduce_small(_pack_small(gs, small_rows)), like)
    for nm, width in (("conv_qkv_w", CONV_W), ("ffn_conv_w", 2 * D_FF)):
        g_small[nm] = lax.dynamic_slice_in_dim(g_small[nm], chip * (width // N_CHIPS), width // N_CHIPS, axis=1)[None]

    grads, deltas, new_m, new_v = {}, {}, {}, {}
    for nm in _BIG:
        g = g_big[nm].T if nm == "w_in" else g_big[nm]
        grads[nm] = g[None]
        deltas[nm], new_m[nm], new_v[nm] = (o[None] for o in _adamw(w_sh[nm][0], g, m_sh[nm][0], v_sh[nm][0],
                                                                     "adamw_" + nm))
    shard_like = {nm: w_sh[nm] for nm in _SMALL}
    rows_l = -(-sum(int(np.prod(shard_like[nm].shape)) for nm in _SMALL) // 1024) * 8
    g_l = _pack_small({nm: g_small[nm].reshape(w_sh[nm].shape) for nm in _SMALL}, rows_l)
    outs = _adamw(_pack_small(w_sh, rows_l), g_l, _pack_small(m_sh, rows_l), _pack_small(v_sh, rows_l), "adamw_small")
    grads.update(_unpack_small(g_l, shard_like))
    for tree, packed in zip((deltas, new_m, new_v), outs):
        tree.update(_unpack_small(packed, shard_like))

    return (loss, gx[None], *[grads[nm] for nm in names], *[deltas[nm] for nm in names],
            *[new_m[nm] for nm in names], *[new_v[nm] for nm in names])
```

```python
import functools
import math

import jax
import jax.numpy as jnp
import numpy as np
from jax import lax
from jax.experimental import pallas as pl
from jax.experimental.pallas import tpu as pltpu

F32 = jnp.float32
BF16 = jnp.bfloat16
HIGHEST = lax.Precision.HIGHEST
MESH = pl.DeviceIdType.MESH

D_MODEL = 1024
GRID_W = 64
ATTN_HEADS = 8
ATTN_KV_HEADS = 2
ATTN_GROUP = ATTN_HEADS // ATTN_KV_HEADS
HEAD_DIM = 128
ROPE_THETA = 10000.0
GDN_HEADS = 8
GDN_CHUNK = 64
D_FF = 2816
NORM_EPS = 1e-6
KV_W = ATTN_KV_HEADS * HEAD_DIM
Q_W = ATTN_HEADS * HEAD_DIM
GDN_W = GDN_HEADS * HEAD_DIM
CONV_W = 3 * GDN_W
MOD_W = 6 * D_MODEL
IN_COLS = 2 * KV_W + CONV_W + 4 * GDN_HEADS + Q_W + GDN_W + 2 * D_MODEL
IN_MAIN = IN_COLS - 4 * GDN_HEADS
SMALL_AT = 2 * KV_W + CONV_W
N_CHIPS = 4
N_DEV = 8

ADAM_LR = 0.001
ADAM_B1 = 0.9
ADAM_B2 = 0.999
ADAM_EPS = 1e-08
ADAM_WD = 0.01
ADAM_STEP = 10

VMEM_LIMIT = 48 * 1024 * 1024
MATMUL_VMEM_BUDGET = 40 * 1024 * 1024
MATMUL_STEP_BYTES = 1200 * 1024


def _pick(dim, prefs):
    for p in prefs:
        if p <= dim and dim % p == 0:
            return p
    return dim


_DIMS = {
    "nn": (((1,), (0,)), ((), ())),
    "nt": (((1,), (1,)), ((), ())),
    "tn": (((0,), (0,)), ((), ())),
}


def _matmul_plan(m, n, k, a_bytes, b_bytes):
    best = None
    for tm in (2304, 2048, 1152, 1024, 768, 512, 384, 256, 128, m):
        for tn in (2560, 1536, 1408, 1024, 768, 512, 256, 128, n):
            for tk in (3840, 2816, 2560, 2304, 2048, 1920, 1408, 1152, 1024, 768, 512, 256, 128, k):
                if tm > m or tn > n or tk > k or m % tm or n % tn or k % tk:
                    continue
                blocks = tm * tk * a_bytes + tk * tn * b_bytes + tm * tn * 4
                casts = (tm * tk * 2 if a_bytes > 2 else 0) + (tk * tn * 2 if b_bytes > 2 else 0) + tm * tn * 4
                if 2 * blocks + casts > MATMUL_VMEM_BUDGET:
                    continue
                nm, nn, nk = m // tm, n // tn, k // tk
                size_a, size_b = m * k * a_bytes, k * n * b_bytes
                for n_inner in (True, False):
                    if n_inner:
                        traffic = (size_a if nk == 1 else nn * size_a) + nm * size_b
                    else:
                        traffic = nn * size_a + (size_b if nk == 1 else nm * size_b)
                    cost = traffic + nm * nn * nk * MATMUL_STEP_BYTES
                    if best is None or cost < best[0]:
                        best = (cost, tm, tn, tk, n_inner)
    return best[1:]


def _matmul(a, b, mode, name):
    if mode == "nn":
        (m, k), (_, n) = a.shape, b.shape
    elif mode == "nt":
        (m, k), (n, _) = a.shape, b.shape
    else:
        (k, m), (_, n) = a.shape, b.shape
    tm, tn, tk, n_inner = _matmul_plan(m, n, k, a.dtype.itemsize, b.dtype.itemsize)
    nk = k // tk
    ij = (lambda g0, g1: (g0, g1)) if n_inner else (lambda g0, g1: (g1, g0))
    if mode == "tn":
        a_spec = pl.BlockSpec((tk, tm), lambda g0, g1, l: (l, ij(g0, g1)[0]))
    else:
        a_spec = pl.BlockSpec((tm, tk), lambda g0, g1, l: (ij(g0, g1)[0], l))
    if mode == "nt":
        b_spec = pl.BlockSpec((tn, tk), lambda g0, g1, l: (ij(g0, g1)[1], l))
    else:
        b_spec = pl.BlockSpec((tk, tn), lambda g0, g1, l: (l, ij(g0, g1)[1]))
    dims = _DIMS[mode]

    def body(a_ref, b_ref, o_ref):
        part = lax.dot_general(a_ref[...].astype(BF16), b_ref[...].astype(BF16), dims, preferred_element_type=F32)
        if nk == 1:
            o_ref[...] = part
        else:
            l = pl.program_id(2)

            @pl.when(l == 0)
            def _():
                o_ref[...] = part

            @pl.when(l > 0)
            def _():
                o_ref[...] += part

    return pl.pallas_call(
        body,
        name=name,
        grid=(m // tm, n // tn, nk) if n_inner else (n // tn, m // tm, nk),
        in_specs=[a_spec, b_spec],
        out_specs=pl.BlockSpec((tm, tn), lambda g0, g1, l: ij(g0, g1)),
        out_shape=jax.ShapeDtypeStruct((m, n), F32),
        compiler_params=pltpu.CompilerParams(dimension_semantics=("parallel", "parallel", "arbitrary"),
                                             vmem_limit_bytes=VMEM_LIMIT),
    )(a, b)


@functools.partial(jax.custom_vjp, nondiff_argnums=(3,))
def pmm(a, w, wz, name):
    del wz
    return _matmul(a, w, "nn", name + "_f")


def _pmm_fwd(a, w, wz, name):
    del wz
    return _matmul(a, w, "nn", name + "_f"), (a, w)


def _pmm_bwd(name, res, g):
    a, w = res
    da = _matmul(g, w, "nt", name + "_da")
    if a.shape[0] < 128:
        pad = 128 - a.shape[0]
        at = jnp.pad(a.T, ((0, 0), (0, pad)))
        gp = jnp.pad(g, ((0, pad), (0, 0)))
        dw = _matmul(at, gp, "nn", name + "_dw")
    else:
        dw = _matmul(a, g, "tn", name + "_dw")
    return da, jnp.zeros_like(w), dw


pmm.defvjp(_pmm_fwd, _pmm_bwd)


@functools.partial(jax.custom_vjp, nondiff_argnums=(3,))
def pmm_t(a, wt, wtz, name):
    del wtz
    return _matmul(a, wt, "nt", name + "_f")


def _pmm_t_fwd(a, wt, wtz, name):
    del wtz
    return _matmul(a, wt, "nt", name + "_f"), (a, wt)


def _pmm_t_bwd(name, res, g):
    a, wt = res
    return _matmul(g, wt, "nn", name + "_da"), jnp.zeros_like(wt), _matmul(g, a, "tn", name + "_dw")


pmm_t.defvjp(_pmm_t_fwd, _pmm_t_bwd)


def rowop(fn, name, rows, bcs=(), crows=(), cbcs=(), tr=256):
    rows, bcs, crows, cbcs = tuple(rows), tuple(bcs), tuple(crows), tuple(cbcs)
    n_rows = rows[0].shape[0]
    tr = _pick(n_rows, (tr, 128, 64, 32, 16, 8))
    nr, nb, ncr, ncb = len(rows), len(bcs), len(crows), len(cbcs)
    n_in = nr + nb + ncr + ncb
    grid = (n_rows // tr,)

    def blk(arr):
        return jax.ShapeDtypeStruct((tr, arr.shape[1]), arr.dtype)

    def row_spec(arr):
        return pl.BlockSpec((tr, arr.shape[1]), lambda i: (i, 0))

    def bc_spec(arr):
        return pl.BlockSpec(arr.shape, lambda i: (0, 0))

    out_blk = jax.eval_shape(fn, *[blk(r) for r in rows], *bcs, *[blk(r) for r in crows], *cbcs)
    n_out = len(out_blk)
    out_shape = tuple(jax.ShapeDtypeStruct((n_rows, o.shape[1]), o.dtype) for o in out_blk)
    in_specs = ([row_spec(r) for r in rows] + [bc_spec(b) for b in bcs]
                + [row_spec(r) for r in crows] + [bc_spec(b) for b in cbcs])

    def order(vals):
        return vals

    def fwd_call(args):
        def body(*refs):
            vals = [r[...] for r in refs[:n_in]]
            res = fn(*order(vals))
            for o_ref, r in zip(refs[n_in:], res):
                o_ref[...] = r

        return pl.pallas_call(
            body, name=name + "_f", grid=grid, in_specs=in_specs,
            out_specs=[row_spec(o) for o in out_shape], out_shape=out_shape,
            compiler_params=pltpu.CompilerParams(dimension_semantics=("parallel",), vmem_limit_bytes=VMEM_LIMIT),
        )(*args)

    def bwd_call(args, cts):
        def body(*refs):
            vals = [r[...] for r in refs[:n_in]]
            ct_refs = refs[n_in:n_in + n_out]
            d_rows = refs[n_in + n_out:n_in + n_out + nr]
            d_bcs = refs[n_in + n_out + nr:]
            consts = vals[nr + nb:]
            _, vjp = jax.vjp(lambda *p: fn(*p, *consts), *vals[:nr + nb])
            grads = vjp(tuple(c[...] for c in ct_refs))
            for ref, g in zip(d_rows, grads[:nr]):
                ref[...] = g

            @pl.when(pl.program_id(0) == 0)
            def _():
                for ref in d_bcs:
                    ref[...] = jnp.zeros_like(ref)

            for ref, g in zip(d_bcs, grads[nr:]):
                ref[...] += g

        d_shape = tuple(jax.ShapeDtypeStruct(r.shape, r.dtype) for r in rows + bcs)
        return pl.pallas_call(
            body, name=name + "_b", grid=grid,
            in_specs=in_specs + [row_spec(o) for o in out_shape],
            out_specs=[row_spec(r) for r in rows] + [bc_spec(b) for b in bcs], out_shape=d_shape,
            compiler_params=pltpu.CompilerParams(dimension_semantics=("arbitrary",), vmem_limit_bytes=VMEM_LIMIT),
        )(*args, *cts)

    @jax.custom_vjp
    def op(diff, const):
        return fwd_call(diff + const)

    def op_fwd(diff, const):
        return fwd_call(diff + const), (diff, const)

    def op_bwd(res, cts):
        diff, const = res
        grads = bwd_call(diff + const, tuple(cts))
        return tuple(grads), tuple(jnp.zeros_like(c) for c in const)

    op.defvjp(op_fwd, op_bwd)
    return op(rows + bcs, crows + cbcs)


def colop(fn, name, arrays, uses, n_const, nblk, cw=128):
    arrays = tuple(arrays)
    n_diff = len(arrays) - n_const
    nd = sum(1 for u in uses if u[0] < n_diff)
    assert all(u[0] < n_diff for u in uses[:nd]) and all(u[0] >= n_diff for u in uses[nd:])

    def spec(u):
        return pl.BlockSpec((arrays[u[0]].shape[0], cw), lambda j, off=u[1]: (0, off + j))

    def out_spec(rows):
        return pl.BlockSpec((rows, cw), lambda j: (0, j))

    out_blk = jax.eval_shape(fn, *[jax.ShapeDtypeStruct((arrays[u[0]].shape[0], cw), arrays[u[0]].dtype)
                                   for u in uses])
    out_shape = tuple(jax.ShapeDtypeStruct((o.shape[0], nblk * cw), o.dtype) for o in out_blk)
    params = pltpu.CompilerParams(dimension_semantics=("parallel",), vmem_limit_bytes=VMEM_LIMIT)

    def fwd_call(arrs):
        def body(*refs):
            res = fn(*[r[...] for r in refs[:len(uses)]])
            for o_ref, r in zip(refs[len(uses):], res):
                o_ref[...] = r

        return pl.pallas_call(
            body, name=name + "_f", grid=(nblk,), in_specs=[spec(u) for u in uses],
            out_specs=[out_spec(o.shape[0]) for o in out_shape], out_shape=out_shape, compiler_params=params,
        )(*[arrs[u[0]] for u in uses])

    def bwd_call(arrs, cts):
        def body(*refs):
            vals = [r[...] for r in refs[:len(uses)]]
            ct_refs = refs[len(uses):len(uses) + len(out_shape)]
            _, vjp = jax.vjp(lambda *p: fn(*p, *vals[nd:]), *vals[:nd])
            for ref, g in zip(refs[len(uses) + len(out_shape):], vjp(tuple(c[...] for c in ct_refs))):
                ref[...] = g

        d_shape = tuple(jax.ShapeDtypeStruct((arrays[u[0]].shape[0], nblk * cw), F32) for u in uses[:nd])
        return pl.pallas_call(
            body, name=name + "_b", grid=(nblk,),
            in_specs=[spec(u) for u in uses] + [out_spec(o.shape[0]) for o in out_shape],
            out_specs=[out_spec(s.shape[0]) for s in d_shape], out_shape=d_shape, compiler_params=params,
        )(*[arrs[u[0]] for u in uses], *cts)

    @jax.custom_vjp
    def op(diff, const):
        return fwd_call(diff + const)

    def op_fwd(diff, const):
        return fwd_call(diff + const), (diff, const)

    def op_bwd(res, cts):
        diff, const = res
        d_uses = bwd_call(diff + const, tuple(cts))
        grads = []
        for i in range(n_diff):
            parts = sorted([(u[1], k) for k, u in enumerate(uses[:nd]) if u[0] == i])
            grads.append(d_uses[parts[0][1]] if len(parts) == 1
                         else jnp.concatenate([d_uses[k] for _, k in parts], axis=1))
        return tuple(grads), tuple(jnp.zeros_like(c) for c in const)

    op.defvjp(op_fwd, op_bwd)
    return op(arrays[:n_diff], arrays[n_diff:])


@functools.partial(jax.custom_vjp, nondiff_argnums=(1,))
def _roll_rows(x, k):
    return pltpu.roll(x, k % x.shape[0], 0)


def _roll_rows_fwd(x, k):
    return _roll_rows(x, k), None


def _roll_rows_bwd(k, _, g):
    return (_roll_rows(g, -k),)


_roll_rows.defvjp(_roll_rows_fwd, _roll_rows_bwd)


def _conv3(x, w0, w1, w2, starts):
    rows = lax.broadcasted_iota(jnp.int32, x.shape, 0)
    ends = tuple(s - 1 for s in starts[1:]) + (x.shape[0] - 1,)
    first = functools.reduce(jnp.logical_or, [rows == s for s in starts])
    last = functools.reduce(jnp.logical_or, [rows == e for e in ends])
    prev = jnp.where(first, 0.0, _roll_rows(x, 1))
    nxt = jnp.where(last, 0.0, _roll_rows(x, -1))
    return prev * w0 + x * w1 + nxt * w2


def _rms(x):
    return x * lax.rsqrt(jnp.mean(x * x, axis=-1, keepdims=True) + NORM_EPS)


def _heads(x, n):
    return [x[:, h * HEAD_DIM:(h + 1) * HEAD_DIM] for h in range(n)]


_NT = (((1,), (1,)), ((), ()))
_TN = (((0,), (0,)), ((), ()))
_TQ = 256


def _attn_probs(q, k):
    s = lax.dot_general(q, k, _NT, preferred_element_type=F32) * (HEAD_DIM ** -0.5)
    p = jnp.exp(s - jnp.max(s, axis=-1, keepdims=True))
    return p * (1.0 / jnp.sum(p, axis=-1, keepdims=True))


def _attn_fwd_call(q, k, v):
    n, t = q.shape[0], k.shape[0]
    tq = _pick(n, (_TQ, 128))

    def body(q_ref, k_ref, v_ref, o_ref):
        p = _attn_probs(q_ref[...].astype(BF16), k_ref[...].astype(BF16))
        o_ref[...] = jnp.dot(p.astype(BF16), v_ref[...].astype(BF16), preferred_element_type=F32)

    return pl.pallas_call(
        body, name="attn_f", grid=(ATTN_HEADS, n // tq),
        in_specs=[pl.BlockSpec((tq, HEAD_DIM), lambda h, i: (i, h)),
                  pl.BlockSpec((t, HEAD_DIM), lambda h, i: (0, h // ATTN_GROUP)),
                  pl.BlockSpec((t, HEAD_DIM), lambda h, i: (0, h // ATTN_GROUP))],
        out_specs=pl.BlockSpec((tq, HEAD_DIM), lambda h, i: (i, h)),
        out_shape=jax.ShapeDtypeStruct(q.shape, F32),
        compiler_params=pltpu.CompilerParams(dimension_semantics=("parallel", "parallel"),
                                             vmem_limit_bytes=VMEM_LIMIT),
    )(q, k, v)


def _attn_bwd_call(q, k, v, do):
    n, t = q.shape[0], k.shape[0]
    tq = _pick(n, (_TQ, 128))

    def body(q_ref, k_ref, v_ref, do_ref, dq_ref, dk_ref, dv_ref):
        @pl.when((pl.program_id(1) == 0) & (pl.program_id(2) == 0))
        def _():
            dk_ref[...] = jnp.zeros_like(dk_ref)
            dv_ref[...] = jnp.zeros_like(dv_ref)

        qb, kb, vb, dob = (r[...].astype(BF16) for r in (q_ref, k_ref, v_ref, do_ref))
        p = _attn_probs(qb, kb)
        dp = lax.dot_general(dob, vb, _NT, preferred_element_type=F32)
        ds = p * (dp - jnp.sum(p * dp, axis=-1, keepdims=True)) * (HEAD_DIM ** -0.5)
        dsb = ds.astype(BF16)
        dq_ref[...] = jnp.dot(dsb, kb, preferred_element_type=F32)
        dk_ref[...] += lax.dot_general(dsb, qb, _TN, preferred_element_type=F32)
        dv_ref[...] += lax.dot_general(p.astype(BF16), dob, _TN, preferred_element_type=F32)

    q_spec = pl.BlockSpec((tq, HEAD_DIM), lambda kh, g, i: (i, kh * ATTN_GROUP + g))
    kv_spec = pl.BlockSpec((t, HEAD_DIM), lambda kh, g, i: (0, kh))
    return pl.pallas_call(
        body, name="attn_b", grid=(ATTN_KV_HEADS, ATTN_GROUP, n // tq),
        in_specs=[q_spec, kv_spec, kv_spec, q_spec],
        out_specs=[q_spec, kv_spec, kv_spec],
        out_shape=(jax.ShapeDtypeStruct(q.shape, F32), jax.ShapeDtypeStruct(k.shape, F32),
                   jax.ShapeDtypeStruct(v.shape, F32)),
        compiler_params=pltpu.CompilerParams(dimension_semantics=("parallel", "arbitrary", "arbitrary"),
                                             vmem_limit_bytes=VMEM_LIMIT),
    )(q, k, v, do)


@jax.custom_vjp
def attention(q, k, v):
    return _attn_fwd_call(q, k, v)


def _attention_fwd(q, k, v):
    return _attn_fwd_call(q, k, v), (q, k, v)


def _attention_bwd(res, do):
    return _attn_bwd_call(*res, do)


attention.defvjp(_attention_fwd, _attention_bwd)


_C = GDN_CHUNK


def _pdot(a, b):
    return jnp.dot(a, b, precision=lax.Precision.HIGH, preferred_element_type=F32)


@jax.custom_vjp
def _hdot(a, b):
    return jnp.dot(a.astype(BF16), b.astype(BF16), preferred_element_type=F32)


def _hdot_fwd(a, b):
    return _hdot(a, b), (a, b)


def _hdot_bwd(res, g):
    a, b = res
    gb = g.astype(BF16)
    return (lax.dot_general(gb, b.astype(BF16), _NT, preferred_element_type=F32),
            lax.dot_general(a.astype(BF16), gb, _TN, preferred_element_type=F32))


_hdot.defvjp(_hdot_fwd, _hdot_bwd)


def _each(fn, *lists):
    return [fn(*args) for args in zip(*lists)]


def _unit_lower_inverse(low, blockdiag):
    eye = (lax.broadcasted_iota(jnp.int32, (_C, _C), 0) == lax.broadcasted_iota(jnp.int32, (_C, _C), 1)).astype(F32)
    ld = _each(lambda a: a * blockdiag, low)
    lo = _each(lambda a, d: a - d, low, ld)
    l2 = _each(_hdot, ld, ld)
    l4 = _each(_hdot, l2, l2)
    l8 = _each(_hdot, l4, l4)
    td = _each(lambda d, a2: _hdot(eye - d, eye + a2), ld, l2)
    td = _each(lambda t, a4: _hdot(t, eye + a4), td, l4)
    td = _each(lambda t, a8: _hdot(t, eye + a8), td, l8)
    nn = _each(_hdot, td, lo)
    n2 = _each(_hdot, nn, nn)
    out = _each(lambda n, m2: _hdot(eye - n, eye + m2), nn, n2)
    return _each(_hdot, out, td)


def _gdn_chunks(heads, blockdiag):
    q, k, v, b_b, be_b, e_b, kd_b, m1, dec, gl, s = (list(col) for col in zip(*heads))
    f32dot = lambda a, b: jnp.dot(a, b, preferred_element_type=F32)
    nt = lambda a, b: lax.dot_general(a, b, _NT, preferred_element_type=F32)
    kk = _each(nt, k, k)
    t_inv = _unit_lower_inverse(_each(lambda m, a: m * a, m1, kk), blockdiag)
    u = _each(lambda t, b, x: _hdot(t, b * x), t_inv, b_b, v)
    w = _each(lambda t, b, x: _hdot(t, b * x), t_inv, be_b, k)
    delta = _each(lambda uu, ww, ss: uu - f32dot(ww, ss), u, w, s)
    p = _each(lambda d, qq, kx: d * nt(qq, kx), dec, q, k)
    o = _each(lambda qq, e, ss, pp, dd: f32dot(qq * e, ss) + f32dot(pp, dd), q, e_b, s, p, delta)
    s_new = _each(lambda g, ss, kx, kd, dd: g * ss + lax.dot_general(kx * kd, dd, _TN, preferred_element_type=F32),
                  gl, s, k, kd_b, delta)
    return o, s_new


def _blockdiag_mask():
    r = lax.broadcasted_iota(jnp.int32, (_C, _C), 0) >> 4
    c = lax.broadcasted_iota(jnp.int32, (_C, _C), 1) >> 4
    return (r == c).astype(F32)


def _gdn_specs(nc, ncc, reverse, backward):
    def ch(s):
        s = nc - 1 - s if backward else s
        return jnp.where(s < ncc, ncc - 1 - s, nc + ncc - 1 - s) if reverse else s

    tok = pl.BlockSpec((_C, 3 * GDN_W), lambda s: (ch(s), 0))
    park = nc - ncc - 1 if reverse else 0
    out = pl.BlockSpec((_C, GDN_W), lambda s: (jnp.where(ch(s) >= ncc, ch(s) - ncc, park), 0))
    per_tok = pl.BlockSpec((GDN_HEADS, _C, HEAD_DIM), lambda s: (0, ch(s), 0))
    mat = pl.BlockSpec((GDN_HEADS, None, _C, _C), lambda s: (0, ch(s), 0, 0))
    row = pl.BlockSpec((GDN_HEADS, None, 1, HEAD_DIM), lambda s: (0, ch(s), 0, 0))
    state = pl.BlockSpec((GDN_HEADS, None, HEAD_DIM, HEAD_DIM), lambda s: (0, ch(s), 0, 0))
    return tok, out, per_tok, mat, row, state, ch


def _head_cols(h, part):
    return slice((part * GDN_HEADS + h) * HEAD_DIM, (part * GDN_HEADS + h + 1) * HEAD_DIM)


def _gdn_heads(qkv_ref, factor_refs, state_ref, first):
    return [[qkv_ref[:, _head_cols(h, 0)], qkv_ref[:, _head_cols(h, 1)], qkv_ref[:, _head_cols(h, 2)]]
            + [r[h] for r in factor_refs] + [state_ref[first + h]] for h in range(GDN_HEADS)]


def _gdn_fwd_call(ncc, qkv, factors):
    t = qkv.shape[0]
    nc, nd = t // _C, len(factors)
    specs = [_gdn_specs(nc, ncc, d == 1, False) for d in range(nd)]

    def body(*refs):
        ins_d = [refs[8 * d:8 * d + 8] for d in range(nd)]
        outs_d = [refs[8 * nd + 2 * d:8 * nd + 2 * d + 2] for d in range(nd)]
        s_ref = refs[-1]

        @pl.when(pl.program_id(0) == 0)
        def _():
            s_ref[...] = jnp.zeros_like(s_ref)

        heads = sum([_gdn_heads(r[0], r[1:], s_ref, GDN_HEADS * d) for d, r in enumerate(ins_d)], [])
        o, s_new = _gdn_chunks(heads, _blockdiag_mask())
        for d, (o_ref, sall_ref) in enumerate(outs_d):
            for h in range(GDN_HEADS):
                i = GDN_HEADS * d + h
                sall_ref[h] = heads[i][10]
                o_ref[:, _head_cols(h, 0)] = o[i]
                s_ref[i] = s_new[i]

    in_specs, out_specs, out_shape, operands = [], [], [], []
    for (tok, out, per_tok, mat, row, state, _), f in zip(specs, factors):
        in_specs += [tok, per_tok, per_tok, per_tok, per_tok, mat, mat, row]
        out_specs += [out, state]
        out_shape += [jax.ShapeDtypeStruct((t - ncc * _C, GDN_W), F32),
                      jax.ShapeDtypeStruct((GDN_HEADS, nc, HEAD_DIM, HEAD_DIM), F32)]
        operands += [qkv, *f]
    res = pl.pallas_call(
        body, name="gdn_f", grid=(nc,), in_specs=in_specs, out_specs=out_specs, out_shape=out_shape,
        scratch_shapes=[pltpu.VMEM((nd * GDN_HEADS, HEAD_DIM, HEAD_DIM), F32)],
        compiler_params=pltpu.CompilerParams(dimension_semantics=("arbitrary",), vmem_limit_bytes=VMEM_LIMIT),
    )(*operands)
    return [(res[2 * d], res[2 * d + 1]) for d in range(nd)]


def _gdn_bwd_call(ncc, qkv, factors, salls, dos):
    t = qkv.shape[0]
    nc, nd = t // _C, len(factors)
    specs = [_gdn_specs(nc, ncc, d == 1, True) for d in range(nd)]

    def body(*refs):
        ins_d = [refs[10 * d:10 * d + 10] for d in range(nd)]
        outs_d = [refs[10 * nd + 8 * d:10 * nd + 8 * d + 8] for d in range(nd)]
        ds_ref = refs[-1]

        @pl.when(pl.program_id(0) == 0)
        def _():
            ds_ref[...] = jnp.zeros_like(ds_ref)

        bd = _blockdiag_mask()
        heads = sum([_gdn_heads(r[0], r[1:8], r[8], 0) for r in ins_d], [])
        _, vjp = jax.vjp(lambda hs: _gdn_chunks(hs, bd), heads)
        live = [spec[6](pl.program_id(0)) >= ncc for spec in specs]
        (all_grads,) = vjp(([jnp.where(live[d], r[9][:, _head_cols(h, 0)], 0.0)
                             for d, r in enumerate(ins_d) for h in range(GDN_HEADS)],
                            [ds_ref[i] for i in range(nd * GDN_HEADS)]))
        for d, out_refs in enumerate(outs_d):
            for h in range(GDN_HEADS):
                grads = all_grads[GDN_HEADS * d + h]
                for part in range(3):
                    out_refs[0][:, _head_cols(h, part)] = grads[part]
                for ref, g in zip(out_refs[1:], grads[3:10]):
                    ref[h] = g
                ds_ref[GDN_HEADS * d + h] = grads[10]

    shp = lambda a: jax.ShapeDtypeStruct(a.shape, F32)
    in_specs, out_specs, out_shape, operands = [], [], [], []
    for (tok, out, per_tok, mat, row, state, _), f, sall, do in zip(specs, factors, salls, dos):
        in_specs += [tok, per_tok, per_tok, per_tok, per_tok, mat, mat, row, state, out]
        out_specs += [tok, per_tok, per_tok, per_tok, per_tok, mat, mat, row]
        out_shape += [shp(qkv)] + [shp(a) for a in f]
        operands += [qkv, *f, sall, do]
    res = pl.pallas_call(
        body, name="gdn_b", grid=(nc,), in_specs=in_specs, out_specs=out_specs, out_shape=out_shape,
        scratch_shapes=[pltpu.VMEM((nd * GDN_HEADS, HEAD_DIM, HEAD_DIM), F32)],
        compiler_params=pltpu.CompilerParams(dimension_semantics=("arbitrary",), vmem_limit_bytes=VMEM_LIMIT),
    )(*operands)
    return [res[8 * d:8 * d + 8] for d in range(nd)]


@functools.partial(jax.custom_vjp, nondiff_argnums=(0,))
def gdn_scan(ncc, qkv, f_fwd, f_rev):
    (o0, _), (o1, _) = _gdn_fwd_call(ncc, qkv, [f_fwd, f_rev])
    return o0, o1


def _gdn_scan_fwd(ncc, qkv, f_fwd, f_rev):
    (o0, s0), (o1, s1) = _gdn_fwd_call(ncc, qkv, [f_fwd, f_rev])
    return (o0, o1), (qkv, f_fwd, f_rev, s0, s1)


def _gdn_scan_bwd(ncc, res, dos):
    qkv, f_fwd, f_rev, s0, s1 = res
    g0, g1 = _gdn_bwd_call(ncc, qkv, [f_fwd, f_rev], [s0, s1], list(dos))
    return g0[0] + g1[0], tuple(g0[1:]), tuple(g1[1:])


gdn_scan.defvjp(_gdn_scan_fwd, _gdn_scan_bwd)


def _rope_tables(n, cl):
    t = np.arange(n)
    inv_freq = (ROPE_THETA ** (-np.arange(0, HEAD_DIM // 2, 2, dtype=np.float32) / (HEAD_DIM // 2))).astype(np.float32)
    ang_r = (t // GRID_W).astype(np.float32)[:, None] * inv_freq
    ang_c = (t % GRID_W).astype(np.float32)[:, None] * inv_freq
    cos = np.concatenate([np.cos(ang_r), np.cos(ang_r), np.cos(ang_c), np.cos(ang_c)], axis=1)
    sin = np.concatenate([-np.sin(ang_r), np.sin(ang_r), -np.sin(ang_c), np.sin(ang_c)], axis=1)
    cos_all = np.concatenate([np.ones((cl, HEAD_DIM), np.float32), cos], axis=0)
    sin_all = np.concatenate([np.zeros((cl, HEAD_DIM), np.float32), sin], axis=0)
    j = np.arange(HEAD_DIM)
    src = np.where((j % 64) < 32, j + 32, j - 32)
    perm = np.zeros((HEAD_DIM, HEAD_DIM), np.float32)
    perm[src, j] = 1.0
    return (jnp.asarray(cos.astype(np.float32)), jnp.asarray(sin.astype(np.float32)),
            jnp.asarray(cos_all), jnp.asarray(sin_all), jnp.asarray(perm))


def _gdn_factors(log_a, beta, reverse):
    t = log_a.shape[0]
    nc = t // _C
    la = log_a.reshape(nc, _C, GDN_HEADS).transpose(2, 0, 1)
    be = beta.reshape(nc, _C, GDN_HEADS).transpose(2, 0, 1)
    gam = lax.cumsum(la, axis=2, reverse=reverse)
    idx = jnp.arange(_C)
    incl = (idx[:, None] <= idx[None, :]) if reverse else (idx[:, None] >= idx[None, :])
    strict = (idx[:, None] < idx[None, :]) if reverse else (idx[:, None] > idx[None, :])
    dec = jnp.exp(jnp.where(incl, gam[..., :, None] - gam[..., None, :], -jnp.inf))
    m1 = jnp.where(strict, be[..., :, None] * dec, 0.0)
    e = jnp.exp(gam)
    g_last = gam[..., :1] if reverse else gam[..., -1:]
    lanes = lambda a: jnp.broadcast_to(a.reshape(GDN_HEADS, t, 1), (GDN_HEADS, t, HEAD_DIM))
    gl = jnp.broadcast_to(jnp.exp(g_last)[..., None], (GDN_HEADS, nc, 1, HEAD_DIM))
    return lanes(be), lanes(be * e), lanes(e), lanes(jnp.exp(g_last - gam)), m1, dec, gl


def local_loss(x, wz, wb, ws, c, ctx, target):
    n, cl = x.shape[0], ctx.shape[0]
    cos_q, sin_q, cos_k, sin_k, perm = _rope_tables(n, cl)

    sc_in = jnp.concatenate([jax.nn.silu(c), jax.nn.silu(ws["c_ctx"])[None, :], jnp.zeros((14, D_MODEL), F32)], axis=0)
    mod = pmm(sc_in, wb["w_mod"], wz["w_mod"], "mm_mod") + ws["b_mod"]
    sh1, sc1, g1, sh2, sc2, g2 = [mod[0:1, i * D_MODEL:(i + 1) * D_MODEL] for i in range(6)]
    csh1, csc1 = mod[1:2, 0:D_MODEL], mod[1:2, D_MODEL:2 * D_MODEL]

    def norm_mod(a, sh, sc):
        return (_rms(a) * (1.0 + sc) + sh,)

    (hx,) = rowop(norm_mod, "normmod_x", (x,), (sh1, sc1))
    (hc,) = rowop(norm_mod, "normmod_c", (ctx,), (csh1, csc1))
    h_all = jnp.concatenate([hc, hx], axis=0)
    p_main = pmm_t(h_all, wb["w_in_main"], wz["w_in_main"], "mm_in")
    p_small = pmm_t(h_all, wb["w_in_small"], wz["w_in_small"], "mm_ins")
    ak, av, qkv, aq, z, gate = jnp.split(p_main, [KV_W, 2 * KV_W, SMALL_AT, SMALL_AT + Q_W, SMALL_AT + Q_W + GDN_W],
                                         axis=1)
    db, da = p_small[:, :2 * GDN_HEADS], p_small[:, 2 * GDN_HEADS:4 * GDN_HEADS]

    def qk_prep(nh):
        def fn(a, w, cos, sin, pm):
            outs = []
            for ah in _heads(a, nh):
                y = _rms(ah) * w
                outs.append(y * cos + _pdot(y, pm) * sin)
            return (jnp.concatenate(outs, axis=1),)
        return fn

    (q_x,) = rowop(qk_prep(ATTN_HEADS), "q_prep", (aq[cl:],), (ws["q_norm_w"],), (cos_q, sin_q), (perm,))
    (k_all,) = rowop(qk_prep(ATTN_KV_HEADS), "k_prep", (ak,), (ws["k_norm_w"],), (cos_k, sin_k), (perm,))
    attn_x = attention(q_x, k_all, av)

    cw = ws["conv_qkv_w"]
    normed = jnp.asarray(np.repeat([1.0, 1.0, 0.0], GDN_W)[None, :], F32)
    scale = jnp.asarray(np.repeat([HEAD_DIM ** -0.5, 1.0, 1.0], GDN_W)[None, :], F32)

    def gdn_prep(a, w0, w1, w2, nf, sc):
        s = jax.nn.silu(_conv3(a, w0, w1, w2, (0, cl)))
        inv = lax.rsqrt(jnp.sum(s * s, axis=-1, keepdims=True) + NORM_EPS)
        return (s * jnp.where(nf > 0.0, inv * sc, 1.0),)

    (qkvn,) = colop(gdn_prep, "gdn_prep", (qkv, cw[0:1], cw[1:2], cw[2:3], normed, scale),
                    [(i, 0) for i in range(6)], 2, 3 * GDN_HEADS)
    beta = jax.nn.sigmoid(db).reshape(-1, 2, GDN_HEADS)
    log_a = -jnp.exp(ws["a_log"])[None] * jax.nn.softplus(da.reshape(-1, 2, GDN_HEADS) + ws["dt_bias"][None])
    o_fwd, o_rev = gdn_scan(cl // _C, qkvn, _gdn_factors(log_a[:, 0], beta[:, 0], False),
                            _gdn_factors(log_a[:, 1], beta[:, 1], True))
    o_x = o_fwd + o_rev

    def gdn_out(o, zz, w):
        outs = [_rms(oh) * w * jax.nn.silu(zh) for oh, zh in zip(_heads(o, GDN_HEADS), _heads(zz, GDN_HEADS))]
        return (jnp.concatenate(outs, axis=1),)

    (gdn_x,) = rowop(gdn_out, "gdn_out", (o_x, z[cl:]), (ws["gdn_norm_w"],))

    pa = pmm(attn_x, wb["w_pa"], wz["w_pa"], "mm_pa")
    pd = pmm(gdn_x, wb["w_pd"], wz["w_pd"], "mm_pd")

    def merge(a, d, g):
        return (jax.nn.sigmoid(g[:, :D_MODEL]) * a + jax.nn.sigmoid(g[:, D_MODEL:]) * d,)

    (y,) = rowop(merge, "merge", (pa, pd, gate[cl:]))
    mo = pmm(y, wb["w_out"], wz["w_out"], "mm_out")

    def res_norm_mod(xx, m, g, sh, sc):
        x1 = xx + g * m
        return x1, _rms(x1) * (1.0 + sc) + sh

    x1, h2 = rowop(res_norm_mod, "res1", (x, mo), (g1, sh2, sc2))
    up = pmm(h2, wb["w_up"], wz["w_up"], "mm_up")
    fw = ws["ffn_conv_w"]

    def ffn_act(ug, uv, w0g, w0v, w1g, w1v, w2g, w2v, bg, bv):
        g = _conv3(ug, w0g, w1g, w2g, (0,)) + bg
        v = _conv3(uv, w0v, w1v, w2v, (0,)) + bv
        return (jax.nn.silu(g) * v,)

    half = D_FF // HEAD_DIM
    (act,) = colop(ffn_act, "ffn_act", (up, fw[0:1], fw[1:2], fw[2:3], ws["ffn_conv_b"]),
                   [(i, off) for i in range(5) for off in (0, half)], 0, half)
    dn = pmm(act, wb["w_down"], wz["w_down"], "mm_down")

    def head(xx, m, g, w, tgt):
        yy = _rms(xx + g * m) * w
        err = (yy - tgt) ** 2
        return (jnp.broadcast_to(0.5 * jnp.mean(err, axis=-1, keepdims=True), (xx.shape[0], HEAD_DIM)),)

    (row_loss,) = rowop(head, "head", (x1, dn), (g2, ws["final_norm_w"][None, :]), (target,))
    return jnp.sum(row_loss[:, 0])


_HBM = pl.BlockSpec(memory_space=pltpu.HBM)


def _chip_peers():
    x, y = lax.axis_index("x"), lax.axis_index("y")
    return [(1 - x, y), (x, 1 - y), (1 - x, 1 - y)]


_SPLIT_COLS = ("w_in",)


def _half_of(view, nm, idx, lead=0):
    r, cdim = view.shape[-2:]
    pre = (slice(None),) * lead
    if nm in _SPLIT_COLS:
        return view.at[pre + (slice(None), pl.ds(pl.multiple_of(idx * (cdim // 2), 128), cdim // 2))]
    return view.at[pre + (pl.ds(pl.multiple_of(idx * (r // 2), 16), r // 2), slice(None))]


def _remote(src, dst, send_sem, recv_sem, dev):
    return pltpu.make_async_remote_copy(src_ref=src, dst_ref=dst, send_sem=send_sem, recv_sem=recv_sem,
                                        device_id=dev, device_id_type=MESH)


def _hbm_call(body, name, ins, out_shape, n_sems, in_place=False):
    names = tuple(ins)
    return dict(zip(names, pl.pallas_call(
        body, name=name, in_specs=[_HBM] * len(names), out_specs=[_HBM] * len(names),
        out_shape=[out_shape(nm, ins[nm]) for nm in names],
        scratch_shapes=[pltpu.SemaphoreType.DMA((k,)) for k in n_sems],
        input_output_aliases={i: i for i in range(len(names))} if in_place else {},
    )(*[ins[nm] for nm in names])))


def all_gather_chips(shards):
    names = tuple(shards)
    n = len(names)

    def body(*refs):
        ins, outs = dict(zip(names, refs[:n])), dict(zip(names, refs[n:2 * n]))
        ici_send, ici_recv, d2d_send, d2d_recv, own_send, own_recv = refs[2 * n:]
        x, y, c = lax.axis_index("x"), lax.axis_index("y"), lax.axis_index("c")
        me, sib = 2 * x + y, (x, y, 1 - c)
        own = [_remote(ins[nm], outs[nm].at[me], own_send.at[i], own_recv.at[i], sib) for i, nm in enumerate(names)]
        for cp in own:
            cp.start()
        sends = []
        for k, (px, py) in enumerate(_chip_peers()):
            for i, nm in enumerate(names):
                cp = _remote(_half_of(ins[nm], nm, c), _half_of(outs[nm].at[me], nm, c), ici_send.at[k * n + i],
                             ici_recv.at[k * n + i], (px, py, c))
                cp.start()
                sends.append(cp)
        for k, (px, py) in enumerate(_chip_peers()):
            for i, nm in enumerate(names):
                landed = _half_of(outs[nm].at[2 * px + py], nm, c)
                _remote(landed, landed, ici_send.at[k * n + i], ici_recv.at[k * n + i], (px, py, c)).wait_recv()
                fw = _remote(landed, landed, d2d_send.at[k * n + i], d2d_recv.at[k * n + i], sib)
                fw.start()
                sends.append(fw)
        for k, (px, py) in enumerate(_chip_peers()):
            for i, nm in enumerate(names):
                other = _half_of(outs[nm].at[2 * px + py], nm, 1 - c)
                _remote(other, other, d2d_send.at[k * n + i], d2d_recv.at[k * n + i], sib).wait_recv()
        for cp in sends:
            cp.wait_send()
        for cp in own:
            cp.wait()

    return _hbm_call(body, "ag_weights", shards, lambda nm, a: jax.ShapeDtypeStruct((N_CHIPS,) + a.shape, a.dtype),
                     (3 * n, 3 * n, 3 * n, 3 * n, n, n))


def sibling_halves(blocks):
    names = tuple(blocks)

    def body(*refs):
        n = len(names)
        ins, outs = dict(zip(names, refs[:n])), dict(zip(names, refs[n:2 * n]))
        send_sems, recv_sems = refs[2 * n:]
        x, y, c = lax.axis_index("x"), lax.axis_index("y"), lax.axis_index("c")
        cps = [_remote(_half_of(ins[nm], nm, 1 - c, lead=1), outs[nm], send_sems.at[i], recv_sems.at[i], (x, y, 1 - c))
               for i, nm in enumerate(names)]
        for cp in cps:
            cp.start()
        for cp in cps:
            cp.wait()

    def half_shape(nm, a):
        r, cdim = a.shape[-2:]
        return jax.ShapeDtypeStruct((N_CHIPS, r, cdim // 2) if nm in _SPLIT_COLS else (N_CHIPS, r // 2, cdim), a.dtype)

    return _hbm_call(body, "rs_sibling", blocks, half_shape, (len(names), len(names)))


def scatter_halves(blocks):
    names = tuple(blocks)
    n = len(names)

    def body(*refs):
        ins, outs = dict(zip(names, refs[:n])), dict(zip(names, refs[n:2 * n]))
        send_sems, recv_sems = refs[2 * n:]
        c = lax.axis_index("c")
        cps = [_remote(ins[nm].at[2 * px + py], outs[nm].at[k], send_sems.at[k * n + i], recv_sems.at[k * n + i],
                       (px, py, c))
               for k, (px, py) in enumerate(_chip_peers()) for i, nm in enumerate(names)]
        for cp in cps:
            cp.start()
        for cp in cps:
            cp.wait_recv()
        for cp in cps:
            cp.wait_send()

    return _hbm_call(body, "rs_grads", blocks, lambda nm, a: jax.ShapeDtypeStruct((3,) + a.shape[1:], a.dtype),
                     (3 * n, 3 * n))


def sibling_assemble(arrays):
    names = tuple(arrays)

    def body(*refs):
        n = len(names)
        ins, outs = dict(zip(names, refs[:n])), dict(zip(names, refs[n:2 * n]))
        send_sems, recv_sems = refs[2 * n:]
        x, y, c = lax.axis_index("x"), lax.axis_index("y"), lax.axis_index("c")
        cps = [_remote(_half_of(ins[nm], nm, c), _half_of(outs[nm], nm, c), send_sems.at[i], recv_sems.at[i],
                       (x, y, 1 - c)) for i, nm in enumerate(names)]
        for cp in cps:
            cp.start()
        for i, nm in enumerate(names):
            other = _half_of(outs[nm], nm, 1 - c)
            _remote(other, other, send_sems.at[i], recv_sems.at[i], (x, y, 1 - c)).wait_recv()
        for cp in cps:
            cp.wait_send()

    return _hbm_call(body, "rs_assemble", arrays, lambda nm, a: jax.ShapeDtypeStruct(a.shape, a.dtype),
                     (len(names), len(names)), in_place=True)


def all_reduce_small(v):
    def body(v_ref, tot_ref, gath_ref, send_sems, recv_sems):
        x, y, c = lax.axis_index("x"), lax.axis_index("y"), lax.axis_index("c")
        me = 4 * x + 2 * y + c
        gath_ref[me] = v_ref[...]

        def peer(k):
            m = k + 1
            return (x ^ (m >> 2 & 1), y ^ (m >> 1 & 1), c ^ (m & 1))

        sends = [pltpu.make_async_remote_copy(src_ref=v_ref, dst_ref=gath_ref.at[me], send_sem=send_sems.at[k],
                                              recv_sem=recv_sems.at[k], device_id=peer(k), device_id_type=MESH)
                 for k in range(N_DEV - 1)]
        for cp in sends:
            cp.start()
        for k in range(N_DEV - 1):
            px, py, pc = peer(k)
            pltpu.make_async_remote_copy(src_ref=v_ref, dst_ref=gath_ref.at[4 * px + 2 * py + pc],
                                         send_sem=send_sems.at[k], recv_sem=recv_sems.at[k], device_id=peer(k),
                                         device_id_type=MESH).wait_recv()
        for cp in sends:
            cp.wait_send()
        acc = gath_ref[0]
        for d in range(1, N_DEV):
            acc = acc + gath_ref[d]
        tot_ref[...] = acc

    vm = pl.BlockSpec(memory_space=pltpu.VMEM)
    return pl.pallas_call(
        body, name="ar_small", in_specs=[vm], out_specs=[vm, vm],
        out_shape=(jax.ShapeDtypeStruct(v.shape, v.dtype), jax.ShapeDtypeStruct((N_DEV,) + v.shape, v.dtype)),
        scratch_shapes=[pltpu.SemaphoreType.DMA((N_DEV - 1,)), pltpu.SemaphoreType.DMA((N_DEV - 1,))],
    )(v)[0]


def _elementwise(fn, name, ins, n_out, out_dtype=F32):
    r, cdim = ins[0].shape
    tr = _pick(r, tuple(p for p in (488, 256, 128, 104, 64, 32, 16, 8) if p * cdim * 4 <= 2 * 1024 * 1024))
    spec = pl.BlockSpec((tr, cdim), lambda i: (i, 0))

    def body(*refs):
        res = fn(*[ref[...] for ref in refs[:len(ins)]])
        for o_ref, v in zip(refs[len(ins):], res):
            o_ref[...] = v

    return pl.pallas_call(
        body, name=name, grid=(r // tr,), in_specs=[spec] * len(ins), out_specs=[spec] * n_out,
        out_shape=tuple(jax.ShapeDtypeStruct((r, cdim), out_dtype) for _ in range(n_out)),
        compiler_params=pltpu.CompilerParams(dimension_semantics=("parallel",), vmem_limit_bytes=VMEM_LIMIT),
    )(*ins)


def _half_block_specs(nm, shard_shape):
    r, cdim = shard_shape
    if nm in _SPLIT_COLS:
        return (None, r, cdim // 2), (lambda j, c: (j, 0, c))
    return (None, r // 2, cdim), (lambda j, c: (j, c, 0))


def _presum(nm, sel, g32, a):
    blk, at = _half_block_specs(nm, g32.shape[1:])

    def body(s_ref, g_ref, a_ref, o_ref):
        del s_ref
        o_ref[...] = (g_ref[...] + a_ref[...]).astype(BF16)

    return pl.pallas_call(
        body, name="rs_presum_" + nm,
        grid_spec=pltpu.PrefetchScalarGridSpec(
            num_scalar_prefetch=1, grid=(N_CHIPS,),
            in_specs=[pl.BlockSpec(blk, lambda j, s: at(j, s[0])), pl.BlockSpec(blk, lambda j, s: (j, 0, 0))],
            out_specs=pl.BlockSpec(blk, lambda j, s: (j, 0, 0))),
        out_shape=jax.ShapeDtypeStruct(a.shape, BF16),
        compiler_params=pltpu.CompilerParams(dimension_semantics=("parallel",), vmem_limit_bytes=VMEM_LIMIT),
    )(sel, g32, a)


def _finalsum(nm, sel, g32, a, got):
    blk, at = _half_block_specs(nm, g32.shape[1:])

    def body(s_ref, g_ref, a_ref, r_ref, o_ref):
        del s_ref
        acc = g_ref[...] + a_ref[...]
        for k in range(3):
            acc = acc + r_ref[k].astype(F32)
        o_ref[...] = acc

    return pl.pallas_call(
        body, name="rs_final_" + nm,
        grid_spec=pltpu.PrefetchScalarGridSpec(
            num_scalar_prefetch=1, grid=(1,),
            in_specs=[pl.BlockSpec(blk, lambda i, s: at(s[1], s[0])), pl.BlockSpec(blk, lambda i, s: (s[1], 0, 0)),
                      pl.BlockSpec(got.shape, lambda i, s: (0, 0, 0))],
            out_specs=pl.BlockSpec(blk[1:], lambda i, s: at(0, s[0])[1:])),
        out_shape=jax.ShapeDtypeStruct(g32.shape[1:], F32),
        compiler_params=pltpu.CompilerParams(dimension_semantics=("arbitrary",), vmem_limit_bytes=VMEM_LIMIT),
    )(sel, g32, a, got)


def _adamw(w, g, m, v, name):
    shape = w.shape
    to2 = lambda a: a.reshape(-1, shape[-1])

    def fn(w_, g_, m_, v_):
        m_new = ADAM_B1 * m_ + (1.0 - ADAM_B1) * g_
        v_new = ADAM_B2 * v_ + (1.0 - ADAM_B2) * (g_ * g_)
        m_hat = m_new / (1.0 - ADAM_B1 ** ADAM_STEP)
        v_hat = v_new / (1.0 - ADAM_B2 ** ADAM_STEP)
        delta = -ADAM_LR * (m_hat / (jnp.sqrt(v_hat) + ADAM_EPS) + ADAM_WD * w_)
        return delta, m_new, v_new

    outs = _elementwise(fn, name, [to2(a) for a in (w, g, m, v)], 3)
    return tuple(o.reshape(shape) for o in outs)


_BIG = ("w_mod", "w_in", "w_pa", "w_pd", "w_out", "w_up", "w_down")
_COL_SHARDED = ("w_mod", "w_up")
_FULL_SHAPE = {"w_mod": (D_MODEL, MOD_W), "w_in": (IN_COLS, D_MODEL), "w_pa": (Q_W, D_MODEL), "w_pd": (GDN_W, D_MODEL),
               "w_out": (D_MODEL, D_MODEL), "w_up": (D_MODEL, 2 * D_FF), "w_down": (D_FF, D_MODEL)}


def _shard_shape(name):
    r, cdim = _FULL_SHAPE[name]
    return (r, cdim // N_CHIPS) if name in _COL_SHARDED else (r // N_CHIPS, cdim)


_CONV_ELEMS = 2 * (3 * CONV_W // N_CHIPS + 3 * 2 * D_FF // N_CHIPS)
_CONV_ROWS = 32


def _blocks_of_full(name, full):
    r, cdim = _FULL_SHAPE[name]
    if name in _COL_SHARDED:
        return full.reshape(r, N_CHIPS, cdim // N_CHIPS).transpose(1, 0, 2)
    return full.reshape(N_CHIPS, r // N_CHIPS, cdim)


def _full_of_blocks(name, blocks):
    r, cdim = _FULL_SHAPE[name]
    if name in _COL_SHARDED:
        return blocks.transpose(1, 0, 2).reshape(r, cdim)
    return blocks.reshape(r, cdim)


def _w_in_regroup(w_in_t):
    main = jnp.concatenate([w_in_t[:SMALL_AT], w_in_t[SMALL_AT + 4 * GDN_HEADS:]], axis=0)
    small = jnp.pad(w_in_t[SMALL_AT:SMALL_AT + 4 * GDN_HEADS], ((0, HEAD_DIM - 4 * GDN_HEADS), (0, 0)))
    return main, small


def _w_in_ungroup(main, small):
    return jnp.concatenate([main[:SMALL_AT], small[:4 * GDN_HEADS], main[SMALL_AT:]], axis=0)


_SMALL = ("c_ctx", "b_mod", "q_norm_w", "k_norm_w", "conv_qkv_w", "a_log", "dt_bias", "gdn_norm_w", "ffn_conv_w",
          "ffn_conv_b", "final_norm_w")


def _pack_small(tree, rows):
    flat = jnp.concatenate([tree[nm].reshape(-1) for nm in _SMALL])
    return jnp.pad(flat, (0, rows * 128 - flat.shape[0])).reshape(rows, 128)


def _unpack_small(packed, like):
    flat, out, off = packed.reshape(-1), {}, 0
    for nm in _SMALL:
        size = int(np.prod(like[nm].shape))
        out[nm] = flat[off:off + size].reshape(like[nm].shape)
        off += size
    return out


def kernel(x, c, ctx, c_ctx, w_mod, b_mod, w_in, q_norm_w, k_norm_w, conv_qkv_w, a_log, dt_bias, gdn_norm_w, w_pa, w_pd, w_out, w_up, ffn_conv_w, ffn_conv_b, w_down, final_norm_w, loss_target, m_c_ctx, m_w_mod, m_b_mod, m_w_in, m_q_norm_w, m_k_norm_w, m_conv_qkv_w, m_a_log, m_dt_bias, m_gdn_norm_w, m_w_pa, m_w_pd, m_w_out, m_w_up, m_ffn_conv_w, m_ffn_conv_b, m_w_down, m_final_norm_w, v_c_ctx, v_w_mod, v_b_mod, v_w_in, v_q_norm_w, v_k_norm_w, v_conv_qkv_w, v_a_log, v_dt_bias, v_gdn_norm_w, v_w_pa, v_w_pd, v_w_out, v_w_up, v_ffn_conv_w, v_ffn_conv_b, v_w_down, v_final_norm_w):
    names = ("c_ctx", "w_mod", "b_mod", "w_in", "q_norm_w", "k_norm_w", "conv_qkv_w", "a_log", "dt_bias", "gdn_norm_w",
             "w_pa", "w_pd", "w_out", "w_up", "ffn_conv_w", "ffn_conv_b", "w_down", "final_norm_w")
    w_sh = dict(c_ctx=c_ctx, w_mod=w_mod, b_mod=b_mod, w_in=w_in, q_norm_w=q_norm_w, k_norm_w=k_norm_w,
                conv_qkv_w=conv_qkv_w, a_log=a_log, dt_bias=dt_bias, gdn_norm_w=gdn_norm_w, w_pa=w_pa, w_pd=w_pd,
                w_out=w_out, w_up=w_up, ffn_conv_w=ffn_conv_w, ffn_conv_b=ffn_conv_b, w_down=w_down,
                final_norm_w=final_norm_w)
    m_sh = dict(c_ctx=m_c_ctx, w_mod=m_w_mod, b_mod=m_b_mod, w_in=m_w_in, q_norm_w=m_q_norm_w, k_norm_w=m_k_norm_w,
                conv_qkv_w=m_conv_qkv_w, a_log=m_a_log, dt_bias=m_dt_bias, gdn_norm_w=m_gdn_norm_w, w_pa=m_w_pa,
                w_pd=m_w_pd, w_out=m_w_out, w_up=m_w_up, ffn_conv_w=m_ffn_conv_w, ffn_conv_b=m_ffn_conv_b,
                w_down=m_w_down, final_norm_w=m_final_norm_w)
    v_sh = dict(c_ctx=v_c_ctx, w_mod=v_w_mod, b_mod=v_b_mod, w_in=v_w_in, q_norm_w=v_q_norm_w, k_norm_w=v_k_norm_w,
                conv_qkv_w=v_conv_qkv_w, a_log=v_a_log, dt_bias=v_dt_bias, gdn_norm_w=v_gdn_norm_w, w_pa=v_w_pa,
                w_pd=v_w_pd, w_out=v_w_out, w_up=v_w_up, ffn_conv_w=v_ffn_conv_w, ffn_conv_b=v_ffn_conv_b,
                w_down=v_w_down, final_norm_w=v_final_norm_w)
    chip = 2 * lax.axis_index("x") + lax.axis_index("y")

    conv_bits = jnp.concatenate([lax.bitcast_convert_type(w_sh[nm][0], BF16).reshape(-1)
                                 for nm in ("conv_qkv_w", "ffn_conv_w")])
    shards = {nm: w_sh[nm][0].astype(BF16).T if nm == "w_in" else w_sh[nm][0].astype(BF16) for nm in _BIG}
    shards["conv"] = jnp.pad(conv_bits, (0, _CONV_ROWS * D_MODEL - _CONV_ELEMS)).reshape(_CONV_ROWS, D_MODEL)
    gathered = all_gather_chips(shards)
    wb = {nm: _full_of_blocks(nm, gathered[nm]) for nm in _BIG}
    wb["w_in_main"], wb["w_in_small"] = _w_in_regroup(wb.pop("w_in"))
    conv_all = gathered["conv"].reshape(N_CHIPS, -1)[:, :_CONV_ELEMS]
    n_cq = 2 * 3 * CONV_W // N_CHIPS
    unbits = lambda a, w: lax.bitcast_convert_type(a.reshape(N_CHIPS, 3, w // N_CHIPS, 2), F32).transpose(1, 0, 2).reshape(3, w)
    ws = dict(c_ctx=c_ctx, b_mod=b_mod, q_norm_w=q_norm_w, k_norm_w=k_norm_w, a_log=a_log[0], dt_bias=dt_bias[0],
              gdn_norm_w=gdn_norm_w, ffn_conv_b=ffn_conv_b, final_norm_w=final_norm_w,
              conv_qkv_w=unbits(conv_all[:, :n_cq], CONV_W), ffn_conv_w=unbits(conv_all[:, n_cq:], 2 * D_FF))
    wz = {nm: jnp.zeros(a.shape, F32) for nm, a in wb.items()}

    loss_local, (gx, gz, gs) = jax.value_and_grad(local_loss, argnums=(0, 1, 3))(
        x[0], wz, wb, ws, c, ctx[0], loss_target[0])
    loss = lax.psum(loss_local, ("x", "y", "c"))

    gz["w_in"] = _w_in_ungroup(gz.pop("w_in_main"), gz.pop("w_in_small"))
    g32 = {nm: _blocks_of_full(nm, gz[nm]) for nm in _BIG}
    sel = jnp.stack([lax.axis_index("c"), chip]).astype(jnp.int32)
    theirs = sibling_halves(g32)
    got = scatter_halves({nm: _presum(nm, sel, g32[nm], theirs[nm]) for nm in _BIG})
    g_big = sibling_assemble({nm: _finalsum(nm, sel, g32[nm], theirs[nm], got[nm]) for nm in _BIG})

    gs["a_log"], gs["dt_bias"] = gs["a_log"][None], gs["dt_bias"][None]
    like = {nm: gs[nm] for nm in _SMALL}
    small_rows = -(-sum(int(np.prod(like[nm].shape)) for nm in _SMALL) // 1024) * 8
    g_small = _unpack_small(all_reduce_small(_pack_small(gs, small_rows)), like)
    for nm, width in (("conv_qkv_w", CONV_W), ("ffn_conv_w", 2 * D_FF)):
        g_small[nm] = lax.dynamic_slice_in_dim(g_small[nm], chip * (width // N_CHIPS), width // N_CHIPS, axis=1)[None]

    grads, deltas, new_m, new_v = {}, {}, {}, {}
    for nm in _BIG:
        g = g_big[nm].T if nm == "w_in" else g_big[nm]
        grads[nm] = g[None]
        deltas[nm], new_m[nm], new_v[nm] = (o[None] for o in _adamw(w_sh[nm][0], g, m_sh[nm][0], v_sh[nm][0],
                                                                     "adamw_" + nm))
    shard_like = {nm: w_sh[nm] for nm in _SMALL}
    rows_l = -(-sum(int(np.prod(shard_like[nm].shape)) for nm in _SMALL) // 1024) * 8
    g_l = _pack_small({nm: g_small[nm].reshape(w_sh[nm].shape) for nm in _SMALL}, rows_l)
    outs = _adamw(_pack_small(w_sh, rows_l), g_l, _pack_small(m_sh, rows_l), _pack_small(v_sh, rows_l), "adamw_small")
    grads.update(_unpack_small(g_l, shard_like))
    for tree, packed in zip((deltas, new_m, new_v), outs):
        tree.update(_unpack_small(packed, shard_like))

    return (loss, gx[None], *[grads[nm] for nm in names], *[deltas[nm] for nm in names],
            *[new_m[nm] for nm in names], *[new_v[nm] for nm in names])
```

```python
import functools
import math

import jax
import jax.numpy as jnp
import numpy as np
from jax import lax
from jax.experimental import pallas as pl
from jax.experimental.pallas import tpu as pltpu

F32 = jnp.float32
BF16 = jnp.bfloat16
HIGHEST = lax.Precision.HIGHEST
MESH = pl.DeviceIdType.MESH

D_MODEL = 1024
GRID_W = 64
ATTN_HEADS = 8
ATTN_KV_HEADS = 2
ATTN_GROUP = ATTN_HEADS // ATTN_KV_HEADS
HEAD_DIM = 128
ROPE_THETA = 10000.0
GDN_HEADS = 8
GDN_CHUNK = 64
D_FF = 2816
NORM_EPS = 1e-6
KV_W = ATTN_KV_HEADS * HEAD_DIM
Q_W = ATTN_HEADS * HEAD_DIM
GDN_W = GDN_HEADS * HEAD_DIM
CONV_W = 3 * GDN_W
MOD_W = 6 * D_MODEL
IN_COLS = 2 * KV_W + CONV_W + 4 * GDN_HEADS + Q_W + GDN_W + 2 * D_MODEL
IN_MAIN = IN_COLS - 4 * GDN_HEADS
SMALL_AT = 2 * KV_W + CONV_W
N_CHIPS = 4
N_DEV = 8

ADAM_LR = 0.001
ADAM_B1 = 0.9
ADAM_B2 = 0.999
ADAM_EPS = 1e-08
ADAM_WD = 0.01
ADAM_STEP = 10

VMEM_LIMIT = 48 * 1024 * 1024
MATMUL_VMEM_BUDGET = 40 * 1024 * 1024
MATMUL_STEP_BYTES = 1200 * 1024


def _pick(dim, prefs):
    for p in prefs:
        if p <= dim and dim % p == 0:
            return p
    return dim


_DIMS = {
    "nn": (((1,), (0,)), ((), ())),
    "nt": (((1,), (1,)), ((), ())),
    "tn": (((0,), (0,)), ((), ())),
}


def _matmul_plan(m, n, k, a_bytes, b_bytes):
    best = None
    for tm in (2304, 2048, 1152, 1024, 768, 512, 384, 256, 128, m):
        for tn in (2560, 1536, 1408, 1024, 768, 512, 256, 128, n):
            for tk in (3840, 2816, 2560, 2304, 2048, 1920, 1408, 1152, 1024, 768, 512, 256, 128, k):
                if tm > m or tn > n or tk > k or m % tm or n % tn or k % tk:
                    continue
                blocks = tm * tk * a_bytes + tk * tn * b_bytes + tm * tn * 4
                casts = (tm * tk * 2 if a_bytes > 2 else 0) + (tk * tn * 2 if b_bytes > 2 else 0) + tm * tn * 4
                if 2 * blocks + casts > MATMUL_VMEM_BUDGET:
                    continue
                nm, nn, nk = m // tm, n // tn, k // tk
                size_a, size_b = m * k * a_bytes, k * n * b_bytes
                for n_inner in (True, False):
                    if n_inner:
                        traffic = (size_a if nk == 1 else nn * size_a) + nm * size_b
                    else:
                        traffic = nn * size_a + (size_b if nk == 1 else nm * size_b)
                    cost = traffic + nm * nn * nk * MATMUL_STEP_BYTES + (nk - 1) * m * n * 4
                    if best is None or cost < best[0]:
                        best = (cost, tm, tn, tk, n_inner)
    return best[1:]


def _matmul(a, b, mode, name):
    if mode == "nn":
        (m, k), (_, n) = a.shape, b.shape
    elif mode == "nt":
        (m, k), (n, _) = a.shape, b.shape
    else:
        (k, m), (_, n) = a.shape, b.shape
    tm, tn, tk, n_inner = _matmul_plan(m, n, k, a.dtype.itemsize, b.dtype.itemsize)
    nk = k // tk
    ij = (lambda g0, g1: (g0, g1)) if n_inner else (lambda g0, g1: (g1, g0))
    if mode == "tn":
        a_spec = pl.BlockSpec((tk, tm), lambda g0, g1, l: (l, ij(g0, g1)[0]))
    else:
        a_spec = pl.BlockSpec((tm, tk), lambda g0, g1, l: (ij(g0, g1)[0], l))
    if mode == "nt":
        b_spec = pl.BlockSpec((tn, tk), lambda g0, g1, l: (ij(g0, g1)[1], l))
    else:
        b_spec = pl.BlockSpec((tk, tn), lambda g0, g1, l: (l, ij(g0, g1)[1]))
    dims = _DIMS[mode]

    def body(a_ref, b_ref, o_ref):
        part = lax.dot_general(a_ref[...].astype(BF16), b_ref[...].astype(BF16), dims, preferred_element_type=F32)
        if nk == 1:
            o_ref[...] = part
        else:
            l = pl.program_id(2)

            @pl.when(l == 0)
            def _():
                o_ref[...] = part

            @pl.when(l > 0)
            def _():
                o_ref[...] += part

    return pl.pallas_call(
        body,
        name=name,
        grid=(m // tm, n // tn, nk) if n_inner else (n // tn, m // tm, nk),
        in_specs=[a_spec, b_spec],
        out_specs=pl.BlockSpec((tm, tn), lambda g0, g1, l: ij(g0, g1)),
        out_shape=jax.ShapeDtypeStruct((m, n), F32),
        compiler_params=pltpu.CompilerParams(dimension_semantics=("parallel", "parallel", "arbitrary"),
                                             vmem_limit_bytes=VMEM_LIMIT),
    )(a, b)


@functools.partial(jax.custom_vjp, nondiff_argnums=(3,))
def pmm(a, w, wz, name):
    del wz
    return _matmul(a, w, "nn", name + "_f")


def _pmm_fwd(a, w, wz, name):
    del wz
    return _matmul(a, w, "nn", name + "_f"), (a, w)


def _pmm_bwd(name, res, g):
    a, w = res
    da = _matmul(g, w, "nt", name + "_da")
    if a.shape[0] < 128:
        pad = 128 - a.shape[0]
        at = jnp.pad(a.T, ((0, 0), (0, pad)))
        gp = jnp.pad(g, ((0, pad), (0, 0)))
        dw = _matmul(at, gp, "nn", name + "_dw")
    else:
        dw = _matmul(a, g, "tn", name + "_dw")
    return da, jnp.zeros_like(w), dw


pmm.defvjp(_pmm_fwd, _pmm_bwd)


@functools.partial(jax.custom_vjp, nondiff_argnums=(3,))
def pmm_t(a, wt, wtz, name):
    del wtz
    return _matmul(a, wt, "nt", name + "_f")


def _pmm_t_fwd(a, wt, wtz, name):
    del wtz
    return _matmul(a, wt, "nt", name + "_f"), (a, wt)


def _pmm_t_bwd(name, res, g):
    a, wt = res
    return _matmul(g, wt, "nn", name + "_da"), jnp.zeros_like(wt), _matmul(g, a, "tn", name + "_dw")


pmm_t.defvjp(_pmm_t_fwd, _pmm_t_bwd)


def rowop(fn, name, rows, bcs=(), crows=(), cbcs=(), tr=256):
    rows, bcs, crows, cbcs = tuple(rows), tuple(bcs), tuple(crows), tuple(cbcs)
    n_rows = rows[0].shape[0]
    tr = _pick(n_rows, (tr, 128, 64, 32, 16, 8))
    nr, nb, ncr, ncb = len(rows), len(bcs), len(crows), len(cbcs)
    n_in = nr + nb + ncr + ncb
    grid = (n_rows // tr,)

    def blk(arr):
        return jax.ShapeDtypeStruct((tr, arr.shape[1]), arr.dtype)

    def row_spec(arr):
        return pl.BlockSpec((tr, arr.shape[1]), lambda i: (i, 0))

    def bc_spec(arr):
        return pl.BlockSpec(arr.shape, lambda i: (0, 0))

    out_blk = jax.eval_shape(fn, *[blk(r) for r in rows], *bcs, *[blk(r) for r in crows], *cbcs)
    n_out = len(out_blk)
    out_shape = tuple(jax.ShapeDtypeStruct((n_rows, o.shape[1]), o.dtype) for o in out_blk)
    in_specs = ([row_spec(r) for r in rows] + [bc_spec(b) for b in bcs]
                + [row_spec(r) for r in crows] + [bc_spec(b) for b in cbcs])

    def order(vals):
        return vals

    def fwd_call(args):
        def body(*refs):
            vals = [r[...] for r in refs[:n_in]]
            res = fn(*order(vals))
            for o_ref, r in zip(refs[n_in:], res):
                o_ref[...] = r

        return pl.pallas_call(
            body, name=name + "_f", grid=grid, in_specs=in_specs,
            out_specs=[row_spec(o) for o in out_shape], out_shape=out_shape,
            compiler_params=pltpu.CompilerParams(dimension_semantics=("parallel",), vmem_limit_bytes=VMEM_LIMIT),
        )(*args)

    def bwd_call(args, cts):
        def body(*refs):
            vals = [r[...] for r in refs[:n_in]]
            ct_refs = refs[n_in:n_in + n_out]
            d_rows = refs[n_in + n_out:n_in + n_out + nr]
            d_bcs = refs[n_in + n_out + nr:]
            consts = vals[nr + nb:]
            _, vjp = jax.vjp(lambda *p: fn(*p, *consts), *vals[:nr + nb])
            grads = vjp(tuple(c[...] for c in ct_refs))
            for ref, g in zip(d_rows, grads[:nr]):
                ref[...] = g

            @pl.when(pl.program_id(0) == 0)
            def _():
                for ref in d_bcs:
                    ref[...] = jnp.zeros_like(ref)

            for ref, g in zip(d_bcs, grads[nr:]):
                ref[...] += g

        d_shape = tuple(jax.ShapeDtypeStruct(r.shape, r.dtype) for r in rows + bcs)
        return pl.pallas_call(
            body, name=name + "_b", grid=grid,
            in_specs=in_specs + [row_spec(o) for o in out_shape],
            out_specs=[row_spec(r) for r in rows] + [bc_spec(b) for b in bcs], out_shape=d_shape,
            compiler_params=pltpu.CompilerParams(dimension_semantics=("arbitrary",), vmem_limit_bytes=VMEM_LIMIT),
        )(*args, *cts)

    @jax.custom_vjp
    def op(diff, const):
        return fwd_call(diff + const)

    def op_fwd(diff, const):
        return fwd_call(diff + const), (diff, const)

    def op_bwd(res, cts):
        diff, const = res
        grads = bwd_call(diff + const, tuple(cts))
        return tuple(grads), tuple(jnp.zeros_like(c) for c in const)

    op.defvjp(op_fwd, op_bwd)
    return op(rows + bcs, crows + cbcs)


def colop(fn, name, arrays, uses, n_const, nblk, cw=128):
    arrays = tuple(arrays)
    n_diff = len(arrays) - n_const
    nd = sum(1 for u in uses if u[0] < n_diff)
    assert all(u[0] < n_diff for u in uses[:nd]) and all(u[0] >= n_diff for u in uses[nd:])

    def spec(u):
        return pl.BlockSpec((arrays[u[0]].shape[0], cw), lambda j, off=u[1]: (0, off + j))

    def out_spec(rows):
        return pl.BlockSpec((rows, cw), lambda j: (0, j))

    out_blk = jax.eval_shape(fn, *[jax.ShapeDtypeStruct((arrays[u[0]].shape[0], cw), arrays[u[0]].dtype)
                                   for u in uses])
    out_shape = tuple(jax.ShapeDtypeStruct((o.shape[0], nblk * cw), o.dtype) for o in out_blk)
    params = pltpu.CompilerParams(dimension_semantics=("parallel",), vmem_limit_bytes=VMEM_LIMIT)

    def fwd_call(arrs):
        def body(*refs):
            res = fn(*[r[...] for r in refs[:len(uses)]])
            for o_ref, r in zip(refs[len(uses):], res):
                o_ref[...] = r

        return pl.pallas_call(
            body, name=name + "_f", grid=(nblk,), in_specs=[spec(u) for u in uses],
            out_specs=[out_spec(o.shape[0]) for o in out_shape], out_shape=out_shape, compiler_params=params,
        )(*[arrs[u[0]] for u in uses])

    def bwd_call(arrs, cts):
        def body(*refs):
            vals = [r[...] for r in refs[:len(uses)]]
            ct_refs = refs[len(uses):len(uses) + len(out_shape)]
            _, vjp = jax.vjp(lambda *p: fn(*p, *vals[nd:]), *vals[:nd])
            for ref, g in zip(refs[len(uses) + len(out_shape):], vjp(tuple(c[...] for c in ct_refs))):
                ref[...] = g

        d_shape = tuple(jax.ShapeDtypeStruct((arrays[u[0]].shape[0], nblk * cw), F32) for u in uses[:nd])
        return pl.pallas_call(
            body, name=name + "_b", grid=(nblk,),
            in_specs=[spec(u) for u in uses] + [out_spec(o.shape[0]) for o in out_shape],
            out_specs=[out_spec(s.shape[0]) for s in d_shape], out_shape=d_shape, compiler_params=params,
        )(*[arrs[u[0]] for u in uses], *cts)

    @jax.custom_vjp
    def op(diff, const):
        return fwd_call(diff + const)

    def op_fwd(diff, const):
        return fwd_call(diff + const), (diff, const)

    def op_bwd(res, cts):
        diff, const = res
        d_uses = bwd_call(diff + const, tuple(cts))
        grads = []
        for i in range(n_diff):
            parts = sorted([(u[1], k) for k, u in enumerate(uses[:nd]) if u[0] == i])
            grads.append(d_uses[parts[0][1]] if len(parts) == 1
                         else jnp.concatenate([d_uses[k] for _, k in parts], axis=1))
        return tuple(grads), tuple(jnp.zeros_like(c) for c in const)

    op.defvjp(op_fwd, op_bwd)
    return op(arrays[:n_diff], arrays[n_diff:])


@functools.partial(jax.custom_vjp, nondiff_argnums=(1,))
def _roll_rows(x, k):
    return pltpu.roll(x, k % x.shape[0], 0)


def _roll_rows_fwd(x, k):
    return _roll_rows(x, k), None


def _roll_rows_bwd(k, _, g):
    return (_roll_rows(g, -k),)


_roll_rows.defvjp(_roll_rows_fwd, _roll_rows_bwd)


def _conv3(x, w0, w1, w2, starts):
    rows = lax.broadcasted_iota(jnp.int32, x.shape, 0)
    ends = tuple(s - 1 for s in starts[1:]) + (x.shape[0] - 1,)
    first = functools.reduce(jnp.logical_or, [rows == s for s in starts])
    last = functools.reduce(jnp.logical_or, [rows == e for e in ends])
    prev = jnp.where(first, 0.0, _roll_rows(x, 1))
    nxt = jnp.where(last, 0.0, _roll_rows(x, -1))
    return prev * w0 + x * w1 + nxt * w2


def _rms(x):
    return x * lax.rsqrt(jnp.mean(x * x, axis=-1, keepdims=True) + NORM_EPS)


def _heads(x, n):
    return [x[:, h * HEAD_DIM:(h + 1) * HEAD_DIM] for h in range(n)]


_NT = (((1,), (1,)), ((), ()))
_TN = (((0,), (0,)), ((), ()))
_TQ = 256


def _attn_probs(q, k):
    s = lax.dot_general(q, k, _NT, preferred_element_type=F32) * (HEAD_DIM ** -0.5)
    p = jnp.exp(s - jnp.max(s, axis=-1, keepdims=True))
    return p * (1.0 / jnp.sum(p, axis=-1, keepdims=True))


def _attn_fwd_call(q, k, v):
    n, t = q.shape[0], k.shape[0]
    tq = _pick(n, (_TQ, 128))

    def body(q_ref, k_ref, v_ref, o_ref):
        p = _attn_probs(q_ref[...].astype(BF16), k_ref[...].astype(BF16))
        o_ref[...] = jnp.dot(p.astype(BF16), v_ref[...].astype(BF16), preferred_element_type=F32)

    return pl.pallas_call(
        body, name="attn_f", grid=(ATTN_HEADS, n // tq),
        in_specs=[pl.BlockSpec((tq, HEAD_DIM), lambda h, i: (i, h)),
                  pl.BlockSpec((t, HEAD_DIM), lambda h, i: (0, h // ATTN_GROUP)),
                  pl.BlockSpec((t, HEAD_DIM), lambda h, i: (0, h // ATTN_GROUP))],
        out_specs=pl.BlockSpec((tq, HEAD_DIM), lambda h, i: (i, h)),
        out_shape=jax.ShapeDtypeStruct(q.shape, F32),
        compiler_params=pltpu.CompilerParams(dimension_semantics=("parallel", "parallel"),
                                             vmem_limit_bytes=VMEM_LIMIT),
    )(q, k, v)


def _attn_bwd_call(q, k, v, do):
    n, t = q.shape[0], k.shape[0]
    tq = _pick(n, (_TQ, 128))

    def body(q_ref, k_ref, v_ref, do_ref, dq_ref, dk_ref, dv_ref):
        @pl.when((pl.program_id(1) == 0) & (pl.program_id(2) == 0))
        def _():
            dk_ref[...] = jnp.zeros_like(dk_ref)
            dv_ref[...] = jnp.zeros_like(dv_ref)

        qb, kb, vb, dob = (r[...].astype(BF16) for r in (q_ref, k_ref, v_ref, do_ref))
        p = _attn_probs(qb, kb)
        dp = lax.dot_general(dob, vb, _NT, preferred_element_type=F32)
        ds = p * (dp - jnp.sum(p * dp, axis=-1, keepdims=True)) * (HEAD_DIM ** -0.5)
        dsb = ds.astype(BF16)
        dq_ref[...] = jnp.dot(dsb, kb, preferred_element_type=F32)
        dk_ref[...] += lax.dot_general(dsb, qb, _TN, preferred_element_type=F32)
        dv_ref[...] += lax.dot_general(p.astype(BF16), dob, _TN, preferred_element_type=F32)

    q_spec = pl.BlockSpec((tq, HEAD_DIM), lambda kh, g, i: (i, kh * ATTN_GROUP + g))
    kv_spec = pl.BlockSpec((t, HEAD_DIM), lambda kh, g, i: (0, kh))
    return pl.pallas_call(
        body, name="attn_b", grid=(ATTN_KV_HEADS, ATTN_GROUP, n // tq),
        in_specs=[q_spec, kv_spec, kv_spec, q_spec],
        out_specs=[q_spec, kv_spec, kv_spec],
        out_shape=(jax.ShapeDtypeStruct(q.shape, F32), jax.ShapeDtypeStruct(k.shape, F32),
                   jax.ShapeDtypeStruct(v.shape, F32)),
        compiler_params=pltpu.CompilerParams(dimension_semantics=("parallel", "arbitrary", "arbitrary"),
                                             vmem_limit_bytes=VMEM_LIMIT),
    )(q, k, v, do)


@jax.custom_vjp
def attention(q, k, v):
    return _attn_fwd_call(q, k, v)


def _attention_fwd(q, k, v):
    return _attn_fwd_call(q, k, v), (q, k, v)


def _attention_bwd(res, do):
    return _attn_bwd_call(*res, do)


attention.defvjp(_attention_fwd, _attention_bwd)


_C = GDN_CHUNK


def _pdot(a, b):
    return jnp.dot(a, b, precision=lax.Precision.HIGH, preferred_element_type=F32)


@jax.custom_vjp
def _hdot(a, b):
    return jnp.dot(a.astype(BF16), b.astype(BF16), preferred_element_type=F32)


def _hdot_fwd(a, b):
    return _hdot(a, b), (a, b)


def _hdot_bwd(res, g):
    a, b = res
    gb = g.astype(BF16)
    return (lax.dot_general(gb, b.astype(BF16), _NT, preferred_element_type=F32),
            lax.dot_general(a.astype(BF16), gb, _TN, preferred_element_type=F32))


_hdot.defvjp(_hdot_fwd, _hdot_bwd)


def _each(fn, *lists):
    return [fn(*args) for args in zip(*lists)]


def _unit_lower_inverse(low, blockdiag):
    eye = (lax.broadcasted_iota(jnp.int32, (_C, _C), 0) == lax.broadcasted_iota(jnp.int32, (_C, _C), 1)).astype(F32)
    ld = _each(lambda a: a * blockdiag, low)
    lo = _each(lambda a, d: a - d, low, ld)
    l2 = _each(_hdot, ld, ld)
    l4 = _each(_hdot, l2, l2)
    l8 = _each(_hdot, l4, l4)
    td = _each(lambda d, a2: _hdot(eye - d, eye + a2), ld, l2)
    td = _each(lambda t, a4: _hdot(t, eye + a4), td, l4)
    td = _each(lambda t, a8: _hdot(t, eye + a8), td, l8)
    nn = _each(_hdot, td, lo)
    n2 = _each(_hdot, nn, nn)
    out = _each(lambda n, m2: _hdot(eye - n, eye + m2), nn, n2)
    return _each(_hdot, out, td)


def _gdn_chunks(heads, blockdiag):
    q, k, v, b_b, be_b, e_b, kd_b, m1, dec, gl, s = (list(col) for col in zip(*heads))
    f32dot = lambda a, b: jnp.dot(a, b, preferred_element_type=F32)
    nt = lambda a, b: lax.dot_general(a, b, _NT, preferred_element_type=F32)
    kk = _each(nt, k, k)
    t_inv = _unit_lower_inverse(_each(lambda m, a: m * a, m1, kk), blockdiag)
    u = _each(lambda t, b, x: _hdot(t, b * x), t_inv, b_b, v)
    w = _each(lambda t, b, x: _hdot(t, b * x), t_inv, be_b, k)
    delta = _each(lambda uu, ww, ss: uu - f32dot(ww, ss), u, w, s)
    p = _each(lambda d, qq, kx: d * nt(qq, kx), dec, q, k)
    o = _each(lambda qq, e, ss, pp, dd: f32dot(qq * e, ss) + f32dot(pp, dd), q, e_b, s, p, delta)
    s_new = _each(lambda g, ss, kx, kd, dd: g * ss + lax.dot_general(kx * kd, dd, _TN, preferred_element_type=F32),
                  gl, s, k, kd_b, delta)
    return o, s_new


def _blockdiag_mask():
    r = lax.broadcasted_iota(jnp.int32, (_C, _C), 0) >> 4
    c = lax.broadcasted_iota(jnp.int32, (_C, _C), 1) >> 4
    return (r == c).astype(F32)


def _gdn_specs(nc, ncc, reverse, backward):
    def ch(s):
        s = nc - 1 - s if backward else s
        return jnp.where(s < ncc, ncc - 1 - s, nc + ncc - 1 - s) if reverse else s

    tok = pl.BlockSpec((_C, 3 * GDN_W), lambda s: (ch(s), 0))
    park = nc - ncc - 1 if reverse else 0
    out = pl.BlockSpec((_C, GDN_W), lambda s: (jnp.where(ch(s) >= ncc, ch(s) - ncc, park), 0))
    per_tok = pl.BlockSpec((GDN_HEADS, _C, HEAD_DIM), lambda s: (0, ch(s), 0))
    mat = pl.BlockSpec((GDN_HEADS, None, _C, _C), lambda s: (0, ch(s), 0, 0))
    row = pl.BlockSpec((GDN_HEADS, None, 1, HEAD_DIM), lambda s: (0, ch(s), 0, 0))
    state = pl.BlockSpec((GDN_HEADS, None, HEAD_DIM, HEAD_DIM), lambda s: (0, ch(s), 0, 0))
    return tok, out, per_tok, mat, row, state, ch


def _head_cols(h, part):
    return slice((part * GDN_HEADS + h) * HEAD_DIM, (part * GDN_HEADS + h + 1) * HEAD_DIM)


def _gdn_heads(qkv_ref, factor_refs, state_ref, first):
    return [[qkv_ref[:, _head_cols(h, 0)], qkv_ref[:, _head_cols(h, 1)], qkv_ref[:, _head_cols(h, 2)]]
            + [r[h] for r in factor_refs] + [state_ref[first + h]] for h in range(GDN_HEADS)]


def _gdn_fwd_call(ncc, qkv, factors):
    t = qkv.shape[0]
    nc, nd = t // _C, len(factors)
    specs = [_gdn_specs(nc, ncc, d == 1, False) for d in range(nd)]

    def body(*refs):
        ins_d = [refs[8 * d:8 * d + 8] for d in range(nd)]
        outs_d = [refs[8 * nd + 2 * d:8 * nd + 2 * d + 2] for d in range(nd)]
        s_ref = refs[-1]

        @pl.when(pl.program_id(0) == 0)
        def _():
            s_ref[...] = jnp.zeros_like(s_ref)

        heads = sum([_gdn_heads(r[0], r[1:], s_ref, GDN_HEADS * d) for d, r in enumerate(ins_d)], [])
        o, s_new = _gdn_chunks(heads, _blockdiag_mask())
        for d, (o_ref, sall_ref) in enumerate(outs_d):
            for h in range(GDN_HEADS):
                i = GDN_HEADS * d + h
                sall_ref[h] = heads[i][10]
                o_ref[:, _head_cols(h, 0)] = o[i]
                s_ref[i] = s_new[i]

    in_specs, out_specs, out_shape, operands = [], [], [], []
    for (tok, out, per_tok, mat, row, state, _), f in zip(specs, factors):
        in_specs += [tok, per_tok, per_tok, per_tok, per_tok, mat, mat, row]
        out_specs += [out, state]
        out_shape += [jax.ShapeDtypeStruct((t - ncc * _C, GDN_W), F32),
                      jax.ShapeDtypeStruct((GDN_HEADS, nc, HEAD_DIM, HEAD_DIM), F32)]
        operands += [qkv, *f]
    res = pl.pallas_call(
        body, name="gdn_f", grid=(nc,), in_specs=in_specs, out_specs=out_specs, out_shape=out_shape,
        scratch_shapes=[pltpu.VMEM((nd * GDN_HEADS, HEAD_DIM, HEAD_DIM), F32)],
        compiler_params=pltpu.CompilerParams(dimension_semantics=("arbitrary",), vmem_limit_bytes=VMEM_LIMIT),
    )(*operands)
    return [(res[2 * d], res[2 * d + 1]) for d in range(nd)]


def _gdn_bwd_call(ncc, qkv, factors, salls, dos):
    t = qkv.shape[0]
    nc, nd = t // _C, len(factors)
    specs = [_gdn_specs(nc, ncc, d == 1, True) for d in range(nd)]

    def body(*refs):
        ins_d = [refs[10 * d:10 * d + 10] for d in range(nd)]
        outs_d = [refs[10 * nd + 8 * d:10 * nd + 8 * d + 8] for d in range(nd)]
        ds_ref = refs[-1]

        @pl.when(pl.program_id(0) == 0)
        def _():
            ds_ref[...] = jnp.zeros_like(ds_ref)

        bd = _blockdiag_mask()
        heads = sum([_gdn_heads(r[0], r[1:8], r[8], 0) for r in ins_d], [])
        _, vjp = jax.vjp(lambda hs: _gdn_chunks(hs, bd), heads)
        live = [spec[6](pl.program_id(0)) >= ncc for spec in specs]
        (all_grads,) = vjp(([jnp.where(live[d], r[9][:, _head_cols(h, 0)], 0.0)
                             for d, r in enumerate(ins_d) for h in range(GDN_HEADS)],
                            [ds_ref[i] for i in range(nd * GDN_HEADS)]))
        for d, out_refs in enumerate(outs_d):
            for h in range(GDN_HEADS):
                grads = all_grads[GDN_HEADS * d + h]
                for part in range(3):
                    out_refs[0][:, _head_cols(h, part)] = grads[part]
                for ref, g in zip(out_refs[1:], grads[3:10]):
                    ref[h] = g
                ds_ref[GDN_HEADS * d + h] = grads[10]

    shp = lambda a: jax.ShapeDtypeStruct(a.shape, F32)
    in_specs, out_specs, out_shape, operands = [], [], [], []
    for (tok, out, per_tok, mat, row, state, _), f, sall, do in zip(specs, factors, salls, dos):
        in_specs += [tok, per_tok, per_tok, per_tok, per_tok, mat, mat, row, state, out]
        out_specs += [tok, per_tok, per_tok, per_tok, per_tok, mat, mat, row]
        out_shape += [shp(qkv)] + [shp(a) for a in f]
        operands += [qkv, *f, sall, do]
    res = pl.pallas_call(
        body, name="gdn_b", grid=(nc,), in_specs=in_specs, out_specs=out_specs, out_shape=out_shape,
        scratch_shapes=[pltpu.VMEM((nd * GDN_HEADS, HEAD_DIM, HEAD_DIM), F32)],
        compiler_params=pltpu.CompilerParams(dimension_semantics=("arbitrary",), vmem_limit_bytes=VMEM_LIMIT),
    )(*operands)
    return [res[8 * d:8 * d + 8] for d in range(nd)]


@functools.partial(jax.custom_vjp, nondiff_argnums=(0,))
def gdn_scan(ncc, qkv, f_fwd, f_rev):
    (o0, _), (o1, _) = _gdn_fwd_call(ncc, qkv, [f_fwd, f_rev])
    return o0, o1


def _gdn_scan_fwd(ncc, qkv, f_fwd, f_rev):
    (o0, s0), (o1, s1) = _gdn_fwd_call(ncc, qkv, [f_fwd, f_rev])
    return (o0, o1), (qkv, f_fwd, f_rev, s0, s1)


def _gdn_scan_bwd(ncc, res, dos):
    qkv, f_fwd, f_rev, s0, s1 = res
    g0, g1 = _gdn_bwd_call(ncc, qkv, [f_fwd, f_rev], [s0, s1], list(dos))
    return g0[0] + g1[0], tuple(g0[1:]), tuple(g1[1:])


gdn_scan.defvjp(_gdn_scan_fwd, _gdn_scan_bwd)


def _rope_tables(n, cl):
    t = np.arange(n)
    inv_freq = (ROPE_THETA ** (-np.arange(0, HEAD_DIM // 2, 2, dtype=np.float32) / (HEAD_DIM // 2))).astype(np.float32)
    ang_r = (t // GRID_W).astype(np.float32)[:, None] * inv_freq
    ang_c = (t % GRID_W).astype(np.float32)[:, None] * inv_freq
    cos = np.concatenate([np.cos(ang_r), np.cos(ang_r), np.cos(ang_c), np.cos(ang_c)], axis=1)
    sin = np.concatenate([-np.sin(ang_r), np.sin(ang_r), -np.sin(ang_c), np.sin(ang_c)], axis=1)
    cos_all = np.concatenate([np.ones((cl, HEAD_DIM), np.float32), cos], axis=0)
    sin_all = np.concatenate([np.zeros((cl, HEAD_DIM), np.float32), sin], axis=0)
    j = np.arange(HEAD_DIM)
    src = np.where((j % 64) < 32, j + 32, j - 32)
    perm = np.zeros((HEAD_DIM, HEAD_DIM), np.float32)
    perm[src, j] = 1.0
    return (jnp.asarray(cos.astype(np.float32)), jnp.asarray(sin.astype(np.float32)),
            jnp.asarray(cos_all), jnp.asarray(sin_all), jnp.asarray(perm))


def _gdn_factors(log_a, beta, reverse):
    t = log_a.shape[0]
    nc = t // _C
    la = log_a.reshape(nc, _C, GDN_HEADS).transpose(2, 0, 1)
    be = beta.reshape(nc, _C, GDN_HEADS).transpose(2, 0, 1)
    gam = lax.cumsum(la, axis=2, reverse=reverse)
    idx = jnp.arange(_C)
    incl = (idx[:, None] <= idx[None, :]) if reverse else (idx[:, None] >= idx[None, :])
    strict = (idx[:, None] < idx[None, :]) if reverse else (idx[:, None] > idx[None, :])
    dec = jnp.exp(jnp.where(incl, gam[..., :, None] - gam[..., None, :], -jnp.inf))
    m1 = jnp.where(strict, be[..., :, None] * dec, 0.0)
    e = jnp.exp(gam)
    g_last = gam[..., :1] if reverse else gam[..., -1:]
    lanes = lambda a: jnp.broadcast_to(a.reshape(GDN_HEADS, t, 1), (GDN_HEADS, t, HEAD_DIM))
    gl = jnp.broadcast_to(jnp.exp(g_last)[..., None], (GDN_HEADS, nc, 1, HEAD_DIM))
    return lanes(be), lanes(be * e), lanes(e), lanes(jnp.exp(g_last - gam)), m1, dec, gl


def local_loss(x, wz, wb, ws, c, ctx, target):
    return channel_mixing(token_mixing(x, wz, wb, ws, c, ctx), wz, wb, ws, target)


def token_mixing(x, wz, wb, ws, c, ctx):
    n, cl = x.shape[0], ctx.shape[0]
    cos_q, sin_q, cos_k, sin_k, perm = _rope_tables(n, cl)

    sc_in = jnp.concatenate([jax.nn.silu(c), jax.nn.silu(ws["c_ctx"])[None, :], jnp.zeros((14, D_MODEL), F32)], axis=0)
    mod = pmm(sc_in, wb["w_mod"], wz["w_mod"], "mm_mod") + ws["b_mod"]
    sh1, sc1, g1, sh2, sc2, g2 = [mod[0:1, i * D_MODEL:(i + 1) * D_MODEL] for i in range(6)]
    csh1, csc1 = mod[1:2, 0:D_MODEL], mod[1:2, D_MODEL:2 * D_MODEL]

    def norm_mod(a, sh, sc):
        return (_rms(a) * (1.0 + sc) + sh,)

    (hx,) = rowop(norm_mod, "normmod_x", (x,), (sh1, sc1))
    (hc,) = rowop(norm_mod, "normmod_c", (ctx,), (csh1, csc1))
    h_all = jnp.concatenate([hc, hx], axis=0)
    p_main = pmm_t(h_all, wb["w_in_main"], wz["w_in_main"], "mm_in")
    p_small = pmm_t(h_all, wb["w_in_small"], wz["w_in_small"], "mm_ins")
    ak, av, qkv, aq, z, gate = jnp.split(p_main, [KV_W, 2 * KV_W, SMALL_AT, SMALL_AT + Q_W, SMALL_AT + Q_W + GDN_W],
                                         axis=1)
    db, da = p_small[:, :2 * GDN_HEADS], p_small[:, 2 * GDN_HEADS:4 * GDN_HEADS]

    def qk_prep(nh):
        def fn(a, w, cos, sin, pm):
            outs = []
            for ah in _heads(a, nh):
                y = _rms(ah) * w
                outs.append(y * cos + _pdot(y, pm) * sin)
            return (jnp.concatenate(outs, axis=1),)
        return fn

    (q_x,) = rowop(qk_prep(ATTN_HEADS), "q_prep", (aq[cl:],), (ws["q_norm_w"],), (cos_q, sin_q), (perm,))
    (k_all,) = rowop(qk_prep(ATTN_KV_HEADS), "k_prep", (ak,), (ws["k_norm_w"],), (cos_k, sin_k), (perm,))
    attn_x = attention(q_x, k_all, av)

    cw = ws["conv_qkv_w"]
    normed = jnp.asarray(np.repeat([1.0, 1.0, 0.0], GDN_W)[None, :], F32)
    scale = jnp.asarray(np.repeat([HEAD_DIM ** -0.5, 1.0, 1.0], GDN_W)[None, :], F32)

    def gdn_prep(a, w0, w1, w2, nf, sc):
        s = jax.nn.silu(_conv3(a, w0, w1, w2, (0, cl)))
        inv = lax.rsqrt(jnp.sum(s * s, axis=-1, keepdims=True) + NORM_EPS)
        return (s * jnp.where(nf > 0.0, inv * sc, 1.0),)

    (qkvn,) = colop(gdn_prep, "gdn_prep", (qkv, cw[0:1], cw[1:2], cw[2:3], normed, scale),
                    [(i, 0) for i in range(6)], 2, 3 * GDN_HEADS)
    beta = jax.nn.sigmoid(db).reshape(-1, 2, GDN_HEADS)
    log_a = -jnp.exp(ws["a_log"])[None] * jax.nn.softplus(da.reshape(-1, 2, GDN_HEADS) + ws["dt_bias"][None])
    o_fwd, o_rev = gdn_scan(cl // _C, qkvn, _gdn_factors(log_a[:, 0], beta[:, 0], False),
                            _gdn_factors(log_a[:, 1], beta[:, 1], True))
    o_x = o_fwd + o_rev

    def gdn_out(o, zz, w):
        outs = [_rms(oh) * w * jax.nn.silu(zh) for oh, zh in zip(_heads(o, GDN_HEADS), _heads(zz, GDN_HEADS))]
        return (jnp.concatenate(outs, axis=1),)

    (gdn_x,) = rowop(gdn_out, "gdn_out", (o_x, z[cl:]), (ws["gdn_norm_w"],))
    return dict(x=x, attn_x=attn_x, gdn_x=gdn_x, gate=gate[cl:], g1=g1, sh2=sh2, sc2=sc2, g2=g2)


def channel_mixing(mixed, wz, wb, ws, target):
    x, attn_x, gdn_x, gate = mixed["x"], mixed["attn_x"], mixed["gdn_x"], mixed["gate"]
    g1, sh2, sc2, g2 = mixed["g1"], mixed["sh2"], mixed["sc2"], mixed["g2"]
    pa = pmm(attn_x, wb["w_pa"], wz["w_pa"], "mm_pa")
    pd = pmm(gdn_x, wb["w_pd"], wz["w_pd"], "mm_pd")

    def merge(a, d, g):
        return (jax.nn.sigmoid(g[:, :D_MODEL]) * a + jax.nn.sigmoid(g[:, D_MODEL:]) * d,)

    (y,) = rowop(merge, "merge", (pa, pd, gate))
    mo = pmm(y, wb["w_out"], wz["w_out"], "mm_out")

    def res_norm_mod(xx, m, g, sh, sc):
        x1 = xx + g * m
        return x1, _rms(x1) * (1.0 + sc) + sh

    x1, h2 = rowop(res_norm_mod, "res1", (x, mo), (g1, sh2, sc2))
    up = pmm(h2, wb["w_up"], wz["w_up"], "mm_up")
    fw = ws["ffn_conv_w"]

    def ffn_act(ug, uv, w0g, w0v, w1g, w1v, w2g, w2v, bg, bv):
        g = _conv3(ug, w0g, w1g, w2g, (0,)) + bg
        v = _conv3(uv, w0v, w1v, w2v, (0,)) + bv
        return (jax.nn.silu(g) * v,)

    half = D_FF // HEAD_DIM
    (act,) = colop(ffn_act, "ffn_act", (up, fw[0:1], fw[1:2], fw[2:3], ws["ffn_conv_b"]),
                   [(i, off) for i in range(5) for off in (0, half)], 0, half)
    dn = pmm(act, wb["w_down"], wz["w_down"], "mm_down")

    def head(xx, m, g, w, tgt):
        yy = _rms(xx + g * m) * w
        err = (yy - tgt) ** 2
        return (jnp.broadcast_to(0.5 * jnp.mean(err, axis=-1, keepdims=True), (xx.shape[0], HEAD_DIM)),)

    (row_loss,) = rowop(head, "head", (x1, dn), (g2, ws["final_norm_w"][None, :]), (target,))
    return jnp.sum(row_loss[:, 0])


_HBM = pl.BlockSpec(memory_space=pltpu.HBM)


def _chip_peers():
    x, y = lax.axis_index("x"), lax.axis_index("y")
    return [(1 - x, y), (x, 1 - y), (1 - x, 1 - y)]


_SPLIT_COLS = ("w_in",)


def _half_of(view, nm, idx, lead=0):
    r, cdim = view.shape[-2:]
    pre = (slice(None),) * lead
    if nm in _SPLIT_COLS:
        return view.at[pre + (slice(None), pl.ds(pl.multiple_of(idx * (cdim // 2), 128), cdim // 2))]
    return view.at[pre + (pl.ds(pl.multiple_of(idx * (r // 2), 16), r // 2), slice(None))]


def _remote(src, dst, send_sem, recv_sem, dev):
    return pltpu.make_async_remote_copy(src_ref=src, dst_ref=dst, send_sem=send_sem, recv_sem=recv_sem,
                                        device_id=dev, device_id_type=MESH)


def _hbm_call(body, name, ins, out_shape, n_sems, in_place=False):
    names = tuple(ins)
    return dict(zip(names, pl.pallas_call(
        body, name=name, in_specs=[_HBM] * len(names), out_specs=[_HBM] * len(names),
        out_shape=[out_shape(nm, ins[nm]) for nm in names],
        scratch_shapes=[pltpu.SemaphoreType.DMA((k,)) for k in n_sems],
        input_output_aliases={i: i for i in range(len(names))} if in_place else {},
    )(*[ins[nm] for nm in names])))


def all_gather_chips(shards):
    names = tuple(shards)
    n = len(names)

    def body(*refs):
        ins, outs = dict(zip(names, refs[:n])), dict(zip(names, refs[n:2 * n]))
        ici_send, ici_recv, d2d_send, d2d_recv, own_send, own_recv = refs[2 * n:]
        x, y, c = lax.axis_index("x"), lax.axis_index("y"), lax.axis_index("c")
        me, sib = 2 * x + y, (x, y, 1 - c)
        own = [_remote(ins[nm], outs[nm].at[me], own_send.at[i], own_recv.at[i], sib) for i, nm in enumerate(names)]
        for cp in own:
            cp.start()
        sends = []
        for k, (px, py) in enumerate(_chip_peers()):
            for i, nm in enumerate(names):
                cp = _remote(_half_of(ins[nm], nm, c), _half_of(outs[nm].at[me], nm, c), ici_send.at[k * n + i],
                             ici_recv.at[k * n + i], (px, py, c))
                cp.start()
                sends.append(cp)
        for k, (px, py) in enumerate(_chip_peers()):
            for i, nm in enumerate(names):
                landed = _half_of(outs[nm].at[2 * px + py], nm, c)
                _remote(landed, landed, ici_send.at[k * n + i], ici_recv.at[k * n + i], (px, py, c)).wait_recv()
                fw = _remote(landed, landed, d2d_send.at[k * n + i], d2d_recv.at[k * n + i], sib)
                fw.start()
                sends.append(fw)
        for k, (px, py) in enumerate(_chip_peers()):
            for i, nm in enumerate(names):
                other = _half_of(outs[nm].at[2 * px + py], nm, 1 - c)
                _remote(other, other, d2d_send.at[k * n + i], d2d_recv.at[k * n + i], sib).wait_recv()
        for cp in sends:
            cp.wait_send()
        for cp in own:
            cp.wait()

    return _hbm_call(body, "ag_weights", shards, lambda nm, a: jax.ShapeDtypeStruct((N_CHIPS,) + a.shape, a.dtype),
                     (3 * n, 3 * n, 3 * n, 3 * n, n, n))


_SEM = pl.BlockSpec(memory_space=pltpu.SEMAPHORE)
_N_DEST = 4


def _late_dests():
    x, y, c = lax.axis_index("x"), lax.axis_index("y"), lax.axis_index("c")
    return [(px, py, c) for px, py in _chip_peers()] + [(x, y, 1 - c)]


def gather_start(shards):
    names = tuple(shards)
    n = len(names)

    def body(*refs):
        ins, lands = refs[:n], refs[n:2 * n]
        send_sems, recv_sems, token = refs[2 * n], refs[2 * n + 1], refs[-1]
        me = 2 * lax.axis_index("x") + lax.axis_index("y")
        for i in range(n):
            for k, dev in enumerate(_late_dests()):
                _remote(ins[i], lands[i].at[me], send_sems.at[_N_DEST * i + k], recv_sems.at[_N_DEST * i + k],
                        dev).start()
        token[...] = jnp.zeros_like(token)

    hbm = lambda a: pltpu.with_memory_space_constraint(a, pltpu.HBM)
    lands = [lax.empty((N_CHIPS,) + shards[nm].shape, shards[nm].dtype) for nm in names]
    res = pl.pallas_call(
        body, name="ag_late_start",
        out_shape=(pltpu.SemaphoreType.DMA((_N_DEST * n,)), pltpu.SemaphoreType.DMA((_N_DEST * n,)),
                   *[pltpu.HBM(shards[nm].shape, shards[nm].dtype) for nm in names],
                   *[pltpu.HBM(a.shape, a.dtype) for a in lands], jax.ShapeDtypeStruct((8, 128), F32)),
        in_specs=[_HBM] * (2 * n),
        out_specs=(_SEM, _SEM, *[_HBM] * (2 * n), pl.BlockSpec(memory_space=pltpu.VMEM)),
        input_output_aliases={i: 2 + i for i in range(2 * n)},
        compiler_params=pltpu.CompilerParams(has_side_effects=pltpu.SideEffectType.DATAFLOW_SIDE_EFFECTING),
    )(*[hbm(shards[nm]) for nm in names], *[hbm(a) for a in lands])
    return names, res[0], res[1], res[2:2 + n], res[2 + n:2 + 2 * n], res[-1]


def gather_wait(started, after):
    names, send_sems, recv_sems, shards, lands, _ = started
    n = len(names)

    def body(*refs):
        ins, land_refs = refs[:n], refs[n:2 * n]
        send_ref, recv_ref = refs[2 * n], refs[2 * n + 1]
        for i in range(n):
            for k, dev in enumerate(_late_dests()):
                cp = _remote(ins[i], land_refs[i].at[0], send_ref.at[_N_DEST * i + k], recv_ref.at[_N_DEST * i + k], dev)
                cp.wait_send()
                cp.wait_recv()

    res = pl.pallas_call(
        body, name="ag_late_wait",
        out_shape=(*[pltpu.HBM(a.shape, a.dtype) for a in shards], *[pltpu.HBM(a.shape, a.dtype) for a in lands]),
        in_specs=[_HBM] * (2 * n) + [_SEM, _SEM, pl.BlockSpec(memory_space=pl.ANY)],
        out_specs=tuple([_HBM] * (2 * n)),
        input_output_aliases={i: i for i in range(2 * n)},
        compiler_params=pltpu.CompilerParams(has_side_effects=pltpu.SideEffectType.DATAFLOW_SIDE_EFFECTING),
    )(*shards, *lands, send_sems, recv_sems, after)
    return dict(zip(names, res[n:]))


def sibling_halves(blocks):
    names = tuple(blocks)

    def body(*refs):
        n = len(names)
        ins, outs = dict(zip(names, refs[:n])), dict(zip(names, refs[n:2 * n]))
        send_sems, recv_sems = refs[2 * n:]
        x, y, c = lax.axis_index("x"), lax.axis_index("y"), lax.axis_index("c")
        cps = [_remote(_half_of(ins[nm], nm, 1 - c, lead=1), outs[nm], send_sems.at[i], recv_sems.at[i], (x, y, 1 - c))
               for i, nm in enumerate(names)]
        for cp in cps:
            cp.start()
        for cp in cps:
            cp.wait()

    def half_shape(nm, a):
        r, cdim = a.shape[-2:]
        return jax.ShapeDtypeStruct((N_CHIPS, r, cdim // 2) if nm in _SPLIT_COLS else (N_CHIPS, r // 2, cdim), a.dtype)

    return _hbm_call(body, "rs_sibling", blocks, half_shape, (len(names), len(names)))


def scatter_halves(blocks):
    names = tuple(blocks)
    n = len(names)

    def body(*refs):
        ins, outs = dict(zip(names, refs[:n])), dict(zip(names, refs[n:2 * n]))
        send_sems, recv_sems = refs[2 * n:]
        c = lax.axis_index("c")
        cps = [_remote(ins[nm].at[2 * px + py], outs[nm].at[k], send_sems.at[k * n + i], recv_sems.at[k * n + i],
                       (px, py, c))
               for k, (px, py) in enumerate(_chip_peers()) for i, nm in enumerate(names)]
        for cp in cps:
            cp.start()
        for cp in cps:
            cp.wait_recv()
        for cp in cps:
            cp.wait_send()

    return _hbm_call(body, "rs_grads", blocks, lambda nm, a: jax.ShapeDtypeStruct((3,) + a.shape[1:], a.dtype),
                     (3 * n, 3 * n))


def sibling_assemble(arrays):
    names = tuple(arrays)

    def body(*refs):
        n = len(names)
        ins, outs = dict(zip(names, refs[:n])), dict(zip(names, refs[n:2 * n]))
        send_sems, recv_sems = refs[2 * n:]
        x, y, c = lax.axis_index("x"), lax.axis_index("y"), lax.axis_index("c")
        cps = [_remote(_half_of(ins[nm], nm, c), _half_of(outs[nm], nm, c), send_sems.at[i], recv_sems.at[i],
                       (x, y, 1 - c)) for i, nm in enumerate(names)]
        for cp in cps:
            cp.start()
        for i, nm in enumerate(names):
            other = _half_of(outs[nm], nm, 1 - c)
            _remote(other, other, send_sems.at[i], recv_sems.at[i], (x, y, 1 - c)).wait_recv()
        for cp in cps:
            cp.wait_send()

    return _hbm_call(body, "rs_assemble", arrays, lambda nm, a: jax.ShapeDtypeStruct(a.shape, a.dtype),
                     (len(names), len(names)), in_place=True)


def all_reduce_small(v):
    def body(v_ref, tot_ref, gath_ref, send_sems, recv_sems):
        x, y, c = lax.axis_index("x"), lax.axis_index("y"), lax.axis_index("c")
        me = 4 * x + 2 * y + c
        gath_ref[me] = v_ref[...]

        def peer(k):
            m = k + 1
            return (x ^ (m >> 2 & 1), y ^ (m >> 1 & 1), c ^ (m & 1))

        sends = [pltpu.make_async_remote_copy(src_ref=v_ref, dst_ref=gath_ref.at[me], send_sem=send_sems.at[k],
                                              recv_sem=recv_sems.at[k], device_id=peer(k), device_id_type=MESH)
                 for k in range(N_DEV - 1)]
        for cp in sends:
            cp.start()
        for k in range(N_DEV - 1):
            px, py, pc = peer(k)
            pltpu.make_async_remote_copy(src_ref=v_ref, dst_ref=gath_ref.at[4 * px + 2 * py + pc],
                                         send_sem=send_sems.at[k], recv_sem=recv_sems.at[k], device_id=peer(k),
                                         device_id_type=MESH).wait_recv()
        for cp in sends:
            cp.wait_send()
        acc = gath_ref[0]
        for d in range(1, N_DEV):
            acc = acc + gath_ref[d]
        tot_ref[...] = acc

    vm = pl.BlockSpec(memory_space=pltpu.VMEM)
    return pl.pallas_call(
        body, name="ar_small", in_specs=[vm], out_specs=[vm, vm],
        out_shape=(jax.ShapeDtypeStruct(v.shape, v.dtype), jax.ShapeDtypeStruct((N_DEV,) + v.shape, v.dtype)),
        scratch_shapes=[pltpu.SemaphoreType.DMA((N_DEV - 1,)), pltpu.SemaphoreType.DMA((N_DEV - 1,))],
    )(v)[0]


def _elementwise(fn, name, ins, n_out, out_dtype=F32):
    r, cdim = ins[0].shape
    tr = _pick(r, tuple(p for p in (488, 256, 128, 104, 64, 32, 16, 8) if p * cdim * 4 <= 2 * 1024 * 1024))
    spec = pl.BlockSpec((tr, cdim), lambda i: (i, 0))

    def body(*refs):
        res = fn(*[ref[...] for ref in refs[:len(ins)]])
        for o_ref, v in zip(refs[len(ins):], res):
            o_ref[...] = v

    return pl.pallas_call(
        body, name=name, grid=(r // tr,), in_specs=[spec] * len(ins), out_specs=[spec] * n_out,
        out_shape=tuple(jax.ShapeDtypeStruct((r, cdim), out_dtype) for _ in range(n_out)),
        compiler_params=pltpu.CompilerParams(dimension_semantics=("parallel",), vmem_limit_bytes=VMEM_LIMIT),
    )(*ins)


def _half_block_specs(nm, shard_shape):
    r, cdim = shard_shape
    if nm in _SPLIT_COLS:
        return (None, r, cdim // 2), (lambda j, c: (j, 0, c))
    return (None, r // 2, cdim), (lambda j, c: (j, c, 0))


def _presum(nm, sel, g32, a):
    blk, at = _half_block_specs(nm, g32.shape[1:])

    def body(s_ref, g_ref, a_ref, o_ref):
        del s_ref
        o_ref[...] = (g_ref[...] + a_ref[...]).astype(BF16)

    return pl.pallas_call(
        body, name="rs_presum_" + nm,
        grid_spec=pltpu.PrefetchScalarGridSpec(
            num_scalar_prefetch=1, grid=(N_CHIPS,),
            in_specs=[pl.BlockSpec(blk, lambda j, s: at(j, s[0])), pl.BlockSpec(blk, lambda j, s: (j, 0, 0))],
            out_specs=pl.BlockSpec(blk, lambda j, s: (j, 0, 0))),
        out_shape=jax.ShapeDtypeStruct(a.shape, BF16),
        compiler_params=pltpu.CompilerParams(dimension_semantics=("parallel",), vmem_limit_bytes=VMEM_LIMIT),
    )(sel, g32, a)


def _finalsum(nm, sel, g32, a, got):
    blk, at = _half_block_specs(nm, g32.shape[1:])

    def body(s_ref, g_ref, a_ref, r_ref, o_ref):
        del s_ref
        acc = g_ref[...] + a_ref[...]
        for k in range(3):
            acc = acc + r_ref[k].astype(F32)
        o_ref[...] = acc

    return pl.pallas_call(
        body, name="rs_final_" + nm,
        grid_spec=pltpu.PrefetchScalarGridSpec(
            num_scalar_prefetch=1, grid=(1,),
            in_specs=[pl.BlockSpec(blk, lambda i, s: at(s[1], s[0])), pl.BlockSpec(blk, lambda i, s: (s[1], 0, 0)),
                      pl.BlockSpec(got.shape, lambda i, s: (0, 0, 0))],
            out_specs=pl.BlockSpec(blk[1:], lambda i, s: at(0, s[0])[1:])),
        out_shape=jax.ShapeDtypeStruct(g32.shape[1:], F32),
        compiler_params=pltpu.CompilerParams(dimension_semantics=("arbitrary",), vmem_limit_bytes=VMEM_LIMIT),
    )(sel, g32, a, got)


def _adamw(w, g, m, v, name):
    shape = w.shape
    to2 = lambda a: a.reshape(-1, shape[-1])

    def fn(w_, g_, m_, v_):
        m_new = ADAM_B1 * m_ + (1.0 - ADAM_B1) * g_
        v_new = ADAM_B2 * v_ + (1.0 - ADAM_B2) * (g_ * g_)
        m_hat = m_new / (1.0 - ADAM_B1 ** ADAM_STEP)
        v_hat = v_new / (1.0 - ADAM_B2 ** ADAM_STEP)
        delta = -ADAM_LR * (m_hat / (jnp.sqrt(v_hat) + ADAM_EPS) + ADAM_WD * w_)
        return delta, m_new, v_new

    outs = _elementwise(fn, name, [to2(a) for a in (w, g, m, v)], 3)
    return tuple(o.reshape(shape) for o in outs)


_BIG = ("w_mod", "w_in", "w_pa", "w_pd", "w_out", "w_up", "w_down")
_EARLY = ("w_mod", "w_in")
_LATE = ("w_pa", "w_pd", "w_out", "w_up", "w_down")
_COL_SHARDED = ("w_mod", "w_up")
_FULL_SHAPE = {"w_mod": (D_MODEL, MOD_W), "w_in": (IN_COLS, D_MODEL), "w_pa": (Q_W, D_MODEL), "w_pd": (GDN_W, D_MODEL),
               "w_out": (D_MODEL, D_MODEL), "w_up": (D_MODEL, 2 * D_FF), "w_down": (D_FF, D_MODEL)}


def _shard_shape(name):
    r, cdim = _FULL_SHAPE[name]
    return (r, cdim // N_CHIPS) if name in _COL_SHARDED else (r // N_CHIPS, cdim)


_CONV_ELEMS = 2 * (3 * CONV_W // N_CHIPS + 3 * 2 * D_FF // N_CHIPS)
_CONV_ROWS = 32


def _blocks_of_full(name, full):
    r, cdim = _FULL_SHAPE[name]
    if name in _COL_SHARDED:
        return full.reshape(r, N_CHIPS, cdim // N_CHIPS).transpose(1, 0, 2)
    return full.reshape(N_CHIPS, r // N_CHIPS, cdim)


def _full_of_blocks(name, blocks):
    r, cdim = _FULL_SHAPE[name]
    if name in _COL_SHARDED:
        return blocks.transpose(1, 0, 2).reshape(r, cdim)
    return blocks.reshape(r, cdim)


def _w_in_regroup(w_in_t):
    main = jnp.concatenate([w_in_t[:SMALL_AT], w_in_t[SMALL_AT + 4 * GDN_HEADS:]], axis=0)
    small = jnp.pad(w_in_t[SMALL_AT:SMALL_AT + 4 * GDN_HEADS], ((0, HEAD_DIM - 4 * GDN_HEADS), (0, 0)))
    return main, small


def _w_in_ungroup(main, small):
    return jnp.concatenate([main[:SMALL_AT], small[:4 * GDN_HEADS], main[SMALL_AT:]], axis=0)


_SMALL = ("c_ctx", "b_mod", "q_norm_w", "k_norm_w", "conv_qkv_w", "a_log", "dt_bias", "gdn_norm_w", "ffn_conv_w",
          "ffn_conv_b", "final_norm_w")


def _pack_small(tree, rows):
    flat = jnp.concatenate([tree[nm].reshape(-1) for nm in _SMALL])
    return jnp.pad(flat, (0, rows * 128 - flat.shape[0])).reshape(rows, 128)


def _unpack_small(packed, like):
    flat, out, off = packed.reshape(-1), {}, 0
    for nm in _SMALL:
        size = int(np.prod(like[nm].shape))
        out[nm] = flat[off:off + size].reshape(like[nm].shape)
        off += size
    return out


def kernel(x, c, ctx, c_ctx, w_mod, b_mod, w_in, q_norm_w, k_norm_w, conv_qkv_w, a_log, dt_bias, gdn_norm_w, w_pa, w_pd, w_out, w_up, ffn_conv_w, ffn_conv_b, w_down, final_norm_w, loss_target, m_c_ctx, m_w_mod, m_b_mod, m_w_in, m_q_norm_w, m_k_norm_w, m_conv_qkv_w, m_a_log, m_dt_bias, m_gdn_norm_w, m_w_pa, m_w_pd, m_w_out, m_w_up, m_ffn_conv_w, m_ffn_conv_b, m_w_down, m_final_norm_w, v_c_ctx, v_w_mod, v_b_mod, v_w_in, v_q_norm_w, v_k_norm_w, v_conv_qkv_w, v_a_log, v_dt_bias, v_gdn_norm_w, v_w_pa, v_w_pd, v_w_out, v_w_up, v_ffn_conv_w, v_ffn_conv_b, v_w_down, v_final_norm_w):
    names = ("c_ctx", "w_mod", "b_mod", "w_in", "q_norm_w", "k_norm_w", "conv_qkv_w", "a_log", "dt_bias", "gdn_norm_w",
             "w_pa", "w_pd", "w_out", "w_up", "ffn_conv_w", "ffn_conv_b", "w_down", "final_norm_w")
    w_sh = dict(c_ctx=c_ctx, w_mod=w_mod, b_mod=b_mod, w_in=w_in, q_norm_w=q_norm_w, k_norm_w=k_norm_w,
                conv_qkv_w=conv_qkv_w, a_log=a_log, dt_bias=dt_bias, gdn_norm_w=gdn_norm_w, w_pa=w_pa, w_pd=w_pd,
                w_out=w_out, w_up=w_up, ffn_conv_w=ffn_conv_w, ffn_conv_b=ffn_conv_b, w_down=w_down,
                final_norm_w=final_norm_w)
    m_sh = dict(c_ctx=m_c_ctx, w_mod=m_w_mod, b_mod=m_b_mod, w_in=m_w_in, q_norm_w=m_q_norm_w, k_norm_w=m_k_norm_w,
                conv_qkv_w=m_conv_qkv_w, a_log=m_a_log, dt_bias=m_dt_bias, gdn_norm_w=m_gdn_norm_w, w_pa=m_w_pa,
                w_pd=m_w_pd, w_out=m_w_out, w_up=m_w_up, ffn_conv_w=m_ffn_conv_w, ffn_conv_b=m_ffn_conv_b,
                w_down=m_w_down, final_norm_w=m_final_norm_w)
    v_sh = dict(c_ctx=v_c_ctx, w_mod=v_w_mod, b_mod=v_b_mod, w_in=v_w_in, q_norm_w=v_q_norm_w, k_norm_w=v_k_norm_w,
                conv_qkv_w=v_conv_qkv_w, a_log=v_a_log, dt_bias=v_dt_bias, gdn_norm_w=v_gdn_norm_w, w_pa=v_w_pa,
                w_pd=v_w_pd, w_out=v_w_out, w_up=v_w_up, ffn_conv_w=v_ffn_conv_w, ffn_conv_b=v_ffn_conv_b,
                w_down=v_w_down, final_norm_w=v_final_norm_w)
    chip = 2 * lax.axis_index("x") + lax.axis_index("y")

    conv_bits = jnp.concatenate([lax.bitcast_convert_type(w_sh[nm][0], BF16).reshape(-1)
                                 for nm in ("conv_qkv_w", "ffn_conv_w")])
    shards = {nm: w_sh[nm][0].astype(BF16).T if nm == "w_in" else w_sh[nm][0].astype(BF16) for nm in _BIG}
    shards["conv"] = jnp.pad(conv_bits, (0, _CONV_ROWS * D_MODEL - _CONV_ELEMS)).reshape(_CONV_ROWS, D_MODEL)
    gathered = all_gather_chips({nm: shards[nm] for nm in _EARLY + ("conv",)})
    gathered, late_shards = lax.optimization_barrier((gathered, {nm: shards[nm] for nm in _LATE}))
    started = gather_start(late_shards)
    c = c + started[-1][0:1, 0:1]

    wb = {nm: _full_of_blocks(nm, gathered[nm]) for nm in _EARLY}
    wb["w_in_main"], wb["w_in_small"] = _w_in_regroup(wb.pop("w_in"))
    conv_all = gathered["conv"].reshape(N_CHIPS, -1)[:, :_CONV_ELEMS]
    n_cq = 2 * 3 * CONV_W // N_CHIPS
    unbits = lambda a, w: lax.bitcast_convert_type(a.reshape(N_CHIPS, 3, w // N_CHIPS, 2), F32).transpose(1, 0, 2).reshape(3, w)
    ws = dict(c_ctx=c_ctx, b_mod=b_mod, q_norm_w=q_norm_w, k_norm_w=k_norm_w, a_log=a_log[0], dt_bias=dt_bias[0],
              gdn_norm_w=gdn_norm_w, ffn_conv_b=ffn_conv_b, final_norm_w=final_norm_w,
              conv_qkv_w=unbits(conv_all[:, :n_cq], CONV_W), ffn_conv_w=unbits(conv_all[:, n_cq:], 2 * D_FF))
    wz = {nm: jnp.zeros(a.shape, F32) for nm, a in wb.items()}
    wz.update({nm: jnp.zeros(_FULL_SHAPE[nm], F32) for nm in _LATE})

    mixed, vjp_mix = jax.vjp(lambda x_, wz_, ws_: token_mixing(x_, wz_, wb, ws_, c, ctx[0]), x[0], wz, ws)
    got = gather_wait(started, mixed["gdn_x"])
    wb_late = {nm: _full_of_blocks(nm, got[nm]) for nm in _LATE}
    loss_local, vjp_chan = jax.vjp(
        lambda mixed_, wz_, ws_: channel_mixing(mixed_, wz_, wb_late, ws_, loss_target[0]), mixed, wz, ws)
    d_mixed, gz_chan, gs_chan = vjp_chan(jnp.ones((), F32))
    gx, gz_mix, gs_mix = vjp_mix(d_mixed)
    gz = {nm: gz_chan[nm] if nm in _LATE else gz_mix[nm] for nm in wz}
    gs = jax.tree.map(jnp.add, gs_mix, gs_chan)
    loss = lax.psum(loss_local, ("x", "y", "c"))

    gz["w_in"] = _w_in_ungroup(gz.pop("w_in_main"), gz.pop("w_in_small"))
    g32 = {nm: _blocks_of_full(nm, gz[nm]) for nm in _BIG}
    sel = jnp.stack([lax.axis_index("c"), chip]).astype(jnp.int32)
    theirs = sibling_halves(g32)
    got = scatter_halves({nm: _presum(nm, sel, g32[nm], theirs[nm]) for nm in _BIG})
    g_big = sibling_assemble({nm: _finalsum(nm, sel, g32[nm], theirs[nm], got[nm]) for nm in _BIG})

    gs["a_log"], gs["dt_bias"] = gs["a_log"][None], gs["dt_bias"][None]
    like = {nm: gs[nm] for nm in _SMALL}
    small_rows = -(-sum(int(np.prod(like[nm].shape)) for nm in _SMALL) // 1024) * 8
    g_small = _unpack_small(all_reduce_small(_pack_small(gs, small_rows)), like)
    for nm, width in (("conv_qkv_w", CONV_W), ("ffn_conv_w", 2 * D_FF)):
        g_small[nm] = lax.dynamic_slice_in_dim(g_small[nm], chip * (width // N_CHIPS), width // N_CHIPS, axis=1)[None]

    grads, deltas, new_m, new_v = {}, {}, {}, {}
    for nm in _BIG:
        g = g_big[nm].T if nm == "w_in" else g_big[nm]
        grads[nm] = g[None]
        deltas[nm], new_m[nm], new_v[nm] = (o[None] for o in _adamw(w_sh[nm][0], g, m_sh[nm][0], v_sh[nm][0],
                                                                     "adamw_" + nm))
    shard_like = {nm: w_sh[nm] for nm in _SMALL}
    rows_l = -(-sum(int(np.prod(shard_like[nm].shape)) for nm in _SMALL) // 1024) * 8
    g_l = _pack_small({nm: g_small[nm].reshape(w_sh[nm].shape) for nm in _SMALL}, rows_l)
    outs = _adamw(_pack_small(w_sh, rows_l), g_l, _pack_small(m_sh, rows_l), _pack_small(v_sh, rows_l), "adamw_small")
    grads.update(_unpack_small(g_l, shard_like))
    for tree, packed in zip((deltas, new_m, new_v), outs):
        tree.update(_unpack_small(packed, shard_like))

    return (loss, gx[None], *[grads[nm] for nm in names], *[deltas[nm] for nm in names],
            *[new_m[nm] for nm in names], *[new_v[nm] for nm in names])
```

```python
import functools
import math

import jax
import jax.numpy as jnp
import numpy as np
from jax import lax
from jax.experimental import pallas as pl
from jax.experimental.pallas import tpu as pltpu

F32 = jnp.float32
BF16 = jnp.bfloat16
HIGHEST = lax.Precision.HIGHEST
MESH = pl.DeviceIdType.MESH

D_MODEL = 1024
GRID_W = 64
ATTN_HEADS = 8
ATTN_KV_HEADS = 2
ATTN_GROUP = ATTN_HEADS // ATTN_KV_HEADS
HEAD_DIM = 128
ROPE_THETA = 10000.0
GDN_HEADS = 8
GDN_CHUNK = 64
D_FF = 2816
NORM_EPS = 1e-6
KV_W = ATTN_KV_HEADS * HEAD_DIM
Q_W = ATTN_HEADS * HEAD_DIM
GDN_W = GDN_HEADS * HEAD_DIM
CONV_W = 3 * GDN_W
MOD_W = 6 * D_MODEL
IN_COLS = 2 * KV_W + CONV_W + 4 * GDN_HEADS + Q_W + GDN_W + 2 * D_MODEL
IN_MAIN = IN_COLS - 4 * GDN_HEADS
SMALL_AT = 2 * KV_W + CONV_W
N_CHIPS = 4
N_DEV = 8

ADAM_LR = 0.001
ADAM_B1 = 0.9
ADAM_B2 = 0.999
ADAM_EPS = 1e-08
ADAM_WD = 0.01
ADAM_STEP = 10

VMEM_LIMIT = 48 * 1024 * 1024
MATMUL_VMEM_BUDGET = 40 * 1024 * 1024
MATMUL_STEP_BYTES = 1200 * 1024


def _pick(dim, prefs):
    for p in prefs:
        if p <= dim and dim % p == 0:
            return p
    return dim


_DIMS = {
    "nn": (((1,), (0,)), ((), ())),
    "nt": (((1,), (1,)), ((), ())),
    "tn": (((0,), (0,)), ((), ())),
}


def _matmul_plan(m, n, k, a_bytes, b_bytes):
    best = None
    for tm in (2304, 2048, 1152, 1024, 768, 512, 384, 256, 128, m):
        for tn in (2560, 1536, 1408, 1024, 768, 512, 256, 128, n):
            for tk in (3840, 2816, 2560, 2304, 2048, 1920, 1408, 1152, 1024, 768, 512, 256, 128, k):
                if tm > m or tn > n or tk > k or m % tm or n % tn or k % tk:
                    continue
                blocks = tm * tk * a_bytes + tk * tn * b_bytes + tm * tn * 4
                casts = (tm * tk * 2 if a_bytes > 2 else 0) + (tk * tn * 2 if b_bytes > 2 else 0) + tm * tn * 4
                if 2 * blocks + casts > MATMUL_VMEM_BUDGET:
                    continue
                nm, nn, nk = m // tm, n // tn, k // tk
                size_a, size_b = m * k * a_bytes, k * n * b_bytes
                for n_inner in (True, False):
                    if n_inner:
                        traffic = (size_a if nk == 1 else nn * size_a) + nm * size_b
                    else:
                        traffic = nn * size_a + (size_b if nk == 1 else nm * size_b)
                    cost = traffic + nm * nn * nk * MATMUL_STEP_BYTES + (nk - 1) * m * n * 4
                    if best is None or cost < best[0]:
                        best = (cost, tm, tn, tk, n_inner)
    return best[1:]


def _matmul(a, b, mode, name):
    if mode == "nn":
        (m, k), (_, n) = a.shape, b.shape
    elif mode == "nt":
        (m, k), (n, _) = a.shape, b.shape
    else:
        (k, m), (_, n) = a.shape, b.shape
    tm, tn, tk, n_inner = _matmul_plan(m, n, k, a.dtype.itemsize, b.dtype.itemsize)
    nk = k // tk
    ij = (lambda g0, g1: (g0, g1)) if n_inner else (lambda g0, g1: (g1, g0))
    if mode == "tn":
        a_spec = pl.BlockSpec((tk, tm), lambda g0, g1, l: (l, ij(g0, g1)[0]))
    else:
        a_spec = pl.BlockSpec((tm, tk), lambda g0, g1, l: (ij(g0, g1)[0], l))
    if mode == "nt":
        b_spec = pl.BlockSpec((tn, tk), lambda g0, g1, l: (ij(g0, g1)[1], l))
    else:
        b_spec = pl.BlockSpec((tk, tn), lambda g0, g1, l: (l, ij(g0, g1)[1]))
    dims = _DIMS[mode]

    def body(a_ref, b_ref, o_ref):
        part = lax.dot_general(a_ref[...].astype(BF16), b_ref[...].astype(BF16), dims, preferred_element_type=F32)
        if nk == 1:
            o_ref[...] = part
        else:
            l = pl.program_id(2)

            @pl.when(l == 0)
            def _():
                o_ref[...] = part

            @pl.when(l > 0)
            def _():
                o_ref[...] += part

    return pl.pallas_call(
        body,
        name=name,
        grid=(m // tm, n // tn, nk) if n_inner else (n // tn, m // tm, nk),
        in_specs=[a_spec, b_spec],
        out_specs=pl.BlockSpec((tm, tn), lambda g0, g1, l: ij(g0, g1)),
        out_shape=jax.ShapeDtypeStruct((m, n), F32),
        compiler_params=pltpu.CompilerParams(dimension_semantics=("parallel", "parallel", "arbitrary"),
                                             vmem_limit_bytes=VMEM_LIMIT),
    )(a, b)


@functools.partial(jax.custom_vjp, nondiff_argnums=(3,))
def pmm(a, w, wz, name):
    del wz
    return _matmul(a, w, "nn", name + "_f")


def _pmm_fwd(a, w, wz, name):
    del wz
    return _matmul(a, w, "nn", name + "_f"), (a, w)


def _pmm_bwd(name, res, g):
    a, w = res
    da = _matmul(g, w, "nt", name + "_da")
    if a.shape[0] < 128:
        pad = 128 - a.shape[0]
        at = jnp.pad(a.T, ((0, 0), (0, pad)))
        gp = jnp.pad(g, ((0, pad), (0, 0)))
        dw = _matmul(at, gp, "nn", name + "_dw")
    else:
        dw = _matmul(a, g, "tn", name + "_dw")
    return da, jnp.zeros_like(w), dw


pmm.defvjp(_pmm_fwd, _pmm_bwd)


@functools.partial(jax.custom_vjp, nondiff_argnums=(3,))
def pmm_t(a, wt, wtz, name):
    del wtz
    return _matmul(a, wt, "nt", name + "_f")


def _pmm_t_fwd(a, wt, wtz, name):
    del wtz
    return _matmul(a, wt, "nt", name + "_f"), (a, wt)


def _pmm_t_bwd(name, res, g):
    a, wt = res
    return _matmul(g, wt, "nn", name + "_da"), jnp.zeros_like(wt), _matmul(g, a, "tn", name + "_dw")


pmm_t.defvjp(_pmm_t_fwd, _pmm_t_bwd)


def rowop(fn, name, rows, bcs=(), crows=(), cbcs=(), tr=256):
    rows, bcs, crows, cbcs = tuple(rows), tuple(bcs), tuple(crows), tuple(cbcs)
    n_rows = rows[0].shape[0]
    tr = _pick(n_rows, (tr, 128, 64, 32, 16, 8))
    nr, nb, ncr, ncb = len(rows), len(bcs), len(crows), len(cbcs)
    n_in = nr + nb + ncr + ncb
    grid = (n_rows // tr,)

    def blk(arr):
        return jax.ShapeDtypeStruct((tr, arr.shape[1]), arr.dtype)

    def row_spec(arr):
        return pl.BlockSpec((tr, arr.shape[1]), lambda i: (i, 0))

    def bc_spec(arr):
        return pl.BlockSpec(arr.shape, lambda i: (0, 0))

    out_blk = jax.eval_shape(fn, *[blk(r) for r in rows], *bcs, *[blk(r) for r in crows], *cbcs)
    n_out = len(out_blk)
    out_shape = tuple(jax.ShapeDtypeStruct((n_rows, o.shape[1]), o.dtype) for o in out_blk)
    in_specs = ([row_spec(r) for r in rows] + [bc_spec(b) for b in bcs]
                + [row_spec(r) for r in crows] + [bc_spec(b) for b in cbcs])

    def order(vals):
        return vals

    def fwd_call(args):
        def body(*refs):
            vals = [r[...] for r in refs[:n_in]]
            res = fn(*order(vals))
            for o_ref, r in zip(refs[n_in:], res):
                o_ref[...] = r

        return pl.pallas_call(
            body, name=name + "_f", grid=grid, in_specs=in_specs,
            out_specs=[row_spec(o) for o in out_shape], out_shape=out_shape,
            compiler_params=pltpu.CompilerParams(dimension_semantics=("parallel",), vmem_limit_bytes=VMEM_LIMIT),
        )(*args)

    def bwd_call(args, cts):
        def body(*refs):
            vals = [r[...] for r in refs[:n_in]]
            ct_refs = refs[n_in:n_in + n_out]
            d_rows = refs[n_in + n_out:n_in + n_out + nr]
            d_bcs = refs[n_in + n_out + nr:]
            consts = vals[nr + nb:]
            _, vjp = jax.vjp(lambda *p: fn(*p, *consts), *vals[:nr + nb])
            grads = vjp(tuple(c[...] for c in ct_refs))
            for ref, g in zip(d_rows, grads[:nr]):
                ref[...] = g

            @pl.when(pl.program_id(0) == 0)
            def _():
                for ref in d_bcs:
                    ref[...] = jnp.zeros_like(ref)

            for ref, g in zip(d_bcs, grads[nr:]):
                ref[...] += g

        d_shape = tuple(jax.ShapeDtypeStruct(r.shape, r.dtype) for r in rows + bcs)
        return pl.pallas_call(
            body, name=name + "_b", grid=grid,
            in_specs=in_specs + [row_spec(o) for o in out_shape],
            out_specs=[row_spec(r) for r in rows] + [bc_spec(b) for b in bcs], out_shape=d_shape,
            compiler_params=pltpu.CompilerParams(dimension_semantics=("arbitrary",), vmem_limit_bytes=VMEM_LIMIT),
        )(*args, *cts)

    @jax.custom_vjp
    def op(diff, const):
        return fwd_call(diff + const)

    def op_fwd(diff, const):
        return fwd_call(diff + const), (diff, const)

    def op_bwd(res, cts):
        diff, const = res
        grads = bwd_call(diff + const, tuple(cts))
        return tuple(grads), tuple(jnp.zeros_like(c) for c in const)

    op.defvjp(op_fwd, op_bwd)
    return op(rows + bcs, crows + cbcs)


def colop(fn, name, arrays, uses, n_const, nblk, cw=128):
    arrays = tuple(arrays)
    n_diff = len(arrays) - n_const
    nd = sum(1 for u in uses if u[0] < n_diff)
    assert all(u[0] < n_diff for u in uses[:nd]) and all(u[0] >= n_diff for u in uses[nd:])

    def spec(u):
        return pl.BlockSpec((arrays[u[0]].shape[0], cw), lambda j, off=u[1]: (0, off + j))

    def out_spec(rows):
        return pl.BlockSpec((rows, cw), lambda j: (0, j))

    out_blk = jax.eval_shape(fn, *[jax.ShapeDtypeStruct((arrays[u[0]].shape[0], cw), arrays[u[0]].dtype)
                                   for u in uses])
    out_shape = tuple(jax.ShapeDtypeStruct((o.shape[0], nblk * cw), o.dtype) for o in out_blk)
    params = pltpu.CompilerParams(dimension_semantics=("parallel",), vmem_limit_bytes=VMEM_LIMIT)

    def fwd_call(arrs):
        def body(*refs):
            res = fn(*[r[...] for r in refs[:len(uses)]])
            for o_ref, r in zip(refs[len(uses):], res):
                o_ref[...] = r

        return pl.pallas_call(
            body, name=name + "_f", grid=(nblk,), in_specs=[spec(u) for u in uses],
            out_specs=[out_spec(o.shape[0]) for o in out_shape], out_shape=out_shape, compiler_params=params,
        )(*[arrs[u[0]] for u in uses])

    def bwd_call(arrs, cts):
        def body(*refs):
            vals = [r[...] for r in refs[:len(uses)]]
            ct_refs = refs[len(uses):len(uses) + len(out_shape)]
            _, vjp = jax.vjp(lambda *p: fn(*p, *vals[nd:]), *vals[:nd])
            for ref, g in zip(refs[len(uses) + len(out_shape):], vjp(tuple(c[...] for c in ct_refs))):
                ref[...] = g

        d_shape = tuple(jax.ShapeDtypeStruct((arrays[u[0]].shape[0], nblk * cw), F32) for u in uses[:nd])
        return pl.pallas_call(
            body, name=name + "_b", grid=(nblk,),
            in_specs=[spec(u) for u in uses] + [out_spec(o.shape[0]) for o in out_shape],
            out_specs=[out_spec(s.shape[0]) for s in d_shape], out_shape=d_shape, compiler_params=params,
        )(*[arrs[u[0]] for u in uses], *cts)

    @jax.custom_vjp
    def op(diff, const):
        return fwd_call(diff + const)

    def op_fwd(diff, const):
        return fwd_call(diff + const), (diff, const)

    def op_bwd(res, cts):
        diff, const = res
        d_uses = bwd_call(diff + const, tuple(cts))
        grads = []
        for i in range(n_diff):
            parts = sorted([(u[1], k) for k, u in enumerate(uses[:nd]) if u[0] == i])
            grads.append(d_uses[parts[0][1]] if len(parts) == 1
                         else jnp.concatenate([d_uses[k] for _, k in parts], axis=1))
        return tuple(grads), tuple(jnp.zeros_like(c) for c in const)

    op.defvjp(op_fwd, op_bwd)
    return op(arrays[:n_diff], arrays[n_diff:])


@functools.partial(jax.custom_vjp, nondiff_argnums=(1,))
def _roll_rows(x, k):
    return pltpu.roll(x, k % x.shape[0], 0)


def _roll_rows_fwd(x, k):
    return _roll_rows(x, k), None


def _roll_rows_bwd(k, _, g):
    return (_roll_rows(g, -k),)


_roll_rows.defvjp(_roll_rows_fwd, _roll_rows_bwd)


def _conv3(x, w0, w1, w2, starts):
    rows = lax.broadcasted_iota(jnp.int32, x.shape, 0)
    ends = tuple(s - 1 for s in starts[1:]) + (x.shape[0] - 1,)
    first = functools.reduce(jnp.logical_or, [rows == s for s in starts])
    last = functools.reduce(jnp.logical_or, [rows == e for e in ends])
    prev = jnp.where(first, 0.0, _roll_rows(x, 1))
    nxt = jnp.where(last, 0.0, _roll_rows(x, -1))
    return prev * w0 + x * w1 + nxt * w2


def _rms(x):
    return x * lax.rsqrt(jnp.mean(x * x, axis=-1, keepdims=True) + NORM_EPS)


def _heads(x, n):
    return [x[:, h * HEAD_DIM:(h + 1) * HEAD_DIM] for h in range(n)]


_NT = (((1,), (1,)), ((), ()))
_TN = (((0,), (0,)), ((), ()))
_TQ = 256


def _attn_probs(q, k):
    s = lax.dot_general(q, k, _NT, preferred_element_type=F32) * (HEAD_DIM ** -0.5)
    p = jnp.exp(s - jnp.max(s, axis=-1, keepdims=True))
    return p * (1.0 / jnp.sum(p, axis=-1, keepdims=True))


def _attn_fwd_call(q, k, v):
    n, t = q.shape[0], k.shape[0]
    tq = _pick(n, (_TQ, 128))

    def body(q_ref, k_ref, v_ref, o_ref):
        p = _attn_probs(q_ref[...].astype(BF16), k_ref[...].astype(BF16))
        o_ref[...] = jnp.dot(p.astype(BF16), v_ref[...].astype(BF16), preferred_element_type=F32)

    return pl.pallas_call(
        body, name="attn_f", grid=(ATTN_HEADS, n // tq),
        in_specs=[pl.BlockSpec((tq, HEAD_DIM), lambda h, i: (i, h)),
                  pl.BlockSpec((t, HEAD_DIM), lambda h, i: (0, h // ATTN_GROUP)),
                  pl.BlockSpec((t, HEAD_DIM), lambda h, i: (0, h // ATTN_GROUP))],
        out_specs=pl.BlockSpec((tq, HEAD_DIM), lambda h, i: (i, h)),
        out_shape=jax.ShapeDtypeStruct(q.shape, F32),
        compiler_params=pltpu.CompilerParams(dimension_semantics=("parallel", "parallel"),
                                             vmem_limit_bytes=VMEM_LIMIT),
    )(q, k, v)


def _attn_bwd_call(q, k, v, do):
    n, t = q.shape[0], k.shape[0]
    tq = _pick(n, (_TQ, 128))

    def body(q_ref, k_ref, v_ref, do_ref, dq_ref, dk_ref, dv_ref):
        @pl.when((pl.program_id(1) == 0) & (pl.program_id(2) == 0))
        def _():
            dk_ref[...] = jnp.zeros_like(dk_ref)
            dv_ref[...] = jnp.zeros_like(dv_ref)

        qb, kb, vb, dob = (r[...].astype(BF16) for r in (q_ref, k_ref, v_ref, do_ref))
        p = _attn_probs(qb, kb)
        dp = lax.dot_general(dob, vb, _NT, preferred_element_type=F32)
        ds = p * (dp - jnp.sum(p * dp, axis=-1, keepdims=True)) * (HEAD_DIM ** -0.5)
        dsb = ds.astype(BF16)
        dq_ref[...] = jnp.dot(dsb, kb, preferred_element_type=F32)
        dk_ref[...] += lax.dot_general(dsb, qb, _TN, preferred_element_type=F32)
        dv_ref[...] += lax.dot_general(p.astype(BF16), dob, _TN, preferred_element_type=F32)

    q_spec = pl.BlockSpec((tq, HEAD_DIM), lambda kh, g, i: (i, kh * ATTN_GROUP + g))
    kv_spec = pl.BlockSpec((t, HEAD_DIM), lambda kh, g, i: (0, kh))
    return pl.pallas_call(
        body, name="attn_b", grid=(ATTN_KV_HEADS, ATTN_GROUP, n // tq),
        in_specs=[q_spec, kv_spec, kv_spec, q_spec],
        out_specs=[q_spec, kv_spec, kv_spec],
        out_shape=(jax.ShapeDtypeStruct(q.shape, F32), jax.ShapeDtypeStruct(k.shape, F32),
                   jax.ShapeDtypeStruct(v.shape, F32)),
        compiler_params=pltpu.CompilerParams(dimension_semantics=("parallel", "arbitrary", "arbitrary"),
                                             vmem_limit_bytes=VMEM_LIMIT),
    )(q, k, v, do)


@jax.custom_vjp
def attention(q, k, v):
    return _attn_fwd_call(q, k, v)


def _attention_fwd(q, k, v):
    return _attn_fwd_call(q, k, v), (q, k, v)


def _attention_bwd(res, do):
    return _attn_bwd_call(*res, do)


attention.defvjp(_attention_fwd, _attention_bwd)


_C = GDN_CHUNK


def _pdot(a, b):
    return jnp.dot(a, b, precision=lax.Precision.HIGH, preferred_element_type=F32)


@jax.custom_vjp
def _hdot(a, b):
    return jnp.dot(a.astype(BF16), b.astype(BF16), preferred_element_type=F32)


def _hdot_fwd(a, b):
    return _hdot(a, b), (a, b)


def _hdot_bwd(res, g):
    a, b = res
    gb = g.astype(BF16)
    return (lax.dot_general(gb, b.astype(BF16), _NT, preferred_element_type=F32),
            lax.dot_general(a.astype(BF16), gb, _TN, preferred_element_type=F32))


_hdot.defvjp(_hdot_fwd, _hdot_bwd)


def _each(fn, *lists):
    return [fn(*args) for args in zip(*lists)]


def _unit_lower_inverse(low, blockdiag):
    eye = (lax.broadcasted_iota(jnp.int32, (_C, _C), 0) == lax.broadcasted_iota(jnp.int32, (_C, _C), 1)).astype(F32)
    ld = _each(lambda a: a * blockdiag, low)
    lo = _each(lambda a, d: a - d, low, ld)
    l2 = _each(_hdot, ld, ld)
    l4 = _each(_hdot, l2, l2)
    l8 = _each(_hdot, l4, l4)
    td = _each(lambda d, a2: _hdot(eye - d, eye + a2), ld, l2)
    td = _each(lambda t, a4: _hdot(t, eye + a4), td, l4)
    td = _each(lambda t, a8: _hdot(t, eye + a8), td, l8)
    nn = _each(_hdot, td, lo)
    n2 = _each(_hdot, nn, nn)
    out = _each(lambda n, m2: _hdot(eye - n, eye + m2), nn, n2)
    return _each(_hdot, out, td)


def _gdn_chunks(heads, blockdiag):
    q, k, v, b_b, be_b, e_b, kd_b, m1, dec, gl, s = (list(col) for col in zip(*heads))
    f32dot = lambda a, b: jnp.dot(a, b, preferred_element_type=F32)
    nt = lambda a, b: lax.dot_general(a, b, _NT, preferred_element_type=F32)
    kk = _each(nt, k, k)
    t_inv = _unit_lower_inverse(_each(lambda m, a: m * a, m1, kk), blockdiag)
    u = _each(lambda t, b, x: _hdot(t, b * x), t_inv, b_b, v)
    w = _each(lambda t, b, x: _hdot(t, b * x), t_inv, be_b, k)
    delta = _each(lambda uu, ww, ss: uu - f32dot(ww, ss), u, w, s)
    p = _each(lambda d, qq, kx: d * nt(qq, kx), dec, q, k)
    o = _each(lambda qq, e, ss, pp, dd: f32dot(qq * e, ss) + f32dot(pp, dd), q, e_b, s, p, delta)
    s_new = _each(lambda g, ss, kx, kd, dd: g * ss + lax.dot_general(kx * kd, dd, _TN, preferred_element_type=F32),
                  gl, s, k, kd_b, delta)
    return o, s_new


def _blockdiag_mask():
    r = lax.broadcasted_iota(jnp.int32, (_C, _C), 0) >> 4
    c = lax.broadcasted_iota(jnp.int32, (_C, _C), 1) >> 4
    return (r == c).astype(F32)


def _gdn_specs(nc, ncc, reverse, backward):
    def ch(s):
        s = nc - 1 - s if backward else s
        return jnp.where(s < ncc, ncc - 1 - s, nc + ncc - 1 - s) if reverse else s

    tok = pl.BlockSpec((_C, 3 * GDN_W), lambda s: (ch(s), 0))
    park = nc - ncc - 1 if reverse else 0
    out = pl.BlockSpec((_C, GDN_W), lambda s: (jnp.where(ch(s) >= ncc, ch(s) - ncc, park), 0))
    per_tok = pl.BlockSpec((GDN_HEADS, _C, HEAD_DIM), lambda s: (0, ch(s), 0))
    mat = pl.BlockSpec((GDN_HEADS, None, _C, _C), lambda s: (0, ch(s), 0, 0))
    row = pl.BlockSpec((GDN_HEADS, None, 1, HEAD_DIM), lambda s: (0, ch(s), 0, 0))
    state = pl.BlockSpec((GDN_HEADS, None, HEAD_DIM, HEAD_DIM), lambda s: (0, ch(s), 0, 0))
    return tok, out, per_tok, mat, row, state, ch


def _head_cols(h, part):
    return slice((part * GDN_HEADS + h) * HEAD_DIM, (part * GDN_HEADS + h + 1) * HEAD_DIM)


def _gdn_heads(qkv_ref, factor_refs, state_ref, first):
    return [[qkv_ref[:, _head_cols(h, 0)], qkv_ref[:, _head_cols(h, 1)], qkv_ref[:, _head_cols(h, 2)]]
            + [r[h] for r in factor_refs] + [state_ref[first + h]] for h in range(GDN_HEADS)]


def _gdn_fwd_call(ncc, qkv, factors):
    t = qkv.shape[0]
    nc, nd = t // _C, len(factors)
    specs = [_gdn_specs(nc, ncc, d == 1, False) for d in range(nd)]

    def body(*refs):
        ins_d = [refs[8 * d:8 * d + 8] for d in range(nd)]
        outs_d = [refs[8 * nd + 2 * d:8 * nd + 2 * d + 2] for d in range(nd)]
        s_ref = refs[-1]

        @pl.when(pl.program_id(0) == 0)
        def _():
            s_ref[...] = jnp.zeros_like(s_ref)

        heads = sum([_gdn_heads(r[0], r[1:], s_ref, GDN_HEADS * d) for d, r in enumerate(ins_d)], [])
        o, s_new = _gdn_chunks(heads, _blockdiag_mask())
        for d, (o_ref, sall_ref) in enumerate(outs_d):
            for h in range(GDN_HEADS):
                i = GDN_HEADS * d + h
                sall_ref[h] = heads[i][10]
                o_ref[:, _head_cols(h, 0)] = o[i]
                s_ref[i] = s_new[i]

    in_specs, out_specs, out_shape, operands = [], [], [], []
    for (tok, out, per_tok, mat, row, state, _), f in zip(specs, factors):
        in_specs += [tok, per_tok, per_tok, per_tok, per_tok, mat, mat, row]
        out_specs += [out, state]
        out_shape += [jax.ShapeDtypeStruct((t - ncc * _C, GDN_W), F32),
                      jax.ShapeDtypeStruct((GDN_HEADS, nc, HEAD_DIM, HEAD_DIM), F32)]
        operands += [qkv, *f]
    res = pl.pallas_call(
        body, name="gdn_f", grid=(nc,), in_specs=in_specs, out_specs=out_specs, out_shape=out_shape,
        scratch_shapes=[pltpu.VMEM((nd * GDN_HEADS, HEAD_DIM, HEAD_DIM), F32)],
        compiler_params=pltpu.CompilerParams(dimension_semantics=("arbitrary",), vmem_limit_bytes=VMEM_LIMIT),
    )(*operands)
    return [(res[2 * d], res[2 * d + 1]) for d in range(nd)]


def _gdn_bwd_call(ncc, qkv, factors, salls, dos):
    t = qkv.shape[0]
    nc, nd = t // _C, len(factors)
    specs = [_gdn_specs(nc, ncc, d == 1, True) for d in range(nd)]

    def body(*refs):
        ins_d = [refs[10 * d:10 * d + 10] for d in range(nd)]
        outs_d = [refs[10 * nd + 8 * d:10 * nd + 8 * d + 8] for d in range(nd)]
        ds_ref = refs[-1]

        @pl.when(pl.program_id(0) == 0)
        def _():
            ds_ref[...] = jnp.zeros_like(ds_ref)

        bd = _blockdiag_mask()
        heads = sum([_gdn_heads(r[0], r[1:8], r[8], 0) for r in ins_d], [])
        _, vjp = jax.vjp(lambda hs: _gdn_chunks(hs, bd), heads)
        live = [spec[6](pl.program_id(0)) >= ncc for spec in specs]
        (all_grads,) = vjp(([jnp.where(live[d], r[9][:, _head_cols(h, 0)], 0.0)
                             for d, r in enumerate(ins_d) for h in range(GDN_HEADS)],
                            [ds_ref[i] for i in range(nd * GDN_HEADS)]))
        for d, out_refs in enumerate(outs_d):
            for h in range(GDN_HEADS):
                grads = all_grads[GDN_HEADS * d + h]
                for part in range(3):
                    out_refs[0][:, _head_cols(h, part)] = grads[part]
                for ref, g in zip(out_refs[1:], grads[3:10]):
                    ref[h] = g
                ds_ref[GDN_HEADS * d + h] = grads[10]

    shp = lambda a: jax.ShapeDtypeStruct(a.shape, F32)
    in_specs, out_specs, out_shape, operands = [], [], [], []
    for (tok, out, per_tok, mat, row, state, _), f, sall, do in zip(specs, factors, salls, dos):
        in_specs += [tok, per_tok, per_tok, per_tok, per_tok, mat, mat, row, state, out]
        out_specs += [tok, per_tok, per_tok, per_tok, per_tok, mat, mat, row]
        out_shape += [shp(qkv)] + [shp(a) for a in f]
        operands += [qkv, *f, sall, do]
    res = pl.pallas_call(
        body, name="gdn_b", grid=(nc,), in_specs=in_specs, out_specs=out_specs, out_shape=out_shape,
        scratch_shapes=[pltpu.VMEM((nd * GDN_HEADS, HEAD_DIM, HEAD_DIM), F32)],
        compiler_params=pltpu.CompilerParams(dimension_semantics=("arbitrary",), vmem_limit_bytes=VMEM_LIMIT),
    )(*operands)
    return [res[8 * d:8 * d + 8] for d in range(nd)]


@functools.partial(jax.custom_vjp, nondiff_argnums=(0,))
def gdn_scan(ncc, qkv, f_fwd, f_rev):
    (o0, _), (o1, _) = _gdn_fwd_call(ncc, qkv, [f_fwd, f_rev])
    return o0, o1


def _gdn_scan_fwd(ncc, qkv, f_fwd, f_rev):
    (o0, s0), (o1, s1) = _gdn_fwd_call(ncc, qkv, [f_fwd, f_rev])
    return (o0, o1), (qkv, f_fwd, f_rev, s0, s1)


def _gdn_scan_bwd(ncc, res, dos):
    qkv, f_fwd, f_rev, s0, s1 = res
    g0, g1 = _gdn_bwd_call(ncc, qkv, [f_fwd, f_rev], [s0, s1], list(dos))
    return g0[0] + g1[0], tuple(g0[1:]), tuple(g1[1:])


gdn_scan.defvjp(_gdn_scan_fwd, _gdn_scan_bwd)


def _rope_tables(n, cl):
    t = np.arange(n)
    inv_freq = (ROPE_THETA ** (-np.arange(0, HEAD_DIM // 2, 2, dtype=np.float32) / (HEAD_DIM // 2))).astype(np.float32)
    ang_r = (t // GRID_W).astype(np.float32)[:, None] * inv_freq
    ang_c = (t % GRID_W).astype(np.float32)[:, None] * inv_freq
    cos = np.concatenate([np.cos(ang_r), np.cos(ang_r), np.cos(ang_c), np.cos(ang_c)], axis=1)
    sin = np.concatenate([-np.sin(ang_r), np.sin(ang_r), -np.sin(ang_c), np.sin(ang_c)], axis=1)
    cos_all = np.concatenate([np.ones((cl, HEAD_DIM), np.float32), cos], axis=0)
    sin_all = np.concatenate([np.zeros((cl, HEAD_DIM), np.float32), sin], axis=0)
    j = np.arange(HEAD_DIM)
    src = np.where((j % 64) < 32, j + 32, j - 32)
    perm = np.zeros((HEAD_DIM, HEAD_DIM), np.float32)
    perm[src, j] = 1.0
    return (jnp.asarray(cos.astype(np.float32)), jnp.asarray(sin.astype(np.float32)),
            jnp.asarray(cos_all), jnp.asarray(sin_all), jnp.asarray(perm))


def _gdn_factors(log_a, beta, reverse):
    t = log_a.shape[0]
    nc = t // _C
    la = log_a.reshape(nc, _C, GDN_HEADS).transpose(2, 0, 1)
    be = beta.reshape(nc, _C, GDN_HEADS).transpose(2, 0, 1)
    gam = lax.cumsum(la, axis=2, reverse=reverse)
    idx = jnp.arange(_C)
    incl = (idx[:, None] <= idx[None, :]) if reverse else (idx[:, None] >= idx[None, :])
    strict = (idx[:, None] < idx[None, :]) if reverse else (idx[:, None] > idx[None, :])
    dec = jnp.exp(jnp.where(incl, gam[..., :, None] - gam[..., None, :], -jnp.inf))
    m1 = jnp.where(strict, be[..., :, None] * dec, 0.0)
    e = jnp.exp(gam)
    g_last = gam[..., :1] if reverse else gam[..., -1:]
    lanes = lambda a: jnp.broadcast_to(a.reshape(GDN_HEADS, t, 1), (GDN_HEADS, t, HEAD_DIM))
    gl = jnp.broadcast_to(jnp.exp(g_last)[..., None], (GDN_HEADS, nc, 1, HEAD_DIM))
    return lanes(be), lanes(be * e), lanes(e), lanes(jnp.exp(g_last - gam)), m1, dec, gl


def local_loss(x, wz, wb, ws, c, ctx, target):
    return channel_mixing(token_mixing(x, wz, wb, ws, c, ctx), wz, wb, ws, target)


def token_mixing(x, wz, wb, ws, c, ctx):
    n, cl = x.shape[0], ctx.shape[0]
    cos_q, sin_q, cos_k, sin_k, perm = _rope_tables(n, cl)

    sc_in = jnp.concatenate([jax.nn.silu(c), jax.nn.silu(ws["c_ctx"])[None, :], jnp.zeros((14, D_MODEL), F32)], axis=0)
    mod = pmm(sc_in, wb["w_mod"], wz["w_mod"], "mm_mod") + ws["b_mod"]
    sh1, sc1, g1, sh2, sc2, g2 = [mod[0:1, i * D_MODEL:(i + 1) * D_MODEL] for i in range(6)]
    csh1, csc1 = mod[1:2, 0:D_MODEL], mod[1:2, D_MODEL:2 * D_MODEL]

    def norm_mod(a, sh, sc):
        return (_rms(a) * (1.0 + sc) + sh,)

    (hx,) = rowop(norm_mod, "normmod_x", (x,), (sh1, sc1))
    (hc,) = rowop(norm_mod, "normmod_c", (ctx,), (csh1, csc1))
    h_all = jnp.concatenate([hc, hx], axis=0)
    p_main = pmm_t(h_all, wb["w_in_main"], wz["w_in_main"], "mm_in")
    p_small = pmm_t(h_all, wb["w_in_small"], wz["w_in_small"], "mm_ins")
    ak, av, qkv, aq, z, gate = jnp.split(p_main, [KV_W, 2 * KV_W, SMALL_AT, SMALL_AT + Q_W, SMALL_AT + Q_W + GDN_W],
                                         axis=1)
    db, da = p_small[:, :2 * GDN_HEADS], p_small[:, 2 * GDN_HEADS:4 * GDN_HEADS]

    def qk_prep(nh):
        def fn(a, w, cos, sin, pm):
            outs = []
            for ah in _heads(a, nh):
                y = _rms(ah) * w
                outs.append(y * cos + _pdot(y, pm) * sin)
            return (jnp.concatenate(outs, axis=1),)
        return fn

    (q_x,) = rowop(qk_prep(ATTN_HEADS), "q_prep", (aq[cl:],), (ws["q_norm_w"],), (cos_q, sin_q), (perm,))
    (k_all,) = rowop(qk_prep(ATTN_KV_HEADS), "k_prep", (ak,), (ws["k_norm_w"],), (cos_k, sin_k), (perm,))
    attn_x = attention(q_x, k_all, av)

    cw = ws["conv_qkv_w"]
    normed = jnp.asarray(np.repeat([1.0, 1.0, 0.0], GDN_W)[None, :], F32)
    scale = jnp.asarray(np.repeat([HEAD_DIM ** -0.5, 1.0, 1.0], GDN_W)[None, :], F32)

    def gdn_prep(a, w0, w1, w2, nf, sc):
        s = jax.nn.silu(_conv3(a, w0, w1, w2, (0, cl)))
        inv = lax.rsqrt(jnp.sum(s * s, axis=-1, keepdims=True) + NORM_EPS)
        return (s * jnp.where(nf > 0.0, inv * sc, 1.0),)

    (qkvn,) = colop(gdn_prep, "gdn_prep", (qkv, cw[0:1], cw[1:2], cw[2:3], normed, scale),
                    [(i, 0) for i in range(6)], 2, 3 * GDN_HEADS)
    beta = jax.nn.sigmoid(db).reshape(-1, 2, GDN_HEADS)
    log_a = -jnp.exp(ws["a_log"])[None] * jax.nn.softplus(da.reshape(-1, 2, GDN_HEADS) + ws["dt_bias"][None])
    o_fwd, o_rev = gdn_scan(cl // _C, qkvn, _gdn_factors(log_a[:, 0], beta[:, 0], False),
                            _gdn_factors(log_a[:, 1], beta[:, 1], True))
    o_x = o_fwd + o_rev

    def gdn_out(o, zz, w):
        outs = [_rms(oh) * w * jax.nn.silu(zh) for oh, zh in zip(_heads(o, GDN_HEADS), _heads(zz, GDN_HEADS))]
        return (jnp.concatenate(outs, axis=1),)

    (gdn_x,) = rowop(gdn_out, "gdn_out", (o_x, z[cl:]), (ws["gdn_norm_w"],))
    return dict(x=x, attn_x=attn_x, gdn_x=gdn_x, gate=gate[cl:], g1=g1, sh2=sh2, sc2=sc2, g2=g2)


def channel_mixing(mixed, wz, wb, ws, target):
    x, attn_x, gdn_x, gate = mixed["x"], mixed["attn_x"], mixed["gdn_x"], mixed["gate"]
    g1, sh2, sc2, g2 = mixed["g1"], mixed["sh2"], mixed["sc2"], mixed["g2"]
    pa = pmm(attn_x, wb["w_pa"], wz["w_pa"], "mm_pa")
    pd = pmm(gdn_x, wb["w_pd"], wz["w_pd"], "mm_pd")

    def merge(a, d, g):
        return (jax.nn.sigmoid(g[:, :D_MODEL]) * a + jax.nn.sigmoid(g[:, D_MODEL:]) * d,)

    (y,) = rowop(merge, "merge", (pa, pd, gate))
    mo = pmm(y, wb["w_out"], wz["w_out"], "mm_out")

    def res_norm_mod(xx, m, g, sh, sc):
        x1 = xx + g * m
        return x1, _rms(x1) * (1.0 + sc) + sh

    x1, h2 = rowop(res_norm_mod, "res1", (x, mo), (g1, sh2, sc2))
    up = pmm(h2, wb["w_up"], wz["w_up"], "mm_up")
    fw = ws["ffn_conv_w"]

    def ffn_act(ug, uv, w0g, w0v, w1g, w1v, w2g, w2v, bg, bv):
        g = _conv3(ug, w0g, w1g, w2g, (0,)) + bg
        v = _conv3(uv, w0v, w1v, w2v, (0,)) + bv
        return (jax.nn.silu(g) * v,)

    half = D_FF // HEAD_DIM
    (act,) = colop(ffn_act, "ffn_act", (up, fw[0:1], fw[1:2], fw[2:3], ws["ffn_conv_b"]),
                   [(i, off) for i in range(5) for off in (0, half)], 0, half)
    dn = pmm(act, wb["w_down"], wz["w_down"], "mm_down")

    def head(xx, m, g, w, tgt):
        yy = _rms(xx + g * m) * w
        err = (yy - tgt) ** 2
        return (jnp.broadcast_to(0.5 * jnp.mean(err, axis=-1, keepdims=True), (xx.shape[0], HEAD_DIM)),)

    (row_loss,) = rowop(head, "head", (x1, dn), (g2, ws["final_norm_w"][None, :]), (target,))
    return jnp.sum(row_loss[:, 0])


_HBM = pl.BlockSpec(memory_space=pltpu.HBM)


def _chip_peers():
    x, y = lax.axis_index("x"), lax.axis_index("y")
    return [(1 - x, y), (x, 1 - y), (1 - x, 1 - y)]


_SPLIT_COLS = ("w_in",)


def _half_of(view, nm, idx, lead=0):
    r, cdim = view.shape[-2:]
    pre = (slice(None),) * lead
    if nm in _SPLIT_COLS:
        return view.at[pre + (slice(None), pl.ds(pl.multiple_of(idx * (cdim // 2), 128), cdim // 2))]
    return view.at[pre + (pl.ds(pl.multiple_of(idx * (r // 2), 16), r // 2), slice(None))]


def _remote(src, dst, send_sem, recv_sem, dev):
    return pltpu.make_async_remote_copy(src_ref=src, dst_ref=dst, send_sem=send_sem, recv_sem=recv_sem,
                                        device_id=dev, device_id_type=MESH)


def _hbm_call(body, name, ins, out_shape, n_sems, in_place=False):
    names = tuple(ins)
    return dict(zip(names, pl.pallas_call(
        body, name=name, in_specs=[_HBM] * len(names), out_specs=[_HBM] * len(names),
        out_shape=[out_shape(nm, ins[nm]) for nm in names],
        scratch_shapes=[pltpu.SemaphoreType.DMA((k,)) for k in n_sems],
        input_output_aliases={i: i for i in range(len(names))} if in_place else {},
    )(*[ins[nm] for nm in names])))


def all_gather_chips(shards):
    names = tuple(shards)
    n = len(names)

    def body(*refs):
        ins, outs = dict(zip(names, refs[:n])), dict(zip(names, refs[n:2 * n]))
        ici_send, ici_recv, d2d_send, d2d_recv, own_send, own_recv = refs[2 * n:]
        x, y, c = lax.axis_index("x"), lax.axis_index("y"), lax.axis_index("c")
        me, sib = 2 * x + y, (x, y, 1 - c)
        own = [_remote(ins[nm], outs[nm].at[me], own_send.at[i], own_recv.at[i], sib) for i, nm in enumerate(names)]
        for cp in own:
            cp.start()
        sends = []
        for k, (px, py) in enumerate(_chip_peers()):
            for i, nm in enumerate(names):
                cp = _remote(_half_of(ins[nm], nm, c), _half_of(outs[nm].at[me], nm, c), ici_send.at[k * n + i],
                             ici_recv.at[k * n + i], (px, py, c))
                cp.start()
                sends.append(cp)
        for k, (px, py) in enumerate(_chip_peers()):
            for i, nm in enumerate(names):
                landed = _half_of(outs[nm].at[2 * px + py], nm, c)
                _remote(landed, landed, ici_send.at[k * n + i], ici_recv.at[k * n + i], (px, py, c)).wait_recv()
                fw = _remote(landed, landed, d2d_send.at[k * n + i], d2d_recv.at[k * n + i], sib)
                fw.start()
                sends.append(fw)
        for k, (px, py) in enumerate(_chip_peers()):
            for i, nm in enumerate(names):
                other = _half_of(outs[nm].at[2 * px + py], nm, 1 - c)
                _remote(other, other, d2d_send.at[k * n + i], d2d_recv.at[k * n + i], sib).wait_recv()
        for cp in sends:
            cp.wait_send()
        for cp in own:
            cp.wait()

    return _hbm_call(body, "ag_weights", shards, lambda nm, a: jax.ShapeDtypeStruct((N_CHIPS,) + a.shape, a.dtype),
                     (3 * n, 3 * n, 3 * n, 3 * n, n, n))


_SEM = pl.BlockSpec(memory_space=pltpu.SEMAPHORE)


def push_start(name, arrays, land_shapes, copies, n_copies):
    names = tuple(arrays)
    n = len(names)

    def body(*refs):
        send_sems, recv_sems, token = refs[2 * n], refs[2 * n + 1], refs[-1]
        for j, (src, dst, dev) in enumerate(copies(refs[:n], refs[n:2 * n])):
            _remote(src, dst, send_sems.at[j], recv_sems.at[j], dev).start()
        token[...] = jnp.zeros_like(token)

    hbm = lambda a: pltpu.with_memory_space_constraint(a, pltpu.HBM)
    lands = [lax.empty(land_shapes[nm], arrays[nm].dtype) for nm in names]
    res = pl.pallas_call(
        body, name=name,
        out_shape=(pltpu.SemaphoreType.DMA((n_copies,)), pltpu.SemaphoreType.DMA((n_copies,)),
                   *[pltpu.HBM(arrays[nm].shape, arrays[nm].dtype) for nm in names],
                   *[pltpu.HBM(a.shape, a.dtype) for a in lands], jax.ShapeDtypeStruct((8, 128), F32)),
        in_specs=[_HBM] * (2 * n),
        out_specs=(_SEM, _SEM, *[_HBM] * (2 * n), pl.BlockSpec(memory_space=pltpu.VMEM)),
        input_output_aliases={i: 2 + i for i in range(2 * n)},
        compiler_params=pltpu.CompilerParams(has_side_effects=pltpu.SideEffectType.DATAFLOW_SIDE_EFFECTING),
    )(*[hbm(arrays[nm]) for nm in names], *[hbm(a) for a in lands])
    return names, res[0], res[1], res[2:2 + n], res[2 + n:2 + 2 * n], res[-1]


def push_wait(name, started, copies, after):
    names, send_sems, recv_sems, srcs, lands, _ = started
    n = len(names)

    def body(*refs):
        send_ref, recv_ref = refs[2 * n], refs[2 * n + 1]
        for j, (src, dst, dev) in enumerate(copies(refs[:n], refs[n:2 * n])):
            cp = _remote(src, dst, send_ref.at[j], recv_ref.at[j], dev)
            cp.wait_send()
            cp.wait_recv()

    res = pl.pallas_call(
        body, name=name,
        out_shape=(*[pltpu.HBM(a.shape, a.dtype) for a in srcs], *[pltpu.HBM(a.shape, a.dtype) for a in lands]),
        in_specs=[_HBM] * (2 * n) + [_SEM, _SEM, pl.BlockSpec(memory_space=pl.ANY)],
        out_specs=tuple([_HBM] * (2 * n)),
        input_output_aliases={i: i for i in range(2 * n)},
        compiler_params=pltpu.CompilerParams(has_side_effects=pltpu.SideEffectType.DATAFLOW_SIDE_EFFECTING),
    )(*srcs, *lands, send_sems, recv_sems, after)
    return dict(zip(names, res[n:]))


def _gather_copies(srcs, lands):
    x, y, c = lax.axis_index("x"), lax.axis_index("y"), lax.axis_index("c")
    devs = [(px, py, c) for px, py in _chip_peers()] + [(x, y, 1 - c)]
    return [(src, land.at[2 * x + y], dev) for src, land in zip(srcs, lands) for dev in devs]


def _scatter_copies(srcs, lands):
    c = lax.axis_index("c")
    return [(src.at[2 * px + py], land.at[k], (px, py, c))
            for src, land in zip(srcs, lands) for k, (px, py) in enumerate(_chip_peers())]


def sibling_halves(blocks, name):
    names = tuple(blocks)

    def body(*refs):
        n = len(names)
        ins, outs = dict(zip(names, refs[:n])), dict(zip(names, refs[n:2 * n]))
        send_sems, recv_sems = refs[2 * n:]
        x, y, c = lax.axis_index("x"), lax.axis_index("y"), lax.axis_index("c")
        cps = [_remote(_half_of(ins[nm], nm, 1 - c, lead=1), outs[nm], send_sems.at[i], recv_sems.at[i], (x, y, 1 - c))
               for i, nm in enumerate(names)]
        for cp in cps:
            cp.start()
        for cp in cps:
            cp.wait()

    def half_shape(nm, a):
        r, cdim = a.shape[-2:]
        return jax.ShapeDtypeStruct((N_CHIPS, r, cdim // 2) if nm in _SPLIT_COLS else (N_CHIPS, r // 2, cdim), a.dtype)

    return _hbm_call(body, name, blocks, half_shape, (len(names), len(names)))


def scatter_halves(blocks):
    names = tuple(blocks)
    n = len(names)

    def body(*refs):
        ins, outs = dict(zip(names, refs[:n])), dict(zip(names, refs[n:2 * n]))
        send_sems, recv_sems = refs[2 * n:]
        c = lax.axis_index("c")
        cps = [_remote(ins[nm].at[2 * px + py], outs[nm].at[k], send_sems.at[k * n + i], recv_sems.at[k * n + i],
                       (px, py, c))
               for k, (px, py) in enumerate(_chip_peers()) for i, nm in enumerate(names)]
        for cp in cps:
            cp.start()
        for cp in cps:
            cp.wait_recv()
        for cp in cps:
            cp.wait_send()

    return _hbm_call(body, "rs_grads", blocks, lambda nm, a: jax.ShapeDtypeStruct((3,) + a.shape[1:], a.dtype),
                     (3 * n, 3 * n))


def sibling_assemble(arrays):
    names = tuple(arrays)

    def body(*refs):
        n = len(names)
        ins, outs = dict(zip(names, refs[:n])), dict(zip(names, refs[n:2 * n]))
        send_sems, recv_sems = refs[2 * n:]
        x, y, c = lax.axis_index("x"), lax.axis_index("y"), lax.axis_index("c")
        cps = [_remote(_half_of(ins[nm], nm, c), _half_of(outs[nm], nm, c), send_sems.at[i], recv_sems.at[i],
                       (x, y, 1 - c)) for i, nm in enumerate(names)]
        for cp in cps:
            cp.start()
        for i, nm in enumerate(names):
            other = _half_of(outs[nm], nm, 1 - c)
            _remote(other, other, send_sems.at[i], recv_sems.at[i], (x, y, 1 - c)).wait_recv()
        for cp in cps:
            cp.wait_send()

    return _hbm_call(body, "rs_assemble", arrays, lambda nm, a: jax.ShapeDtypeStruct(a.shape, a.dtype),
                     (len(names), len(names)), in_place=True)


def all_reduce_small(v):
    def body(v_ref, tot_ref, gath_ref, send_sems, recv_sems):
        x, y, c = lax.axis_index("x"), lax.axis_index("y"), lax.axis_index("c")
        me = 4 * x + 2 * y + c
        gath_ref[me] = v_ref[...]

        def peer(k):
            m = k + 1
            return (x ^ (m >> 2 & 1), y ^ (m >> 1 & 1), c ^ (m & 1))

        sends = [pltpu.make_async_remote_copy(src_ref=v_ref, dst_ref=gath_ref.at[me], send_sem=send_sems.at[k],
                                              recv_sem=recv_sems.at[k], device_id=peer(k), device_id_type=MESH)
                 for k in range(N_DEV - 1)]
        for cp in sends:
            cp.start()
        for k in range(N_DEV - 1):
            px, py, pc = peer(k)
            pltpu.make_async_remote_copy(src_ref=v_ref, dst_ref=gath_ref.at[4 * px + 2 * py + pc],
                                         send_sem=send_sems.at[k], recv_sem=recv_sems.at[k], device_id=peer(k),
                                         device_id_type=MESH).wait_recv()
        for cp in sends:
            cp.wait_send()
        acc = gath_ref[0]
        for d in range(1, N_DEV):
            acc = acc + gath_ref[d]
        tot_ref[...] = acc

    vm = pl.BlockSpec(memory_space=pltpu.VMEM)
    return pl.pallas_call(
        body, name="ar_small", in_specs=[vm], out_specs=[vm, vm],
        out_shape=(jax.ShapeDtypeStruct(v.shape, v.dtype), jax.ShapeDtypeStruct((N_DEV,) + v.shape, v.dtype)),
        scratch_shapes=[pltpu.SemaphoreType.DMA((N_DEV - 1,)), pltpu.SemaphoreType.DMA((N_DEV - 1,))],
    )(v)[0]


def _elementwise(fn, name, ins, n_out, out_dtype=F32):
    r, cdim = ins[0].shape
    tr = _pick(r, tuple(p for p in (488, 256, 128, 104, 64, 32, 16, 8) if p * cdim * 4 <= 2 * 1024 * 1024))
    spec = pl.BlockSpec((tr, cdim), lambda i: (i, 0))

    def body(*refs):
        res = fn(*[ref[...] for ref in refs[:len(ins)]])
        for o_ref, v in zip(refs[len(ins):], res):
            o_ref[...] = v

    return pl.pallas_call(
        body, name=name, grid=(r // tr,), in_specs=[spec] * len(ins), out_specs=[spec] * n_out,
        out_shape=tuple(jax.ShapeDtypeStruct((r, cdim), out_dtype) for _ in range(n_out)),
        compiler_params=pltpu.CompilerParams(dimension_semantics=("parallel",), vmem_limit_bytes=VMEM_LIMIT),
    )(*ins)


def _half_block_specs(nm, shard_shape):
    r, cdim = shard_shape
    if nm in _SPLIT_COLS:
        return (None, r, cdim // 2), (lambda j, c: (j, 0, c))
    return (None, r // 2, cdim), (lambda j, c: (j, c, 0))


def _presum(nm, sel, g32, a):
    blk, at = _half_block_specs(nm, g32.shape[1:])

    def body(s_ref, g_ref, a_ref, o_ref):
        del s_ref
        o_ref[...] = (g_ref[...] + a_ref[...]).astype(BF16)

    return pl.pallas_call(
        body, name="rs_presum_" + nm,
        grid_spec=pltpu.PrefetchScalarGridSpec(
            num_scalar_prefetch=1, grid=(N_CHIPS,),
            in_specs=[pl.BlockSpec(blk, lambda j, s: at(j, s[0])), pl.BlockSpec(blk, lambda j, s: (j, 0, 0))],
            out_specs=pl.BlockSpec(blk, lambda j, s: (j, 0, 0))),
        out_shape=jax.ShapeDtypeStruct(a.shape, BF16),
        compiler_params=pltpu.CompilerParams(dimension_semantics=("parallel",), vmem_limit_bytes=VMEM_LIMIT),
    )(sel, g32, a)


def _finalsum(nm, sel, g32, a, got):
    blk, at = _half_block_specs(nm, g32.shape[1:])

    def body(s_ref, g_ref, a_ref, r_ref, o_ref):
        del s_ref
        acc = g_ref[...] + a_ref[...]
        for k in range(3):
            acc = acc + r_ref[k].astype(F32)
        o_ref[...] = acc

    return pl.pallas_call(
        body, name="rs_final_" + nm,
        grid_spec=pltpu.PrefetchScalarGridSpec(
            num_scalar_prefetch=1, grid=(1,),
            in_specs=[pl.BlockSpec(blk, lambda i, s: at(s[1], s[0])), pl.BlockSpec(blk, lambda i, s: (s[1], 0, 0)),
                      pl.BlockSpec(got.shape, lambda i, s: (0, 0, 0))],
            out_specs=pl.BlockSpec(blk[1:], lambda i, s: at(0, s[0])[1:])),
        out_shape=jax.ShapeDtypeStruct(g32.shape[1:], F32),
        compiler_params=pltpu.CompilerParams(dimension_semantics=("arbitrary",), vmem_limit_bytes=VMEM_LIMIT),
    )(sel, g32, a, got)


def _adamw(w, g, m, v, name):
    shape = w.shape
    to2 = lambda a: a.reshape(-1, shape[-1])

    def fn(w_, g_, m_, v_):
        m_new = ADAM_B1 * m_ + (1.0 - ADAM_B1) * g_
        v_new = ADAM_B2 * v_ + (1.0 - ADAM_B2) * (g_ * g_)
        m_hat = m_new / (1.0 - ADAM_B1 ** ADAM_STEP)
        v_hat = v_new / (1.0 - ADAM_B2 ** ADAM_STEP)
        delta = -ADAM_LR * (m_hat / (jnp.sqrt(v_hat) + ADAM_EPS) + ADAM_WD * w_)
        return delta, m_new, v_new

    outs = _elementwise(fn, name, [to2(a) for a in (w, g, m, v)], 3)
    return tuple(o.reshape(shape) for o in outs)


_BIG = ("w_mod", "w_in", "w_pa", "w_pd", "w_out", "w_up", "w_down")
_EARLY = ("w_mod", "w_in")
_LATE = ("w_pa", "w_pd", "w_out", "w_up", "w_down")
_COL_SHARDED = ("w_mod", "w_up")
_FULL_SHAPE = {"w_mod": (D_MODEL, MOD_W), "w_in": (IN_COLS, D_MODEL), "w_pa": (Q_W, D_MODEL), "w_pd": (GDN_W, D_MODEL),
               "w_out": (D_MODEL, D_MODEL), "w_up": (D_MODEL, 2 * D_FF), "w_down": (D_FF, D_MODEL)}


def _shard_shape(name):
    r, cdim = _FULL_SHAPE[name]
    return (r, cdim // N_CHIPS) if name in _COL_SHARDED else (r // N_CHIPS, cdim)


_CONV_ELEMS = 2 * (3 * CONV_W // N_CHIPS + 3 * 2 * D_FF // N_CHIPS)
_CONV_ROWS = 32


def _blocks_of_full(name, full):
    r, cdim = _FULL_SHAPE[name]
    if name in _COL_SHARDED:
        return full.reshape(r, N_CHIPS, cdim // N_CHIPS).transpose(1, 0, 2)
    return full.reshape(N_CHIPS, r // N_CHIPS, cdim)


def _full_of_blocks(name, blocks):
    r, cdim = _FULL_SHAPE[name]
    if name in _COL_SHARDED:
        return blocks.transpose(1, 0, 2).reshape(r, cdim)
    return blocks.reshape(r, cdim)


def _w_in_regroup(w_in_t):
    main = jnp.concatenate([w_in_t[:SMALL_AT], w_in_t[SMALL_AT + 4 * GDN_HEADS:]], axis=0)
    small = jnp.pad(w_in_t[SMALL_AT:SMALL_AT + 4 * GDN_HEADS], ((0, HEAD_DIM - 4 * GDN_HEADS), (0, 0)))
    return main, small


def _w_in_ungroup(main, small):
    return jnp.concatenate([main[:SMALL_AT], small[:4 * GDN_HEADS], main[SMALL_AT:]], axis=0)


_SMALL = ("c_ctx", "b_mod", "q_norm_w", "k_norm_w", "conv_qkv_w", "a_log", "dt_bias", "gdn_norm_w", "ffn_conv_w",
          "ffn_conv_b", "final_norm_w")


def _pack_small(tree, rows):
    flat = jnp.concatenate([tree[nm].reshape(-1) for nm in _SMALL])
    return jnp.pad(flat, (0, rows * 128 - flat.shape[0])).reshape(rows, 128)


def _unpack_small(packed, like):
    flat, out, off = packed.reshape(-1), {}, 0
    for nm in _SMALL:
        size = int(np.prod(like[nm].shape))
        out[nm] = flat[off:off + size].reshape(like[nm].shape)
        off += size
    return out


def kernel(x, c, ctx, c_ctx, w_mod, b_mod, w_in, q_norm_w, k_norm_w, conv_qkv_w, a_log, dt_bias, gdn_norm_w, w_pa, w_pd, w_out, w_up, ffn_conv_w, ffn_conv_b, w_down, final_norm_w, loss_target, m_c_ctx, m_w_mod, m_b_mod, m_w_in, m_q_norm_w, m_k_norm_w, m_conv_qkv_w, m_a_log, m_dt_bias, m_gdn_norm_w, m_w_pa, m_w_pd, m_w_out, m_w_up, m_ffn_conv_w, m_ffn_conv_b, m_w_down, m_final_norm_w, v_c_ctx, v_w_mod, v_b_mod, v_w_in, v_q_norm_w, v_k_norm_w, v_conv_qkv_w, v_a_log, v_dt_bias, v_gdn_norm_w, v_w_pa, v_w_pd, v_w_out, v_w_up, v_ffn_conv_w, v_ffn_conv_b, v_w_down, v_final_norm_w):
    names = ("c_ctx", "w_mod", "b_mod", "w_in", "q_norm_w", "k_norm_w", "conv_qkv_w", "a_log", "dt_bias", "gdn_norm_w",
             "w_pa", "w_pd", "w_out", "w_up", "ffn_conv_w", "ffn_conv_b", "w_down", "final_norm_w")
    w_sh = dict(c_ctx=c_ctx, w_mod=w_mod, b_mod=b_mod, w_in=w_in, q_norm_w=q_norm_w, k_norm_w=k_norm_w,
                conv_qkv_w=conv_qkv_w, a_log=a_log, dt_bias=dt_bias, gdn_norm_w=gdn_norm_w, w_pa=w_pa, w_pd=w_pd,
                w_out=w_out, w_up=w_up, ffn_conv_w=ffn_conv_w, ffn_conv_b=ffn_conv_b, w_down=w_down,
                final_norm_w=final_norm_w)
    m_sh = dict(c_ctx=m_c_ctx, w_mod=m_w_mod, b_mod=m_b_mod, w_in=m_w_in, q_norm_w=m_q_norm_w, k_norm_w=m_k_norm_w,
                conv_qkv_w=m_conv_qkv_w, a_log=m_a_log, dt_bias=m_dt_bias, gdn_norm_w=m_gdn_norm_w, w_pa=m_w_pa,
                w_pd=m_w_pd, w_out=m_w_out, w_up=m_w_up, ffn_conv_w=m_ffn_conv_w, ffn_conv_b=m_ffn_conv_b,
                w_down=m_w_down, final_norm_w=m_final_norm_w)
    v_sh = dict(c_ctx=v_c_ctx, w_mod=v_w_mod, b_mod=v_b_mod, w_in=v_w_in, q_norm_w=v_q_norm_w, k_norm_w=v_k_norm_w,
                conv_qkv_w=v_conv_qkv_w, a_log=v_a_log, dt_bias=v_dt_bias, gdn_norm_w=v_gdn_norm_w, w_pa=v_w_pa,
                w_pd=v_w_pd, w_out=v_w_out, w_up=v_w_up, ffn_conv_w=v_ffn_conv_w, ffn_conv_b=v_ffn_conv_b,
                w_down=v_w_down, final_norm_w=v_final_norm_w)
    chip = 2 * lax.axis_index("x") + lax.axis_index("y")

    conv_bits = jnp.concatenate([lax.bitcast_convert_type(w_sh[nm][0], BF16).reshape(-1)
                                 for nm in ("conv_qkv_w", "ffn_conv_w")])
    shards = {nm: w_sh[nm][0].astype(BF16).T if nm == "w_in" else w_sh[nm][0].astype(BF16) for nm in _BIG}
    shards["conv"] = jnp.pad(conv_bits, (0, _CONV_ROWS * D_MODEL - _CONV_ELEMS)).reshape(_CONV_ROWS, D_MODEL)
    gathered = all_gather_chips({nm: shards[nm] for nm in _EARLY + ("conv",)})
    gathered, late_shards = lax.optimization_barrier((gathered, {nm: shards[nm] for nm in _LATE}))
    started = push_start("ag_late_start", late_shards, {nm: (N_CHIPS,) + a.shape for nm, a in late_shards.items()},
                         _gather_copies, 4 * len(_LATE))
    c = c + started[-1][0:1, 0:1]

    wb = {nm: _full_of_blocks(nm, gathered[nm]) for nm in _EARLY}
    wb["w_in_main"], wb["w_in_small"] = _w_in_regroup(wb.pop("w_in"))
    conv_all = gathered["conv"].reshape(N_CHIPS, -1)[:, :_CONV_ELEMS]
    n_cq = 2 * 3 * CONV_W // N_CHIPS
    unbits = lambda a, w: lax.bitcast_convert_type(a.reshape(N_CHIPS, 3, w // N_CHIPS, 2), F32).transpose(1, 0, 2).reshape(3, w)
    ws = dict(c_ctx=c_ctx, b_mod=b_mod, q_norm_w=q_norm_w, k_norm_w=k_norm_w, a_log=a_log[0], dt_bias=dt_bias[0],
              gdn_norm_w=gdn_norm_w, ffn_conv_b=ffn_conv_b, final_norm_w=final_norm_w,
              conv_qkv_w=unbits(conv_all[:, :n_cq], CONV_W), ffn_conv_w=unbits(conv_all[:, n_cq:], 2 * D_FF))
    wz = {nm: jnp.zeros(a.shape, F32) for nm, a in wb.items()}
    wz.update({nm: jnp.zeros(_FULL_SHAPE[nm], F32) for nm in _LATE})

    mixed, vjp_mix = jax.vjp(lambda x_, wz_, ws_: token_mixing(x_, wz_, wb, ws_, c, ctx[0]), x[0], wz, ws)
    got = push_wait("ag_late_wait", started, _gather_copies, mixed["gdn_x"])
    wb_late = {nm: _full_of_blocks(nm, got[nm]) for nm in _LATE}
    loss_local, vjp_chan = jax.vjp(
        lambda mixed_, wz_, ws_: channel_mixing(mixed_, wz_, wb_late, ws_, loss_target[0]), mixed, wz, ws)
    d_mixed, gz_chan, gs_chan = vjp_chan(jnp.ones((), F32))

    sel = jnp.stack([lax.axis_index("c"), chip]).astype(jnp.int32)
    g32_late = {nm: _blocks_of_full(nm, gz_chan[nm]) for nm in _LATE}
    theirs_late = sibling_halves(g32_late, "rs_sibling_late")
    sums_late = {nm: _presum(nm, sel, g32_late[nm], theirs_late[nm]) for nm in _LATE}
    scattering = push_start("rs_late_start", sums_late, {nm: (3,) + a.shape[1:] for nm, a in sums_late.items()},
                            _scatter_copies, 3 * len(_LATE))
    d_mixed, _ = lax.optimization_barrier((d_mixed, scattering[-1]))
    gx, gz_mix, gs_mix = vjp_mix(d_mixed)
    gs = jax.tree.map(jnp.add, gs_mix, gs_chan)
    got_late = push_wait("rs_late_wait", scattering, _scatter_copies, gx)
    loss = lax.psum(loss_local, ("x", "y", "c"))

    gz_mix["w_in"] = _w_in_ungroup(gz_mix.pop("w_in_main"), gz_mix.pop("w_in_small"))
    g32 = {nm: _blocks_of_full(nm, gz_mix[nm]) for nm in _EARLY}
    theirs = sibling_halves(g32, "rs_sibling")
    got = scatter_halves({nm: _presum(nm, sel, g32[nm], theirs[nm]) for nm in _EARLY})
    g32.update(g32_late), theirs.update(theirs_late), got.update(got_late)
    g_big = sibling_assemble({nm: _finalsum(nm, sel, g32[nm], theirs[nm], got[nm]) for nm in _BIG})

    gs["a_log"], gs["dt_bias"] = gs["a_log"][None], gs["dt_bias"][None]
    like = {nm: gs[nm] for nm in _SMALL}
    small_rows = -(-sum(int(np.prod(like[nm].shape)) for nm in _SMALL) // 1024) * 8
    g_small = _unpack_small(all_reduce_small(_pack_small(gs, small_rows)), like)
    for nm, width in (("conv_qkv_w", CONV_W), ("ffn_conv_w", 2 * D_FF)):
        g_small[nm] = lax.dynamic_slice_in_dim(g_small[nm], chip * (width // N_CHIPS), width // N_CHIPS, axis=1)[None]

    grads, deltas, new_m, new_v = {}, {}, {}, {}
    for nm in _BIG:
        g = g_big[nm].T if nm == "w_in" else g_big[nm]
        grads[nm] = g[None]
        deltas[nm], new_m[nm], new_v[nm] = (o[None] for o in _adamw(w_sh[nm][0], g, m_sh[nm][0], v_sh[nm][0],
                                                                     "adamw_" + nm))
    shard_like = {nm: w_sh[nm] for nm in _SMALL}
    rows_l = -(-sum(int(np.prod(shard_like[nm].shape)) for nm in _SMALL) // 1024) * 8
    g_l = _pack_small({nm: g_small[nm].reshape(w_sh[nm].shape) for nm in _SMALL}, rows_l)
    outs = _adamw(_pack_small(w_sh, rows_l), g_l, _pack_small(m_sh, rows_l), _pack_small(v_sh, rows_l), "adamw_small")
    grads.update(_unpack_small(g_l, shard_like))
    for tree, packed in zip((deltas, new_m, new_v), outs):
        tree.update(_unpack_small(packed, shard_like))

    return (loss, gx[None], *[grads[nm] for nm in names], *[deltas[nm] for nm in names],
            *[new_m[nm] for nm in names], *[new_v[nm] for nm in names])
```

```python
import functools
import math

import jax
import jax.numpy as jnp
import numpy as np
from jax import lax
from jax.experimental import pallas as pl
from jax.experimental.pallas import tpu as pltpu

F32 = jnp.float32
BF16 = jnp.bfloat16
HIGHEST = lax.Precision.HIGHEST
MESH = pl.DeviceIdType.MESH

D_MODEL = 1024
GRID_W = 64
ATTN_HEADS = 8
ATTN_KV_HEADS = 2
ATTN_GROUP = ATTN_HEADS // ATTN_KV_HEADS
HEAD_DIM = 128
ROPE_THETA = 10000.0
GDN_HEADS = 8
GDN_CHUNK = 64
D_FF = 2816
NORM_EPS = 1e-6
KV_W = ATTN_KV_HEADS * HEAD_DIM
Q_W = ATTN_HEADS * HEAD_DIM
GDN_W = GDN_HEADS * HEAD_DIM
CONV_W = 3 * GDN_W
MOD_W = 6 * D_MODEL
IN_COLS = 2 * KV_W + CONV_W + 4 * GDN_HEADS + Q_W + GDN_W + 2 * D_MODEL
IN_MAIN = IN_COLS - 4 * GDN_HEADS
SMALL_AT = 2 * KV_W + CONV_W
N_CHIPS = 4
N_DEV = 8

ADAM_LR = 0.001
ADAM_B1 = 0.9
ADAM_B2 = 0.999
ADAM_EPS = 1e-08
ADAM_WD = 0.01
ADAM_STEP = 10

VMEM_LIMIT = 48 * 1024 * 1024
MATMUL_VMEM_BUDGET = 40 * 1024 * 1024
MATMUL_STEP_BYTES = 1200 * 1024


def _pick(dim, prefs):
    for p in prefs:
        if p <= dim and dim % p == 0:
            return p
    return dim


_DIMS = {
    "nn": (((1,), (0,)), ((), ())),
    "nt": (((1,), (1,)), ((), ())),
    "tn": (((0,), (0,)), ((), ())),
}


def _matmul_plan(m, n, k, a_bytes, b_bytes):
    best = None
    for tm in (2304, 2048, 1152, 1024, 768, 512, 384, 256, 128, m):
        for tn in (2560, 1536, 1408, 1024, 768, 512, 256, 128, n):
            for tk in (3840, 2816, 2560, 2304, 2048, 1920, 1408, 1152, 1024, 768, 512, 256, 128, k):
                if tm > m or tn > n or tk > k or m % tm or n % tn or k % tk:
                    continue
                blocks = tm * tk * a_bytes + tk * tn * b_bytes + tm * tn * 4
                casts = (tm * tk * 2 if a_bytes > 2 else 0) + (tk * tn * 2 if b_bytes > 2 else 0) + tm * tn * 4
                if 2 * blocks + casts > MATMUL_VMEM_BUDGET:
                    continue
                nm, nn, nk = m // tm, n // tn, k // tk
                size_a, size_b = m * k * a_bytes, k * n * b_bytes
                for n_inner in (True, False):
                    if n_inner:
                        traffic = (size_a if nk == 1 else nn * size_a) + nm * size_b
                    else:
                        traffic = nn * size_a + (size_b if nk == 1 else nm * size_b)
                    cost = traffic + nm * nn * nk * MATMUL_STEP_BYTES + (nk - 1) * m * n * 4
                    if best is None or cost < best[0]:
                        best = (cost, tm, tn, tk, n_inner)
    return best[1:]


def _matmul(a, b, mode, name):
    if mode == "nn":
        (m, k), (_, n) = a.shape, b.shape
    elif mode == "nt":
        (m, k), (n, _) = a.shape, b.shape
    else:
        (k, m), (_, n) = a.shape, b.shape
    tm, tn, tk, n_inner = _matmul_plan(m, n, k, a.dtype.itemsize, b.dtype.itemsize)
    nk = k // tk
    ij = (lambda g0, g1: (g0, g1)) if n_inner else (lambda g0, g1: (g1, g0))
    if mode == "tn":
        a_spec = pl.BlockSpec((tk, tm), lambda g0, g1, l: (l, ij(g0, g1)[0]))
    else:
        a_spec = pl.BlockSpec((tm, tk), lambda g0, g1, l: (ij(g0, g1)[0], l))
    if mode == "nt":
        b_spec = pl.BlockSpec((tn, tk), lambda g0, g1, l: (ij(g0, g1)[1], l))
    else:
        b_spec = pl.BlockSpec((tk, tn), lambda g0, g1, l: (l, ij(g0, g1)[1]))
    dims = _DIMS[mode]

    def body(a_ref, b_ref, o_ref):
        part = lax.dot_general(a_ref[...].astype(BF16), b_ref[...].astype(BF16), dims, preferred_element_type=F32)
        if nk == 1:
            o_ref[...] = part
        else:
            l = pl.program_id(2)

            @pl.when(l == 0)
            def _():
                o_ref[...] = part

            @pl.when(l > 0)
            def _():
                o_ref[...] += part

    return pl.pallas_call(
        body,
        name=name,
        grid=(m // tm, n // tn, nk) if n_inner else (n // tn, m // tm, nk),
        in_specs=[a_spec, b_spec],
        out_specs=pl.BlockSpec((tm, tn), lambda g0, g1, l: ij(g0, g1)),
        out_shape=jax.ShapeDtypeStruct((m, n), F32),
        compiler_params=pltpu.CompilerParams(dimension_semantics=("parallel", "parallel", "arbitrary"),
                                             vmem_limit_bytes=VMEM_LIMIT),
    )(a, b)


@functools.partial(jax.custom_vjp, nondiff_argnums=(3,))
def pmm(a, w, wz, name):
    del wz
    return _matmul(a, w, "nn", name + "_f")


def _pmm_fwd(a, w, wz, name):
    del wz
    return _matmul(a, w, "nn", name + "_f"), (a, w)


def _pmm_bwd(name, res, g):
    a, w = res
    da = _matmul(g, w, "nt", name + "_da")
    if a.shape[0] < 128:
        pad = 128 - a.shape[0]
        at = jnp.pad(a.T, ((0, 0), (0, pad)))
        gp = jnp.pad(g, ((0, pad), (0, 0)))
        dw = _matmul(at, gp, "nn", name + "_dw")
    else:
        dw = _matmul(a, g, "tn", name + "_dw")
    return da, jnp.zeros_like(w), dw


pmm.defvjp(_pmm_fwd, _pmm_bwd)


@functools.partial(jax.custom_vjp, nondiff_argnums=(3,))
def pmm_t(a, wt, wtz, name):
    del wtz
    return _matmul(a, wt, "nt", name + "_f")


def _pmm_t_fwd(a, wt, wtz, name):
    del wtz
    return _matmul(a, wt, "nt", name + "_f"), (a, wt)


def _pmm_t_bwd(name, res, g):
    a, wt = res
    return _matmul(g, wt, "nn", name + "_da"), jnp.zeros_like(wt), _matmul(g, a, "tn", name + "_dw")


pmm_t.defvjp(_pmm_t_fwd, _pmm_t_bwd)


def rowop(fn, name, rows, bcs=(), crows=(), cbcs=(), tr=256):
    rows, bcs, crows, cbcs = tuple(rows), tuple(bcs), tuple(crows), tuple(cbcs)
    n_rows = rows[0].shape[0]
    tr = _pick(n_rows, (tr, 128, 64, 32, 16, 8))
    nr, nb, ncr, ncb = len(rows), len(bcs), len(crows), len(cbcs)
    n_in = nr + nb + ncr + ncb
    grid = (n_rows // tr,)

    def blk(arr):
        return jax.ShapeDtypeStruct((tr, arr.shape[1]), arr.dtype)

    def row_spec(arr):
        return pl.BlockSpec((tr, arr.shape[1]), lambda i: (i, 0))

    def bc_spec(arr):
        return pl.BlockSpec(arr.shape, lambda i: (0, 0))

    out_blk = jax.eval_shape(fn, *[blk(r) for r in rows], *bcs, *[blk(r) for r in crows], *cbcs)
    n_out = len(out_blk)
    out_shape = tuple(jax.ShapeDtypeStruct((n_rows, o.shape[1]), o.dtype) for o in out_blk)
    in_specs = ([row_spec(r) for r in rows] + [bc_spec(b) for b in bcs]
                + [row_spec(r) for r in crows] + [bc_spec(b) for b in cbcs])

    def order(vals):
        return vals

    def fwd_call(args):
        def body(*refs):
            vals = [r[...] for r in refs[:n_in]]
            res = fn(*order(vals))
            for o_ref, r in zip(refs[n_in:], res):
                o_ref[...] = r

        return pl.pallas_call(
            body, name=name + "_f", grid=grid, in_specs=in_specs,
            out_specs=[row_spec(o) for o in out_shape], out_shape=out_shape,
            compiler_params=pltpu.CompilerParams(dimension_semantics=("parallel",), vmem_limit_bytes=VMEM_LIMIT),
        )(*args)

    def bwd_call(args, cts):
        def body(*refs):
            vals = [r[...] for r in refs[:n_in]]
            ct_refs = refs[n_in:n_in + n_out]
            d_rows = refs[n_in + n_out:n_in + n_out + nr]
            d_bcs = refs[n_in + n_out + nr:]
            consts = vals[nr + nb:]
            _, vjp = jax.vjp(lambda *p: fn(*p, *consts), *vals[:nr + nb])
            grads = vjp(tuple(c[...] for c in ct_refs))
            for ref, g in zip(d_rows, grads[:nr]):
                ref[...] = g

            @pl.when(pl.program_id(0) == 0)
            def _():
                for ref in d_bcs:
                    ref[...] = jnp.zeros_like(ref)

            for ref, g in zip(d_bcs, grads[nr:]):
                ref[...] += g

        d_shape = tuple(jax.ShapeDtypeStruct(r.shape, r.dtype) for r in rows + bcs)
        return pl.pallas_call(
            body, name=name + "_b", grid=grid,
            in_specs=in_specs + [row_spec(o) for o in out_shape],
            out_specs=[row_spec(r) for r in rows] + [bc_spec(b) for b in bcs], out_shape=d_shape,
            compiler_params=pltpu.CompilerParams(dimension_semantics=("arbitrary",), vmem_limit_bytes=VMEM_LIMIT),
        )(*args, *cts)

    @jax.custom_vjp
    def op(diff, const):
        return fwd_call(diff + const)

    def op_fwd(diff, const):
        return fwd_call(diff + const), (diff, const)

    def op_bwd(res, cts):
        diff, const = res
        grads = bwd_call(diff + const, tuple(cts))
        return tuple(grads), tuple(jnp.zeros_like(c) for c in const)

    op.defvjp(op_fwd, op_bwd)
    return op(rows + bcs, crows + cbcs)


def colop(fn, name, arrays, uses, n_const, nblk, cw=128):
    arrays = tuple(arrays)
    n_diff = len(arrays) - n_const
    nd = sum(1 for u in uses if u[0] < n_diff)
    assert all(u[0] < n_diff for u in uses[:nd]) and all(u[0] >= n_diff for u in uses[nd:])

    def spec(u):
        return pl.BlockSpec((arrays[u[0]].shape[0], cw), lambda j, off=u[1]: (0, off + j))

    def out_spec(rows):
        return pl.BlockSpec((rows, cw), lambda j: (0, j))

    out_blk = jax.eval_shape(fn, *[jax.ShapeDtypeStruct((arrays[u[0]].shape[0], cw), arrays[u[0]].dtype)
                                   for u in uses])
    out_shape = tuple(jax.ShapeDtypeStruct((o.shape[0], nblk * cw), o.dtype) for o in out_blk)
    params = pltpu.CompilerParams(dimension_semantics=("parallel",), vmem_limit_bytes=VMEM_LIMIT)

    def fwd_call(arrs):
        def body(*refs):
            res = fn(*[r[...] for r in refs[:len(uses)]])
            for o_ref, r in zip(refs[len(uses):], res):
                o_ref[...] = r

        return pl.pallas_call(
            body, name=name + "_f", grid=(nblk,), in_specs=[spec(u) for u in uses],
            out_specs=[out_spec(o.shape[0]) for o in out_shape], out_shape=out_shape, compiler_params=params,
        )(*[arrs[u[0]] for u in uses])

    def bwd_call(arrs, cts):
        def body(*refs):
            vals = [r[...] for r in refs[:len(uses)]]
            ct_refs = refs[len(uses):len(uses) + len(out_shape)]
            _, vjp = jax.vjp(lambda *p: fn(*p, *vals[nd:]), *vals[:nd])
            for ref, g in zip(refs[len(uses) + len(out_shape):], vjp(tuple(c[...] for c in ct_refs))):
                ref[...] = g

        d_shape = tuple(jax.ShapeDtypeStruct((arrays[u[0]].shape[0], nblk * cw), F32) for u in uses[:nd])
        return pl.pallas_call(
            body, name=name + "_b", grid=(nblk,),
            in_specs=[spec(u) for u in uses] + [out_spec(o.shape[0]) for o in out_shape],
            out_specs=[out_spec(s.shape[0]) for s in d_shape], out_shape=d_shape, compiler_params=params,
        )(*[arrs[u[0]] for u in uses], *cts)

    @jax.custom_vjp
    def op(diff, const):
        return fwd_call(diff + const)

    def op_fwd(diff, const):
        return fwd_call(diff + const), (diff, const)

    def op_bwd(res, cts):
        diff, const = res
        d_uses = bwd_call(diff + const, tuple(cts))
        grads = []
        for i in range(n_diff):
            parts = sorted([(u[1], k) for k, u in enumerate(uses[:nd]) if u[0] == i])
            grads.append(d_uses[parts[0][1]] if len(parts) == 1
                         else jnp.concatenate([d_uses[k] for _, k in parts], axis=1))
        return tuple(grads), tuple(jnp.zeros_like(c) for c in const)

    op.defvjp(op_fwd, op_bwd)
    return op(arrays[:n_diff], arrays[n_diff:])


@functools.partial(jax.custom_vjp, nondiff_argnums=(1,))
def _roll_rows(x, k):
    return pltpu.roll(x, k % x.shape[0], 0)


def _roll_rows_fwd(x, k):
    return _roll_rows(x, k), None


def _roll_rows_bwd(k, _, g):
    return (_roll_rows(g, -k),)


_roll_rows.defvjp(_roll_rows_fwd, _roll_rows_bwd)


def _conv3(x, w0, w1, w2, starts):
    rows = lax.broadcasted_iota(jnp.int32, x.shape, 0)
    ends = tuple(s - 1 for s in starts[1:]) + (x.shape[0] - 1,)
    first = functools.reduce(jnp.logical_or, [rows == s for s in starts])
    last = functools.reduce(jnp.logical_or, [rows == e for e in ends])
    prev = jnp.where(first, 0.0, _roll_rows(x, 1))
    nxt = jnp.where(last, 0.0, _roll_rows(x, -1))
    return prev * w0 + x * w1 + nxt * w2


def _rms(x):
    return x * lax.rsqrt(jnp.mean(x * x, axis=-1, keepdims=True) + NORM_EPS)


def _heads(x, n):
    return [x[:, h * HEAD_DIM:(h + 1) * HEAD_DIM] for h in range(n)]


_NT = (((1,), (1,)), ((), ()))
_TN = (((0,), (0,)), ((), ()))
_TQ = 256


def _attn_probs(q, k):
    s = lax.dot_general(q, k, _NT, preferred_element_type=F32) * (HEAD_DIM ** -0.5)
    p = jnp.exp(s - jnp.max(s, axis=-1, keepdims=True))
    return p * (1.0 / jnp.sum(p, axis=-1, keepdims=True))


def _attn_fwd_call(q, k, v):
    n, t = q.shape[0], k.shape[0]
    tq = _pick(n, (_TQ, 128))

    def body(q_ref, k_ref, v_ref, o_ref):
        p = _attn_probs(q_ref[...].astype(BF16), k_ref[...].astype(BF16))
        o_ref[...] = jnp.dot(p.astype(BF16), v_ref[...].astype(BF16), preferred_element_type=F32)

    return pl.pallas_call(
        body, name="attn_f", grid=(ATTN_HEADS, n // tq),
        in_specs=[pl.BlockSpec((tq, HEAD_DIM), lambda h, i: (i, h)),
                  pl.BlockSpec((t, HEAD_DIM), lambda h, i: (0, h // ATTN_GROUP)),
                  pl.BlockSpec((t, HEAD_DIM), lambda h, i: (0, h // ATTN_GROUP))],
        out_specs=pl.BlockSpec((tq, HEAD_DIM), lambda h, i: (i, h)),
        out_shape=jax.ShapeDtypeStruct(q.shape, F32),
        compiler_params=pltpu.CompilerParams(dimension_semantics=("parallel", "parallel"),
                                             vmem_limit_bytes=VMEM_LIMIT),
    )(q, k, v)


def _attn_bwd_call(q, k, v, do):
    n, t = q.shape[0], k.shape[0]
    tq = _pick(n, (_TQ, 128))

    def body(q_ref, k_ref, v_ref, do_ref, dq_ref, dk_ref, dv_ref):
        @pl.when((pl.program_id(1) == 0) & (pl.program_id(2) == 0))
        def _():
            dk_ref[...] = jnp.zeros_like(dk_ref)
            dv_ref[...] = jnp.zeros_like(dv_ref)

        qb, kb, vb, dob = (r[...].astype(BF16) for r in (q_ref, k_ref, v_ref, do_ref))
        p = _attn_probs(qb, kb)
        dp = lax.dot_general(dob, vb, _NT, preferred_element_type=F32)
        ds = p * (dp - jnp.sum(p * dp, axis=-1, keepdims=True)) * (HEAD_DIM ** -0.5)
        dsb = ds.astype(BF16)
        dq_ref[...] = jnp.dot(dsb, kb, preferred_element_type=F32)
        dk_ref[...] += lax.dot_general(dsb, qb, _TN, preferred_element_type=F32)
        dv_ref[...] += lax.dot_general(p.astype(BF16), dob, _TN, preferred_element_type=F32)

    q_spec = pl.BlockSpec((tq, HEAD_DIM), lambda kh, g, i: (i, kh * ATTN_GROUP + g))
    kv_spec = pl.BlockSpec((t, HEAD_DIM), lambda kh, g, i: (0, kh))
    return pl.pallas_call(
        body, name="attn_b", grid=(ATTN_KV_HEADS, ATTN_GROUP, n // tq),
        in_specs=[q_spec, kv_spec, kv_spec, q_spec],
        out_specs=[q_spec, kv_spec, kv_spec],
        out_shape=(jax.ShapeDtypeStruct(q.shape, F32), jax.ShapeDtypeStruct(k.shape, F32),
                   jax.ShapeDtypeStruct(v.shape, F32)),
        compiler_params=pltpu.CompilerParams(dimension_semantics=("parallel", "arbitrary", "arbitrary"),
                                             vmem_limit_bytes=VMEM_LIMIT),
    )(q, k, v, do)


@jax.custom_vjp
def attention(q, k, v):
    return _attn_fwd_call(q, k, v)


def _attention_fwd(q, k, v):
    return _attn_fwd_call(q, k, v), (q, k, v)


def _attention_bwd(res, do):
    return _attn_bwd_call(*res, do)


attention.defvjp(_attention_fwd, _attention_bwd)


_C = GDN_CHUNK


def _pdot(a, b):
    return jnp.dot(a, b, precision=lax.Precision.HIGH, preferred_element_type=F32)


@jax.custom_vjp
def _hdot(a, b):
    return jnp.dot(a.astype(BF16), b.astype(BF16), preferred_element_type=F32)


def _hdot_fwd(a, b):
    return _hdot(a, b), (a, b)


def _hdot_bwd(res, g):
    a, b = res
    gb = g.astype(BF16)
    return (lax.dot_general(gb, b.astype(BF16), _NT, preferred_element_type=F32),
            lax.dot_general(a.astype(BF16), gb, _TN, preferred_element_type=F32))


_hdot.defvjp(_hdot_fwd, _hdot_bwd)


def _each(fn, *lists):
    return [fn(*args) for args in zip(*lists)]


def _unit_lower_inverse(low, blockdiag):
    eye = (lax.broadcasted_iota(jnp.int32, (_C, _C), 0) == lax.broadcasted_iota(jnp.int32, (_C, _C), 1)).astype(F32)
    ld = _each(lambda a: a * blockdiag, low)
    lo = _each(lambda a, d: a - d, low, ld)
    l2 = _each(_hdot, ld, ld)
    l4 = _each(_hdot, l2, l2)
    l8 = _each(_hdot, l4, l4)
    td = _each(lambda d, a2: _hdot(eye - d, eye + a2), ld, l2)
    td = _each(lambda t, a4: _hdot(t, eye + a4), td, l4)
    td = _each(lambda t, a8: _hdot(t, eye + a8), td, l8)
    nn = _each(_hdot, td, lo)
    n2 = _each(_hdot, nn, nn)
    out = _each(lambda n, m2: _hdot(eye - n, eye + m2), nn, n2)
    return _each(_hdot, out, td)


def _gdn_chunks(heads, blockdiag):
    q, k, v, b_b, be_b, e_b, kd_b, m1, dec, gl, s = (list(col) for col in zip(*heads))
    f32dot = lambda a, b: jnp.dot(a, b, preferred_element_type=F32)
    nt = lambda a, b: lax.dot_general(a, b, _NT, preferred_element_type=F32)
    kk = _each(nt, k, k)
    t_inv = _unit_lower_inverse(_each(lambda m, a: m * a, m1, kk), blockdiag)
    u = _each(lambda t, b, x: _hdot(t, b * x), t_inv, b_b, v)
    w = _each(lambda t, b, x: _hdot(t, b * x), t_inv, be_b, k)
    delta = _each(lambda uu, ww, ss: uu - f32dot(ww, ss), u, w, s)
    p = _each(lambda d, qq, kx: d * nt(qq, kx), dec, q, k)
    o = _each(lambda qq, e, ss, pp, dd: f32dot(qq * e, ss) + f32dot(pp, dd), q, e_b, s, p, delta)
    s_new = _each(lambda g, ss, kx, kd, dd: g * ss + lax.dot_general(kx * kd, dd, _TN, preferred_element_type=F32),
                  gl, s, k, kd_b, delta)
    return o, s_new


def _blockdiag_mask():
    r = lax.broadcasted_iota(jnp.int32, (_C, _C), 0) >> 4
    c = lax.broadcasted_iota(jnp.int32, (_C, _C), 1) >> 4
    return (r == c).astype(F32)


def _gdn_specs(nc, ncc, reverse, backward):
    def ch(s):
        s = nc - 1 - s if backward else s
        return jnp.where(s < ncc, ncc - 1 - s, nc + ncc - 1 - s) if reverse else s

    tok = pl.BlockSpec((_C, 3 * GDN_W), lambda s: (ch(s), 0))
    park = nc - ncc - 1 if reverse else 0
    out = pl.BlockSpec((_C, GDN_W), lambda s: (jnp.where(ch(s) >= ncc, ch(s) - ncc, park), 0))
    per_tok = pl.BlockSpec((GDN_HEADS, _C, HEAD_DIM), lambda s: (0, ch(s), 0))
    mat = pl.BlockSpec((GDN_HEADS, None, _C, _C), lambda s: (0, ch(s), 0, 0))
    row = pl.BlockSpec((GDN_HEADS, None, 1, HEAD_DIM), lambda s: (0, ch(s), 0, 0))
    state = pl.BlockSpec((GDN_HEADS, None, HEAD_DIM, HEAD_DIM), lambda s: (0, ch(s), 0, 0))
    return tok, out, per_tok, mat, row, state, ch


def _head_cols(h, part):
    return slice((part * GDN_HEADS + h) * HEAD_DIM, (part * GDN_HEADS + h + 1) * HEAD_DIM)


def _gdn_heads(qkv_ref, factor_refs, state_ref, first):
    return [[qkv_ref[:, _head_cols(h, 0)], qkv_ref[:, _head_cols(h, 1)], qkv_ref[:, _head_cols(h, 2)]]
            + [r[h] for r in factor_refs] + [state_ref[first + h]] for h in range(GDN_HEADS)]


def _gdn_fwd_call(ncc, qkv, factors):
    t = qkv.shape[0]
    nc, nd = t // _C, len(factors)
    specs = [_gdn_specs(nc, ncc, d == 1, False) for d in range(nd)]

    def body(*refs):
        ins_d = [refs[8 * d:8 * d + 8] for d in range(nd)]
        outs_d = [refs[8 * nd + 2 * d:8 * nd + 2 * d + 2] for d in range(nd)]
        s_ref = refs[-1]

        @pl.when(pl.program_id(0) == 0)
        def _():
            s_ref[...] = jnp.zeros_like(s_ref)

        heads = sum([_gdn_heads(r[0], r[1:], s_ref, GDN_HEADS * d) for d, r in enumerate(ins_d)], [])
        o, s_new = _gdn_chunks(heads, _blockdiag_mask())
        for d, (o_ref, sall_ref) in enumerate(outs_d):
            for h in range(GDN_HEADS):
                i = GDN_HEADS * d + h
                sall_ref[h] = heads[i][10]
                o_ref[:, _head_cols(h, 0)] = o[i]
                s_ref[i] = s_new[i]

    in_specs, out_specs, out_shape, operands = [], [], [], []
    for (tok, out, per_tok, mat, row, state, _), f in zip(specs, factors):
        in_specs += [tok, per_tok, per_tok, per_tok, per_tok, mat, mat, row]
        out_specs += [out, state]
        out_shape += [jax.ShapeDtypeStruct((t - ncc * _C, GDN_W), F32),
                      jax.ShapeDtypeStruct((GDN_HEADS, nc, HEAD_DIM, HEAD_DIM), F32)]
        operands += [qkv, *f]
    res = pl.pallas_call(
        body, name="gdn_f", grid=(nc,), in_specs=in_specs, out_specs=out_specs, out_shape=out_shape,
        scratch_shapes=[pltpu.VMEM((nd * GDN_HEADS, HEAD_DIM, HEAD_DIM), F32)],
        compiler_params=pltpu.CompilerParams(dimension_semantics=("arbitrary",), vmem_limit_bytes=VMEM_LIMIT),
    )(*operands)
    return [(res[2 * d], res[2 * d + 1]) for d in range(nd)]


def _gdn_bwd_call(ncc, qkv, factors, salls, dos):
    t = qkv.shape[0]
    nc, nd = t // _C, len(factors)
    specs = [_gdn_specs(nc, ncc, d == 1, True) for d in range(nd)]

    def body(*refs):
        ins_d = [refs[10 * d:10 * d + 10] for d in range(nd)]
        outs_d = [refs[10 * nd + 8 * d:10 * nd + 8 * d + 8] for d in range(nd)]
        ds_ref = refs[-1]

        @pl.when(pl.program_id(0) == 0)
        def _():
            ds_ref[...] = jnp.zeros_like(ds_ref)

        bd = _blockdiag_mask()
        heads = sum([_gdn_heads(r[0], r[1:8], r[8], 0) for r in ins_d], [])
        _, vjp = jax.vjp(lambda hs: _gdn_chunks(hs, bd), heads)
        live = [spec[6](pl.program_id(0)) >= ncc for spec in specs]
        (all_grads,) = vjp(([jnp.where(live[d], r[9][:, _head_cols(h, 0)], 0.0)
                             for d, r in enumerate(ins_d) for h in range(GDN_HEADS)],
                            [ds_ref[i] for i in range(nd * GDN_HEADS)]))
        for d, out_refs in enumerate(outs_d):
            for h in range(GDN_HEADS):
                grads = all_grads[GDN_HEADS * d + h]
                for part in range(3):
                    out_refs[0][:, _head_cols(h, part)] = grads[part]
                for ref, g in zip(out_refs[1:], grads[3:10]):
                    ref[h] = g
                ds_ref[GDN_HEADS * d + h] = grads[10]

    shp = lambda a: jax.ShapeDtypeStruct(a.shape, F32)
    in_specs, out_specs, out_shape, operands = [], [], [], []
    for (tok, out, per_tok, mat, row, state, _), f, sall, do in zip(specs, factors, salls, dos):
        in_specs += [tok, per_tok, per_tok, per_tok, per_tok, mat, mat, row, state, out]
        out_specs += [tok, per_tok, per_tok, per_tok, per_tok, mat, mat, row]
        out_shape += [shp(qkv)] + [shp(a) for a in f]
        operands += [qkv, *f, sall, do]
    res = pl.pallas_call(
        body, name="gdn_b", grid=(nc,), in_specs=in_specs, out_specs=out_specs, out_shape=out_shape,
        scratch_shapes=[pltpu.VMEM((nd * GDN_HEADS, HEAD_DIM, HEAD_DIM), F32)],
        compiler_params=pltpu.CompilerParams(dimension_semantics=("arbitrary",), vmem_limit_bytes=VMEM_LIMIT),
    )(*operands)
    return [res[8 * d:8 * d + 8] for d in range(nd)]


@functools.partial(jax.custom_vjp, nondiff_argnums=(0,))
def gdn_scan(ncc, qkv, f_fwd, f_rev):
    (o0, _), (o1, _) = _gdn_fwd_call(ncc, qkv, [f_fwd, f_rev])
    return o0, o1


def _gdn_scan_fwd(ncc, qkv, f_fwd, f_rev):
    (o0, s0), (o1, s1) = _gdn_fwd_call(ncc, qkv, [f_fwd, f_rev])
    return (o0, o1), (qkv, f_fwd, f_rev, s0, s1)


def _gdn_scan_bwd(ncc, res, dos):
    qkv, f_fwd, f_rev, s0, s1 = res
    g0, g1 = _gdn_bwd_call(ncc, qkv, [f_fwd, f_rev], [s0, s1], list(dos))
    return g0[0] + g1[0], tuple(g0[1:]), tuple(g1[1:])


gdn_scan.defvjp(_gdn_scan_fwd, _gdn_scan_bwd)


def _rope_tables(n, cl):
    t = np.arange(n)
    inv_freq = (ROPE_THETA ** (-np.arange(0, HEAD_DIM // 2, 2, dtype=np.float32) / (HEAD_DIM // 2))).astype(np.float32)
    ang_r = (t // GRID_W).astype(np.float32)[:, None] * inv_freq
    ang_c = (t % GRID_W).astype(np.float32)[:, None] * inv_freq
    cos = np.concatenate([np.cos(ang_r), np.cos(ang_r), np.cos(ang_c), np.cos(ang_c)], axis=1)
    sin = np.concatenate([-np.sin(ang_r), np.sin(ang_r), -np.sin(ang_c), np.sin(ang_c)], axis=1)
    cos_all = np.concatenate([np.ones((cl, HEAD_DIM), np.float32), cos], axis=0)
    sin_all = np.concatenate([np.zeros((cl, HEAD_DIM), np.float32), sin], axis=0)
    j = np.arange(HEAD_DIM)
    src = np.where((j % 64) < 32, j + 32, j - 32)
    perm = np.zeros((HEAD_DIM, HEAD_DIM), np.float32)
    perm[src, j] = 1.0
    return (jnp.asarray(cos.astype(np.float32)), jnp.asarray(sin.astype(np.float32)),
            jnp.asarray(cos_all), jnp.asarray(sin_all), jnp.asarray(perm))


def _gdn_factors(log_a, beta, reverse):
    t = log_a.shape[0]
    nc = t // _C
    la = log_a.reshape(nc, _C, GDN_HEADS).transpose(2, 0, 1)
    be = beta.reshape(nc, _C, GDN_HEADS).transpose(2, 0, 1)
    gam = lax.cumsum(la, axis=2, reverse=reverse)
    idx = jnp.arange(_C)
    incl = (idx[:, None] <= idx[None, :]) if reverse else (idx[:, None] >= idx[None, :])
    strict = (idx[:, None] < idx[None, :]) if reverse else (idx[:, None] > idx[None, :])
    dec = jnp.exp(jnp.where(incl, gam[..., :, None] - gam[..., None, :], -jnp.inf))
    m1 = jnp.where(strict, be[..., :, None] * dec, 0.0)
    e = jnp.exp(gam)
    g_last = gam[..., :1] if reverse else gam[..., -1:]
    lanes = lambda a: jnp.broadcast_to(a.reshape(GDN_HEADS, t, 1), (GDN_HEADS, t, HEAD_DIM))
    gl = jnp.broadcast_to(jnp.exp(g_last)[..., None], (GDN_HEADS, nc, 1, HEAD_DIM))
    return lanes(be), lanes(be * e), lanes(e), lanes(jnp.exp(g_last - gam)), m1, dec, gl


def local_loss(x, wz, wb, ws, c, ctx, target):
    return channel_mixing(token_mixing(x, wz, wb, ws, c, ctx), wz, wb, ws, target)


def token_mixing(x, wz, wb, ws, c, ctx):
    n, cl = x.shape[0], ctx.shape[0]
    cos_q, sin_q, cos_k, sin_k, perm = _rope_tables(n, cl)

    sc_in = jnp.concatenate([jax.nn.silu(c), jax.nn.silu(ws["c_ctx"])[None, :], jnp.zeros((14, D_MODEL), F32)], axis=0)
    mod = pmm(sc_in, wb["w_mod"], wz["w_mod"], "mm_mod") + ws["b_mod"]
    sh1, sc1, g1, sh2, sc2, g2 = [mod[0:1, i * D_MODEL:(i + 1) * D_MODEL] for i in range(6)]
    csh1, csc1 = mod[1:2, 0:D_MODEL], mod[1:2, D_MODEL:2 * D_MODEL]

    def norm_mod(a, sh, sc):
        return (_rms(a) * (1.0 + sc) + sh,)

    (hx,) = rowop(norm_mod, "normmod_x", (x,), (sh1, sc1))
    (hc,) = rowop(norm_mod, "normmod_c", (ctx,), (csh1, csc1))
    h_all = jnp.concatenate([hc, hx], axis=0)
    p_main = pmm_t(h_all, wb["w_in_main"], wz["w_in_main"], "mm_in")
    p_small = pmm_t(h_all, wb["w_in_small"], wz["w_in_small"], "mm_ins")
    ak, av, qkv, aq, z, gate = jnp.split(p_main, [KV_W, 2 * KV_W, SMALL_AT, SMALL_AT + Q_W, SMALL_AT + Q_W + GDN_W],
                                         axis=1)
    db, da = p_small[:, :2 * GDN_HEADS], p_small[:, 2 * GDN_HEADS:4 * GDN_HEADS]

    def qk_prep(nh):
        def fn(a, w, cos, sin, pm):
            outs = []
            for ah in _heads(a, nh):
                y = _rms(ah) * w
                outs.append(y * cos + _pdot(y, pm) * sin)
            return (jnp.concatenate(outs, axis=1),)
        return fn

    (q_x,) = rowop(qk_prep(ATTN_HEADS), "q_prep", (aq[cl:],), (ws["q_norm_w"],), (cos_q, sin_q), (perm,))
    (k_all,) = rowop(qk_prep(ATTN_KV_HEADS), "k_prep", (ak,), (ws["k_norm_w"],), (cos_k, sin_k), (perm,))
    attn_x = attention(q_x, k_all, av)

    cw = ws["conv_qkv_w"]
    normed = jnp.asarray(np.repeat([1.0, 1.0, 0.0], GDN_W)[None, :], F32)
    scale = jnp.asarray(np.repeat([HEAD_DIM ** -0.5, 1.0, 1.0], GDN_W)[None, :], F32)

    def gdn_prep(a, w0, w1, w2, nf, sc):
        s = jax.nn.silu(_conv3(a, w0, w1, w2, (0, cl)))
        inv = lax.rsqrt(jnp.sum(s * s, axis=-1, keepdims=True) + NORM_EPS)
        return (s * jnp.where(nf > 0.0, inv * sc, 1.0),)

    (qkvn,) = colop(gdn_prep, "gdn_prep", (qkv, cw[0:1], cw[1:2], cw[2:3], normed, scale),
                    [(i, 0) for i in range(6)], 2, 3 * GDN_HEADS)
    beta = jax.nn.sigmoid(db).reshape(-1, 2, GDN_HEADS)
    log_a = -jnp.exp(ws["a_log"])[None] * jax.nn.softplus(da.reshape(-1, 2, GDN_HEADS) + ws["dt_bias"][None])
    o_fwd, o_rev = gdn_scan(cl // _C, qkvn, _gdn_factors(log_a[:, 0], beta[:, 0], False),
                            _gdn_factors(log_a[:, 1], beta[:, 1], True))
    o_x = o_fwd + o_rev

    def gdn_out(o, zz, w):
        outs = [_rms(oh) * w * jax.nn.silu(zh) for oh, zh in zip(_heads(o, GDN_HEADS), _heads(zz, GDN_HEADS))]
        return (jnp.concatenate(outs, axis=1),)

    (gdn_x,) = rowop(gdn_out, "gdn_out", (o_x, z[cl:]), (ws["gdn_norm_w"],))
    return dict(x=x, attn_x=attn_x, gdn_x=gdn_x, gate=gate[cl:], g1=g1, sh2=sh2, sc2=sc2, g2=g2)


def channel_mixing(mixed, wz, wb, ws, target):
    x, attn_x, gdn_x, gate = mixed["x"], mixed["attn_x"], mixed["gdn_x"], mixed["gate"]
    g1, sh2, sc2, g2 = mixed["g1"], mixed["sh2"], mixed["sc2"], mixed["g2"]
    pa = pmm(attn_x, wb["w_pa"], wz["w_pa"], "mm_pa")
    pd = pmm(gdn_x, wb["w_pd"], wz["w_pd"], "mm_pd")

    def merge(a, d, g):
        return (jax.nn.sigmoid(g[:, :D_MODEL]) * a + jax.nn.sigmoid(g[:, D_MODEL:]) * d,)

    (y,) = rowop(merge, "merge", (pa, pd, gate))
    mo = pmm(y, wb["w_out"], wz["w_out"], "mm_out")

    def res_norm_mod(xx, m, g, sh, sc):
        x1 = xx + g * m
        return x1, _rms(x1) * (1.0 + sc) + sh

    x1, h2 = rowop(res_norm_mod, "res1", (x, mo), (g1, sh2, sc2))
    up = pmm(h2, wb["w_up"], wz["w_up"], "mm_up")
    fw = ws["ffn_conv_w"]

    def ffn_act(ug, uv, w0g, w0v, w1g, w1v, w2g, w2v, bg, bv):
        g = _conv3(ug, w0g, w1g, w2g, (0,)) + bg
        v = _conv3(uv, w0v, w1v, w2v, (0,)) + bv
        return (jax.nn.silu(g) * v,)

    half = D_FF // HEAD_DIM
    (act,) = colop(ffn_act, "ffn_act", (up, fw[0:1], fw[1:2], fw[2:3], ws["ffn_conv_b"]),
                   [(i, off) for i in range(5) for off in (0, half)], 0, half)
    dn = pmm(act, wb["w_down"], wz["w_down"], "mm_down")

    def head(xx, m, g, w, tgt):
        yy = _rms(xx + g * m) * w
        err = (yy - tgt) ** 2
        return (jnp.broadcast_to(0.5 * jnp.mean(err, axis=-1, keepdims=True), (xx.shape[0], HEAD_DIM)),)

    (row_loss,) = rowop(head, "head", (x1, dn), (g2, ws["final_norm_w"][None, :]), (target,))
    return jnp.sum(row_loss[:, 0])


_HBM = pl.BlockSpec(memory_space=pltpu.HBM)


def _chip_peers():
    x, y = lax.axis_index("x"), lax.axis_index("y")
    return [(1 - x, y), (x, 1 - y), (1 - x, 1 - y)]


_SPLIT_COLS = ("w_in",)


def _half_of(view, nm, idx, lead=0):
    r, cdim = view.shape[-2:]
    pre = (slice(None),) * lead
    if nm in _SPLIT_COLS:
        return view.at[pre + (slice(None), pl.ds(pl.multiple_of(idx * (cdim // 2), 128), cdim // 2))]
    return view.at[pre + (pl.ds(pl.multiple_of(idx * (r // 2), 16), r // 2), slice(None))]


def _remote(src, dst, send_sem, recv_sem, dev):
    return pltpu.make_async_remote_copy(src_ref=src, dst_ref=dst, send_sem=send_sem, recv_sem=recv_sem,
                                        device_id=dev, device_id_type=MESH)


def _hbm_call(body, name, ins, out_shape, n_sems, in_place=False):
    names = tuple(ins)
    return dict(zip(names, pl.pallas_call(
        body, name=name, in_specs=[_HBM] * len(names), out_specs=[_HBM] * len(names),
        out_shape=[out_shape(nm, ins[nm]) for nm in names],
        scratch_shapes=[pltpu.SemaphoreType.DMA((k,)) for k in n_sems],
        input_output_aliases={i: i for i in range(len(names))} if in_place else {},
    )(*[ins[nm] for nm in names])))


def all_gather_chips(shards):
    names = tuple(shards)
    n = len(names)

    def body(*refs):
        ins, outs = dict(zip(names, refs[:n])), dict(zip(names, refs[n:2 * n]))
        ici_send, ici_recv, d2d_send, d2d_recv, own_send, own_recv = refs[2 * n:]
        x, y, c = lax.axis_index("x"), lax.axis_index("y"), lax.axis_index("c")
        me, sib = 2 * x + y, (x, y, 1 - c)
        own = [_remote(ins[nm], outs[nm].at[me], own_send.at[i], own_recv.at[i], sib) for i, nm in enumerate(names)]
        for cp in own:
            cp.start()
        sends = []
        for k, (px, py) in enumerate(_chip_peers()):
            for i, nm in enumerate(names):
                cp = _remote(_half_of(ins[nm], nm, c), _half_of(outs[nm].at[me], nm, c), ici_send.at[k * n + i],
                             ici_recv.at[k * n + i], (px, py, c))
                cp.start()
                sends.append(cp)
        for k, (px, py) in enumerate(_chip_peers()):
            for i, nm in enumerate(names):
                landed = _half_of(outs[nm].at[2 * px + py], nm, c)
                _remote(landed, landed, ici_send.at[k * n + i], ici_recv.at[k * n + i], (px, py, c)).wait_recv()
                fw = _remote(landed, landed, d2d_send.at[k * n + i], d2d_recv.at[k * n + i], sib)
                fw.start()
                sends.append(fw)
        for k, (px, py) in enumerate(_chip_peers()):
            for i, nm in enumerate(names):
                other = _half_of(outs[nm].at[2 * px + py], nm, 1 - c)
                _remote(other, other, d2d_send.at[k * n + i], d2d_recv.at[k * n + i], sib).wait_recv()
        for cp in sends:
            cp.wait_send()
        for cp in own:
            cp.wait()

    return _hbm_call(body, "ag_weights", shards, lambda nm, a: jax.ShapeDtypeStruct((N_CHIPS,) + a.shape, a.dtype),
                     (3 * n, 3 * n, 3 * n, 3 * n, n, n))


_SEM = pl.BlockSpec(memory_space=pltpu.SEMAPHORE)


def push_start(name, arrays, land_shapes, copies, n_copies):
    names = tuple(arrays)
    n = len(names)

    def body(*refs):
        send_sems, recv_sems, token = refs[2 * n], refs[2 * n + 1], refs[-1]
        for j, (src, dst, dev) in enumerate(copies(refs[:n], refs[n:2 * n])):
            _remote(src, dst, send_sems.at[j], recv_sems.at[j], dev).start()
        token[...] = jnp.zeros_like(token)

    hbm = lambda a: pltpu.with_memory_space_constraint(a, pltpu.HBM)
    lands = [lax.empty(land_shapes[nm], arrays[nm].dtype) for nm in names]
    res = pl.pallas_call(
        body, name=name,
        out_shape=(pltpu.SemaphoreType.DMA((n_copies,)), pltpu.SemaphoreType.DMA((n_copies,)),
                   *[pltpu.HBM(arrays[nm].shape, arrays[nm].dtype) for nm in names],
                   *[pltpu.HBM(a.shape, a.dtype) for a in lands], jax.ShapeDtypeStruct((8, 128), F32)),
        in_specs=[_HBM] * (2 * n),
        out_specs=(_SEM, _SEM, *[_HBM] * (2 * n), pl.BlockSpec(memory_space=pltpu.VMEM)),
        input_output_aliases={i: 2 + i for i in range(2 * n)},
        compiler_params=pltpu.CompilerParams(has_side_effects=pltpu.SideEffectType.DATAFLOW_SIDE_EFFECTING),
    )(*[hbm(arrays[nm]) for nm in names], *[hbm(a) for a in lands])
    return names, res[0], res[1], res[2:2 + n], res[2 + n:2 + 2 * n], res[-1]


def push_wait(name, started, copies, after):
    names, send_sems, recv_sems, srcs, lands, _ = started
    n = len(names)

    def body(*refs):
        send_ref, recv_ref = refs[2 * n], refs[2 * n + 1]
        for j, (src, dst, dev) in enumerate(copies(refs[:n], refs[n:2 * n])):
            cp = _remote(src, dst, send_ref.at[j], recv_ref.at[j], dev)
            cp.wait_send()
            cp.wait_recv()

    res = pl.pallas_call(
        body, name=name,
        out_shape=(*[pltpu.HBM(a.shape, a.dtype) for a in srcs], *[pltpu.HBM(a.shape, a.dtype) for a in lands]),
        in_specs=[_HBM] * (2 * n) + [_SEM, _SEM, pl.BlockSpec(memory_space=pl.ANY)],
        out_specs=tuple([_HBM] * (2 * n)),
        input_output_aliases={i: i for i in range(2 * n)},
        compiler_params=pltpu.CompilerParams(has_side_effects=pltpu.SideEffectType.DATAFLOW_SIDE_EFFECTING),
    )(*srcs, *lands, send_sems, recv_sems, after)
    return dict(zip(names, res[n:]))


def _gather_copies(srcs, lands):
    x, y, c = lax.axis_index("x"), lax.axis_index("y"), lax.axis_index("c")
    devs = [(px, py, c) for px, py in _chip_peers()] + [(x, y, 1 - c)]
    return [(src, land.at[2 * x + y], dev) for src, land in zip(srcs, lands) for dev in devs]


def _scatter_copies(srcs, lands):
    c = lax.axis_index("c")
    return [(src.at[2 * px + py], land.at[k], (px, py, c))
            for src, land in zip(srcs, lands) for k, (px, py) in enumerate(_chip_peers())]


def sibling_halves(blocks, name):
    names = tuple(blocks)

    def body(*refs):
        n = len(names)
        ins, outs = dict(zip(names, refs[:n])), dict(zip(names, refs[n:2 * n]))
        send_sems, recv_sems = refs[2 * n:]
        x, y, c = lax.axis_index("x"), lax.axis_index("y"), lax.axis_index("c")
        cps = [_remote(_half_of(ins[nm], nm, 1 - c, lead=1), outs[nm], send_sems.at[i], recv_sems.at[i], (x, y, 1 - c))
               for i, nm in enumerate(names)]
        for cp in cps:
            cp.start()
        for cp in cps:
            cp.wait()

    def half_shape(nm, a):
        r, cdim = a.shape[-2:]
        return jax.ShapeDtypeStruct((N_CHIPS, r, cdim // 2) if nm in _SPLIT_COLS else (N_CHIPS, r // 2, cdim), a.dtype)

    return _hbm_call(body, name, blocks, half_shape, (len(names), len(names)))


def scatter_halves(blocks):
    names = tuple(blocks)
    n = len(names)

    def body(*refs):
        ins, outs = dict(zip(names, refs[:n])), dict(zip(names, refs[n:2 * n]))
        send_sems, recv_sems = refs[2 * n:]
        c = lax.axis_index("c")
        cps = [_remote(ins[nm].at[2 * px + py], outs[nm].at[k], send_sems.at[k * n + i], recv_sems.at[k * n + i],
                       (px, py, c))
               for k, (px, py) in enumerate(_chip_peers()) for i, nm in enumerate(names)]
        for cp in cps:
            cp.start()
        for cp in cps:
            cp.wait_recv()
        for cp in cps:
            cp.wait_send()

    return _hbm_call(body, "rs_grads", blocks, lambda nm, a: jax.ShapeDtypeStruct((3,) + a.shape[1:], a.dtype),
                     (3 * n, 3 * n))


def sibling_assemble(arrays):
    names = tuple(arrays)

    def body(*refs):
        n = len(names)
        ins, outs = dict(zip(names, refs[:n])), dict(zip(names, refs[n:2 * n]))
        send_sems, recv_sems = refs[2 * n:]
        x, y, c = lax.axis_index("x"), lax.axis_index("y"), lax.axis_index("c")
        cps = [_remote(_half_of(ins[nm], nm, c), _half_of(outs[nm], nm, c), send_sems.at[i], recv_sems.at[i],
                       (x, y, 1 - c)) for i, nm in enumerate(names)]
        for cp in cps:
            cp.start()
        for i, nm in enumerate(names):
            other = _half_of(outs[nm], nm, 1 - c)
            _remote(other, other, send_sems.at[i], recv_sems.at[i], (x, y, 1 - c)).wait_recv()
        for cp in cps:
            cp.wait_send()

    return _hbm_call(body, "rs_assemble", arrays, lambda nm, a: jax.ShapeDtypeStruct(a.shape, a.dtype),
                     (len(names), len(names)), in_place=True)


def all_reduce_small(v):
    def body(v_ref, tot_ref, gath_ref, send_sems, recv_sems):
        x, y, c = lax.axis_index("x"), lax.axis_index("y"), lax.axis_index("c")
        me = 4 * x + 2 * y + c
        gath_ref[me] = v_ref[...]

        def peer(k):
            m = k + 1
            return (x ^ (m >> 2 & 1), y ^ (m >> 1 & 1), c ^ (m & 1))

        sends = [pltpu.make_async_remote_copy(src_ref=v_ref, dst_ref=gath_ref.at[me], send_sem=send_sems.at[k],
                                              recv_sem=recv_sems.at[k], device_id=peer(k), device_id_type=MESH)
                 for k in range(N_DEV - 1)]
        for cp in sends:
            cp.start()
        for k in range(N_DEV - 1):
            px, py, pc = peer(k)
            pltpu.make_async_remote_copy(src_ref=v_ref, dst_ref=gath_ref.at[4 * px + 2 * py + pc],
                                         send_sem=send_sems.at[k], recv_sem=recv_sems.at[k], device_id=peer(k),
                                         device_id_type=MESH).wait_recv()
        for cp in sends:
            cp.wait_send()
        acc = gath_ref[0]
        for d in range(1, N_DEV):
            acc = acc + gath_ref[d]
        tot_ref[...] = acc

    vm = pl.BlockSpec(memory_space=pltpu.VMEM)
    return pl.pallas_call(
        body, name="ar_small", in_specs=[vm], out_specs=[vm, vm],
        out_shape=(jax.ShapeDtypeStruct(v.shape, v.dtype), jax.ShapeDtypeStruct((N_DEV,) + v.shape, v.dtype)),
        scratch_shapes=[pltpu.SemaphoreType.DMA((N_DEV - 1,)), pltpu.SemaphoreType.DMA((N_DEV - 1,))],
    )(v)[0]


def _elementwise(fn, name, ins, n_out, out_dtype=F32):
    r, cdim = ins[0].shape
    tr = _pick(r, tuple(p for p in (488, 256, 128, 104, 64, 32, 16, 8) if p * cdim * 4 <= 2 * 1024 * 1024))
    spec = pl.BlockSpec((tr, cdim), lambda i: (i, 0))

    def body(*refs):
        res = fn(*[ref[...] for ref in refs[:len(ins)]])
        for o_ref, v in zip(refs[len(ins):], res):
            o_ref[...] = v

    return pl.pallas_call(
        body, name=name, grid=(r // tr,), in_specs=[spec] * len(ins), out_specs=[spec] * n_out,
        out_shape=tuple(jax.ShapeDtypeStruct((r, cdim), out_dtype) for _ in range(n_out)),
        compiler_params=pltpu.CompilerParams(dimension_semantics=("parallel",), vmem_limit_bytes=VMEM_LIMIT),
    )(*ins)


def _half_block_specs(nm, shard_shape):
    r, cdim = shard_shape
    if nm in _SPLIT_COLS:
        return (None, r, cdim // 2), (lambda j, c: (j, 0, c))
    return (None, r // 2, cdim), (lambda j, c: (j, c, 0))


def _presum(nm, sel, g32, a):
    blk, at = _half_block_specs(nm, g32.shape[1:])

    def body(s_ref, g_ref, a_ref, o_ref):
        del s_ref
        o_ref[...] = (g_ref[...] + a_ref[...]).astype(BF16)

    return pl.pallas_call(
        body, name="rs_presum_" + nm,
        grid_spec=pltpu.PrefetchScalarGridSpec(
            num_scalar_prefetch=1, grid=(N_CHIPS,),
            in_specs=[pl.BlockSpec(blk, lambda j, s: at(j, s[0])), pl.BlockSpec(blk, lambda j, s: (j, 0, 0))],
            out_specs=pl.BlockSpec(blk, lambda j, s: (j, 0, 0))),
        out_shape=jax.ShapeDtypeStruct(a.shape, BF16),
        compiler_params=pltpu.CompilerParams(dimension_semantics=("parallel",), vmem_limit_bytes=VMEM_LIMIT),
    )(sel, g32, a)


def _finalsum(nm, sel, g32, a, got):
    blk, at = _half_block_specs(nm, g32.shape[1:])

    def body(s_ref, g_ref, a_ref, r_ref, o_ref):
        del s_ref
        acc = g_ref[...] + a_ref[...]
        for k in range(3):
            acc = acc + r_ref[k].astype(F32)
        o_ref[...] = acc

    return pl.pallas_call(
        body, name="rs_final_" + nm,
        grid_spec=pltpu.PrefetchScalarGridSpec(
            num_scalar_prefetch=1, grid=(1,),
            in_specs=[pl.BlockSpec(blk, lambda i, s: at(s[1], s[0])), pl.BlockSpec(blk, lambda i, s: (s[1], 0, 0)),
                      pl.BlockSpec(got.shape, lambda i, s: (0, 0, 0))],
            out_specs=pl.BlockSpec(blk[1:], lambda i, s: at(0, s[0])[1:])),
        out_shape=jax.ShapeDtypeStruct(g32.shape[1:], F32),
        compiler_params=pltpu.CompilerParams(dimension_semantics=("arbitrary",), vmem_limit_bytes=VMEM_LIMIT),
    )(sel, g32, a, got)


def _adamw(w, g, m, v, name):
    shape = w.shape
    to2 = lambda a: a.reshape(-1, shape[-1])

    def fn(w_, g_, m_, v_):
        m_new = ADAM_B1 * m_ + (1.0 - ADAM_B1) * g_
        v_new = ADAM_B2 * v_ + (1.0 - ADAM_B2) * (g_ * g_)
        m_hat = m_new / (1.0 - ADAM_B1 ** ADAM_STEP)
        v_hat = v_new / (1.0 - ADAM_B2 ** ADAM_STEP)
        delta = -ADAM_LR * (m_hat / (jnp.sqrt(v_hat) + ADAM_EPS) + ADAM_WD * w_)
        return delta, m_new, v_new

    outs = _elementwise(fn, name, [to2(a) for a in (w, g, m, v)], 3)
    return tuple(o.reshape(shape) for o in outs)


_BIG = ("w_mod", "w_in", "w_pa", "w_pd", "w_out", "w_up", "w_down")
_EARLY = ("w_mod", "w_in")
_LATE = ("w_pa", "w_pd", "w_out", "w_up", "w_down")
_COL_SHARDED = ("w_mod", "w_up")
_FULL_SHAPE = {"w_mod": (D_MODEL, MOD_W), "w_in": (IN_COLS, D_MODEL), "w_pa": (Q_W, D_MODEL), "w_pd": (GDN_W, D_MODEL),
               "w_out": (D_MODEL, D_MODEL), "w_up": (D_MODEL, 2 * D_FF), "w_down": (D_FF, D_MODEL)}


def _shard_shape(name):
    r, cdim = _FULL_SHAPE[name]
    return (r, cdim // N_CHIPS) if name in _COL_SHARDED else (r // N_CHIPS, cdim)


_CONV_ELEMS = 2 * (3 * CONV_W // N_CHIPS + 3 * 2 * D_FF // N_CHIPS)
_CONV_ROWS = 32


def _blocks_of_full(name, full):
    r, cdim = _FULL_SHAPE[name]
    if name in _COL_SHARDED:
        return full.reshape(r, N_CHIPS, cdim // N_CHIPS).transpose(1, 0, 2)
    return full.reshape(N_CHIPS, r // N_CHIPS, cdim)


def _full_of_blocks(name, blocks):
    r, cdim = _FULL_SHAPE[name]
    if name in _COL_SHARDED:
        return blocks.transpose(1, 0, 2).reshape(r, cdim)
    return blocks.reshape(r, cdim)


def _w_in_regroup(w_in_t):
    main = jnp.concatenate([w_in_t[:SMALL_AT], w_in_t[SMALL_AT + 4 * GDN_HEADS:]], axis=0)
    small = jnp.pad(w_in_t[SMALL_AT:SMALL_AT + 4 * GDN_HEADS], ((0, HEAD_DIM - 4 * GDN_HEADS), (0, 0)))
    return main, small


def _w_in_ungroup(main, small):
    return jnp.concatenate([main[:SMALL_AT], small[:4 * GDN_HEADS], main[SMALL_AT:]], axis=0)


_SMALL = ("c_ctx", "b_mod", "q_norm_w", "k_norm_w", "conv_qkv_w", "a_log", "dt_bias", "gdn_norm_w", "ffn_conv_w",
          "ffn_conv_b", "final_norm_w")


def _pack_small(tree, rows):
    flat = jnp.concatenate([tree[nm].reshape(-1) for nm in _SMALL])
    return jnp.pad(flat, (0, rows * 128 - flat.shape[0])).reshape(rows, 128)


def _unpack_small(packed, like):
    flat, out, off = packed.reshape(-1), {}, 0
    for nm in _SMALL:
        size = int(np.prod(like[nm].shape))
        out[nm] = flat[off:off + size].reshape(like[nm].shape)
        off += size
    return out


def kernel(x, c, ctx, c_ctx, w_mod, b_mod, w_in, q_norm_w, k_norm_w, conv_qkv_w, a_log, dt_bias, gdn_norm_w, w_pa, w_pd, w_out, w_up, ffn_conv_w, ffn_conv_b, w_down, final_norm_w, loss_target, m_c_ctx, m_w_mod, m_b_mod, m_w_in, m_q_norm_w, m_k_norm_w, m_conv_qkv_w, m_a_log, m_dt_bias, m_gdn_norm_w, m_w_pa, m_w_pd, m_w_out, m_w_up, m_ffn_conv_w, m_ffn_conv_b, m_w_down, m_final_norm_w, v_c_ctx, v_w_mod, v_b_mod, v_w_in, v_q_norm_w, v_k_norm_w, v_conv_qkv_w, v_a_log, v_dt_bias, v_gdn_norm_w, v_w_pa, v_w_pd, v_w_out, v_w_up, v_ffn_conv_w, v_ffn_conv_b, v_w_down, v_final_norm_w):
    names = ("c_ctx", "w_mod", "b_mod", "w_in", "q_norm_w", "k_norm_w", "conv_qkv_w", "a_log", "dt_bias", "gdn_norm_w",
             "w_pa", "w_pd", "w_out", "w_up", "ffn_conv_w", "ffn_conv_b", "w_down", "final_norm_w")
    w_sh = dict(c_ctx=c_ctx, w_mod=w_mod, b_mod=b_mod, w_in=w_in, q_norm_w=q_norm_w, k_norm_w=k_norm_w,
                conv_qkv_w=conv_qkv_w, a_log=a_log, dt_bias=dt_bias, gdn_norm_w=gdn_norm_w, w_pa=w_pa, w_pd=w_pd,
                w_out=w_out, w_up=w_up, ffn_conv_w=ffn_conv_w, ffn_conv_b=ffn_conv_b, w_down=w_down,
                final_norm_w=final_norm_w)
    m_sh = dict(c_ctx=m_c_ctx, w_mod=m_w_mod, b_mod=m_b_mod, w_in=m_w_in, q_norm_w=m_q_norm_w, k_norm_w=m_k_norm_w,
                conv_qkv_w=m_conv_qkv_w, a_log=m_a_log, dt_bias=m_dt_bias, gdn_norm_w=m_gdn_norm_w, w_pa=m_w_pa,
                w_pd=m_w_pd, w_out=m_w_out, w_up=m_w_up, ffn_conv_w=m_ffn_conv_w, ffn_conv_b=m_ffn_conv_b,
                w_down=m_w_down, final_norm_w=m_final_norm_w)
    v_sh = dict(c_ctx=v_c_ctx, w_mod=v_w_mod, b_mod=v_b_mod, w_in=v_w_in, q_norm_w=v_q_norm_w, k_norm_w=v_k_norm_w,
                conv_qkv_w=v_conv_qkv_w, a_log=v_a_log, dt_bias=v_dt_bias, gdn_norm_w=v_gdn_norm_w, w_pa=v_w_pa,
                w_pd=v_w_pd, w_out=v_w_out, w_up=v_w_up, ffn_conv_w=v_ffn_conv_w, ffn_conv_b=v_ffn_conv_b,
                w_down=v_w_down, final_norm_w=v_final_norm_w)
    chip = 2 * lax.axis_index("x") + lax.axis_index("y")

    conv_bits = jnp.concatenate([lax.bitcast_convert_type(w_sh[nm][0], BF16).reshape(-1)
                                 for nm in ("conv_qkv_w", "ffn_conv_w")])
    shards = {nm: w_sh[nm][0].astype(BF16).T if nm == "w_in" else w_sh[nm][0].astype(BF16) for nm in _BIG}
    shards["conv"] = jnp.pad(conv_bits, (0, _CONV_ROWS * D_MODEL - _CONV_ELEMS)).reshape(_CONV_ROWS, D_MODEL)
    gathered = all_gather_chips({nm: shards[nm] for nm in _EARLY + ("conv",)})
    gathered, late_shards = lax.optimization_barrier((gathered, {nm: shards[nm] for nm in _LATE}))
    started = push_start("ag_late_start", late_shards, {nm: (N_CHIPS,) + a.shape for nm, a in late_shards.items()},
                         _gather_copies, 4 * len(_LATE))
    c = c + started[-1][0:1, 0:1]

    wb = {nm: _full_of_blocks(nm, gathered[nm]) for nm in _EARLY}
    wb["w_in_main"], wb["w_in_small"] = _w_in_regroup(wb.pop("w_in"))
    conv_all = gathered["conv"].reshape(N_CHIPS, -1)[:, :_CONV_ELEMS]
    n_cq = 2 * 3 * CONV_W // N_CHIPS
    unbits = lambda a, w: lax.bitcast_convert_type(a.reshape(N_CHIPS, 3, w // N_CHIPS, 2), F32).transpose(1, 0, 2).reshape(3, w)
    ws = dict(c_ctx=c_ctx, b_mod=b_mod, q_norm_w=q_norm_w, k_norm_w=k_norm_w, a_log=a_log[0], dt_bias=dt_bias[0],
              gdn_norm_w=gdn_norm_w, ffn_conv_b=ffn_conv_b, final_norm_w=final_norm_w,
              conv_qkv_w=unbits(conv_all[:, :n_cq], CONV_W), ffn_conv_w=unbits(conv_all[:, n_cq:], 2 * D_FF))
    wz = {nm: jnp.zeros(a.shape, F32) for nm, a in wb.items()}
    wz.update({nm: jnp.zeros(_FULL_SHAPE[nm], F32) for nm in _LATE})

    mixed, vjp_mix = jax.vjp(lambda x_, wz_, ws_: token_mixing(x_, wz_, wb, ws_, c, ctx[0]), x[0], wz, ws)
    got = push_wait("ag_late_wait", started, _gather_copies, mixed["gdn_x"])
    wb_late = {nm: _full_of_blocks(nm, got[nm]) for nm in _LATE}
    loss_local, vjp_chan = jax.vjp(
        lambda mixed_, wz_, ws_: channel_mixing(mixed_, wz_, wb_late, ws_, loss_target[0]), mixed, wz, ws)
    d_mixed, gz_chan, gs_chan = vjp_chan(jnp.ones((), F32))

    sel = jnp.stack([lax.axis_index("c"), chip]).astype(jnp.int32)
    g32_late = {nm: _blocks_of_full(nm, gz_chan[nm]) for nm in _LATE}
    theirs_late = sibling_halves(g32_late, "rs_sibling_late")
    sums_late = {nm: _presum(nm, sel, g32_late[nm], theirs_late[nm]) for nm in _LATE}
    scattering = push_start("rs_late_start", sums_late, {nm: (3,) + a.shape[1:] for nm, a in sums_late.items()},
                            _scatter_copies, 3 * len(_LATE))
    d_mixed = {**d_mixed, "gdn_x": d_mixed["gdn_x"] + scattering[-1][0:1, 0:1]}
    gx, gz_mix, gs_mix = vjp_mix(d_mixed)
    gs = jax.tree.map(jnp.add, gs_mix, gs_chan)
    got_late = push_wait("rs_late_wait", scattering, _scatter_copies, gx)
    loss = lax.psum(loss_local, ("x", "y", "c"))

    gz_mix["w_in"] = _w_in_ungroup(gz_mix.pop("w_in_main"), gz_mix.pop("w_in_small"))
    g32 = {nm: _blocks_of_full(nm, gz_mix[nm]) for nm in _EARLY}
    theirs = sibling_halves(g32, "rs_sibling")
    got = scatter_halves({nm: _presum(nm, sel, g32[nm], theirs[nm]) for nm in _EARLY})
    g32.update(g32_late), theirs.update(theirs_late), got.update(got_late)
    g_big = sibling_assemble({nm: _finalsum(nm, sel, g32[nm], theirs[nm], got[nm]) for nm in _BIG})

    gs["a_log"], gs["dt_bias"] = gs["a_log"][None], gs["dt_bias"][None]
    like = {nm: gs[nm] for nm in _SMALL}
    small_rows = -(-sum(int(np.prod(like[nm].shape)) for nm in _SMALL) // 1024) * 8
    g_small = _unpack_small(all_reduce_small(_pack_small(gs, small_rows)), like)
    for nm, width in (("conv_qkv_w", CONV_W), ("ffn_conv_w", 2 * D_FF)):
        g_small[nm] = lax.dynamic_slice_in_dim(g_small[nm], chip * (width // N_CHIPS), width // N_CHIPS, axis=1)[None]

    grads, deltas, new_m, new_v = {}, {}, {}, {}
    for nm in _BIG:
        g = g_big[nm].T if nm == "w_in" else g_big[nm]
        grads[nm] = g[None]
        deltas[nm], new_m[nm], new_v[nm] = (o[None] for o in _adamw(w_sh[nm][0], g, m_sh[nm][0], v_sh[nm][0],
                                                                     "adamw_" + nm))
    shard_like = {nm: w_sh[nm] for nm in _SMALL}
    rows_l = -(-sum(int(np.prod(shard_like[nm].shape)) for nm in _SMALL) // 1024) * 8
    g_l = _pack_small({nm: g_small[nm].reshape(w_sh[nm].shape) for nm in _SMALL}, rows_l)
    outs = _adamw(_pack_small(w_sh, rows_l), g_l, _pack_small(m_sh, rows_l), _pack_small(v_sh, rows_l), "adamw_small")
    grads.update(_unpack_small(g_l, shard_like))
    for tree, packed in zip((deltas, new_m, new_v), outs):
        tree.update(_unpack_small(packed, shard_like))

    return (loss, gx[None], *[grads[nm] for nm in names], *[deltas[nm] for nm in names],
            *[new_m[nm] for nm in names], *[new_v[nm] for nm in names])
```

```python
import functools
import math

import jax
import jax.numpy as jnp
import numpy as np
from jax import lax
from jax.experimental import pallas as pl
from jax.experimental.pallas import tpu as pltpu

F32 = jnp.float32
BF16 = jnp.bfloat16
HIGHEST = lax.Precision.HIGHEST
MESH = pl.DeviceIdType.MESH

D_MODEL = 1024
GRID_W = 64
ATTN_HEADS = 8
ATTN_KV_HEADS = 2
ATTN_GROUP = ATTN_HEADS // ATTN_KV_HEADS
HEAD_DIM = 128
ROPE_THETA = 10000.0
GDN_HEADS = 8
GDN_CHUNK = 64
D_FF = 2816
NORM_EPS = 1e-6
KV_W = ATTN_KV_HEADS * HEAD_DIM
Q_W = ATTN_HEADS * HEAD_DIM
GDN_W = GDN_HEADS * HEAD_DIM
CONV_W = 3 * GDN_W
MOD_W = 6 * D_MODEL
IN_COLS = 2 * KV_W + CONV_W + 4 * GDN_HEADS + Q_W + GDN_W + 2 * D_MODEL
IN_MAIN = IN_COLS - 4 * GDN_HEADS
SMALL_AT = 2 * KV_W + CONV_W
N_CHIPS = 4
N_DEV = 8

ADAM_LR = 0.001
ADAM_B1 = 0.9
ADAM_B2 = 0.999
ADAM_EPS = 1e-08
ADAM_WD = 0.01
ADAM_STEP = 10

VMEM_LIMIT = 48 * 1024 * 1024
MATMUL_VMEM_BUDGET = 40 * 1024 * 1024
MATMUL_STEP_BYTES = 1200 * 1024


def _pick(dim, prefs):
    for p in prefs:
        if p <= dim and dim % p == 0:
            return p
    return dim


_DIMS = {
    "nn": (((1,), (0,)), ((), ())),
    "nt": (((1,), (1,)), ((), ())),
    "tn": (((0,), (0,)), ((), ())),
}


def _matmul_plan(m, n, k, a_bytes, b_bytes):
    best = None
    for tm in (2304, 2048, 1152, 1024, 768, 512, 384, 256, 128, m):
        for tn in (2560, 1536, 1408, 1024, 768, 512, 256, 128, n):
            for tk in (3840, 2816, 2560, 2304, 2048, 1920, 1408, 1152, 1024, 768, 512, 256, 128, k):
                if tm > m or tn > n or tk > k or m % tm or n % tn or k % tk:
                    continue
                blocks = tm * tk * a_bytes + tk * tn * b_bytes + tm * tn * 4
                casts = (tm * tk * 2 if a_bytes > 2 else 0) + (tk * tn * 2 if b_bytes > 2 else 0) + tm * tn * 4
                if 2 * blocks + casts > MATMUL_VMEM_BUDGET:
                    continue
                nm, nn, nk = m // tm, n // tn, k // tk
                size_a, size_b = m * k * a_bytes, k * n * b_bytes
                for n_inner in (True, False):
                    if n_inner:
                        traffic = (size_a if nk == 1 else nn * size_a) + nm * size_b
                    else:
                        traffic = nn * size_a + (size_b if nk == 1 else nm * size_b)
                    cost = traffic + nm * nn * nk * MATMUL_STEP_BYTES + (nk - 1) * m * n * 4
                    if best is None or cost < best[0]:
                        best = (cost, tm, tn, tk, n_inner)
    return best[1:]


def _matmul(a, b, mode, name):
    if mode == "nn":
        (m, k), (_, n) = a.shape, b.shape
    elif mode == "nt":
        (m, k), (n, _) = a.shape, b.shape
    else:
        (k, m), (_, n) = a.shape, b.shape
    tm, tn, tk, n_inner = _matmul_plan(m, n, k, a.dtype.itemsize, b.dtype.itemsize)
    nk = k // tk
    ij = (lambda g0, g1: (g0, g1)) if n_inner else (lambda g0, g1: (g1, g0))
    if mode == "tn":
        a_spec = pl.BlockSpec((tk, tm), lambda g0, g1, l: (l, ij(g0, g1)[0]))
    else:
        a_spec = pl.BlockSpec((tm, tk), lambda g0, g1, l: (ij(g0, g1)[0], l))
    if mode == "nt":
        b_spec = pl.BlockSpec((tn, tk), lambda g0, g1, l: (ij(g0, g1)[1], l))
    else:
        b_spec = pl.BlockSpec((tk, tn), lambda g0, g1, l: (l, ij(g0, g1)[1]))
    dims = _DIMS[mode]

    def body(a_ref, b_ref, o_ref):
        part = lax.dot_general(a_ref[...].astype(BF16), b_ref[...].astype(BF16), dims, preferred_element_type=F32)
        if nk == 1:
            o_ref[...] = part
        else:
            l = pl.program_id(2)

            @pl.when(l == 0)
            def _():
                o_ref[...] = part

            @pl.when(l > 0)
            def _():
                o_ref[...] += part

    return pl.pallas_call(
        body,
        name=name,
        grid=(m // tm, n // tn, nk) if n_inner else (n // tn, m // tm, nk),
        in_specs=[a_spec, b_spec],
        out_specs=pl.BlockSpec((tm, tn), lambda g0, g1, l: ij(g0, g1)),
        out_shape=jax.ShapeDtypeStruct((m, n), F32),
        compiler_params=pltpu.CompilerParams(dimension_semantics=("parallel", "parallel", "arbitrary"),
                                             vmem_limit_bytes=VMEM_LIMIT),
    )(a, b)


@functools.partial(jax.custom_vjp, nondiff_argnums=(3,))
def pmm(a, w, wz, name):
    del wz
    return _matmul(a, w, "nn", name + "_f")


def _pmm_fwd(a, w, wz, name):
    del wz
    return _matmul(a, w, "nn", name + "_f"), (a, w)


def _pmm_bwd(name, res, g):
    a, w = res
    da = _matmul(g, w, "nt", name + "_da")
    if a.shape[0] < 128:
        pad = 128 - a.shape[0]
        at = jnp.pad(a.T, ((0, 0), (0, pad)))
        gp = jnp.pad(g, ((0, pad), (0, 0)))
        dw = _matmul(at, gp, "nn", name + "_dw")
    else:
        dw = _matmul(a, g, "tn", name + "_dw")
    return da, jnp.zeros_like(w), dw


pmm.defvjp(_pmm_fwd, _pmm_bwd)


@functools.partial(jax.custom_vjp, nondiff_argnums=(3,))
def pmm_t(a, wt, wtz, name):
    del wtz
    return _matmul(a, wt, "nt", name + "_f")


def _pmm_t_fwd(a, wt, wtz, name):
    del wtz
    return _matmul(a, wt, "nt", name + "_f"), (a, wt)


def _pmm_t_bwd(name, res, g):
    a, wt = res
    return _matmul(g, wt, "nn", name + "_da"), jnp.zeros_like(wt), _matmul(g, a, "tn", name + "_dw")


pmm_t.defvjp(_pmm_t_fwd, _pmm_t_bwd)


def rowop(fn, name, rows, bcs=(), crows=(), cbcs=(), tr=256):
    rows, bcs, crows, cbcs = tuple(rows), tuple(bcs), tuple(crows), tuple(cbcs)
    n_rows = rows[0].shape[0]
    tr = _pick(n_rows, (tr, 128, 64, 32, 16, 8))
    nr, nb, ncr, ncb = len(rows), len(bcs), len(crows), len(cbcs)
    n_in = nr + nb + ncr + ncb
    grid = (n_rows // tr,)

    def blk(arr):
        return jax.ShapeDtypeStruct((tr, arr.shape[1]), arr.dtype)

    def row_spec(arr):
        return pl.BlockSpec((tr, arr.shape[1]), lambda i: (i, 0))

    def bc_spec(arr):
        return pl.BlockSpec(arr.shape, lambda i: (0, 0))

    out_blk = jax.eval_shape(fn, *[blk(r) for r in rows], *bcs, *[blk(r) for r in crows], *cbcs)
    n_out = len(out_blk)
    out_shape = tuple(jax.ShapeDtypeStruct((n_rows, o.shape[1]), o.dtype) for o in out_blk)
    in_specs = ([row_spec(r) for r in rows] + [bc_spec(b) for b in bcs]
                + [row_spec(r) for r in crows] + [bc_spec(b) for b in cbcs])

    def order(vals):
        return vals

    def fwd_call(args):
        def body(*refs):
            vals = [r[...] for r in refs[:n_in]]
            res = fn(*order(vals))
            for o_ref, r in zip(refs[n_in:], res):
                o_ref[...] = r

        return pl.pallas_call(
            body, name=name + "_f", grid=grid, in_specs=in_specs,
            out_specs=[row_spec(o) for o in out_shape], out_shape=out_shape,
            compiler_params=pltpu.CompilerParams(dimension_semantics=("parallel",), vmem_limit_bytes=VMEM_LIMIT),
        )(*args)

    def bwd_call(args, cts):
        def body(*refs):
            vals = [r[...] for r in refs[:n_in]]
            ct_refs = refs[n_in:n_in + n_out]
            d_rows = refs[n_in + n_out:n_in + n_out + nr]
            d_bcs = refs[n_in + n_out + nr:]
            consts = vals[nr + nb:]
            _, vjp = jax.vjp(lambda *p: fn(*p, *consts), *vals[:nr + nb])
            grads = vjp(tuple(c[...] for c in ct_refs))
            for ref, g in zip(d_rows, grads[:nr]):
                ref[...] = g

            @pl.when(pl.program_id(0) == 0)
            def _():
                for ref in d_bcs:
                    ref[...] = jnp.zeros_like(ref)

            for ref, g in zip(d_bcs, grads[nr:]):
                ref[...] += g

        d_shape = tuple(jax.ShapeDtypeStruct(r.shape, r.dtype) for r in rows + bcs)
        return pl.pallas_call(
            body, name=name + "_b", grid=grid,
            in_specs=in_specs + [row_spec(o) for o in out_shape],
            out_specs=[row_spec(r) for r in rows] + [bc_spec(b) for b in bcs], out_shape=d_shape,
            compiler_params=pltpu.CompilerParams(dimension_semantics=("arbitrary",), vmem_limit_bytes=VMEM_LIMIT),
        )(*args, *cts)

    @jax.custom_vjp
    def op(diff, const):
        return fwd_call(diff + const)

    def op_fwd(diff, const):
        return fwd_call(diff + const), (diff, const)

    def op_bwd(res, cts):
        diff, const = res
        grads = bwd_call(diff + const, tuple(cts))
        return tuple(grads), tuple(jnp.zeros_like(c) for c in const)

    op.defvjp(op_fwd, op_bwd)
    return op(rows + bcs, crows + cbcs)


def colop(fn, name, arrays, uses, n_const, nblk, cw=128):
    arrays = tuple(arrays)
    n_diff = len(arrays) - n_const
    nd = sum(1 for u in uses if u[0] < n_diff)
    assert all(u[0] < n_diff for u in uses[:nd]) and all(u[0] >= n_diff for u in uses[nd:])

    def spec(u):
        return pl.BlockSpec((arrays[u[0]].shape[0], cw), lambda j, off=u[1]: (0, off + j))

    def out_spec(rows):
        return pl.BlockSpec((rows, cw), lambda j: (0, j))

    out_blk = jax.eval_shape(fn, *[jax.ShapeDtypeStruct((arrays[u[0]].shape[0], cw), arrays[u[0]].dtype)
                                   for u in uses])
    out_shape = tuple(jax.ShapeDtypeStruct((o.shape[0], nblk * cw), o.dtype) for o in out_blk)
    params = pltpu.CompilerParams(dimension_semantics=("parallel",), vmem_limit_bytes=VMEM_LIMIT)

    def fwd_call(arrs):
        def body(*refs):
            res = fn(*[r[...] for r in refs[:len(uses)]])
            for o_ref, r in zip(refs[len(uses):], res):
                o_ref[...] = r

        return pl.pallas_call(
            body, name=name + "_f", grid=(nblk,), in_specs=[spec(u) for u in uses],
            out_specs=[out_spec(o.shape[0]) for o in out_shape], out_shape=out_shape, compiler_params=params,
        )(*[arrs[u[0]] for u in uses])

    def bwd_call(arrs, cts):
        def body(*refs):
            vals = [r[...] for r in refs[:len(uses)]]
            ct_refs = refs[len(uses):len(uses) + len(out_shape)]
            _, vjp = jax.vjp(lambda *p: fn(*p, *vals[nd:]), *vals[:nd])
            for ref, g in zip(refs[len(uses) + len(out_shape):], vjp(tuple(c[...] for c in ct_refs))):
                ref[...] = g

        d_shape = tuple(jax.ShapeDtypeStruct((arrays[u[0]].shape[0], nblk * cw), F32) for u in uses[:nd])
        return pl.pallas_call(
            body, name=name + "_b", grid=(nblk,),
            in_specs=[spec(u) for u in uses] + [out_spec(o.shape[0]) for o in out_shape],
            out_specs=[out_spec(s.shape[0]) for s in d_shape], out_shape=d_shape, compiler_params=params,
        )(*[arrs[u[0]] for u in uses], *cts)

    @jax.custom_vjp
    def op(diff, const):
        return fwd_call(diff + const)

    def op_fwd(diff, const):
        return fwd_call(diff + const), (diff, const)

    def op_bwd(res, cts):
        diff, const = res
        d_uses = bwd_call(diff + const, tuple(cts))
        grads = []
        for i in range(n_diff):
            parts = sorted([(u[1], k) for k, u in enumerate(uses[:nd]) if u[0] == i])
            grads.append(d_uses[parts[0][1]] if len(parts) == 1
                         else jnp.concatenate([d_uses[k] for _, k in parts], axis=1))
        return tuple(grads), tuple(jnp.zeros_like(c) for c in const)

    op.defvjp(op_fwd, op_bwd)
    return op(arrays[:n_diff], arrays[n_diff:])


@functools.partial(jax.custom_vjp, nondiff_argnums=(1,))
def _roll_rows(x, k):
    return pltpu.roll(x, k % x.shape[0], 0)


def _roll_rows_fwd(x, k):
    return _roll_rows(x, k), None


def _roll_rows_bwd(k, _, g):
    return (_roll_rows(g, -k),)


_roll_rows.defvjp(_roll_rows_fwd, _roll_rows_bwd)


def _conv3(x, w0, w1, w2, starts):
    rows = lax.broadcasted_iota(jnp.int32, x.shape, 0)
    ends = tuple(s - 1 for s in starts[1:]) + (x.shape[0] - 1,)
    first = functools.reduce(jnp.logical_or, [rows == s for s in starts])
    last = functools.reduce(jnp.logical_or, [rows == e for e in ends])
    prev = jnp.where(first, 0.0, _roll_rows(x, 1))
    nxt = jnp.where(last, 0.0, _roll_rows(x, -1))
    return prev * w0 + x * w1 + nxt * w2


def _rms(x):
    return x * lax.rsqrt(jnp.mean(x * x, axis=-1, keepdims=True) + NORM_EPS)


def _heads(x, n):
    return [x[:, h * HEAD_DIM:(h + 1) * HEAD_DIM] for h in range(n)]


_NT = (((1,), (1,)), ((), ()))
_TN = (((0,), (0,)), ((), ()))
_TQ = 256


def _attn_probs(q, k):
    s = lax.dot_general(q, k, _NT, preferred_element_type=F32) * (HEAD_DIM ** -0.5)
    p = jnp.exp(s - jnp.max(s, axis=-1, keepdims=True))
    return p * (1.0 / jnp.sum(p, axis=-1, keepdims=True))


def _attn_fwd_call(q, k, v):
    n, t = q.shape[0], k.shape[0]
    tq = _pick(n, (_TQ, 128))

    def body(q_ref, k_ref, v_ref, o_ref):
        p = _attn_probs(q_ref[...].astype(BF16), k_ref[...].astype(BF16))
        o_ref[...] = jnp.dot(p.astype(BF16), v_ref[...].astype(BF16), preferred_element_type=F32)

    return pl.pallas_call(
        body, name="attn_f", grid=(ATTN_HEADS, n // tq),
        in_specs=[pl.BlockSpec((tq, HEAD_DIM), lambda h, i: (i, h)),
                  pl.BlockSpec((t, HEAD_DIM), lambda h, i: (0, h // ATTN_GROUP)),
                  pl.BlockSpec((t, HEAD_DIM), lambda h, i: (0, h // ATTN_GROUP))],
        out_specs=pl.BlockSpec((tq, HEAD_DIM), lambda h, i: (i, h)),
        out_shape=jax.ShapeDtypeStruct(q.shape, F32),
        compiler_params=pltpu.CompilerParams(dimension_semantics=("parallel", "parallel"),
                                             vmem_limit_bytes=VMEM_LIMIT),
    )(q, k, v)


def _attn_bwd_call(q, k, v, do):
    n, t = q.shape[0], k.shape[0]
    tq = _pick(n, (_TQ, 128))

    def body(q_ref, k_ref, v_ref, do_ref, dq_ref, dk_ref, dv_ref):
        @pl.when((pl.program_id(1) == 0) & (pl.program_id(2) == 0))
        def _():
            dk_ref[...] = jnp.zeros_like(dk_ref)
            dv_ref[...] = jnp.zeros_like(dv_ref)

        qb, kb, vb, dob = (r[...].astype(BF16) for r in (q_ref, k_ref, v_ref, do_ref))
        p = _attn_probs(qb, kb)
        dp = lax.dot_general(dob, vb, _NT, preferred_element_type=F32)
        ds = p * (dp - jnp.sum(p * dp, axis=-1, keepdims=True)) * (HEAD_DIM ** -0.5)
        dsb = ds.astype(BF16)
        dq_ref[...] = jnp.dot(dsb, kb, preferred_element_type=F32)
        dk_ref[...] += lax.dot_general(dsb, qb, _TN, preferred_element_type=F32)
        dv_ref[...] += lax.dot_general(p.astype(BF16), dob, _TN, preferred_element_type=F32)

    q_spec = pl.BlockSpec((tq, HEAD_DIM), lambda kh, g, i: (i, kh * ATTN_GROUP + g))
    kv_spec = pl.BlockSpec((t, HEAD_DIM), lambda kh, g, i: (0, kh))
    return pl.pallas_call(
        body, name="attn_b", grid=(ATTN_KV_HEADS, ATTN_GROUP, n // tq),
        in_specs=[q_spec, kv_spec, kv_spec, q_spec],
        out_specs=[q_spec, kv_spec, kv_spec],
        out_shape=(jax.ShapeDtypeStruct(q.shape, F32), jax.ShapeDtypeStruct(k.shape, F32),
                   jax.ShapeDtypeStruct(v.shape, F32)),
        compiler_params=pltpu.CompilerParams(dimension_semantics=("parallel", "arbitrary", "arbitrary"),
                                             vmem_limit_bytes=VMEM_LIMIT),
    )(q, k, v, do)


@jax.custom_vjp
def attention(q, k, v):
    return _attn_fwd_call(q, k, v)


def _attention_fwd(q, k, v):
    return _attn_fwd_call(q, k, v), (q, k, v)


def _attention_bwd(res, do):
    return _attn_bwd_call(*res, do)


attention.defvjp(_attention_fwd, _attention_bwd)


_C = GDN_CHUNK


def _pdot(a, b):
    return jnp.dot(a, b, precision=lax.Precision.HIGH, preferred_element_type=F32)


@jax.custom_vjp
def _hdot(a, b):
    return jnp.dot(a.astype(BF16), b.astype(BF16), preferred_element_type=F32)


def _hdot_fwd(a, b):
    return _hdot(a, b), (a, b)


def _hdot_bwd(res, g):
    a, b = res
    gb = g.astype(BF16)
    return (lax.dot_general(gb, b.astype(BF16), _NT, preferred_element_type=F32),
            lax.dot_general(a.astype(BF16), gb, _TN, preferred_element_type=F32))


_hdot.defvjp(_hdot_fwd, _hdot_bwd)


def _each(fn, *lists):
    return [fn(*args) for args in zip(*lists)]


def _unit_lower_inverse(low, blockdiag):
    eye = (lax.broadcasted_iota(jnp.int32, (_C, _C), 0) == lax.broadcasted_iota(jnp.int32, (_C, _C), 1)).astype(F32)
    ld = _each(lambda a: a * blockdiag, low)
    lo = _each(lambda a, d: a - d, low, ld)
    l2 = _each(_hdot, ld, ld)
    l4 = _each(_hdot, l2, l2)
    l8 = _each(_hdot, l4, l4)
    td = _each(lambda d, a2: _hdot(eye - d, eye + a2), ld, l2)
    td = _each(lambda t, a4: _hdot(t, eye + a4), td, l4)
    td = _each(lambda t, a8: _hdot(t, eye + a8), td, l8)
    nn = _each(_hdot, td, lo)
    n2 = _each(_hdot, nn, nn)
    out = _each(lambda n, m2: _hdot(eye - n, eye + m2), nn, n2)
    return _each(_hdot, out, td)


def _gdn_chunks(heads, blockdiag):
    q, k, v, b_b, be_b, e_b, kd_b, m1, dec, gl, s = (list(col) for col in zip(*heads))
    f32dot = lambda a, b: jnp.dot(a, b, preferred_element_type=F32)
    nt = lambda a, b: lax.dot_general(a, b, _NT, preferred_element_type=F32)
    kq_k = _each(lambda kx, qq: nt(jnp.concatenate([kx, qq], axis=0), kx), k, q)
    t_inv = _unit_lower_inverse(_each(lambda m, a: m * a[:_C], m1, kq_k), blockdiag)
    uw = _each(lambda t, b, x, be, kx: _hdot(t, jnp.concatenate([b * x, be * kx], axis=1)), t_inv, b_b, v, be_b, k)
    wq_s = _each(lambda a, qq, e, ss: f32dot(jnp.concatenate([a[:, HEAD_DIM:], qq * e], axis=0), ss), uw, q, e_b, s)
    delta = _each(lambda a, ws: a[:, :HEAD_DIM] - ws[:_C], uw, wq_s)
    p = _each(lambda d, a: d * a[_C:], dec, kq_k)
    o = _each(lambda ws, pp, dd: ws[_C:] + f32dot(pp, dd), wq_s, p, delta)
    s_new = _each(lambda g, ss, kx, kd, dd: g * ss + lax.dot_general(kx * kd, dd, _TN, preferred_element_type=F32),
                  gl, s, k, kd_b, delta)
    return o, s_new


def _blockdiag_mask():
    r = lax.broadcasted_iota(jnp.int32, (_C, _C), 0) >> 4
    c = lax.broadcasted_iota(jnp.int32, (_C, _C), 1) >> 4
    return (r == c).astype(F32)


def _gdn_specs(nc, ncc, reverse, backward):
    def ch(s):
        s = nc - 1 - s if backward else s
        return jnp.where(s < ncc, ncc - 1 - s, nc + ncc - 1 - s) if reverse else s

    tok = pl.BlockSpec((_C, 3 * GDN_W), lambda s: (ch(s), 0))
    park = nc - ncc - 1 if reverse else 0
    out = pl.BlockSpec((_C, GDN_W), lambda s: (jnp.where(ch(s) >= ncc, ch(s) - ncc, park), 0))
    per_tok = pl.BlockSpec((GDN_HEADS, _C, HEAD_DIM), lambda s: (0, ch(s), 0))
    mat = pl.BlockSpec((GDN_HEADS, None, _C, _C), lambda s: (0, ch(s), 0, 0))
    row = pl.BlockSpec((GDN_HEADS, None, 1, HEAD_DIM), lambda s: (0, ch(s), 0, 0))
    state = pl.BlockSpec((GDN_HEADS, None, HEAD_DIM, HEAD_DIM), lambda s: (0, ch(s), 0, 0))
    return tok, out, per_tok, mat, row, state, ch


def _head_cols(h, part):
    return slice((part * GDN_HEADS + h) * HEAD_DIM, (part * GDN_HEADS + h + 1) * HEAD_DIM)


def _gdn_heads(qkv_ref, factor_refs, state_ref, first):
    return [[qkv_ref[:, _head_cols(h, 0)], qkv_ref[:, _head_cols(h, 1)], qkv_ref[:, _head_cols(h, 2)]]
            + [r[h] for r in factor_refs] + [state_ref[first + h]] for h in range(GDN_HEADS)]


def _gdn_fwd_call(ncc, qkv, factors):
    t = qkv.shape[0]
    nc, nd = t // _C, len(factors)
    specs = [_gdn_specs(nc, ncc, d == 1, False) for d in range(nd)]

    def body(*refs):
        ins_d = [refs[8 * d:8 * d + 8] for d in range(nd)]
        outs_d = [refs[8 * nd + 2 * d:8 * nd + 2 * d + 2] for d in range(nd)]
        s_ref = refs[-1]

        @pl.when(pl.program_id(0) == 0)
        def _():
            s_ref[...] = jnp.zeros_like(s_ref)

        heads = sum([_gdn_heads(r[0], r[1:], s_ref, GDN_HEADS * d) for d, r in enumerate(ins_d)], [])
        o, s_new = _gdn_chunks(heads, _blockdiag_mask())
        for d, (o_ref, sall_ref) in enumerate(outs_d):
            for h in range(GDN_HEADS):
                i = GDN_HEADS * d + h
                sall_ref[h] = heads[i][10]
                o_ref[:, _head_cols(h, 0)] = o[i]
                s_ref[i] = s_new[i]

    in_specs, out_specs, out_shape, operands = [], [], [], []
    for (tok, out, per_tok, mat, row, state, _), f in zip(specs, factors):
        in_specs += [tok, per_tok, per_tok, per_tok, per_tok, mat, mat, row]
        out_specs += [out, state]
        out_shape += [jax.ShapeDtypeStruct((t - ncc * _C, GDN_W), F32),
                      jax.ShapeDtypeStruct((GDN_HEADS, nc, HEAD_DIM, HEAD_DIM), F32)]
        operands += [qkv, *f]
    res = pl.pallas_call(
        body, name="gdn_f", grid=(nc,), in_specs=in_specs, out_specs=out_specs, out_shape=out_shape,
        scratch_shapes=[pltpu.VMEM((nd * GDN_HEADS, HEAD_DIM, HEAD_DIM), F32)],
        compiler_params=pltpu.CompilerParams(dimension_semantics=("arbitrary",), vmem_limit_bytes=VMEM_LIMIT),
    )(*operands)
    return [(res[2 * d], res[2 * d + 1]) for d in range(nd)]


def _gdn_bwd_call(ncc, qkv, factors, salls, dos):
    t = qkv.shape[0]
    nc, nd = t // _C, len(factors)
    specs = [_gdn_specs(nc, ncc, d == 1, True) for d in range(nd)]

    def body(*refs):
        ins_d = [refs[10 * d:10 * d + 10] for d in range(nd)]
        outs_d = [refs[10 * nd + 8 * d:10 * nd + 8 * d + 8] for d in range(nd)]
        ds_ref = refs[-1]

        @pl.when(pl.program_id(0) == 0)
        def _():
            ds_ref[...] = jnp.zeros_like(ds_ref)

        bd = _blockdiag_mask()
        heads = sum([_gdn_heads(r[0], r[1:8], r[8], 0) for r in ins_d], [])
        _, vjp = jax.vjp(lambda hs: _gdn_chunks(hs, bd), heads)
        live = [spec[6](pl.program_id(0)) >= ncc for spec in specs]
        (all_grads,) = vjp(([jnp.where(live[d], r[9][:, _head_cols(h, 0)], 0.0)
                             for d, r in enumerate(ins_d) for h in range(GDN_HEADS)],
                            [ds_ref[i] for i in range(nd * GDN_HEADS)]))
        for d, out_refs in enumerate(outs_d):
            for h in range(GDN_HEADS):
                grads = all_grads[GDN_HEADS * d + h]
                for part in range(3):
                    out_refs[0][:, _head_cols(h, part)] = grads[part]
                for ref, g in zip(out_refs[1:], grads[3:10]):
                    ref[h] = g
                ds_ref[GDN_HEADS * d + h] = grads[10]

    shp = lambda a: jax.ShapeDtypeStruct(a.shape, F32)
    in_specs, out_specs, out_shape, operands = [], [], [], []
    for (tok, out, per_tok, mat, row, state, _), f, sall, do in zip(specs, factors, salls, dos):
        in_specs += [tok, per_tok, per_tok, per_tok, per_tok, mat, mat, row, state, out]
        out_specs += [tok, per_tok, per_tok, per_tok, per_tok, mat, mat, row]
        out_shape += [shp(qkv)] + [shp(a) for a in f]
        operands += [qkv, *f, sall, do]
    res = pl.pallas_call(
        body, name="gdn_b", grid=(nc,), in_specs=in_specs, out_specs=out_specs, out_shape=out_shape,
        scratch_shapes=[pltpu.VMEM((nd * GDN_HEADS, HEAD_DIM, HEAD_DIM), F32)],
        compiler_params=pltpu.CompilerParams(dimension_semantics=("arbitrary",), vmem_limit_bytes=VMEM_LIMIT),
    )(*operands)
    return [res[8 * d:8 * d + 8] for d in range(nd)]


@functools.partial(jax.custom_vjp, nondiff_argnums=(0,))
def gdn_scan(ncc, qkv, f_fwd, f_rev):
    (o0, _), (o1, _) = _gdn_fwd_call(ncc, qkv, [f_fwd, f_rev])
    return o0, o1


def _gdn_scan_fwd(ncc, qkv, f_fwd, f_rev):
    (o0, s0), (o1, s1) = _gdn_fwd_call(ncc, qkv, [f_fwd, f_rev])
    return (o0, o1), (qkv, f_fwd, f_rev, s0, s1)


def _gdn_scan_bwd(ncc, res, dos):
    qkv, f_fwd, f_rev, s0, s1 = res
    g0, g1 = _gdn_bwd_call(ncc, qkv, [f_fwd, f_rev], [s0, s1], list(dos))
    return g0[0] + g1[0], tuple(g0[1:]), tuple(g1[1:])


gdn_scan.defvjp(_gdn_scan_fwd, _gdn_scan_bwd)


def _rope_tables(n, cl):
    t = np.arange(n)
    inv_freq = (ROPE_THETA ** (-np.arange(0, HEAD_DIM // 2, 2, dtype=np.float32) / (HEAD_DIM // 2))).astype(np.float32)
    ang_r = (t // GRID_W).astype(np.float32)[:, None] * inv_freq
    ang_c = (t % GRID_W).astype(np.float32)[:, None] * inv_freq
    cos = np.concatenate([np.cos(ang_r), np.cos(ang_r), np.cos(ang_c), np.cos(ang_c)], axis=1)
    sin = np.concatenate([-np.sin(ang_r), np.sin(ang_r), -np.sin(ang_c), np.sin(ang_c)], axis=1)
    cos_all = np.concatenate([np.ones((cl, HEAD_DIM), np.float32), cos], axis=0)
    sin_all = np.concatenate([np.zeros((cl, HEAD_DIM), np.float32), sin], axis=0)
    j = np.arange(HEAD_DIM)
    src = np.where((j % 64) < 32, j + 32, j - 32)
    perm = np.zeros((HEAD_DIM, HEAD_DIM), np.float32)
    perm[src, j] = 1.0
    return (jnp.asarray(cos.astype(np.float32)), jnp.asarray(sin.astype(np.float32)),
            jnp.asarray(cos_all), jnp.asarray(sin_all), jnp.asarray(perm))


def _gdn_factors(log_a, beta, reverse):
    t = log_a.shape[0]
    nc = t // _C
    la = log_a.reshape(nc, _C, GDN_HEADS).transpose(2, 0, 1)
    be = beta.reshape(nc, _C, GDN_HEADS).transpose(2, 0, 1)
    gam = lax.cumsum(la, axis=2, reverse=reverse)
    idx = jnp.arange(_C)
    incl = (idx[:, None] <= idx[None, :]) if reverse else (idx[:, None] >= idx[None, :])
    strict = (idx[:, None] < idx[None, :]) if reverse else (idx[:, None] > idx[None, :])
    dec = jnp.exp(jnp.where(incl, gam[..., :, None] - gam[..., None, :], -jnp.inf))
    m1 = jnp.where(strict, be[..., :, None] * dec, 0.0)
    e = jnp.exp(gam)
    g_last = gam[..., :1] if reverse else gam[..., -1:]
    lanes = lambda a: jnp.broadcast_to(a.reshape(GDN_HEADS, t, 1), (GDN_HEADS, t, HEAD_DIM))
    gl = jnp.broadcast_to(jnp.exp(g_last)[..., None], (GDN_HEADS, nc, 1, HEAD_DIM))
    return lanes(be), lanes(be * e), lanes(e), lanes(jnp.exp(g_last - gam)), m1, dec, gl


def local_loss(x, wz, wb, ws, c, ctx, target):
    return channel_mixing(token_mixing(x, wz, wb, ws, c, ctx), wz, wb, ws, target)


def token_mixing(x, wz, wb, ws, c, ctx):
    n, cl = x.shape[0], ctx.shape[0]
    cos_q, sin_q, cos_k, sin_k, perm = _rope_tables(n, cl)

    sc_in = jnp.concatenate([jax.nn.silu(c), jax.nn.silu(ws["c_ctx"])[None, :], jnp.zeros((14, D_MODEL), F32)], axis=0)
    mod = pmm(sc_in, wb["w_mod"], wz["w_mod"], "mm_mod") + ws["b_mod"]
    sh1, sc1, g1, sh2, sc2, g2 = [mod[0:1, i * D_MODEL:(i + 1) * D_MODEL] for i in range(6)]
    csh1, csc1 = mod[1:2, 0:D_MODEL], mod[1:2, D_MODEL:2 * D_MODEL]

    def norm_mod(a, sh, sc):
        return (_rms(a) * (1.0 + sc) + sh,)

    (hx,) = rowop(norm_mod, "normmod_x", (x,), (sh1, sc1))
    (hc,) = rowop(norm_mod, "normmod_c", (ctx,), (csh1, csc1))
    h_all = jnp.concatenate([hc, hx], axis=0)
    p_main = pmm_t(h_all, wb["w_in_main"], wz["w_in_main"], "mm_in")
    p_small = pmm_t(h_all, wb["w_in_small"], wz["w_in_small"], "mm_ins")
    ak, av, qkv, aq, z, gate = jnp.split(p_main, [KV_W, 2 * KV_W, SMALL_AT, SMALL_AT + Q_W, SMALL_AT + Q_W + GDN_W],
                                         axis=1)
    db, da = p_small[:, :2 * GDN_HEADS], p_small[:, 2 * GDN_HEADS:4 * GDN_HEADS]

    def qk_prep(nh):
        def fn(a, w, cos, sin, pm):
            outs = []
            for ah in _heads(a, nh):
                y = _rms(ah) * w
                outs.append(y * cos + _pdot(y, pm) * sin)
            return (jnp.concatenate(outs, axis=1),)
        return fn

    (q_x,) = rowop(qk_prep(ATTN_HEADS), "q_prep", (aq[cl:],), (ws["q_norm_w"],), (cos_q, sin_q), (perm,))
    (k_all,) = rowop(qk_prep(ATTN_KV_HEADS), "k_prep", (ak,), (ws["k_norm_w"],), (cos_k, sin_k), (perm,))
    attn_x = attention(q_x, k_all, av)

    cw = ws["conv_qkv_w"]
    normed = jnp.asarray(np.repeat([1.0, 1.0, 0.0], GDN_W)[None, :], F32)
    scale = jnp.asarray(np.repeat([HEAD_DIM ** -0.5, 1.0, 1.0], GDN_W)[None, :], F32)

    def gdn_prep(a, w0, w1, w2, nf, sc):
        s = jax.nn.silu(_conv3(a, w0, w1, w2, (0, cl)))
        inv = lax.rsqrt(jnp.sum(s * s, axis=-1, keepdims=True) + NORM_EPS)
        return (s * jnp.where(nf > 0.0, inv * sc, 1.0),)

    (qkvn,) = colop(gdn_prep, "gdn_prep", (qkv, cw[0:1], cw[1:2], cw[2:3], normed, scale),
                    [(i, 0) for i in range(6)], 2, 3 * GDN_HEADS)
    beta = jax.nn.sigmoid(db).reshape(-1, 2, GDN_HEADS)
    log_a = -jnp.exp(ws["a_log"])[None] * jax.nn.softplus(da.reshape(-1, 2, GDN_HEADS) + ws["dt_bias"][None])
    o_fwd, o_rev = gdn_scan(cl // _C, qkvn, _gdn_factors(log_a[:, 0], beta[:, 0], False),
                            _gdn_factors(log_a[:, 1], beta[:, 1], True))
    o_x = o_fwd + o_rev

    def gdn_out(o, zz, w):
        outs = [_rms(oh) * w * jax.nn.silu(zh) for oh, zh in zip(_heads(o, GDN_HEADS), _heads(zz, GDN_HEADS))]
        return (jnp.concatenate(outs, axis=1),)

    (gdn_x,) = rowop(gdn_out, "gdn_out", (o_x, z[cl:]), (ws["gdn_norm_w"],))
    return dict(x=x, attn_x=attn_x, gdn_x=gdn_x, gate=gate[cl:], g1=g1, sh2=sh2, sc2=sc2, g2=g2)


def channel_mixing(mixed, wz, wb, ws, target):
    x, attn_x, gdn_x, gate = mixed["x"], mixed["attn_x"], mixed["gdn_x"], mixed["gate"]
    g1, sh2, sc2, g2 = mixed["g1"], mixed["sh2"], mixed["sc2"], mixed["g2"]
    pa = pmm(attn_x, wb["w_pa"], wz["w_pa"], "mm_pa")
    pd = pmm(gdn_x, wb["w_pd"], wz["w_pd"], "mm_pd")

    def merge(a, d, g):
        return (jax.nn.sigmoid(g[:, :D_MODEL]) * a + jax.nn.sigmoid(g[:, D_MODEL:]) * d,)

    (y,) = rowop(merge, "merge", (pa, pd, gate))
    mo = pmm(y, wb["w_out"], wz["w_out"], "mm_out")

    def res_norm_mod(xx, m, g, sh, sc):
        x1 = xx + g * m
        return x1, _rms(x1) * (1.0 + sc) + sh

    x1, h2 = rowop(res_norm_mod, "res1", (x, mo), (g1, sh2, sc2))
    up = pmm(h2, wb["w_up"], wz["w_up"], "mm_up")
    fw = ws["ffn_conv_w"]

    def ffn_act(ug, uv, w0g, w0v, w1g, w1v, w2g, w2v, bg, bv):
        g = _conv3(ug, w0g, w1g, w2g, (0,)) + bg
        v = _conv3(uv, w0v, w1v, w2v, (0,)) + bv
        return (jax.nn.silu(g) * v,)

    half = D_FF // HEAD_DIM
    (act,) = colop(ffn_act, "ffn_act", (up, fw[0:1], fw[1:2], fw[2:3], ws["ffn_conv_b"]),
                   [(i, off) for i in range(5) for off in (0, half)], 0, half)
    dn = pmm(act, wb["w_down"], wz["w_down"], "mm_down")

    def head(xx, m, g, w, tgt):
        yy = _rms(xx + g * m) * w
        err = (yy - tgt) ** 2
        return (jnp.broadcast_to(0.5 * jnp.mean(err, axis=-1, keepdims=True), (xx.shape[0], HEAD_DIM)),)

    (row_loss,) = rowop(head, "head", (x1, dn), (g2, ws["final_norm_w"][None, :]), (target,))
    return jnp.sum(row_loss[:, 0])


_HBM = pl.BlockSpec(memory_space=pltpu.HBM)


def _chip_peers():
    x, y = lax.axis_index("x"), lax.axis_index("y")
    return [(1 - x, y), (x, 1 - y), (1 - x, 1 - y)]


_SPLIT_COLS = ("w_in",)


def _half_of(view, nm, idx, lead=0):
    r, cdim = view.shape[-2:]
    pre = (slice(None),) * lead
    if nm in _SPLIT_COLS:
        return view.at[pre + (slice(None), pl.ds(pl.multiple_of(idx * (cdim // 2), 128), cdim // 2))]
    return view.at[pre + (pl.ds(pl.multiple_of(idx * (r // 2), 16), r // 2), slice(None))]


def _remote(src, dst, send_sem, recv_sem, dev):
    return pltpu.make_async_remote_copy(src_ref=src, dst_ref=dst, send_sem=send_sem, recv_sem=recv_sem,
                                        device_id=dev, device_id_type=MESH)


def _hbm_call(body, name, ins, out_shape, n_sems, in_place=False):
    names = tuple(ins)
    return dict(zip(names, pl.pallas_call(
        body, name=name, in_specs=[_HBM] * len(names), out_specs=[_HBM] * len(names),
        out_shape=[out_shape(nm, ins[nm]) for nm in names],
        scratch_shapes=[pltpu.SemaphoreType.DMA((k,)) for k in n_sems],
        input_output_aliases={i: i for i in range(len(names))} if in_place else {},
    )(*[ins[nm] for nm in names])))


def all_gather_chips(shards):
    names = tuple(shards)
    n = len(names)

    def body(*refs):
        ins, outs = dict(zip(names, refs[:n])), dict(zip(names, refs[n:2 * n]))
        ici_send, ici_recv, d2d_send, d2d_recv, own_send, own_recv = refs[2 * n:]
        x, y, c = lax.axis_index("x"), lax.axis_index("y"), lax.axis_index("c")
        me, sib = 2 * x + y, (x, y, 1 - c)
        own = [_remote(ins[nm], outs[nm].at[me], own_send.at[i], own_recv.at[i], sib) for i, nm in enumerate(names)]
        for cp in own:
            cp.start()
        sends = []
        for k, (px, py) in enumerate(_chip_peers()):
            for i, nm in enumerate(names):
                cp = _remote(_half_of(ins[nm], nm, c), _half_of(outs[nm].at[me], nm, c), ici_send.at[k * n + i],
                             ici_recv.at[k * n + i], (px, py, c))
                cp.start()
                sends.append(cp)
        for k, (px, py) in enumerate(_chip_peers()):
            for i, nm in enumerate(names):
                landed = _half_of(outs[nm].at[2 * px + py], nm, c)
                _remote(landed, landed, ici_send.at[k * n + i], ici_recv.at[k * n + i], (px, py, c)).wait_recv()
                fw = _remote(landed, landed, d2d_send.at[k * n + i], d2d_recv.at[k * n + i], sib)
                fw.start()
                sends.append(fw)
        for k, (px, py) in enumerate(_chip_peers()):
            for i, nm in enumerate(names):
                other = _half_of(outs[nm].at[2 * px + py], nm, 1 - c)
                _remote(other, other, d2d_send.at[k * n + i], d2d_recv.at[k * n + i], sib).wait_recv()
        for cp in sends:
            cp.wait_send()
        for cp in own:
            cp.wait()

    return _hbm_call(body, "ag_weights", shards, lambda nm, a: jax.ShapeDtypeStruct((N_CHIPS,) + a.shape, a.dtype),
                     (3 * n, 3 * n, 3 * n, 3 * n, n, n))


_SEM = pl.BlockSpec(memory_space=pltpu.SEMAPHORE)


def push_start(name, arrays, land_shapes, copies, n_copies):
    names = tuple(arrays)
    n = len(names)

    def body(*refs):
        send_sems, recv_sems, token = refs[2 * n], refs[2 * n + 1], refs[-1]
        for j, (src, dst, dev) in enumerate(copies(refs[:n], refs[n:2 * n])):
            _remote(src, dst, send_sems.at[j], recv_sems.at[j], dev).start()
        token[...] = jnp.zeros_like(token)

    hbm = lambda a: pltpu.with_memory_space_constraint(a, pltpu.HBM)
    lands = [lax.empty(land_shapes[nm], arrays[nm].dtype) for nm in names]
    res = pl.pallas_call(
        body, name=name,
        out_shape=(pltpu.SemaphoreType.DMA((n_copies,)), pltpu.SemaphoreType.DMA((n_copies,)),
                   *[pltpu.HBM(arrays[nm].shape, arrays[nm].dtype) for nm in names],
                   *[pltpu.HBM(a.shape, a.dtype) for a in lands], jax.ShapeDtypeStruct((8, 128), F32)),
        in_specs=[_HBM] * (2 * n),
        out_specs=(_SEM, _SEM, *[_HBM] * (2 * n), pl.BlockSpec(memory_space=pltpu.VMEM)),
        input_output_aliases={i: 2 + i for i in range(2 * n)},
        compiler_params=pltpu.CompilerParams(has_side_effects=pltpu.SideEffectType.DATAFLOW_SIDE_EFFECTING),
    )(*[hbm(arrays[nm]) for nm in names], *[hbm(a) for a in lands])
    return names, res[0], res[1], res[2:2 + n], res[2 + n:2 + 2 * n], res[-1]


def push_wait(name, started, copies, after):
    names, send_sems, recv_sems, srcs, lands, _ = started
    n = len(names)

    def body(*refs):
        send_ref, recv_ref = refs[2 * n], refs[2 * n + 1]
        for j, (src, dst, dev) in enumerate(copies(refs[:n], refs[n:2 * n])):
            cp = _remote(src, dst, send_ref.at[j], recv_ref.at[j], dev)
            cp.wait_send()
            cp.wait_recv()

    res = pl.pallas_call(
        body, name=name,
        out_shape=(*[pltpu.HBM(a.shape, a.dtype) for a in srcs], *[pltpu.HBM(a.shape, a.dtype) for a in lands]),
        in_specs=[_HBM] * (2 * n) + [_SEM, _SEM, pl.BlockSpec(memory_space=pl.ANY)],
        out_specs=tuple([_HBM] * (2 * n)),
        input_output_aliases={i: i for i in range(2 * n)},
        compiler_params=pltpu.CompilerParams(has_side_effects=pltpu.SideEffectType.DATAFLOW_SIDE_EFFECTING),
    )(*srcs, *lands, send_sems, recv_sems, after)
    return dict(zip(names, res[n:]))


def _gather_copies(srcs, lands):
    x, y, c = lax.axis_index("x"), lax.axis_index("y"), lax.axis_index("c")
    devs = [(px, py, c) for px, py in _chip_peers()] + [(x, y, 1 - c)]
    return [(src, land.at[2 * x + y], dev) for src, land in zip(srcs, lands) for dev in devs]


def _scatter_copies(srcs, lands):
    c = lax.axis_index("c")
    return [(src.at[2 * px + py], land.at[k], (px, py, c))
            for src, land in zip(srcs, lands) for k, (px, py) in enumerate(_chip_peers())]


def sibling_halves(blocks, name):
    names = tuple(blocks)

    def body(*refs):
        n = len(names)
        ins, outs = dict(zip(names, refs[:n])), dict(zip(names, refs[n:2 * n]))
        send_sems, recv_sems = refs[2 * n:]
        x, y, c = lax.axis_index("x"), lax.axis_index("y"), lax.axis_index("c")
        cps = [_remote(_half_of(ins[nm], nm, 1 - c, lead=1), outs[nm], send_sems.at[i], recv_sems.at[i], (x, y, 1 - c))
               for i, nm in enumerate(names)]
        for cp in cps:
            cp.start()
        for cp in cps:
            cp.wait()

    def half_shape(nm, a):
        r, cdim = a.shape[-2:]
        return jax.ShapeDtypeStruct((N_CHIPS, r, cdim // 2) if nm in _SPLIT_COLS else (N_CHIPS, r // 2, cdim), a.dtype)

    return _hbm_call(body, name, blocks, half_shape, (len(names), len(names)))


def scatter_halves(blocks):
    names = tuple(blocks)
    n = len(names)

    def body(*refs):
        ins, outs = dict(zip(names, refs[:n])), dict(zip(names, refs[n:2 * n]))
        send_sems, recv_sems = refs[2 * n:]
        c = lax.axis_index("c")
        cps = [_remote(ins[nm].at[2 * px + py], outs[nm].at[k], send_sems.at[k * n + i], recv_sems.at[k * n + i],
                       (px, py, c))
               for k, (px, py) in enumerate(_chip_peers()) for i, nm in enumerate(names)]
        for cp in cps:
            cp.start()
        for cp in cps:
            cp.wait_recv()
        for cp in cps:
            cp.wait_send()

    return _hbm_call(body, "rs_grads", blocks, lambda nm, a: jax.ShapeDtypeStruct((3,) + a.shape[1:], a.dtype),
                     (3 * n, 3 * n))


def sibling_assemble(arrays):
    names = tuple(arrays)

    def body(*refs):
        n = len(names)
        ins, outs = dict(zip(names, refs[:n])), dict(zip(names, refs[n:2 * n]))
        send_sems, recv_sems = refs[2 * n:]
        x, y, c = lax.axis_index("x"), lax.axis_index("y"), lax.axis_index("c")
        cps = [_remote(_half_of(ins[nm], nm, c), _half_of(outs[nm], nm, c), send_sems.at[i], recv_sems.at[i],
                       (x, y, 1 - c)) for i, nm in enumerate(names)]
        for cp in cps:
            cp.start()
        for i, nm in enumerate(names):
            other = _half_of(outs[nm], nm, 1 - c)
            _remote(other, other, send_sems.at[i], recv_sems.at[i], (x, y, 1 - c)).wait_recv()
        for cp in cps:
            cp.wait_send()

    return _hbm_call(body, "rs_assemble", arrays, lambda nm, a: jax.ShapeDtypeStruct(a.shape, a.dtype),
                     (len(names), len(names)), in_place=True)


def all_reduce_small(v):
    def body(v_ref, tot_ref, gath_ref, send_sems, recv_sems):
        x, y, c = lax.axis_index("x"), lax.axis_index("y"), lax.axis_index("c")
        me = 4 * x + 2 * y + c
        gath_ref[me] = v_ref[...]

        def peer(k):
            m = k + 1
            return (x ^ (m >> 2 & 1), y ^ (m >> 1 & 1), c ^ (m & 1))

        sends = [pltpu.make_async_remote_copy(src_ref=v_ref, dst_ref=gath_ref.at[me], send_sem=send_sems.at[k],
                                              recv_sem=recv_sems.at[k], device_id=peer(k), device_id_type=MESH)
                 for k in range(N_DEV - 1)]
        for cp in sends:
            cp.start()
        for k in range(N_DEV - 1):
            px, py, pc = peer(k)
            pltpu.make_async_remote_copy(src_ref=v_ref, dst_ref=gath_ref.at[4 * px + 2 * py + pc],
                                         send_sem=send_sems.at[k], recv_sem=recv_sems.at[k], device_id=peer(k),
                                         device_id_type=MESH).wait_recv()
        for cp in sends:
            cp.wait_send()
        acc = gath_ref[0]
        for d in range(1, N_DEV):
            acc = acc + gath_ref[d]
        tot_ref[...] = acc

    vm = pl.BlockSpec(memory_space=pltpu.VMEM)
    return pl.pallas_call(
        body, name="ar_small", in_specs=[vm], out_specs=[vm, vm],
        out_shape=(jax.ShapeDtypeStruct(v.shape, v.dtype), jax.ShapeDtypeStruct((N_DEV,) + v.shape, v.dtype)),
        scratch_shapes=[pltpu.SemaphoreType.DMA((N_DEV - 1,)), pltpu.SemaphoreType.DMA((N_DEV - 1,))],
    )(v)[0]


def _elementwise(fn, name, ins, n_out, out_dtype=F32):
    r, cdim = ins[0].shape
    tr = _pick(r, tuple(p for p in (488, 256, 128, 104, 64, 32, 16, 8) if p * cdim * 4 <= 2 * 1024 * 1024))
    spec = pl.BlockSpec((tr, cdim), lambda i: (i, 0))

    def body(*refs):
        res = fn(*[ref[...] for ref in refs[:len(ins)]])
        for o_ref, v in zip(refs[len(ins):], res):
            o_ref[...] = v

    return pl.pallas_call(
        body, name=name, grid=(r // tr,), in_specs=[spec] * len(ins), out_specs=[spec] * n_out,
        out_shape=tuple(jax.ShapeDtypeStruct((r, cdim), out_dtype) for _ in range(n_out)),
        compiler_params=pltpu.CompilerParams(dimension_semantics=("parallel",), vmem_limit_bytes=VMEM_LIMIT),
    )(*ins)


def _half_block_specs(nm, shard_shape):
    r, cdim = shard_shape
    if nm in _SPLIT_COLS:
        return (None, r, cdim // 2), (lambda j, c: (j, 0, c))
    return (None, r // 2, cdim), (lambda j, c: (j, c, 0))


def _presum(nm, sel, g32, a):
    blk, at = _half_block_specs(nm, g32.shape[1:])

    def body(s_ref, g_ref, a_ref, o_ref):
        del s_ref
        o_ref[...] = (g_ref[...] + a_ref[...]).astype(BF16)

    return pl.pallas_call(
        body, name="rs_presum_" + nm,
        grid_spec=pltpu.PrefetchScalarGridSpec(
            num_scalar_prefetch=1, grid=(N_CHIPS,),
            in_specs=[pl.BlockSpec(blk, lambda j, s: at(j, s[0])), pl.BlockSpec(blk, lambda j, s: (j, 0, 0))],
            out_specs=pl.BlockSpec(blk, lambda j, s: (j, 0, 0))),
        out_shape=jax.ShapeDtypeStruct(a.shape, BF16),
        compiler_params=pltpu.CompilerParams(dimension_semantics=("parallel",), vmem_limit_bytes=VMEM_LIMIT),
    )(sel, g32, a)


def _finalsum(nm, sel, g32, a, got):
    blk, at = _half_block_specs(nm, g32.shape[1:])

    def body(s_ref, g_ref, a_ref, r_ref, o_ref):
        del s_ref
        acc = g_ref[...] + a_ref[...]
        for k in range(3):
            acc = acc + r_ref[k].astype(F32)
        o_ref[...] = acc

    return pl.pallas_call(
        body, name="rs_final_" + nm,
        grid_spec=pltpu.PrefetchScalarGridSpec(
            num_scalar_prefetch=1, grid=(1,),
            in_specs=[pl.BlockSpec(blk, lambda i, s: at(s[1], s[0])), pl.BlockSpec(blk, lambda i, s: (s[1], 0, 0)),
                      pl.BlockSpec(got.shape, lambda i, s: (0, 0, 0))],
            out_specs=pl.BlockSpec(blk[1:], lambda i, s: at(0, s[0])[1:])),
        out_shape=jax.ShapeDtypeStruct(g32.shape[1:], F32),
        compiler_params=pltpu.CompilerParams(dimension_semantics=("arbitrary",), vmem_limit_bytes=VMEM_LIMIT),
    )(sel, g32, a, got)


def _adamw(w, g, m, v, name):
    shape = w.shape
    to2 = lambda a: a.reshape(-1, shape[-1])

    def fn(w_, g_, m_, v_):
        m_new = ADAM_B1 * m_ + (1.0 - ADAM_B1) * g_
        v_new = ADAM_B2 * v_ + (1.0 - ADAM_B2) * (g_ * g_)
        m_hat = m_new / (1.0 - ADAM_B1 ** ADAM_STEP)
        v_hat = v_new / (1.0 - ADAM_B2 ** ADAM_STEP)
        delta = -ADAM_LR * (m_hat / (jnp.sqrt(v_hat) + ADAM_EPS) + ADAM_WD * w_)
        return g_, delta, m_new, v_new

    outs = _elementwise(fn, name, [to2(a) for a in (w, g, m, v)], 4)
    return tuple(o.reshape(shape) for o in outs)


_BIG = ("w_mod", "w_in", "w_pa", "w_pd", "w_out", "w_up", "w_down")
_EARLY = ("w_mod", "w_in")
_LATE = ("w_pa", "w_pd", "w_out", "w_up", "w_down")
_COL_SHARDED = ("w_mod", "w_up")
_FULL_SHAPE = {"w_mod": (D_MODEL, MOD_W), "w_in": (IN_COLS, D_MODEL), "w_pa": (Q_W, D_MODEL), "w_pd": (GDN_W, D_MODEL),
               "w_out": (D_MODEL, D_MODEL), "w_up": (D_MODEL, 2 * D_FF), "w_down": (D_FF, D_MODEL)}


def _shard_shape(name):
    r, cdim = _FULL_SHAPE[name]
    return (r, cdim // N_CHIPS) if name in _COL_SHARDED else (r // N_CHIPS, cdim)


_CONV_ELEMS = 2 * (3 * CONV_W // N_CHIPS + 3 * 2 * D_FF // N_CHIPS)
_CONV_ROWS = 32


def _blocks_of_full(name, full):
    r, cdim = _FULL_SHAPE[name]
    if name in _COL_SHARDED:
        return full.reshape(r, N_CHIPS, cdim // N_CHIPS).transpose(1, 0, 2)
    return full.reshape(N_CHIPS, r // N_CHIPS, cdim)


def _full_of_blocks(name, blocks):
    r, cdim = _FULL_SHAPE[name]
    if name in _COL_SHARDED:
        return blocks.transpose(1, 0, 2).reshape(r, cdim)
    return blocks.reshape(r, cdim)


def _w_in_regroup(w_in_t):
    main = jnp.concatenate([w_in_t[:SMALL_AT], w_in_t[SMALL_AT + 4 * GDN_HEADS:]], axis=0)
    small = jnp.pad(w_in_t[SMALL_AT:SMALL_AT + 4 * GDN_HEADS], ((0, HEAD_DIM - 4 * GDN_HEADS), (0, 0)))
    return main, small


def _w_in_ungroup(main, small):
    return jnp.concatenate([main[:SMALL_AT], small[:4 * GDN_HEADS], main[SMALL_AT:]], axis=0)


_SMALL = ("c_ctx", "b_mod", "q_norm_w", "k_norm_w", "conv_qkv_w", "a_log", "dt_bias", "gdn_norm_w", "ffn_conv_w",
          "ffn_conv_b", "final_norm_w")


def _pack_small(tree, rows):
    flat = jnp.concatenate([tree[nm].reshape(-1) for nm in _SMALL])
    return jnp.pad(flat, (0, rows * 128 - flat.shape[0])).reshape(rows, 128)


def _unpack_small(packed, like):
    flat, out, off = packed.reshape(-1), {}, 0
    for nm in _SMALL:
        size = int(np.prod(like[nm].shape))
        out[nm] = flat[off:off + size].reshape(like[nm].shape)
        off += size
    return out


def kernel(x, c, ctx, c_ctx, w_mod, b_mod, w_in, q_norm_w, k_norm_w, conv_qkv_w, a_log, dt_bias, gdn_norm_w, w_pa, w_pd, w_out, w_up, ffn_conv_w, ffn_conv_b, w_down, final_norm_w, loss_target, m_c_ctx, m_w_mod, m_b_mod, m_w_in, m_q_norm_w, m_k_norm_w, m_conv_qkv_w, m_a_log, m_dt_bias, m_gdn_norm_w, m_w_pa, m_w_pd, m_w_out, m_w_up, m_ffn_conv_w, m_ffn_conv_b, m_w_down, m_final_norm_w, v_c_ctx, v_w_mod, v_b_mod, v_w_in, v_q_norm_w, v_k_norm_w, v_conv_qkv_w, v_a_log, v_dt_bias, v_gdn_norm_w, v_w_pa, v_w_pd, v_w_out, v_w_up, v_ffn_conv_w, v_ffn_conv_b, v_w_down, v_final_norm_w):
    names = ("c_ctx", "w_mod", "b_mod", "w_in", "q_norm_w", "k_norm_w", "conv_qkv_w", "a_log", "dt_bias", "gdn_norm_w",
             "w_pa", "w_pd", "w_out", "w_up", "ffn_conv_w", "ffn_conv_b", "w_down", "final_norm_w")
    w_sh = dict(c_ctx=c_ctx, w_mod=w_mod, b_mod=b_mod, w_in=w_in, q_norm_w=q_norm_w, k_norm_w=k_norm_w,
                conv_qkv_w=conv_qkv_w, a_log=a_log, dt_bias=dt_bias, gdn_norm_w=gdn_norm_w, w_pa=w_pa, w_pd=w_pd,
                w_out=w_out, w_up=w_up, ffn_conv_w=ffn_conv_w, ffn_conv_b=ffn_conv_b, w_down=w_down,
                final_norm_w=final_norm_w)
    m_sh = dict(c_ctx=m_c_ctx, w_mod=m_w_mod, b_mod=m_b_mod, w_in=m_w_in, q_norm_w=m_q_norm_w, k_norm_w=m_k_norm_w,
                conv_qkv_w=m_conv_qkv_w, a_log=m_a_log, dt_bias=m_dt_bias, gdn_norm_w=m_gdn_norm_w, w_pa=m_w_pa,
                w_pd=m_w_pd, w_out=m_w_out, w_up=m_w_up, ffn_conv_w=m_ffn_conv_w, ffn_conv_b=m_ffn_conv_b,
                w_down=m_w_down, final_norm_w=m_final_norm_w)
    v_sh = dict(c_ctx=v_c_ctx, w_mod=v_w_mod, b_mod=v_b_mod, w_in=v_w_in, q_norm_w=v_q_norm_w, k_norm_w=v_k_norm_w,
                conv_qkv_w=v_conv_qkv_w, a_log=v_a_log, dt_bias=v_dt_bias, gdn_norm_w=v_gdn_norm_w, w_pa=v_w_pa,
                w_pd=v_w_pd, w_out=v_w_out, w_up=v_w_up, ffn_conv_w=v_ffn_conv_w, ffn_conv_b=v_ffn_conv_b,
                w_down=v_w_down, final_norm_w=v_final_norm_w)
    chip = 2 * lax.axis_index("x") + lax.axis_index("y")

    conv_bits = jnp.concatenate([lax.bitcast_convert_type(w_sh[nm][0], BF16).reshape(-1)
                                 for nm in ("conv_qkv_w", "ffn_conv_w")])
    shards = {nm: w_sh[nm][0].astype(BF16).T if nm == "w_in" else w_sh[nm][0].astype(BF16) for nm in _BIG}
    shards["conv"] = jnp.pad(conv_bits, (0, _CONV_ROWS * D_MODEL - _CONV_ELEMS)).reshape(_CONV_ROWS, D_MODEL)
    gathered = all_gather_chips({nm: shards[nm] for nm in _EARLY + ("conv",)})
    gathered, late_shards = lax.optimization_barrier((gathered, {nm: shards[nm] for nm in _LATE}))
    started = push_start("ag_late_start", late_shards, {nm: (N_CHIPS,) + a.shape for nm, a in late_shards.items()},
                         _gather_copies, 4 * len(_LATE))
    c = c + started[-1][0:1, 0:1]

    wb = {nm: _full_of_blocks(nm, gathered[nm]) for nm in _EARLY}
    wb["w_in_main"], wb["w_in_small"] = _w_in_regroup(wb.pop("w_in"))
    conv_all = gathered["conv"].reshape(N_CHIPS, -1)[:, :_CONV_ELEMS]
    n_cq = 2 * 3 * CONV_W // N_CHIPS
    unbits = lambda a, w: lax.bitcast_convert_type(a.reshape(N_CHIPS, 3, w // N_CHIPS, 2), F32).transpose(1, 0, 2).reshape(3, w)
    ws = dict(c_ctx=c_ctx, b_mod=b_mod, q_norm_w=q_norm_w, k_norm_w=k_norm_w, a_log=a_log[0], dt_bias=dt_bias[0],
              gdn_norm_w=gdn_norm_w, ffn_conv_b=ffn_conv_b, final_norm_w=final_norm_w,
              conv_qkv_w=unbits(conv_all[:, :n_cq], CONV_W), ffn_conv_w=unbits(conv_all[:, n_cq:], 2 * D_FF))
    wz = {nm: jnp.zeros(a.shape, F32) for nm, a in wb.items()}
    wz.update({nm: jnp.zeros(_FULL_SHAPE[nm], F32) for nm in _LATE})

    mixed, vjp_mix = jax.vjp(lambda x_, wz_, ws_: token_mixing(x_, wz_, wb, ws_, c, ctx[0]), x[0], wz, ws)
    got = push_wait("ag_late_wait", started, _gather_copies, mixed["gdn_x"])
    wb_late = {nm: _full_of_blocks(nm, got[nm]) for nm in _LATE}
    loss_local, vjp_chan = jax.vjp(
        lambda mixed_, wz_, ws_: channel_mixing(mixed_, wz_, wb_late, ws_, loss_target[0]), mixed, wz, ws)
    d_mixed, gz_chan, gs_chan = vjp_chan(jnp.ones((), F32))

    sel = jnp.stack([lax.axis_index("c"), chip]).astype(jnp.int32)
    g32_late = {nm: _blocks_of_full(nm, gz_chan[nm]) for nm in _LATE}
    theirs_late = sibling_halves(g32_late, "rs_sibling_late")
    sums_late = {nm: _presum(nm, sel, g32_late[nm], theirs_late[nm]) for nm in _LATE}
    scattering = push_start("rs_late_start", sums_late, {nm: (3,) + a.shape[1:] for nm, a in sums_late.items()},
                            _scatter_copies, 3 * len(_LATE))
    d_mixed = {**d_mixed, "gdn_x": d_mixed["gdn_x"] + scattering[-1][0:1, 0:1]}
    gx, gz_mix, gs_mix = vjp_mix(d_mixed)
    gs = jax.tree.map(jnp.add, gs_mix, gs_chan)
    got_late = push_wait("rs_late_wait", scattering, _scatter_copies, gx)
    loss = lax.psum(loss_local, ("x", "y", "c"))

    gz_mix["w_in"] = _w_in_ungroup(gz_mix.pop("w_in_main"), gz_mix.pop("w_in_small"))
    g32 = {nm: _blocks_of_full(nm, gz_mix[nm]) for nm in _EARLY}
    theirs = sibling_halves(g32, "rs_sibling")
    got = scatter_halves({nm: _presum(nm, sel, g32[nm], theirs[nm]) for nm in _EARLY})
    g32.update(g32_late), theirs.update(theirs_late), got.update(got_late)
    g_big = sibling_assemble({nm: _finalsum(nm, sel, g32[nm], theirs[nm], got[nm]) for nm in _BIG})

    gs["a_log"], gs["dt_bias"] = gs["a_log"][None], gs["dt_bias"][None]
    like = {nm: gs[nm] for nm in _SMALL}
    small_rows = -(-sum(int(np.prod(like[nm].shape)) for nm in _SMALL) // 1024) * 8
    g_small = _unpack_small(all_reduce_small(_pack_small(gs, small_rows)), like)
    for nm, width in (("conv_qkv_w", CONV_W), ("ffn_conv_w", 2 * D_FF)):
        g_small[nm] = lax.dynamic_slice_in_dim(g_small[nm], chip * (width // N_CHIPS), width // N_CHIPS, axis=1)[None]

    grads, deltas, new_m, new_v = {}, {}, {}, {}
    for nm in _BIG:
        g = g_big[nm].T if nm == "w_in" else g_big[nm]
        grads[nm], deltas[nm], new_m[nm], new_v[nm] = (
            o[None] for o in _adamw(w_sh[nm][0], g, m_sh[nm][0], v_sh[nm][0], "adamw_" + nm))
    shard_like = {nm: w_sh[nm] for nm in _SMALL}
    rows_l = -(-sum(int(np.prod(shard_like[nm].shape)) for nm in _SMALL) // 1024) * 8
    g_l = _pack_small({nm: g_small[nm].reshape(w_sh[nm].shape) for nm in _SMALL}, rows_l)
    outs = _adamw(_pack_small(w_sh, rows_l), g_l, _pack_small(m_sh, rows_l), _pack_small(v_sh, rows_l), "adamw_small")
    for tree, packed in zip((grads, deltas, new_m, new_v), outs):
        tree.update(_unpack_small(packed, shard_like))

    return (loss, gx[None], *[grads[nm] for nm in names], *[deltas[nm] for nm in names],
            *[new_m[nm] for nm in names], *[new_v[nm] for nm in names])
```

```python
import functools
import math

import jax
import jax.numpy as jnp
import numpy as np
from jax import lax
from jax.experimental import pallas as pl
from jax.experimental.pallas import tpu as pltpu

F32 = jnp.float32
BF16 = jnp.bfloat16
HIGHEST = lax.Precision.HIGHEST
MESH = pl.DeviceIdType.MESH

D_MODEL = 1024
GRID_W = 64
ATTN_HEADS = 8
ATTN_KV_HEADS = 2
ATTN_GROUP = ATTN_HEADS // ATTN_KV_HEADS
HEAD_DIM = 128
ROPE_THETA = 10000.0
GDN_HEADS = 8
GDN_CHUNK = 64
D_FF = 2816
NORM_EPS = 1e-6
KV_W = ATTN_KV_HEADS * HEAD_DIM
Q_W = ATTN_HEADS * HEAD_DIM
GDN_W = GDN_HEADS * HEAD_DIM
CONV_W = 3 * GDN_W
MOD_W = 6 * D_MODEL
IN_COLS = 2 * KV_W + CONV_W + 4 * GDN_HEADS + Q_W + GDN_W + 2 * D_MODEL
IN_MAIN = IN_COLS - 4 * GDN_HEADS
SMALL_AT = 2 * KV_W + CONV_W
N_CHIPS = 4
N_DEV = 8

ADAM_LR = 0.001
ADAM_B1 = 0.9
ADAM_B2 = 0.999
ADAM_EPS = 1e-08
ADAM_WD = 0.01
ADAM_STEP = 10

VMEM_LIMIT = 48 * 1024 * 1024
MATMUL_VMEM_BUDGET = 40 * 1024 * 1024
MATMUL_STEP_BYTES = 1200 * 1024


def _pick(dim, prefs):
    for p in prefs:
        if p <= dim and dim % p == 0:
            return p
    return dim


_DIMS = {
    "nn": (((1,), (0,)), ((), ())),
    "nt": (((1,), (1,)), ((), ())),
    "tn": (((0,), (0,)), ((), ())),
}


def _matmul_plan(m, n, k, a_bytes, b_bytes):
    best = None
    for tm in (2304, 2048, 1152, 1024, 768, 512, 384, 256, 128, m):
        for tn in (2560, 1536, 1408, 1024, 768, 512, 256, 128, n):
            for tk in (3840, 2816, 2560, 2304, 2048, 1920, 1408, 1152, 1024, 768, 512, 256, 128, k):
                if tm > m or tn > n or tk > k or m % tm or n % tn or k % tk:
                    continue
                blocks = tm * tk * a_bytes + tk * tn * b_bytes + tm * tn * 4
                casts = (tm * tk * 2 if a_bytes > 2 else 0) + (tk * tn * 2 if b_bytes > 2 else 0) + tm * tn * 4
                if 2 * blocks + casts > MATMUL_VMEM_BUDGET:
                    continue
                nm, nn, nk = m // tm, n // tn, k // tk
                size_a, size_b = m * k * a_bytes, k * n * b_bytes
                for n_inner in (True, False):
                    if n_inner:
                        traffic = (size_a if nk == 1 else nn * size_a) + nm * size_b
                    else:
                        traffic = nn * size_a + (size_b if nk == 1 else nm * size_b)
                    cost = traffic + nm * nn * nk * MATMUL_STEP_BYTES + (nk - 1) * m * n * 4
                    if best is None or cost < best[0]:
                        best = (cost, tm, tn, tk, n_inner)
    return best[1:]


def _matmul(a, b, mode, name):
    if mode == "nn":
        (m, k), (_, n) = a.shape, b.shape
    elif mode == "nt":
        (m, k), (n, _) = a.shape, b.shape
    else:
        (k, m), (_, n) = a.shape, b.shape
    tm, tn, tk, n_inner = _matmul_plan(m, n, k, a.dtype.itemsize, b.dtype.itemsize)
    nk = k // tk
    ij = (lambda g0, g1: (g0, g1)) if n_inner else (lambda g0, g1: (g1, g0))
    if mode == "tn":
        a_spec = pl.BlockSpec((tk, tm), lambda g0, g1, l: (l, ij(g0, g1)[0]))
    else:
        a_spec = pl.BlockSpec((tm, tk), lambda g0, g1, l: (ij(g0, g1)[0], l))
    if mode == "nt":
        b_spec = pl.BlockSpec((tn, tk), lambda g0, g1, l: (ij(g0, g1)[1], l))
    else:
        b_spec = pl.BlockSpec((tk, tn), lambda g0, g1, l: (l, ij(g0, g1)[1]))
    dims = _DIMS[mode]

    def body(a_ref, b_ref, o_ref):
        part = lax.dot_general(a_ref[...].astype(BF16), b_ref[...].astype(BF16), dims, preferred_element_type=F32)
        if nk == 1:
            o_ref[...] = part
        else:
            l = pl.program_id(2)

            @pl.when(l == 0)
            def _():
                o_ref[...] = part

            @pl.when(l > 0)
            def _():
                o_ref[...] += part

    return pl.pallas_call(
        body,
        name=name,
        grid=(m // tm, n // tn, nk) if n_inner else (n // tn, m // tm, nk),
        in_specs=[a_spec, b_spec],
        out_specs=pl.BlockSpec((tm, tn), lambda g0, g1, l: ij(g0, g1)),
        out_shape=jax.ShapeDtypeStruct((m, n), F32),
        compiler_params=pltpu.CompilerParams(dimension_semantics=("parallel", "parallel", "arbitrary"),
                                             vmem_limit_bytes=VMEM_LIMIT),
    )(a, b)


@functools.partial(jax.custom_vjp, nondiff_argnums=(3,))
def pmm(a, w, wz, name):
    del wz
    return _matmul(a, w, "nn", name + "_f")


def _pmm_fwd(a, w, wz, name):
    del wz
    return _matmul(a, w, "nn", name + "_f"), (a, w)


def _pmm_bwd(name, res, g):
    a, w = res
    da = _matmul(g, w, "nt", name + "_da")
    if a.shape[0] < 128:
        pad = 128 - a.shape[0]
        at = jnp.pad(a.T, ((0, 0), (0, pad)))
        gp = jnp.pad(g, ((0, pad), (0, 0)))
        dw = _matmul(at, gp, "nn", name + "_dw")
    else:
        dw = _matmul(a, g, "tn", name + "_dw")
    return da, jnp.zeros_like(w), dw


pmm.defvjp(_pmm_fwd, _pmm_bwd)


@functools.partial(jax.custom_vjp, nondiff_argnums=(3,))
def pmm_t(a, wt, wtz, name):
    del wtz
    return _matmul(a, wt, "nt", name + "_f")


def _pmm_t_fwd(a, wt, wtz, name):
    del wtz
    return _matmul(a, wt, "nt", name + "_f"), (a, wt)


def _pmm_t_bwd(name, res, g):
    a, wt = res
    return _matmul(g, wt, "nn", name + "_da"), jnp.zeros_like(wt), _matmul(g, a, "tn", name + "_dw")


pmm_t.defvjp(_pmm_t_fwd, _pmm_t_bwd)


def rowop(fn, name, rows, bcs=(), crows=(), cbcs=(), tr=256):
    rows, bcs, crows, cbcs = tuple(rows), tuple(bcs), tuple(crows), tuple(cbcs)
    n_rows = rows[0].shape[0]
    tr = _pick(n_rows, (tr, 128, 64, 32, 16, 8))
    nr, nb, ncr, ncb = len(rows), len(bcs), len(crows), len(cbcs)
    n_in = nr + nb + ncr + ncb
    grid = (n_rows // tr,)

    def blk(arr):
        return jax.ShapeDtypeStruct((tr, arr.shape[1]), arr.dtype)

    def row_spec(arr):
        return pl.BlockSpec((tr, arr.shape[1]), lambda i: (i, 0))

    def bc_spec(arr):
        return pl.BlockSpec(arr.shape, lambda i: (0, 0))

    out_blk = jax.eval_shape(fn, *[blk(r) for r in rows], *bcs, *[blk(r) for r in crows], *cbcs)
    n_out = len(out_blk)
    out_shape = tuple(jax.ShapeDtypeStruct((n_rows, o.shape[1]), o.dtype) for o in out_blk)
    in_specs = ([row_spec(r) for r in rows] + [bc_spec(b) for b in bcs]
                + [row_spec(r) for r in crows] + [bc_spec(b) for b in cbcs])

    def order(vals):
        return vals

    def fwd_call(args):
        def body(*refs):
            vals = [r[...] for r in refs[:n_in]]
            res = fn(*order(vals))
            for o_ref, r in zip(refs[n_in:], res):
                o_ref[...] = r

        return pl.pallas_call(
            body, name=name + "_f", grid=grid, in_specs=in_specs,
            out_specs=[row_spec(o) for o in out_shape], out_shape=out_shape,
            compiler_params=pltpu.CompilerParams(dimension_semantics=("parallel",), vmem_limit_bytes=VMEM_LIMIT),
        )(*args)

    def bwd_call(args, cts):
        def body(*refs):
            vals = [r[...] for r in refs[:n_in]]
            ct_refs = refs[n_in:n_in + n_out]
            d_rows = refs[n_in + n_out:n_in + n_out + nr]
            d_bcs = refs[n_in + n_out + nr:]
            consts = vals[nr + nb:]
            _, vjp = jax.vjp(lambda *p: fn(*p, *consts), *vals[:nr + nb])
            grads = vjp(tuple(c[...] for c in ct_refs))
            for ref, g in zip(d_rows, grads[:nr]):
                ref[...] = g

            @pl.when(pl.program_id(0) == 0)
            def _():
                for ref in d_bcs:
                    ref[...] = jnp.zeros_like(ref)

            for ref, g in zip(d_bcs, grads[nr:]):
                ref[...] += g

        d_shape = tuple(jax.ShapeDtypeStruct(r.shape, r.dtype) for r in rows + bcs)
        return pl.pallas_call(
            body, name=name + "_b", grid=grid,
            in_specs=in_specs + [row_spec(o) for o in out_shape],
            out_specs=[row_spec(r) for r in rows] + [bc_spec(b) for b in bcs], out_shape=d_shape,
            compiler_params=pltpu.CompilerParams(dimension_semantics=("arbitrary",), vmem_limit_bytes=VMEM_LIMIT),
        )(*args, *cts)

    @jax.custom_vjp
    def op(diff, const):
        return fwd_call(diff + const)

    def op_fwd(diff, const):
        return fwd_call(diff + const), (diff, const)

    def op_bwd(res, cts):
        diff, const = res
        grads = bwd_call(diff + const, tuple(cts))
        return tuple(grads), tuple(jnp.zeros_like(c) for c in const)

    op.defvjp(op_fwd, op_bwd)
    return op(rows + bcs, crows + cbcs)


def colop(fn, name, arrays, uses, n_const, nblk, cw=128):
    arrays = tuple(arrays)
    n_diff = len(arrays) - n_const
    nd = sum(1 for u in uses if u[0] < n_diff)
    assert all(u[0] < n_diff for u in uses[:nd]) and all(u[0] >= n_diff for u in uses[nd:])

    def spec(u):
        return pl.BlockSpec((arrays[u[0]].shape[0], cw), lambda j, off=u[1]: (0, off + j))

    def out_spec(rows):
        return pl.BlockSpec((rows, cw), lambda j: (0, j))

    out_blk = jax.eval_shape(fn, *[jax.ShapeDtypeStruct((arrays[u[0]].shape[0], cw), arrays[u[0]].dtype)
                                   for u in uses])
    out_shape = tuple(jax.ShapeDtypeStruct((o.shape[0], nblk * cw), o.dtype) for o in out_blk)
    params = pltpu.CompilerParams(dimension_semantics=("parallel",), vmem_limit_bytes=VMEM_LIMIT)

    def fwd_call(arrs):
        def body(*refs):
            res = fn(*[r[...] for r in refs[:len(uses)]])
            for o_ref, r in zip(refs[len(uses):], res):
                o_ref[...] = r

        return pl.pallas_call(
            body, name=name + "_f", grid=(nblk,), in_specs=[spec(u) for u in uses],
            out_specs=[out_spec(o.shape[0]) for o in out_shape], out_shape=out_shape, compiler_params=params,
        )(*[arrs[u[0]] for u in uses])

    def bwd_call(arrs, cts):
        def body(*refs):
            vals = [r[...] for r in refs[:len(uses)]]
            ct_refs = refs[len(uses):len(uses) + len(out_shape)]
            _, vjp = jax.vjp(lambda *p: fn(*p, *vals[nd:]), *vals[:nd])
            for ref, g in zip(refs[len(uses) + len(out_shape):], vjp(tuple(c[...] for c in ct_refs))):
                ref[...] = g

        d_shape = tuple(jax.ShapeDtypeStruct((arrays[u[0]].shape[0], nblk * cw), F32) for u in uses[:nd])
        return pl.pallas_call(
            body, name=name + "_b", grid=(nblk,),
            in_specs=[spec(u) for u in uses] + [out_spec(o.shape[0]) for o in out_shape],
            out_specs=[out_spec(s.shape[0]) for s in d_shape], out_shape=d_shape, compiler_params=params,
        )(*[arrs[u[0]] for u in uses], *cts)

    @jax.custom_vjp
    def op(diff, const):
        return fwd_call(diff + const)

    def op_fwd(diff, const):
        return fwd_call(diff + const), (diff, const)

    def op_bwd(res, cts):
        diff, const = res
        d_uses = bwd_call(diff + const, tuple(cts))
        grads = []
        for i in range(n_diff):
            parts = sorted([(u[1], k) for k, u in enumerate(uses[:nd]) if u[0] == i])
            grads.append(d_uses[parts[0][1]] if len(parts) == 1
                         else jnp.concatenate([d_uses[k] for _, k in parts], axis=1))
        return tuple(grads), tuple(jnp.zeros_like(c) for c in const)

    op.defvjp(op_fwd, op_bwd)
    return op(arrays[:n_diff], arrays[n_diff:])


@functools.partial(jax.custom_vjp, nondiff_argnums=(1,))
def _roll_rows(x, k):
    return pltpu.roll(x, k % x.shape[0], 0)


def _roll_rows_fwd(x, k):
    return _roll_rows(x, k), None


def _roll_rows_bwd(k, _, g):
    return (_roll_rows(g, -k),)


_roll_rows.defvjp(_roll_rows_fwd, _roll_rows_bwd)


def _conv3(x, w0, w1, w2, starts):
    rows = lax.broadcasted_iota(jnp.int32, x.shape, 0)
    ends = tuple(s - 1 for s in starts[1:]) + (x.shape[0] - 1,)
    first = functools.reduce(jnp.logical_or, [rows == s for s in starts])
    last = functools.reduce(jnp.logical_or, [rows == e for e in ends])
    prev = jnp.where(first, 0.0, _roll_rows(x, 1))
    nxt = jnp.where(last, 0.0, _roll_rows(x, -1))
    return prev * w0 + x * w1 + nxt * w2


def _rms(x):
    return x * lax.rsqrt(jnp.mean(x * x, axis=-1, keepdims=True) + NORM_EPS)


def _heads(x, n):
    return [x[:, h * HEAD_DIM:(h + 1) * HEAD_DIM] for h in range(n)]


_NT = (((1,), (1,)), ((), ()))
_TN = (((0,), (0,)), ((), ()))
_TQ = 256


def _attn_probs(q, k):
    s = lax.dot_general(q, k, _NT, preferred_element_type=F32) * (HEAD_DIM ** -0.5)
    p = jnp.exp(s - jnp.max(s, axis=-1, keepdims=True))
    return p * (1.0 / jnp.sum(p, axis=-1, keepdims=True))


def _attn_fwd_call(q, k, v):
    n, t = q.shape[0], k.shape[0]
    tq = _pick(n, (_TQ, 128))

    def body(q_ref, k_ref, v_ref, o_ref):
        p = _attn_probs(q_ref[...].astype(BF16), k_ref[...].astype(BF16))
        o_ref[...] = jnp.dot(p.astype(BF16), v_ref[...].astype(BF16), preferred_element_type=F32)

    return pl.pallas_call(
        body, name="attn_f", grid=(ATTN_HEADS, n // tq),
        in_specs=[pl.BlockSpec((tq, HEAD_DIM), lambda h, i: (i, h)),
                  pl.BlockSpec((t, HEAD_DIM), lambda h, i: (0, h // ATTN_GROUP)),
                  pl.BlockSpec((t, HEAD_DIM), lambda h, i: (0, h // ATTN_GROUP))],
        out_specs=pl.BlockSpec((tq, HEAD_DIM), lambda h, i: (i, h)),
        out_shape=jax.ShapeDtypeStruct(q.shape, F32),
        compiler_params=pltpu.CompilerParams(dimension_semantics=("parallel", "parallel"),
                                             vmem_limit_bytes=VMEM_LIMIT),
    )(q, k, v)


def _attn_bwd_call(q, k, v, do):
    n, t = q.shape[0], k.shape[0]
    tq = _pick(n, (_TQ, 128))

    def body(q_ref, k_ref, v_ref, do_ref, dq_ref, dk_ref, dv_ref):
        @pl.when((pl.program_id(1) == 0) & (pl.program_id(2) == 0))
        def _():
            dk_ref[...] = jnp.zeros_like(dk_ref)
            dv_ref[...] = jnp.zeros_like(dv_ref)

        qb, kb, vb, dob = (r[...].astype(BF16) for r in (q_ref, k_ref, v_ref, do_ref))
        p = _attn_probs(qb, kb)
        dp = lax.dot_general(dob, vb, _NT, preferred_element_type=F32)
        ds = p * (dp - jnp.sum(p * dp, axis=-1, keepdims=True)) * (HEAD_DIM ** -0.5)
        dsb = ds.astype(BF16)
        dq_ref[...] = jnp.dot(dsb, kb, preferred_element_type=F32)
        dk_ref[...] += lax.dot_general(dsb, qb, _TN, preferred_element_type=F32)
        dv_ref[...] += lax.dot_general(p.astype(BF16), dob, _TN, preferred_element_type=F32)

    q_spec = pl.BlockSpec((tq, HEAD_DIM), lambda kh, g, i: (i, kh * ATTN_GROUP + g))
    kv_spec = pl.BlockSpec((t, HEAD_DIM), lambda kh, g, i: (0, kh))
    return pl.pallas_call(
        body, name="attn_b", grid=(ATTN_KV_HEADS, ATTN_GROUP, n // tq),
        in_specs=[q_spec, kv_spec, kv_spec, q_spec],
        out_specs=[q_spec, kv_spec, kv_spec],
        out_shape=(jax.ShapeDtypeStruct(q.shape, F32), jax.ShapeDtypeStruct(k.shape, F32),
                   jax.ShapeDtypeStruct(v.shape, F32)),
        compiler_params=pltpu.CompilerParams(dimension_semantics=("parallel", "arbitrary", "arbitrary"),
                                             vmem_limit_bytes=VMEM_LIMIT),
    )(q, k, v, do)


@jax.custom_vjp
def attention(q, k, v):
    return _attn_fwd_call(q, k, v)


def _attention_fwd(q, k, v):
    return _attn_fwd_call(q, k, v), (q, k, v)


def _attention_bwd(res, do):
    return _attn_bwd_call(*res, do)


attention.defvjp(_attention_fwd, _attention_bwd)


_C = GDN_CHUNK


def _pdot(a, b):
    return jnp.dot(a, b, precision=lax.Precision.HIGH, preferred_element_type=F32)


@jax.custom_vjp
def _hdot(a, b):
    return jnp.dot(a.astype(BF16), b.astype(BF16), preferred_element_type=F32)


def _hdot_fwd(a, b):
    return _hdot(a, b), (a, b)


def _hdot_bwd(res, g):
    a, b = res
    gb = g.astype(BF16)
    return (lax.dot_general(gb, b.astype(BF16), _NT, preferred_element_type=F32),
            lax.dot_general(a.astype(BF16), gb, _TN, preferred_element_type=F32))


_hdot.defvjp(_hdot_fwd, _hdot_bwd)


def _each(fn, *lists):
    return [fn(*args) for args in zip(*lists)]


def _unit_lower_inverse(low, blockdiag):
    eye = (lax.broadcasted_iota(jnp.int32, (_C, _C), 0) == lax.broadcasted_iota(jnp.int32, (_C, _C), 1)).astype(F32)
    ld = _each(lambda a: a * blockdiag, low)
    lo = _each(lambda a, d: a - d, low, ld)
    l2 = _each(_hdot, ld, ld)
    l4 = _each(_hdot, l2, l2)
    l8 = _each(_hdot, l4, l4)
    td = _each(lambda d, a2: _hdot(eye - d, eye + a2), ld, l2)
    td = _each(lambda t, a4: _hdot(t, eye + a4), td, l4)
    td = _each(lambda t, a8: _hdot(t, eye + a8), td, l8)
    nn = _each(_hdot, td, lo)
    n2 = _each(_hdot, nn, nn)
    out = _each(lambda n, m2: _hdot(eye - n, eye + m2), nn, n2)
    return _each(_hdot, out, td)


def _gdn_chunks(heads, blockdiag):
    q, k, v, b_b, be_b, e_b, kd_b, m1, dec, gl, s = (list(col) for col in zip(*heads))
    f32dot = lambda a, b: jnp.dot(a, b, preferred_element_type=F32)
    nt = lambda a, b: lax.dot_general(a, b, _NT, preferred_element_type=F32)
    kq_k = _each(lambda kx, qq: nt(jnp.concatenate([kx, qq], axis=0), kx), k, q)
    t_inv = _unit_lower_inverse(_each(lambda m, a: m * a[:_C], m1, kq_k), blockdiag)
    uw = _each(lambda t, b, x, be, kx: _hdot(t, jnp.concatenate([b * x, be * kx], axis=1)), t_inv, b_b, v, be_b, k)
    wq_s = _each(lambda a, qq, e, ss: f32dot(jnp.concatenate([a[:, HEAD_DIM:], qq * e], axis=0), ss), uw, q, e_b, s)
    delta = _each(lambda a, ws: a[:, :HEAD_DIM] - ws[:_C], uw, wq_s)
    p = _each(lambda d, a: d * a[_C:], dec, kq_k)
    o = _each(lambda ws, pp, dd: ws[_C:] + f32dot(pp, dd), wq_s, p, delta)
    s_new = _each(lambda g, ss, kx, kd, dd: g * ss + lax.dot_general(kx * kd, dd, _TN, preferred_element_type=F32),
                  gl, s, k, kd_b, delta)
    return o, s_new


def _blockdiag_mask():
    r = lax.broadcasted_iota(jnp.int32, (_C, _C), 0) >> 4
    c = lax.broadcasted_iota(jnp.int32, (_C, _C), 1) >> 4
    return (r == c).astype(F32)


_N_DIR = 2


def _scan_chunk(s, nc, ncc, reverse):
    return jnp.where(s < ncc, ncc - 1 - s, nc + ncc - 1 - s) if reverse else s


def _gdn_specs(nc, ncc, backward):
    step = (lambda s: nc - 1 - s) if backward else (lambda s: s)
    chunk = [lambda s, d=d: _scan_chunk(step(s), nc, ncc, d == 1) for d in range(_N_DIR)]
    tok = [pl.BlockSpec((_C, 3 * GDN_W), lambda s, d=d: (chunk[d](s), 0)) for d in range(_N_DIR)]
    park = [0, nc - ncc - 1]
    out = [pl.BlockSpec((_C, GDN_W), lambda s, d=d: (jnp.where(chunk[d](s) >= ncc, chunk[d](s) - ncc, park[d]), 0))
           for d in range(_N_DIR)]
    per_tok = pl.BlockSpec((_N_DIR, GDN_HEADS, _C, HEAD_DIM), lambda s: (0, 0, step(s), 0))
    mat = pl.BlockSpec((_N_DIR, GDN_HEADS, None, _C, _C), lambda s: (0, 0, step(s), 0, 0))
    row = pl.BlockSpec((_N_DIR, GDN_HEADS, None, 1, HEAD_DIM), lambda s: (0, 0, step(s), 0, 0))
    state = pl.BlockSpec((_N_DIR, GDN_HEADS, None, HEAD_DIM, HEAD_DIM), lambda s: (0, 0, step(s), 0, 0))
    return tok, out, per_tok, mat, row, state, chunk


def _head_cols(h, part):
    return slice((part * GDN_HEADS + h) * HEAD_DIM, (part * GDN_HEADS + h + 1) * HEAD_DIM)


def _gdn_heads(qkv_refs, factor_refs, state_of):
    return [[qkv_refs[d][:, _head_cols(h, 0)], qkv_refs[d][:, _head_cols(h, 1)], qkv_refs[d][:, _head_cols(h, 2)]]
            + [r[d, h] for r in factor_refs] + [state_of(d, h)]
            for d in range(_N_DIR) for h in range(GDN_HEADS)]


def _gdn_fwd_call(ncc, qkv, factors):
    t = qkv.shape[0]
    nc = t // _C
    tok, out, per_tok, mat, row, state, _ = _gdn_specs(nc, ncc, False)

    def body(*refs):
        qkv_refs, f_refs = refs[:_N_DIR], refs[_N_DIR:_N_DIR + 7]
        o_refs, sall_ref, s_ref = refs[_N_DIR + 7:2 * _N_DIR + 7], refs[2 * _N_DIR + 7], refs[-1]

        @pl.when(pl.program_id(0) == 0)
        def _():
            s_ref[...] = jnp.zeros_like(s_ref)

        heads = _gdn_heads(qkv_refs, f_refs, lambda d, h: s_ref[d, h])
        o, s_new = _gdn_chunks(heads, _blockdiag_mask())
        for d in range(_N_DIR):
            for h in range(GDN_HEADS):
                i = GDN_HEADS * d + h
                sall_ref[d, h] = heads[i][10]
                o_refs[d][:, _head_cols(h, 0)] = o[i]
                s_ref[d, h] = s_new[i]

    o_shape = jax.ShapeDtypeStruct((t - ncc * _C, GDN_W), F32)
    s_shape = (_N_DIR, GDN_HEADS, nc, HEAD_DIM, HEAD_DIM)
    return pl.pallas_call(
        body, name="gdn_f", grid=(nc,),
        in_specs=[*tok, per_tok, per_tok, per_tok, per_tok, mat, mat, row],
        out_specs=[*out, state], out_shape=[o_shape, o_shape, jax.ShapeDtypeStruct(s_shape, F32)],
        scratch_shapes=[pltpu.VMEM((_N_DIR, GDN_HEADS, HEAD_DIM, HEAD_DIM), F32)],
        compiler_params=pltpu.CompilerParams(dimension_semantics=("arbitrary",), vmem_limit_bytes=VMEM_LIMIT),
    )(qkv, qkv, *factors)


def _gdn_bwd_call(ncc, qkv, factors, sall, dos):
    t = qkv.shape[0]
    nc = t // _C
    tok, out, per_tok, mat, row, state, chunk = _gdn_specs(nc, ncc, True)

    def body(*refs):
        qkv_refs, f_refs, sall_ref = refs[:_N_DIR], refs[_N_DIR:_N_DIR + 7], refs[_N_DIR + 7]
        do_refs = refs[_N_DIR + 8:2 * _N_DIR + 8]
        dqkv_refs = refs[2 * _N_DIR + 8:3 * _N_DIR + 8]
        df_refs, ds_ref = refs[3 * _N_DIR + 8:3 * _N_DIR + 15], refs[-1]

        @pl.when(pl.program_id(0) == 0)
        def _():
            ds_ref[...] = jnp.zeros_like(ds_ref)

        bd = _blockdiag_mask()
        heads = _gdn_heads(qkv_refs, f_refs, lambda d, h: sall_ref[d, h])
        _, vjp = jax.vjp(lambda hs: _gdn_chunks(hs, bd), heads)
        live = [chunk[d](pl.program_id(0)) >= ncc for d in range(_N_DIR)]
        (all_grads,) = vjp(([jnp.where(live[d], do_refs[d][:, _head_cols(h, 0)], 0.0)
                             for d in range(_N_DIR) for h in range(GDN_HEADS)],
                            [ds_ref[d, h] for d in range(_N_DIR) for h in range(GDN_HEADS)]))
        for d in range(_N_DIR):
            for h in range(GDN_HEADS):
                grads = all_grads[GDN_HEADS * d + h]
                for part in range(3):
                    dqkv_refs[d][:, _head_cols(h, part)] = grads[part]
                for ref, g in zip(df_refs, grads[3:10]):
                    ref[d, h] = g
                ds_ref[d, h] = grads[10]

    shp = lambda a: jax.ShapeDtypeStruct(a.shape, F32)
    res = pl.pallas_call(
        body, name="gdn_b", grid=(nc,),
        in_specs=[*tok, per_tok, per_tok, per_tok, per_tok, mat, mat, row, state, *out],
        out_specs=[*tok, per_tok, per_tok, per_tok, per_tok, mat, mat, row],
        out_shape=[shp(qkv), shp(qkv)] + [shp(a) for a in factors],
        scratch_shapes=[pltpu.VMEM((_N_DIR, GDN_HEADS, HEAD_DIM, HEAD_DIM), F32)],
        compiler_params=pltpu.CompilerParams(dimension_semantics=("arbitrary",), vmem_limit_bytes=VMEM_LIMIT),
    )(qkv, qkv, *factors, sall, *dos)
    return res[0] + res[1], tuple(res[_N_DIR:])


@functools.partial(jax.custom_vjp, nondiff_argnums=(0,))
def gdn_scan(ncc, qkv, factors):
    o0, o1, _ = _gdn_fwd_call(ncc, qkv, factors)
    return o0, o1


def _gdn_scan_fwd(ncc, qkv, factors):
    o0, o1, sall = _gdn_fwd_call(ncc, qkv, factors)
    return (o0, o1), (qkv, factors, sall)


def _gdn_scan_bwd(ncc, res, dos):
    qkv, factors, sall = res
    return _gdn_bwd_call(ncc, qkv, factors, sall, list(dos))


gdn_scan.defvjp(_gdn_scan_fwd, _gdn_scan_bwd)


def _rope_tables(n, cl):
    t = np.arange(n)
    inv_freq = (ROPE_THETA ** (-np.arange(0, HEAD_DIM // 2, 2, dtype=np.float32) / (HEAD_DIM // 2))).astype(np.float32)
    ang_r = (t // GRID_W).astype(np.float32)[:, None] * inv_freq
    ang_c = (t % GRID_W).astype(np.float32)[:, None] * inv_freq
    cos = np.concatenate([np.cos(ang_r), np.cos(ang_r), np.cos(ang_c), np.cos(ang_c)], axis=1)
    sin = np.concatenate([-np.sin(ang_r), np.sin(ang_r), -np.sin(ang_c), np.sin(ang_c)], axis=1)
    cos_all = np.concatenate([np.ones((cl, HEAD_DIM), np.float32), cos], axis=0)
    sin_all = np.concatenate([np.zeros((cl, HEAD_DIM), np.float32), sin], axis=0)
    j = np.arange(HEAD_DIM)
    src = np.where((j % 64) < 32, j + 32, j - 32)
    perm = np.zeros((HEAD_DIM, HEAD_DIM), np.float32)
    perm[src, j] = 1.0
    return (jnp.asarray(cos.astype(np.float32)), jnp.asarray(sin.astype(np.float32)),
            jnp.asarray(cos_all), jnp.asarray(sin_all), jnp.asarray(perm))


def _gdn_factors(log_a, beta, ncc):
    t = log_a.shape[0]
    nc = t // _C
    la = log_a.reshape(nc, _C, _N_DIR, GDN_HEADS).transpose(2, 3, 0, 1)
    be = beta.reshape(nc, _C, _N_DIR, GDN_HEADS).transpose(2, 3, 0, 1)
    scan_order = lambda a: jnp.stack([a[0], jnp.concatenate([jnp.flip(a[1][:, :ncc], axis=1),
                                                              jnp.flip(a[1][:, ncc:], axis=1)], axis=1)])
    la, be = scan_order(la), scan_order(be)
    rev = jnp.asarray(np.array([False, True])[:, None, None, None])
    run = jnp.cumsum(la, axis=3)
    gam = jnp.where(rev, jnp.sum(la, axis=3, keepdims=True) - run + la, run)
    idx = np.arange(_C)
    incl = jnp.asarray(np.stack([idx[:, None] >= idx[None, :], idx[:, None] <= idx[None, :]])[:, None, None])
    strict = jnp.asarray(np.stack([idx[:, None] > idx[None, :], idx[:, None] < idx[None, :]])[:, None, None])
    dec = jnp.exp(jnp.where(incl, gam[..., :, None] - gam[..., None, :], -jnp.inf))
    m1 = jnp.where(strict, be[..., :, None] * dec, 0.0)
    e = jnp.exp(gam)
    g_last = jnp.where(rev, gam[..., :1], gam[..., -1:])
    lanes = lambda a: jnp.broadcast_to(a.reshape(_N_DIR, GDN_HEADS, t, 1), (_N_DIR, GDN_HEADS, t, HEAD_DIM))
    gl = jnp.broadcast_to(jnp.exp(g_last)[..., None], (_N_DIR, GDN_HEADS, nc, 1, HEAD_DIM))
    return lanes(be), lanes(be * e), lanes(e), lanes(jnp.exp(g_last - gam)), m1, dec, gl


def local_loss(x, wz, wb, ws, c, ctx, target):
    return channel_mixing(token_mixing(x, wz, wb, ws, c, ctx), wz, wb, ws, target)


def token_mixing(x, wz, wb, ws, c, ctx):
    n, cl = x.shape[0], ctx.shape[0]
    cos_q, sin_q, cos_k, sin_k, perm = _rope_tables(n, cl)

    sc_in = jnp.concatenate([jax.nn.silu(c), jax.nn.silu(ws["c_ctx"])[None, :], jnp.zeros((14, D_MODEL), F32)], axis=0)
    mod = pmm(sc_in, wb["w_mod"], wz["w_mod"], "mm_mod") + ws["b_mod"]
    sh1, sc1, g1, sh2, sc2, g2 = [mod[0:1, i * D_MODEL:(i + 1) * D_MODEL] for i in range(6)]
    csh1, csc1 = mod[1:2, 0:D_MODEL], mod[1:2, D_MODEL:2 * D_MODEL]

    def norm_mod(a, sh, sc):
        return (_rms(a) * (1.0 + sc) + sh,)

    (hx,) = rowop(norm_mod, "normmod_x", (x,), (sh1, sc1))
    (hc,) = rowop(norm_mod, "normmod_c", (ctx,), (csh1, csc1))
    h_all = jnp.concatenate([hc, hx], axis=0)
    p_main = pmm_t(h_all, wb["w_in_main"], wz["w_in_main"], "mm_in")
    p_small = pmm_t(h_all, wb["w_in_small"], wz["w_in_small"], "mm_ins")
    ak, av, qkv, aq, z, gate = jnp.split(p_main, [KV_W, 2 * KV_W, SMALL_AT, SMALL_AT + Q_W, SMALL_AT + Q_W + GDN_W],
                                         axis=1)
    db, da = p_small[:, :2 * GDN_HEADS], p_small[:, 2 * GDN_HEADS:4 * GDN_HEADS]

    def qk_prep(nh):
        def fn(a, w, cos, sin, pm):
            outs = []
            for ah in _heads(a, nh):
                y = _rms(ah) * w
                outs.append(y * cos + _pdot(y, pm) * sin)
            return (jnp.concatenate(outs, axis=1),)
        return fn

    (q_x,) = rowop(qk_prep(ATTN_HEADS), "q_prep", (aq[cl:],), (ws["q_norm_w"],), (cos_q, sin_q), (perm,))
    (k_all,) = rowop(qk_prep(ATTN_KV_HEADS), "k_prep", (ak,), (ws["k_norm_w"],), (cos_k, sin_k), (perm,))
    attn_x = attention(q_x, k_all, av)

    cw = ws["conv_qkv_w"]
    normed = jnp.asarray(np.repeat([1.0, 1.0, 0.0], GDN_W)[None, :], F32)
    scale = jnp.asarray(np.repeat([HEAD_DIM ** -0.5, 1.0, 1.0], GDN_W)[None, :], F32)

    def gdn_prep(a, w0, w1, w2, nf, sc):
        s = jax.nn.silu(_conv3(a, w0, w1, w2, (0, cl)))
        inv = lax.rsqrt(jnp.sum(s * s, axis=-1, keepdims=True) + NORM_EPS)
        return (s * jnp.where(nf > 0.0, inv * sc, 1.0),)

    (qkvn,) = colop(gdn_prep, "gdn_prep", (qkv, cw[0:1], cw[1:2], cw[2:3], normed, scale),
                    [(i, 0) for i in range(6)], 2, 3 * GDN_HEADS)
    beta = jax.nn.sigmoid(db).reshape(-1, 2, GDN_HEADS)
    log_a = -jnp.exp(ws["a_log"])[None] * jax.nn.softplus(da.reshape(-1, 2, GDN_HEADS) + ws["dt_bias"][None])
    o_fwd, o_rev = gdn_scan(cl // _C, qkvn, _gdn_factors(log_a, beta, cl // _C))
    o_x = o_fwd + o_rev

    def gdn_out(o, zz, w):
        outs = [_rms(oh) * w * jax.nn.silu(zh) for oh, zh in zip(_heads(o, GDN_HEADS), _heads(zz, GDN_HEADS))]
        return (jnp.concatenate(outs, axis=1),)

    (gdn_x,) = rowop(gdn_out, "gdn_out", (o_x, z[cl:]), (ws["gdn_norm_w"],))
    return dict(x=x, attn_x=attn_x, gdn_x=gdn_x, gate=gate[cl:], g1=g1, sh2=sh2, sc2=sc2, g2=g2)


def channel_mixing(mixed, wz, wb, ws, target):
    x, attn_x, gdn_x, gate = mixed["x"], mixed["attn_x"], mixed["gdn_x"], mixed["gate"]
    g1, sh2, sc2, g2 = mixed["g1"], mixed["sh2"], mixed["sc2"], mixed["g2"]
    pa = pmm(attn_x, wb["w_pa"], wz["w_pa"], "mm_pa")
    pd = pmm(gdn_x, wb["w_pd"], wz["w_pd"], "mm_pd")

    def merge(a, d, g):
        return (jax.nn.sigmoid(g[:, :D_MODEL]) * a + jax.nn.sigmoid(g[:, D_MODEL:]) * d,)

    (y,) = rowop(merge, "merge", (pa, pd, gate))
    mo = pmm(y, wb["w_out"], wz["w_out"], "mm_out")

    def res_norm_mod(xx, m, g, sh, sc):
        x1 = xx + g * m
        return x1, _rms(x1) * (1.0 + sc) + sh

    x1, h2 = rowop(res_norm_mod, "res1", (x, mo), (g1, sh2, sc2))
    up = pmm(h2, wb["w_up"], wz["w_up"], "mm_up")
    fw = ws["ffn_conv_w"]

    def ffn_act(ug, uv, w0g, w0v, w1g, w1v, w2g, w2v, bg, bv):
        g = _conv3(ug, w0g, w1g, w2g, (0,)) + bg
        v = _conv3(uv, w0v, w1v, w2v, (0,)) + bv
        return (jax.nn.silu(g) * v,)

    half = D_FF // HEAD_DIM
    (act,) = colop(ffn_act, "ffn_act", (up, fw[0:1], fw[1:2], fw[2:3], ws["ffn_conv_b"]),
                   [(i, off) for i in range(5) for off in (0, half)], 0, half)
    dn = pmm(act, wb["w_down"], wz["w_down"], "mm_down")

    def head(xx, m, g, w, tgt):
        yy = _rms(xx + g * m) * w
        err = (yy - tgt) ** 2
        return (jnp.broadcast_to(0.5 * jnp.mean(err, axis=-1, keepdims=True), (xx.shape[0], HEAD_DIM)),)

    (row_loss,) = rowop(head, "head", (x1, dn), (g2, ws["final_norm_w"][None, :]), (target,))
    return jnp.sum(row_loss[:, 0])


_HBM = pl.BlockSpec(memory_space=pltpu.HBM)


def _chip_peers():
    x, y = lax.axis_index("x"), lax.axis_index("y")
    return [(1 - x, y), (x, 1 - y), (1 - x, 1 - y)]


_SPLIT_COLS = ("w_in",)


def _half_of(view, nm, idx, lead=0):
    r, cdim = view.shape[-2:]
    pre = (slice(None),) * lead
    if nm in _SPLIT_COLS:
        return view.at[pre + (slice(None), pl.ds(pl.multiple_of(idx * (cdim // 2), 128), cdim // 2))]
    return view.at[pre + (pl.ds(pl.multiple_of(idx * (r // 2), 16), r // 2), slice(None))]


def _remote(src, dst, send_sem, recv_sem, dev):
    return pltpu.make_async_remote_copy(src_ref=src, dst_ref=dst, send_sem=send_sem, recv_sem=recv_sem,
                                        device_id=dev, device_id_type=MESH)


def _hbm_call(body, name, ins, out_shape, n_sems, in_place=False):
    names = tuple(ins)
    return dict(zip(names, pl.pallas_call(
        body, name=name, in_specs=[_HBM] * len(names), out_specs=[_HBM] * len(names),
        out_shape=[out_shape(nm, ins[nm]) for nm in names],
        scratch_shapes=[pltpu.SemaphoreType.DMA((k,)) for k in n_sems],
        input_output_aliases={i: i for i in range(len(names))} if in_place else {},
    )(*[ins[nm] for nm in names])))


def all_gather_chips(shards):
    names = tuple(shards)
    n = len(names)

    def body(*refs):
        ins, outs = dict(zip(names, refs[:n])), dict(zip(names, refs[n:2 * n]))
        ici_send, ici_recv, d2d_send, d2d_recv, own_send, own_recv = refs[2 * n:]
        x, y, c = lax.axis_index("x"), lax.axis_index("y"), lax.axis_index("c")
        me, sib = 2 * x + y, (x, y, 1 - c)
        own = [_remote(ins[nm], outs[nm].at[me], own_send.at[i], own_recv.at[i], sib) for i, nm in enumerate(names)]
        for cp in own:
            cp.start()
        sends = []
        for k, (px, py) in enumerate(_chip_peers()):
            for i, nm in enumerate(names):
                cp = _remote(_half_of(ins[nm], nm, c), _half_of(outs[nm].at[me], nm, c), ici_send.at[k * n + i],
                             ici_recv.at[k * n + i], (px, py, c))
                cp.start()
                sends.append(cp)
        for k, (px, py) in enumerate(_chip_peers()):
            for i, nm in enumerate(names):
                landed = _half_of(outs[nm].at[2 * px + py], nm, c)
                _remote(landed, landed, ici_send.at[k * n + i], ici_recv.at[k * n + i], (px, py, c)).wait_recv()
                fw = _remote(landed, landed, d2d_send.at[k * n + i], d2d_recv.at[k * n + i], sib)
                fw.start()
                sends.append(fw)
        for k, (px, py) in enumerate(_chip_peers()):
            for i, nm in enumerate(names):
                other = _half_of(outs[nm].at[2 * px + py], nm, 1 - c)
                _remote(other, other, d2d_send.at[k * n + i], d2d_recv.at[k * n + i], sib).wait_recv()
        for cp in sends:
            cp.wait_send()
        for cp in own:
            cp.wait()

    return _hbm_call(body, "ag_weights", shards, lambda nm, a: jax.ShapeDtypeStruct((N_CHIPS,) + a.shape, a.dtype),
                     (3 * n, 3 * n, 3 * n, 3 * n, n, n))


_SEM = pl.BlockSpec(memory_space=pltpu.SEMAPHORE)


def push_start(name, arrays, land_shapes, copies, n_copies):
    names = tuple(arrays)
    n = len(names)

    def body(*refs):
        send_sems, recv_sems, token = refs[2 * n], refs[2 * n + 1], refs[-1]
        for j, (src, dst, dev) in enumerate(copies(refs[:n], refs[n:2 * n])):
            _remote(src, dst, send_sems.at[j], recv_sems.at[j], dev).start()
        token[...] = jnp.zeros_like(token)

    hbm = lambda a: pltpu.with_memory_space_constraint(a, pltpu.HBM)
    lands = [lax.empty(land_shapes[nm], arrays[nm].dtype) for nm in names]
    res = pl.pallas_call(
        body, name=name,
        out_shape=(pltpu.SemaphoreType.DMA((n_copies,)), pltpu.SemaphoreType.DMA((n_copies,)),
                   *[pltpu.HBM(arrays[nm].shape, arrays[nm].dtype) for nm in names],
                   *[pltpu.HBM(a.shape, a.dtype) for a in lands], jax.ShapeDtypeStruct((8, 128), F32)),
        in_specs=[_HBM] * (2 * n),
        out_specs=(_SEM, _SEM, *[_HBM] * (2 * n), pl.BlockSpec(memory_space=pltpu.VMEM)),
        input_output_aliases={i: 2 + i for i in range(2 * n)},
        compiler_params=pltpu.CompilerParams(has_side_effects=pltpu.SideEffectType.DATAFLOW_SIDE_EFFECTING),
    )(*[hbm(arrays[nm]) for nm in names], *[hbm(a) for a in lands])
    return names, res[0], res[1], res[2:2 + n], res[2 + n:2 + 2 * n], res[-1]


def push_wait(name, started, copies, after):
    names, send_sems, recv_sems, srcs, lands, _ = started
    n = len(names)

    def body(*refs):
        send_ref, recv_ref = refs[2 * n], refs[2 * n + 1]
        for j, (src, dst, dev) in enumerate(copies(refs[:n], refs[n:2 * n])):
            cp = _remote(src, dst, send_ref.at[j], recv_ref.at[j], dev)
            cp.wait_send()
            cp.wait_recv()

    res = pl.pallas_call(
        body, name=name,
        out_shape=(*[pltpu.HBM(a.shape, a.dtype) for a in srcs], *[pltpu.HBM(a.shape, a.dtype) for a in lands]),
        in_specs=[_HBM] * (2 * n) + [_SEM, _SEM, pl.BlockSpec(memory_space=pl.ANY)],
        out_specs=tuple([_HBM] * (2 * n)),
        input_output_aliases={i: i for i in range(2 * n)},
        compiler_params=pltpu.CompilerParams(has_side_effects=pltpu.SideEffectType.DATAFLOW_SIDE_EFFECTING),
    )(*srcs, *lands, send_sems, recv_sems, after)
    return dict(zip(names, res[n:]))


def _gather_copies(srcs, lands):
    x, y, c = lax.axis_index("x"), lax.axis_index("y"), lax.axis_index("c")
    devs = [(px, py, c) for px, py in _chip_peers()] + [(x, y, 1 - c)]
    return [(src, land.at[2 * x + y], dev) for src, land in zip(srcs, lands) for dev in devs]


def _scatter_copies(srcs, lands):
    c = lax.axis_index("c")
    return [(src.at[2 * px + py], land.at[k], (px, py, c))
            for src, land in zip(srcs, lands) for k, (px, py) in enumerate(_chip_peers())]


def sibling_halves(blocks, name):
    names = tuple(blocks)

    def body(*refs):
        n = len(names)
        ins, outs = dict(zip(names, refs[:n])), dict(zip(names, refs[n:2 * n]))
        send_sems, recv_sems = refs[2 * n:]
        x, y, c = lax.axis_index("x"), lax.axis_index("y"), lax.axis_index("c")
        cps = [_remote(_half_of(ins[nm], nm, 1 - c, lead=1), outs[nm], send_sems.at[i], recv_sems.at[i], (x, y, 1 - c))
               for i, nm in enumerate(names)]
        for cp in cps:
            cp.start()
        for cp in cps:
            cp.wait()

    def half_shape(nm, a):
        r, cdim = a.shape[-2:]
        return jax.ShapeDtypeStruct((N_CHIPS, r, cdim // 2) if nm in _SPLIT_COLS else (N_CHIPS, r // 2, cdim), a.dtype)

    return _hbm_call(body, name, blocks, half_shape, (len(names), len(names)))


def scatter_halves(blocks):
    names = tuple(blocks)
    n = len(names)

    def body(*refs):
        ins, outs = dict(zip(names, refs[:n])), dict(zip(names, refs[n:2 * n]))
        send_sems, recv_sems = refs[2 * n:]
        c = lax.axis_index("c")
        cps = [_remote(ins[nm].at[2 * px + py], outs[nm].at[k], send_sems.at[k * n + i], recv_sems.at[k * n + i],
                       (px, py, c))
               for k, (px, py) in enumerate(_chip_peers()) for i, nm in enumerate(names)]
        for cp in cps:
            cp.start()
        for cp in cps:
            cp.wait_recv()
        for cp in cps:
            cp.wait_send()

    return _hbm_call(body, "rs_grads", blocks, lambda nm, a: jax.ShapeDtypeStruct((3,) + a.shape[1:], a.dtype),
                     (3 * n, 3 * n))


def sibling_assemble(arrays):
    names = tuple(arrays)

    def body(*refs):
        n = len(names)
        ins, outs = dict(zip(names, refs[:n])), dict(zip(names, refs[n:2 * n]))
        send_sems, recv_sems = refs[2 * n:]
        x, y, c = lax.axis_index("x"), lax.axis_index("y"), lax.axis_index("c")
        cps = [_remote(_half_of(ins[nm], nm, c), _half_of(outs[nm], nm, c), send_sems.at[i], recv_sems.at[i],
                       (x, y, 1 - c)) for i, nm in enumerate(names)]
        for cp in cps:
            cp.start()
        for i, nm in enumerate(names):
            other = _half_of(outs[nm], nm, 1 - c)
            _remote(other, other, send_sems.at[i], recv_sems.at[i], (x, y, 1 - c)).wait_recv()
        for cp in cps:
            cp.wait_send()

    return _hbm_call(body, "rs_assemble", arrays, lambda nm, a: jax.ShapeDtypeStruct(a.shape, a.dtype),
                     (len(names), len(names)), in_place=True)


def all_reduce_small(v):
    def body(v_ref, tot_ref, gath_ref, send_sems, recv_sems):
        x, y, c = lax.axis_index("x"), lax.axis_index("y"), lax.axis_index("c")
        me = 4 * x + 2 * y + c
        gath_ref[me] = v_ref[...]

        def peer(k):
            m = k + 1
            return (x ^ (m >> 2 & 1), y ^ (m >> 1 & 1), c ^ (m & 1))

        sends = [pltpu.make_async_remote_copy(src_ref=v_ref, dst_ref=gath_ref.at[me], send_sem=send_sems.at[k],
                                              recv_sem=recv_sems.at[k], device_id=peer(k), device_id_type=MESH)
                 for k in range(N_DEV - 1)]
        for cp in sends:
            cp.start()
        for k in range(N_DEV - 1):
            px, py, pc = peer(k)
            pltpu.make_async_remote_copy(src_ref=v_ref, dst_ref=gath_ref.at[4 * px + 2 * py + pc],
                                         send_sem=send_sems.at[k], recv_sem=recv_sems.at[k], device_id=peer(k),
                                         device_id_type=MESH).wait_recv()
        for cp in sends:
            cp.wait_send()
        acc = gath_ref[0]
        for d in range(1, N_DEV):
            acc = acc + gath_ref[d]
        tot_ref[...] = acc

    vm = pl.BlockSpec(memory_space=pltpu.VMEM)
    return pl.pallas_call(
        body, name="ar_small", in_specs=[vm], out_specs=[vm, vm],
        out_shape=(jax.ShapeDtypeStruct(v.shape, v.dtype), jax.ShapeDtypeStruct((N_DEV,) + v.shape, v.dtype)),
        scratch_shapes=[pltpu.SemaphoreType.DMA((N_DEV - 1,)), pltpu.SemaphoreType.DMA((N_DEV - 1,))],
    )(v)[0]


def _elementwise(fn, name, ins, n_out, out_dtype=F32):
    r, cdim = ins[0].shape
    tr = _pick(r, tuple(p for p in (488, 256, 128, 104, 64, 32, 16, 8) if p * cdim * 4 <= 2 * 1024 * 1024))
    spec = pl.BlockSpec((tr, cdim), lambda i: (i, 0))

    def body(*refs):
        res = fn(*[ref[...] for ref in refs[:len(ins)]])
        for o_ref, v in zip(refs[len(ins):], res):
            o_ref[...] = v

    return pl.pallas_call(
        body, name=name, grid=(r // tr,), in_specs=[spec] * len(ins), out_specs=[spec] * n_out,
        out_shape=tuple(jax.ShapeDtypeStruct((r, cdim), out_dtype) for _ in range(n_out)),
        compiler_params=pltpu.CompilerParams(dimension_semantics=("parallel",), vmem_limit_bytes=VMEM_LIMIT),
    )(*ins)


def _half_block_specs(nm, shard_shape):
    r, cdim = shard_shape
    if nm in _SPLIT_COLS:
        return (None, r, cdim // 2), (lambda j, c: (j, 0, c))
    return (None, r // 2, cdim), (lambda j, c: (j, c, 0))


def _presum(nm, sel, g32, a):
    blk, at = _half_block_specs(nm, g32.shape[1:])

    def body(s_ref, g_ref, a_ref, o_ref):
        del s_ref
        o_ref[...] = (g_ref[...] + a_ref[...]).astype(BF16)

    return pl.pallas_call(
        body, name="rs_presum_" + nm,
        grid_spec=pltpu.PrefetchScalarGridSpec(
            num_scalar_prefetch=1, grid=(N_CHIPS,),
            in_specs=[pl.BlockSpec(blk, lambda j, s: at(j, s[0])), pl.BlockSpec(blk, lambda j, s: (j, 0, 0))],
            out_specs=pl.BlockSpec(blk, lambda j, s: (j, 0, 0))),
        out_shape=jax.ShapeDtypeStruct(a.shape, BF16),
        compiler_params=pltpu.CompilerParams(dimension_semantics=("parallel",), vmem_limit_bytes=VMEM_LIMIT),
    )(sel, g32, a)


def _finalsum(nm, sel, g32, a, got):
    blk, at = _half_block_specs(nm, g32.shape[1:])

    def body(s_ref, g_ref, a_ref, r_ref, o_ref):
        del s_ref
        acc = g_ref[...] + a_ref[...]
        for k in range(3):
            acc = acc + r_ref[k].astype(F32)
        o_ref[...] = acc

    return pl.pallas_call(
        body, name="rs_final_" + nm,
        grid_spec=pltpu.PrefetchScalarGridSpec(
            num_scalar_prefetch=1, grid=(1,),
            in_specs=[pl.BlockSpec(blk, lambda i, s: at(s[1], s[0])), pl.BlockSpec(blk, lambda i, s: (s[1], 0, 0)),
                      pl.BlockSpec(got.shape, lambda i, s: (0, 0, 0))],
            out_specs=pl.BlockSpec(blk[1:], lambda i, s: at(0, s[0])[1:])),
        out_shape=jax.ShapeDtypeStruct(g32.shape[1:], F32),
        compiler_params=pltpu.CompilerParams(dimension_semantics=("arbitrary",), vmem_limit_bytes=VMEM_LIMIT),
    )(sel, g32, a, got)


def _adamw(w, g, m, v, name):
    shape = w.shape
    to2 = lambda a: a.reshape(-1, shape[-1])

    def fn(w_, g_, m_, v_):
        m_new = ADAM_B1 * m_ + (1.0 - ADAM_B1) * g_
        v_new = ADAM_B2 * v_ + (1.0 - ADAM_B2) * (g_ * g_)
        m_hat = m_new / (1.0 - ADAM_B1 ** ADAM_STEP)
        v_hat = v_new / (1.0 - ADAM_B2 ** ADAM_STEP)
        delta = -ADAM_LR * (m_hat / (jnp.sqrt(v_hat) + ADAM_EPS) + ADAM_WD * w_)
        return g_, delta, m_new, v_new

    outs = _elementwise(fn, name, [to2(a) for a in (w, g, m, v)], 4)
    return tuple(o.reshape(shape) for o in outs)


_BIG = ("w_mod", "w_in", "w_pa", "w_pd", "w_out", "w_up", "w_down")
_EARLY = ("w_mod", "w_in")
_LATE = ("w_pa", "w_pd", "w_out", "w_up", "w_down")
_COL_SHARDED = ("w_mod", "w_up")
_FULL_SHAPE = {"w_mod": (D_MODEL, MOD_W), "w_in": (IN_COLS, D_MODEL), "w_pa": (Q_W, D_MODEL), "w_pd": (GDN_W, D_MODEL),
               "w_out": (D_MODEL, D_MODEL), "w_up": (D_MODEL, 2 * D_FF), "w_down": (D_FF, D_MODEL)}


def _shard_shape(name):
    r, cdim = _FULL_SHAPE[name]
    return (r, cdim // N_CHIPS) if name in _COL_SHARDED else (r // N_CHIPS, cdim)


_CONV_ELEMS = 2 * (3 * CONV_W // N_CHIPS + 3 * 2 * D_FF // N_CHIPS)
_CONV_ROWS = 32


def _blocks_of_full(name, full):
    r, cdim = _FULL_SHAPE[name]
    if name in _COL_SHARDED:
        return full.reshape(r, N_CHIPS, cdim // N_CHIPS).transpose(1, 0, 2)
    return full.reshape(N_CHIPS, r // N_CHIPS, cdim)


def _full_of_blocks(name, blocks):
    r, cdim = _FULL_SHAPE[name]
    if name in _COL_SHARDED:
        return blocks.transpose(1, 0, 2).reshape(r, cdim)
    return blocks.reshape(r, cdim)


def _w_in_regroup(w_in_t):
    main = jnp.concatenate([w_in_t[:SMALL_AT], w_in_t[SMALL_AT + 4 * GDN_HEADS:]], axis=0)
    small = jnp.pad(w_in_t[SMALL_AT:SMALL_AT + 4 * GDN_HEADS], ((0, HEAD_DIM - 4 * GDN_HEADS), (0, 0)))
    return main, small


def _w_in_ungroup(main, small):
    return jnp.concatenate([main[:SMALL_AT], small[:4 * GDN_HEADS], main[SMALL_AT:]], axis=0)


_SMALL = ("c_ctx", "b_mod", "q_norm_w", "k_norm_w", "conv_qkv_w", "a_log", "dt_bias", "gdn_norm_w", "ffn_conv_w",
          "ffn_conv_b", "final_norm_w")


def _pack_small(tree, rows):
    flat = jnp.concatenate([tree[nm].reshape(-1) for nm in _SMALL])
    return jnp.pad(flat, (0, rows * 128 - flat.shape[0])).reshape(rows, 128)


def _unpack_small(packed, like):
    flat, out, off = packed.reshape(-1), {}, 0
    for nm in _SMALL:
        size = int(np.prod(like[nm].shape))
        out[nm] = flat[off:off + size].reshape(like[nm].shape)
        off += size
    return out


def kernel(x, c, ctx, c_ctx, w_mod, b_mod, w_in, q_norm_w, k_norm_w, conv_qkv_w, a_log, dt_bias, gdn_norm_w, w_pa, w_pd, w_out, w_up, ffn_conv_w, ffn_conv_b, w_down, final_norm_w, loss_target, m_c_ctx, m_w_mod, m_b_mod, m_w_in, m_q_norm_w, m_k_norm_w, m_conv_qkv_w, m_a_log, m_dt_bias, m_gdn_norm_w, m_w_pa, m_w_pd, m_w_out, m_w_up, m_ffn_conv_w, m_ffn_conv_b, m_w_down, m_final_norm_w, v_c_ctx, v_w_mod, v_b_mod, v_w_in, v_q_norm_w, v_k_norm_w, v_conv_qkv_w, v_a_log, v_dt_bias, v_gdn_norm_w, v_w_pa, v_w_pd, v_w_out, v_w_up, v_ffn_conv_w, v_ffn_conv_b, v_w_down, v_final_norm_w):
    names = ("c_ctx", "w_mod", "b_mod", "w_in", "q_norm_w", "k_norm_w", "conv_qkv_w", "a_log", "dt_bias", "gdn_norm_w",
             "w_pa", "w_pd", "w_out", "w_up", "ffn_conv_w", "ffn_conv_b", "w_down", "final_norm_w")
    w_sh = dict(c_ctx=c_ctx, w_mod=w_mod, b_mod=b_mod, w_in=w_in, q_norm_w=q_norm_w, k_norm_w=k_norm_w,
                conv_qkv_w=conv_qkv_w, a_log=a_log, dt_bias=dt_bias, gdn_norm_w=gdn_norm_w, w_pa=w_pa, w_pd=w_pd,
                w_out=w_out, w_up=w_up, ffn_conv_w=ffn_conv_w, ffn_conv_b=ffn_conv_b, w_down=w_down,
                final_norm_w=final_norm_w)
    m_sh = dict(c_ctx=m_c_ctx, w_mod=m_w_mod, b_mod=m_b_mod, w_in=m_w_in, q_norm_w=m_q_norm_w, k_norm_w=m_k_norm_w,
                conv_qkv_w=m_conv_qkv_w, a_log=m_a_log, dt_bias=m_dt_bias, gdn_norm_w=m_gdn_norm_w, w_pa=m_w_pa,
                w_pd=m_w_pd, w_out=m_w_out, w_up=m_w_up, ffn_conv_w=m_ffn_conv_w, ffn_conv_b=m_ffn_conv_b,
                w_down=m_w_down, final_norm_w=m_final_norm_w)
    v_sh = dict(c_ctx=v_c_ctx, w_mod=v_w_mod, b_mod=v_b_mod, w_in=v_w_in, q_norm_w=v_q_norm_w, k_norm_w=v_k_norm_w,
                conv_qkv_w=v_conv_qkv_w, a_log=v_a_log, dt_bias=v_dt_bias, gdn_norm_w=v_gdn_norm_w, w_pa=v_w_pa,
                w_pd=v_w_pd, w_out=v_w_out, w_up=v_w_up, ffn_conv_w=v_ffn_conv_w, ffn_conv_b=v_ffn_conv_b,
                w_down=v_w_down, final_norm_w=v_final_norm_w)
    chip = 2 * lax.axis_index("x") + lax.axis_index("y")

    conv_bits = jnp.concatenate([lax.bitcast_convert_type(w_sh[nm][0], BF16).reshape(-1)
                                 for nm in ("conv_qkv_w", "ffn_conv_w")])
    shards = {nm: w_sh[nm][0].astype(BF16).T if nm == "w_in" else w_sh[nm][0].astype(BF16) for nm in _BIG}
    shards["conv"] = jnp.pad(conv_bits, (0, _CONV_ROWS * D_MODEL - _CONV_ELEMS)).reshape(_CONV_ROWS, D_MODEL)
    gathered = all_gather_chips({nm: shards[nm] for nm in _EARLY + ("conv",)})
    gathered, late_shards = lax.optimization_barrier((gathered, {nm: shards[nm] for nm in _LATE}))
    started = push_start("ag_late_start", late_shards, {nm: (N_CHIPS,) + a.shape for nm, a in late_shards.items()},
                         _gather_copies, 4 * len(_LATE))
    c = c + started[-1][0:1, 0:1]

    wb = {nm: _full_of_blocks(nm, gathered[nm]) for nm in _EARLY}
    wb["w_in_main"], wb["w_in_small"] = _w_in_regroup(wb.pop("w_in"))
    conv_all = gathered["conv"].reshape(N_CHIPS, -1)[:, :_CONV_ELEMS]
    n_cq = 2 * 3 * CONV_W // N_CHIPS
    unbits = lambda a, w: lax.bitcast_convert_type(a.reshape(N_CHIPS, 3, w // N_CHIPS, 2), F32).transpose(1, 0, 2).reshape(3, w)
    ws = dict(c_ctx=c_ctx, b_mod=b_mod, q_norm_w=q_norm_w, k_norm_w=k_norm_w, a_log=a_log[0], dt_bias=dt_bias[0],
              gdn_norm_w=gdn_norm_w, ffn_conv_b=ffn_conv_b, final_norm_w=final_norm_w,
              conv_qkv_w=unbits(conv_all[:, :n_cq], CONV_W), ffn_conv_w=unbits(conv_all[:, n_cq:], 2 * D_FF))
    wz = {nm: jnp.zeros(a.shape, F32) for nm, a in wb.items()}
    wz.update({nm: jnp.zeros(_FULL_SHAPE[nm], F32) for nm in _LATE})

    mixed, vjp_mix = jax.vjp(lambda x_, wz_, ws_: token_mixing(x_, wz_, wb, ws_, c, ctx[0]), x[0], wz, ws)
    got = push_wait("ag_late_wait", started, _gather_copies, mixed["gdn_x"])
    wb_late = {nm: _full_of_blocks(nm, got[nm]) for nm in _LATE}
    loss_local, vjp_chan = jax.vjp(
        lambda mixed_, wz_, ws_: channel_mixing(mixed_, wz_, wb_late, ws_, loss_target[0]), mixed, wz, ws)
    d_mixed, gz_chan, gs_chan = vjp_chan(jnp.ones((), F32))

    sel = jnp.stack([lax.axis_index("c"), chip]).astype(jnp.int32)
    g32_late = {nm: _blocks_of_full(nm, gz_chan[nm]) for nm in _LATE}
    theirs_late = sibling_halves(g32_late, "rs_sibling_late")
    sums_late = {nm: _presum(nm, sel, g32_late[nm], theirs_late[nm]) for nm in _LATE}
    scattering = push_start("rs_late_start", sums_late, {nm: (3,) + a.shape[1:] for nm, a in sums_late.items()},
                            _scatter_copies, 3 * len(_LATE))
    d_mixed = {**d_mixed, "gdn_x": d_mixed["gdn_x"] + scattering[-1][0:1, 0:1]}
    gx, gz_mix, gs_mix = vjp_mix(d_mixed)
    gs = jax.tree.map(jnp.add, gs_mix, gs_chan)
    got_late = push_wait("rs_late_wait", scattering, _scatter_copies, gx)
    loss = lax.psum(loss_local, ("x", "y", "c"))

    gz_mix["w_in"] = _w_in_ungroup(gz_mix.pop("w_in_main"), gz_mix.pop("w_in_small"))
    g32 = {nm: _blocks_of_full(nm, gz_mix[nm]) for nm in _EARLY}
    theirs = sibling_halves(g32, "rs_sibling")
    got = scatter_halves({nm: _presum(nm, sel, g32[nm], theirs[nm]) for nm in _EARLY})
    g32.update(g32_late), theirs.update(theirs_late), got.update(got_late)
    g_big = sibling_assemble({nm: _finalsum(nm, sel, g32[nm], theirs[nm], got[nm]) for nm in _BIG})

    gs["a_log"], gs["dt_bias"] = gs["a_log"][None], gs["dt_bias"][None]
    like = {nm: gs[nm] for nm in _SMALL}
    small_rows = -(-sum(int(np.prod(like[nm].shape)) for nm in _SMALL) // 1024) * 8
    g_small = _unpack_small(all_reduce_small(_pack_small(gs, small_rows)), like)
    for nm, width in (("conv_qkv_w", CONV_W), ("ffn_conv_w", 2 * D_FF)):
        g_small[nm] = lax.dynamic_slice_in_dim(g_small[nm], chip * (width // N_CHIPS), width // N_CHIPS, axis=1)[None]

    grads, deltas, new_m, new_v = {}, {}, {}, {}
    for nm in _BIG:
        g = g_big[nm].T if nm == "w_in" else g_big[nm]
        grads[nm], deltas[nm], new_m[nm], new_v[nm] = (
            o[None] for o in _adamw(w_sh[nm][0], g, m_sh[nm][0], v_sh[nm][0], "adamw_" + nm))
    shard_like = {nm: w_sh[nm] for nm in _SMALL}
    rows_l = -(-sum(int(np.prod(shard_like[nm].shape)) for nm in _SMALL) // 1024) * 8
    g_l = _pack_small({nm: g_small[nm].reshape(w_sh[nm].shape) for nm in _SMALL}, rows_l)
    outs = _adamw(_pack_small(w_sh, rows_l), g_l, _pack_small(m_sh, rows_l), _pack_small(v_sh, rows_l), "adamw_small")
    for tree, packed in zip((grads, deltas, new_m, new_v), outs):
        tree.update(_unpack_small(packed, shard_like))

    return (loss, gx[None], *[grads[nm] for nm in names], *[deltas[nm] for nm in names],
            *[new_m[nm] for nm in names], *[new_v[nm] for nm in names])
```

```python
import functools
import math

import jax
import jax.numpy as jnp
import numpy as np
from jax import lax
from jax.experimental import pallas as pl
from jax.experimental.pallas import tpu as pltpu

F32 = jnp.float32
BF16 = jnp.bfloat16
HIGHEST = lax.Precision.HIGHEST
MESH = pl.DeviceIdType.MESH

D_MODEL = 1024
GRID_W = 64
ATTN_HEADS = 8
ATTN_KV_HEADS = 2
ATTN_GROUP = ATTN_HEADS // ATTN_KV_HEADS
HEAD_DIM = 128
ROPE_THETA = 10000.0
GDN_HEADS = 8
GDN_CHUNK = 64
D_FF = 2816
NORM_EPS = 1e-6
KV_W = ATTN_KV_HEADS * HEAD_DIM
Q_W = ATTN_HEADS * HEAD_DIM
GDN_W = GDN_HEADS * HEAD_DIM
CONV_W = 3 * GDN_W
MOD_W = 6 * D_MODEL
IN_COLS = 2 * KV_W + CONV_W + 4 * GDN_HEADS + Q_W + GDN_W + 2 * D_MODEL
IN_MAIN = IN_COLS - 4 * GDN_HEADS
SMALL_AT = 2 * KV_W + CONV_W
N_CHIPS = 4
N_DEV = 8

ADAM_LR = 0.001
ADAM_B1 = 0.9
ADAM_B2 = 0.999
ADAM_EPS = 1e-08
ADAM_WD = 0.01
ADAM_STEP = 10

VMEM_LIMIT = 48 * 1024 * 1024
MATMUL_VMEM_BUDGET = 40 * 1024 * 1024
MATMUL_STEP_BYTES = 1200 * 1024


def _pick(dim, prefs):
    for p in prefs:
        if p <= dim and dim % p == 0:
            return p
    return dim


_DIMS = {
    "nn": (((1,), (0,)), ((), ())),
    "nt": (((1,), (1,)), ((), ())),
    "tn": (((0,), (0,)), ((), ())),
}


def _matmul_plan(m, n, k, a_bytes, b_bytes):
    best = None
    for tm in (2304, 2048, 1152, 1024, 768, 512, 384, 256, 128, m):
        for tn in (2560, 1536, 1408, 1024, 768, 512, 256, 128, n):
            for tk in (3840, 2816, 2560, 2304, 2048, 1920, 1408, 1152, 1024, 768, 512, 256, 128, k):
                if tm > m or tn > n or tk > k or m % tm or n % tn or k % tk:
                    continue
                blocks = tm * tk * a_bytes + tk * tn * b_bytes + tm * tn * 4
                casts = (tm * tk * 2 if a_bytes > 2 else 0) + (tk * tn * 2 if b_bytes > 2 else 0) + tm * tn * 4
                if 2 * blocks + casts > MATMUL_VMEM_BUDGET:
                    continue
                nm, nn, nk = m // tm, n // tn, k // tk
                size_a, size_b = m * k * a_bytes, k * n * b_bytes
                for n_inner in (True, False):
                    if n_inner:
                        traffic = (size_a if nk == 1 else nn * size_a) + nm * size_b
                    else:
                        traffic = nn * size_a + (size_b if nk == 1 else nm * size_b)
                    cost = traffic + nm * nn * nk * MATMUL_STEP_BYTES + (nk - 1) * m * n * 4
                    if best is None or cost < best[0]:
                        best = (cost, tm, tn, tk, n_inner)
    return best[1:]


def _matmul(a, b, mode, name):
    if mode == "nn":
        (m, k), (_, n) = a.shape, b.shape
    elif mode == "nt":
        (m, k), (n, _) = a.shape, b.shape
    else:
        (k, m), (_, n) = a.shape, b.shape
    tm, tn, tk, n_inner = _matmul_plan(m, n, k, a.dtype.itemsize, b.dtype.itemsize)
    nk = k // tk
    ij = (lambda g0, g1: (g0, g1)) if n_inner else (lambda g0, g1: (g1, g0))
    if mode == "tn":
        a_spec = pl.BlockSpec((tk, tm), lambda g0, g1, l: (l, ij(g0, g1)[0]))
    else:
        a_spec = pl.BlockSpec((tm, tk), lambda g0, g1, l: (ij(g0, g1)[0], l))
    if mode == "nt":
        b_spec = pl.BlockSpec((tn, tk), lambda g0, g1, l: (ij(g0, g1)[1], l))
    else:
        b_spec = pl.BlockSpec((tk, tn), lambda g0, g1, l: (l, ij(g0, g1)[1]))
    dims = _DIMS[mode]

    def body(a_ref, b_ref, o_ref):
        part = lax.dot_general(a_ref[...].astype(BF16), b_ref[...].astype(BF16), dims, preferred_element_type=F32)
        if nk == 1:
            o_ref[...] = part
        else:
            l = pl.program_id(2)

            @pl.when(l == 0)
            def _():
                o_ref[...] = part

            @pl.when(l > 0)
            def _():
                o_ref[...] += part

    return pl.pallas_call(
        body,
        name=name,
        grid=(m // tm, n // tn, nk) if n_inner else (n // tn, m // tm, nk),
        in_specs=[a_spec, b_spec],
        out_specs=pl.BlockSpec((tm, tn), lambda g0, g1, l: ij(g0, g1)),
        out_shape=jax.ShapeDtypeStruct((m, n), F32),
        compiler_params=pltpu.CompilerParams(dimension_semantics=("parallel", "parallel", "arbitrary"),
                                             vmem_limit_bytes=VMEM_LIMIT),
    )(a, b)


@functools.partial(jax.custom_vjp, nondiff_argnums=(3,))
def pmm(a, w, wz, name):
    del wz
    return _matmul(a, w, "nn", name + "_f")


def _pmm_fwd(a, w, wz, name):
    del wz
    return _matmul(a, w, "nn", name + "_f"), (a, w)


def _pmm_bwd(name, res, g):
    a, w = res
    da = _matmul(g, w, "nt", name + "_da")
    if a.shape[0] < 128:
        pad = 128 - a.shape[0]
        at = jnp.pad(a.T, ((0, 0), (0, pad)))
        gp = jnp.pad(g, ((0, pad), (0, 0)))
        dw = _matmul(at, gp, "nn", name + "_dw")
    else:
        dw = _matmul(a, g, "tn", name + "_dw")
    return da, jnp.zeros_like(w), dw


pmm.defvjp(_pmm_fwd, _pmm_bwd)


@functools.partial(jax.custom_vjp, nondiff_argnums=(3,))
def pmm_t(a, wt, wtz, name):
    del wtz
    return _matmul(a, wt, "nt", name + "_f")


def _pmm_t_fwd(a, wt, wtz, name):
    del wtz
    return _matmul(a, wt, "nt", name + "_f"), (a, wt)


def _pmm_t_bwd(name, res, g):
    a, wt = res
    return _matmul(g, wt, "nn", name + "_da"), jnp.zeros_like(wt), _matmul(g, a, "tn", name + "_dw")


pmm_t.defvjp(_pmm_t_fwd, _pmm_t_bwd)


def rowop(fn, name, rows, bcs=(), crows=(), cbcs=(), tr=256):
    rows, bcs, crows, cbcs = tuple(rows), tuple(bcs), tuple(crows), tuple(cbcs)
    n_rows = rows[0].shape[0]
    tr = _pick(n_rows, (tr, 128, 64, 32, 16, 8))
    nr, nb, ncr, ncb = len(rows), len(bcs), len(crows), len(cbcs)
    n_in = nr + nb + ncr + ncb
    grid = (n_rows // tr,)

    def blk(arr):
        return jax.ShapeDtypeStruct((tr, arr.shape[1]), arr.dtype)

    def row_spec(arr):
        return pl.BlockSpec((tr, arr.shape[1]), lambda i: (i, 0))

    def bc_spec(arr):
        return pl.BlockSpec(arr.shape, lambda i: (0, 0))

    out_blk = jax.eval_shape(fn, *[blk(r) for r in rows], *bcs, *[blk(r) for r in crows], *cbcs)
    n_out = len(out_blk)
    out_shape = tuple(jax.ShapeDtypeStruct((n_rows, o.shape[1]), o.dtype) for o in out_blk)
    in_specs = ([row_spec(r) for r in rows] + [bc_spec(b) for b in bcs]
                + [row_spec(r) for r in crows] + [bc_spec(b) for b in cbcs])

    def order(vals):
        return vals

    def fwd_call(args):
        def body(*refs):
            vals = [r[...] for r in refs[:n_in]]
            res = fn(*order(vals))
            for o_ref, r in zip(refs[n_in:], res):
                o_ref[...] = r

        return pl.pallas_call(
            body, name=name + "_f", grid=grid, in_specs=in_specs,
            out_specs=[row_spec(o) for o in out_shape], out_shape=out_shape,
            compiler_params=pltpu.CompilerParams(dimension_semantics=("parallel",), vmem_limit_bytes=VMEM_LIMIT),
        )(*args)

    def bwd_call(args, cts):
        def body(*refs):
            vals = [r[...] for r in refs[:n_in]]
            ct_refs = refs[n_in:n_in + n_out]
            d_rows = refs[n_in + n_out:n_in + n_out + nr]
            d_bcs = refs[n_in + n_out + nr:]
            consts = vals[nr + nb:]
            _, vjp = jax.vjp(lambda *p: fn(*p, *consts), *vals[:nr + nb])
            grads = vjp(tuple(c[...] for c in ct_refs))
            for ref, g in zip(d_rows, grads[:nr]):
                ref[...] = g

            @pl.when(pl.program_id(0) == 0)
            def _():
                for ref in d_bcs:
                    ref[...] = jnp.zeros_like(ref)

            for ref, g in zip(d_bcs, grads[nr:]):
                ref[...] += g

        d_shape = tuple(jax.ShapeDtypeStruct(r.shape, r.dtype) for r in rows + bcs)
        return pl.pallas_call(
            body, name=name + "_b", grid=grid,
            in_specs=in_specs + [row_spec(o) for o in out_shape],
            out_specs=[row_spec(r) for r in rows] + [bc_spec(b) for b in bcs], out_shape=d_shape,
            compiler_params=pltpu.CompilerParams(dimension_semantics=("arbitrary",), vmem_limit_bytes=VMEM_LIMIT),
        )(*args, *cts)

    @jax.custom_vjp
    def op(diff, const):
        return fwd_call(diff + const)

    def op_fwd(diff, const):
        return fwd_call(diff + const), (diff, const)

    def op_bwd(res, cts):
        diff, const = res
        grads = bwd_call(diff + const, tuple(cts))
        return tuple(grads), tuple(jnp.zeros_like(c) for c in const)

    op.defvjp(op_fwd, op_bwd)
    return op(rows + bcs, crows + cbcs)


def colop(fn, name, arrays, uses, n_const, nblk, cw=128):
    arrays = tuple(arrays)
    n_diff = len(arrays) - n_const
    nd = sum(1 for u in uses if u[0] < n_diff)
    assert all(u[0] < n_diff for u in uses[:nd]) and all(u[0] >= n_diff for u in uses[nd:])

    def spec(u):
        return pl.BlockSpec((arrays[u[0]].shape[0], cw), lambda j, off=u[1]: (0, off + j))

    def out_spec(rows):
        return pl.BlockSpec((rows, cw), lambda j: (0, j))

    out_blk = jax.eval_shape(fn, *[jax.ShapeDtypeStruct((arrays[u[0]].shape[0], cw), arrays[u[0]].dtype)
                                   for u in uses])
    out_shape = tuple(jax.ShapeDtypeStruct((o.shape[0], nblk * cw), o.dtype) for o in out_blk)
    params = pltpu.CompilerParams(dimension_semantics=("parallel",), vmem_limit_bytes=VMEM_LIMIT)

    def fwd_call(arrs):
        def body(*refs):
            res = fn(*[r[...] for r in refs[:len(uses)]])
            for o_ref, r in zip(refs[len(uses):], res):
                o_ref[...] = r

        return pl.pallas_call(
            body, name=name + "_f", grid=(nblk,), in_specs=[spec(u) for u in uses],
            out_specs=[out_spec(o.shape[0]) for o in out_shape], out_shape=out_shape, compiler_params=params,
        )(*[arrs[u[0]] for u in uses])

    def bwd_call(arrs, cts):
        def body(*refs):
            vals = [r[...] for r in refs[:len(uses)]]
            ct_refs = refs[len(uses):len(uses) + len(out_shape)]
            _, vjp = jax.vjp(lambda *p: fn(*p, *vals[nd:]), *vals[:nd])
            for ref, g in zip(refs[len(uses) + len(out_shape):], vjp(tuple(c[...] for c in ct_refs))):
                ref[...] = g

        d_shape = tuple(jax.ShapeDtypeStruct((arrays[u[0]].shape[0], nblk * cw), F32) for u in uses[:nd])
        return pl.pallas_call(
            body, name=name + "_b", grid=(nblk,),
            in_specs=[spec(u) for u in uses] + [out_spec(o.shape[0]) for o in out_shape],
            out_specs=[out_spec(s.shape[0]) for s in d_shape], out_shape=d_shape, compiler_params=params,
        )(*[arrs[u[0]] for u in uses], *cts)

    @jax.custom_vjp
    def op(diff, const):
        return fwd_call(diff + const)

    def op_fwd(diff, const):
        return fwd_call(diff + const), (diff, const)

    def op_bwd(res, cts):
        diff, const = res
        d_uses = bwd_call(diff + const, tuple(cts))
        grads = []
        for i in range(n_diff):
            parts = sorted([(u[1], k) for k, u in enumerate(uses[:nd]) if u[0] == i])
            grads.append(d_uses[parts[0][1]] if len(parts) == 1
                         else jnp.concatenate([d_uses[k] for _, k in parts], axis=1))
        return tuple(grads), tuple(jnp.zeros_like(c) for c in const)

    op.defvjp(op_fwd, op_bwd)
    return op(arrays[:n_diff], arrays[n_diff:])


@functools.partial(jax.custom_vjp, nondiff_argnums=(1,))
def _roll_rows(x, k):
    return pltpu.roll(x, k % x.shape[0], 0)


def _roll_rows_fwd(x, k):
    return _roll_rows(x, k), None


def _roll_rows_bwd(k, _, g):
    return (_roll_rows(g, -k),)


_roll_rows.defvjp(_roll_rows_fwd, _roll_rows_bwd)


def _conv3(x, w0, w1, w2, starts):
    rows = lax.broadcasted_iota(jnp.int32, x.shape, 0)
    ends = tuple(s - 1 for s in starts[1:]) + (x.shape[0] - 1,)
    first = functools.reduce(jnp.logical_or, [rows == s for s in starts])
    last = functools.reduce(jnp.logical_or, [rows == e for e in ends])
    prev = jnp.where(first, 0.0, _roll_rows(x, 1))
    nxt = jnp.where(last, 0.0, _roll_rows(x, -1))
    return prev * w0 + x * w1 + nxt * w2


def _rms(x):
    return x * lax.rsqrt(jnp.mean(x * x, axis=-1, keepdims=True) + NORM_EPS)


def _heads(x, n):
    return [x[:, h * HEAD_DIM:(h + 1) * HEAD_DIM] for h in range(n)]


_NT = (((1,), (1,)), ((), ()))
_TN = (((0,), (0,)), ((), ()))
_TQ = 256


_N_SUB = 2


def _sub_rows(ref, i):
    rows = ref.shape[0] // _N_SUB
    return ref[i * rows:(i + 1) * rows, :].astype(BF16)


def _attn_probs(qs, k):
    s = _each(lambda q: lax.dot_general(q, k, _NT, preferred_element_type=F32) * (HEAD_DIM ** -0.5), qs)
    m = _each(lambda a: jnp.max(a, axis=-1, keepdims=True), s)
    e = _each(lambda a, b: jnp.exp(a - b), s, m)
    inv = _each(lambda a: 1.0 / jnp.sum(a, axis=-1, keepdims=True), e)
    return _each(lambda a, b: a * b, e, inv)


def _attn_fwd_call(q, k, v):
    n, t = q.shape[0], k.shape[0]
    tq = _pick(n, (_TQ, 128))

    def body(q_ref, k_ref, v_ref, o_ref):
        vb = v_ref[...].astype(BF16)
        ps = _attn_probs([_sub_rows(q_ref, i) for i in range(_N_SUB)], k_ref[...].astype(BF16))
        rows = tq // _N_SUB
        for i, p in enumerate(ps):
            o_ref[i * rows:(i + 1) * rows, :] = jnp.dot(p.astype(BF16), vb, preferred_element_type=F32)

    return pl.pallas_call(
        body, name="attn_f", grid=(ATTN_HEADS, n // tq),
        in_specs=[pl.BlockSpec((tq, HEAD_DIM), lambda h, i: (i, h)),
                  pl.BlockSpec((t, HEAD_DIM), lambda h, i: (0, h // ATTN_GROUP)),
                  pl.BlockSpec((t, HEAD_DIM), lambda h, i: (0, h // ATTN_GROUP))],
        out_specs=pl.BlockSpec((tq, HEAD_DIM), lambda h, i: (i, h)),
        out_shape=jax.ShapeDtypeStruct(q.shape, F32),
        compiler_params=pltpu.CompilerParams(dimension_semantics=("parallel", "parallel"),
                                             vmem_limit_bytes=VMEM_LIMIT),
    )(q, k, v)


def _attn_bwd_call(q, k, v, do):
    n, t = q.shape[0], k.shape[0]
    tq = _pick(n, (_TQ, 128))

    def body(q_ref, k_ref, v_ref, do_ref, dq_ref, dk_ref, dv_ref):
        @pl.when((pl.program_id(1) == 0) & (pl.program_id(2) == 0))
        def _():
            dk_ref[...] = jnp.zeros_like(dk_ref)
            dv_ref[...] = jnp.zeros_like(dv_ref)

        kb, vb = k_ref[...].astype(BF16), v_ref[...].astype(BF16)
        qs = [_sub_rows(q_ref, i) for i in range(_N_SUB)]
        dos = [_sub_rows(do_ref, i) for i in range(_N_SUB)]
        ps = _attn_probs(qs, kb)
        dps = _each(lambda d: lax.dot_general(d, vb, _NT, preferred_element_type=F32), dos)
        dss = _each(lambda p, dp: (p * (dp - jnp.sum(p * dp, axis=-1, keepdims=True)) * (HEAD_DIM ** -0.5)).astype(BF16),
                    ps, dps)
        rows = tq // _N_SUB
        for i, ds in enumerate(dss):
            dq_ref[i * rows:(i + 1) * rows, :] = jnp.dot(ds, kb, preferred_element_type=F32)
        dk_ref[...] += sum(_each(lambda ds, q: lax.dot_general(ds, q, _TN, preferred_element_type=F32), dss, qs))
        dv_ref[...] += sum(_each(lambda p, d: lax.dot_general(p.astype(BF16), d, _TN, preferred_element_type=F32),
                                 ps, dos))

    q_spec = pl.BlockSpec((tq, HEAD_DIM), lambda kh, g, i: (i, kh * ATTN_GROUP + g))
    kv_spec = pl.BlockSpec((t, HEAD_DIM), lambda kh, g, i: (0, kh))
    return pl.pallas_call(
        body, name="attn_b", grid=(ATTN_KV_HEADS, ATTN_GROUP, n // tq),
        in_specs=[q_spec, kv_spec, kv_spec, q_spec],
        out_specs=[q_spec, kv_spec, kv_spec],
        out_shape=(jax.ShapeDtypeStruct(q.shape, F32), jax.ShapeDtypeStruct(k.shape, F32),
                   jax.ShapeDtypeStruct(v.shape, F32)),
        compiler_params=pltpu.CompilerParams(dimension_semantics=("parallel", "arbitrary", "arbitrary"),
                                             vmem_limit_bytes=VMEM_LIMIT),
    )(q, k, v, do)


@jax.custom_vjp
def attention(q, k, v):
    return _attn_fwd_call(q, k, v)


def _attention_fwd(q, k, v):
    return _attn_fwd_call(q, k, v), (q, k, v)


def _attention_bwd(res, do):
    return _attn_bwd_call(*res, do)


attention.defvjp(_attention_fwd, _attention_bwd)


_C = GDN_CHUNK


def _pdot(a, b):
    return jnp.dot(a, b, precision=lax.Precision.HIGH, preferred_element_type=F32)


@jax.custom_vjp
def _hdot(a, b):
    return jnp.dot(a.astype(BF16), b.astype(BF16), preferred_element_type=F32)


def _hdot_fwd(a, b):
    return _hdot(a, b), (a, b)


def _hdot_bwd(res, g):
    a, b = res
    gb = g.astype(BF16)
    return (lax.dot_general(gb, b.astype(BF16), _NT, preferred_element_type=F32),
            lax.dot_general(a.astype(BF16), gb, _TN, preferred_element_type=F32))


_hdot.defvjp(_hdot_fwd, _hdot_bwd)


def _each(fn, *lists):
    return [fn(*args) for args in zip(*lists)]


def _unit_lower_inverse(low, blockdiag):
    eye = (lax.broadcasted_iota(jnp.int32, (_C, _C), 0) == lax.broadcasted_iota(jnp.int32, (_C, _C), 1)).astype(F32)
    ld = _each(lambda a: a * blockdiag, low)
    lo = _each(lambda a, d: a - d, low, ld)
    l2 = _each(_hdot, ld, ld)
    l4 = _each(_hdot, l2, l2)
    l8 = _each(_hdot, l4, l4)
    td = _each(lambda d, a2: _hdot(eye - d, eye + a2), ld, l2)
    td = _each(lambda t, a4: _hdot(t, eye + a4), td, l4)
    td = _each(lambda t, a8: _hdot(t, eye + a8), td, l8)
    nn = _each(_hdot, td, lo)
    n2 = _each(_hdot, nn, nn)
    out = _each(lambda n, m2: _hdot(eye - n, eye + m2), nn, n2)
    return _each(_hdot, out, td)


def _gdn_chunks(heads, blockdiag):
    q, k, v, b_b, be_b, e_b, kd_b, m1, dec, gl, s = (list(col) for col in zip(*heads))
    f32dot = lambda a, b: jnp.dot(a, b, preferred_element_type=F32)
    nt = lambda a, b: lax.dot_general(a, b, _NT, preferred_element_type=F32)
    kq_k = _each(lambda kx, qq: nt(jnp.concatenate([kx, qq], axis=0), kx), k, q)
    t_inv = _unit_lower_inverse(_each(lambda m, a: m * a[:_C], m1, kq_k), blockdiag)
    uw = _each(lambda t, b, x, be, kx: _hdot(t, jnp.concatenate([b * x, be * kx], axis=1)), t_inv, b_b, v, be_b, k)
    wq_s = _each(lambda a, qq, e, ss: f32dot(jnp.concatenate([a[:, HEAD_DIM:], qq * e], axis=0), ss), uw, q, e_b, s)
    delta = _each(lambda a, ws: a[:, :HEAD_DIM] - ws[:_C], uw, wq_s)
    p = _each(lambda d, a: d * a[_C:], dec, kq_k)
    o = _each(lambda ws, pp, dd: ws[_C:] + f32dot(pp, dd), wq_s, p, delta)
    s_new = _each(lambda g, ss, kx, kd, dd: g * ss + lax.dot_general(kx * kd, dd, _TN, preferred_element_type=F32),
                  gl, s, k, kd_b, delta)
    return o, s_new


def _blockdiag_mask():
    r = lax.broadcasted_iota(jnp.int32, (_C, _C), 0) >> 4
    c = lax.broadcasted_iota(jnp.int32, (_C, _C), 1) >> 4
    return (r == c).astype(F32)


_N_DIR = 2


def _scan_chunk(s, nc, ncc, reverse):
    return jnp.where(s < ncc, ncc - 1 - s, nc + ncc - 1 - s) if reverse else s


def _gdn_specs(nc, ncc, backward):
    step = (lambda s: nc - 1 - s) if backward else (lambda s: s)
    chunk = [lambda s, d=d: _scan_chunk(step(s), nc, ncc, d == 1) for d in range(_N_DIR)]
    tok = [pl.BlockSpec((_C, 3 * GDN_W), lambda s, d=d: (chunk[d](s), 0)) for d in range(_N_DIR)]
    park = [0, nc - ncc - 1]
    out = [pl.BlockSpec((_C, GDN_W), lambda s, d=d: (jnp.where(chunk[d](s) >= ncc, chunk[d](s) - ncc, park[d]), 0))
           for d in range(_N_DIR)]
    per_tok = pl.BlockSpec((_N_DIR, GDN_HEADS, _C, HEAD_DIM), lambda s: (0, 0, step(s), 0))
    mat = pl.BlockSpec((_N_DIR, GDN_HEADS, None, _C, _C), lambda s: (0, 0, step(s), 0, 0))
    row = pl.BlockSpec((_N_DIR, GDN_HEADS, None, 1, HEAD_DIM), lambda s: (0, 0, step(s), 0, 0))
    state = pl.BlockSpec((_N_DIR, GDN_HEADS, None, HEAD_DIM, HEAD_DIM), lambda s: (0, 0, step(s), 0, 0))
    return tok, out, per_tok, mat, row, state, chunk


def _head_cols(h, part):
    return slice((part * GDN_HEADS + h) * HEAD_DIM, (part * GDN_HEADS + h + 1) * HEAD_DIM)


def _gdn_heads(qkv_refs, factor_refs, state_of):
    return [[qkv_refs[d][:, _head_cols(h, 0)], qkv_refs[d][:, _head_cols(h, 1)], qkv_refs[d][:, _head_cols(h, 2)]]
            + [r[d, h] for r in factor_refs] + [state_of(d, h)]
            for d in range(_N_DIR) for h in range(GDN_HEADS)]


def _gdn_fwd_call(ncc, qkv, factors):
    t = qkv.shape[0]
    nc = t // _C
    tok, out, per_tok, mat, row, state, _ = _gdn_specs(nc, ncc, False)

    def body(*refs):
        qkv_refs, f_refs = refs[:_N_DIR], refs[_N_DIR:_N_DIR + 7]
        o_refs, sall_ref, s_ref = refs[_N_DIR + 7:2 * _N_DIR + 7], refs[2 * _N_DIR + 7], refs[-1]

        @pl.when(pl.program_id(0) == 0)
        def _():
            s_ref[...] = jnp.zeros_like(s_ref)

        heads = _gdn_heads(qkv_refs, f_refs, lambda d, h: s_ref[d, h])
        o, s_new = _gdn_chunks(heads, _blockdiag_mask())
        for d in range(_N_DIR):
            for h in range(GDN_HEADS):
                i = GDN_HEADS * d + h
                sall_ref[d, h] = heads[i][10]
                o_refs[d][:, _head_cols(h, 0)] = o[i]
                s_ref[d, h] = s_new[i]

    o_shape = jax.ShapeDtypeStruct((t - ncc * _C, GDN_W), F32)
    s_shape = (_N_DIR, GDN_HEADS, nc, HEAD_DIM, HEAD_DIM)
    return pl.pallas_call(
        body, name="gdn_f", grid=(nc,),
        in_specs=[*tok, per_tok, per_tok, per_tok, per_tok, mat, mat, row],
        out_specs=[*out, state], out_shape=[o_shape, o_shape, jax.ShapeDtypeStruct(s_shape, F32)],
        scratch_shapes=[pltpu.VMEM((_N_DIR, GDN_HEADS, HEAD_DIM, HEAD_DIM), F32)],
        compiler_params=pltpu.CompilerParams(dimension_semantics=("arbitrary",), vmem_limit_bytes=VMEM_LIMIT),
    )(qkv, qkv, *factors)


def _gdn_bwd_call(ncc, qkv, factors, sall, dos):
    t = qkv.shape[0]
    nc = t // _C
    tok, out, per_tok, mat, row, state, chunk = _gdn_specs(nc, ncc, True)

    def body(*refs):
        qkv_refs, f_refs, sall_ref = refs[:_N_DIR], refs[_N_DIR:_N_DIR + 7], refs[_N_DIR + 7]
        do_refs = refs[_N_DIR + 8:2 * _N_DIR + 8]
        dqkv_refs = refs[2 * _N_DIR + 8:3 * _N_DIR + 8]
        df_refs, ds_ref = refs[3 * _N_DIR + 8:3 * _N_DIR + 15], refs[-1]

        @pl.when(pl.program_id(0) == 0)
        def _():
            ds_ref[...] = jnp.zeros_like(ds_ref)

        bd = _blockdiag_mask()
        heads = _gdn_heads(qkv_refs, f_refs, lambda d, h: sall_ref[d, h])
        _, vjp = jax.vjp(lambda hs: _gdn_chunks(hs, bd), heads)
        live = [chunk[d](pl.program_id(0)) >= ncc for d in range(_N_DIR)]
        (all_grads,) = vjp(([jnp.where(live[d], do_refs[d][:, _head_cols(h, 0)], 0.0)
                             for d in range(_N_DIR) for h in range(GDN_HEADS)],
                            [ds_ref[d, h] for d in range(_N_DIR) for h in range(GDN_HEADS)]))
        for d in range(_N_DIR):
            for h in range(GDN_HEADS):
                grads = all_grads[GDN_HEADS * d + h]
                for part in range(3):
                    dqkv_refs[d][:, _head_cols(h, part)] = grads[part]
                for ref, g in zip(df_refs, grads[3:10]):
                    ref[d, h] = g
                ds_ref[d, h] = grads[10]

    shp = lambda a: jax.ShapeDtypeStruct(a.shape, F32)
    res = pl.pallas_call(
        body, name="gdn_b", grid=(nc,),
        in_specs=[*tok, per_tok, per_tok, per_tok, per_tok, mat, mat, row, state, *out],
        out_specs=[*tok, per_tok, per_tok, per_tok, per_tok, mat, mat, row],
        out_shape=[shp(qkv), shp(qkv)] + [shp(a) for a in factors],
        scratch_shapes=[pltpu.VMEM((_N_DIR, GDN_HEADS, HEAD_DIM, HEAD_DIM), F32)],
        compiler_params=pltpu.CompilerParams(dimension_semantics=("arbitrary",), vmem_limit_bytes=VMEM_LIMIT),
    )(qkv, qkv, *factors, sall, *dos)
    return res[0] + res[1], tuple(res[_N_DIR:])


@functools.partial(jax.custom_vjp, nondiff_argnums=(0,))
def gdn_scan(ncc, qkv, factors):
    o0, o1, _ = _gdn_fwd_call(ncc, qkv, factors)
    return o0, o1


def _gdn_scan_fwd(ncc, qkv, factors):
    o0, o1, sall = _gdn_fwd_call(ncc, qkv, factors)
    return (o0, o1), (qkv, factors, sall)


def _gdn_scan_bwd(ncc, res, dos):
    qkv, factors, sall = res
    return _gdn_bwd_call(ncc, qkv, factors, sall, list(dos))


gdn_scan.defvjp(_gdn_scan_fwd, _gdn_scan_bwd)


def _rope_tables(n, cl):
    t = np.arange(n)
    inv_freq = (ROPE_THETA ** (-np.arange(0, HEAD_DIM // 2, 2, dtype=np.float32) / (HEAD_DIM // 2))).astype(np.float32)
    ang_r = (t // GRID_W).astype(np.float32)[:, None] * inv_freq
    ang_c = (t % GRID_W).astype(np.float32)[:, None] * inv_freq
    cos = np.concatenate([np.cos(ang_r), np.cos(ang_r), np.cos(ang_c), np.cos(ang_c)], axis=1)
    sin = np.concatenate([-np.sin(ang_r), np.sin(ang_r), -np.sin(ang_c), np.sin(ang_c)], axis=1)
    cos_all = np.concatenate([np.ones((cl, HEAD_DIM), np.float32), cos], axis=0)
    sin_all = np.concatenate([np.zeros((cl, HEAD_DIM), np.float32), sin], axis=0)
    j = np.arange(HEAD_DIM)
    src = np.where((j % 64) < 32, j + 32, j - 32)
    perm = np.zeros((HEAD_DIM, HEAD_DIM), np.float32)
    perm[src, j] = 1.0
    return (jnp.asarray(cos.astype(np.float32)), jnp.asarray(sin.astype(np.float32)),
            jnp.asarray(cos_all), jnp.asarray(sin_all), jnp.asarray(perm))


def _gdn_factors(log_a, beta, ncc):
    t = log_a.shape[0]
    nc = t // _C
    la = log_a.reshape(nc, _C, _N_DIR, GDN_HEADS).transpose(2, 3, 0, 1)
    be = beta.reshape(nc, _C, _N_DIR, GDN_HEADS).transpose(2, 3, 0, 1)
    scan_order = lambda a: jnp.stack([a[0], jnp.concatenate([jnp.flip(a[1][:, :ncc], axis=1),
                                                              jnp.flip(a[1][:, ncc:], axis=1)], axis=1)])
    la, be = scan_order(la), scan_order(be)
    rev = jnp.asarray(np.array([False, True])[:, None, None, None])
    run = jnp.cumsum(la, axis=3)
    gam = jnp.where(rev, jnp.sum(la, axis=3, keepdims=True) - run + la, run)
    idx = np.arange(_C)
    incl = jnp.asarray(np.stack([idx[:, None] >= idx[None, :], idx[:, None] <= idx[None, :]])[:, None, None])
    strict = jnp.asarray(np.stack([idx[:, None] > idx[None, :], idx[:, None] < idx[None, :]])[:, None, None])
    dec = jnp.exp(jnp.where(incl, gam[..., :, None] - gam[..., None, :], -jnp.inf))
    m1 = jnp.where(strict, be[..., :, None] * dec, 0.0)
    e = jnp.exp(gam)
    g_last = jnp.where(rev, gam[..., :1], gam[..., -1:])
    lanes = lambda a: jnp.broadcast_to(a.reshape(_N_DIR, GDN_HEADS, t, 1), (_N_DIR, GDN_HEADS, t, HEAD_DIM))
    gl = jnp.broadcast_to(jnp.exp(g_last)[..., None], (_N_DIR, GDN_HEADS, nc, 1, HEAD_DIM))
    return lanes(be), lanes(be * e), lanes(e), lanes(jnp.exp(g_last - gam)), m1, dec, gl


def local_loss(x, wz, wb, ws, c, ctx, target):
    return channel_mixing(token_mixing(x, wz, wb, ws, c, ctx), wz, wb, ws, target)


def token_mixing(x, wz, wb, ws, c, ctx):
    n, cl = x.shape[0], ctx.shape[0]
    cos_q, sin_q, cos_k, sin_k, perm = _rope_tables(n, cl)

    sc_in = jnp.concatenate([jax.nn.silu(c), jax.nn.silu(ws["c_ctx"])[None, :], jnp.zeros((14, D_MODEL), F32)], axis=0)
    mod = pmm(sc_in, wb["w_mod"], wz["w_mod"], "mm_mod") + ws["b_mod"]
    sh1, sc1, g1, sh2, sc2, g2 = [mod[0:1, i * D_MODEL:(i + 1) * D_MODEL] for i in range(6)]
    csh1, csc1 = mod[1:2, 0:D_MODEL], mod[1:2, D_MODEL:2 * D_MODEL]

    def norm_mod(a, sh, sc):
        return (_rms(a) * (1.0 + sc) + sh,)

    (hx,) = rowop(norm_mod, "normmod_x", (x,), (sh1, sc1))
    (hc,) = rowop(norm_mod, "normmod_c", (ctx,), (csh1, csc1))
    h_all = jnp.concatenate([hc, hx], axis=0)
    p_main = pmm_t(h_all, wb["w_in_main"], wz["w_in_main"], "mm_in")
    p_small = pmm_t(h_all, wb["w_in_small"], wz["w_in_small"], "mm_ins")
    ak, av, qkv, aq, z, gate = jnp.split(p_main, [KV_W, 2 * KV_W, SMALL_AT, SMALL_AT + Q_W, SMALL_AT + Q_W + GDN_W],
                                         axis=1)
    db, da = p_small[:, :2 * GDN_HEADS], p_small[:, 2 * GDN_HEADS:4 * GDN_HEADS]

    def qk_prep(nh):
        def fn(a, w, cos, sin, pm):
            outs = []
            for ah in _heads(a, nh):
                y = _rms(ah) * w
                outs.append(y * cos + _pdot(y, pm) * sin)
            return (jnp.concatenate(outs, axis=1),)
        return fn

    (q_x,) = rowop(qk_prep(ATTN_HEADS), "q_prep", (aq[cl:],), (ws["q_norm_w"],), (cos_q, sin_q), (perm,))
    (k_all,) = rowop(qk_prep(ATTN_KV_HEADS), "k_prep", (ak,), (ws["k_norm_w"],), (cos_k, sin_k), (perm,))
    attn_x = attention(q_x, k_all, av)

    cw = ws["conv_qkv_w"]
    normed = jnp.asarray(np.repeat([1.0, 1.0, 0.0], GDN_W)[None, :], F32)
    scale = jnp.asarray(np.repeat([HEAD_DIM ** -0.5, 1.0, 1.0], GDN_W)[None, :], F32)

    def gdn_prep(a, w0, w1, w2, nf, sc):
        s = jax.nn.silu(_conv3(a, w0, w1, w2, (0, cl)))
        inv = lax.rsqrt(jnp.sum(s * s, axis=-1, keepdims=True) + NORM_EPS)
        return (s * jnp.where(nf > 0.0, inv * sc, 1.0),)

    (qkvn,) = colop(gdn_prep, "gdn_prep", (qkv, cw[0:1], cw[1:2], cw[2:3], normed, scale),
                    [(i, 0) for i in range(6)], 2, 3 * GDN_HEADS)
    beta = jax.nn.sigmoid(db).reshape(-1, 2, GDN_HEADS)
    log_a = -jnp.exp(ws["a_log"])[None] * jax.nn.softplus(da.reshape(-1, 2, GDN_HEADS) + ws["dt_bias"][None])
    o_fwd, o_rev = gdn_scan(cl // _C, qkvn, _gdn_factors(log_a, beta, cl // _C))
    o_x = o_fwd + o_rev

    def gdn_out(o, zz, w):
        outs = [_rms(oh) * w * jax.nn.silu(zh) for oh, zh in zip(_heads(o, GDN_HEADS), _heads(zz, GDN_HEADS))]
        return (jnp.concatenate(outs, axis=1),)

    (gdn_x,) = rowop(gdn_out, "gdn_out", (o_x, z[cl:]), (ws["gdn_norm_w"],))
    return dict(x=x, attn_x=attn_x, gdn_x=gdn_x, gate=gate[cl:], g1=g1, sh2=sh2, sc2=sc2, g2=g2)


def channel_mixing(mixed, wz, wb, ws, target):
    x, attn_x, gdn_x, gate = mixed["x"], mixed["attn_x"], mixed["gdn_x"], mixed["gate"]
    g1, sh2, sc2, g2 = mixed["g1"], mixed["sh2"], mixed["sc2"], mixed["g2"]
    pa = pmm(attn_x, wb["w_pa"], wz["w_pa"], "mm_pa")
    pd = pmm(gdn_x, wb["w_pd"], wz["w_pd"], "mm_pd")

    def merge(a, d, g):
        return (jax.nn.sigmoid(g[:, :D_MODEL]) * a + jax.nn.sigmoid(g[:, D_MODEL:]) * d,)

    (y,) = rowop(merge, "merge", (pa, pd, gate))
    mo = pmm(y, wb["w_out"], wz["w_out"], "mm_out")

    def res_norm_mod(xx, m, g, sh, sc):
        x1 = xx + g * m
        return x1, _rms(x1) * (1.0 + sc) + sh

    x1, h2 = rowop(res_norm_mod, "res1", (x, mo), (g1, sh2, sc2))
    up = pmm(h2, wb["w_up"], wz["w_up"], "mm_up")
    fw = ws["ffn_conv_w"]

    def ffn_act(ug, uv, w0g, w0v, w1g, w1v, w2g, w2v, bg, bv):
        g = _conv3(ug, w0g, w1g, w2g, (0,)) + bg
        v = _conv3(uv, w0v, w1v, w2v, (0,)) + bv
        return (jax.nn.silu(g) * v,)

    half = D_FF // HEAD_DIM
    (act,) = colop(ffn_act, "ffn_act", (up, fw[0:1], fw[1:2], fw[2:3], ws["ffn_conv_b"]),
                   [(i, off) for i in range(5) for off in (0, half)], 0, half)
    dn = pmm(act, wb["w_down"], wz["w_down"], "mm_down")

    def head(xx, m, g, w, tgt):
        yy = _rms(xx + g * m) * w
        err = (yy - tgt) ** 2
        return (jnp.broadcast_to(0.5 * jnp.mean(err, axis=-1, keepdims=True), (xx.shape[0], HEAD_DIM)),)

    (row_loss,) = rowop(head, "head", (x1, dn), (g2, ws["final_norm_w"][None, :]), (target,))
    return jnp.sum(row_loss[:, 0])


_HBM = pl.BlockSpec(memory_space=pltpu.HBM)


def _chip_peers():
    x, y = lax.axis_index("x"), lax.axis_index("y")
    return [(1 - x, y), (x, 1 - y), (1 - x, 1 - y)]


_SPLIT_COLS = ("w_in",)


def _half_of(view, nm, idx, lead=0):
    r, cdim = view.shape[-2:]
    pre = (slice(None),) * lead
    if nm in _SPLIT_COLS:
        return view.at[pre + (slice(None), pl.ds(pl.multiple_of(idx * (cdim // 2), 128), cdim // 2))]
    return view.at[pre + (pl.ds(pl.multiple_of(idx * (r // 2), 16), r // 2), slice(None))]


def _remote(src, dst, send_sem, recv_sem, dev):
    return pltpu.make_async_remote_copy(src_ref=src, dst_ref=dst, send_sem=send_sem, recv_sem=recv_sem,
                                        device_id=dev, device_id_type=MESH)


def _hbm_call(body, name, ins, out_shape, n_sems, in_place=False):
    names = tuple(ins)
    return dict(zip(names, pl.pallas_call(
        body, name=name, in_specs=[_HBM] * len(names), out_specs=[_HBM] * len(names),
        out_shape=[out_shape(nm, ins[nm]) for nm in names],
        scratch_shapes=[pltpu.SemaphoreType.DMA((k,)) for k in n_sems],
        input_output_aliases={i: i for i in range(len(names))} if in_place else {},
    )(*[ins[nm] for nm in names])))


def all_gather_chips(shards):
    names = tuple(shards)
    n = len(names)

    def body(*refs):
        ins, outs = dict(zip(names, refs[:n])), dict(zip(names, refs[n:2 * n]))
        ici_send, ici_recv, d2d_send, d2d_recv, own_send, own_recv = refs[2 * n:]
        x, y, c = lax.axis_index("x"), lax.axis_index("y"), lax.axis_index("c")
        me, sib = 2 * x + y, (x, y, 1 - c)
        own = [_remote(ins[nm], outs[nm].at[me], own_send.at[i], own_recv.at[i], sib) for i, nm in enumerate(names)]
        for cp in own:
            cp.start()
        sends = []
        for k, (px, py) in enumerate(_chip_peers()):
            for i, nm in enumerate(names):
                cp = _remote(_half_of(ins[nm], nm, c), _half_of(outs[nm].at[me], nm, c), ici_send.at[k * n + i],
                             ici_recv.at[k * n + i], (px, py, c))
                cp.start()
                sends.append(cp)
        for k, (px, py) in enumerate(_chip_peers()):
            for i, nm in enumerate(names):
                landed = _half_of(outs[nm].at[2 * px + py], nm, c)
                _remote(landed, landed, ici_send.at[k * n + i], ici_recv.at[k * n + i], (px, py, c)).wait_recv()
                fw = _remote(landed, landed, d2d_send.at[k * n + i], d2d_recv.at[k * n + i], sib)
                fw.start()
                sends.append(fw)
        for k, (px, py) in enumerate(_chip_peers()):
            for i, nm in enumerate(names):
                other = _half_of(outs[nm].at[2 * px + py], nm, 1 - c)
                _remote(other, other, d2d_send.at[k * n + i], d2d_recv.at[k * n + i], sib).wait_recv()
        for cp in sends:
            cp.wait_send()
        for cp in own:
            cp.wait()

    return _hbm_call(body, "ag_weights", shards, lambda nm, a: jax.ShapeDtypeStruct((N_CHIPS,) + a.shape, a.dtype),
                     (3 * n, 3 * n, 3 * n, 3 * n, n, n))


_SEM = pl.BlockSpec(memory_space=pltpu.SEMAPHORE)


def push_start(name, arrays, land_shapes, copies, n_copies):
    names = tuple(arrays)
    n = len(names)

    def body(*refs):
        send_sems, recv_sems, token = refs[2 * n], refs[2 * n + 1], refs[-1]
        for j, (src, dst, dev) in enumerate(copies(refs[:n], refs[n:2 * n])):
            _remote(src, dst, send_sems.at[j], recv_sems.at[j], dev).start()
        token[...] = jnp.zeros_like(token)

    hbm = lambda a: pltpu.with_memory_space_constraint(a, pltpu.HBM)
    lands = [lax.empty(land_shapes[nm], arrays[nm].dtype) for nm in names]
    res = pl.pallas_call(
        body, name=name,
        out_shape=(pltpu.SemaphoreType.DMA((n_copies,)), pltpu.SemaphoreType.DMA((n_copies,)),
                   *[pltpu.HBM(arrays[nm].shape, arrays[nm].dtype) for nm in names],
                   *[pltpu.HBM(a.shape, a.dtype) for a in lands], jax.ShapeDtypeStruct((8, 128), F32)),
        in_specs=[_HBM] * (2 * n),
        out_specs=(_SEM, _SEM, *[_HBM] * (2 * n), pl.BlockSpec(memory_space=pltpu.VMEM)),
        input_output_aliases={i: 2 + i for i in range(2 * n)},
        compiler_params=pltpu.CompilerParams(has_side_effects=pltpu.SideEffectType.DATAFLOW_SIDE_EFFECTING),
    )(*[hbm(arrays[nm]) for nm in names], *[hbm(a) for a in lands])
    return names, res[0], res[1], res[2:2 + n], res[2 + n:2 + 2 * n], res[-1]


def push_wait(name, started, copies, after):
    names, send_sems, recv_sems, srcs, lands, _ = started
    n = len(names)

    def body(*refs):
        send_ref, recv_ref = refs[2 * n], refs[2 * n + 1]
        for j, (src, dst, dev) in enumerate(copies(refs[:n], refs[n:2 * n])):
            cp = _remote(src, dst, send_ref.at[j], recv_ref.at[j], dev)
            cp.wait_send()
            cp.wait_recv()

    res = pl.pallas_call(
        body, name=name,
        out_shape=(*[pltpu.HBM(a.shape, a.dtype) for a in srcs], *[pltpu.HBM(a.shape, a.dtype) for a in lands]),
        in_specs=[_HBM] * (2 * n) + [_SEM, _SEM, pl.BlockSpec(memory_space=pl.ANY)],
        out_specs=tuple([_HBM] * (2 * n)),
        input_output_aliases={i: i for i in range(2 * n)},
        compiler_params=pltpu.CompilerParams(has_side_effects=pltpu.SideEffectType.DATAFLOW_SIDE_EFFECTING),
    )(*srcs, *lands, send_sems, recv_sems, after)
    return dict(zip(names, res[n:]))


def _gather_copies(srcs, lands):
    x, y, c = lax.axis_index("x"), lax.axis_index("y"), lax.axis_index("c")
    devs = [(px, py, c) for px, py in _chip_peers()] + [(x, y, 1 - c)]
    return [(src, land.at[2 * x + y], dev) for src, land in zip(srcs, lands) for dev in devs]


def _scatter_copies(srcs, lands):
    c = lax.axis_index("c")
    return [(src.at[2 * px + py], land.at[k], (px, py, c))
            for src, land in zip(srcs, lands) for k, (px, py) in enumerate(_chip_peers())]


def sibling_halves(blocks, name):
    names = tuple(blocks)

    def body(*refs):
        n = len(names)
        ins, outs = dict(zip(names, refs[:n])), dict(zip(names, refs[n:2 * n]))
        send_sems, recv_sems = refs[2 * n:]
        x, y, c = lax.axis_index("x"), lax.axis_index("y"), lax.axis_index("c")
        cps = [_remote(_half_of(ins[nm], nm, 1 - c, lead=1), outs[nm], send_sems.at[i], recv_sems.at[i], (x, y, 1 - c))
               for i, nm in enumerate(names)]
        for cp in cps:
            cp.start()
        for cp in cps:
            cp.wait()

    def half_shape(nm, a):
        r, cdim = a.shape[-2:]
        return jax.ShapeDtypeStruct((N_CHIPS, r, cdim // 2) if nm in _SPLIT_COLS else (N_CHIPS, r // 2, cdim), a.dtype)

    return _hbm_call(body, name, blocks, half_shape, (len(names), len(names)))


def scatter_halves(blocks):
    names = tuple(blocks)
    n = len(names)

    def body(*refs):
        ins, outs = dict(zip(names, refs[:n])), dict(zip(names, refs[n:2 * n]))
        send_sems, recv_sems = refs[2 * n:]
        c = lax.axis_index("c")
        cps = [_remote(ins[nm].at[2 * px + py], outs[nm].at[k], send_sems.at[k * n + i], recv_sems.at[k * n + i],
                       (px, py, c))
               for k, (px, py) in enumerate(_chip_peers()) for i, nm in enumerate(names)]
        for cp in cps:
            cp.start()
        for cp in cps:
            cp.wait_recv()
        for cp in cps:
            cp.wait_send()

    return _hbm_call(body, "rs_grads", blocks, lambda nm, a: jax.ShapeDtypeStruct((3,) + a.shape[1:], a.dtype),
                     (3 * n, 3 * n))


def sibling_assemble(arrays, name):
    names = tuple(arrays)

    def body(*refs):
        n = len(names)
        ins, outs = dict(zip(names, refs[:n])), dict(zip(names, refs[n:2 * n]))
        send_sems, recv_sems = refs[2 * n:]
        x, y, c = lax.axis_index("x"), lax.axis_index("y"), lax.axis_index("c")
        cps = [_remote(_half_of(ins[nm], nm, c), _half_of(outs[nm], nm, c), send_sems.at[i], recv_sems.at[i],
                       (x, y, 1 - c)) for i, nm in enumerate(names)]
        for cp in cps:
            cp.start()
        for i, nm in enumerate(names):
            other = _half_of(outs[nm], nm, 1 - c)
            _remote(other, other, send_sems.at[i], recv_sems.at[i], (x, y, 1 - c)).wait_recv()
        for cp in cps:
            cp.wait_send()

    return _hbm_call(body, name, arrays, lambda nm, a: jax.ShapeDtypeStruct(a.shape, a.dtype),
                     (len(names), len(names)), in_place=True)


def all_reduce_small(v):
    def body(v_ref, tot_ref, gath_ref, send_sems, recv_sems):
        x, y, c = lax.axis_index("x"), lax.axis_index("y"), lax.axis_index("c")
        me = 4 * x + 2 * y + c
        gath_ref[me] = v_ref[...]

        def peer(k):
            m = k + 1
            return (x ^ (m >> 2 & 1), y ^ (m >> 1 & 1), c ^ (m & 1))

        sends = [pltpu.make_async_remote_copy(src_ref=v_ref, dst_ref=gath_ref.at[me], send_sem=send_sems.at[k],
                                              recv_sem=recv_sems.at[k], device_id=peer(k), device_id_type=MESH)
                 for k in range(N_DEV - 1)]
        for cp in sends:
            cp.start()
        for k in range(N_DEV - 1):
            px, py, pc = peer(k)
            pltpu.make_async_remote_copy(src_ref=v_ref, dst_ref=gath_ref.at[4 * px + 2 * py + pc],
                                         send_sem=send_sems.at[k], recv_sem=recv_sems.at[k], device_id=peer(k),
                                         device_id_type=MESH).wait_recv()
        for cp in sends:
            cp.wait_send()
        acc = gath_ref[0]
        for d in range(1, N_DEV):
            acc = acc + gath_ref[d]
        tot_ref[...] = acc

    vm = pl.BlockSpec(memory_space=pltpu.VMEM)
    return pl.pallas_call(
        body, name="ar_small", in_specs=[vm], out_specs=[vm, vm],
        out_shape=(jax.ShapeDtypeStruct(v.shape, v.dtype), jax.ShapeDtypeStruct((N_DEV,) + v.shape, v.dtype)),
        scratch_shapes=[pltpu.SemaphoreType.DMA((N_DEV - 1,)), pltpu.SemaphoreType.DMA((N_DEV - 1,))],
    )(v)[0]


def _elementwise(fn, name, ins, n_out, out_dtype=F32):
    r, cdim = ins[0].shape
    tr = _pick(r, tuple(p for p in (488, 256, 128, 104, 64, 32, 16, 8) if p * cdim * 4 <= 2 * 1024 * 1024))
    spec = pl.BlockSpec((tr, cdim), lambda i: (i, 0))

    def body(*refs):
        res = fn(*[ref[...] for ref in refs[:len(ins)]])
        for o_ref, v in zip(refs[len(ins):], res):
            o_ref[...] = v

    return pl.pallas_call(
        body, name=name, grid=(r // tr,), in_specs=[spec] * len(ins), out_specs=[spec] * n_out,
        out_shape=tuple(jax.ShapeDtypeStruct((r, cdim), out_dtype) for _ in range(n_out)),
        compiler_params=pltpu.CompilerParams(dimension_semantics=("parallel",), vmem_limit_bytes=VMEM_LIMIT),
    )(*ins)


def _half_block_specs(nm, shard_shape):
    r, cdim = shard_shape
    if nm in _SPLIT_COLS:
        return (None, r, cdim // 2), (lambda j, c: (j, 0, c))
    return (None, r // 2, cdim), (lambda j, c: (j, c, 0))


def _presum(nm, sel, g32, a):
    blk, at = _half_block_specs(nm, g32.shape[1:])

    def body(s_ref, g_ref, a_ref, o_ref):
        del s_ref
        o_ref[...] = (g_ref[...] + a_ref[...]).astype(BF16)

    return pl.pallas_call(
        body, name="rs_presum_" + nm,
        grid_spec=pltpu.PrefetchScalarGridSpec(
            num_scalar_prefetch=1, grid=(N_CHIPS,),
            in_specs=[pl.BlockSpec(blk, lambda j, s: at(j, s[0])), pl.BlockSpec(blk, lambda j, s: (j, 0, 0))],
            out_specs=pl.BlockSpec(blk, lambda j, s: (j, 0, 0))),
        out_shape=jax.ShapeDtypeStruct(a.shape, BF16),
        compiler_params=pltpu.CompilerParams(dimension_semantics=("parallel",), vmem_limit_bytes=VMEM_LIMIT),
    )(sel, g32, a)


def _finalsum(nm, sel, g32, a, got):
    blk, at = _half_block_specs(nm, g32.shape[1:])

    def body(s_ref, g_ref, a_ref, r_ref, o_ref):
        del s_ref
        acc = g_ref[...] + a_ref[...]
        for k in range(3):
            acc = acc + r_ref[k].astype(F32)
        o_ref[...] = acc

    return pl.pallas_call(
        body, name="rs_final_" + nm,
        grid_spec=pltpu.PrefetchScalarGridSpec(
            num_scalar_prefetch=1, grid=(1,),
            in_specs=[pl.BlockSpec(blk, lambda i, s: at(s[1], s[0])), pl.BlockSpec(blk, lambda i, s: (s[1], 0, 0)),
                      pl.BlockSpec(got.shape, lambda i, s: (0, 0, 0))],
            out_specs=pl.BlockSpec(blk[1:], lambda i, s: at(0, s[0])[1:])),
        out_shape=jax.ShapeDtypeStruct(g32.shape[1:], F32),
        compiler_params=pltpu.CompilerParams(dimension_semantics=("arbitrary",), vmem_limit_bytes=VMEM_LIMIT),
    )(sel, g32, a, got)


def _adamw(w, g, m, v, name):
    shape = w.shape
    to2 = lambda a: a.reshape(-1, shape[-1])

    def fn(w_, g_, m_, v_):
        m_new = ADAM_B1 * m_ + (1.0 - ADAM_B1) * g_
        v_new = ADAM_B2 * v_ + (1.0 - ADAM_B2) * (g_ * g_)
        m_hat = m_new / (1.0 - ADAM_B1 ** ADAM_STEP)
        v_hat = v_new / (1.0 - ADAM_B2 ** ADAM_STEP)
        delta = -ADAM_LR * (m_hat / (jnp.sqrt(v_hat) + ADAM_EPS) + ADAM_WD * w_)
        return g_, delta, m_new, v_new

    outs = _elementwise(fn, name, [to2(a) for a in (w, g, m, v)], 4)
    return tuple(o.reshape(shape) for o in outs)


_BIG = ("w_mod", "w_in", "w_pa", "w_pd", "w_out", "w_up", "w_down")
_EARLY = ("w_mod", "w_in")
_LATE = ("w_pa", "w_pd", "w_out", "w_up", "w_down")
_COL_SHARDED = ("w_mod", "w_up")
_FULL_SHAPE = {"w_mod": (D_MODEL, MOD_W), "w_in": (IN_COLS, D_MODEL), "w_pa": (Q_W, D_MODEL), "w_pd": (GDN_W, D_MODEL),
               "w_out": (D_MODEL, D_MODEL), "w_up": (D_MODEL, 2 * D_FF), "w_down": (D_FF, D_MODEL)}


def _shard_shape(name):
    r, cdim = _FULL_SHAPE[name]
    return (r, cdim // N_CHIPS) if name in _COL_SHARDED else (r // N_CHIPS, cdim)


_CONV_ELEMS = 2 * (3 * CONV_W // N_CHIPS + 3 * 2 * D_FF // N_CHIPS)
_CONV_ROWS = 32


def _blocks_of_full(name, full):
    r, cdim = _FULL_SHAPE[name]
    if name in _COL_SHARDED:
        return full.reshape(r, N_CHIPS, cdim // N_CHIPS).transpose(1, 0, 2)
    return full.reshape(N_CHIPS, r // N_CHIPS, cdim)


def _full_of_blocks(name, blocks):
    r, cdim = _FULL_SHAPE[name]
    if name in _COL_SHARDED:
        return blocks.transpose(1, 0, 2).reshape(r, cdim)
    return blocks.reshape(r, cdim)


def _w_in_regroup(w_in_t):
    main = jnp.concatenate([w_in_t[:SMALL_AT], w_in_t[SMALL_AT + 4 * GDN_HEADS:]], axis=0)
    small = jnp.pad(w_in_t[SMALL_AT:SMALL_AT + 4 * GDN_HEADS], ((0, HEAD_DIM - 4 * GDN_HEADS), (0, 0)))
    return main, small


def _w_in_ungroup(main, small):
    return jnp.concatenate([main[:SMALL_AT], small[:4 * GDN_HEADS], main[SMALL_AT:]], axis=0)


_SMALL = ("c_ctx", "b_mod", "q_norm_w", "k_norm_w", "conv_qkv_w", "a_log", "dt_bias", "gdn_norm_w", "ffn_conv_w",
          "ffn_conv_b", "final_norm_w")


def _pack_small(tree, rows):
    flat = jnp.concatenate([tree[nm].reshape(-1) for nm in _SMALL])
    return jnp.pad(flat, (0, rows * 128 - flat.shape[0])).reshape(rows, 128)


def _unpack_small(packed, like):
    flat, out, off = packed.reshape(-1), {}, 0
    for nm in _SMALL:
        size = int(np.prod(like[nm].shape))
        out[nm] = flat[off:off + size].reshape(like[nm].shape)
        off += size
    return out


def kernel(x, c, ctx, c_ctx, w_mod, b_mod, w_in, q_norm_w, k_norm_w, conv_qkv_w, a_log, dt_bias, gdn_norm_w, w_pa, w_pd, w_out, w_up, ffn_conv_w, ffn_conv_b, w_down, final_norm_w, loss_target, m_c_ctx, m_w_mod, m_b_mod, m_w_in, m_q_norm_w, m_k_norm_w, m_conv_qkv_w, m_a_log, m_dt_bias, m_gdn_norm_w, m_w_pa, m_w_pd, m_w_out, m_w_up, m_ffn_conv_w, m_ffn_conv_b, m_w_down, m_final_norm_w, v_c_ctx, v_w_mod, v_b_mod, v_w_in, v_q_norm_w, v_k_norm_w, v_conv_qkv_w, v_a_log, v_dt_bias, v_gdn_norm_w, v_w_pa, v_w_pd, v_w_out, v_w_up, v_ffn_conv_w, v_ffn_conv_b, v_w_down, v_final_norm_w):
    names = ("c_ctx", "w_mod", "b_mod", "w_in", "q_norm_w", "k_norm_w", "conv_qkv_w", "a_log", "dt_bias", "gdn_norm_w",
             "w_pa", "w_pd", "w_out", "w_up", "ffn_conv_w", "ffn_conv_b", "w_down", "final_norm_w")
    w_sh = dict(c_ctx=c_ctx, w_mod=w_mod, b_mod=b_mod, w_in=w_in, q_norm_w=q_norm_w, k_norm_w=k_norm_w,
                conv_qkv_w=conv_qkv_w, a_log=a_log, dt_bias=dt_bias, gdn_norm_w=gdn_norm_w, w_pa=w_pa, w_pd=w_pd,
                w_out=w_out, w_up=w_up, ffn_conv_w=ffn_conv_w, ffn_conv_b=ffn_conv_b, w_down=w_down,
                final_norm_w=final_norm_w)
    m_sh = dict(c_ctx=m_c_ctx, w_mod=m_w_mod, b_mod=m_b_mod, w_in=m_w_in, q_norm_w=m_q_norm_w, k_norm_w=m_k_norm_w,
                conv_qkv_w=m_conv_qkv_w, a_log=m_a_log, dt_bias=m_dt_bias, gdn_norm_w=m_gdn_norm_w, w_pa=m_w_pa,
                w_pd=m_w_pd, w_out=m_w_out, w_up=m_w_up, ffn_conv_w=m_ffn_conv_w, ffn_conv_b=m_ffn_conv_b,
                w_down=m_w_down, final_norm_w=m_final_norm_w)
    v_sh = dict(c_ctx=v_c_ctx, w_mod=v_w_mod, b_mod=v_b_mod, w_in=v_w_in, q_norm_w=v_q_norm_w, k_norm_w=v_k_norm_w,
                conv_qkv_w=v_conv_qkv_w, a_log=v_a_log, dt_bias=v_dt_bias, gdn_norm_w=v_gdn_norm_w, w_pa=v_w_pa,
                w_pd=v_w_pd, w_out=v_w_out, w_up=v_w_up, ffn_conv_w=v_ffn_conv_w, ffn_conv_b=v_ffn_conv_b,
                w_down=v_w_down, final_norm_w=v_final_norm_w)
    chip = 2 * lax.axis_index("x") + lax.axis_index("y")

    conv_bits = jnp.concatenate([lax.bitcast_convert_type(w_sh[nm][0], BF16).reshape(-1)
                                 for nm in ("conv_qkv_w", "ffn_conv_w")])
    shards = {nm: w_sh[nm][0].astype(BF16).T if nm == "w_in" else w_sh[nm][0].astype(BF16) for nm in _BIG}
    shards["conv"] = jnp.pad(conv_bits, (0, _CONV_ROWS * D_MODEL - _CONV_ELEMS)).reshape(_CONV_ROWS, D_MODEL)
    gathered = all_gather_chips({nm: shards[nm] for nm in _EARLY + ("conv",)})
    gathered, late_shards = lax.optimization_barrier((gathered, {nm: shards[nm] for nm in _LATE}))
    started = push_start("ag_late_start", late_shards, {nm: (N_CHIPS,) + a.shape for nm, a in late_shards.items()},
                         _gather_copies, 4 * len(_LATE))
    c = c + started[-1][0:1, 0:1]

    wb = {nm: _full_of_blocks(nm, gathered[nm]) for nm in _EARLY}
    wb["w_in_main"], wb["w_in_small"] = _w_in_regroup(wb.pop("w_in"))
    conv_all = gathered["conv"].reshape(N_CHIPS, -1)[:, :_CONV_ELEMS]
    n_cq = 2 * 3 * CONV_W // N_CHIPS
    unbits = lambda a, w: lax.bitcast_convert_type(a.reshape(N_CHIPS, 3, w // N_CHIPS, 2), F32).transpose(1, 0, 2).reshape(3, w)
    ws = dict(c_ctx=c_ctx, b_mod=b_mod, q_norm_w=q_norm_w, k_norm_w=k_norm_w, a_log=a_log[0], dt_bias=dt_bias[0],
              gdn_norm_w=gdn_norm_w, ffn_conv_b=ffn_conv_b, final_norm_w=final_norm_w,
              conv_qkv_w=unbits(conv_all[:, :n_cq], CONV_W), ffn_conv_w=unbits(conv_all[:, n_cq:], 2 * D_FF))
    wz = {nm: jnp.zeros(a.shape, F32) for nm, a in wb.items()}
    wz.update({nm: jnp.zeros(_FULL_SHAPE[nm], F32) for nm in _LATE})

    mixed, vjp_mix = jax.vjp(lambda x_, wz_, ws_: token_mixing(x_, wz_, wb, ws_, c, ctx[0]), x[0], wz, ws)
    got = push_wait("ag_late_wait", started, _gather_copies, mixed["gdn_x"])
    wb_late = {nm: _full_of_blocks(nm, got[nm]) for nm in _LATE}
    loss_local, vjp_chan = jax.vjp(
        lambda mixed_, wz_, ws_: channel_mixing(mixed_, wz_, wb_late, ws_, loss_target[0]), mixed, wz, ws)
    d_mixed, gz_chan, gs_chan = vjp_chan(jnp.ones((), F32))

    sel = jnp.stack([lax.axis_index("c"), chip]).astype(jnp.int32)
    g32_late = {nm: _blocks_of_full(nm, gz_chan[nm]) for nm in _LATE}
    theirs_late = sibling_halves(g32_late, "rs_sibling_late")
    sums_late = {nm: _presum(nm, sel, g32_late[nm], theirs_late[nm]) for nm in _LATE}
    scattering = push_start("rs_late_start", sums_late, {nm: (3,) + a.shape[1:] for nm, a in sums_late.items()},
                            _scatter_copies, 3 * len(_LATE))
    d_mixed = {**d_mixed, "gdn_x": d_mixed["gdn_x"] + scattering[-1][0:1, 0:1]}
    gx, gz_mix, gs_mix = vjp_mix(d_mixed)
    gs = jax.tree.map(jnp.add, gs_mix, gs_chan)
    got_late = push_wait("rs_late_wait", scattering, _scatter_copies, gx)
    loss = lax.psum(loss_local, ("x", "y", "c"))

    gz_mix["w_in"] = _w_in_ungroup(gz_mix.pop("w_in_main"), gz_mix.pop("w_in_small"))
    g32 = {nm: _blocks_of_full(nm, gz_mix[nm]) for nm in _EARLY}
    theirs = sibling_halves(g32, "rs_sibling")
    sums = {nm: _presum(nm, sel, g32[nm], theirs[nm]) for nm in _EARLY}
    scattering = push_start("rs_early_start", sums, {nm: (3,) + a.shape[1:] for nm, a in sums.items()},
                            _scatter_copies, 3 * len(_EARLY))
    zero = scattering[-1][0:1, 0:1].astype(BF16)
    got_late = {nm: a + zero if nm in ("w_up", "w_down") else a for nm, a in got_late.items()}
    g_big = sibling_assemble({nm: _finalsum(nm, sel, g32_late[nm], theirs_late[nm], got_late[nm]) for nm in _LATE},
                             "rs_assemble_late")
    grads, deltas, new_m, new_v = {}, {}, {}, {}

    def adamw_big(nm):
        g = g_big[nm].T if nm == "w_in" else g_big[nm]
        grads[nm], deltas[nm], new_m[nm], new_v[nm] = (
            o[None] for o in _adamw(w_sh[nm][0], g, m_sh[nm][0], v_sh[nm][0], "adamw_" + nm))

    for nm in _LATE:
        adamw_big(nm)
    got = push_wait("rs_early_wait", scattering, _scatter_copies, deltas["w_up"])
    g_big.update(sibling_assemble({nm: _finalsum(nm, sel, g32[nm], theirs[nm], got[nm]) for nm in _EARLY},
                                  "rs_assemble"))
    for nm in _EARLY:
        adamw_big(nm)

    gs["a_log"], gs["dt_bias"] = gs["a_log"][None], gs["dt_bias"][None]
    like = {nm: gs[nm] for nm in _SMALL}
    small_rows = -(-sum(int(np.prod(like[nm].shape)) for nm in _SMALL) // 1024) * 8
    g_small = _unpack_small(all_reduce_small(_pack_small(gs, small_rows)), like)
    for nm, width in (("conv_qkv_w", CONV_W), ("ffn_conv_w", 2 * D_FF)):
        g_small[nm] = lax.dynamic_slice_in_dim(g_small[nm], chip * (width // N_CHIPS), width // N_CHIPS, axis=1)[None]

    shard_like = {nm: w_sh[nm] for nm in _SMALL}
    rows_l = -(-sum(int(np.prod(shard_like[nm].shape)) for nm in _SMALL) // 1024) * 8
    g_l = _pack_small({nm: g_small[nm].reshape(w_sh[nm].shape) for nm in _SMALL}, rows_l)
    outs = _adamw(_pack_small(w_sh, rows_l), g_l, _pack_small(m_sh, rows_l), _pack_small(v_sh, rows_l), "adamw_small")
    for tree, packed in zip((grads, deltas, new_m, new_v), outs):
        tree.update(_unpack_small(packed, shard_like))

    return (loss, gx[None], *[grads[nm] for nm in names], *[deltas[nm] for nm in names],
            *[new_m[nm] for nm in names], *[new_v[nm] for nm in names])
```

```python
import functools
import math

import jax
import jax.numpy as jnp
import numpy as np
from jax import lax
from jax.experimental import pallas as pl
from jax.experimental.pallas import tpu as pltpu

F32 = jnp.float32
BF16 = jnp.bfloat16
HIGHEST = lax.Precision.HIGHEST
MESH = pl.DeviceIdType.MESH

D_MODEL = 1024
GRID_W = 64
ATTN_HEADS = 8
ATTN_KV_HEADS = 2
ATTN_GROUP = ATTN_HEADS // ATTN_KV_HEADS
HEAD_DIM = 128
ROPE_THETA = 10000.0
GDN_HEADS = 8
GDN_CHUNK = 64
D_FF = 2816
NORM_EPS = 1e-6
KV_W = ATTN_KV_HEADS * HEAD_DIM
Q_W = ATTN_HEADS * HEAD_DIM
GDN_W = GDN_HEADS * HEAD_DIM
CONV_W = 3 * GDN_W
MOD_W = 6 * D_MODEL
IN_COLS = 2 * KV_W + CONV_W + 4 * GDN_HEADS + Q_W + GDN_W + 2 * D_MODEL
IN_MAIN = IN_COLS - 4 * GDN_HEADS
SMALL_AT = 2 * KV_W + CONV_W
N_CHIPS = 4
N_DEV = 8

ADAM_LR = 0.001
ADAM_B1 = 0.9
ADAM_B2 = 0.999
ADAM_EPS = 1e-08
ADAM_WD = 0.01
ADAM_STEP = 10

VMEM_LIMIT = 48 * 1024 * 1024
MATMUL_VMEM_BUDGET = 40 * 1024 * 1024
MATMUL_STEP_BYTES = 1200 * 1024


def _pick(dim, prefs):
    for p in prefs:
        if p <= dim and dim % p == 0:
            return p
    return dim


_DIMS = {
    "nn": (((1,), (0,)), ((), ())),
    "nt": (((1,), (1,)), ((), ())),
    "tn": (((0,), (0,)), ((), ())),
}


def _matmul_plan(m, n, k, a_bytes, b_bytes):
    best = None
    for tm in (2304, 2048, 1152, 1024, 768, 512, 384, 256, 128, m):
        for tn in (2560, 1536, 1408, 1024, 768, 512, 256, 128, n):
            for tk in (3840, 2816, 2560, 2304, 2048, 1920, 1408, 1152, 1024, 768, 512, 256, 128, k):
                if tm > m or tn > n or tk > k or m % tm or n % tn or k % tk:
                    continue
                blocks = tm * tk * a_bytes + tk * tn * b_bytes + tm * tn * 4
                casts = (tm * tk * 2 if a_bytes > 2 else 0) + (tk * tn * 2 if b_bytes > 2 else 0) + tm * tn * 4
                if 2 * blocks + casts > MATMUL_VMEM_BUDGET:
                    continue
                nm, nn, nk = m // tm, n // tn, k // tk
                size_a, size_b = m * k * a_bytes, k * n * b_bytes
                for n_inner in (True, False):
                    if n_inner:
                        traffic = (size_a if nk == 1 else nn * size_a) + nm * size_b
                    else:
                        traffic = nn * size_a + (size_b if nk == 1 else nm * size_b)
                    cost = traffic + nm * nn * nk * MATMUL_STEP_BYTES + (nk - 1) * m * n * 4
                    if best is None or cost < best[0]:
                        best = (cost, tm, tn, tk, n_inner)
    return best[1:]


def _matmul(a, b, mode, name):
    if mode == "nn":
        (m, k), (_, n) = a.shape, b.shape
    elif mode == "nt":
        (m, k), (n, _) = a.shape, b.shape
    else:
        (k, m), (_, n) = a.shape, b.shape
    tm, tn, tk, n_inner = _matmul_plan(m, n, k, a.dtype.itemsize, b.dtype.itemsize)
    nk = k // tk
    ij = (lambda g0, g1: (g0, g1)) if n_inner else (lambda g0, g1: (g1, g0))
    if mode == "tn":
        a_spec = pl.BlockSpec((tk, tm), lambda g0, g1, l: (l, ij(g0, g1)[0]))
    else:
        a_spec = pl.BlockSpec((tm, tk), lambda g0, g1, l: (ij(g0, g1)[0], l))
    if mode == "nt":
        b_spec = pl.BlockSpec((tn, tk), lambda g0, g1, l: (ij(g0, g1)[1], l))
    else:
        b_spec = pl.BlockSpec((tk, tn), lambda g0, g1, l: (l, ij(g0, g1)[1]))
    dims = _DIMS[mode]

    def body(a_ref, b_ref, o_ref):
        part = lax.dot_general(a_ref[...].astype(BF16), b_ref[...].astype(BF16), dims, preferred_element_type=F32)
        if nk == 1:
            o_ref[...] = part
        else:
            l = pl.program_id(2)

            @pl.when(l == 0)
            def _():
                o_ref[...] = part

            @pl.when(l > 0)
            def _():
                o_ref[...] += part

    return pl.pallas_call(
        body,
        name=name,
        grid=(m // tm, n // tn, nk) if n_inner else (n // tn, m // tm, nk),
        in_specs=[a_spec, b_spec],
        out_specs=pl.BlockSpec((tm, tn), lambda g0, g1, l: ij(g0, g1)),
        out_shape=jax.ShapeDtypeStruct((m, n), F32),
        compiler_params=pltpu.CompilerParams(dimension_semantics=("parallel", "parallel", "arbitrary"),
                                             vmem_limit_bytes=VMEM_LIMIT),
    )(a, b)


@functools.partial(jax.custom_vjp, nondiff_argnums=(3,))
def pmm(a, w, wz, name):
    del wz
    return _matmul(a, w, "nn", name + "_f")


def _pmm_fwd(a, w, wz, name):
    del wz
    return _matmul(a, w, "nn", name + "_f"), (a, w)


def _pmm_bwd(name, res, g):
    a, w = res
    da = _matmul(g, w, "nt", name + "_da")
    if a.shape[0] < 128:
        pad = 128 - a.shape[0]
        at = jnp.pad(a.T, ((0, 0), (0, pad)))
        gp = jnp.pad(g, ((0, pad), (0, 0)))
        dw = _matmul(at, gp, "nn", name + "_dw")
    else:
        dw = _matmul(a, g, "tn", name + "_dw")
    return da, jnp.zeros_like(w), dw


pmm.defvjp(_pmm_fwd, _pmm_bwd)


@functools.partial(jax.custom_vjp, nondiff_argnums=(3,))
def pmm_t(a, wt, wtz, name):
    del wtz
    return _matmul(a, wt, "nt", name + "_f")


def _pmm_t_fwd(a, wt, wtz, name):
    del wtz
    return _matmul(a, wt, "nt", name + "_f"), (a, wt)


def _pmm_t_bwd(name, res, g):
    a, wt = res
    return _matmul(g, wt, "nn", name + "_da"), jnp.zeros_like(wt), _matmul(g, a, "tn", name + "_dw")


pmm_t.defvjp(_pmm_t_fwd, _pmm_t_bwd)


def rowop(fn, name, rows, bcs=(), crows=(), cbcs=(), tr=256):
    rows, bcs, crows, cbcs = tuple(rows), tuple(bcs), tuple(crows), tuple(cbcs)
    n_rows = rows[0].shape[0]
    tr = _pick(n_rows, (tr, 128, 64, 32, 16, 8))
    nr, nb, ncr, ncb = len(rows), len(bcs), len(crows), len(cbcs)
    n_in = nr + nb + ncr + ncb
    grid = (n_rows // tr,)

    def blk(arr):
        return jax.ShapeDtypeStruct((tr, arr.shape[1]), arr.dtype)

    def row_spec(arr):
        return pl.BlockSpec((tr, arr.shape[1]), lambda i: (i, 0))

    def bc_spec(arr):
        return pl.BlockSpec(arr.shape, lambda i: (0, 0))

    out_blk = jax.eval_shape(fn, *[blk(r) for r in rows], *bcs, *[blk(r) for r in crows], *cbcs)
    n_out = len(out_blk)
    out_shape = tuple(jax.ShapeDtypeStruct((n_rows, o.shape[1]), o.dtype) for o in out_blk)
    in_specs = ([row_spec(r) for r in rows] + [bc_spec(b) for b in bcs]
                + [row_spec(r) for r in crows] + [bc_spec(b) for b in cbcs])

    def order(vals):
        return vals

    def fwd_call(args):
        def body(*refs):
            vals = [r[...] for r in refs[:n_in]]
            res = fn(*order(vals))
            for o_ref, r in zip(refs[n_in:], res):
                o_ref[...] = r

        return pl.pallas_call(
            body, name=name + "_f", grid=grid, in_specs=in_specs,
            out_specs=[row_spec(o) for o in out_shape], out_shape=out_shape,
            compiler_params=pltpu.CompilerParams(dimension_semantics=("parallel",), vmem_limit_bytes=VMEM_LIMIT),
        )(*args)

    def bwd_call(args, cts):
        def body(*refs):
            vals = [r[...] for r in refs[:n_in]]
            ct_refs = refs[n_in:n_in + n_out]
            d_rows = refs[n_in + n_out:n_in + n_out + nr]
            d_bcs = refs[n_in + n_out + nr:]
            consts = vals[nr + nb:]
            _, vjp = jax.vjp(lambda *p: fn(*p, *consts), *vals[:nr + nb])
            grads = vjp(tuple(c[...] for c in ct_refs))
            for ref, g in zip(d_rows, grads[:nr]):
                ref[...] = g

            @pl.when(pl.program_id(0) == 0)
            def _():
                for ref in d_bcs:
                    ref[...] = jnp.zeros_like(ref)

            for ref, g in zip(d_bcs, grads[nr:]):
                ref[...] += g

        d_shape = tuple(jax.ShapeDtypeStruct(r.shape, r.dtype) for r in rows + bcs)
        return pl.pallas_call(
            body, name=name + "_b", grid=grid,
            in_specs=in_specs + [row_spec(o) for o in out_shape],
            out_specs=[row_spec(r) for r in rows] + [bc_spec(b) for b in bcs], out_shape=d_shape,
            compiler_params=pltpu.CompilerParams(dimension_semantics=("arbitrary",), vmem_limit_bytes=VMEM_LIMIT),
        )(*args, *cts)

    @jax.custom_vjp
    def op(diff, const):
        return fwd_call(diff + const)

    def op_fwd(diff, const):
        return fwd_call(diff + const), (diff, const)

    def op_bwd(res, cts):
        diff, const = res
        grads = bwd_call(diff + const, tuple(cts))
        return tuple(grads), tuple(jnp.zeros_like(c) for c in const)

    op.defvjp(op_fwd, op_bwd)
    return op(rows + bcs, crows + cbcs)


def colop(fn, name, arrays, uses, n_const, nblk, cw=128):
    arrays = tuple(arrays)
    n_diff = len(arrays) - n_const
    nd = sum(1 for u in uses if u[0] < n_diff)
    assert all(u[0] < n_diff for u in uses[:nd]) and all(u[0] >= n_diff for u in uses[nd:])

    def spec(u):
        return pl.BlockSpec((arrays[u[0]].shape[0], cw), lambda j, off=u[1]: (0, off + j))

    def out_spec(rows):
        return pl.BlockSpec((rows, cw), lambda j: (0, j))

    out_blk = jax.eval_shape(fn, *[jax.ShapeDtypeStruct((arrays[u[0]].shape[0], cw), arrays[u[0]].dtype)
                                   for u in uses])
    out_shape = tuple(jax.ShapeDtypeStruct((o.shape[0], nblk * cw), o.dtype) for o in out_blk)
    params = pltpu.CompilerParams(dimension_semantics=("parallel",), vmem_limit_bytes=VMEM_LIMIT)

    def fwd_call(arrs):
        def body(*refs):
            res = fn(*[r[...] for r in refs[:len(uses)]])
            for o_ref, r in zip(refs[len(uses):], res):
                o_ref[...] = r

        return pl.pallas_call(
            body, name=name + "_f", grid=(nblk,), in_specs=[spec(u) for u in uses],
            out_specs=[out_spec(o.shape[0]) for o in out_shape], out_shape=out_shape, compiler_params=params,
        )(*[arrs[u[0]] for u in uses])

    def bwd_call(arrs, cts):
        def body(*refs):
            vals = [r[...] for r in refs[:len(uses)]]
            ct_refs = refs[len(uses):len(uses) + len(out_shape)]
            _, vjp = jax.vjp(lambda *p: fn(*p, *vals[nd:]), *vals[:nd])
            for ref, g in zip(refs[len(uses) + len(out_shape):], vjp(tuple(c[...] for c in ct_refs))):
                ref[...] = g

        d_shape = tuple(jax.ShapeDtypeStruct((arrays[u[0]].shape[0], nblk * cw), F32) for u in uses[:nd])
        return pl.pallas_call(
            body, name=name + "_b", grid=(nblk,),
            in_specs=[spec(u) for u in uses] + [out_spec(o.shape[0]) for o in out_shape],
            out_specs=[out_spec(s.shape[0]) for s in d_shape], out_shape=d_shape, compiler_params=params,
        )(*[arrs[u[0]] for u in uses], *cts)

    @jax.custom_vjp
    def op(diff, const):
        return fwd_call(diff + const)

    def op_fwd(diff, const):
        return fwd_call(diff + const), (diff, const)

    def op_bwd(res, cts):
        diff, const = res
        d_uses = bwd_call(diff + const, tuple(cts))
        grads = []
        for i in range(n_diff):
            parts = sorted([(u[1], k) for k, u in enumerate(uses[:nd]) if u[0] == i])
            grads.append(d_uses[parts[0][1]] if len(parts) == 1
                         else jnp.concatenate([d_uses[k] for _, k in parts], axis=1))
        return tuple(grads), tuple(jnp.zeros_like(c) for c in const)

    op.defvjp(op_fwd, op_bwd)
    return op(arrays[:n_diff], arrays[n_diff:])


@functools.partial(jax.custom_vjp, nondiff_argnums=(1,))
def _roll_rows(x, k):
    return pltpu.roll(x, k % x.shape[0], 0)


def _roll_rows_fwd(x, k):
    return _roll_rows(x, k), None


def _roll_rows_bwd(k, _, g):
    return (_roll_rows(g, -k),)


_roll_rows.defvjp(_roll_rows_fwd, _roll_rows_bwd)


def _conv3(x, w0, w1, w2, starts):
    rows = lax.broadcasted_iota(jnp.int32, x.shape, 0)
    ends = tuple(s - 1 for s in starts[1:]) + (x.shape[0] - 1,)
    first = functools.reduce(jnp.logical_or, [rows == s for s in starts])
    last = functools.reduce(jnp.logical_or, [rows == e for e in ends])
    prev = jnp.where(first, 0.0, _roll_rows(x, 1))
    nxt = jnp.where(last, 0.0, _roll_rows(x, -1))
    return prev * w0 + x * w1 + nxt * w2


def _rms(x):
    return x * lax.rsqrt(jnp.mean(x * x, axis=-1, keepdims=True) + NORM_EPS)


def _heads(x, n):
    return [x[:, h * HEAD_DIM:(h + 1) * HEAD_DIM] for h in range(n)]


_NT = (((1,), (1,)), ((), ()))
_TN = (((0,), (0,)), ((), ()))
_TQ = 256


_N_SUB = 2


def _sub_rows(ref, i):
    rows = ref.shape[0] // _N_SUB
    return ref[i * rows:(i + 1) * rows, :].astype(BF16)


def _attn_probs(qs, k):
    s = _each(lambda q: lax.dot_general(q, k, _NT, preferred_element_type=F32) * (HEAD_DIM ** -0.5), qs)
    m = _each(lambda a: jnp.max(a, axis=-1, keepdims=True), s)
    e = _each(lambda a, b: jnp.exp(a - b), s, m)
    inv = _each(lambda a: 1.0 / jnp.sum(a, axis=-1, keepdims=True), e)
    return _each(lambda a, b: a * b, e, inv)


def _attn_fwd_call(q, k, v):
    n, t = q.shape[0], k.shape[0]
    tq = _pick(n, (_TQ, 128))

    def body(q_ref, k_ref, v_ref, o_ref):
        vb = v_ref[...].astype(BF16)
        ps = _attn_probs([_sub_rows(q_ref, i) for i in range(_N_SUB)], k_ref[...].astype(BF16))
        rows = tq // _N_SUB
        for i, p in enumerate(ps):
            o_ref[i * rows:(i + 1) * rows, :] = jnp.dot(p.astype(BF16), vb, preferred_element_type=F32)

    return pl.pallas_call(
        body, name="attn_f", grid=(ATTN_HEADS, n // tq),
        in_specs=[pl.BlockSpec((tq, HEAD_DIM), lambda h, i: (i, h)),
                  pl.BlockSpec((t, HEAD_DIM), lambda h, i: (0, h // ATTN_GROUP)),
                  pl.BlockSpec((t, HEAD_DIM), lambda h, i: (0, h // ATTN_GROUP))],
        out_specs=pl.BlockSpec((tq, HEAD_DIM), lambda h, i: (i, h)),
        out_shape=jax.ShapeDtypeStruct(q.shape, F32),
        compiler_params=pltpu.CompilerParams(dimension_semantics=("parallel", "parallel"),
                                             vmem_limit_bytes=VMEM_LIMIT),
    )(q, k, v)


def _attn_bwd_call(q, k, v, do):
    n, t = q.shape[0], k.shape[0]
    tq = _pick(n, (_TQ, 128))

    def body(q_ref, k_ref, v_ref, do_ref, dq_ref, dk_ref, dv_ref):
        @pl.when((pl.program_id(1) == 0) & (pl.program_id(2) == 0))
        def _():
            dk_ref[...] = jnp.zeros_like(dk_ref)
            dv_ref[...] = jnp.zeros_like(dv_ref)

        kb, vb = k_ref[...].astype(BF16), v_ref[...].astype(BF16)
        qs = [_sub_rows(q_ref, i) for i in range(_N_SUB)]
        dos = [_sub_rows(do_ref, i) for i in range(_N_SUB)]
        ps = _attn_probs(qs, kb)
        dps = _each(lambda d: lax.dot_general(d, vb, _NT, preferred_element_type=F32), dos)
        dss = _each(lambda p, dp: (p * (dp - jnp.sum(p * dp, axis=-1, keepdims=True)) * (HEAD_DIM ** -0.5)).astype(BF16),
                    ps, dps)
        rows = tq // _N_SUB
        for i, ds in enumerate(dss):
            dq_ref[i * rows:(i + 1) * rows, :] = jnp.dot(ds, kb, preferred_element_type=F32)
        dk_ref[...] += sum(_each(lambda ds, q: lax.dot_general(ds, q, _TN, preferred_element_type=F32), dss, qs))
        dv_ref[...] += sum(_each(lambda p, d: lax.dot_general(p.astype(BF16), d, _TN, preferred_element_type=F32),
                                 ps, dos))

    q_spec = pl.BlockSpec((tq, HEAD_DIM), lambda kh, g, i: (i, kh * ATTN_GROUP + g))
    kv_spec = pl.BlockSpec((t, HEAD_DIM), lambda kh, g, i: (0, kh))
    return pl.pallas_call(
        body, name="attn_b", grid=(ATTN_KV_HEADS, ATTN_GROUP, n // tq),
        in_specs=[q_spec, kv_spec, kv_spec, q_spec],
        out_specs=[q_spec, kv_spec, kv_spec],
        out_shape=(jax.ShapeDtypeStruct(q.shape, F32), jax.ShapeDtypeStruct(k.shape, F32),
                   jax.ShapeDtypeStruct(v.shape, F32)),
        compiler_params=pltpu.CompilerParams(dimension_semantics=("parallel", "arbitrary", "arbitrary"),
                                             vmem_limit_bytes=VMEM_LIMIT),
    )(q, k, v, do)


@jax.custom_vjp
def attention(q, k, v):
    return _attn_fwd_call(q, k, v)


def _attention_fwd(q, k, v):
    return _attn_fwd_call(q, k, v), (q, k, v)


def _attention_bwd(res, do):
    return _attn_bwd_call(*res, do)


attention.defvjp(_attention_fwd, _attention_bwd)


_C = GDN_CHUNK


def _pdot(a, b):
    return jnp.dot(a, b, precision=lax.Precision.HIGH, preferred_element_type=F32)


@jax.custom_vjp
def _hdot(a, b):
    return jnp.dot(a.astype(BF16), b.astype(BF16), preferred_element_type=F32)


def _hdot_fwd(a, b):
    return _hdot(a, b), (a, b)


def _hdot_bwd(res, g):
    a, b = res
    gb = g.astype(BF16)
    return (lax.dot_general(gb, b.astype(BF16), _NT, preferred_element_type=F32),
            lax.dot_general(a.astype(BF16), gb, _TN, preferred_element_type=F32))


_hdot.defvjp(_hdot_fwd, _hdot_bwd)


def _each(fn, *lists):
    return [fn(*args) for args in zip(*lists)]


@jax.custom_vjp
def _unit_lower_inverse(low, blockdiag):
    return _unit_lower_inverse_chain(low, blockdiag)


def _unit_lower_inverse_fwd(low, blockdiag):
    t_inv = _unit_lower_inverse_chain(low, blockdiag)
    return t_inv, (t_inv, blockdiag)


def _unit_lower_inverse_bwd(res, d_inv):
    t_inv, blockdiag = res
    bf = lambda a: a.astype(BF16)
    left = _each(lambda t, g: lax.dot_general(bf(t), bf(g), _TN, preferred_element_type=F32), t_inv, d_inv)
    d_low = _each(lambda a, t: -lax.dot_general(bf(a), bf(t), _NT, preferred_element_type=F32), left, t_inv)
    return d_low, jnp.zeros_like(blockdiag)


_unit_lower_inverse.defvjp(_unit_lower_inverse_fwd, _unit_lower_inverse_bwd)


def _unit_lower_inverse_chain(low, blockdiag):
    eye = (lax.broadcasted_iota(jnp.int32, (_C, _C), 0) == lax.broadcasted_iota(jnp.int32, (_C, _C), 1)).astype(F32)
    ld = _each(lambda a: a * blockdiag, low)
    lo = _each(lambda a, d: a - d, low, ld)
    l2 = _each(_hdot, ld, ld)
    l4 = _each(_hdot, l2, l2)
    l8 = _each(_hdot, l4, l4)
    td = _each(lambda d, a2: _hdot(eye - d, eye + a2), ld, l2)
    td = _each(lambda t, a4: _hdot(t, eye + a4), td, l4)
    td = _each(lambda t, a8: _hdot(t, eye + a8), td, l8)
    nn = _each(_hdot, td, lo)
    n2 = _each(_hdot, nn, nn)
    out = _each(lambda n, m2: _hdot(eye - n, eye + m2), nn, n2)
    return _each(_hdot, out, td)


def _gdn_chunks(heads, blockdiag):
    q, k, v, b_b, be_b, e_b, kd_b, m1, dec, gl, s = (list(col) for col in zip(*heads))
    f32dot = lambda a, b: jnp.dot(a, b, preferred_element_type=F32)
    nt = lambda a, b: lax.dot_general(a, b, _NT, preferred_element_type=F32)
    kq_k = _each(lambda kx, qq: nt(jnp.concatenate([kx, qq], axis=0), kx), k, q)
    t_inv = _unit_lower_inverse(_each(lambda m, a: m * a[:_C], m1, kq_k), blockdiag)
    uw = _each(lambda t, b, x, be, kx: _hdot(t, jnp.concatenate([b * x, be * kx], axis=1)), t_inv, b_b, v, be_b, k)
    wq_s = _each(lambda a, qq, e, ss: f32dot(jnp.concatenate([a[:, HEAD_DIM:], qq * e], axis=0), ss), uw, q, e_b, s)
    delta = _each(lambda a, ws: a[:, :HEAD_DIM] - ws[:_C], uw, wq_s)
    p = _each(lambda d, a: d * a[_C:], dec, kq_k)
    o = _each(lambda ws, pp, dd: ws[_C:] + f32dot(pp, dd), wq_s, p, delta)
    s_new = _each(lambda g, ss, kx, kd, dd: g * ss + lax.dot_general(kx * kd, dd, _TN, preferred_element_type=F32),
                  gl, s, k, kd_b, delta)
    return o, s_new


def _blockdiag_mask():
    r = lax.broadcasted_iota(jnp.int32, (_C, _C), 0) >> 4
    c = lax.broadcasted_iota(jnp.int32, (_C, _C), 1) >> 4
    return (r == c).astype(F32)


_N_DIR = 2


def _scan_chunk(s, nc, ncc, reverse):
    return jnp.where(s < ncc, ncc - 1 - s, nc + ncc - 1 - s) if reverse else s


def _gdn_specs(nc, ncc, backward):
    step = (lambda s: nc - 1 - s) if backward else (lambda s: s)
    chunk = [lambda s, d=d: _scan_chunk(step(s), nc, ncc, d == 1) for d in range(_N_DIR)]
    tok = [pl.BlockSpec((_C, 3 * GDN_W), lambda s, d=d: (chunk[d](s), 0)) for d in range(_N_DIR)]
    park = [0, nc - ncc - 1]
    out = [pl.BlockSpec((_C, GDN_W), lambda s, d=d: (jnp.where(chunk[d](s) >= ncc, chunk[d](s) - ncc, park[d]), 0))
           for d in range(_N_DIR)]
    per_tok = pl.BlockSpec((_N_DIR, GDN_HEADS, _C, HEAD_DIM), lambda s: (0, 0, step(s), 0))
    mat = pl.BlockSpec((_N_DIR, GDN_HEADS, None, _C, _C), lambda s: (0, 0, step(s), 0, 0))
    row = pl.BlockSpec((_N_DIR, GDN_HEADS, None, 1, HEAD_DIM), lambda s: (0, 0, step(s), 0, 0))
    state = pl.BlockSpec((_N_DIR, GDN_HEADS, None, HEAD_DIM, HEAD_DIM), lambda s: (0, 0, step(s), 0, 0))
    return tok, out, per_tok, mat, row, state, chunk


def _head_cols(h, part):
    return slice((part * GDN_HEADS + h) * HEAD_DIM, (part * GDN_HEADS + h + 1) * HEAD_DIM)


def _gdn_heads(qkv_refs, factor_refs, state_of):
    return [[qkv_refs[d][:, _head_cols(h, 0)], qkv_refs[d][:, _head_cols(h, 1)], qkv_refs[d][:, _head_cols(h, 2)]]
            + [r[d, h] for r in factor_refs] + [state_of(d, h)]
            for d in range(_N_DIR) for h in range(GDN_HEADS)]


def _gdn_fwd_call(ncc, qkv, factors):
    t = qkv.shape[0]
    nc = t // _C
    tok, out, per_tok, mat, row, state, _ = _gdn_specs(nc, ncc, False)

    def body(*refs):
        qkv_refs, f_refs = refs[:_N_DIR], refs[_N_DIR:_N_DIR + 7]
        o_refs, sall_ref, s_ref = refs[_N_DIR + 7:2 * _N_DIR + 7], refs[2 * _N_DIR + 7], refs[-1]

        @pl.when(pl.program_id(0) == 0)
        def _():
            s_ref[...] = jnp.zeros_like(s_ref)

        heads = _gdn_heads(qkv_refs, f_refs, lambda d, h: s_ref[d, h])
        o, s_new = _gdn_chunks(heads, _blockdiag_mask())
        for d in range(_N_DIR):
            for h in range(GDN_HEADS):
                i = GDN_HEADS * d + h
                sall_ref[d, h] = heads[i][10]
                o_refs[d][:, _head_cols(h, 0)] = o[i]
                s_ref[d, h] = s_new[i]

    o_shape = jax.ShapeDtypeStruct((t - ncc * _C, GDN_W), F32)
    s_shape = (_N_DIR, GDN_HEADS, nc, HEAD_DIM, HEAD_DIM)
    return pl.pallas_call(
        body, name="gdn_f", grid=(nc,),
        in_specs=[*tok, per_tok, per_tok, per_tok, per_tok, mat, mat, row],
        out_specs=[*out, state], out_shape=[o_shape, o_shape, jax.ShapeDtypeStruct(s_shape, F32)],
        scratch_shapes=[pltpu.VMEM((_N_DIR, GDN_HEADS, HEAD_DIM, HEAD_DIM), F32)],
        compiler_params=pltpu.CompilerParams(dimension_semantics=("arbitrary",), vmem_limit_bytes=VMEM_LIMIT),
    )(qkv, qkv, *factors)


def _gdn_bwd_call(ncc, qkv, factors, sall, dos):
    t = qkv.shape[0]
    nc = t // _C
    tok, out, per_tok, mat, row, state, chunk = _gdn_specs(nc, ncc, True)

    def body(*refs):
        qkv_refs, f_refs, sall_ref = refs[:_N_DIR], refs[_N_DIR:_N_DIR + 7], refs[_N_DIR + 7]
        do_refs = refs[_N_DIR + 8:2 * _N_DIR + 8]
        dqkv_refs = refs[2 * _N_DIR + 8:3 * _N_DIR + 8]
        df_refs, ds_ref = refs[3 * _N_DIR + 8:3 * _N_DIR + 15], refs[-1]

        @pl.when(pl.program_id(0) == 0)
        def _():
            ds_ref[...] = jnp.zeros_like(ds_ref)

        bd = _blockdiag_mask()
        heads = _gdn_heads(qkv_refs, f_refs, lambda d, h: sall_ref[d, h])
        _, vjp = jax.vjp(lambda hs: _gdn_chunks(hs, bd), heads)
        live = [chunk[d](pl.program_id(0)) >= ncc for d in range(_N_DIR)]
        (all_grads,) = vjp(([jnp.where(live[d], do_refs[d][:, _head_cols(h, 0)], 0.0)
                             for d in range(_N_DIR) for h in range(GDN_HEADS)],
                            [ds_ref[d, h] for d in range(_N_DIR) for h in range(GDN_HEADS)]))
        for d in range(_N_DIR):
            for h in range(GDN_HEADS):
                grads = all_grads[GDN_HEADS * d + h]
                for part in range(3):
                    dqkv_refs[d][:, _head_cols(h, part)] = grads[part]
                for ref, g in zip(df_refs, grads[3:10]):
                    ref[d, h] = g
                ds_ref[d, h] = grads[10]

    shp = lambda a: jax.ShapeDtypeStruct(a.shape, F32)
    res = pl.pallas_call(
        body, name="gdn_b", grid=(nc,),
        in_specs=[*tok, per_tok, per_tok, per_tok, per_tok, mat, mat, row, state, *out],
        out_specs=[*tok, per_tok, per_tok, per_tok, per_tok, mat, mat, row],
        out_shape=[shp(qkv), shp(qkv)] + [shp(a) for a in factors],
        scratch_shapes=[pltpu.VMEM((_N_DIR, GDN_HEADS, HEAD_DIM, HEAD_DIM), F32)],
        compiler_params=pltpu.CompilerParams(dimension_semantics=("arbitrary",), vmem_limit_bytes=VMEM_LIMIT),
    )(qkv, qkv, *factors, sall, *dos)
    return res[0] + res[1], tuple(res[_N_DIR:])


@functools.partial(jax.custom_vjp, nondiff_argnums=(0,))
def gdn_scan(ncc, qkv, factors):
    o0, o1, _ = _gdn_fwd_call(ncc, qkv, factors)
    return o0, o1


def _gdn_scan_fwd(ncc, qkv, factors):
    o0, o1, sall = _gdn_fwd_call(ncc, qkv, factors)
    return (o0, o1), (qkv, factors, sall)


def _gdn_scan_bwd(ncc, res, dos):
    qkv, factors, sall = res
    return _gdn_bwd_call(ncc, qkv, factors, sall, list(dos))


gdn_scan.defvjp(_gdn_scan_fwd, _gdn_scan_bwd)


def _rope_tables(n, cl):
    t = np.arange(n)
    inv_freq = (ROPE_THETA ** (-np.arange(0, HEAD_DIM // 2, 2, dtype=np.float32) / (HEAD_DIM // 2))).astype(np.float32)
    ang_r = (t // GRID_W).astype(np.float32)[:, None] * inv_freq
    ang_c = (t % GRID_W).astype(np.float32)[:, None] * inv_freq
    cos = np.concatenate([np.cos(ang_r), np.cos(ang_r), np.cos(ang_c), np.cos(ang_c)], axis=1)
    sin = np.concatenate([-np.sin(ang_r), np.sin(ang_r), -np.sin(ang_c), np.sin(ang_c)], axis=1)
    cos_all = np.concatenate([np.ones((cl, HEAD_DIM), np.float32), cos], axis=0)
    sin_all = np.concatenate([np.zeros((cl, HEAD_DIM), np.float32), sin], axis=0)
    j = np.arange(HEAD_DIM)
    src = np.where((j % 64) < 32, j + 32, j - 32)
    perm = np.zeros((HEAD_DIM, HEAD_DIM), np.float32)
    perm[src, j] = 1.0
    return (jnp.asarray(cos.astype(np.float32)), jnp.asarray(sin.astype(np.float32)),
            jnp.asarray(cos_all), jnp.asarray(sin_all), jnp.asarray(perm))


def _gdn_factors(log_a, beta, ncc):
    t = log_a.shape[0]
    nc = t // _C
    la = log_a.reshape(nc, _C, _N_DIR, GDN_HEADS).transpose(2, 3, 0, 1)
    be = beta.reshape(nc, _C, _N_DIR, GDN_HEADS).transpose(2, 3, 0, 1)
    scan_order = lambda a: jnp.stack([a[0], jnp.concatenate([jnp.flip(a[1][:, :ncc], axis=1),
                                                              jnp.flip(a[1][:, ncc:], axis=1)], axis=1)])
    la, be = scan_order(la), scan_order(be)
    rev = jnp.asarray(np.array([False, True])[:, None, None, None])
    run = jnp.cumsum(la, axis=3)
    gam = jnp.where(rev, jnp.sum(la, axis=3, keepdims=True) - run + la, run)
    idx = np.arange(_C)
    incl = jnp.asarray(np.stack([idx[:, None] >= idx[None, :], idx[:, None] <= idx[None, :]])[:, None, None])
    strict = jnp.asarray(np.stack([idx[:, None] > idx[None, :], idx[:, None] < idx[None, :]])[:, None, None])
    dec = jnp.exp(jnp.where(incl, gam[..., :, None] - gam[..., None, :], -jnp.inf))
    m1 = jnp.where(strict, be[..., :, None] * dec, 0.0)
    e = jnp.exp(gam)
    g_last = jnp.where(rev, gam[..., :1], gam[..., -1:])
    lanes = lambda a: jnp.broadcast_to(a.reshape(_N_DIR, GDN_HEADS, t, 1), (_N_DIR, GDN_HEADS, t, HEAD_DIM))
    gl = jnp.broadcast_to(jnp.exp(g_last)[..., None], (_N_DIR, GDN_HEADS, nc, 1, HEAD_DIM))
    return lanes(be), lanes(be * e), lanes(e), lanes(jnp.exp(g_last - gam)), m1, dec, gl


def local_loss(x, wz, wb, ws, c, ctx, target):
    return channel_mixing(token_mixing(x, wz, wb, ws, c, ctx), wz, wb, ws, target)


def token_mixing(x, wz, wb, ws, c, ctx):
    n, cl = x.shape[0], ctx.shape[0]
    cos_q, sin_q, cos_k, sin_k, perm = _rope_tables(n, cl)

    sc_in = jnp.concatenate([jax.nn.silu(c), jax.nn.silu(ws["c_ctx"])[None, :], jnp.zeros((14, D_MODEL), F32)], axis=0)
    mod = pmm(sc_in, wb["w_mod"], wz["w_mod"], "mm_mod") + ws["b_mod"]
    sh1, sc1, g1, sh2, sc2, g2 = [mod[0:1, i * D_MODEL:(i + 1) * D_MODEL] for i in range(6)]
    csh1, csc1 = mod[1:2, 0:D_MODEL], mod[1:2, D_MODEL:2 * D_MODEL]

    def norm_mod(a, sh, sc):
        return (_rms(a) * (1.0 + sc) + sh,)

    (hx,) = rowop(norm_mod, "normmod_x", (x,), (sh1, sc1))
    (hc,) = rowop(norm_mod, "normmod_c", (ctx,), (csh1, csc1))
    h_all = jnp.concatenate([hc, hx], axis=0)
    p_main = pmm_t(h_all, wb["w_in_main"], wz["w_in_main"], "mm_in")
    p_small = pmm_t(h_all, wb["w_in_small"], wz["w_in_small"], "mm_ins")
    ak, av, qkv, aq, z, gate = jnp.split(p_main, [KV_W, 2 * KV_W, SMALL_AT, SMALL_AT + Q_W, SMALL_AT + Q_W + GDN_W],
                                         axis=1)
    db, da = p_small[:, :2 * GDN_HEADS], p_small[:, 2 * GDN_HEADS:4 * GDN_HEADS]

    def qk_prep(nh):
        def fn(a, w, cos, sin, pm):
            outs = []
            for ah in _heads(a, nh):
                y = _rms(ah) * w
                outs.append(y * cos + _pdot(y, pm) * sin)
            return (jnp.concatenate(outs, axis=1),)
        return fn

    (q_x,) = rowop(qk_prep(ATTN_HEADS), "q_prep", (aq[cl:],), (ws["q_norm_w"],), (cos_q, sin_q), (perm,))
    (k_all,) = rowop(qk_prep(ATTN_KV_HEADS), "k_prep", (ak,), (ws["k_norm_w"],), (cos_k, sin_k), (perm,))
    attn_x = attention(q_x, k_all, av)

    cw = ws["conv_qkv_w"]
    normed = jnp.asarray(np.repeat([1.0, 1.0, 0.0], GDN_W)[None, :], F32)
    scale = jnp.asarray(np.repeat([HEAD_DIM ** -0.5, 1.0, 1.0], GDN_W)[None, :], F32)

    def gdn_prep(a, w0, w1, w2, nf, sc):
        s = jax.nn.silu(_conv3(a, w0, w1, w2, (0, cl)))
        inv = lax.rsqrt(jnp.sum(s * s, axis=-1, keepdims=True) + NORM_EPS)
        return (s * jnp.where(nf > 0.0, inv * sc, 1.0),)

    (qkvn,) = colop(gdn_prep, "gdn_prep", (qkv, cw[0:1], cw[1:2], cw[2:3], normed, scale),
                    [(i, 0) for i in range(6)], 2, 3 * GDN_HEADS)
    beta = jax.nn.sigmoid(db).reshape(-1, 2, GDN_HEADS)
    log_a = -jnp.exp(ws["a_log"])[None] * jax.nn.softplus(da.reshape(-1, 2, GDN_HEADS) + ws["dt_bias"][None])
    o_fwd, o_rev = gdn_scan(cl // _C, qkvn, _gdn_factors(log_a, beta, cl // _C))
    o_x = o_fwd + o_rev

    def gdn_out(o, zz, w):
        outs = [_rms(oh) * w * jax.nn.silu(zh) for oh, zh in zip(_heads(o, GDN_HEADS), _heads(zz, GDN_HEADS))]
        return (jnp.concatenate(outs, axis=1),)

    (gdn_x,) = rowop(gdn_out, "gdn_out", (o_x, z[cl:]), (ws["gdn_norm_w"],))
    return dict(x=x, attn_x=attn_x, gdn_x=gdn_x, gate=gate[cl:], g1=g1, sh2=sh2, sc2=sc2, g2=g2)


def channel_mixing(mixed, wz, wb, ws, target):
    x, attn_x, gdn_x, gate = mixed["x"], mixed["attn_x"], mixed["gdn_x"], mixed["gate"]
    g1, sh2, sc2, g2 = mixed["g1"], mixed["sh2"], mixed["sc2"], mixed["g2"]
    pa = pmm(attn_x, wb["w_pa"], wz["w_pa"], "mm_pa")
    pd = pmm(gdn_x, wb["w_pd"], wz["w_pd"], "mm_pd")

    def merge(a, d, g):
        return (jax.nn.sigmoid(g[:, :D_MODEL]) * a + jax.nn.sigmoid(g[:, D_MODEL:]) * d,)

    (y,) = rowop(merge, "merge", (pa, pd, gate))
    mo = pmm(y, wb["w_out"], wz["w_out"], "mm_out")

    def res_norm_mod(xx, m, g, sh, sc):
        x1 = xx + g * m
        return x1, _rms(x1) * (1.0 + sc) + sh

    x1, h2 = rowop(res_norm_mod, "res1", (x, mo), (g1, sh2, sc2))
    up = pmm(h2, wb["w_up"], wz["w_up"], "mm_up")
    fw = ws["ffn_conv_w"]

    def ffn_act(ug, uv, w0g, w0v, w1g, w1v, w2g, w2v, bg, bv):
        g = _conv3(ug, w0g, w1g, w2g, (0,)) + bg
        v = _conv3(uv, w0v, w1v, w2v, (0,)) + bv
        return (jax.nn.silu(g) * v,)

    half = D_FF // HEAD_DIM
    (act,) = colop(ffn_act, "ffn_act", (up, fw[0:1], fw[1:2], fw[2:3], ws["ffn_conv_b"]),
                   [(i, off) for i in range(5) for off in (0, half)], 0, half)
    dn = pmm(act, wb["w_down"], wz["w_down"], "mm_down")

    def head(xx, m, g, w, tgt):
        yy = _rms(xx + g * m) * w
        err = (yy - tgt) ** 2
        return (jnp.broadcast_to(0.5 * jnp.mean(err, axis=-1, keepdims=True), (xx.shape[0], HEAD_DIM)),)

    (row_loss,) = rowop(head, "head", (x1, dn), (g2, ws["final_norm_w"][None, :]), (target,))
    return jnp.sum(row_loss[:, 0])


_HBM = pl.BlockSpec(memory_space=pltpu.HBM)


def _chip_peers():
    x, y = lax.axis_index("x"), lax.axis_index("y")
    return [(1 - x, y), (x, 1 - y), (1 - x, 1 - y)]


_SPLIT_COLS = ("w_in",)


def _half_of(view, nm, idx, lead=0):
    r, cdim = view.shape[-2:]
    pre = (slice(None),) * lead
    if nm in _SPLIT_COLS:
        return view.at[pre + (slice(None), pl.ds(pl.multiple_of(idx * (cdim // 2), 128), cdim // 2))]
    return view.at[pre + (pl.ds(pl.multiple_of(idx * (r // 2), 16), r // 2), slice(None))]


def _remote(src, dst, send_sem, recv_sem, dev):
    return pltpu.make_async_remote_copy(src_ref=src, dst_ref=dst, send_sem=send_sem, recv_sem=recv_sem,
                                        device_id=dev, device_id_type=MESH)


def _hbm_call(body, name, ins, out_shape, n_sems, in_place=False):
    names = tuple(ins)
    return dict(zip(names, pl.pallas_call(
        body, name=name, in_specs=[_HBM] * len(names), out_specs=[_HBM] * len(names),
        out_shape=[out_shape(nm, ins[nm]) for nm in names],
        scratch_shapes=[pltpu.SemaphoreType.DMA((k,)) for k in n_sems],
        input_output_aliases={i: i for i in range(len(names))} if in_place else {},
    )(*[ins[nm] for nm in names])))


def all_gather_chips(shards):
    names = tuple(shards)
    n = len(names)

    def body(*refs):
        ins, outs = dict(zip(names, refs[:n])), dict(zip(names, refs[n:2 * n]))
        ici_send, ici_recv, d2d_send, d2d_recv, own_send, own_recv = refs[2 * n:]
        x, y, c = lax.axis_index("x"), lax.axis_index("y"), lax.axis_index("c")
        me, sib = 2 * x + y, (x, y, 1 - c)
        own = [_remote(ins[nm], outs[nm].at[me], own_send.at[i], own_recv.at[i], sib) for i, nm in enumerate(names)]
        for cp in own:
            cp.start()
        sends = []
        for k, (px, py) in enumerate(_chip_peers()):
            for i, nm in enumerate(names):
                cp = _remote(_half_of(ins[nm], nm, c), _half_of(outs[nm].at[me], nm, c), ici_send.at[k * n + i],
                             ici_recv.at[k * n + i], (px, py, c))
                cp.start()
                sends.append(cp)
        for k, (px, py) in enumerate(_chip_peers()):
            for i, nm in enumerate(names):
                landed = _half_of(outs[nm].at[2 * px + py], nm, c)
                _remote(landed, landed, ici_send.at[k * n + i], ici_recv.at[k * n + i], (px, py, c)).wait_recv()
                fw = _remote(landed, landed, d2d_send.at[k * n + i], d2d_recv.at[k * n + i], sib)
                fw.start()
                sends.append(fw)
        for k, (px, py) in enumerate(_chip_peers()):
            for i, nm in enumerate(names):
                other = _half_of(outs[nm].at[2 * px + py], nm, 1 - c)
                _remote(other, other, d2d_send.at[k * n + i], d2d_recv.at[k * n + i], sib).wait_recv()
        for cp in sends:
            cp.wait_send()
        for cp in own:
            cp.wait()

    return _hbm_call(body, "ag_weights", shards, lambda nm, a: jax.ShapeDtypeStruct((N_CHIPS,) + a.shape, a.dtype),
                     (3 * n, 3 * n, 3 * n, 3 * n, n, n))


_SEM = pl.BlockSpec(memory_space=pltpu.SEMAPHORE)


def push_start(name, arrays, land_shapes, copies, n_copies):
    names = tuple(arrays)
    n = len(names)

    def body(*refs):
        send_sems, recv_sems, token = refs[2 * n], refs[2 * n + 1], refs[-1]
        for j, (src, dst, dev) in enumerate(copies(refs[:n], refs[n:2 * n])):
            _remote(src, dst, send_sems.at[j], recv_sems.at[j], dev).start()
        token[...] = jnp.zeros_like(token)

    hbm = lambda a: pltpu.with_memory_space_constraint(a, pltpu.HBM)
    lands = [lax.empty(land_shapes[nm], arrays[nm].dtype) for nm in names]
    res = pl.pallas_call(
        body, name=name,
        out_shape=(pltpu.SemaphoreType.DMA((n_copies,)), pltpu.SemaphoreType.DMA((n_copies,)),
                   *[pltpu.HBM(arrays[nm].shape, arrays[nm].dtype) for nm in names],
                   *[pltpu.HBM(a.shape, a.dtype) for a in lands], jax.ShapeDtypeStruct((8, 128), F32)),
        in_specs=[_HBM] * (2 * n),
        out_specs=(_SEM, _SEM, *[_HBM] * (2 * n), pl.BlockSpec(memory_space=pltpu.VMEM)),
        input_output_aliases={i: 2 + i for i in range(2 * n)},
        compiler_params=pltpu.CompilerParams(has_side_effects=pltpu.SideEffectType.DATAFLOW_SIDE_EFFECTING),
    )(*[hbm(arrays[nm]) for nm in names], *[hbm(a) for a in lands])
    return names, res[0], res[1], res[2:2 + n], res[2 + n:2 + 2 * n], res[-1]


def push_wait(name, started, copies, after):
    names, send_sems, recv_sems, srcs, lands, _ = started
    n = len(names)

    def body(*refs):
        send_ref, recv_ref = refs[2 * n], refs[2 * n + 1]
        for j, (src, dst, dev) in enumerate(copies(refs[:n], refs[n:2 * n])):
            cp = _remote(src, dst, send_ref.at[j], recv_ref.at[j], dev)
            cp.wait_send()
            cp.wait_recv()

    res = pl.pallas_call(
        body, name=name,
        out_shape=(*[pltpu.HBM(a.shape, a.dtype) for a in srcs], *[pltpu.HBM(a.shape, a.dtype) for a in lands]),
        in_specs=[_HBM] * (2 * n) + [_SEM, _SEM, pl.BlockSpec(memory_space=pl.ANY)],
        out_specs=tuple([_HBM] * (2 * n)),
        input_output_aliases={i: i for i in range(2 * n)},
        compiler_params=pltpu.CompilerParams(has_side_effects=pltpu.SideEffectType.DATAFLOW_SIDE_EFFECTING),
    )(*srcs, *lands, send_sems, recv_sems, after)
    return dict(zip(names, res[n:]))


def _gather_copies(srcs, lands):
    x, y, c = lax.axis_index("x"), lax.axis_index("y"), lax.axis_index("c")
    devs = [(px, py, c) for px, py in _chip_peers()] + [(x, y, 1 - c)]
    return [(src, land.at[2 * x + y], dev) for src, land in zip(srcs, lands) for dev in devs]


def _scatter_copies(srcs, lands):
    c = lax.axis_index("c")
    return [(src.at[2 * px + py], land.at[k], (px, py, c))
            for src, land in zip(srcs, lands) for k, (px, py) in enumerate(_chip_peers())]


def sibling_halves(blocks, name):
    names = tuple(blocks)

    def body(*refs):
        n = len(names)
        ins, outs = dict(zip(names, refs[:n])), dict(zip(names, refs[n:2 * n]))
        send_sems, recv_sems = refs[2 * n:]
        x, y, c = lax.axis_index("x"), lax.axis_index("y"), lax.axis_index("c")
        cps = [_remote(_half_of(ins[nm], nm, 1 - c, lead=1), outs[nm], send_sems.at[i], recv_sems.at[i], (x, y, 1 - c))
               for i, nm in enumerate(names)]
        for cp in cps:
            cp.start()
        for cp in cps:
            cp.wait()

    def half_shape(nm, a):
        r, cdim = a.shape[-2:]
        return jax.ShapeDtypeStruct((N_CHIPS, r, cdim // 2) if nm in _SPLIT_COLS else (N_CHIPS, r // 2, cdim), a.dtype)

    return _hbm_call(body, name, blocks, half_shape, (len(names), len(names)))


def scatter_halves(blocks):
    names = tuple(blocks)
    n = len(names)

    def body(*refs):
        ins, outs = dict(zip(names, refs[:n])), dict(zip(names, refs[n:2 * n]))
        send_sems, recv_sems = refs[2 * n:]
        c = lax.axis_index("c")
        cps = [_remote(ins[nm].at[2 * px + py], outs[nm].at[k], send_sems.at[k * n + i], recv_sems.at[k * n + i],
                       (px, py, c))
               for k, (px, py) in enumerate(_chip_peers()) for i, nm in enumerate(names)]
        for cp in cps:
            cp.start()
        for cp in cps:
            cp.wait_recv()
        for cp in cps:
            cp.wait_send()

    return _hbm_call(body, "rs_grads", blocks, lambda nm, a: jax.ShapeDtypeStruct((3,) + a.shape[1:], a.dtype),
                     (3 * n, 3 * n))


def sibling_assemble(arrays, name):
    names = tuple(arrays)

    def body(*refs):
        n = len(names)
        ins, outs = dict(zip(names, refs[:n])), dict(zip(names, refs[n:2 * n]))
        send_sems, recv_sems = refs[2 * n:]
        x, y, c = lax.axis_index("x"), lax.axis_index("y"), lax.axis_index("c")
        cps = [_remote(_half_of(ins[nm], nm, c), _half_of(outs[nm], nm, c), send_sems.at[i], recv_sems.at[i],
                       (x, y, 1 - c)) for i, nm in enumerate(names)]
        for cp in cps:
            cp.start()
        for i, nm in enumerate(names):
            other = _half_of(outs[nm], nm, 1 - c)
            _remote(other, other, send_sems.at[i], recv_sems.at[i], (x, y, 1 - c)).wait_recv()
        for cp in cps:
            cp.wait_send()

    return _hbm_call(body, name, arrays, lambda nm, a: jax.ShapeDtypeStruct(a.shape, a.dtype),
                     (len(names), len(names)), in_place=True)


def all_reduce_small(v):
    def body(v_ref, tot_ref, gath_ref, send_sems, recv_sems):
        x, y, c = lax.axis_index("x"), lax.axis_index("y"), lax.axis_index("c")
        me = 4 * x + 2 * y + c
        gath_ref[me] = v_ref[...]

        def peer(k):
            m = k + 1
            return (x ^ (m >> 2 & 1), y ^ (m >> 1 & 1), c ^ (m & 1))

        sends = [pltpu.make_async_remote_copy(src_ref=v_ref, dst_ref=gath_ref.at[me], send_sem=send_sems.at[k],
                                              recv_sem=recv_sems.at[k], device_id=peer(k), device_id_type=MESH)
                 for k in range(N_DEV - 1)]
        for cp in sends:
            cp.start()
        for k in range(N_DEV - 1):
            px, py, pc = peer(k)
            pltpu.make_async_remote_copy(src_ref=v_ref, dst_ref=gath_ref.at[4 * px + 2 * py + pc],
                                         send_sem=send_sems.at[k], recv_sem=recv_sems.at[k], device_id=peer(k),
                                         device_id_type=MESH).wait_recv()
        for cp in sends:
            cp.wait_send()
        acc = gath_ref[0]
        for d in range(1, N_DEV):
            acc = acc + gath_ref[d]
        tot_ref[...] = acc

    vm = pl.BlockSpec(memory_space=pltpu.VMEM)
    return pl.pallas_call(
        body, name="ar_small", in_specs=[vm], out_specs=[vm, vm],
        out_shape=(jax.ShapeDtypeStruct(v.shape, v.dtype), jax.ShapeDtypeStruct((N_DEV,) + v.shape, v.dtype)),
        scratch_shapes=[pltpu.SemaphoreType.DMA((N_DEV - 1,)), pltpu.SemaphoreType.DMA((N_DEV - 1,))],
    )(v)[0]


def _elementwise(fn, name, ins, n_out, out_dtype=F32):
    r, cdim = ins[0].shape
    tr = _pick(r, tuple(p for p in (488, 256, 128, 104, 64, 32, 16, 8) if p * cdim * 4 <= 2 * 1024 * 1024))
    spec = pl.BlockSpec((tr, cdim), lambda i: (i, 0))

    def body(*refs):
        res = fn(*[ref[...] for ref in refs[:len(ins)]])
        for o_ref, v in zip(refs[len(ins):], res):
            o_ref[...] = v

    return pl.pallas_call(
        body, name=name, grid=(r // tr,), in_specs=[spec] * len(ins), out_specs=[spec] * n_out,
        out_shape=tuple(jax.ShapeDtypeStruct((r, cdim), out_dtype) for _ in range(n_out)),
        compiler_params=pltpu.CompilerParams(dimension_semantics=("parallel",), vmem_limit_bytes=VMEM_LIMIT),
    )(*ins)


def _half_block_specs(nm, shard_shape):
    r, cdim = shard_shape
    if nm in _SPLIT_COLS:
        return (None, r, cdim // 2), (lambda j, c: (j, 0, c))
    return (None, r // 2, cdim), (lambda j, c: (j, c, 0))


def _presum(nm, sel, g32, a):
    blk, at = _half_block_specs(nm, g32.shape[1:])

    def body(s_ref, g_ref, a_ref, o_ref):
        del s_ref
        o_ref[...] = (g_ref[...] + a_ref[...]).astype(BF16)

    return pl.pallas_call(
        body, name="rs_presum_" + nm,
        grid_spec=pltpu.PrefetchScalarGridSpec(
            num_scalar_prefetch=1, grid=(N_CHIPS,),
            in_specs=[pl.BlockSpec(blk, lambda j, s: at(j, s[0])), pl.BlockSpec(blk, lambda j, s: (j, 0, 0))],
            out_specs=pl.BlockSpec(blk, lambda j, s: (j, 0, 0))),
        out_shape=jax.ShapeDtypeStruct(a.shape, BF16),
        compiler_params=pltpu.CompilerParams(dimension_semantics=("parallel",), vmem_limit_bytes=VMEM_LIMIT),
    )(sel, g32, a)


def _finalsum(nm, sel, g32, a, got):
    blk, at = _half_block_specs(nm, g32.shape[1:])

    def body(s_ref, g_ref, a_ref, r_ref, o_ref):
        del s_ref
        acc = g_ref[...] + a_ref[...]
        for k in range(3):
            acc = acc + r_ref[k].astype(F32)
        o_ref[...] = acc

    return pl.pallas_call(
        body, name="rs_final_" + nm,
        grid_spec=pltpu.PrefetchScalarGridSpec(
            num_scalar_prefetch=1, grid=(1,),
            in_specs=[pl.BlockSpec(blk, lambda i, s: at(s[1], s[0])), pl.BlockSpec(blk, lambda i, s: (s[1], 0, 0)),
                      pl.BlockSpec(got.shape, lambda i, s: (0, 0, 0))],
            out_specs=pl.BlockSpec(blk[1:], lambda i, s: at(0, s[0])[1:])),
        out_shape=jax.ShapeDtypeStruct(g32.shape[1:], F32),
        compiler_params=pltpu.CompilerParams(dimension_semantics=("arbitrary",), vmem_limit_bytes=VMEM_LIMIT),
    )(sel, g32, a, got)


def _adamw(w, g, m, v, name):
    shape = w.shape
    to2 = lambda a: a.reshape(-1, shape[-1])

    def fn(w_, g_, m_, v_):
        m_new = ADAM_B1 * m_ + (1.0 - ADAM_B1) * g_
        v_new = ADAM_B2 * v_ + (1.0 - ADAM_B2) * (g_ * g_)
        m_hat = m_new / (1.0 - ADAM_B1 ** ADAM_STEP)
        v_hat = v_new / (1.0 - ADAM_B2 ** ADAM_STEP)
        delta = -ADAM_LR * (m_hat / (jnp.sqrt(v_hat) + ADAM_EPS) + ADAM_WD * w_)
        return g_, delta, m_new, v_new

    outs = _elementwise(fn, name, [to2(a) for a in (w, g, m, v)], 4)
    return tuple(o.reshape(shape) for o in outs)


_BIG = ("w_mod", "w_in", "w_pa", "w_pd", "w_out", "w_up", "w_down")
_EARLY = ("w_mod", "w_in")
_LATE = ("w_pa", "w_pd", "w_out", "w_up", "w_down")
_COL_SHARDED = ("w_mod", "w_up")
_FULL_SHAPE = {"w_mod": (D_MODEL, MOD_W), "w_in": (IN_COLS, D_MODEL), "w_pa": (Q_W, D_MODEL), "w_pd": (GDN_W, D_MODEL),
               "w_out": (D_MODEL, D_MODEL), "w_up": (D_MODEL, 2 * D_FF), "w_down": (D_FF, D_MODEL)}


def _shard_shape(name):
    r, cdim = _FULL_SHAPE[name]
    return (r, cdim // N_CHIPS) if name in _COL_SHARDED else (r // N_CHIPS, cdim)


_CONV_ELEMS = 2 * (3 * CONV_W // N_CHIPS + 3 * 2 * D_FF // N_CHIPS)
_CONV_ROWS = 32


def _blocks_of_full(name, full):
    r, cdim = _FULL_SHAPE[name]
    if name in _COL_SHARDED:
        return full.reshape(r, N_CHIPS, cdim // N_CHIPS).transpose(1, 0, 2)
    return full.reshape(N_CHIPS, r // N_CHIPS, cdim)


def _full_of_blocks(name, blocks):
    r, cdim = _FULL_SHAPE[name]
    if name in _COL_SHARDED:
        return blocks.transpose(1, 0, 2).reshape(r, cdim)
    return blocks.reshape(r, cdim)


def _w_in_regroup(w_in_t):
    main = jnp.concatenate([w_in_t[:SMALL_AT], w_in_t[SMALL_AT + 4 * GDN_HEADS:]], axis=0)
    small = jnp.pad(w_in_t[SMALL_AT:SMALL_AT + 4 * GDN_HEADS], ((0, HEAD_DIM - 4 * GDN_HEADS), (0, 0)))
    return main, small


def _w_in_ungroup(main, small):
    return jnp.concatenate([main[:SMALL_AT], small[:4 * GDN_HEADS], main[SMALL_AT:]], axis=0)


_SMALL = ("c_ctx", "b_mod", "q_norm_w", "k_norm_w", "conv_qkv_w", "a_log", "dt_bias", "gdn_norm_w", "ffn_conv_w",
          "ffn_conv_b", "final_norm_w")


def _pack_small(tree, rows):
    flat = jnp.concatenate([tree[nm].reshape(-1) for nm in _SMALL])
    return jnp.pad(flat, (0, rows * 128 - flat.shape[0])).reshape(rows, 128)


def _unpack_small(packed, like):
    flat, out, off = packed.reshape(-1), {}, 0
    for nm in _SMALL:
        size = int(np.prod(like[nm].shape))
        out[nm] = flat[off:off + size].reshape(like[nm].shape)
        off += size
    return out


def kernel(x, c, ctx, c_ctx, w_mod, b_mod, w_in, q_norm_w, k_norm_w, conv_qkv_w, a_log, dt_bias, gdn_norm_w, w_pa, w_pd, w_out, w_up, ffn_conv_w, ffn_conv_b, w_down, final_norm_w, loss_target, m_c_ctx, m_w_mod, m_b_mod, m_w_in, m_q_norm_w, m_k_norm_w, m_conv_qkv_w, m_a_log, m_dt_bias, m_gdn_norm_w, m_w_pa, m_w_pd, m_w_out, m_w_up, m_ffn_conv_w, m_ffn_conv_b, m_w_down, m_final_norm_w, v_c_ctx, v_w_mod, v_b_mod, v_w_in, v_q_norm_w, v_k_norm_w, v_conv_qkv_w, v_a_log, v_dt_bias, v_gdn_norm_w, v_w_pa, v_w_pd, v_w_out, v_w_up, v_ffn_conv_w, v_ffn_conv_b, v_w_down, v_final_norm_w):
    names = ("c_ctx", "w_mod", "b_mod", "w_in", "q_norm_w", "k_norm_w", "conv_qkv_w", "a_log", "dt_bias", "gdn_norm_w",
             "w_pa", "w_pd", "w_out", "w_up", "ffn_conv_w", "ffn_conv_b", "w_down", "final_norm_w")
    w_sh = dict(c_ctx=c_ctx, w_mod=w_mod, b_mod=b_mod, w_in=w_in, q_norm_w=q_norm_w, k_norm_w=k_norm_w,
                conv_qkv_w=conv_qkv_w, a_log=a_log, dt_bias=dt_bias, gdn_norm_w=gdn_norm_w, w_pa=w_pa, w_pd=w_pd,
                w_out=w_out, w_up=w_up, ffn_conv_w=ffn_conv_w, ffn_conv_b=ffn_conv_b, w_down=w_down,
                final_norm_w=final_norm_w)
    m_sh = dict(c_ctx=m_c_ctx, w_mod=m_w_mod, b_mod=m_b_mod, w_in=m_w_in, q_norm_w=m_q_norm_w, k_norm_w=m_k_norm_w,
                conv_qkv_w=m_conv_qkv_w, a_log=m_a_log, dt_bias=m_dt_bias, gdn_norm_w=m_gdn_norm_w, w_pa=m_w_pa,
                w_pd=m_w_pd, w_out=m_w_out, w_up=m_w_up, ffn_conv_w=m_ffn_conv_w, ffn_conv_b=m_ffn_conv_b,
                w_down=m_w_down, final_norm_w=m_final_norm_w)
    v_sh = dict(c_ctx=v_c_ctx, w_mod=v_w_mod, b_mod=v_b_mod, w_in=v_w_in, q_norm_w=v_q_norm_w, k_norm_w=v_k_norm_w,
                conv_qkv_w=v_conv_qkv_w, a_log=v_a_log, dt_bias=v_dt_bias, gdn_norm_w=v_gdn_norm_w, w_pa=v_w_pa,
                w_pd=v_w_pd, w_out=v_w_out, w_up=v_w_up, ffn_conv_w=v_ffn_conv_w, ffn_conv_b=v_ffn_conv_b,
                w_down=v_w_down, final_norm_w=v_final_norm_w)
    chip = 2 * lax.axis_index("x") + lax.axis_index("y")

    conv_bits = jnp.concatenate([lax.bitcast_convert_type(w_sh[nm][0], BF16).reshape(-1)
                                 for nm in ("conv_qkv_w", "ffn_conv_w")])
    shards = {nm: w_sh[nm][0].astype(BF16).T if nm == "w_in" else w_sh[nm][0].astype(BF16) for nm in _BIG}
    shards["conv"] = jnp.pad(conv_bits, (0, _CONV_ROWS * D_MODEL - _CONV_ELEMS)).reshape(_CONV_ROWS, D_MODEL)
    gathered = all_gather_chips({nm: shards[nm] for nm in _EARLY + ("conv",)})
    gathered, late_shards = lax.optimization_barrier((gathered, {nm: shards[nm] for nm in _LATE}))
    started = push_start("ag_late_start", late_shards, {nm: (N_CHIPS,) + a.shape for nm, a in late_shards.items()},
                         _gather_copies, 4 * len(_LATE))
    c = c + started[-1][0:1, 0:1]

    wb = {nm: _full_of_blocks(nm, gathered[nm]) for nm in _EARLY}
    wb["w_in_main"], wb["w_in_small"] = _w_in_regroup(wb.pop("w_in"))
    conv_all = gathered["conv"].reshape(N_CHIPS, -1)[:, :_CONV_ELEMS]
    n_cq = 2 * 3 * CONV_W // N_CHIPS
    unbits = lambda a, w: lax.bitcast_convert_type(a.reshape(N_CHIPS, 3, w // N_CHIPS, 2), F32).transpose(1, 0, 2).reshape(3, w)
    ws = dict(c_ctx=c_ctx, b_mod=b_mod, q_norm_w=q_norm_w, k_norm_w=k_norm_w, a_log=a_log[0], dt_bias=dt_bias[0],
              gdn_norm_w=gdn_norm_w, ffn_conv_b=ffn_conv_b, final_norm_w=final_norm_w,
              conv_qkv_w=unbits(conv_all[:, :n_cq], CONV_W), ffn_conv_w=unbits(conv_all[:, n_cq:], 2 * D_FF))
    wz = {nm: jnp.zeros(a.shape, F32) for nm, a in wb.items()}
    wz.update({nm: jnp.zeros(_FULL_SHAPE[nm], F32) for nm in _LATE})

    mixed, vjp_mix = jax.vjp(lambda x_, wz_, ws_: token_mixing(x_, wz_, wb, ws_, c, ctx[0]), x[0], wz, ws)
    got = push_wait("ag_late_wait", started, _gather_copies, mixed["gdn_x"])
    wb_late = {nm: _full_of_blocks(nm, got[nm]) for nm in _LATE}
    loss_local, vjp_chan = jax.vjp(
        lambda mixed_, wz_, ws_: channel_mixing(mixed_, wz_, wb_late, ws_, loss_target[0]), mixed, wz, ws)
    d_mixed, gz_chan, gs_chan = vjp_chan(jnp.ones((), F32))

    sel = jnp.stack([lax.axis_index("c"), chip]).astype(jnp.int32)
    g32_late = {nm: _blocks_of_full(nm, gz_chan[nm]) for nm in _LATE}
    theirs_late = sibling_halves(g32_late, "rs_sibling_late")
    sums_late = {nm: _presum(nm, sel, g32_late[nm], theirs_late[nm]) for nm in _LATE}
    scattering = push_start("rs_late_start", sums_late, {nm: (3,) + a.shape[1:] for nm, a in sums_late.items()},
                            _scatter_copies, 3 * len(_LATE))
    d_mixed = {**d_mixed, "gdn_x": d_mixed["gdn_x"] + scattering[-1][0:1, 0:1]}
    gx, gz_mix, gs_mix = vjp_mix(d_mixed)
    gs = jax.tree.map(jnp.add, gs_mix, gs_chan)
    got_late = push_wait("rs_late_wait", scattering, _scatter_copies, gx)
    loss = lax.psum(loss_local, ("x", "y", "c"))

    gz_mix["w_in"] = _w_in_ungroup(gz_mix.pop("w_in_main"), gz_mix.pop("w_in_small"))
    g32 = {nm: _blocks_of_full(nm, gz_mix[nm]) for nm in _EARLY}
    theirs = sibling_halves(g32, "rs_sibling")
    sums = {nm: _presum(nm, sel, g32[nm], theirs[nm]) for nm in _EARLY}
    scattering = push_start("rs_early_start", sums, {nm: (3,) + a.shape[1:] for nm, a in sums.items()},
                            _scatter_copies, 3 * len(_EARLY))
    zero = scattering[-1][0:1, 0:1].astype(BF16)
    got_late = {nm: a + zero if nm in ("w_up", "w_down") else a for nm, a in got_late.items()}
    g_big = sibling_assemble({nm: _finalsum(nm, sel, g32_late[nm], theirs_late[nm], got_late[nm]) for nm in _LATE},
                             "rs_assemble_late")
    grads, deltas, new_m, new_v = {}, {}, {}, {}

    def adamw_big(nm):
        g = g_big[nm].T if nm == "w_in" else g_big[nm]
        grads[nm], deltas[nm], new_m[nm], new_v[nm] = (
            o[None] for o in _adamw(w_sh[nm][0], g, m_sh[nm][0], v_sh[nm][0], "adamw_" + nm))

    for nm in _LATE:
        adamw_big(nm)

    gs["a_log"], gs["dt_bias"] = gs["a_log"][None], gs["dt_bias"][None]
    like = {nm: gs[nm] for nm in _SMALL}
    small_rows = -(-sum(int(np.prod(like[nm].shape)) for nm in _SMALL) // 1024) * 8
    g_small = _unpack_small(all_reduce_small(_pack_small(gs, small_rows) + scattering[-1][0:1, 0:1]), like)
    for nm, width in (("conv_qkv_w", CONV_W), ("ffn_conv_w", 2 * D_FF)):
        g_small[nm] = lax.dynamic_slice_in_dim(g_small[nm], chip * (width // N_CHIPS), width // N_CHIPS, axis=1)[None]

    shard_like = {nm: w_sh[nm] for nm in _SMALL}
    rows_l = -(-sum(int(np.prod(shard_like[nm].shape)) for nm in _SMALL) // 1024) * 8
    g_l = _pack_small({nm: g_small[nm].reshape(w_sh[nm].shape) for nm in _SMALL}, rows_l)
    outs = _adamw(_pack_small(w_sh, rows_l), g_l, _pack_small(m_sh, rows_l), _pack_small(v_sh, rows_l), "adamw_small")
    for tree, packed in zip((grads, deltas, new_m, new_v), outs):
        tree.update(_unpack_small(packed, shard_like))

    done_meanwhile = deltas["w_up"][0, :8, :HEAD_DIM] + outs[1][:8, :]
    got = push_wait("rs_early_wait", scattering, _scatter_copies, done_meanwhile)
    g_big.update(sibling_assemble({nm: _finalsum(nm, sel, g32[nm], theirs[nm], got[nm]) for nm in _EARLY},
                                  "rs_assemble"))
    for nm in _EARLY:
        adamw_big(nm)

    return (loss, gx[None], *[grads[nm] for nm in names], *[deltas[nm] for nm in names],
            *[new_m[nm] for nm in names], *[new_v[nm] for nm in names])
```

```python
import functools
import math

import jax
import jax.numpy as jnp
import numpy as np
from jax import lax
from jax.experimental import pallas as pl
from jax.experimental.pallas import tpu as pltpu

F32 = jnp.float32
BF16 = jnp.bfloat16
HIGHEST = lax.Precision.HIGHEST
MESH = pl.DeviceIdType.MESH

D_MODEL = 1024
GRID_W = 64
ATTN_HEADS = 8
ATTN_KV_HEADS = 2
ATTN_GROUP = ATTN_HEADS // ATTN_KV_HEADS
HEAD_DIM = 128
ROPE_THETA = 10000.0
GDN_HEADS = 8
GDN_CHUNK = 64
D_FF = 2816
NORM_EPS = 1e-6
KV_W = ATTN_KV_HEADS * HEAD_DIM
Q_W = ATTN_HEADS * HEAD_DIM
GDN_W = GDN_HEADS * HEAD_DIM
CONV_W = 3 * GDN_W
MOD_W = 6 * D_MODEL
IN_COLS = 2 * KV_W + CONV_W + 4 * GDN_HEADS + Q_W + GDN_W + 2 * D_MODEL
IN_MAIN = IN_COLS - 4 * GDN_HEADS
SMALL_AT = 2 * KV_W + CONV_W
N_CHIPS = 4
N_DEV = 8

ADAM_LR = 0.001
ADAM_B1 = 0.9
ADAM_B2 = 0.999
ADAM_EPS = 1e-08
ADAM_WD = 0.01
ADAM_STEP = 10

VMEM_LIMIT = 48 * 1024 * 1024
MATMUL_VMEM_BUDGET = 40 * 1024 * 1024
MATMUL_STEP_BYTES = 1200 * 1024


def _pick(dim, prefs):
    for p in prefs:
        if p <= dim and dim % p == 0:
            return p
    return dim


_DIMS = {
    "nn": (((1,), (0,)), ((), ())),
    "nt": (((1,), (1,)), ((), ())),
    "tn": (((0,), (0,)), ((), ())),
}


def _matmul_plan(m, n, k, a_bytes, b_bytes):
    best = None
    for tm in (2304, 2048, 1152, 1024, 768, 512, 384, 256, 128, m):
        for tn in (2560, 1536, 1408, 1024, 768, 512, 256, 128, n):
            for tk in (3840, 2816, 2560, 2304, 2048, 1920, 1408, 1152, 1024, 768, 512, 256, 128, k):
                if tm > m or tn > n or tk > k or m % tm or n % tn or k % tk:
                    continue
                blocks = tm * tk * a_bytes + tk * tn * b_bytes + tm * tn * 4
                casts = (tm * tk * 2 if a_bytes > 2 else 0) + (tk * tn * 2 if b_bytes > 2 else 0) + tm * tn * 4
                if 2 * blocks + casts > MATMUL_VMEM_BUDGET:
                    continue
                nm, nn, nk = m // tm, n // tn, k // tk
                size_a, size_b = m * k * a_bytes, k * n * b_bytes
                for n_inner in (True, False):
                    if n_inner:
                        traffic = (size_a if nk == 1 else nn * size_a) + nm * size_b
                    else:
                        traffic = nn * size_a + (size_b if nk == 1 else nm * size_b)
                    cost = traffic + nm * nn * nk * MATMUL_STEP_BYTES + (nk - 1) * m * n * 4
                    if best is None or cost < best[0]:
                        best = (cost, tm, tn, tk, n_inner)
    return best[1:]


def _matmul(a, b, mode, name):
    if mode == "nn":
        (m, k), (_, n) = a.shape, b.shape
    elif mode == "nt":
        (m, k), (n, _) = a.shape, b.shape
    else:
        (k, m), (_, n) = a.shape, b.shape
    tm, tn, tk, n_inner = _matmul_plan(m, n, k, a.dtype.itemsize, b.dtype.itemsize)
    nk = k // tk
    ij = (lambda g0, g1: (g0, g1)) if n_inner else (lambda g0, g1: (g1, g0))
    if mode == "tn":
        a_spec = pl.BlockSpec((tk, tm), lambda g0, g1, l: (l, ij(g0, g1)[0]))
    else:
        a_spec = pl.BlockSpec((tm, tk), lambda g0, g1, l: (ij(g0, g1)[0], l))
    if mode == "nt":
        b_spec = pl.BlockSpec((tn, tk), lambda g0, g1, l: (ij(g0, g1)[1], l))
    else:
        b_spec = pl.BlockSpec((tk, tn), lambda g0, g1, l: (l, ij(g0, g1)[1]))
    dims = _DIMS[mode]

    def body(a_ref, b_ref, o_ref):
        part = lax.dot_general(a_ref[...].astype(BF16), b_ref[...].astype(BF16), dims, preferred_element_type=F32)
        if nk == 1:
            o_ref[...] = part
        else:
            l = pl.program_id(2)

            @pl.when(l == 0)
            def _():
                o_ref[...] = part

            @pl.when(l > 0)
            def _():
                o_ref[...] += part

    return pl.pallas_call(
        body,
        name=name,
        grid=(m // tm, n // tn, nk) if n_inner else (n // tn, m // tm, nk),
        in_specs=[a_spec, b_spec],
        out_specs=pl.BlockSpec((tm, tn), lambda g0, g1, l: ij(g0, g1)),
        out_shape=jax.ShapeDtypeStruct((m, n), F32),
        compiler_params=pltpu.CompilerParams(dimension_semantics=("parallel", "parallel", "arbitrary"),
                                             vmem_limit_bytes=VMEM_LIMIT),
    )(a, b)


@functools.partial(jax.custom_vjp, nondiff_argnums=(3,))
def pmm(a, w, wz, name):
    del wz
    return _matmul(a, w, "nn", name + "_f")


def _pmm_fwd(a, w, wz, name):
    del wz
    return _matmul(a, w, "nn", name + "_f"), (a, w)


def _pmm_bwd(name, res, g):
    a, w = res
    da = _matmul(g, w, "nt", name + "_da")
    if a.shape[0] < 128:
        pad = 128 - a.shape[0]
        at = jnp.pad(a.T, ((0, 0), (0, pad)))
        gp = jnp.pad(g, ((0, pad), (0, 0)))
        dw = _matmul(at, gp, "nn", name + "_dw")
    else:
        dw = _matmul(a, g, "tn", name + "_dw")
    return da, jnp.zeros_like(w), dw


pmm.defvjp(_pmm_fwd, _pmm_bwd)


@functools.partial(jax.custom_vjp, nondiff_argnums=(3,))
def pmm_t(a, wt, wtz, name):
    del wtz
    return _matmul(a, wt, "nt", name + "_f")


def _pmm_t_fwd(a, wt, wtz, name):
    del wtz
    return _matmul(a, wt, "nt", name + "_f"), (a, wt)


def _pmm_t_bwd(name, res, g):
    a, wt = res
    return _matmul(g, wt, "nn", name + "_da"), jnp.zeros_like(wt), _matmul(g, a, "tn", name + "_dw")


pmm_t.defvjp(_pmm_t_fwd, _pmm_t_bwd)


def rowop(fn, name, rows, bcs=(), crows=(), cbcs=(), tr=256):
    rows, bcs, crows, cbcs = tuple(rows), tuple(bcs), tuple(crows), tuple(cbcs)
    n_rows = rows[0].shape[0]
    tr = _pick(n_rows, (tr, 128, 64, 32, 16, 8))
    nr, nb, ncr, ncb = len(rows), len(bcs), len(crows), len(cbcs)
    n_in = nr + nb + ncr + ncb
    grid = (n_rows // tr,)

    def blk(arr):
        return jax.ShapeDtypeStruct((tr, arr.shape[1]), arr.dtype)

    def row_spec(arr):
        return pl.BlockSpec((tr, arr.shape[1]), lambda i: (i, 0))

    def bc_spec(arr):
        return pl.BlockSpec(arr.shape, lambda i: (0, 0))

    out_blk = jax.eval_shape(fn, *[blk(r) for r in rows], *bcs, *[blk(r) for r in crows], *cbcs)
    n_out = len(out_blk)
    out_shape = tuple(jax.ShapeDtypeStruct((n_rows, o.shape[1]), o.dtype) for o in out_blk)
    in_specs = ([row_spec(r) for r in rows] + [bc_spec(b) for b in bcs]
                + [row_spec(r) for r in crows] + [bc_spec(b) for b in cbcs])

    def order(vals):
        return vals

    def fwd_call(args):
        def body(*refs):
            vals = [r[...] for r in refs[:n_in]]
            res = fn(*order(vals))
            for o_ref, r in zip(refs[n_in:], res):
                o_ref[...] = r

        return pl.pallas_call(
            body, name=name + "_f", grid=grid, in_specs=in_specs,
            out_specs=[row_spec(o) for o in out_shape], out_shape=out_shape,
            compiler_params=pltpu.CompilerParams(dimension_semantics=("parallel",), vmem_limit_bytes=VMEM_LIMIT),
        )(*args)

    def bwd_call(args, cts):
        def body(*refs):
            vals = [r[...] for r in refs[:n_in]]
            ct_refs = refs[n_in:n_in + n_out]
            d_rows = refs[n_in + n_out:n_in + n_out + nr]
            d_bcs = refs[n_in + n_out + nr:]
            consts = vals[nr + nb:]
            _, vjp = jax.vjp(lambda *p: fn(*p, *consts), *vals[:nr + nb])
            grads = vjp(tuple(c[...] for c in ct_refs))
            for ref, g in zip(d_rows, grads[:nr]):
                ref[...] = g

            @pl.when(pl.program_id(0) == 0)
            def _():
                for ref in d_bcs:
                    ref[...] = jnp.zeros_like(ref)

            for ref, g in zip(d_bcs, grads[nr:]):
                ref[...] += g

        d_shape = tuple(jax.ShapeDtypeStruct(r.shape, r.dtype) for r in rows + bcs)
        return pl.pallas_call(
            body, name=name + "_b", grid=grid,
            in_specs=in_specs + [row_spec(o) for o in out_shape],
            out_specs=[row_spec(r) for r in rows] + [bc_spec(b) for b in bcs], out_shape=d_shape,
            compiler_params=pltpu.CompilerParams(dimension_semantics=("arbitrary",), vmem_limit_bytes=VMEM_LIMIT),
        )(*args, *cts)

    @jax.custom_vjp
    def op(diff, const):
        return fwd_call(diff + const)

    def op_fwd(diff, const):
        return fwd_call(diff + const), (diff, const)

    def op_bwd(res, cts):
        diff, const = res
        grads = bwd_call(diff + const, tuple(cts))
        return tuple(grads), tuple(jnp.zeros_like(c) for c in const)

    op.defvjp(op_fwd, op_bwd)
    return op(rows + bcs, crows + cbcs)


def colop(fn, name, arrays, uses, n_const, nblk, cw=128):
    arrays = tuple(arrays)
    n_diff = len(arrays) - n_const
    nd = sum(1 for u in uses if u[0] < n_diff)
    assert all(u[0] < n_diff for u in uses[:nd]) and all(u[0] >= n_diff for u in uses[nd:])

    def spec(u):
        return pl.BlockSpec((arrays[u[0]].shape[0], cw), lambda j, off=u[1]: (0, off + j))

    def out_spec(rows):
        return pl.BlockSpec((rows, cw), lambda j: (0, j))

    out_blk = jax.eval_shape(fn, *[jax.ShapeDtypeStruct((arrays[u[0]].shape[0], cw), arrays[u[0]].dtype)
                                   for u in uses])
    out_shape = tuple(jax.ShapeDtypeStruct((o.shape[0], nblk * cw), o.dtype) for o in out_blk)
    params = pltpu.CompilerParams(dimension_semantics=("parallel",), vmem_limit_bytes=VMEM_LIMIT)

    def fwd_call(arrs):
        def body(*refs):
            res = fn(*[r[...] for r in refs[:len(uses)]])
            for o_ref, r in zip(refs[len(uses):], res):
                o_ref[...] = r

        return pl.pallas_call(
            body, name=name + "_f", grid=(nblk,), in_specs=[spec(u) for u in uses],
            out_specs=[out_spec(o.shape[0]) for o in out_shape], out_shape=out_shape, compiler_params=params,
        )(*[arrs[u[0]] for u in uses])

    def bwd_call(arrs, cts):
        def body(*refs):
            vals = [r[...] for r in refs[:len(uses)]]
            ct_refs = refs[len(uses):len(uses) + len(out_shape)]
            _, vjp = jax.vjp(lambda *p: fn(*p, *vals[nd:]), *vals[:nd])
            for ref, g in zip(refs[len(uses) + len(out_shape):], vjp(tuple(c[...] for c in ct_refs))):
                ref[...] = g

        d_shape = tuple(jax.ShapeDtypeStruct((arrays[u[0]].shape[0], nblk * cw), F32) for u in uses[:nd])
        return pl.pallas_call(
            body, name=name + "_b", grid=(nblk,),
            in_specs=[spec(u) for u in uses] + [out_spec(o.shape[0]) for o in out_shape],
            out_specs=[out_spec(s.shape[0]) for s in d_shape], out_shape=d_shape, compiler_params=params,
        )(*[arrs[u[0]] for u in uses], *cts)

    @jax.custom_vjp
    def op(diff, const):
        return fwd_call(diff + const)

    def op_fwd(diff, const):
        return fwd_call(diff + const), (diff, const)

    def op_bwd(res, cts):
        diff, const = res
        d_uses = bwd_call(diff + const, tuple(cts))
        grads = []
        for i in range(n_diff):
            parts = sorted([(u[1], k) for k, u in enumerate(uses[:nd]) if u[0] == i])
            grads.append(d_uses[parts[0][1]] if len(parts) == 1
                         else jnp.concatenate([d_uses[k] for _, k in parts], axis=1))
        return tuple(grads), tuple(jnp.zeros_like(c) for c in const)

    op.defvjp(op_fwd, op_bwd)
    return op(arrays[:n_diff], arrays[n_diff:])


@functools.partial(jax.custom_vjp, nondiff_argnums=(1,))
def _roll_rows(x, k):
    return pltpu.roll(x, k % x.shape[0], 0)


def _roll_rows_fwd(x, k):
    return _roll_rows(x, k), None


def _roll_rows_bwd(k, _, g):
    return (_roll_rows(g, -k),)


_roll_rows.defvjp(_roll_rows_fwd, _roll_rows_bwd)


def _conv3(x, w0, w1, w2, starts):
    rows = lax.broadcasted_iota(jnp.int32, x.shape, 0)
    ends = tuple(s - 1 for s in starts[1:]) + (x.shape[0] - 1,)
    first = functools.reduce(jnp.logical_or, [rows == s for s in starts])
    last = functools.reduce(jnp.logical_or, [rows == e for e in ends])
    prev = jnp.where(first, 0.0, _roll_rows(x, 1))
    nxt = jnp.where(last, 0.0, _roll_rows(x, -1))
    return prev * w0 + x * w1 + nxt * w2


def _rms(x):
    return x * lax.rsqrt(jnp.mean(x * x, axis=-1, keepdims=True) + NORM_EPS)


def _heads(x, n):
    return [x[:, h * HEAD_DIM:(h + 1) * HEAD_DIM] for h in range(n)]


_NT = (((1,), (1,)), ((), ()))
_TN = (((0,), (0,)), ((), ()))
_TQ = 256


_N_SUB = 2


def _sub_rows(ref, i):
    rows = ref.shape[0] // _N_SUB
    return ref[i * rows:(i + 1) * rows, :].astype(BF16)


def _attn_probs(qs, k):
    s = _each(lambda q: lax.dot_general(q, k, _NT, preferred_element_type=F32) * (HEAD_DIM ** -0.5), qs)
    m = _each(lambda a: jnp.max(a, axis=-1, keepdims=True), s)
    e = _each(lambda a, b: jnp.exp(a - b), s, m)
    inv = _each(lambda a: 1.0 / jnp.sum(a, axis=-1, keepdims=True), e)
    return _each(lambda a, b: a * b, e, inv)


def _attn_fwd_call(q, k, v):
    n, t = q.shape[0], k.shape[0]
    tq = _pick(n, (_TQ, 128))

    def body(q_ref, k_ref, v_ref, o_ref):
        vb = v_ref[...].astype(BF16)
        ps = _attn_probs([_sub_rows(q_ref, i) for i in range(_N_SUB)], k_ref[...].astype(BF16))
        rows = tq // _N_SUB
        for i, p in enumerate(ps):
            o_ref[i * rows:(i + 1) * rows, :] = jnp.dot(p.astype(BF16), vb, preferred_element_type=F32)

    return pl.pallas_call(
        body, name="attn_f", grid=(ATTN_HEADS, n // tq),
        in_specs=[pl.BlockSpec((tq, HEAD_DIM), lambda h, i: (i, h)),
                  pl.BlockSpec((t, HEAD_DIM), lambda h, i: (0, h // ATTN_GROUP)),
                  pl.BlockSpec((t, HEAD_DIM), lambda h, i: (0, h // ATTN_GROUP))],
        out_specs=pl.BlockSpec((tq, HEAD_DIM), lambda h, i: (i, h)),
        out_shape=jax.ShapeDtypeStruct(q.shape, F32),
        compiler_params=pltpu.CompilerParams(dimension_semantics=("parallel", "parallel"),
                                             vmem_limit_bytes=VMEM_LIMIT),
    )(q, k, v)


def _attn_bwd_call(q, k, v, do):
    n, t = q.shape[0], k.shape[0]
    tq = _pick(n, (_TQ, 128))

    def body(q_ref, k_ref, v_ref, do_ref, dq_ref, dk_ref, dv_ref):
        @pl.when((pl.program_id(1) == 0) & (pl.program_id(2) == 0))
        def _():
            dk_ref[...] = jnp.zeros_like(dk_ref)
            dv_ref[...] = jnp.zeros_like(dv_ref)

        kb, vb = k_ref[...].astype(BF16), v_ref[...].astype(BF16)
        qs = [_sub_rows(q_ref, i) for i in range(_N_SUB)]
        dos = [_sub_rows(do_ref, i) for i in range(_N_SUB)]
        ps = _attn_probs(qs, kb)
        dps = _each(lambda d: lax.dot_general(d, vb, _NT, preferred_element_type=F32), dos)
        dss = _each(lambda p, dp: (p * (dp - jnp.sum(p * dp, axis=-1, keepdims=True)) * (HEAD_DIM ** -0.5)).astype(BF16),
                    ps, dps)
        rows = tq // _N_SUB
        for i, ds in enumerate(dss):
            dq_ref[i * rows:(i + 1) * rows, :] = jnp.dot(ds, kb, preferred_element_type=F32)
        dk_ref[...] += sum(_each(lambda ds, q: lax.dot_general(ds, q, _TN, preferred_element_type=F32), dss, qs))
        dv_ref[...] += sum(_each(lambda p, d: lax.dot_general(p.astype(BF16), d, _TN, preferred_element_type=F32),
                                 ps, dos))

    q_spec = pl.BlockSpec((tq, HEAD_DIM), lambda kh, g, i: (i, kh * ATTN_GROUP + g))
    kv_spec = pl.BlockSpec((t, HEAD_DIM), lambda kh, g, i: (0, kh))
    return pl.pallas_call(
        body, name="attn_b", grid=(ATTN_KV_HEADS, ATTN_GROUP, n // tq),
        in_specs=[q_spec, kv_spec, kv_spec, q_spec],
        out_specs=[q_spec, kv_spec, kv_spec],
        out_shape=(jax.ShapeDtypeStruct(q.shape, F32), jax.ShapeDtypeStruct(k.shape, F32),
                   jax.ShapeDtypeStruct(v.shape, F32)),
        compiler_params=pltpu.CompilerParams(dimension_semantics=("parallel", "arbitrary", "arbitrary"),
                                             vmem_limit_bytes=VMEM_LIMIT),
    )(q, k, v, do)


@jax.custom_vjp
def attention(q, k, v):
    return _attn_fwd_call(q, k, v)


def _attention_fwd(q, k, v):
    return _attn_fwd_call(q, k, v), (q, k, v)


def _attention_bwd(res, do):
    return _attn_bwd_call(*res, do)


attention.defvjp(_attention_fwd, _attention_bwd)


_C = GDN_CHUNK


def _pdot(a, b):
    return jnp.dot(a, b, precision=lax.Precision.HIGH, preferred_element_type=F32)


@jax.custom_vjp
def _hdot(a, b):
    return jnp.dot(a.astype(BF16), b.astype(BF16), preferred_element_type=F32)


def _hdot_fwd(a, b):
    return _hdot(a, b), (a, b)


def _hdot_bwd(res, g):
    a, b = res
    gb = g.astype(BF16)
    return (lax.dot_general(gb, b.astype(BF16), _NT, preferred_element_type=F32),
            lax.dot_general(a.astype(BF16), gb, _TN, preferred_element_type=F32))


_hdot.defvjp(_hdot_fwd, _hdot_bwd)


def _each(fn, *lists):
    return [fn(*args) for args in zip(*lists)]


@jax.custom_vjp
def _unit_lower_inverse(low, blockdiag):
    return _unit_lower_inverse_chain(low, blockdiag)


def _unit_lower_inverse_fwd(low, blockdiag):
    t_inv = _unit_lower_inverse_chain(low, blockdiag)
    return t_inv, (t_inv, blockdiag)


def _unit_lower_inverse_bwd(res, d_inv):
    t_inv, blockdiag = res
    bf = lambda a: a.astype(BF16)
    left = _each(lambda t, g: lax.dot_general(bf(t), bf(g), _TN, preferred_element_type=F32), t_inv, d_inv)
    d_low = _each(lambda a, t: -lax.dot_general(bf(a), bf(t), _NT, preferred_element_type=F32), left, t_inv)
    return d_low, jnp.zeros_like(blockdiag)


_unit_lower_inverse.defvjp(_unit_lower_inverse_fwd, _unit_lower_inverse_bwd)


def _unit_lower_inverse_chain(low, blockdiag):
    eye = (lax.broadcasted_iota(jnp.int32, (_C, _C), 0) == lax.broadcasted_iota(jnp.int32, (_C, _C), 1)).astype(F32)
    ld = _each(lambda a: a * blockdiag, low)
    lo = _each(lambda a, d: a - d, low, ld)
    l2 = _each(_hdot, ld, ld)
    l4 = _each(_hdot, l2, l2)
    l8 = _each(_hdot, l4, l4)
    td = _each(lambda d, a2: _hdot(eye - d, eye + a2), ld, l2)
    td = _each(lambda t, a4: _hdot(t, eye + a4), td, l4)
    td = _each(lambda t, a8: _hdot(t, eye + a8), td, l8)
    nn = _each(_hdot, td, lo)
    n2 = _each(_hdot, nn, nn)
    out = _each(lambda n, m2: _hdot(eye - n, eye + m2), nn, n2)
    return _each(_hdot, out, td)


def _gdn_chunks(heads, blockdiag):
    q, k, v, b_b, be_b, e_b, kd_b, m1, dec, gl, s = (list(col) for col in zip(*heads))
    f32dot = lambda a, b: jnp.dot(a, b, preferred_element_type=F32)
    nt = lambda a, b: lax.dot_general(a, b, _NT, preferred_element_type=F32)
    kq_k = _each(lambda kx, qq: nt(jnp.concatenate([kx, qq], axis=0), kx), k, q)
    t_inv = _unit_lower_inverse(_each(lambda m, a: m * a[:_C], m1, kq_k), blockdiag)
    uw = _each(lambda t, b, x, be, kx: _hdot(t, jnp.concatenate([b * x, be * kx], axis=1)), t_inv, b_b, v, be_b, k)
    wq_s = _each(lambda a, qq, e, ss: f32dot(jnp.concatenate([a[:, HEAD_DIM:], qq * e], axis=0), ss), uw, q, e_b, s)
    delta = _each(lambda a, ws: a[:, :HEAD_DIM] - ws[:_C], uw, wq_s)
    p = _each(lambda d, a: d * a[_C:], dec, kq_k)
    o = _each(lambda ws, pp, dd: ws[_C:] + f32dot(pp, dd), wq_s, p, delta)
    s_new = _each(lambda g, ss, kx, kd, dd: g * ss + lax.dot_general(kx * kd, dd, _TN, preferred_element_type=F32),
                  gl, s, k, kd_b, delta)
    return o, s_new


def _blockdiag_mask():
    r = lax.broadcasted_iota(jnp.int32, (_C, _C), 0) >> 4
    c = lax.broadcasted_iota(jnp.int32, (_C, _C), 1) >> 4
    return (r == c).astype(F32)


_N_DIR = 2


def _scan_chunk(s, nc, ncc, reverse):
    return jnp.where(s < ncc, ncc - 1 - s, nc + ncc - 1 - s) if reverse else s


def _gdn_specs(nc, ncc, backward):
    step = (lambda s: nc - 1 - s) if backward else (lambda s: s)
    chunk = [lambda s, d=d: _scan_chunk(step(s), nc, ncc, d == 1) for d in range(_N_DIR)]
    tok = [pl.BlockSpec((_C, 3 * GDN_W), lambda s, d=d: (chunk[d](s), 0)) for d in range(_N_DIR)]
    park = [0, nc - ncc - 1]
    out = [pl.BlockSpec((_C, GDN_W), lambda s, d=d: (jnp.where(chunk[d](s) >= ncc, chunk[d](s) - ncc, park[d]), 0))
           for d in range(_N_DIR)]
    per_tok = pl.BlockSpec((_N_DIR, GDN_HEADS, _C, HEAD_DIM), lambda s: (0, 0, step(s), 0))
    mat = pl.BlockSpec((_N_DIR, GDN_HEADS, None, _C, _C), lambda s: (0, 0, step(s), 0, 0))
    row = pl.BlockSpec((_N_DIR, GDN_HEADS, None, 1, HEAD_DIM), lambda s: (0, 0, step(s), 0, 0))
    state = pl.BlockSpec((_N_DIR, GDN_HEADS, None, HEAD_DIM, HEAD_DIM), lambda s: (0, 0, step(s), 0, 0))
    return tok, out, per_tok, mat, row, state, chunk


def _head_cols(h, part):
    return slice((part * GDN_HEADS + h) * HEAD_DIM, (part * GDN_HEADS + h + 1) * HEAD_DIM)


def _gdn_heads(qkv_refs, factor_refs, state_of):
    return [[qkv_refs[d][:, _head_cols(h, 0)], qkv_refs[d][:, _head_cols(h, 1)], qkv_refs[d][:, _head_cols(h, 2)]]
            + [r[d, h] for r in factor_refs] + [state_of(d, h)]
            for d in range(_N_DIR) for h in range(GDN_HEADS)]


def _gdn_fwd_call(ncc, qkv, factors):
    t = qkv.shape[0]
    nc = t // _C
    tok, out, per_tok, mat, row, state, _ = _gdn_specs(nc, ncc, False)

    def body(*refs):
        qkv_refs, f_refs = refs[:_N_DIR], refs[_N_DIR:_N_DIR + 7]
        o_refs, sall_ref, s_ref = refs[_N_DIR + 7:2 * _N_DIR + 7], refs[2 * _N_DIR + 7], refs[-1]

        @pl.when(pl.program_id(0) == 0)
        def _():
            s_ref[...] = jnp.zeros_like(s_ref)

        heads = _gdn_heads(qkv_refs, f_refs, lambda d, h: s_ref[d, h])
        o, s_new = _gdn_chunks(heads, _blockdiag_mask())
        for d in range(_N_DIR):
            for h in range(GDN_HEADS):
                i = GDN_HEADS * d + h
                sall_ref[d, h] = heads[i][10]
                o_refs[d][:, _head_cols(h, 0)] = o[i]
                s_ref[d, h] = s_new[i]

    o_shape = jax.ShapeDtypeStruct((t - ncc * _C, GDN_W), F32)
    s_shape = (_N_DIR, GDN_HEADS, nc, HEAD_DIM, HEAD_DIM)
    return pl.pallas_call(
        body, name="gdn_f", grid=(nc,),
        in_specs=[*tok, per_tok, per_tok, per_tok, per_tok, mat, mat, row],
        out_specs=[*out, state], out_shape=[o_shape, o_shape, jax.ShapeDtypeStruct(s_shape, F32)],
        scratch_shapes=[pltpu.VMEM((_N_DIR, GDN_HEADS, HEAD_DIM, HEAD_DIM), F32)],
        compiler_params=pltpu.CompilerParams(dimension_semantics=("arbitrary",), vmem_limit_bytes=VMEM_LIMIT),
    )(qkv, qkv, *factors)


def _gdn_bwd_call(ncc, qkv, factors, sall, dos):
    t = qkv.shape[0]
    nc = t // _C
    tok, out, per_tok, mat, row, state, chunk = _gdn_specs(nc, ncc, True)

    def body(*refs):
        qkv_refs, f_refs, sall_ref = refs[:_N_DIR], refs[_N_DIR:_N_DIR + 7], refs[_N_DIR + 7]
        do_refs = refs[_N_DIR + 8:2 * _N_DIR + 8]
        dqkv_refs = refs[2 * _N_DIR + 8:3 * _N_DIR + 8]
        df_refs, ds_ref = refs[3 * _N_DIR + 8:3 * _N_DIR + 15], refs[-1]

        @pl.when(pl.program_id(0) == 0)
        def _():
            ds_ref[...] = jnp.zeros_like(ds_ref)

        bd = _blockdiag_mask()
        heads = _gdn_heads(qkv_refs, f_refs, lambda d, h: sall_ref[d, h])
        _, vjp = jax.vjp(lambda hs: _gdn_chunks(hs, bd), heads)
        live = [chunk[d](pl.program_id(0)) >= ncc for d in range(_N_DIR)]
        (all_grads,) = vjp(([jnp.where(live[d], do_refs[d][:, _head_cols(h, 0)], 0.0)
                             for d in range(_N_DIR) for h in range(GDN_HEADS)],
                            [ds_ref[d, h] for d in range(_N_DIR) for h in range(GDN_HEADS)]))
        for d in range(_N_DIR):
            for h in range(GDN_HEADS):
                grads = all_grads[GDN_HEADS * d + h]
                for part in range(3):
                    dqkv_refs[d][:, _head_cols(h, part)] = grads[part]
                for ref, g in zip(df_refs, grads[3:10]):
                    ref[d, h] = g
                ds_ref[d, h] = grads[10]

    shp = lambda a: jax.ShapeDtypeStruct(a.shape, F32)
    res = pl.pallas_call(
        body, name="gdn_b", grid=(nc,),
        in_specs=[*tok, per_tok, per_tok, per_tok, per_tok, mat, mat, row, state, *out],
        out_specs=[*tok, per_tok, per_tok, per_tok, per_tok, mat, mat, row],
        out_shape=[shp(qkv), shp(qkv)] + [shp(a) for a in factors],
        scratch_shapes=[pltpu.VMEM((_N_DIR, GDN_HEADS, HEAD_DIM, HEAD_DIM), F32)],
        compiler_params=pltpu.CompilerParams(dimension_semantics=("arbitrary",), vmem_limit_bytes=VMEM_LIMIT),
    )(qkv, qkv, *factors, sall, *dos)
    return res[0] + res[1], tuple(res[_N_DIR:])


@functools.partial(jax.custom_vjp, nondiff_argnums=(0,))
def gdn_scan(ncc, qkv, factors):
    o0, o1, _ = _gdn_fwd_call(ncc, qkv, factors)
    return o0, o1


def _gdn_scan_fwd(ncc, qkv, factors):
    o0, o1, sall = _gdn_fwd_call(ncc, qkv, factors)
    return (o0, o1), (qkv, factors, sall)


def _gdn_scan_bwd(ncc, res, dos):
    qkv, factors, sall = res
    return _gdn_bwd_call(ncc, qkv, factors, sall, list(dos))


gdn_scan.defvjp(_gdn_scan_fwd, _gdn_scan_bwd)


def _rope_tables(n, cl):
    t = np.arange(n)
    inv_freq = (ROPE_THETA ** (-np.arange(0, HEAD_DIM // 2, 2, dtype=np.float32) / (HEAD_DIM // 2))).astype(np.float32)
    ang_r = (t // GRID_W).astype(np.float32)[:, None] * inv_freq
    ang_c = (t % GRID_W).astype(np.float32)[:, None] * inv_freq
    cos = np.concatenate([np.cos(ang_r), np.cos(ang_r), np.cos(ang_c), np.cos(ang_c)], axis=1)
    sin = np.concatenate([-np.sin(ang_r), np.sin(ang_r), -np.sin(ang_c), np.sin(ang_c)], axis=1)
    cos_all = np.concatenate([np.ones((cl, HEAD_DIM), np.float32), cos], axis=0)
    sin_all = np.concatenate([np.zeros((cl, HEAD_DIM), np.float32), sin], axis=0)
    j = np.arange(HEAD_DIM)
    src = np.where((j % 64) < 32, j + 32, j - 32)
    perm = np.zeros((HEAD_DIM, HEAD_DIM), np.float32)
    perm[src, j] = 1.0
    return (jnp.asarray(cos.astype(np.float32)), jnp.asarray(sin.astype(np.float32)),
            jnp.asarray(cos_all), jnp.asarray(sin_all), jnp.asarray(perm))


def _gdn_factors(log_a, beta, ncc):
    t = log_a.shape[0]
    nc = t // _C
    la = log_a.reshape(nc, _C, _N_DIR, GDN_HEADS).transpose(2, 3, 0, 1)
    be = beta.reshape(nc, _C, _N_DIR, GDN_HEADS).transpose(2, 3, 0, 1)
    scan_order = lambda a: jnp.stack([a[0], jnp.concatenate([jnp.flip(a[1][:, :ncc], axis=1),
                                                              jnp.flip(a[1][:, ncc:], axis=1)], axis=1)])
    la, be = scan_order(la), scan_order(be)
    rev = jnp.asarray(np.array([False, True])[:, None, None, None])
    run = jnp.cumsum(la, axis=3)
    gam = jnp.where(rev, jnp.sum(la, axis=3, keepdims=True) - run + la, run)
    idx = np.arange(_C)
    incl = jnp.asarray(np.stack([idx[:, None] >= idx[None, :], idx[:, None] <= idx[None, :]])[:, None, None])
    strict = jnp.asarray(np.stack([idx[:, None] > idx[None, :], idx[:, None] < idx[None, :]])[:, None, None])
    dec = jnp.exp(jnp.where(incl, gam[..., :, None] - gam[..., None, :], -jnp.inf))
    m1 = jnp.where(strict, be[..., :, None] * dec, 0.0)
    e = jnp.exp(gam)
    g_last = jnp.where(rev, gam[..., :1], gam[..., -1:])
    lanes = lambda a: jnp.broadcast_to(a.reshape(_N_DIR, GDN_HEADS, t, 1), (_N_DIR, GDN_HEADS, t, HEAD_DIM))
    gl = jnp.broadcast_to(jnp.exp(g_last)[..., None], (_N_DIR, GDN_HEADS, nc, 1, HEAD_DIM))
    return lanes(be), lanes(be * e), lanes(e), lanes(jnp.exp(g_last - gam)), m1, dec, gl


def local_loss(x, wz, wb, ws, c, ctx, target):
    return channel_mixing(token_mixing(x, wz, wb, ws, c, ctx), wz, wb, ws, target)


def token_mixing(x, wz, wb, ws, c, ctx):
    n, cl = x.shape[0], ctx.shape[0]
    cos_q, sin_q, cos_k, sin_k, perm = _rope_tables(n, cl)

    sc_in = jnp.concatenate([jax.nn.silu(c), jax.nn.silu(ws["c_ctx"])[None, :], jnp.zeros((14, D_MODEL), F32)], axis=0)
    mod = pmm(sc_in, wb["w_mod"], wz["w_mod"], "mm_mod") + ws["b_mod"]
    sh1, sc1, g1, sh2, sc2, g2 = [mod[0:1, i * D_MODEL:(i + 1) * D_MODEL] for i in range(6)]
    csh1, csc1 = mod[1:2, 0:D_MODEL], mod[1:2, D_MODEL:2 * D_MODEL]

    def norm_mod(a, sh, sc):
        return (_rms(a) * (1.0 + sc) + sh,)

    (hx,) = rowop(norm_mod, "normmod_x", (x,), (sh1, sc1))
    (hc,) = rowop(norm_mod, "normmod_c", (ctx,), (csh1, csc1))
    h_all = jnp.concatenate([hc, hx], axis=0)
    p_main = pmm_t(h_all, wb["w_in_main"], wz["w_in_main"], "mm_in")
    p_small = pmm_t(h_all, wb["w_in_small"], wz["w_in_small"], "mm_ins")
    ak, av, qkv, aq, z, gate = jnp.split(p_main, [KV_W, 2 * KV_W, SMALL_AT, SMALL_AT + Q_W, SMALL_AT + Q_W + GDN_W],
                                         axis=1)
    db, da = p_small[:, :2 * GDN_HEADS], p_small[:, 2 * GDN_HEADS:4 * GDN_HEADS]

    def qk_prep(nh):
        def fn(a, w, cos, sin, pm):
            outs = []
            for ah in _heads(a, nh):
                y = _rms(ah) * w
                outs.append(y * cos + _pdot(y, pm) * sin)
            return (jnp.concatenate(outs, axis=1),)
        return fn

    (q_x,) = rowop(qk_prep(ATTN_HEADS), "q_prep", (aq[cl:],), (ws["q_norm_w"],), (cos_q, sin_q), (perm,))
    (k_all,) = rowop(qk_prep(ATTN_KV_HEADS), "k_prep", (ak,), (ws["k_norm_w"],), (cos_k, sin_k), (perm,))
    attn_x = attention(q_x, k_all, av)

    cw = ws["conv_qkv_w"]
    normed = jnp.asarray(np.repeat([1.0, 1.0, 0.0], GDN_W)[None, :], F32)
    scale = jnp.asarray(np.repeat([HEAD_DIM ** -0.5, 1.0, 1.0], GDN_W)[None, :], F32)

    def gdn_prep(a, w0, w1, w2, nf, sc):
        s = jax.nn.silu(_conv3(a, w0, w1, w2, (0, cl)))
        inv = lax.rsqrt(jnp.sum(s * s, axis=-1, keepdims=True) + NORM_EPS)
        return (s * jnp.where(nf > 0.0, inv * sc, 1.0),)

    (qkvn,) = colop(gdn_prep, "gdn_prep", (qkv, cw[0:1], cw[1:2], cw[2:3], normed, scale),
                    [(i, 0) for i in range(6)], 2, 3 * GDN_HEADS)
    beta = jax.nn.sigmoid(db).reshape(-1, 2, GDN_HEADS)
    log_a = -jnp.exp(ws["a_log"])[None] * jax.nn.softplus(da.reshape(-1, 2, GDN_HEADS) + ws["dt_bias"][None])
    o_fwd, o_rev = gdn_scan(cl // _C, qkvn, _gdn_factors(log_a, beta, cl // _C))
    o_x = o_fwd + o_rev

    def gdn_out(o, zz, w):
        outs = [_rms(oh) * w * jax.nn.silu(zh) for oh, zh in zip(_heads(o, GDN_HEADS), _heads(zz, GDN_HEADS))]
        return (jnp.concatenate(outs, axis=1),)

    (gdn_x,) = rowop(gdn_out, "gdn_out", (o_x, z[cl:]), (ws["gdn_norm_w"],))
    return dict(x=x, attn_x=attn_x, gdn_x=gdn_x, gate=gate[cl:], g1=g1, sh2=sh2, sc2=sc2, g2=g2)


def channel_mixing(mixed, wz, wb, ws, target):
    x, attn_x, gdn_x, gate = mixed["x"], mixed["attn_x"], mixed["gdn_x"], mixed["gate"]
    g1, sh2, sc2, g2 = mixed["g1"], mixed["sh2"], mixed["sc2"], mixed["g2"]
    pa = pmm(attn_x, wb["w_pa"], wz["w_pa"], "mm_pa")
    pd = pmm(gdn_x, wb["w_pd"], wz["w_pd"], "mm_pd")

    def merge(a, d, g):
        return (jax.nn.sigmoid(g[:, :D_MODEL]) * a + jax.nn.sigmoid(g[:, D_MODEL:]) * d,)

    (y,) = rowop(merge, "merge", (pa, pd, gate))
    mo = pmm(y, wb["w_out"], wz["w_out"], "mm_out")

    def res_norm_mod(xx, m, g, sh, sc):
        x1 = xx + g * m
        return x1, _rms(x1) * (1.0 + sc) + sh

    x1, h2 = rowop(res_norm_mod, "res1", (x, mo), (g1, sh2, sc2))
    up = pmm(h2, wb["w_up"], wz["w_up"], "mm_up")
    fw = ws["ffn_conv_w"]

    def ffn_act(ug, uv, w0g, w0v, w1g, w1v, w2g, w2v, bg, bv):
        g = _conv3(ug, w0g, w1g, w2g, (0,)) + bg
        v = _conv3(uv, w0v, w1v, w2v, (0,)) + bv
        return (jax.nn.silu(g) * v,)

    half = D_FF // HEAD_DIM
    (act,) = colop(ffn_act, "ffn_act", (up, fw[0:1], fw[1:2], fw[2:3], ws["ffn_conv_b"]),
                   [(i, off) for i in range(5) for off in (0, half)], 0, half)
    dn = pmm(act, wb["w_down"], wz["w_down"], "mm_down")

    def head(xx, m, g, w, tgt):
        yy = _rms(xx + g * m) * w
        err = (yy - tgt) ** 2
        return (jnp.broadcast_to(0.5 * jnp.mean(err, axis=-1, keepdims=True), (xx.shape[0], HEAD_DIM)),)

    (row_loss,) = rowop(head, "head", (x1, dn), (g2, ws["final_norm_w"][None, :]), (target,))
    return jnp.sum(row_loss[:, 0])


_HBM = pl.BlockSpec(memory_space=pltpu.HBM)


def _chip_peers():
    x, y = lax.axis_index("x"), lax.axis_index("y")
    return [(1 - x, y), (x, 1 - y), (1 - x, 1 - y)]


_SPLIT_COLS = ("w_in",)


def _half_of(view, nm, idx, lead=0):
    r, cdim = view.shape[-2:]
    pre = (slice(None),) * lead
    if nm in _SPLIT_COLS:
        return view.at[pre + (slice(None), pl.ds(pl.multiple_of(idx * (cdim // 2), 128), cdim // 2))]
    return view.at[pre + (pl.ds(pl.multiple_of(idx * (r // 2), 16), r // 2), slice(None))]


def _remote(src, dst, send_sem, recv_sem, dev):
    return pltpu.make_async_remote_copy(src_ref=src, dst_ref=dst, send_sem=send_sem, recv_sem=recv_sem,
                                        device_id=dev, device_id_type=MESH)


def _hbm_call(body, name, ins, out_shape, n_sems, in_place=False):
    names = tuple(ins)
    return dict(zip(names, pl.pallas_call(
        body, name=name, in_specs=[_HBM] * len(names), out_specs=[_HBM] * len(names),
        out_shape=[out_shape(nm, ins[nm]) for nm in names],
        scratch_shapes=[pltpu.SemaphoreType.DMA((k,)) for k in n_sems],
        input_output_aliases={i: i for i in range(len(names))} if in_place else {},
    )(*[ins[nm] for nm in names])))


def all_gather_chips(shards):
    names = tuple(shards)
    n = len(names)

    def body(*refs):
        ins, outs = dict(zip(names, refs[:n])), dict(zip(names, refs[n:2 * n]))
        ici_send, ici_recv, d2d_send, d2d_recv, own_send, own_recv = refs[2 * n:]
        x, y, c = lax.axis_index("x"), lax.axis_index("y"), lax.axis_index("c")
        me, sib = 2 * x + y, (x, y, 1 - c)
        own = [_remote(ins[nm], outs[nm].at[me], own_send.at[i], own_recv.at[i], sib) for i, nm in enumerate(names)]
        for cp in own:
            cp.start()
        sends = []
        for k, (px, py) in enumerate(_chip_peers()):
            for i, nm in enumerate(names):
                cp = _remote(_half_of(ins[nm], nm, c), _half_of(outs[nm].at[me], nm, c), ici_send.at[k * n + i],
                             ici_recv.at[k * n + i], (px, py, c))
                cp.start()
                sends.append(cp)
        for k, (px, py) in enumerate(_chip_peers()):
            for i, nm in enumerate(names):
                landed = _half_of(outs[nm].at[2 * px + py], nm, c)
                _remote(landed, landed, ici_send.at[k * n + i], ici_recv.at[k * n + i], (px, py, c)).wait_recv()
                fw = _remote(landed, landed, d2d_send.at[k * n + i], d2d_recv.at[k * n + i], sib)
                fw.start()
                sends.append(fw)
        for k, (px, py) in enumerate(_chip_peers()):
            for i, nm in enumerate(names):
                other = _half_of(outs[nm].at[2 * px + py], nm, 1 - c)
                _remote(other, other, d2d_send.at[k * n + i], d2d_recv.at[k * n + i], sib).wait_recv()
        for cp in sends:
            cp.wait_send()
        for cp in own:
            cp.wait()

    return _hbm_call(body, "ag_weights", shards, lambda nm, a: jax.ShapeDtypeStruct((N_CHIPS,) + a.shape, a.dtype),
                     (3 * n, 3 * n, 3 * n, 3 * n, n, n))


_SEM = pl.BlockSpec(memory_space=pltpu.SEMAPHORE)


def push_start(name, arrays, land_shapes, copies, n_copies):
    names = tuple(arrays)
    n = len(names)

    def body(*refs):
        send_sems, recv_sems, token = refs[2 * n], refs[2 * n + 1], refs[-1]
        for j, (src, dst, dev) in enumerate(copies(refs[:n], refs[n:2 * n])):
            _remote(src, dst, send_sems.at[j], recv_sems.at[j], dev).start()
        token[...] = jnp.zeros_like(token)

    hbm = lambda a: pltpu.with_memory_space_constraint(a, pltpu.HBM)
    lands = [lax.empty(land_shapes[nm], arrays[nm].dtype) for nm in names]
    res = pl.pallas_call(
        body, name=name,
        out_shape=(pltpu.SemaphoreType.DMA((n_copies,)), pltpu.SemaphoreType.DMA((n_copies,)),
                   *[pltpu.HBM(arrays[nm].shape, arrays[nm].dtype) for nm in names],
                   *[pltpu.HBM(a.shape, a.dtype) for a in lands], jax.ShapeDtypeStruct((8, 128), F32)),
        in_specs=[_HBM] * (2 * n),
        out_specs=(_SEM, _SEM, *[_HBM] * (2 * n), pl.BlockSpec(memory_space=pltpu.VMEM)),
        input_output_aliases={i: 2 + i for i in range(2 * n)},
        compiler_params=pltpu.CompilerParams(has_side_effects=pltpu.SideEffectType.DATAFLOW_SIDE_EFFECTING),
    )(*[hbm(arrays[nm]) for nm in names], *[hbm(a) for a in lands])
    return names, res[0], res[1], res[2:2 + n], res[2 + n:2 + 2 * n], res[-1]


def push_wait(name, started, copies, after):
    names, send_sems, recv_sems, srcs, lands, _ = started
    n = len(names)

    def body(*refs):
        send_ref, recv_ref = refs[2 * n], refs[2 * n + 1]
        for j, (src, dst, dev) in enumerate(copies(refs[:n], refs[n:2 * n])):
            cp = _remote(src, dst, send_ref.at[j], recv_ref.at[j], dev)
            cp.wait_send()
            cp.wait_recv()

    res = pl.pallas_call(
        body, name=name,
        out_shape=(*[pltpu.HBM(a.shape, a.dtype) for a in srcs], *[pltpu.HBM(a.shape, a.dtype) for a in lands]),
        in_specs=[_HBM] * (2 * n) + [_SEM, _SEM, pl.BlockSpec(memory_space=pl.ANY)],
        out_specs=tuple([_HBM] * (2 * n)),
        input_output_aliases={i: i for i in range(2 * n)},
        compiler_params=pltpu.CompilerParams(has_side_effects=pltpu.SideEffectType.DATAFLOW_SIDE_EFFECTING),
    )(*srcs, *lands, send_sems, recv_sems, after)
    return dict(zip(names, res[n:]))


def _gather_copies(srcs, lands):
    x, y, c = lax.axis_index("x"), lax.axis_index("y"), lax.axis_index("c")
    devs = [(px, py, c) for px, py in _chip_peers()] + [(x, y, 1 - c)]
    return [(src, land.at[2 * x + y], dev) for src, land in zip(srcs, lands) for dev in devs]


def _scatter_copies(srcs, lands):
    c = lax.axis_index("c")
    return [(src.at[2 * px + py], land.at[k], (px, py, c))
            for src, land in zip(srcs, lands) for k, (px, py) in enumerate(_chip_peers())]


def sibling_halves(blocks, name):
    names = tuple(blocks)

    def body(*refs):
        n = len(names)
        ins, outs = dict(zip(names, refs[:n])), dict(zip(names, refs[n:2 * n]))
        send_sems, recv_sems = refs[2 * n:]
        x, y, c = lax.axis_index("x"), lax.axis_index("y"), lax.axis_index("c")
        cps = [_remote(_half_of(ins[nm], nm, 1 - c, lead=1), outs[nm], send_sems.at[i], recv_sems.at[i], (x, y, 1 - c))
               for i, nm in enumerate(names)]
        for cp in cps:
            cp.start()
        for cp in cps:
            cp.wait()

    def half_shape(nm, a):
        r, cdim = a.shape[-2:]
        return jax.ShapeDtypeStruct((N_CHIPS, r, cdim // 2) if nm in _SPLIT_COLS else (N_CHIPS, r // 2, cdim), a.dtype)

    return _hbm_call(body, name, blocks, half_shape, (len(names), len(names)))


def scatter_halves(blocks):
    names = tuple(blocks)
    n = len(names)

    def body(*refs):
        ins, outs = dict(zip(names, refs[:n])), dict(zip(names, refs[n:2 * n]))
        send_sems, recv_sems = refs[2 * n:]
        c = lax.axis_index("c")
        cps = [_remote(ins[nm].at[2 * px + py], outs[nm].at[k], send_sems.at[k * n + i], recv_sems.at[k * n + i],
                       (px, py, c))
               for k, (px, py) in enumerate(_chip_peers()) for i, nm in enumerate(names)]
        for cp in cps:
            cp.start()
        for cp in cps:
            cp.wait_recv()
        for cp in cps:
            cp.wait_send()

    return _hbm_call(body, "rs_grads", blocks, lambda nm, a: jax.ShapeDtypeStruct((3,) + a.shape[1:], a.dtype),
                     (3 * n, 3 * n))


def sibling_assemble(arrays, name):
    names = tuple(arrays)

    def body(*refs):
        n = len(names)
        ins, outs = dict(zip(names, refs[:n])), dict(zip(names, refs[n:2 * n]))
        send_sems, recv_sems = refs[2 * n:]
        x, y, c = lax.axis_index("x"), lax.axis_index("y"), lax.axis_index("c")
        cps = [_remote(_half_of(ins[nm], nm, c), _half_of(outs[nm], nm, c), send_sems.at[i], recv_sems.at[i],
                       (x, y, 1 - c)) for i, nm in enumerate(names)]
        for cp in cps:
            cp.start()
        for i, nm in enumerate(names):
            other = _half_of(outs[nm], nm, 1 - c)
            _remote(other, other, send_sems.at[i], recv_sems.at[i], (x, y, 1 - c)).wait_recv()
        for cp in cps:
            cp.wait_send()

    return _hbm_call(body, name, arrays, lambda nm, a: jax.ShapeDtypeStruct(a.shape, a.dtype),
                     (len(names), len(names)), in_place=True)


def all_reduce_small(v):
    def body(v_ref, tot_ref, gath_ref, send_sems, recv_sems):
        x, y, c = lax.axis_index("x"), lax.axis_index("y"), lax.axis_index("c")
        me = 4 * x + 2 * y + c
        gath_ref[me] = v_ref[...]

        def peer(k):
            m = k + 1
            return (x ^ (m >> 2 & 1), y ^ (m >> 1 & 1), c ^ (m & 1))

        sends = [pltpu.make_async_remote_copy(src_ref=v_ref, dst_ref=gath_ref.at[me], send_sem=send_sems.at[k],
                                              recv_sem=recv_sems.at[k], device_id=peer(k), device_id_type=MESH)
                 for k in range(N_DEV - 1)]
        for cp in sends:
            cp.start()
        for k in range(N_DEV - 1):
            px, py, pc = peer(k)
            pltpu.make_async_remote_copy(src_ref=v_ref, dst_ref=gath_ref.at[4 * px + 2 * py + pc],
                                         send_sem=send_sems.at[k], recv_sem=recv_sems.at[k], device_id=peer(k),
                                         device_id_type=MESH).wait_recv()
        for cp in sends:
            cp.wait_send()
        acc = gath_ref[0]
        for d in range(1, N_DEV):
            acc = acc + gath_ref[d]
        tot_ref[...] = acc

    vm = pl.BlockSpec(memory_space=pltpu.VMEM)
    return pl.pallas_call(
        body, name="ar_small", in_specs=[vm], out_specs=[vm, vm],
        out_shape=(jax.ShapeDtypeStruct(v.shape, v.dtype), jax.ShapeDtypeStruct((N_DEV,) + v.shape, v.dtype)),
        scratch_shapes=[pltpu.SemaphoreType.DMA((N_DEV - 1,)), pltpu.SemaphoreType.DMA((N_DEV - 1,))],
    )(v)[0]


def _elementwise(fn, name, ins, n_out, out_dtype=F32):
    r, cdim = ins[0].shape
    tr = _pick(r, tuple(p for p in (488, 256, 128, 104, 64, 32, 16, 8) if p * cdim * 4 <= 2 * 1024 * 1024))
    spec = pl.BlockSpec((tr, cdim), lambda i: (i, 0))

    def body(*refs):
        res = fn(*[ref[...] for ref in refs[:len(ins)]])
        for o_ref, v in zip(refs[len(ins):], res):
            o_ref[...] = v

    return pl.pallas_call(
        body, name=name, grid=(r // tr,), in_specs=[spec] * len(ins), out_specs=[spec] * n_out,
        out_shape=tuple(jax.ShapeDtypeStruct((r, cdim), out_dtype) for _ in range(n_out)),
        compiler_params=pltpu.CompilerParams(dimension_semantics=("parallel",), vmem_limit_bytes=VMEM_LIMIT),
    )(*ins)


def _half_block_specs(nm, shard_shape):
    r, cdim = shard_shape
    if nm in _SPLIT_COLS:
        return (None, r, cdim // 2), (lambda j, c: (j, 0, c))
    return (None, r // 2, cdim), (lambda j, c: (j, c, 0))


def _presum(nm, sel, g32, a):
    blk, at = _half_block_specs(nm, g32.shape[1:])

    def body(s_ref, g_ref, a_ref, o_ref):
        del s_ref
        o_ref[...] = (g_ref[...] + a_ref[...]).astype(BF16)

    return pl.pallas_call(
        body, name="rs_presum_" + nm,
        grid_spec=pltpu.PrefetchScalarGridSpec(
            num_scalar_prefetch=1, grid=(N_CHIPS,),
            in_specs=[pl.BlockSpec(blk, lambda j, s: at(j, s[0])), pl.BlockSpec(blk, lambda j, s: (j, 0, 0))],
            out_specs=pl.BlockSpec(blk, lambda j, s: (j, 0, 0))),
        out_shape=jax.ShapeDtypeStruct(a.shape, BF16),
        compiler_params=pltpu.CompilerParams(dimension_semantics=("parallel",), vmem_limit_bytes=VMEM_LIMIT),
    )(sel, g32, a)


def _finalsum(nm, sel, g32, a, got):
    blk, at = _half_block_specs(nm, g32.shape[1:])

    def body(s_ref, g_ref, a_ref, r_ref, o_ref):
        del s_ref
        acc = g_ref[...] + a_ref[...]
        for k in range(3):
            acc = acc + r_ref[k].astype(F32)
        o_ref[...] = acc

    return pl.pallas_call(
        body, name="rs_final_" + nm,
        grid_spec=pltpu.PrefetchScalarGridSpec(
            num_scalar_prefetch=1, grid=(1,),
            in_specs=[pl.BlockSpec(blk, lambda i, s: at(s[1], s[0])), pl.BlockSpec(blk, lambda i, s: (s[1], 0, 0)),
                      pl.BlockSpec(got.shape, lambda i, s: (0, 0, 0))],
            out_specs=pl.BlockSpec(blk[1:], lambda i, s: at(0, s[0])[1:])),
        out_shape=jax.ShapeDtypeStruct(g32.shape[1:], F32),
        compiler_params=pltpu.CompilerParams(dimension_semantics=("arbitrary",), vmem_limit_bytes=VMEM_LIMIT),
    )(sel, g32, a, got)


def _adamw(w, g, m, v, name):
    shape = w.shape
    to2 = lambda a: a.reshape(-1, shape[-1])

    def fn(w_, g_, m_, v_):
        m_new = ADAM_B1 * m_ + (1.0 - ADAM_B1) * g_
        v_new = ADAM_B2 * v_ + (1.0 - ADAM_B2) * (g_ * g_)
        m_hat = m_new / (1.0 - ADAM_B1 ** ADAM_STEP)
        v_hat = v_new / (1.0 - ADAM_B2 ** ADAM_STEP)
        delta = -ADAM_LR * (m_hat / (jnp.sqrt(v_hat) + ADAM_EPS) + ADAM_WD * w_)
        return g_, delta, m_new, v_new

    outs = _elementwise(fn, name, [to2(a) for a in (w, g, m, v)], 4)
    return tuple(o.reshape(shape) for o in outs)


_BIG = ("w_mod", "w_in", "w_pa", "w_pd", "w_out", "w_up", "w_down")
_EARLY = ("w_mod", "w_in")
_LATE = ("w_pa", "w_pd", "w_out", "w_up", "w_down")
_COL_SHARDED = ("w_mod", "w_up")
_FULL_SHAPE = {"w_mod": (D_MODEL, MOD_W), "w_in": (IN_COLS, D_MODEL), "w_pa": (Q_W, D_MODEL), "w_pd": (GDN_W, D_MODEL),
               "w_out": (D_MODEL, D_MODEL), "w_up": (D_MODEL, 2 * D_FF), "w_down": (D_FF, D_MODEL)}


def _shard_shape(name):
    r, cdim = _FULL_SHAPE[name]
    return (r, cdim // N_CHIPS) if name in _COL_SHARDED else (r // N_CHIPS, cdim)


_CONV_ELEMS = 2 * (3 * CONV_W // N_CHIPS + 3 * 2 * D_FF // N_CHIPS)
_CONV_ROWS = 32


def _blocks_of_full(name, full):
    r, cdim = _FULL_SHAPE[name]
    if name in _COL_SHARDED:
        return full.reshape(r, N_CHIPS, cdim // N_CHIPS).transpose(1, 0, 2)
    return full.reshape(N_CHIPS, r // N_CHIPS, cdim)


def _full_of_blocks(name, blocks):
    r, cdim = _FULL_SHAPE[name]
    if name in _COL_SHARDED:
        return blocks.transpose(1, 0, 2).reshape(r, cdim)
    return blocks.reshape(r, cdim)


def _w_in_regroup(w_in_t):
    main = jnp.concatenate([w_in_t[:SMALL_AT], w_in_t[SMALL_AT + 4 * GDN_HEADS:]], axis=0)
    small = jnp.pad(w_in_t[SMALL_AT:SMALL_AT + 4 * GDN_HEADS], ((0, HEAD_DIM - 4 * GDN_HEADS), (0, 0)))
    return main, small


def _w_in_ungroup(main, small):
    return jnp.concatenate([main[:SMALL_AT], small[:4 * GDN_HEADS], main[SMALL_AT:]], axis=0)


_SMALL = ("c_ctx", "b_mod", "q_norm_w", "k_norm_w", "conv_qkv_w", "a_log", "dt_bias", "gdn_norm_w", "ffn_conv_w",
          "ffn_conv_b", "final_norm_w")


def _pack_small(tree, rows):
    flat = jnp.concatenate([tree[nm].reshape(-1) for nm in _SMALL])
    return jnp.pad(flat, (0, rows * 128 - flat.shape[0])).reshape(rows, 128)


def _unpack_small(packed, like):
    flat, out, off = packed.reshape(-1), {}, 0
    for nm in _SMALL:
        size = int(np.prod(like[nm].shape))
        out[nm] = flat[off:off + size].reshape(like[nm].shape)
        off += size
    return out


def kernel(x, c, ctx, c_ctx, w_mod, b_mod, w_in, q_norm_w, k_norm_w, conv_qkv_w, a_log, dt_bias, gdn_norm_w, w_pa, w_pd, w_out, w_up, ffn_conv_w, ffn_conv_b, w_down, final_norm_w, loss_target, m_c_ctx, m_w_mod, m_b_mod, m_w_in, m_q_norm_w, m_k_norm_w, m_conv_qkv_w, m_a_log, m_dt_bias, m_gdn_norm_w, m_w_pa, m_w_pd, m_w_out, m_w_up, m_ffn_conv_w, m_ffn_conv_b, m_w_down, m_final_norm_w, v_c_ctx, v_w_mod, v_b_mod, v_w_in, v_q_norm_w, v_k_norm_w, v_conv_qkv_w, v_a_log, v_dt_bias, v_gdn_norm_w, v_w_pa, v_w_pd, v_w_out, v_w_up, v_ffn_conv_w, v_ffn_conv_b, v_w_down, v_final_norm_w):
    names = ("c_ctx", "w_mod", "b_mod", "w_in", "q_norm_w", "k_norm_w", "conv_qkv_w", "a_log", "dt_bias", "gdn_norm_w",
             "w_pa", "w_pd", "w_out", "w_up", "ffn_conv_w", "ffn_conv_b", "w_down", "final_norm_w")
    w_sh = dict(c_ctx=c_ctx, w_mod=w_mod, b_mod=b_mod, w_in=w_in, q_norm_w=q_norm_w, k_norm_w=k_norm_w,
                conv_qkv_w=conv_qkv_w, a_log=a_log, dt_bias=dt_bias, gdn_norm_w=gdn_norm_w, w_pa=w_pa, w_pd=w_pd,
                w_out=w_out, w_up=w_up, ffn_conv_w=ffn_conv_w, ffn_conv_b=ffn_conv_b, w_down=w_down,
                final_norm_w=final_norm_w)
    m_sh = dict(c_ctx=m_c_ctx, w_mod=m_w_mod, b_mod=m_b_mod, w_in=m_w_in, q_norm_w=m_q_norm_w, k_norm_w=m_k_norm_w,
                conv_qkv_w=m_conv_qkv_w, a_log=m_a_log, dt_bias=m_dt_bias, gdn_norm_w=m_gdn_norm_w, w_pa=m_w_pa,
                w_pd=m_w_pd, w_out=m_w_out, w_up=m_w_up, ffn_conv_w=m_ffn_conv_w, ffn_conv_b=m_ffn_conv_b,
                w_down=m_w_down, final_norm_w=m_final_norm_w)
    v_sh = dict(c_ctx=v_c_ctx, w_mod=v_w_mod, b_mod=v_b_mod, w_in=v_w_in, q_norm_w=v_q_norm_w, k_norm_w=v_k_norm_w,
                conv_qkv_w=v_conv_qkv_w, a_log=v_a_log, dt_bias=v_dt_bias, gdn_norm_w=v_gdn_norm_w, w_pa=v_w_pa,
                w_pd=v_w_pd, w_out=v_w_out, w_up=v_w_up, ffn_conv_w=v_ffn_conv_w, ffn_conv_b=v_ffn_conv_b,
                w_down=v_w_down, final_norm_w=v_final_norm_w)
    chip = 2 * lax.axis_index("x") + lax.axis_index("y")

    conv_bits = jnp.concatenate([lax.bitcast_convert_type(w_sh[nm][0], BF16).reshape(-1)
                                 for nm in ("conv_qkv_w", "ffn_conv_w")])
    shards = {nm: w_sh[nm][0].astype(BF16).T if nm == "w_in" else w_sh[nm][0].astype(BF16) for nm in _BIG}
    shards["conv"] = jnp.pad(conv_bits, (0, _CONV_ROWS * D_MODEL - _CONV_ELEMS)).reshape(_CONV_ROWS, D_MODEL)
    gathered = all_gather_chips({nm: shards[nm] for nm in _EARLY + ("conv",)})
    gathered, late_shards = lax.optimization_barrier((gathered, {nm: shards[nm] for nm in _LATE}))
    started = push_start("ag_late_start", late_shards, {nm: (N_CHIPS,) + a.shape for nm, a in late_shards.items()},
                         _gather_copies, 4 * len(_LATE))
    c = c + started[-1][0:1, 0:1]

    wb = {nm: _full_of_blocks(nm, gathered[nm]) for nm in _EARLY}
    wb["w_in_main"], wb["w_in_small"] = _w_in_regroup(wb.pop("w_in"))
    conv_all = gathered["conv"].reshape(N_CHIPS, -1)[:, :_CONV_ELEMS]
    n_cq = 2 * 3 * CONV_W // N_CHIPS
    unbits = lambda a, w: lax.bitcast_convert_type(a.reshape(N_CHIPS, 3, w // N_CHIPS, 2), F32).transpose(1, 0, 2).reshape(3, w)
    ws = dict(c_ctx=c_ctx, b_mod=b_mod, q_norm_w=q_norm_w, k_norm_w=k_norm_w, a_log=a_log[0], dt_bias=dt_bias[0],
              gdn_norm_w=gdn_norm_w, ffn_conv_b=ffn_conv_b, final_norm_w=final_norm_w,
              conv_qkv_w=unbits(conv_all[:, :n_cq], CONV_W), ffn_conv_w=unbits(conv_all[:, n_cq:], 2 * D_FF))
    wz = {nm: jnp.zeros(a.shape, F32) for nm, a in wb.items()}
    wz.update({nm: jnp.zeros(_FULL_SHAPE[nm], F32) for nm in _LATE})

    mixed, vjp_mix = jax.vjp(lambda x_, wz_, ws_: token_mixing(x_, wz_, wb, ws_, c, ctx[0]), x[0], wz, ws)
    got = push_wait("ag_late_wait", started, _gather_copies, mixed["gdn_x"])
    wb_late = {nm: _full_of_blocks(nm, got[nm]) for nm in _LATE}
    loss_local, vjp_chan = jax.vjp(
        lambda mixed_, wz_, ws_: channel_mixing(mixed_, wz_, wb_late, ws_, loss_target[0]), mixed, wz, ws)
    d_mixed, gz_chan, gs_chan = vjp_chan(jnp.ones((), F32))

    sel = jnp.stack([lax.axis_index("c"), chip]).astype(jnp.int32)
    g32_late = {nm: _blocks_of_full(nm, gz_chan[nm]) for nm in _LATE}
    theirs_late = sibling_halves(g32_late, "rs_sibling_late")
    sums_late = {nm: _presum(nm, sel, g32_late[nm], theirs_late[nm]) for nm in _LATE}
    scattering = push_start("rs_late_start", sums_late, {nm: (3,) + a.shape[1:] for nm, a in sums_late.items()},
                            _scatter_copies, 3 * len(_LATE))
    d_mixed = {**d_mixed, "gdn_x": d_mixed["gdn_x"] + scattering[-1][0:1, 0:1]}
    gx, gz_mix, gs_mix = vjp_mix(d_mixed)
    gs = jax.tree.map(jnp.add, gs_mix, gs_chan)
    got_late = push_wait("rs_late_wait", scattering, _scatter_copies, gx)
    loss = lax.psum(loss_local, ("x", "y", "c"))

    gs["a_log"], gs["dt_bias"] = gs["a_log"][None], gs["dt_bias"][None]
    like = {nm: gs[nm] for nm in _SMALL}
    small_rows = -(-sum(int(np.prod(like[nm].shape)) for nm in _SMALL) // 1024) * 8
    small_sum = all_reduce_small(_pack_small(gs, small_rows))
    gz_mix["w_mod"], small_sum = lax.optimization_barrier((gz_mix["w_mod"], small_sum))
    g_small = _unpack_small(small_sum, like)
    for nm, width in (("conv_qkv_w", CONV_W), ("ffn_conv_w", 2 * D_FF)):
        g_small[nm] = lax.dynamic_slice_in_dim(g_small[nm], chip * (width // N_CHIPS), width // N_CHIPS, axis=1)[None]

    gz_mix["w_in"] = _w_in_ungroup(gz_mix.pop("w_in_main"), gz_mix.pop("w_in_small"))
    g32 = {nm: _blocks_of_full(nm, gz_mix[nm]) for nm in _EARLY}
    theirs = sibling_halves(g32, "rs_sibling")
    sums = {nm: _presum(nm, sel, g32[nm], theirs[nm]) for nm in _EARLY}
    scattering = push_start("rs_early_start", sums, {nm: (3,) + a.shape[1:] for nm, a in sums.items()},
                            _scatter_copies, 3 * len(_EARLY))
    zero = scattering[-1][0:1, 0:1]
    got_late = {nm: a + zero.astype(BF16) if nm in ("w_up", "w_down") else a for nm, a in got_late.items()}
    g_big = sibling_assemble({nm: _finalsum(nm, sel, g32_late[nm], theirs_late[nm], got_late[nm]) for nm in _LATE},
                             "rs_assemble_late")
    grads, deltas, new_m, new_v = {}, {}, {}, {}

    def adamw_big(nm):
        g = g_big[nm].T if nm == "w_in" else g_big[nm]
        grads[nm], deltas[nm], new_m[nm], new_v[nm] = (
            o[None] for o in _adamw(w_sh[nm][0], g, m_sh[nm][0], v_sh[nm][0], "adamw_" + nm))

    for nm in _LATE:
        adamw_big(nm)

    shard_like = {nm: w_sh[nm] for nm in _SMALL}
    rows_l = -(-sum(int(np.prod(shard_like[nm].shape)) for nm in _SMALL) // 1024) * 8
    g_l = _pack_small({nm: g_small[nm].reshape(w_sh[nm].shape) for nm in _SMALL}, rows_l) + zero
    outs = _adamw(_pack_small(w_sh, rows_l), g_l, _pack_small(m_sh, rows_l), _pack_small(v_sh, rows_l), "adamw_small")
    for tree, packed in zip((grads, deltas, new_m, new_v), outs):
        tree.update(_unpack_small(packed, shard_like))

    done_meanwhile = deltas["w_up"][0, :8, :HEAD_DIM] + outs[1][:8, :]
    got = push_wait("rs_early_wait", scattering, _scatter_copies, done_meanwhile)
    g_big.update(sibling_assemble({nm: _finalsum(nm, sel, g32[nm], theirs[nm], got[nm]) for nm in _EARLY},
                                  "rs_assemble"))
    for nm in _EARLY:
        adamw_big(nm)

    return (loss, gx[None], *[grads[nm] for nm in names], *[deltas[nm] for nm in names],
            *[new_m[nm] for nm in names], *[new_v[nm] for nm in names])
```

```python
import functools
import math

import jax
import jax.numpy as jnp
import numpy as np
from jax import lax
from jax.experimental import pallas as pl
from jax.experimental.pallas import tpu as pltpu

F32 = jnp.float32
BF16 = jnp.bfloat16
HIGHEST = lax.Precision.HIGHEST
MESH = pl.DeviceIdType.MESH

D_MODEL = 1024
GRID_W = 64
ATTN_HEADS = 8
ATTN_KV_HEADS = 2
ATTN_GROUP = ATTN_HEADS // ATTN_KV_HEADS
HEAD_DIM = 128
ROPE_THETA = 10000.0
GDN_HEADS = 8
GDN_CHUNK = 64
D_FF = 2816
NORM_EPS = 1e-6
KV_W = ATTN_KV_HEADS * HEAD_DIM
Q_W = ATTN_HEADS * HEAD_DIM
GDN_W = GDN_HEADS * HEAD_DIM
CONV_W = 3 * GDN_W
MOD_W = 6 * D_MODEL
IN_COLS = 2 * KV_W + CONV_W + 4 * GDN_HEADS + Q_W + GDN_W + 2 * D_MODEL
IN_MAIN = IN_COLS - 4 * GDN_HEADS
SMALL_AT = 2 * KV_W + CONV_W
N_CHIPS = 4
N_DEV = 8

ADAM_LR = 0.001
ADAM_B1 = 0.9
ADAM_B2 = 0.999
ADAM_EPS = 1e-08
ADAM_WD = 0.01
ADAM_STEP = 10

VMEM_LIMIT = 48 * 1024 * 1024
MATMUL_VMEM_BUDGET = 40 * 1024 * 1024
MATMUL_STEP_BYTES = 1200 * 1024


def _pick(dim, prefs):
    for p in prefs:
        if p <= dim and dim % p == 0:
            return p
    return dim


_DIMS = {
    "nn": (((1,), (0,)), ((), ())),
    "nt": (((1,), (1,)), ((), ())),
    "tn": (((0,), (0,)), ((), ())),
}


def _matmul_plan(m, n, k, a_bytes, b_bytes):
    best = None
    for tm in (2304, 2048, 1152, 1024, 768, 512, 384, 256, 128, m):
        for tn in (2560, 1536, 1408, 1024, 768, 512, 256, 128, n):
            for tk in (3840, 2816, 2560, 2304, 2048, 1920, 1408, 1152, 1024, 768, 512, 256, 128, k):
                if tm > m or tn > n or tk > k or m % tm or n % tn or k % tk:
                    continue
                blocks = tm * tk * a_bytes + tk * tn * b_bytes + tm * tn * 4
                casts = (tm * tk * 2 if a_bytes > 2 else 0) + (tk * tn * 2 if b_bytes > 2 else 0) + tm * tn * 4
                if 2 * blocks + casts > MATMUL_VMEM_BUDGET:
                    continue
                nm, nn, nk = m // tm, n // tn, k // tk
                size_a, size_b = m * k * a_bytes, k * n * b_bytes
                for n_inner in (True, False):
                    if n_inner:
                        traffic = (size_a if nk == 1 else nn * size_a) + nm * size_b
                    else:
                        traffic = nn * size_a + (size_b if nk == 1 else nm * size_b)
                    cost = traffic + nm * nn * nk * MATMUL_STEP_BYTES + (nk - 1) * m * n * 4
                    if best is None or cost < best[0]:
                        best = (cost, tm, tn, tk, n_inner)
    return best[1:]


def _matmul(a, b, mode, name):
    if mode == "nn":
        (m, k), (_, n) = a.shape, b.shape
    elif mode == "nt":
        (m, k), (n, _) = a.shape, b.shape
    else:
        (k, m), (_, n) = a.shape, b.shape
    tm, tn, tk, n_inner = _matmul_plan(m, n, k, a.dtype.itemsize, b.dtype.itemsize)
    nk = k // tk
    ij = (lambda g0, g1: (g0, g1)) if n_inner else (lambda g0, g1: (g1, g0))
    if mode == "tn":
        a_spec = pl.BlockSpec((tk, tm), lambda g0, g1, l: (l, ij(g0, g1)[0]))
    else:
        a_spec = pl.BlockSpec((tm, tk), lambda g0, g1, l: (ij(g0, g1)[0], l))
    if mode == "nt":
        b_spec = pl.BlockSpec((tn, tk), lambda g0, g1, l: (ij(g0, g1)[1], l))
    else:
        b_spec = pl.BlockSpec((tk, tn), lambda g0, g1, l: (l, ij(g0, g1)[1]))
    dims = _DIMS[mode]

    def body(a_ref, b_ref, o_ref):
        part = lax.dot_general(a_ref[...].astype(BF16), b_ref[...].astype(BF16), dims, preferred_element_type=F32)
        if nk == 1:
            o_ref[...] = part
        else:
            l = pl.program_id(2)

            @pl.when(l == 0)
            def _():
                o_ref[...] = part

            @pl.when(l > 0)
            def _():
                o_ref[...] += part

    return pl.pallas_call(
        body,
        name=name,
        grid=(m // tm, n // tn, nk) if n_inner else (n // tn, m // tm, nk),
        in_specs=[a_spec, b_spec],
        out_specs=pl.BlockSpec((tm, tn), lambda g0, g1, l: ij(g0, g1)),
        out_shape=jax.ShapeDtypeStruct((m, n), F32),
        compiler_params=pltpu.CompilerParams(dimension_semantics=("parallel", "parallel", "arbitrary"),
                                             vmem_limit_bytes=VMEM_LIMIT),
    )(a, b)


@functools.partial(jax.custom_vjp, nondiff_argnums=(3,))
def pmm(a, w, wz, name):
    del wz
    return _matmul(a, w, "nn", name + "_f")


def _pmm_fwd(a, w, wz, name):
    del wz
    return _matmul(a, w, "nn", name + "_f"), (a, w)


def _pmm_bwd(name, res, g):
    a, w = res
    da = _matmul(g, w, "nt", name + "_da")
    if a.shape[0] < 128:
        pad = 128 - a.shape[0]
        at = jnp.pad(a.T, ((0, 0), (0, pad)))
        gp = jnp.pad(g, ((0, pad), (0, 0)))
        dw = _matmul(at, gp, "nn", name + "_dw")
    else:
        dw = _matmul(a, g, "tn", name + "_dw")
    return da, jnp.zeros_like(w), dw


pmm.defvjp(_pmm_fwd, _pmm_bwd)


@functools.partial(jax.custom_vjp, nondiff_argnums=(3,))
def pmm_t(a, wt, wtz, name):
    del wtz
    return _matmul(a, wt, "nt", name + "_f")


def _pmm_t_fwd(a, wt, wtz, name):
    del wtz
    return _matmul(a, wt, "nt", name + "_f"), (a, wt)


def _pmm_t_bwd(name, res, g):
    a, wt = res
    return _matmul(g, wt, "nn", name + "_da"), jnp.zeros_like(wt), _matmul(g, a, "tn", name + "_dw")


pmm_t.defvjp(_pmm_t_fwd, _pmm_t_bwd)


def rowop(fn, name, rows, bcs=(), crows=(), cbcs=(), tr=256):
    rows, bcs, crows, cbcs = tuple(rows), tuple(bcs), tuple(crows), tuple(cbcs)
    n_rows = rows[0].shape[0]
    tr = _pick(n_rows, (tr, 128, 64, 32, 16, 8))
    nr, nb, ncr, ncb = len(rows), len(bcs), len(crows), len(cbcs)
    n_in = nr + nb + ncr + ncb
    grid = (n_rows // tr,)

    def blk(arr):
        return jax.ShapeDtypeStruct((tr, arr.shape[1]), arr.dtype)

    def row_spec(arr):
        return pl.BlockSpec((tr, arr.shape[1]), lambda i: (i, 0))

    def bc_spec(arr):
        return pl.BlockSpec(arr.shape, lambda i: (0, 0))

    out_blk = jax.eval_shape(fn, *[blk(r) for r in rows], *bcs, *[blk(r) for r in crows], *cbcs)
    n_out = len(out_blk)
    out_shape = tuple(jax.ShapeDtypeStruct((n_rows, o.shape[1]), o.dtype) for o in out_blk)
    in_specs = ([row_spec(r) for r in rows] + [bc_spec(b) for b in bcs]
                + [row_spec(r) for r in crows] + [bc_spec(b) for b in cbcs])

    def order(vals):
        return vals

    def fwd_call(args):
        def body(*refs):
            vals = [r[...] for r in refs[:n_in]]
            res = fn(*order(vals))
            for o_ref, r in zip(refs[n_in:], res):
                o_ref[...] = r

        return pl.pallas_call(
            body, name=name + "_f", grid=grid, in_specs=in_specs,
            out_specs=[row_spec(o) for o in out_shape], out_shape=out_shape,
            compiler_params=pltpu.CompilerParams(dimension_semantics=("parallel",), vmem_limit_bytes=VMEM_LIMIT),
        )(*args)

    def bwd_call(args, cts):
        def body(*refs):
            vals = [r[...] for r in refs[:n_in]]
            ct_refs = refs[n_in:n_in + n_out]
            d_rows = refs[n_in + n_out:n_in + n_out + nr]
            d_bcs = refs[n_in + n_out + nr:]
            consts = vals[nr + nb:]
            _, vjp = jax.vjp(lambda *p: fn(*p, *consts), *vals[:nr + nb])
            grads = vjp(tuple(c[...] for c in ct_refs))
            for ref, g in zip(d_rows, grads[:nr]):
                ref[...] = g

            @pl.when(pl.program_id(0) == 0)
            def _():
                for ref in d_bcs:
                    ref[...] = jnp.zeros_like(ref)

            for ref, g in zip(d_bcs, grads[nr:]):
                ref[...] += g

        d_shape = tuple(jax.ShapeDtypeStruct(r.shape, r.dtype) for r in rows + bcs)
        return pl.pallas_call(
            body, name=name + "_b", grid=grid,
            in_specs=in_specs + [row_spec(o) for o in out_shape],
            out_specs=[row_spec(r) for r in rows] + [bc_spec(b) for b in bcs], out_shape=d_shape,
            compiler_params=pltpu.CompilerParams(dimension_semantics=("arbitrary",), vmem_limit_bytes=VMEM_LIMIT),
        )(*args, *cts)

    @jax.custom_vjp
    def op(diff, const):
        return fwd_call(diff + const)

    def op_fwd(diff, const):
        return fwd_call(diff + const), (diff, const)

    def op_bwd(res, cts):
        diff, const = res
        grads = bwd_call(diff + const, tuple(cts))
        return tuple(grads), tuple(jnp.zeros_like(c) for c in const)

    op.defvjp(op_fwd, op_bwd)
    return op(rows + bcs, crows + cbcs)


def colop(fn, name, arrays, uses, n_const, nblk, cw=128):
    arrays = tuple(arrays)
    n_diff = len(arrays) - n_const
    nd = sum(1 for u in uses if u[0] < n_diff)
    assert all(u[0] < n_diff for u in uses[:nd]) and all(u[0] >= n_diff for u in uses[nd:])

    def spec(u):
        return pl.BlockSpec((arrays[u[0]].shape[0], cw), lambda j, off=u[1]: (0, off + j))

    def out_spec(rows):
        return pl.BlockSpec((rows, cw), lambda j: (0, j))

    out_blk = jax.eval_shape(fn, *[jax.ShapeDtypeStruct((arrays[u[0]].shape[0], cw), arrays[u[0]].dtype)
                                   for u in uses])
    out_shape = tuple(jax.ShapeDtypeStruct((o.shape[0], nblk * cw), o.dtype) for o in out_blk)
    params = pltpu.CompilerParams(dimension_semantics=("parallel",), vmem_limit_bytes=VMEM_LIMIT)

    def fwd_call(arrs):
        def body(*refs):
            res = fn(*[r[...] for r in refs[:len(uses)]])
            for o_ref, r in zip(refs[len(uses):], res):
                o_ref[...] = r

        return pl.pallas_call(
            body, name=name + "_f", grid=(nblk,), in_specs=[spec(u) for u in uses],
            out_specs=[out_spec(o.shape[0]) for o in out_shape], out_shape=out_shape, compiler_params=params,
        )(*[arrs[u[0]] for u in uses])

    def bwd_call(arrs, cts):
        def body(*refs):
            vals = [r[...] for r in refs[:len(uses)]]
            ct_refs = refs[len(uses):len(uses) + len(out_shape)]
            _, vjp = jax.vjp(lambda *p: fn(*p, *vals[nd:]), *vals[:nd])
            for ref, g in zip(refs[len(uses) + len(out_shape):], vjp(tuple(c[...] for c in ct_refs))):
                ref[...] = g

        d_shape = tuple(jax.ShapeDtypeStruct((arrays[u[0]].shape[0], nblk * cw), F32) for u in uses[:nd])
        return pl.pallas_call(
            body, name=name + "_b", grid=(nblk,),
            in_specs=[spec(u) for u in uses] + [out_spec(o.shape[0]) for o in out_shape],
            out_specs=[out_spec(s.shape[0]) for s in d_shape], out_shape=d_shape, compiler_params=params,
        )(*[arrs[u[0]] for u in uses], *cts)

    @jax.custom_vjp
    def op(diff, const):
        return fwd_call(diff + const)

    def op_fwd(diff, const):
        return fwd_call(diff + const), (diff, const)

    def op_bwd(res, cts):
        diff, const = res
        d_uses = bwd_call(diff + const, tuple(cts))
        grads = []
        for i in range(n_diff):
            parts = sorted([(u[1], k) for k, u in enumerate(uses[:nd]) if u[0] == i])
            grads.append(d_uses[parts[0][1]] if len(parts) == 1
                         else jnp.concatenate([d_uses[k] for _, k in parts], axis=1))
        return tuple(grads), tuple(jnp.zeros_like(c) for c in const)

    op.defvjp(op_fwd, op_bwd)
    return op(arrays[:n_diff], arrays[n_diff:])


@functools.partial(jax.custom_vjp, nondiff_argnums=(1,))
def _roll_rows(x, k):
    return pltpu.roll(x, k % x.shape[0], 0)


def _roll_rows_fwd(x, k):
    return _roll_rows(x, k), None


def _roll_rows_bwd(k, _, g):
    return (_roll_rows(g, -k),)


_roll_rows.defvjp(_roll_rows_fwd, _roll_rows_bwd)


def _conv3(x, w0, w1, w2, starts):
    rows = lax.broadcasted_iota(jnp.int32, x.shape, 0)
    ends = tuple(s - 1 for s in starts[1:]) + (x.shape[0] - 1,)
    first = functools.reduce(jnp.logical_or, [rows == s for s in starts])
    last = functools.reduce(jnp.logical_or, [rows == e for e in ends])
    prev = jnp.where(first, 0.0, _roll_rows(x, 1))
    nxt = jnp.where(last, 0.0, _roll_rows(x, -1))
    return prev * w0 + x * w1 + nxt * w2


def _rms(x):
    return x * lax.rsqrt(jnp.mean(x * x, axis=-1, keepdims=True) + NORM_EPS)


def _heads(x, n):
    return [x[:, h * HEAD_DIM:(h + 1) * HEAD_DIM] for h in range(n)]


_NT = (((1,), (1,)), ((), ()))
_TN = (((0,), (0,)), ((), ()))
_TQ = 256


_N_SUB = 2


def _sub_rows(ref, i):
    rows = ref.shape[0] // _N_SUB
    return ref[i * rows:(i + 1) * rows, :].astype(BF16)


def _attn_probs(qs, k):
    s = _each(lambda q: lax.dot_general(q, k, _NT, preferred_element_type=F32) * (HEAD_DIM ** -0.5), qs)
    m = _each(lambda a: jnp.max(a, axis=-1, keepdims=True), s)
    e = _each(lambda a, b: jnp.exp(a - b), s, m)
    inv = _each(lambda a: 1.0 / jnp.sum(a, axis=-1, keepdims=True), e)
    return _each(lambda a, b: a * b, e, inv)


def _attn_fwd_call(q, k, v):
    n, t = q.shape[0], k.shape[0]
    tq = _pick(n, (_TQ, 128))

    def body(q_ref, k_ref, v_ref, o_ref):
        vb = v_ref[...].astype(BF16)
        ps = _attn_probs([_sub_rows(q_ref, i) for i in range(_N_SUB)], k_ref[...].astype(BF16))
        rows = tq // _N_SUB
        for i, p in enumerate(ps):
            o_ref[i * rows:(i + 1) * rows, :] = jnp.dot(p.astype(BF16), vb, preferred_element_type=F32)

    return pl.pallas_call(
        body, name="attn_f", grid=(ATTN_HEADS, n // tq),
        in_specs=[pl.BlockSpec((tq, HEAD_DIM), lambda h, i: (i, h)),
                  pl.BlockSpec((t, HEAD_DIM), lambda h, i: (0, h // ATTN_GROUP)),
                  pl.BlockSpec((t, HEAD_DIM), lambda h, i: (0, h // ATTN_GROUP))],
        out_specs=pl.BlockSpec((tq, HEAD_DIM), lambda h, i: (i, h)),
        out_shape=jax.ShapeDtypeStruct(q.shape, F32),
        compiler_params=pltpu.CompilerParams(dimension_semantics=("parallel", "parallel"),
                                             vmem_limit_bytes=VMEM_LIMIT),
    )(q, k, v)


def _attn_bwd_call(q, k, v, do):
    n, t = q.shape[0], k.shape[0]
    tq = _pick(n, (_TQ, 128))

    def body(q_ref, k_ref, v_ref, do_ref, dq_ref, dk_ref, dv_ref):
        @pl.when((pl.program_id(1) == 0) & (pl.program_id(2) == 0))
        def _():
            dk_ref[...] = jnp.zeros_like(dk_ref)
            dv_ref[...] = jnp.zeros_like(dv_ref)

        kb, vb = k_ref[...].astype(BF16), v_ref[...].astype(BF16)
        qs = [_sub_rows(q_ref, i) for i in range(_N_SUB)]
        dos = [_sub_rows(do_ref, i) for i in range(_N_SUB)]
        ps = _attn_probs(qs, kb)
        dps = _each(lambda d: lax.dot_general(d, vb, _NT, preferred_element_type=F32), dos)
        dss = _each(lambda p, dp: (p * (dp - jnp.sum(p * dp, axis=-1, keepdims=True)) * (HEAD_DIM ** -0.5)).astype(BF16),
                    ps, dps)
        rows = tq // _N_SUB
        for i, ds in enumerate(dss):
            dq_ref[i * rows:(i + 1) * rows, :] = jnp.dot(ds, kb, preferred_element_type=F32)
        dk_ref[...] += sum(_each(lambda ds, q: lax.dot_general(ds, q, _TN, preferred_element_type=F32), dss, qs))
        dv_ref[...] += sum(_each(lambda p, d: lax.dot_general(p.astype(BF16), d, _TN, preferred_element_type=F32),
                                 ps, dos))

    q_spec = pl.BlockSpec((tq, HEAD_DIM), lambda kh, g, i: (i, kh * ATTN_GROUP + g))
    kv_spec = pl.BlockSpec((t, HEAD_DIM), lambda kh, g, i: (0, kh))
    return pl.pallas_call(
        body, name="attn_b", grid=(ATTN_KV_HEADS, ATTN_GROUP, n // tq),
        in_specs=[q_spec, kv_spec, kv_spec, q_spec],
        out_specs=[q_spec, kv_spec, kv_spec],
        out_shape=(jax.ShapeDtypeStruct(q.shape, F32), jax.ShapeDtypeStruct(k.shape, F32),
                   jax.ShapeDtypeStruct(v.shape, F32)),
        compiler_params=pltpu.CompilerParams(dimension_semantics=("parallel", "arbitrary", "arbitrary"),
                                             vmem_limit_bytes=VMEM_LIMIT),
    )(q, k, v, do)


@jax.custom_vjp
def attention(q, k, v):
    return _attn_fwd_call(q, k, v)


def _attention_fwd(q, k, v):
    return _attn_fwd_call(q, k, v), (q, k, v)


def _attention_bwd(res, do):
    return _attn_bwd_call(*res, do)


attention.defvjp(_attention_fwd, _attention_bwd)


_C = GDN_CHUNK


def _pdot(a, b):
    return jnp.dot(a, b, precision=lax.Precision.HIGH, preferred_element_type=F32)


@jax.custom_vjp
def _hdot(a, b):
    return jnp.dot(a.astype(BF16), b.astype(BF16), preferred_element_type=F32)


def _hdot_fwd(a, b):
    return _hdot(a, b), (a, b)


def _hdot_bwd(res, g):
    a, b = res
    gb = g.astype(BF16)
    return (lax.dot_general(gb, b.astype(BF16), _NT, preferred_element_type=F32),
            lax.dot_general(a.astype(BF16), gb, _TN, preferred_element_type=F32))


_hdot.defvjp(_hdot_fwd, _hdot_bwd)


def _each(fn, *lists):
    return [fn(*args) for args in zip(*lists)]


@jax.custom_vjp
def _unit_lower_inverse(low, blockdiag):
    return _unit_lower_inverse_chain(low, blockdiag)


def _unit_lower_inverse_fwd(low, blockdiag):
    t_inv = _unit_lower_inverse_chain(low, blockdiag)
    return t_inv, (t_inv, blockdiag)


def _unit_lower_inverse_bwd(res, d_inv):
    t_inv, blockdiag = res
    bf = lambda a: a.astype(BF16)
    left = _each(lambda t, g: lax.dot_general(bf(t), bf(g), _TN, preferred_element_type=F32), t_inv, d_inv)
    d_low = _each(lambda a, t: -lax.dot_general(bf(a), bf(t), _NT, preferred_element_type=F32), left, t_inv)
    return d_low, jnp.zeros_like(blockdiag)


_unit_lower_inverse.defvjp(_unit_lower_inverse_fwd, _unit_lower_inverse_bwd)


def _unit_lower_inverse_chain(low, blockdiag):
    eye = (lax.broadcasted_iota(jnp.int32, (_C, _C), 0) == lax.broadcasted_iota(jnp.int32, (_C, _C), 1)).astype(F32)
    ld = _each(lambda a: a * blockdiag, low)
    lo = _each(lambda a, d: a - d, low, ld)
    l2 = _each(_hdot, ld, ld)
    l4 = _each(_hdot, l2, l2)
    l8 = _each(_hdot, l4, l4)
    td = _each(lambda d, a2: _hdot(eye - d, eye + a2), ld, l2)
    td = _each(lambda t, a4: _hdot(t, eye + a4), td, l4)
    td = _each(lambda t, a8: _hdot(t, eye + a8), td, l8)
    nn = _each(_hdot, td, lo)
    n2 = _each(_hdot, nn, nn)
    out = _each(lambda n, m2: _hdot(eye - n, eye + m2), nn, n2)
    return _each(_hdot, out, td)


@jax.custom_vjp
def _inverse_given(low, t_inv):
    del low
    return t_inv


_inverse_given.defvjp(lambda low, t_inv: (t_inv, (t_inv, low[0])),
                      lambda res, d_inv: (_unit_lower_inverse_bwd(res, d_inv)[0], [jnp.zeros_like(t) for t in res[0]]))


def _gdn_chunks(heads, blockdiag, kept_inverses=None):
    q, k, v, b_b, be_b, e_b, kd_b, m1, dec, gl, s = (list(col) for col in zip(*heads))
    f32dot = lambda a, b: jnp.dot(a, b, preferred_element_type=F32)
    nt = lambda a, b: lax.dot_general(a, b, _NT, preferred_element_type=F32)
    kq_k = _each(lambda kx, qq: nt(jnp.concatenate([kx, qq], axis=0), kx), k, q)
    low = _each(lambda m, a: m * a[:_C], m1, kq_k)
    t_inv = _unit_lower_inverse(low, blockdiag) if kept_inverses is None else _inverse_given(low, kept_inverses)
    uw = _each(lambda t, b, x, be, kx: _hdot(t, jnp.concatenate([b * x, be * kx], axis=1)), t_inv, b_b, v, be_b, k)
    wq_s = _each(lambda a, qq, e, ss: f32dot(jnp.concatenate([a[:, HEAD_DIM:], qq * e], axis=0), ss), uw, q, e_b, s)
    delta = _each(lambda a, ws: a[:, :HEAD_DIM] - ws[:_C], uw, wq_s)
    p = _each(lambda d, a: d * a[_C:], dec, kq_k)
    o = _each(lambda ws, pp, dd: ws[_C:] + f32dot(pp, dd), wq_s, p, delta)
    s_new = _each(lambda g, ss, kx, kd, dd: g * ss + lax.dot_general(kx * kd, dd, _TN, preferred_element_type=F32),
                  gl, s, k, kd_b, delta)
    return (o, s_new) if kept_inverses is not None else (o, s_new, t_inv)


def _blockdiag_mask():
    r = lax.broadcasted_iota(jnp.int32, (_C, _C), 0) >> 4
    c = lax.broadcasted_iota(jnp.int32, (_C, _C), 1) >> 4
    return (r == c).astype(F32)


_N_DIR = 2


def _scan_chunk(s, nc, ncc, reverse):
    return jnp.where(s < ncc, ncc - 1 - s, nc + ncc - 1 - s) if reverse else s


def _gdn_specs(nc, ncc, backward):
    step = (lambda s: nc - 1 - s) if backward else (lambda s: s)
    chunk = [lambda s, d=d: _scan_chunk(step(s), nc, ncc, d == 1) for d in range(_N_DIR)]
    tok = [pl.BlockSpec((_C, 3 * GDN_W), lambda s, d=d: (chunk[d](s), 0)) for d in range(_N_DIR)]
    park = [0, nc - ncc - 1]
    out = [pl.BlockSpec((_C, GDN_W), lambda s, d=d: (jnp.where(chunk[d](s) >= ncc, chunk[d](s) - ncc, park[d]), 0))
           for d in range(_N_DIR)]
    per_tok = pl.BlockSpec((_N_DIR, GDN_HEADS, _C, HEAD_DIM), lambda s: (0, 0, step(s), 0))
    mat = pl.BlockSpec((_N_DIR, GDN_HEADS, None, _C, _C), lambda s: (0, 0, step(s), 0, 0))
    row = pl.BlockSpec((_N_DIR, GDN_HEADS, None, 1, HEAD_DIM), lambda s: (0, 0, step(s), 0, 0))
    state = pl.BlockSpec((_N_DIR, GDN_HEADS, None, HEAD_DIM, HEAD_DIM), lambda s: (0, 0, step(s), 0, 0))
    return tok, out, per_tok, mat, row, state, chunk


def _head_cols(h, part):
    return slice((part * GDN_HEADS + h) * HEAD_DIM, (part * GDN_HEADS + h + 1) * HEAD_DIM)


def _gdn_heads(qkv_refs, factor_refs, state_of):
    return [[qkv_refs[d][:, _head_cols(h, 0)], qkv_refs[d][:, _head_cols(h, 1)], qkv_refs[d][:, _head_cols(h, 2)]]
            + [r[d, h] for r in factor_refs] + [state_of(d, h)]
            for d in range(_N_DIR) for h in range(GDN_HEADS)]


def _gdn_fwd_call(ncc, qkv, factors):
    t = qkv.shape[0]
    nc = t // _C
    tok, out, per_tok, mat, row, state, _ = _gdn_specs(nc, ncc, False)

    def body(*refs):
        qkv_refs, f_refs = refs[:_N_DIR], refs[_N_DIR:_N_DIR + 7]
        o_refs, sall_ref, tall_ref, s_ref = refs[_N_DIR + 7:2 * _N_DIR + 7], refs[2 * _N_DIR + 7], refs[-2], refs[-1]

        @pl.when(pl.program_id(0) == 0)
        def _():
            s_ref[...] = jnp.zeros_like(s_ref)

        heads = _gdn_heads(qkv_refs, f_refs, lambda d, h: s_ref[d, h])
        o, s_new, t_inv = _gdn_chunks(heads, _blockdiag_mask())
        for d in range(_N_DIR):
            for h in range(GDN_HEADS):
                i = GDN_HEADS * d + h
                sall_ref[d, h] = heads[i][10]
                tall_ref[d, h] = t_inv[i]
                o_refs[d][:, _head_cols(h, 0)] = o[i]
                s_ref[d, h] = s_new[i]

    o_shape = jax.ShapeDtypeStruct((t - ncc * _C, GDN_W), F32)
    s_shape = (_N_DIR, GDN_HEADS, nc, HEAD_DIM, HEAD_DIM)
    t_shape = (_N_DIR, GDN_HEADS, nc, _C, _C)
    return pl.pallas_call(
        body, name="gdn_f", grid=(nc,),
        in_specs=[*tok, per_tok, per_tok, per_tok, per_tok, mat, mat, row],
        out_specs=[*out, state, mat],
        out_shape=[o_shape, o_shape, jax.ShapeDtypeStruct(s_shape, F32), jax.ShapeDtypeStruct(t_shape, F32)],
        scratch_shapes=[pltpu.VMEM((_N_DIR, GDN_HEADS, HEAD_DIM, HEAD_DIM), F32)],
        compiler_params=pltpu.CompilerParams(dimension_semantics=("arbitrary",), vmem_limit_bytes=VMEM_LIMIT),
    )(qkv, qkv, *factors)


def _gdn_bwd_call(ncc, qkv, factors, sall, tall, dos):
    t = qkv.shape[0]
    nc = t // _C
    tok, out, per_tok, mat, row, state, chunk = _gdn_specs(nc, ncc, True)

    def body(*refs):
        qkv_refs, f_refs, sall_ref, tall_ref = refs[:_N_DIR], refs[_N_DIR:_N_DIR + 7], refs[_N_DIR + 7], refs[_N_DIR + 8]
        do_refs = refs[_N_DIR + 9:2 * _N_DIR + 9]
        dqkv_refs = refs[2 * _N_DIR + 9:3 * _N_DIR + 9]
        df_refs, ds_ref = refs[3 * _N_DIR + 9:3 * _N_DIR + 16], refs[-1]

        @pl.when(pl.program_id(0) == 0)
        def _():
            ds_ref[...] = jnp.zeros_like(ds_ref)

        bd = _blockdiag_mask()
        heads = _gdn_heads(qkv_refs, f_refs, lambda d, h: sall_ref[d, h])
        kept = [tall_ref[d, h] for d in range(_N_DIR) for h in range(GDN_HEADS)]
        _, vjp = jax.vjp(lambda hs: _gdn_chunks(hs, bd, kept), heads)
        live = [chunk[d](pl.program_id(0)) >= ncc for d in range(_N_DIR)]
        (all_grads,) = vjp(([jnp.where(live[d], do_refs[d][:, _head_cols(h, 0)], 0.0)
                             for d in range(_N_DIR) for h in range(GDN_HEADS)],
                            [ds_ref[d, h] for d in range(_N_DIR) for h in range(GDN_HEADS)]))
        for d in range(_N_DIR):
            for h in range(GDN_HEADS):
                grads = all_grads[GDN_HEADS * d + h]
                for part in range(3):
                    dqkv_refs[d][:, _head_cols(h, part)] = grads[part]
                for ref, g in zip(df_refs, grads[3:10]):
                    ref[d, h] = g
                ds_ref[d, h] = grads[10]

    shp = lambda a: jax.ShapeDtypeStruct(a.shape, F32)
    res = pl.pallas_call(
        body, name="gdn_b", grid=(nc,),
        in_specs=[*tok, per_tok, per_tok, per_tok, per_tok, mat, mat, row, state, mat, *out],
        out_specs=[*tok, per_tok, per_tok, per_tok, per_tok, mat, mat, row],
        out_shape=[shp(qkv), shp(qkv)] + [shp(a) for a in factors],
        scratch_shapes=[pltpu.VMEM((_N_DIR, GDN_HEADS, HEAD_DIM, HEAD_DIM), F32)],
        compiler_params=pltpu.CompilerParams(dimension_semantics=("arbitrary",), vmem_limit_bytes=VMEM_LIMIT),
    )(qkv, qkv, *factors, sall, tall, *dos)
    return res[0] + res[1], tuple(res[_N_DIR:])


@functools.partial(jax.custom_vjp, nondiff_argnums=(0,))
def gdn_scan(ncc, qkv, factors):
    o0, o1, _, _ = _gdn_fwd_call(ncc, qkv, factors)
    return o0, o1


def _gdn_scan_fwd(ncc, qkv, factors):
    o0, o1, sall, tall = _gdn_fwd_call(ncc, qkv, factors)
    return (o0, o1), (qkv, factors, sall, tall)


def _gdn_scan_bwd(ncc, res, dos):
    qkv, factors, sall, tall = res
    return _gdn_bwd_call(ncc, qkv, factors, sall, tall, list(dos))


gdn_scan.defvjp(_gdn_scan_fwd, _gdn_scan_bwd)


def _rope_tables(n, cl):
    t = np.arange(n)
    inv_freq = (ROPE_THETA ** (-np.arange(0, HEAD_DIM // 2, 2, dtype=np.float32) / (HEAD_DIM // 2))).astype(np.float32)
    ang_r = (t // GRID_W).astype(np.float32)[:, None] * inv_freq
    ang_c = (t % GRID_W).astype(np.float32)[:, None] * inv_freq
    cos = np.concatenate([np.cos(ang_r), np.cos(ang_r), np.cos(ang_c), np.cos(ang_c)], axis=1)
    sin = np.concatenate([-np.sin(ang_r), np.sin(ang_r), -np.sin(ang_c), np.sin(ang_c)], axis=1)
    cos_all = np.concatenate([np.ones((cl, HEAD_DIM), np.float32), cos], axis=0)
    sin_all = np.concatenate([np.zeros((cl, HEAD_DIM), np.float32), sin], axis=0)
    j = np.arange(HEAD_DIM)
    src = np.where((j % 64) < 32, j + 32, j - 32)
    perm = np.zeros((HEAD_DIM, HEAD_DIM), np.float32)
    perm[src, j] = 1.0
    return (jnp.asarray(cos.astype(np.float32)), jnp.asarray(sin.astype(np.float32)),
            jnp.asarray(cos_all), jnp.asarray(sin_all), jnp.asarray(perm))


def _gdn_factors(log_a, beta, ncc):
    t = log_a.shape[0]
    nc = t // _C
    la = log_a.reshape(nc, _C, _N_DIR, GDN_HEADS).transpose(2, 3, 0, 1)
    be = beta.reshape(nc, _C, _N_DIR, GDN_HEADS).transpose(2, 3, 0, 1)
    scan_order = lambda a: jnp.stack([a[0], jnp.concatenate([jnp.flip(a[1][:, :ncc], axis=1),
                                                              jnp.flip(a[1][:, ncc:], axis=1)], axis=1)])
    la, be = scan_order(la), scan_order(be)
    rev = jnp.asarray(np.array([False, True])[:, None, None, None])
    run = jnp.cumsum(la, axis=3)
    gam = jnp.where(rev, jnp.sum(la, axis=3, keepdims=True) - run + la, run)
    idx = np.arange(_C)
    incl = jnp.asarray(np.stack([idx[:, None] >= idx[None, :], idx[:, None] <= idx[None, :]])[:, None, None])
    strict = jnp.asarray(np.stack([idx[:, None] > idx[None, :], idx[:, None] < idx[None, :]])[:, None, None])
    dec = jnp.exp(jnp.where(incl, gam[..., :, None] - gam[..., None, :], -jnp.inf))
    m1 = jnp.where(strict, be[..., :, None] * dec, 0.0)
    e = jnp.exp(gam)
    g_last = jnp.where(rev, gam[..., :1], gam[..., -1:])
    lanes = lambda a: jnp.broadcast_to(a.reshape(_N_DIR, GDN_HEADS, t, 1), (_N_DIR, GDN_HEADS, t, HEAD_DIM))
    gl = jnp.broadcast_to(jnp.exp(g_last)[..., None], (_N_DIR, GDN_HEADS, nc, 1, HEAD_DIM))
    return lanes(be), lanes(be * e), lanes(e), lanes(jnp.exp(g_last - gam)), m1, dec, gl


def local_loss(x, wz, wb, ws, c, ctx, target):
    return channel_mixing(token_mixing(x, wz, wb, ws, c, ctx), wz, wb, ws, target)


def token_mixing(x, wz, wb, ws, c, ctx):
    n, cl = x.shape[0], ctx.shape[0]
    cos_q, sin_q, cos_k, sin_k, perm = _rope_tables(n, cl)

    sc_in = jnp.concatenate([jax.nn.silu(c), jax.nn.silu(ws["c_ctx"])[None, :], jnp.zeros((14, D_MODEL), F32)], axis=0)
    mod = pmm(sc_in, wb["w_mod"], wz["w_mod"], "mm_mod") + ws["b_mod"]
    sh1, sc1, g1, sh2, sc2, g2 = [mod[0:1, i * D_MODEL:(i + 1) * D_MODEL] for i in range(6)]
    csh1, csc1 = mod[1:2, 0:D_MODEL], mod[1:2, D_MODEL:2 * D_MODEL]

    def norm_mod(a, sh, sc):
        return (_rms(a) * (1.0 + sc) + sh,)

    (hx,) = rowop(norm_mod, "normmod_x", (x,), (sh1, sc1))
    (hc,) = rowop(norm_mod, "normmod_c", (ctx,), (csh1, csc1))
    h_all = jnp.concatenate([hc, hx], axis=0)
    p_main = pmm_t(h_all, wb["w_in_main"], wz["w_in_main"], "mm_in")
    p_small = pmm_t(h_all, wb["w_in_small"], wz["w_in_small"], "mm_ins")
    ak, av, qkv, aq, z, gate = jnp.split(p_main, [KV_W, 2 * KV_W, SMALL_AT, SMALL_AT + Q_W, SMALL_AT + Q_W + GDN_W],
                                         axis=1)
    db, da = p_small[:, :2 * GDN_HEADS], p_small[:, 2 * GDN_HEADS:4 * GDN_HEADS]

    def qk_prep(nh):
        def fn(a, w, cos, sin, pm):
            outs = []
            for ah in _heads(a, nh):
                y = _rms(ah) * w
                outs.append(y * cos + _pdot(y, pm) * sin)
            return (jnp.concatenate(outs, axis=1),)
        return fn

    (q_x,) = rowop(qk_prep(ATTN_HEADS), "q_prep", (aq[cl:],), (ws["q_norm_w"],), (cos_q, sin_q), (perm,))
    (k_all,) = rowop(qk_prep(ATTN_KV_HEADS), "k_prep", (ak,), (ws["k_norm_w"],), (cos_k, sin_k), (perm,))
    attn_x = attention(q_x, k_all, av)

    cw = ws["conv_qkv_w"]
    normed = jnp.asarray(np.repeat([1.0, 1.0, 0.0], GDN_W)[None, :], F32)
    scale = jnp.asarray(np.repeat([HEAD_DIM ** -0.5, 1.0, 1.0], GDN_W)[None, :], F32)

    def gdn_prep(a, w0, w1, w2, nf, sc):
        s = jax.nn.silu(_conv3(a, w0, w1, w2, (0, cl)))
        inv = lax.rsqrt(jnp.sum(s * s, axis=-1, keepdims=True) + NORM_EPS)
        return (s * jnp.where(nf > 0.0, inv * sc, 1.0),)

    (qkvn,) = colop(gdn_prep, "gdn_prep", (qkv, cw[0:1], cw[1:2], cw[2:3], normed, scale),
                    [(i, 0) for i in range(6)], 2, 3 * GDN_HEADS)
    beta = jax.nn.sigmoid(db).reshape(-1, 2, GDN_HEADS)
    log_a = -jnp.exp(ws["a_log"])[None] * jax.nn.softplus(da.reshape(-1, 2, GDN_HEADS) + ws["dt_bias"][None])
    o_fwd, o_rev = gdn_scan(cl // _C, qkvn, _gdn_factors(log_a, beta, cl // _C))
    o_x = o_fwd + o_rev

    def gdn_out(o, zz, w):
        outs = [_rms(oh) * w * jax.nn.silu(zh) for oh, zh in zip(_heads(o, GDN_HEADS), _heads(zz, GDN_HEADS))]
        return (jnp.concatenate(outs, axis=1),)

    (gdn_x,) = rowop(gdn_out, "gdn_out", (o_x, z[cl:]), (ws["gdn_norm_w"],))
    return dict(x=x, attn_x=attn_x, gdn_x=gdn_x, gate=gate[cl:], g1=g1, sh2=sh2, sc2=sc2, g2=g2)


def channel_mixing(mixed, wz, wb, ws, target):
    x, attn_x, gdn_x, gate = mixed["x"], mixed["attn_x"], mixed["gdn_x"], mixed["gate"]
    g1, sh2, sc2, g2 = mixed["g1"], mixed["sh2"], mixed["sc2"], mixed["g2"]
    pa = pmm(attn_x, wb["w_pa"], wz["w_pa"], "mm_pa")
    pd = pmm(gdn_x, wb["w_pd"], wz["w_pd"], "mm_pd")

    def merge(a, d, g):
        return (jax.nn.sigmoid(g[:, :D_MODEL]) * a + jax.nn.sigmoid(g[:, D_MODEL:]) * d,)

    (y,) = rowop(merge, "merge", (pa, pd, gate))
    mo = pmm(y, wb["w_out"], wz["w_out"], "mm_out")

    def res_norm_mod(xx, m, g, sh, sc):
        x1 = xx + g * m
        return x1, _rms(x1) * (1.0 + sc) + sh

    x1, h2 = rowop(res_norm_mod, "res1", (x, mo), (g1, sh2, sc2))
    up = pmm(h2, wb["w_up"], wz["w_up"], "mm_up")
    fw = ws["ffn_conv_w"]

    def ffn_act(ug, uv, w0g, w0v, w1g, w1v, w2g, w2v, bg, bv):
        g = _conv3(ug, w0g, w1g, w2g, (0,)) + bg
        v = _conv3(uv, w0v, w1v, w2v, (0,)) + bv
        return (jax.nn.silu(g) * v,)

    half = D_FF // HEAD_DIM
    (act,) = colop(ffn_act, "ffn_act", (up, fw[0:1], fw[1:2], fw[2:3], ws["ffn_conv_b"]),
                   [(i, off) for i in range(5) for off in (0, half)], 0, half)
    dn = pmm(act, wb["w_down"], wz["w_down"], "mm_down")

    def head(xx, m, g, w, tgt):
        yy = _rms(xx + g * m) * w
        err = (yy - tgt) ** 2
        return (jnp.broadcast_to(0.5 * jnp.mean(err, axis=-1, keepdims=True), (xx.shape[0], HEAD_DIM)),)

    (row_loss,) = rowop(head, "head", (x1, dn), (g2, ws["final_norm_w"][None, :]), (target,))
    return jnp.sum(row_loss[:, 0])


_HBM = pl.BlockSpec(memory_space=pltpu.HBM)


def _chip_peers():
    x, y = lax.axis_index("x"), lax.axis_index("y")
    return [(1 - x, y), (x, 1 - y), (1 - x, 1 - y)]


_SPLIT_COLS = ("w_in",)


def _half_of(view, nm, idx, lead=0):
    r, cdim = view.shape[-2:]
    pre = (slice(None),) * lead
    if nm in _SPLIT_COLS:
        return view.at[pre + (slice(None), pl.ds(pl.multiple_of(idx * (cdim // 2), 128), cdim // 2))]
    return view.at[pre + (pl.ds(pl.multiple_of(idx * (r // 2), 16), r // 2), slice(None))]


def _remote(src, dst, send_sem, recv_sem, dev):
    return pltpu.make_async_remote_copy(src_ref=src, dst_ref=dst, send_sem=send_sem, recv_sem=recv_sem,
                                        device_id=dev, device_id_type=MESH)


def _hbm_call(body, name, ins, out_shape, n_sems, in_place=False):
    names = tuple(ins)
    return dict(zip(names, pl.pallas_call(
        body, name=name, in_specs=[_HBM] * len(names), out_specs=[_HBM] * len(names),
        out_shape=[out_shape(nm, ins[nm]) for nm in names],
        scratch_shapes=[pltpu.SemaphoreType.DMA((k,)) for k in n_sems],
        input_output_aliases={i: i for i in range(len(names))} if in_place else {},
    )(*[ins[nm] for nm in names])))


def all_gather_chips(shards):
    names = tuple(shards)
    n = len(names)

    def body(*refs):
        ins, outs = dict(zip(names, refs[:n])), dict(zip(names, refs[n:2 * n]))
        ici_send, ici_recv, d2d_send, d2d_recv, own_send, own_recv = refs[2 * n:]
        x, y, c = lax.axis_index("x"), lax.axis_index("y"), lax.axis_index("c")
        me, sib = 2 * x + y, (x, y, 1 - c)
        own = [_remote(ins[nm], outs[nm].at[me], own_send.at[i], own_recv.at[i], sib) for i, nm in enumerate(names)]
        for cp in own:
            cp.start()
        sends = []
        for k, (px, py) in enumerate(_chip_peers()):
            for i, nm in enumerate(names):
                cp = _remote(_half_of(ins[nm], nm, c), _half_of(outs[nm].at[me], nm, c), ici_send.at[k * n + i],
                             ici_recv.at[k * n + i], (px, py, c))
                cp.start()
                sends.append(cp)
        for k, (px, py) in enumerate(_chip_peers()):
            for i, nm in enumerate(names):
                landed = _half_of(outs[nm].at[2 * px + py], nm, c)
                _remote(landed, landed, ici_send.at[k * n + i], ici_recv.at[k * n + i], (px, py, c)).wait_recv()
                fw = _remote(landed, landed, d2d_send.at[k * n + i], d2d_recv.at[k * n + i], sib)
                fw.start()
                sends.append(fw)
        for k, (px, py) in enumerate(_chip_peers()):
            for i, nm in enumerate(names):
                other = _half_of(outs[nm].at[2 * px + py], nm, 1 - c)
                _remote(other, other, d2d_send.at[k * n + i], d2d_recv.at[k * n + i], sib).wait_recv()
        for cp in sends:
            cp.wait_send()
        for cp in own:
            cp.wait()

    return _hbm_call(body, "ag_weights", shards, lambda nm, a: jax.ShapeDtypeStruct((N_CHIPS,) + a.shape, a.dtype),
                     (3 * n, 3 * n, 3 * n, 3 * n, n, n))


_SEM = pl.BlockSpec(memory_space=pltpu.SEMAPHORE)


def push_start(name, arrays, land_shapes, copies, n_copies):
    names = tuple(arrays)
    n = len(names)

    def body(*refs):
        send_sems, recv_sems, token = refs[2 * n], refs[2 * n + 1], refs[-1]
        for j, (src, dst, dev) in enumerate(copies(refs[:n], refs[n:2 * n])):
            _remote(src, dst, send_sems.at[j], recv_sems.at[j], dev).start()
        token[...] = jnp.zeros_like(token)

    hbm = lambda a: pltpu.with_memory_space_constraint(a, pltpu.HBM)
    lands = [lax.empty(land_shapes[nm], arrays[nm].dtype) for nm in names]
    res = pl.pallas_call(
        body, name=name,
        out_shape=(pltpu.SemaphoreType.DMA((n_copies,)), pltpu.SemaphoreType.DMA((n_copies,)),
                   *[pltpu.HBM(arrays[nm].shape, arrays[nm].dtype) for nm in names],
                   *[pltpu.HBM(a.shape, a.dtype) for a in lands], jax.ShapeDtypeStruct((8, 128), F32)),
        in_specs=[_HBM] * (2 * n),
        out_specs=(_SEM, _SEM, *[_HBM] * (2 * n), pl.BlockSpec(memory_space=pltpu.VMEM)),
        input_output_aliases={i: 2 + i for i in range(2 * n)},
        compiler_params=pltpu.CompilerParams(has_side_effects=pltpu.SideEffectType.DATAFLOW_SIDE_EFFECTING),
    )(*[hbm(arrays[nm]) for nm in names], *[hbm(a) for a in lands])
    return names, res[0], res[1], res[2:2 + n], res[2 + n:2 + 2 * n], res[-1]


def push_wait(name, started, copies, after):
    names, send_sems, recv_sems, srcs, lands, _ = started
    n = len(names)

    def body(*refs):
        send_ref, recv_ref = refs[2 * n], refs[2 * n + 1]
        for j, (src, dst, dev) in enumerate(copies(refs[:n], refs[n:2 * n])):
            cp = _remote(src, dst, send_ref.at[j], recv_ref.at[j], dev)
            cp.wait_send()
            cp.wait_recv()

    res = pl.pallas_call(
        body, name=name,
        out_shape=(*[pltpu.HBM(a.shape, a.dtype) for a in srcs], *[pltpu.HBM(a.shape, a.dtype) for a in lands]),
        in_specs=[_HBM] * (2 * n) + [_SEM, _SEM, pl.BlockSpec(memory_space=pl.ANY)],
        out_specs=tuple([_HBM] * (2 * n)),
        input_output_aliases={i: i for i in range(2 * n)},
        compiler_params=pltpu.CompilerParams(has_side_effects=pltpu.SideEffectType.DATAFLOW_SIDE_EFFECTING),
    )(*srcs, *lands, send_sems, recv_sems, after)
    return dict(zip(names, res[n:]))


def _gather_copies(srcs, lands):
    x, y, c = lax.axis_index("x"), lax.axis_index("y"), lax.axis_index("c")
    devs = [(px, py, c) for px, py in _chip_peers()] + [(x, y, 1 - c)]
    return [(src, land.at[2 * x + y], dev) for src, land in zip(srcs, lands) for dev in devs]


def _scatter_copies(srcs, lands):
    c = lax.axis_index("c")
    return [(src.at[2 * px + py], land.at[k], (px, py, c))
            for src, land in zip(srcs, lands) for k, (px, py) in enumerate(_chip_peers())]


def sibling_halves(blocks, name):
    names = tuple(blocks)

    def body(*refs):
        n = len(names)
        ins, outs = dict(zip(names, refs[:n])), dict(zip(names, refs[n:2 * n]))
        send_sems, recv_sems = refs[2 * n:]
        x, y, c = lax.axis_index("x"), lax.axis_index("y"), lax.axis_index("c")
        cps = [_remote(_half_of(ins[nm], nm, 1 - c, lead=1), outs[nm], send_sems.at[i], recv_sems.at[i], (x, y, 1 - c))
               for i, nm in enumerate(names)]
        for cp in cps:
            cp.start()
        for cp in cps:
            cp.wait()

    def half_shape(nm, a):
        r, cdim = a.shape[-2:]
        return jax.ShapeDtypeStruct((N_CHIPS, r, cdim // 2) if nm in _SPLIT_COLS else (N_CHIPS, r // 2, cdim), a.dtype)

    return _hbm_call(body, name, blocks, half_shape, (len(names), len(names)))


def scatter_halves(blocks):
    names = tuple(blocks)
    n = len(names)

    def body(*refs):
        ins, outs = dict(zip(names, refs[:n])), dict(zip(names, refs[n:2 * n]))
        send_sems, recv_sems = refs[2 * n:]
        c = lax.axis_index("c")
        cps = [_remote(ins[nm].at[2 * px + py], outs[nm].at[k], send_sems.at[k * n + i], recv_sems.at[k * n + i],
                       (px, py, c))
               for k, (px, py) in enumerate(_chip_peers()) for i, nm in enumerate(names)]
        for cp in cps:
            cp.start()
        for cp in cps:
            cp.wait_recv()
        for cp in cps:
            cp.wait_send()

    return _hbm_call(body, "rs_grads", blocks, lambda nm, a: jax.ShapeDtypeStruct((3,) + a.shape[1:], a.dtype),
                     (3 * n, 3 * n))


def sibling_assemble(arrays, name):
    names = tuple(arrays)

    def body(*refs):
        n = len(names)
        ins, outs = dict(zip(names, refs[:n])), dict(zip(names, refs[n:2 * n]))
        send_sems, recv_sems = refs[2 * n:]
        x, y, c = lax.axis_index("x"), lax.axis_index("y"), lax.axis_index("c")
        cps = [_remote(_half_of(ins[nm], nm, c), _half_of(outs[nm], nm, c), send_sems.at[i], recv_sems.at[i],
                       (x, y, 1 - c)) for i, nm in enumerate(names)]
        for cp in cps:
            cp.start()
        for i, nm in enumerate(names):
            other = _half_of(outs[nm], nm, 1 - c)
            _remote(other, other, send_sems.at[i], recv_sems.at[i], (x, y, 1 - c)).wait_recv()
        for cp in cps:
            cp.wait_send()

    return _hbm_call(body, name, arrays, lambda nm, a: jax.ShapeDtypeStruct(a.shape, a.dtype),
                     (len(names), len(names)), in_place=True)


def all_reduce_small(v):
    def body(v_ref, tot_ref, gath_ref, send_sems, recv_sems):
        x, y, c = lax.axis_index("x"), lax.axis_index("y"), lax.axis_index("c")
        me = 4 * x + 2 * y + c
        gath_ref[me] = v_ref[...]

        def peer(k):
            m = k + 1
            return (x ^ (m >> 2 & 1), y ^ (m >> 1 & 1), c ^ (m & 1))

        sends = [pltpu.make_async_remote_copy(src_ref=v_ref, dst_ref=gath_ref.at[me], send_sem=send_sems.at[k],
                                              recv_sem=recv_sems.at[k], device_id=peer(k), device_id_type=MESH)
                 for k in range(N_DEV - 1)]
        for cp in sends:
            cp.start()
        for k in range(N_DEV - 1):
            px, py, pc = peer(k)
            pltpu.make_async_remote_copy(src_ref=v_ref, dst_ref=gath_ref.at[4 * px + 2 * py + pc],
                                         send_sem=send_sems.at[k], recv_sem=recv_sems.at[k], device_id=peer(k),
                                         device_id_type=MESH).wait_recv()
        for cp in sends:
            cp.wait_send()
        acc = gath_ref[0]
        for d in range(1, N_DEV):
            acc = acc + gath_ref[d]
        tot_ref[...] = acc

    vm = pl.BlockSpec(memory_space=pltpu.VMEM)
    return pl.pallas_call(
        body, name="ar_small", in_specs=[vm], out_specs=[vm, vm],
        out_shape=(jax.ShapeDtypeStruct(v.shape, v.dtype), jax.ShapeDtypeStruct((N_DEV,) + v.shape, v.dtype)),
        scratch_shapes=[pltpu.SemaphoreType.DMA((N_DEV - 1,)), pltpu.SemaphoreType.DMA((N_DEV - 1,))],
    )(v)[0]


def _elementwise(fn, name, ins, n_out, out_dtype=F32):
    r, cdim = ins[0].shape
    tr = _pick(r, tuple(p for p in (488, 256, 128, 104, 64, 32, 16, 8) if p * cdim * 4 <= 2 * 1024 * 1024))
    spec = pl.BlockSpec((tr, cdim), lambda i: (i, 0))

    def body(*refs):
        res = fn(*[ref[...] for ref in refs[:len(ins)]])
        for o_ref, v in zip(refs[len(ins):], res):
            o_ref[...] = v

    return pl.pallas_call(
        body, name=name, grid=(r // tr,), in_specs=[spec] * len(ins), out_specs=[spec] * n_out,
        out_shape=tuple(jax.ShapeDtypeStruct((r, cdim), out_dtype) for _ in range(n_out)),
        compiler_params=pltpu.CompilerParams(dimension_semantics=("parallel",), vmem_limit_bytes=VMEM_LIMIT),
    )(*ins)


def _half_block_specs(nm, shard_shape):
    r, cdim = shard_shape
    if nm in _SPLIT_COLS:
        return (None, r, cdim // 2), (lambda j, c: (j, 0, c))
    return (None, r // 2, cdim), (lambda j, c: (j, c, 0))


def _presum(nm, sel, g32, a):
    blk, at = _half_block_specs(nm, g32.shape[1:])

    def body(s_ref, g_ref, a_ref, o_ref):
        del s_ref
        o_ref[...] = (g_ref[...] + a_ref[...]).astype(BF16)

    return pl.pallas_call(
        body, name="rs_presum_" + nm,
        grid_spec=pltpu.PrefetchScalarGridSpec(
            num_scalar_prefetch=1, grid=(N_CHIPS,),
            in_specs=[pl.BlockSpec(blk, lambda j, s: at(j, s[0])), pl.BlockSpec(blk, lambda j, s: (j, 0, 0))],
            out_specs=pl.BlockSpec(blk, lambda j, s: (j, 0, 0))),
        out_shape=jax.ShapeDtypeStruct(a.shape, BF16),
        compiler_params=pltpu.CompilerParams(dimension_semantics=("parallel",), vmem_limit_bytes=VMEM_LIMIT),
    )(sel, g32, a)


def _finalsum(nm, sel, g32, a, got):
    blk, at = _half_block_specs(nm, g32.shape[1:])

    def body(s_ref, g_ref, a_ref, r_ref, o_ref):
        del s_ref
        acc = g_ref[...] + a_ref[...]
        for k in range(3):
            acc = acc + r_ref[k].astype(F32)
        o_ref[...] = acc

    return pl.pallas_call(
        body, name="rs_final_" + nm,
        grid_spec=pltpu.PrefetchScalarGridSpec(
            num_scalar_prefetch=1, grid=(1,),
            in_specs=[pl.BlockSpec(blk, lambda i, s: at(s[1], s[0])), pl.BlockSpec(blk, lambda i, s: (s[1], 0, 0)),
                      pl.BlockSpec(got.shape, lambda i, s: (0, 0, 0))],
            out_specs=pl.BlockSpec(blk[1:], lambda i, s: at(0, s[0])[1:])),
        out_shape=jax.ShapeDtypeStruct(g32.shape[1:], F32),
        compiler_params=pltpu.CompilerParams(dimension_semantics=("arbitrary",), vmem_limit_bytes=VMEM_LIMIT),
    )(sel, g32, a, got)


def _adamw(w, g, m, v, name):
    shape = w.shape
    to2 = lambda a: a.reshape(-1, shape[-1])

    def fn(w_, g_, m_, v_):
        m_new = ADAM_B1 * m_ + (1.0 - ADAM_B1) * g_
        v_new = ADAM_B2 * v_ + (1.0 - ADAM_B2) * (g_ * g_)
        m_hat = m_new / (1.0 - ADAM_B1 ** ADAM_STEP)
        v_hat = v_new / (1.0 - ADAM_B2 ** ADAM_STEP)
        delta = -ADAM_LR * (m_hat / (jnp.sqrt(v_hat) + ADAM_EPS) + ADAM_WD * w_)
        return g_, delta, m_new, v_new

    outs = _elementwise(fn, name, [to2(a) for a in (w, g, m, v)], 4)
    return tuple(o.reshape(shape) for o in outs)


_BIG = ("w_mod", "w_in", "w_pa", "w_pd", "w_out", "w_up", "w_down")
_EARLY = ("w_mod", "w_in")
_LATE = ("w_pa", "w_pd", "w_out", "w_up", "w_down")
_COL_SHARDED = ("w_mod", "w_up")
_FULL_SHAPE = {"w_mod": (D_MODEL, MOD_W), "w_in": (IN_COLS, D_MODEL), "w_pa": (Q_W, D_MODEL), "w_pd": (GDN_W, D_MODEL),
               "w_out": (D_MODEL, D_MODEL), "w_up": (D_MODEL, 2 * D_FF), "w_down": (D_FF, D_MODEL)}


def _shard_shape(name):
    r, cdim = _FULL_SHAPE[name]
    return (r, cdim // N_CHIPS) if name in _COL_SHARDED else (r // N_CHIPS, cdim)


_CONV_ELEMS = 2 * (3 * CONV_W // N_CHIPS + 3 * 2 * D_FF // N_CHIPS)
_CONV_ROWS = 32


def _blocks_of_full(name, full):
    r, cdim = _FULL_SHAPE[name]
    if name in _COL_SHARDED:
        return full.reshape(r, N_CHIPS, cdim // N_CHIPS).transpose(1, 0, 2)
    return full.reshape(N_CHIPS, r // N_CHIPS, cdim)


def _full_of_blocks(name, blocks):
    r, cdim = _FULL_SHAPE[name]
    if name in _COL_SHARDED:
        return blocks.transpose(1, 0, 2).reshape(r, cdim)
    return blocks.reshape(r, cdim)


def _w_in_regroup(w_in_t):
    main = jnp.concatenate([w_in_t[:SMALL_AT], w_in_t[SMALL_AT + 4 * GDN_HEADS:]], axis=0)
    small = jnp.pad(w_in_t[SMALL_AT:SMALL_AT + 4 * GDN_HEADS], ((0, HEAD_DIM - 4 * GDN_HEADS), (0, 0)))
    return main, small


def _w_in_ungroup(main, small):
    return jnp.concatenate([main[:SMALL_AT], small[:4 * GDN_HEADS], main[SMALL_AT:]], axis=0)


_SMALL = ("c_ctx", "b_mod", "q_norm_w", "k_norm_w", "conv_qkv_w", "a_log", "dt_bias", "gdn_norm_w", "ffn_conv_w",
          "ffn_conv_b", "final_norm_w")


def _pack_small(tree, rows):
    flat = jnp.concatenate([tree[nm].reshape(-1) for nm in _SMALL])
    return jnp.pad(flat, (0, rows * 128 - flat.shape[0])).reshape(rows, 128)


def _unpack_small(packed, like):
    flat, out, off = packed.reshape(-1), {}, 0
    for nm in _SMALL:
        size = int(np.prod(like[nm].shape))
        out[nm] = flat[off:off + size].reshape(like[nm].shape)
        off += size
    return out


def kernel(x, c, ctx, c_ctx, w_mod, b_mod, w_in, q_norm_w, k_norm_w, conv_qkv_w, a_log, dt_bias, gdn_norm_w, w_pa, w_pd, w_out, w_up, ffn_conv_w, ffn_conv_b, w_down, final_norm_w, loss_target, m_c_ctx, m_w_mod, m_b_mod, m_w_in, m_q_norm_w, m_k_norm_w, m_conv_qkv_w, m_a_log, m_dt_bias, m_gdn_norm_w, m_w_pa, m_w_pd, m_w_out, m_w_up, m_ffn_conv_w, m_ffn_conv_b, m_w_down, m_final_norm_w, v_c_ctx, v_w_mod, v_b_mod, v_w_in, v_q_norm_w, v_k_norm_w, v_conv_qkv_w, v_a_log, v_dt_bias, v_gdn_norm_w, v_w_pa, v_w_pd, v_w_out, v_w_up, v_ffn_conv_w, v_ffn_conv_b, v_w_down, v_final_norm_w):
    names = ("c_ctx", "w_mod", "b_mod", "w_in", "q_norm_w", "k_norm_w", "conv_qkv_w", "a_log", "dt_bias", "gdn_norm_w",
             "w_pa", "w_pd", "w_out", "w_up", "ffn_conv_w", "ffn_conv_b", "w_down", "final_norm_w")
    w_sh = dict(c_ctx=c_ctx, w_mod=w_mod, b_mod=b_mod, w_in=w_in, q_norm_w=q_norm_w, k_norm_w=k_norm_w,
                conv_qkv_w=conv_qkv_w, a_log=a_log, dt_bias=dt_bias, gdn_norm_w=gdn_norm_w, w_pa=w_pa, w_pd=w_pd,
                w_out=w_out, w_up=w_up, ffn_conv_w=ffn_conv_w, ffn_conv_b=ffn_conv_b, w_down=w_down,
                final_norm_w=final_norm_w)
    m_sh = dict(c_ctx=m_c_ctx, w_mod=m_w_mod, b_mod=m_b_mod, w_in=m_w_in, q_norm_w=m_q_norm_w, k_norm_w=m_k_norm_w,
                conv_qkv_w=m_conv_qkv_w, a_log=m_a_log, dt_bias=m_dt_bias, gdn_norm_w=m_gdn_norm_w, w_pa=m_w_pa,
                w_pd=m_w_pd, w_out=m_w_out, w_up=m_w_up, ffn_conv_w=m_ffn_conv_w, ffn_conv_b=m_ffn_conv_b,
                w_down=m_w_down, final_norm_w=m_final_norm_w)
    v_sh = dict(c_ctx=v_c_ctx, w_mod=v_w_mod, b_mod=v_b_mod, w_in=v_w_in, q_norm_w=v_q_norm_w, k_norm_w=v_k_norm_w,
                conv_qkv_w=v_conv_qkv_w, a_log=v_a_log, dt_bias=v_dt_bias, gdn_norm_w=v_gdn_norm_w, w_pa=v_w_pa,
                w_pd=v_w_pd, w_out=v_w_out, w_up=v_w_up, ffn_conv_w=v_ffn_conv_w, ffn_conv_b=v_ffn_conv_b,
                w_down=v_w_down, final_norm_w=v_final_norm_w)
    chip = 2 * lax.axis_index("x") + lax.axis_index("y")

    conv_bits = jnp.concatenate([lax.bitcast_convert_type(w_sh[nm][0], BF16).reshape(-1)
                                 for nm in ("conv_qkv_w", "ffn_conv_w")])
    shards = {nm: w_sh[nm][0].astype(BF16).T if nm == "w_in" else w_sh[nm][0].astype(BF16) for nm in _BIG}
    shards["conv"] = jnp.pad(conv_bits, (0, _CONV_ROWS * D_MODEL - _CONV_ELEMS)).reshape(_CONV_ROWS, D_MODEL)
    gathered = all_gather_chips({nm: shards[nm] for nm in _EARLY + ("conv",)})
    gathered, late_shards = lax.optimization_barrier((gathered, {nm: shards[nm] for nm in _LATE}))
    started = push_start("ag_late_start", late_shards, {nm: (N_CHIPS,) + a.shape for nm, a in late_shards.items()},
                         _gather_copies, 4 * len(_LATE))
    c = c + started[-1][0:1, 0:1]

    wb = {nm: _full_of_blocks(nm, gathered[nm]) for nm in _EARLY}
    wb["w_in_main"], wb["w_in_small"] = _w_in_regroup(wb.pop("w_in"))
    conv_all = gathered["conv"].reshape(N_CHIPS, -1)[:, :_CONV_ELEMS]
    n_cq = 2 * 3 * CONV_W // N_CHIPS
    unbits = lambda a, w: lax.bitcast_convert_type(a.reshape(N_CHIPS, 3, w // N_CHIPS, 2), F32).transpose(1, 0, 2).reshape(3, w)
    ws = dict(c_ctx=c_ctx, b_mod=b_mod, q_norm_w=q_norm_w, k_norm_w=k_norm_w, a_log=a_log[0], dt_bias=dt_bias[0],
              gdn_norm_w=gdn_norm_w, ffn_conv_b=ffn_conv_b, final_norm_w=final_norm_w,
              conv_qkv_w=unbits(conv_all[:, :n_cq], CONV_W), ffn_conv_w=unbits(conv_all[:, n_cq:], 2 * D_FF))
    wz = {nm: jnp.zeros(a.shape, F32) for nm, a in wb.items()}
    wz.update({nm: jnp.zeros(_FULL_SHAPE[nm], F32) for nm in _LATE})

    mixed, vjp_mix = jax.vjp(lambda x_, wz_, ws_: token_mixing(x_, wz_, wb, ws_, c, ctx[0]), x[0], wz, ws)
    got = push_wait("ag_late_wait", started, _gather_copies, mixed["gdn_x"])
    wb_late = {nm: _full_of_blocks(nm, got[nm]) for nm in _LATE}
    loss_local, vjp_chan = jax.vjp(
        lambda mixed_, wz_, ws_: channel_mixing(mixed_, wz_, wb_late, ws_, loss_target[0]), mixed, wz, ws)
    d_mixed, gz_chan, gs_chan = vjp_chan(jnp.ones((), F32))

    sel = jnp.stack([lax.axis_index("c"), chip]).astype(jnp.int32)
    g32_late = {nm: _blocks_of_full(nm, gz_chan[nm]) for nm in _LATE}
    theirs_late = sibling_halves(g32_late, "rs_sibling_late")
    sums_late = {nm: _presum(nm, sel, g32_late[nm], theirs_late[nm]) for nm in _LATE}
    scattering = push_start("rs_late_start", sums_late, {nm: (3,) + a.shape[1:] for nm, a in sums_late.items()},
                            _scatter_copies, 3 * len(_LATE))
    d_mixed = {**d_mixed, "gdn_x": d_mixed["gdn_x"] + scattering[-1][0:1, 0:1]}
    gx, gz_mix, gs_mix = vjp_mix(d_mixed)
    gs = jax.tree.map(jnp.add, gs_mix, gs_chan)
    got_late = push_wait("rs_late_wait", scattering, _scatter_copies, gx)
    loss = lax.psum(loss_local, ("x", "y", "c"))

    gs["a_log"], gs["dt_bias"] = gs["a_log"][None], gs["dt_bias"][None]
    like = {nm: gs[nm] for nm in _SMALL}
    small_rows = -(-sum(int(np.prod(like[nm].shape)) for nm in _SMALL) // 1024) * 8
    small_sum = all_reduce_small(_pack_small(gs, small_rows))
    gz_mix["w_mod"], small_sum = lax.optimization_barrier((gz_mix["w_mod"], small_sum))
    g_small = _unpack_small(small_sum, like)
    for nm, width in (("conv_qkv_w", CONV_W), ("ffn_conv_w", 2 * D_FF)):
        g_small[nm] = lax.dynamic_slice_in_dim(g_small[nm], chip * (width // N_CHIPS), width // N_CHIPS, axis=1)[None]

    gz_mix["w_in"] = _w_in_ungroup(gz_mix.pop("w_in_main"), gz_mix.pop("w_in_small"))
    g32 = {nm: _blocks_of_full(nm, gz_mix[nm]) for nm in _EARLY}
    theirs = sibling_halves(g32, "rs_sibling")
    sums = {nm: _presum(nm, sel, g32[nm], theirs[nm]) for nm in _EARLY}
    scattering = push_start("rs_early_start", sums, {nm: (3,) + a.shape[1:] for nm, a in sums.items()},
                            _scatter_copies, 3 * len(_EARLY))
    zero = scattering[-1][0:1, 0:1]
    got_late = {nm: a + zero.astype(BF16) if nm in ("w_up", "w_down") else a for nm, a in got_late.items()}
    g_big = sibling_assemble({nm: _finalsum(nm, sel, g32_late[nm], theirs_late[nm], got_late[nm]) for nm in _LATE},
                             "rs_assemble_late")
    grads, deltas, new_m, new_v = {}, {}, {}, {}

    def adamw_big(nm):
        g = g_big[nm].T if nm == "w_in" else g_big[nm]
        grads[nm], deltas[nm], new_m[nm], new_v[nm] = (
            o[None] for o in _adamw(w_sh[nm][0], g, m_sh[nm][0], v_sh[nm][0], "adamw_" + nm))

    for nm in _LATE:
        adamw_big(nm)

    shard_like = {nm: w_sh[nm] for nm in _SMALL}
    rows_l = -(-sum(int(np.prod(shard_like[nm].shape)) for nm in _SMALL) // 1024) * 8
    g_l = _pack_small({nm: g_small[nm].reshape(w_sh[nm].shape) for nm in _SMALL}, rows_l) + zero
    outs = _adamw(_pack_small(w_sh, rows_l), g_l, _pack_small(m_sh, rows_l), _pack_small(v_sh, rows_l), "adamw_small")
    for tree, packed in zip((grads, deltas, new_m, new_v), outs):
        tree.update(_unpack_small(packed, shard_like))

    done_meanwhile = deltas["w_up"][0, :8, :HEAD_DIM] + outs[1][:8, :]
    got = push_wait("rs_early_wait", scattering, _scatter_copies, done_meanwhile)
    g_big.update(sibling_assemble({nm: _finalsum(nm, sel, g32[nm], theirs[nm], got[nm]) for nm in _EARLY},
                                  "rs_assemble"))
    for nm in _EARLY:
        adamw_big(nm)

    return (loss, gx[None], *[grads[nm] for nm in names], *[deltas[nm] for nm in names],
            *[new_m[nm] for nm in names], *[new_v[nm] for nm in names])
```

```python
import functools
import math

import jax
import jax.numpy as jnp
import numpy as np
from jax import lax
from jax.experimental import pallas as pl
from jax.experimental.pallas import tpu as pltpu

F32 = jnp.float32
BF16 = jnp.bfloat16
HIGHEST = lax.Precision.HIGHEST
MESH = pl.DeviceIdType.MESH

D_MODEL = 1024
GRID_W = 64
ATTN_HEADS = 8
ATTN_KV_HEADS = 2
ATTN_GROUP = ATTN_HEADS // ATTN_KV_HEADS
HEAD_DIM = 128
ROPE_THETA = 10000.0
GDN_HEADS = 8
GDN_CHUNK = 64
D_FF = 2816
NORM_EPS = 1e-6
KV_W = ATTN_KV_HEADS * HEAD_DIM
Q_W = ATTN_HEADS * HEAD_DIM
GDN_W = GDN_HEADS * HEAD_DIM
CONV_W = 3 * GDN_W
MOD_W = 6 * D_MODEL
IN_COLS = 2 * KV_W + CONV_W + 4 * GDN_HEADS + Q_W + GDN_W + 2 * D_MODEL
IN_MAIN = IN_COLS - 4 * GDN_HEADS
SMALL_AT = 2 * KV_W + CONV_W
N_CHIPS = 4
N_DEV = 8

ADAM_LR = 0.001
ADAM_B1 = 0.9
ADAM_B2 = 0.999
ADAM_EPS = 1e-08
ADAM_WD = 0.01
ADAM_STEP = 10

VMEM_LIMIT = 48 * 1024 * 1024
MATMUL_VMEM_BUDGET = 40 * 1024 * 1024
MATMUL_STEP_BYTES = 1200 * 1024


def _pick(dim, prefs):
    for p in prefs:
        if p <= dim and dim % p == 0:
            return p
    return dim


_DIMS = {
    "nn": (((1,), (0,)), ((), ())),
    "nt": (((1,), (1,)), ((), ())),
    "tn": (((0,), (0,)), ((), ())),
}


def _matmul_plan(m, n, k, a_bytes, b_bytes):
    best = None
    for tm in (2304, 2048, 1152, 1024, 768, 512, 384, 256, 128, m):
        for tn in (2560, 1536, 1408, 1024, 768, 512, 256, 128, n):
            for tk in (3840, 2816, 2560, 2304, 2048, 1920, 1408, 1152, 1024, 768, 512, 256, 128, k):
                if tm > m or tn > n or tk > k or m % tm or n % tn or k % tk:
                    continue
                blocks = tm * tk * a_bytes + tk * tn * b_bytes + tm * tn * 4
                casts = (tm * tk * 2 if a_bytes > 2 else 0) + (tk * tn * 2 if b_bytes > 2 else 0) + tm * tn * 4
                if 2 * blocks + casts > MATMUL_VMEM_BUDGET:
                    continue
                nm, nn, nk = m // tm, n // tn, k // tk
                size_a, size_b = m * k * a_bytes, k * n * b_bytes
                for n_inner in (True, False):
                    if n_inner:
                        traffic = (size_a if nk == 1 else nn * size_a) + nm * size_b
                    else:
                        traffic = nn * size_a + (size_b if nk == 1 else nm * size_b)
                    cost = traffic + nm * nn * nk * MATMUL_STEP_BYTES + (nk - 1) * m * n * 4
                    if best is None or cost < best[0]:
                        best = (cost, tm, tn, tk, n_inner)
    return best[1:]


def _matmul(a, b, mode, name):
    if mode == "nn":
        (m, k), (_, n) = a.shape, b.shape
    elif mode == "nt":
        (m, k), (n, _) = a.shape, b.shape
    else:
        (k, m), (_, n) = a.shape, b.shape
    tm, tn, tk, n_inner = _matmul_plan(m, n, k, a.dtype.itemsize, b.dtype.itemsize)
    nk = k // tk
    ij = (lambda g0, g1: (g0, g1)) if n_inner else (lambda g0, g1: (g1, g0))
    if mode == "tn":
        a_spec = pl.BlockSpec((tk, tm), lambda g0, g1, l: (l, ij(g0, g1)[0]))
    else:
        a_spec = pl.BlockSpec((tm, tk), lambda g0, g1, l: (ij(g0, g1)[0], l))
    if mode == "nt":
        b_spec = pl.BlockSpec((tn, tk), lambda g0, g1, l: (ij(g0, g1)[1], l))
    else:
        b_spec = pl.BlockSpec((tk, tn), lambda g0, g1, l: (l, ij(g0, g1)[1]))
    dims = _DIMS[mode]

    def body(a_ref, b_ref, o_ref):
        part = lax.dot_general(a_ref[...].astype(BF16), b_ref[...].astype(BF16), dims, preferred_element_type=F32)
        if nk == 1:
            o_ref[...] = part
        else:
            l = pl.program_id(2)

            @pl.when(l == 0)
            def _():
                o_ref[...] = part

            @pl.when(l > 0)
            def _():
                o_ref[...] += part

    return pl.pallas_call(
        body,
        name=name,
        grid=(m // tm, n // tn, nk) if n_inner else (n // tn, m // tm, nk),
        in_specs=[a_spec, b_spec],
        out_specs=pl.BlockSpec((tm, tn), lambda g0, g1, l: ij(g0, g1)),
        out_shape=jax.ShapeDtypeStruct((m, n), F32),
        compiler_params=pltpu.CompilerParams(dimension_semantics=("parallel", "parallel", "arbitrary"),
                                             vmem_limit_bytes=VMEM_LIMIT),
    )(a, b)


@functools.partial(jax.custom_vjp, nondiff_argnums=(3,))
def pmm(a, w, wz, name):
    del wz
    return _matmul(a, w, "nn", name + "_f")


def _pmm_fwd(a, w, wz, name):
    del wz
    return _matmul(a, w, "nn", name + "_f"), (a, w)


def _pmm_bwd(name, res, g):
    a, w = res
    da = _matmul(g, w, "nt", name + "_da")
    if a.shape[0] < 128:
        pad = 128 - a.shape[0]
        at = jnp.pad(a.T, ((0, 0), (0, pad)))
        gp = jnp.pad(g, ((0, pad), (0, 0)))
        dw = _matmul(at, gp, "nn", name + "_dw")
    else:
        dw = _matmul(a, g, "tn", name + "_dw")
    return da, jnp.zeros_like(w), dw


pmm.defvjp(_pmm_fwd, _pmm_bwd)


@functools.partial(jax.custom_vjp, nondiff_argnums=(3,))
def pmm_t(a, wt, wtz, name):
    del wtz
    return _matmul(a, wt, "nt", name + "_f")


def _pmm_t_fwd(a, wt, wtz, name):
    del wtz
    return _matmul(a, wt, "nt", name + "_f"), (a, wt)


def _pmm_t_bwd(name, res, g):
    a, wt = res
    return _matmul(g, wt, "nn", name + "_da"), jnp.zeros_like(wt), _matmul(g, a, "tn", name + "_dw")


pmm_t.defvjp(_pmm_t_fwd, _pmm_t_bwd)


def rowop(fn, name, rows, bcs=(), crows=(), cbcs=(), tr=256):
    rows, bcs, crows, cbcs = tuple(rows), tuple(bcs), tuple(crows), tuple(cbcs)
    n_rows = rows[0].shape[0]
    tr = _pick(n_rows, (tr, 128, 64, 32, 16, 8))
    nr, nb, ncr, ncb = len(rows), len(bcs), len(crows), len(cbcs)
    n_in = nr + nb + ncr + ncb
    grid = (n_rows // tr,)

    def blk(arr):
        return jax.ShapeDtypeStruct((tr, arr.shape[1]), arr.dtype)

    def row_spec(arr):
        return pl.BlockSpec((tr, arr.shape[1]), lambda i: (i, 0))

    def bc_spec(arr):
        return pl.BlockSpec(arr.shape, lambda i: (0, 0))

    out_blk = jax.eval_shape(fn, *[blk(r) for r in rows], *bcs, *[blk(r) for r in crows], *cbcs)
    n_out = len(out_blk)
    out_shape = tuple(jax.ShapeDtypeStruct((n_rows, o.shape[1]), o.dtype) for o in out_blk)
    in_specs = ([row_spec(r) for r in rows] + [bc_spec(b) for b in bcs]
                + [row_spec(r) for r in crows] + [bc_spec(b) for b in cbcs])

    def order(vals):
        return vals

    def fwd_call(args):
        def body(*refs):
            vals = [r[...] for r in refs[:n_in]]
            res = fn(*order(vals))
            for o_ref, r in zip(refs[n_in:], res):
                o_ref[...] = r

        return pl.pallas_call(
            body, name=name + "_f", grid=grid, in_specs=in_specs,
            out_specs=[row_spec(o) for o in out_shape], out_shape=out_shape,
            compiler_params=pltpu.CompilerParams(dimension_semantics=("parallel",), vmem_limit_bytes=VMEM_LIMIT),
        )(*args)

    def bwd_call(args, cts):
        def body(*refs):
            vals = [r[...] for r in refs[:n_in]]
            ct_refs = refs[n_in:n_in + n_out]
            d_rows = refs[n_in + n_out:n_in + n_out + nr]
            d_bcs = refs[n_in + n_out + nr:]
            consts = vals[nr + nb:]
            _, vjp = jax.vjp(lambda *p: fn(*p, *consts), *vals[:nr + nb])
            grads = vjp(tuple(c[...] for c in ct_refs))
            for ref, g in zip(d_rows, grads[:nr]):
                ref[...] = g

            @pl.when(pl.program_id(0) == 0)
            def _():
                for ref in d_bcs:
                    ref[...] = jnp.zeros_like(ref)

            for ref, g in zip(d_bcs, grads[nr:]):
                ref[...] += g

        d_shape = tuple(jax.ShapeDtypeStruct(r.shape, r.dtype) for r in rows + bcs)
        return pl.pallas_call(
            body, name=name + "_b", grid=grid,
            in_specs=in_specs + [row_spec(o) for o in out_shape],
            out_specs=[row_spec(r) for r in rows] + [bc_spec(b) for b in bcs], out_shape=d_shape,
            compiler_params=pltpu.CompilerParams(dimension_semantics=("arbitrary",), vmem_limit_bytes=VMEM_LIMIT),
        )(*args, *cts)

    @jax.custom_vjp
    def op(diff, const):
        return fwd_call(diff + const)

    def op_fwd(diff, const):
        return fwd_call(diff + const), (diff, const)

    def op_bwd(res, cts):
        diff, const = res
        grads = bwd_call(diff + const, tuple(cts))
        return tuple(grads), tuple(jnp.zeros_like(c) for c in const)

    op.defvjp(op_fwd, op_bwd)
    return op(rows + bcs, crows + cbcs)


def colop(fn, name, arrays, uses, n_const, nblk, cw=128):
    arrays = tuple(arrays)
    n_diff = len(arrays) - n_const
    nd = sum(1 for u in uses if u[0] < n_diff)
    assert all(u[0] < n_diff for u in uses[:nd]) and all(u[0] >= n_diff for u in uses[nd:])

    def spec(u):
        return pl.BlockSpec((arrays[u[0]].shape[0], cw), lambda j, off=u[1]: (0, off + j))

    def out_spec(rows):
        return pl.BlockSpec((rows, cw), lambda j: (0, j))

    out_blk = jax.eval_shape(fn, *[jax.ShapeDtypeStruct((arrays[u[0]].shape[0], cw), arrays[u[0]].dtype)
                                   for u in uses])
    out_shape = tuple(jax.ShapeDtypeStruct((o.shape[0], nblk * cw), o.dtype) for o in out_blk)
    params = pltpu.CompilerParams(dimension_semantics=("parallel",), vmem_limit_bytes=VMEM_LIMIT)

    def fwd_call(arrs):
        def body(*refs):
            res = fn(*[r[...] for r in refs[:len(uses)]])
            for o_ref, r in zip(refs[len(uses):], res):
                o_ref[...] = r

        return pl.pallas_call(
            body, name=name + "_f", grid=(nblk,), in_specs=[spec(u) for u in uses],
            out_specs=[out_spec(o.shape[0]) for o in out_shape], out_shape=out_shape, compiler_params=params,
        )(*[arrs[u[0]] for u in uses])

    def bwd_call(arrs, cts):
        def body(*refs):
            vals = [r[...] for r in refs[:len(uses)]]
            ct_refs = refs[len(uses):len(uses) + len(out_shape)]
            _, vjp = jax.vjp(lambda *p: fn(*p, *vals[nd:]), *vals[:nd])
            for ref, g in zip(refs[len(uses) + len(out_shape):], vjp(tuple(c[...] for c in ct_refs))):
                ref[...] = g

        d_shape = tuple(jax.ShapeDtypeStruct((arrays[u[0]].shape[0], nblk * cw), F32) for u in uses[:nd])
        return pl.pallas_call(
            body, name=name + "_b", grid=(nblk,),
            in_specs=[spec(u) for u in uses] + [out_spec(o.shape[0]) for o in out_shape],
            out_specs=[out_spec(s.shape[0]) for s in d_shape], out_shape=d_shape, compiler_params=params,
        )(*[arrs[u[0]] for u in uses], *cts)

    @jax.custom_vjp
    def op(diff, const):
        return fwd_call(diff + const)

    def op_fwd(diff, const):
        return fwd_call(diff + const), (diff, const)

    def op_bwd(res, cts):
        diff, const = res
        d_uses = bwd_call(diff + const, tuple(cts))
        grads = []
        for i in range(n_diff):
            parts = sorted([(u[1], k) for k, u in enumerate(uses[:nd]) if u[0] == i])
            grads.append(d_uses[parts[0][1]] if len(parts) == 1
                         else jnp.concatenate([d_uses[k] for _, k in parts], axis=1))
        return tuple(grads), tuple(jnp.zeros_like(c) for c in const)

    op.defvjp(op_fwd, op_bwd)
    return op(arrays[:n_diff], arrays[n_diff:])


@functools.partial(jax.custom_vjp, nondiff_argnums=(1,))
def _roll_rows(x, k):
    return pltpu.roll(x, k % x.shape[0], 0)


def _roll_rows_fwd(x, k):
    return _roll_rows(x, k), None


def _roll_rows_bwd(k, _, g):
    return (_roll_rows(g, -k),)


_roll_rows.defvjp(_roll_rows_fwd, _roll_rows_bwd)


def _conv3(x, w0, w1, w2, starts):
    rows = lax.broadcasted_iota(jnp.int32, x.shape, 0)
    ends = tuple(s - 1 for s in starts[1:]) + (x.shape[0] - 1,)
    first = functools.reduce(jnp.logical_or, [rows == s for s in starts])
    last = functools.reduce(jnp.logical_or, [rows == e for e in ends])
    prev = jnp.where(first, 0.0, _roll_rows(x, 1))
    nxt = jnp.where(last, 0.0, _roll_rows(x, -1))
    return prev * w0 + x * w1 + nxt * w2


def _rms(x):
    return x * lax.rsqrt(jnp.mean(x * x, axis=-1, keepdims=True) + NORM_EPS)


def _heads(x, n):
    return [x[:, h * HEAD_DIM:(h + 1) * HEAD_DIM] for h in range(n)]


_NT = (((1,), (1,)), ((), ()))
_TN = (((0,), (0,)), ((), ()))
_TQ = 256


_N_SUB = 2


def _sub_rows(ref, i):
    rows = ref.shape[0] // _N_SUB
    return ref[i * rows:(i + 1) * rows, :].astype(BF16)


def _attn_probs(qs, k):
    s = _each(lambda q: lax.dot_general(q, k, _NT, preferred_element_type=F32) * (HEAD_DIM ** -0.5), qs)
    m = _each(lambda a: jnp.max(a, axis=-1, keepdims=True), s)
    e = _each(lambda a, b: jnp.exp(a - b), s, m)
    inv = _each(lambda a: 1.0 / jnp.sum(a, axis=-1, keepdims=True), e)
    return _each(lambda a, b: a * b, e, inv)


def _attn_fwd_call(q, k, v):
    n, t = q.shape[0], k.shape[0]
    tq = _pick(n, (_TQ, 128))

    def body(q_ref, k_ref, v_ref, o_ref):
        vb = v_ref[...].astype(BF16)
        ps = _attn_probs([_sub_rows(q_ref, i) for i in range(_N_SUB)], k_ref[...].astype(BF16))
        rows = tq // _N_SUB
        for i, p in enumerate(ps):
            o_ref[i * rows:(i + 1) * rows, :] = jnp.dot(p.astype(BF16), vb, preferred_element_type=F32)

    return pl.pallas_call(
        body, name="attn_f", grid=(ATTN_HEADS, n // tq),
        in_specs=[pl.BlockSpec((tq, HEAD_DIM), lambda h, i: (i, h)),
                  pl.BlockSpec((t, HEAD_DIM), lambda h, i: (0, h // ATTN_GROUP)),
                  pl.BlockSpec((t, HEAD_DIM), lambda h, i: (0, h // ATTN_GROUP))],
        out_specs=pl.BlockSpec((tq, HEAD_DIM), lambda h, i: (i, h)),
        out_shape=jax.ShapeDtypeStruct(q.shape, F32),
        compiler_params=pltpu.CompilerParams(dimension_semantics=("parallel", "parallel"),
                                             vmem_limit_bytes=VMEM_LIMIT),
    )(q, k, v)


def _attn_bwd_call(q, k, v, do):
    n, t = q.shape[0], k.shape[0]
    tq = _pick(n, (_TQ, 128))

    def body(q_ref, k_ref, v_ref, do_ref, dq_ref, dk_ref, dv_ref):
        @pl.when((pl.program_id(1) == 0) & (pl.program_id(2) == 0))
        def _():
            dk_ref[...] = jnp.zeros_like(dk_ref)
            dv_ref[...] = jnp.zeros_like(dv_ref)

        kb, vb = k_ref[...].astype(BF16), v_ref[...].astype(BF16)
        qs = [_sub_rows(q_ref, i) for i in range(_N_SUB)]
        dos = [_sub_rows(do_ref, i) for i in range(_N_SUB)]
        ps = _attn_probs(qs, kb)
        dps = _each(lambda d: lax.dot_general(d, vb, _NT, preferred_element_type=F32), dos)
        dss = _each(lambda p, dp: (p * (dp - jnp.sum(p * dp, axis=-1, keepdims=True)) * (HEAD_DIM ** -0.5)).astype(BF16),
                    ps, dps)
        rows = tq // _N_SUB
        for i, ds in enumerate(dss):
            dq_ref[i * rows:(i + 1) * rows, :] = jnp.dot(ds, kb, preferred_element_type=F32)
        dk_ref[...] += sum(_each(lambda ds, q: lax.dot_general(ds, q, _TN, preferred_element_type=F32), dss, qs))
        dv_ref[...] += sum(_each(lambda p, d: lax.dot_general(p.astype(BF16), d, _TN, preferred_element_type=F32),
                                 ps, dos))

    q_spec = pl.BlockSpec((tq, HEAD_DIM), lambda kh, g, i: (i, kh * ATTN_GROUP + g))
    kv_spec = pl.BlockSpec((t, HEAD_DIM), lambda kh, g, i: (0, kh))
    return pl.pallas_call(
        body, name="attn_b", grid=(ATTN_KV_HEADS, ATTN_GROUP, n // tq),
        in_specs=[q_spec, kv_spec, kv_spec, q_spec],
        out_specs=[q_spec, kv_spec, kv_spec],
        out_shape=(jax.ShapeDtypeStruct(q.shape, F32), jax.ShapeDtypeStruct(k.shape, F32),
                   jax.ShapeDtypeStruct(v.shape, F32)),
        compiler_params=pltpu.CompilerParams(dimension_semantics=("parallel", "arbitrary", "arbitrary"),
                                             vmem_limit_bytes=VMEM_LIMIT),
    )(q, k, v, do)


@jax.custom_vjp
def attention(q, k, v):
    return _attn_fwd_call(q, k, v)


def _attention_fwd(q, k, v):
    return _attn_fwd_call(q, k, v), (q, k, v)


def _attention_bwd(res, do):
    return _attn_bwd_call(*res, do)


attention.defvjp(_attention_fwd, _attention_bwd)


_C = GDN_CHUNK


def _pdot(a, b):
    return jnp.dot(a, b, precision=lax.Precision.HIGH, preferred_element_type=F32)


@jax.custom_vjp
def _hdot(a, b):
    return jnp.dot(a.astype(BF16), b.astype(BF16), preferred_element_type=F32)


def _hdot_fwd(a, b):
    return _hdot(a, b), (a, b)


def _hdot_bwd(res, g):
    a, b = res
    gb = g.astype(BF16)
    return (lax.dot_general(gb, b.astype(BF16), _NT, preferred_element_type=F32),
            lax.dot_general(a.astype(BF16), gb, _TN, preferred_element_type=F32))


_hdot.defvjp(_hdot_fwd, _hdot_bwd)


def _each(fn, *lists):
    return [fn(*args) for args in zip(*lists)]


@jax.custom_vjp
def _unit_lower_inverse(low, blockdiag):
    return _unit_lower_inverse_chain(low, blockdiag)


def _unit_lower_inverse_fwd(low, blockdiag):
    t_inv = _unit_lower_inverse_chain(low, blockdiag)
    return t_inv, (t_inv, blockdiag)


def _unit_lower_inverse_bwd(res, d_inv):
    t_inv, blockdiag = res
    bf = lambda a: a.astype(BF16)
    left = _each(lambda t, g: lax.dot_general(bf(t), bf(g), _TN, preferred_element_type=F32), t_inv, d_inv)
    d_low = _each(lambda a, t: -lax.dot_general(bf(a), bf(t), _NT, preferred_element_type=F32), left, t_inv)
    return d_low, jnp.zeros_like(blockdiag)


_unit_lower_inverse.defvjp(_unit_lower_inverse_fwd, _unit_lower_inverse_bwd)


def _unit_lower_inverse_chain(low, blockdiag):
    eye = (lax.broadcasted_iota(jnp.int32, (_C, _C), 0) == lax.broadcasted_iota(jnp.int32, (_C, _C), 1)).astype(F32)
    ld = _each(lambda a: a * blockdiag, low)
    lo = _each(lambda a, d: a - d, low, ld)
    l2 = _each(_hdot, ld, ld)
    l4 = _each(_hdot, l2, l2)
    l8 = _each(_hdot, l4, l4)
    td = _each(lambda d, a2: _hdot(eye - d, eye + a2), ld, l2)
    td = _each(lambda t, a4: _hdot(t, eye + a4), td, l4)
    td = _each(lambda t, a8: _hdot(t, eye + a8), td, l8)
    nn = _each(_hdot, td, lo)
    n2 = _each(_hdot, nn, nn)
    out = _each(lambda n, m2: _hdot(eye - n, eye + m2), nn, n2)
    return _each(_hdot, out, td)


@jax.custom_vjp
def _inverse_given(low, t_inv):
    del low
    return t_inv


_inverse_given.defvjp(lambda low, t_inv: (t_inv, (t_inv, low[0])),
                      lambda res, d_inv: (_unit_lower_inverse_bwd(res, d_inv)[0], [jnp.zeros_like(t) for t in res[0]]))


def _gdn_chunks(heads, blockdiag, kept_inverses=None):
    q, k, v, b_b, be_b, e_b, kd_b, m1, dec, gl, s = (list(col) for col in zip(*heads))
    f32dot = lambda a, b: jnp.dot(a, b, preferred_element_type=F32)
    nt = lambda a, b: lax.dot_general(a, b, _NT, preferred_element_type=F32)
    kq_k = _each(lambda kx, qq: nt(jnp.concatenate([kx, qq], axis=0), kx), k, q)
    low = _each(lambda m, a: m * a[:_C], m1, kq_k)
    t_inv = _unit_lower_inverse(low, blockdiag) if kept_inverses is None else _inverse_given(low, kept_inverses)
    uw = _each(lambda t, b, x, be, kx: _hdot(t, jnp.concatenate([b * x, be * kx], axis=1)), t_inv, b_b, v, be_b, k)
    wq_s = _each(lambda a, qq, e, ss: f32dot(jnp.concatenate([a[:, HEAD_DIM:], qq * e], axis=0), ss), uw, q, e_b, s)
    delta = _each(lambda a, ws: a[:, :HEAD_DIM] - ws[:_C], uw, wq_s)
    p = _each(lambda d, a: d * a[_C:], dec, kq_k)
    o = _each(lambda ws, pp, dd: ws[_C:] + f32dot(pp, dd), wq_s, p, delta)
    s_new = _each(lambda g, ss, kx, kd, dd: g * ss + lax.dot_general(kx * kd, dd, _TN, preferred_element_type=F32),
                  gl, s, k, kd_b, delta)
    return (o, s_new) if kept_inverses is not None else (o, s_new, t_inv)


def _blockdiag_mask():
    r = lax.broadcasted_iota(jnp.int32, (_C, _C), 0) >> 4
    c = lax.broadcasted_iota(jnp.int32, (_C, _C), 1) >> 4
    return (r == c).astype(F32)


_N_DIR = 2


def _scan_chunk(s, nc, ncc, reverse):
    return jnp.where(s < ncc, ncc - 1 - s, nc + ncc - 1 - s) if reverse else s


def _gdn_specs(nc, ncc, backward):
    step = (lambda s: nc - 1 - s) if backward else (lambda s: s)
    chunk = [lambda s, d=d: _scan_chunk(step(s), nc, ncc, d == 1) for d in range(_N_DIR)]
    tok = [pl.BlockSpec((_C, 3 * GDN_W), lambda s, d=d: (chunk[d](s), 0)) for d in range(_N_DIR)]
    park = [0, nc - ncc - 1]
    out = [pl.BlockSpec((_C, GDN_W), lambda s, d=d: (jnp.where(chunk[d](s) >= ncc, chunk[d](s) - ncc, park[d]), 0))
           for d in range(_N_DIR)]
    per_tok = pl.BlockSpec((_N_DIR, GDN_HEADS, _C, HEAD_DIM), lambda s: (0, 0, step(s), 0))
    mat = pl.BlockSpec((_N_DIR, GDN_HEADS, None, _C, _C), lambda s: (0, 0, step(s), 0, 0))
    row = pl.BlockSpec((_N_DIR, GDN_HEADS, None, 1, HEAD_DIM), lambda s: (0, 0, step(s), 0, 0))
    state = pl.BlockSpec((_N_DIR, GDN_HEADS, None, HEAD_DIM, HEAD_DIM), lambda s: (0, 0, step(s), 0, 0))
    return tok, out, per_tok, mat, row, state, chunk


def _head_cols(h, part):
    return slice((part * GDN_HEADS + h) * HEAD_DIM, (part * GDN_HEADS + h + 1) * HEAD_DIM)


def _gdn_heads(qkv_refs, factor_refs, state_of):
    return [[qkv_refs[d][:, _head_cols(h, 0)], qkv_refs[d][:, _head_cols(h, 1)], qkv_refs[d][:, _head_cols(h, 2)]]
            + [r[d, h] for r in factor_refs] + [state_of(d, h)]
            for d in range(_N_DIR) for h in range(GDN_HEADS)]


def _gdn_fwd_call(ncc, qkv, factors):
    t = qkv.shape[0]
    nc = t // _C
    tok, out, per_tok, mat, row, state, _ = _gdn_specs(nc, ncc, False)

    def body(*refs):
        qkv_refs, f_refs = refs[:_N_DIR], refs[_N_DIR:_N_DIR + 7]
        o_refs, sall_ref, tall_ref, s_ref = refs[_N_DIR + 7:2 * _N_DIR + 7], refs[2 * _N_DIR + 7], refs[-2], refs[-1]

        @pl.when(pl.program_id(0) == 0)
        def _():
            s_ref[...] = jnp.zeros_like(s_ref)

        heads = _gdn_heads(qkv_refs, f_refs, lambda d, h: s_ref[d, h])
        o, s_new, t_inv = _gdn_chunks(heads, _blockdiag_mask())
        for d in range(_N_DIR):
            for h in range(GDN_HEADS):
                i = GDN_HEADS * d + h
                sall_ref[d, h] = heads[i][10]
                tall_ref[d, h] = t_inv[i]
                o_refs[d][:, _head_cols(h, 0)] = o[i]
                s_ref[d, h] = s_new[i]

    o_shape = jax.ShapeDtypeStruct((t - ncc * _C, GDN_W), F32)
    s_shape = (_N_DIR, GDN_HEADS, nc, HEAD_DIM, HEAD_DIM)
    t_shape = (_N_DIR, GDN_HEADS, nc, _C, _C)
    return pl.pallas_call(
        body, name="gdn_f", grid=(nc,),
        in_specs=[*tok, per_tok, per_tok, per_tok, per_tok, mat, mat, row],
        out_specs=[*out, state, mat],
        out_shape=[o_shape, o_shape, jax.ShapeDtypeStruct(s_shape, F32), jax.ShapeDtypeStruct(t_shape, F32)],
        scratch_shapes=[pltpu.VMEM((_N_DIR, GDN_HEADS, HEAD_DIM, HEAD_DIM), F32)],
        compiler_params=pltpu.CompilerParams(dimension_semantics=("arbitrary",), vmem_limit_bytes=VMEM_LIMIT),
    )(qkv, qkv, *factors)


def _gdn_bwd_call(ncc, qkv, factors, sall, tall, dos):
    t = qkv.shape[0]
    nc = t // _C
    tok, out, per_tok, mat, row, state, chunk = _gdn_specs(nc, ncc, True)

    def body(*refs):
        qkv_refs, f_refs, sall_ref, tall_ref = refs[:_N_DIR], refs[_N_DIR:_N_DIR + 7], refs[_N_DIR + 7], refs[_N_DIR + 8]
        do_refs = refs[_N_DIR + 9:2 * _N_DIR + 9]
        dqkv_refs = refs[2 * _N_DIR + 9:3 * _N_DIR + 9]
        df_refs, ds_ref = refs[3 * _N_DIR + 9:3 * _N_DIR + 16], refs[-1]

        @pl.when(pl.program_id(0) == 0)
        def _():
            ds_ref[...] = jnp.zeros_like(ds_ref)

        bd = _blockdiag_mask()
        heads = _gdn_heads(qkv_refs, f_refs, lambda d, h: sall_ref[d, h])
        kept = [tall_ref[d, h] for d in range(_N_DIR) for h in range(GDN_HEADS)]
        _, vjp = jax.vjp(lambda hs: _gdn_chunks(hs, bd, kept), heads)
        live = [chunk[d](pl.program_id(0)) >= ncc for d in range(_N_DIR)]
        (all_grads,) = vjp(([jnp.where(live[d], do_refs[d][:, _head_cols(h, 0)], 0.0)
                             for d in range(_N_DIR) for h in range(GDN_HEADS)],
                            [ds_ref[d, h] for d in range(_N_DIR) for h in range(GDN_HEADS)]))
        for d in range(_N_DIR):
            for h in range(GDN_HEADS):
                grads = all_grads[GDN_HEADS * d + h]
                for part in range(3):
                    dqkv_refs[d][:, _head_cols(h, part)] = grads[part]
                for ref, g in zip(df_refs, grads[3:10]):
                    ref[d, h] = g
                ds_ref[d, h] = grads[10]

    shp = lambda a: jax.ShapeDtypeStruct(a.shape, F32)
    res = pl.pallas_call(
        body, name="gdn_b", grid=(nc,),
        in_specs=[*tok, per_tok, per_tok, per_tok, per_tok, mat, mat, row, state, mat, *out],
        out_specs=[*tok, per_tok, per_tok, per_tok, per_tok, mat, mat, row],
        out_shape=[shp(qkv), shp(qkv)] + [shp(a) for a in factors],
        scratch_shapes=[pltpu.VMEM((_N_DIR, GDN_HEADS, HEAD_DIM, HEAD_DIM), F32)],
        compiler_params=pltpu.CompilerParams(dimension_semantics=("arbitrary",), vmem_limit_bytes=VMEM_LIMIT),
    )(qkv, qkv, *factors, sall, tall, *dos)
    return res[0] + res[1], tuple(res[_N_DIR:])


@functools.partial(jax.custom_vjp, nondiff_argnums=(0,))
def gdn_scan(ncc, qkv, factors):
    o0, o1, _, _ = _gdn_fwd_call(ncc, qkv, factors)
    return o0, o1


def _gdn_scan_fwd(ncc, qkv, factors):
    o0, o1, sall, tall = _gdn_fwd_call(ncc, qkv, factors)
    return (o0, o1), (qkv, factors, sall, tall)


def _gdn_scan_bwd(ncc, res, dos):
    qkv, factors, sall, tall = res
    return _gdn_bwd_call(ncc, qkv, factors, sall, tall, list(dos))


gdn_scan.defvjp(_gdn_scan_fwd, _gdn_scan_bwd)


def _rope_tables(n, cl):
    t = np.arange(n)
    inv_freq = (ROPE_THETA ** (-np.arange(0, HEAD_DIM // 2, 2, dtype=np.float32) / (HEAD_DIM // 2))).astype(np.float32)
    ang_r = (t // GRID_W).astype(np.float32)[:, None] * inv_freq
    ang_c = (t % GRID_W).astype(np.float32)[:, None] * inv_freq
    cos = np.concatenate([np.cos(ang_r), np.cos(ang_r), np.cos(ang_c), np.cos(ang_c)], axis=1)
    sin = np.concatenate([-np.sin(ang_r), np.sin(ang_r), -np.sin(ang_c), np.sin(ang_c)], axis=1)
    cos_all = np.concatenate([np.ones((cl, HEAD_DIM), np.float32), cos], axis=0)
    sin_all = np.concatenate([np.zeros((cl, HEAD_DIM), np.float32), sin], axis=0)
    j = np.arange(HEAD_DIM)
    src = np.where((j % 64) < 32, j + 32, j - 32)
    perm = np.zeros((HEAD_DIM, HEAD_DIM), np.float32)
    perm[src, j] = 1.0
    return (jnp.asarray(cos.astype(np.float32)), jnp.asarray(sin.astype(np.float32)),
            jnp.asarray(cos_all), jnp.asarray(sin_all), jnp.asarray(perm))


def _gdn_factors(log_a, beta, ncc):
    t = log_a.shape[0]
    nc = t // _C
    la = log_a.reshape(nc, _C, _N_DIR, GDN_HEADS).transpose(2, 3, 0, 1)
    be = beta.reshape(nc, _C, _N_DIR, GDN_HEADS).transpose(2, 3, 0, 1)
    scan_order = lambda a: jnp.stack([a[0], jnp.concatenate([jnp.flip(a[1][:, :ncc], axis=1),
                                                              jnp.flip(a[1][:, ncc:], axis=1)], axis=1)])
    la, be = scan_order(la), scan_order(be)
    rev = jnp.asarray(np.array([False, True])[:, None, None, None])
    run = jnp.cumsum(la, axis=3)
    gam = jnp.where(rev, jnp.sum(la, axis=3, keepdims=True) - run + la, run)
    idx = np.arange(_C)
    incl = jnp.asarray(np.stack([idx[:, None] >= idx[None, :], idx[:, None] <= idx[None, :]])[:, None, None])
    strict = jnp.asarray(np.stack([idx[:, None] > idx[None, :], idx[:, None] < idx[None, :]])[:, None, None])
    dec = jnp.exp(jnp.where(incl, gam[..., :, None] - gam[..., None, :], -jnp.inf))
    m1 = jnp.where(strict, be[..., :, None] * dec, 0.0)
    e = jnp.exp(gam)
    g_last = jnp.where(rev, gam[..., :1], gam[..., -1:])
    lanes = lambda a: jnp.broadcast_to(a.reshape(_N_DIR, GDN_HEADS, t, 1), (_N_DIR, GDN_HEADS, t, HEAD_DIM))
    gl = jnp.broadcast_to(jnp.exp(g_last)[..., None], (_N_DIR, GDN_HEADS, nc, 1, HEAD_DIM))
    return lanes(be), lanes(be * e), lanes(e), lanes(jnp.exp(g_last - gam)), m1, dec, gl


def local_loss(x, wz, wb, ws, c, ctx, target):
    return channel_mixing(token_mixing(x, wz, wb, ws, c, ctx), wz, wb, ws, target)


def token_mixing(x, wz, wb, ws, c, ctx):
    n, cl = x.shape[0], ctx.shape[0]
    cos_q, sin_q, cos_k, sin_k, perm = _rope_tables(n, cl)

    sc_in = jnp.concatenate([jax.nn.silu(c), jax.nn.silu(ws["c_ctx"])[None, :], jnp.zeros((14, D_MODEL), F32)], axis=0)
    mod = pmm(sc_in, wb["w_mod"], wz["w_mod"], "mm_mod") + ws["b_mod"]
    sh1, sc1, g1, sh2, sc2, g2 = [mod[0:1, i * D_MODEL:(i + 1) * D_MODEL] for i in range(6)]
    csh1, csc1 = mod[1:2, 0:D_MODEL], mod[1:2, D_MODEL:2 * D_MODEL]

    def norm_mod(a, sh, sc):
        return (_rms(a) * (1.0 + sc) + sh,)

    (hx,) = rowop(norm_mod, "normmod_x", (x,), (sh1, sc1))
    (hc,) = rowop(norm_mod, "normmod_c", (ctx,), (csh1, csc1))
    h_all = jnp.concatenate([hc, hx], axis=0)
    p_main = pmm_t(h_all, wb["w_in_main"], wz["w_in_main"], "mm_in")
    p_small = pmm_t(h_all, wb["w_in_small"], wz["w_in_small"], "mm_ins")
    ak, av, qkv, aq, z, gate = jnp.split(p_main, [KV_W, 2 * KV_W, SMALL_AT, SMALL_AT + Q_W, SMALL_AT + Q_W + GDN_W],
                                         axis=1)
    db, da = p_small[:, :2 * GDN_HEADS], p_small[:, 2 * GDN_HEADS:4 * GDN_HEADS]

    def qk_prep(nh):
        def fn(a, w, cos, sin, pm):
            outs = []
            for ah in _heads(a, nh):
                y = _rms(ah) * w
                outs.append(y * cos + _pdot(y, pm) * sin)
            return (jnp.concatenate(outs, axis=1),)
        return fn

    (q_x,) = rowop(qk_prep(ATTN_HEADS), "q_prep", (aq[cl:],), (ws["q_norm_w"],), (cos_q, sin_q), (perm,))
    (k_all,) = rowop(qk_prep(ATTN_KV_HEADS), "k_prep", (ak,), (ws["k_norm_w"],), (cos_k, sin_k), (perm,))
    attn_x = attention(q_x, k_all, av)

    cw = ws["conv_qkv_w"]
    normed = jnp.asarray(np.repeat([1.0, 1.0, 0.0], GDN_W)[None, :], F32)
    scale = jnp.asarray(np.repeat([HEAD_DIM ** -0.5, 1.0, 1.0], GDN_W)[None, :], F32)

    def gdn_prep(a, w0, w1, w2, nf, sc):
        s = jax.nn.silu(_conv3(a, w0, w1, w2, (0, cl)))
        inv = lax.rsqrt(jnp.sum(s * s, axis=-1, keepdims=True) + NORM_EPS)
        return (s * jnp.where(nf > 0.0, inv * sc, 1.0),)

    (qkvn,) = colop(gdn_prep, "gdn_prep", (qkv, cw[0:1], cw[1:2], cw[2:3], normed, scale),
                    [(i, 0) for i in range(6)], 2, 3 * GDN_HEADS)
    beta = jax.nn.sigmoid(db).reshape(-1, 2, GDN_HEADS)
    log_a = -jnp.exp(ws["a_log"])[None] * jax.nn.softplus(da.reshape(-1, 2, GDN_HEADS) + ws["dt_bias"][None])
    o_fwd, o_rev = gdn_scan(cl // _C, qkvn, _gdn_factors(log_a, beta, cl // _C))
    o_x = o_fwd + o_rev

    def gdn_out(o, zz, w):
        outs = [_rms(oh) * w * jax.nn.silu(zh) for oh, zh in zip(_heads(o, GDN_HEADS), _heads(zz, GDN_HEADS))]
        return (jnp.concatenate(outs, axis=1),)

    (gdn_x,) = rowop(gdn_out, "gdn_out", (o_x, z[cl:]), (ws["gdn_norm_w"],))
    return dict(x=x, attn_x=attn_x, gdn_x=gdn_x, gate=gate[cl:], g1=g1, sh2=sh2, sc2=sc2, g2=g2)


def channel_mixing(mixed, wz, wb, ws, target):
    x, attn_x, gdn_x, gate = mixed["x"], mixed["attn_x"], mixed["gdn_x"], mixed["gate"]
    g1, sh2, sc2, g2 = mixed["g1"], mixed["sh2"], mixed["sc2"], mixed["g2"]
    pa = pmm(attn_x, wb["w_pa"], wz["w_pa"], "mm_pa")
    pd = pmm(gdn_x, wb["w_pd"], wz["w_pd"], "mm_pd")

    def merge(a, d, g):
        return (jax.nn.sigmoid(g[:, :D_MODEL]) * a + jax.nn.sigmoid(g[:, D_MODEL:]) * d,)

    (y,) = rowop(merge, "merge", (pa, pd, gate))
    mo = pmm(y, wb["w_out"], wz["w_out"], "mm_out")

    def res_norm_mod(xx, m, g, sh, sc):
        x1 = xx + g * m
        return x1, _rms(x1) * (1.0 + sc) + sh

    x1, h2 = rowop(res_norm_mod, "res1", (x, mo), (g1, sh2, sc2))
    up = pmm(h2, wb["w_up"], wz["w_up"], "mm_up")
    fw = ws["ffn_conv_w"]

    def ffn_act(ug, uv, w0g, w0v, w1g, w1v, w2g, w2v, bg, bv):
        g = _conv3(ug, w0g, w1g, w2g, (0,)) + bg
        v = _conv3(uv, w0v, w1v, w2v, (0,)) + bv
        return (jax.nn.silu(g) * v,)

    half = D_FF // HEAD_DIM
    (act,) = colop(ffn_act, "ffn_act", (up, fw[0:1], fw[1:2], fw[2:3], ws["ffn_conv_b"]),
                   [(i, off) for i in range(5) for off in (0, half)], 0, half)
    dn = pmm(act, wb["w_down"], wz["w_down"], "mm_down")

    def head(xx, m, g, w, tgt):
        yy = _rms(xx + g * m) * w
        err = (yy - tgt) ** 2
        return (jnp.broadcast_to(0.5 * jnp.mean(err, axis=-1, keepdims=True), (xx.shape[0], HEAD_DIM)),)

    (row_loss,) = rowop(head, "head", (x1, dn), (g2, ws["final_norm_w"][None, :]), (target,))
    return jnp.sum(row_loss[:, 0])


_HBM = pl.BlockSpec(memory_space=pltpu.HBM)


def _chip_peers():
    x, y = lax.axis_index("x"), lax.axis_index("y")
    return [(1 - x, y), (x, 1 - y), (1 - x, 1 - y)]


_SPLIT_COLS = ("w_in",)


def _half_of(view, nm, idx, lead=0):
    r, cdim = view.shape[-2:]
    pre = (slice(None),) * lead
    if nm in _SPLIT_COLS:
        return view.at[pre + (slice(None), pl.ds(pl.multiple_of(idx * (cdim // 2), 128), cdim // 2))]
    return view.at[pre + (pl.ds(pl.multiple_of(idx * (r // 2), 16), r // 2), slice(None))]


def _remote(src, dst, send_sem, recv_sem, dev):
    return pltpu.make_async_remote_copy(src_ref=src, dst_ref=dst, send_sem=send_sem, recv_sem=recv_sem,
                                        device_id=dev, device_id_type=MESH)


def _hbm_call(body, name, ins, out_shape, n_sems, in_place=False):
    names = tuple(ins)
    return dict(zip(names, pl.pallas_call(
        body, name=name, in_specs=[_HBM] * len(names), out_specs=[_HBM] * len(names),
        out_shape=[out_shape(nm, ins[nm]) for nm in names],
        scratch_shapes=[pltpu.SemaphoreType.DMA((k,)) for k in n_sems],
        input_output_aliases={i: i for i in range(len(names))} if in_place else {},
    )(*[ins[nm] for nm in names])))


def all_gather_chips(shards):
    names = tuple(shards)
    n = len(names)

    def body(*refs):
        ins, outs = dict(zip(names, refs[:n])), dict(zip(names, refs[n:2 * n]))
        ici_send, ici_recv, d2d_send, d2d_recv, own_send, own_recv = refs[2 * n:]
        x, y, c = lax.axis_index("x"), lax.axis_index("y"), lax.axis_index("c")
        me, sib = 2 * x + y, (x, y, 1 - c)
        own = [_remote(ins[nm], outs[nm].at[me], own_send.at[i], own_recv.at[i], sib) for i, nm in enumerate(names)]
        for cp in own:
            cp.start()
        sends = []
        for k, (px, py) in enumerate(_chip_peers()):
            for i, nm in enumerate(names):
                cp = _remote(_half_of(ins[nm], nm, c), _half_of(outs[nm].at[me], nm, c), ici_send.at[k * n + i],
                             ici_recv.at[k * n + i], (px, py, c))
                cp.start()
                sends.append(cp)
        for k, (px, py) in enumerate(_chip_peers()):
            for i, nm in enumerate(names):
                landed = _half_of(outs[nm].at[2 * px + py], nm, c)
                _remote(landed, landed, ici_send.at[k * n + i], ici_recv.at[k * n + i], (px, py, c)).wait_recv()
                fw = _remote(landed, landed, d2d_send.at[k * n + i], d2d_recv.at[k * n + i], sib)
                fw.start()
                sends.append(fw)
        for k, (px, py) in enumerate(_chip_peers()):
            for i, nm in enumerate(names):
                other = _half_of(outs[nm].at[2 * px + py], nm, 1 - c)
                _remote(other, other, d2d_send.at[k * n + i], d2d_recv.at[k * n + i], sib).wait_recv()
        for cp in sends:
            cp.wait_send()
        for cp in own:
            cp.wait()

    return _hbm_call(body, "ag_weights", shards, lambda nm, a: jax.ShapeDtypeStruct((N_CHIPS,) + a.shape, a.dtype),
                     (3 * n, 3 * n, 3 * n, 3 * n, n, n))


_SEM = pl.BlockSpec(memory_space=pltpu.SEMAPHORE)


def push_start(name, arrays, land_shapes, copies, n_copies):
    names = tuple(arrays)
    n = len(names)

    def body(*refs):
        send_sems, recv_sems, token = refs[2 * n], refs[2 * n + 1], refs[-1]
        for j, (src, dst, dev) in enumerate(copies(refs[:n], refs[n:2 * n])):
            _remote(src, dst, send_sems.at[j], recv_sems.at[j], dev).start()
        token[...] = jnp.zeros_like(token)

    hbm = lambda a: pltpu.with_memory_space_constraint(a, pltpu.HBM)
    lands = [lax.empty(land_shapes[nm], arrays[nm].dtype) for nm in names]
    res = pl.pallas_call(
        body, name=name,
        out_shape=(pltpu.SemaphoreType.DMA((n_copies,)), pltpu.SemaphoreType.DMA((n_copies,)),
                   *[pltpu.HBM(arrays[nm].shape, arrays[nm].dtype) for nm in names],
                   *[pltpu.HBM(a.shape, a.dtype) for a in lands], jax.ShapeDtypeStruct((8, 128), F32)),
        in_specs=[_HBM] * (2 * n),
        out_specs=(_SEM, _SEM, *[_HBM] * (2 * n), pl.BlockSpec(memory_space=pltpu.VMEM)),
        input_output_aliases={i: 2 + i for i in range(2 * n)},
        compiler_params=pltpu.CompilerParams(has_side_effects=pltpu.SideEffectType.DATAFLOW_SIDE_EFFECTING),
    )(*[hbm(arrays[nm]) for nm in names], *[hbm(a) for a in lands])
    return names, res[0], res[1], res[2:2 + n], res[2 + n:2 + 2 * n], res[-1]


def push_wait(name, started, copies, after):
    names, send_sems, recv_sems, srcs, lands, _ = started
    n = len(names)

    def body(*refs):
        send_ref, recv_ref = refs[2 * n], refs[2 * n + 1]
        for j, (src, dst, dev) in enumerate(copies(refs[:n], refs[n:2 * n])):
            cp = _remote(src, dst, send_ref.at[j], recv_ref.at[j], dev)
            cp.wait_send()
            cp.wait_recv()

    res = pl.pallas_call(
        body, name=name,
        out_shape=(*[pltpu.HBM(a.shape, a.dtype) for a in srcs], *[pltpu.HBM(a.shape, a.dtype) for a in lands]),
        in_specs=[_HBM] * (2 * n) + [_SEM, _SEM, pl.BlockSpec(memory_space=pl.ANY)],
        out_specs=tuple([_HBM] * (2 * n)),
        input_output_aliases={i: i for i in range(2 * n)},
        compiler_params=pltpu.CompilerParams(has_side_effects=pltpu.SideEffectType.DATAFLOW_SIDE_EFFECTING),
    )(*srcs, *lands, send_sems, recv_sems, after)
    return dict(zip(names, res[n:]))


def _gather_copies(srcs, lands):
    x, y, c = lax.axis_index("x"), lax.axis_index("y"), lax.axis_index("c")
    devs = [(px, py, c) for px, py in _chip_peers()] + [(x, y, 1 - c)]
    return [(src, land.at[2 * x + y], dev) for src, land in zip(srcs, lands) for dev in devs]


def _scatter_copies(srcs, lands):
    c = lax.axis_index("c")
    return [(src.at[2 * px + py], land.at[k], (px, py, c))
            for src, land in zip(srcs, lands) for k, (px, py) in enumerate(_chip_peers())]


def sibling_halves(blocks, name):
    names = tuple(blocks)

    def body(*refs):
        n = len(names)
        ins, outs = dict(zip(names, refs[:n])), dict(zip(names, refs[n:2 * n]))
        send_sems, recv_sems = refs[2 * n:]
        x, y, c = lax.axis_index("x"), lax.axis_index("y"), lax.axis_index("c")
        cps = [_remote(_half_of(ins[nm], nm, 1 - c, lead=1), outs[nm], send_sems.at[i], recv_sems.at[i], (x, y, 1 - c))
               for i, nm in enumerate(names)]
        for cp in cps:
            cp.start()
        for cp in cps:
            cp.wait()

    def half_shape(nm, a):
        r, cdim = a.shape[-2:]
        return jax.ShapeDtypeStruct((N_CHIPS, r, cdim // 2) if nm in _SPLIT_COLS else (N_CHIPS, r // 2, cdim), a.dtype)

    return _hbm_call(body, name, blocks, half_shape, (len(names), len(names)))


def scatter_halves(blocks):
    names = tuple(blocks)
    n = len(names)

    def body(*refs):
        ins, outs = dict(zip(names, refs[:n])), dict(zip(names, refs[n:2 * n]))
        send_sems, recv_sems = refs[2 * n:]
        c = lax.axis_index("c")
        cps = [_remote(ins[nm].at[2 * px + py], outs[nm].at[k], send_sems.at[k * n + i], recv_sems.at[k * n + i],
                       (px, py, c))
               for k, (px, py) in enumerate(_chip_peers()) for i, nm in enumerate(names)]
        for cp in cps:
            cp.start()
        for cp in cps:
            cp.wait_recv()
        for cp in cps:
            cp.wait_send()

    return _hbm_call(body, "rs_grads", blocks, lambda nm, a: jax.ShapeDtypeStruct((3,) + a.shape[1:], a.dtype),
                     (3 * n, 3 * n))


def sibling_assemble(arrays, name):
    names = tuple(arrays)

    def body(*refs):
        n = len(names)
        ins, outs = dict(zip(names, refs[:n])), dict(zip(names, refs[n:2 * n]))
        send_sems, recv_sems = refs[2 * n:]
        x, y, c = lax.axis_index("x"), lax.axis_index("y"), lax.axis_index("c")
        cps = [_remote(_half_of(ins[nm], nm, c), _half_of(outs[nm], nm, c), send_sems.at[i], recv_sems.at[i],
                       (x, y, 1 - c)) for i, nm in enumerate(names)]
        for cp in cps:
            cp.start()
        for i, nm in enumerate(names):
            other = _half_of(outs[nm], nm, 1 - c)
            _remote(other, other, send_sems.at[i], recv_sems.at[i], (x, y, 1 - c)).wait_recv()
        for cp in cps:
            cp.wait_send()

    return _hbm_call(body, name, arrays, lambda nm, a: jax.ShapeDtypeStruct(a.shape, a.dtype),
                     (len(names), len(names)), in_place=True)


def all_reduce_small(v):
    def body(v_ref, tot_ref, gath_ref, send_sems, recv_sems):
        x, y, c = lax.axis_index("x"), lax.axis_index("y"), lax.axis_index("c")
        me = 4 * x + 2 * y + c
        gath_ref[me] = v_ref[...]

        def peer(k):
            m = k + 1
            return (x ^ (m >> 2 & 1), y ^ (m >> 1 & 1), c ^ (m & 1))

        sends = [pltpu.make_async_remote_copy(src_ref=v_ref, dst_ref=gath_ref.at[me], send_sem=send_sems.at[k],
                                              recv_sem=recv_sems.at[k], device_id=peer(k), device_id_type=MESH)
                 for k in range(N_DEV - 1)]
        for cp in sends:
            cp.start()
        for k in range(N_DEV - 1):
            px, py, pc = peer(k)
            pltpu.make_async_remote_copy(src_ref=v_ref, dst_ref=gath_ref.at[4 * px + 2 * py + pc],
                                         send_sem=send_sems.at[k], recv_sem=recv_sems.at[k], device_id=peer(k),
                                         device_id_type=MESH).wait_recv()
        for cp in sends:
            cp.wait_send()
        acc = gath_ref[0]
        for d in range(1, N_DEV):
            acc = acc + gath_ref[d]
        tot_ref[...] = acc

    vm = pl.BlockSpec(memory_space=pltpu.VMEM)
    return pl.pallas_call(
        body, name="ar_small", in_specs=[vm], out_specs=[vm, vm],
        out_shape=(jax.ShapeDtypeStruct(v.shape, v.dtype), jax.ShapeDtypeStruct((N_DEV,) + v.shape, v.dtype)),
        scratch_shapes=[pltpu.SemaphoreType.DMA((N_DEV - 1,)), pltpu.SemaphoreType.DMA((N_DEV - 1,))],
    )(v)[0]


def _elementwise(fn, name, ins, n_out, out_dtype=F32):
    r, cdim = ins[0].shape
    tr = _pick(r, tuple(p for p in (488, 256, 128, 104, 64, 32, 16, 8) if p * cdim * 4 <= 2 * 1024 * 1024))
    spec = pl.BlockSpec((tr, cdim), lambda i: (i, 0))

    def body(*refs):
        res = fn(*[ref[...] for ref in refs[:len(ins)]])
        for o_ref, v in zip(refs[len(ins):], res):
            o_ref[...] = v

    return pl.pallas_call(
        body, name=name, grid=(r // tr,), in_specs=[spec] * len(ins), out_specs=[spec] * n_out,
        out_shape=tuple(jax.ShapeDtypeStruct((r, cdim), out_dtype) for _ in range(n_out)),
        compiler_params=pltpu.CompilerParams(dimension_semantics=("parallel",), vmem_limit_bytes=VMEM_LIMIT),
    )(*ins)


def _half_block_specs(nm, shard_shape):
    r, cdim = shard_shape
    if nm in _SPLIT_COLS:
        return (None, r, cdim // 2), (lambda j, c: (j, 0, c))
    return (None, r // 2, cdim), (lambda j, c: (j, c, 0))


def _presum(nm, sel, g32, a):
    blk, at = _half_block_specs(nm, g32.shape[1:])

    def body(s_ref, g_ref, a_ref, o_ref):
        del s_ref
        o_ref[...] = (g_ref[...] + a_ref[...]).astype(BF16)

    return pl.pallas_call(
        body, name="rs_presum_" + nm,
        grid_spec=pltpu.PrefetchScalarGridSpec(
            num_scalar_prefetch=1, grid=(N_CHIPS,),
            in_specs=[pl.BlockSpec(blk, lambda j, s: at(j, s[0])), pl.BlockSpec(blk, lambda j, s: (j, 0, 0))],
            out_specs=pl.BlockSpec(blk, lambda j, s: (j, 0, 0))),
        out_shape=jax.ShapeDtypeStruct(a.shape, BF16),
        compiler_params=pltpu.CompilerParams(dimension_semantics=("parallel",), vmem_limit_bytes=VMEM_LIMIT),
    )(sel, g32, a)


def _finalsum(nm, sel, g32, a, got):
    blk, at = _half_block_specs(nm, g32.shape[1:])

    def body(s_ref, g_ref, a_ref, r_ref, o_ref):
        del s_ref
        acc = g_ref[...] + a_ref[...]
        for k in range(3):
            acc = acc + r_ref[k].astype(F32)
        o_ref[...] = acc

    return pl.pallas_call(
        body, name="rs_final_" + nm,
        grid_spec=pltpu.PrefetchScalarGridSpec(
            num_scalar_prefetch=1, grid=(1,),
            in_specs=[pl.BlockSpec(blk, lambda i, s: at(s[1], s[0])), pl.BlockSpec(blk, lambda i, s: (s[1], 0, 0)),
                      pl.BlockSpec(got.shape, lambda i, s: (0, 0, 0))],
            out_specs=pl.BlockSpec(blk[1:], lambda i, s: at(0, s[0])[1:])),
        out_shape=jax.ShapeDtypeStruct(g32.shape[1:], F32),
        compiler_params=pltpu.CompilerParams(dimension_semantics=("arbitrary",), vmem_limit_bytes=VMEM_LIMIT),
    )(sel, g32, a, got)


def _adamw(w, g, m, v, name):
    shape = w.shape
    to2 = lambda a: a.reshape(-1, shape[-1])

    def fn(w_, g_, m_, v_):
        m_new = ADAM_B1 * m_ + (1.0 - ADAM_B1) * g_
        v_new = ADAM_B2 * v_ + (1.0 - ADAM_B2) * (g_ * g_)
        m_hat = m_new / (1.0 - ADAM_B1 ** ADAM_STEP)
        v_hat = v_new / (1.0 - ADAM_B2 ** ADAM_STEP)
        delta = -ADAM_LR * (m_hat / (jnp.sqrt(v_hat) + ADAM_EPS) + ADAM_WD * w_)
        return g_, delta, m_new, v_new

    outs = _elementwise(fn, name, [to2(a) for a in (w, g, m, v)], 4)
    return tuple(o.reshape(shape) for o in outs)


_BIG = ("w_mod", "w_in", "w_pa", "w_pd", "w_out", "w_up", "w_down")
_EARLY = ("w_mod", "w_in")
_LATE = ("w_pa", "w_pd", "w_out", "w_up", "w_down")
_COL_SHARDED = ("w_mod", "w_up")
_FULL_SHAPE = {"w_mod": (D_MODEL, MOD_W), "w_in": (IN_COLS, D_MODEL), "w_pa": (Q_W, D_MODEL), "w_pd": (GDN_W, D_MODEL),
               "w_out": (D_MODEL, D_MODEL), "w_up": (D_MODEL, 2 * D_FF), "w_down": (D_FF, D_MODEL)}


def _shard_shape(name):
    r, cdim = _FULL_SHAPE[name]
    return (r, cdim // N_CHIPS) if name in _COL_SHARDED else (r // N_CHIPS, cdim)


_CONV_ELEMS = 2 * (3 * CONV_W // N_CHIPS + 3 * 2 * D_FF // N_CHIPS)
_CONV_ROWS = 32


def _blocks_of_full(name, full):
    r, cdim = _FULL_SHAPE[name]
    if name in _COL_SHARDED:
        return full.reshape(r, N_CHIPS, cdim // N_CHIPS).transpose(1, 0, 2)
    return full.reshape(N_CHIPS, r // N_CHIPS, cdim)


def _full_of_blocks(name, blocks):
    r, cdim = _FULL_SHAPE[name]
    if name in _COL_SHARDED:
        return blocks.transpose(1, 0, 2).reshape(r, cdim)
    return blocks.reshape(r, cdim)


def _w_in_regroup(w_in_t):
    main = jnp.concatenate([w_in_t[:SMALL_AT], w_in_t[SMALL_AT + 4 * GDN_HEADS:]], axis=0)
    small = jnp.pad(w_in_t[SMALL_AT:SMALL_AT + 4 * GDN_HEADS], ((0, HEAD_DIM - 4 * GDN_HEADS), (0, 0)))
    return main, small


def _w_in_ungroup(main, small):
    return jnp.concatenate([main[:SMALL_AT], small[:4 * GDN_HEADS], main[SMALL_AT:]], axis=0)


_SMALL = ("c_ctx", "b_mod", "q_norm_w", "k_norm_w", "conv_qkv_w", "a_log", "dt_bias", "gdn_norm_w", "ffn_conv_w",
          "ffn_conv_b", "final_norm_w")


def _pack_small(tree, rows):
    flat = jnp.concatenate([tree[nm].reshape(-1) for nm in _SMALL])
    return jnp.pad(flat, (0, rows * 128 - flat.shape[0])).reshape(rows, 128)


def _unpack_small(packed, like):
    flat, out, off = packed.reshape(-1), {}, 0
    for nm in _SMALL:
        size = int(np.prod(like[nm].shape))
        out[nm] = flat[off:off + size].reshape(like[nm].shape)
        off += size
    return out


def kernel(x, c, ctx, c_ctx, w_mod, b_mod, w_in, q_norm_w, k_norm_w, conv_qkv_w, a_log, dt_bias, gdn_norm_w, w_pa, w_pd, w_out, w_up, ffn_conv_w, ffn_conv_b, w_down, final_norm_w, loss_target, m_c_ctx, m_w_mod, m_b_mod, m_w_in, m_q_norm_w, m_k_norm_w, m_conv_qkv_w, m_a_log, m_dt_bias, m_gdn_norm_w, m_w_pa, m_w_pd, m_w_out, m_w_up, m_ffn_conv_w, m_ffn_conv_b, m_w_down, m_final_norm_w, v_c_ctx, v_w_mod, v_b_mod, v_w_in, v_q_norm_w, v_k_norm_w, v_conv_qkv_w, v_a_log, v_dt_bias, v_gdn_norm_w, v_w_pa, v_w_pd, v_w_out, v_w_up, v_ffn_conv_w, v_ffn_conv_b, v_w_down, v_final_norm_w):
    names = ("c_ctx", "w_mod", "b_mod", "w_in", "q_norm_w", "k_norm_w", "conv_qkv_w", "a_log", "dt_bias", "gdn_norm_w",
             "w_pa", "w_pd", "w_out", "w_up", "ffn_conv_w", "ffn_conv_b", "w_down", "final_norm_w")
    w_sh = dict(c_ctx=c_ctx, w_mod=w_mod, b_mod=b_mod, w_in=w_in, q_norm_w=q_norm_w, k_norm_w=k_norm_w,
                conv_qkv_w=conv_qkv_w, a_log=a_log, dt_bias=dt_bias, gdn_norm_w=gdn_norm_w, w_pa=w_pa, w_pd=w_pd,
                w_out=w_out, w_up=w_up, ffn_conv_w=ffn_conv_w, ffn_conv_b=ffn_conv_b, w_down=w_down,
                final_norm_w=final_norm_w)
    m_sh = dict(c_ctx=m_c_ctx, w_mod=m_w_mod, b_mod=m_b_mod, w_in=m_w_in, q_norm_w=m_q_norm_w, k_norm_w=m_k_norm_w,
                conv_qkv_w=m_conv_qkv_w, a_log=m_a_log, dt_bias=m_dt_bias, gdn_norm_w=m_gdn_norm_w, w_pa=m_w_pa,
                w_pd=m_w_pd, w_out=m_w_out, w_up=m_w_up, ffn_conv_w=m_ffn_conv_w, ffn_conv_b=m_ffn_conv_b,
                w_down=m_w_down, final_norm_w=m_final_norm_w)
    v_sh = dict(c_ctx=v_c_ctx, w_mod=v_w_mod, b_mod=v_b_mod, w_in=v_w_in, q_norm_w=v_q_norm_w, k_norm_w=v_k_norm_w,
                conv_qkv_w=v_conv_qkv_w, a_log=v_a_log, dt_bias=v_dt_bias, gdn_norm_w=v_gdn_norm_w, w_pa=v_w_pa,
                w_pd=v_w_pd, w_out=v_w_out, w_up=v_w_up, ffn_conv_w=v_ffn_conv_w, ffn_conv_b=v_ffn_conv_b,
                w_down=v_w_down, final_norm_w=v_final_norm_w)
    chip = 2 * lax.axis_index("x") + lax.axis_index("y")

    conv_bits = jnp.concatenate([lax.bitcast_convert_type(w_sh[nm][0], BF16).reshape(-1)
                                 for nm in ("conv_qkv_w", "ffn_conv_w")])
    shards = {nm: w_sh[nm][0].astype(BF16).T if nm == "w_in" else w_sh[nm][0].astype(BF16) for nm in _BIG}
    shards["conv"] = jnp.pad(conv_bits, (0, _CONV_ROWS * D_MODEL - _CONV_ELEMS)).reshape(_CONV_ROWS, D_MODEL)
    gathered = all_gather_chips({nm: shards[nm] for nm in _EARLY + ("conv",)})
    gathered, late_shards = lax.optimization_barrier((gathered, {nm: shards[nm] for nm in _LATE}))
    started = push_start("ag_late_start", late_shards, {nm: (N_CHIPS,) + a.shape for nm, a in late_shards.items()},
                         _gather_copies, 4 * len(_LATE))
    c = c + started[-1][0:1, 0:1]

    wb = {nm: _full_of_blocks(nm, gathered[nm]) for nm in _EARLY}
    wb["w_in_main"], wb["w_in_small"] = _w_in_regroup(wb.pop("w_in"))
    conv_all = gathered["conv"].reshape(N_CHIPS, -1)[:, :_CONV_ELEMS]
    n_cq = 2 * 3 * CONV_W // N_CHIPS
    unbits = lambda a, w: lax.bitcast_convert_type(a.reshape(N_CHIPS, 3, w // N_CHIPS, 2), F32).transpose(1, 0, 2).reshape(3, w)
    ws = dict(c_ctx=c_ctx, b_mod=b_mod, q_norm_w=q_norm_w, k_norm_w=k_norm_w, a_log=a_log[0], dt_bias=dt_bias[0],
              gdn_norm_w=gdn_norm_w, ffn_conv_b=ffn_conv_b, final_norm_w=final_norm_w,
              conv_qkv_w=unbits(conv_all[:, :n_cq], CONV_W), ffn_conv_w=unbits(conv_all[:, n_cq:], 2 * D_FF))
    wz = {nm: jnp.zeros(a.shape, F32) for nm, a in wb.items()}
    wz.update({nm: jnp.zeros(_FULL_SHAPE[nm], F32) for nm in _LATE})

    mixed, vjp_mix = jax.vjp(lambda x_, wz_, ws_: token_mixing(x_, wz_, wb, ws_, c, ctx[0]), x[0], wz, ws)
    got = push_wait("ag_late_wait", started, _gather_copies, mixed["gdn_x"])
    wb_late = {nm: _full_of_blocks(nm, got[nm]) for nm in _LATE}
    loss_local, vjp_chan = jax.vjp(
        lambda mixed_, wz_, ws_: channel_mixing(mixed_, wz_, wb_late, ws_, loss_target[0]), mixed, wz, ws)
    d_mixed, gz_chan, gs_chan = vjp_chan(jnp.ones((), F32))

    sel = jnp.stack([lax.axis_index("c"), chip]).astype(jnp.int32)
    g32_late = {nm: _blocks_of_full(nm, gz_chan[nm]) for nm in _LATE}
    theirs_late = sibling_halves(g32_late, "rs_sibling_late")
    sums_late = {nm: _presum(nm, sel, g32_late[nm], theirs_late[nm]) for nm in _LATE}
    scattering = push_start("rs_late_start", sums_late, {nm: (3,) + a.shape[1:] for nm, a in sums_late.items()},
                            _scatter_copies, 3 * len(_LATE))
    d_mixed = {**d_mixed, "gdn_x": d_mixed["gdn_x"] + scattering[-1][0:1, 0:1]}
    gx, gz_mix, gs_mix = vjp_mix(d_mixed)
    gs = jax.tree.map(jnp.add, gs_mix, gs_chan)
    got_late = push_wait("rs_late_wait", scattering, _scatter_copies, gx)
    loss = lax.psum(loss_local, ("x", "y", "c"))

    gs["a_log"], gs["dt_bias"] = gs["a_log"][None], gs["dt_bias"][None]
    like = {nm: gs[nm] for nm in _SMALL}
    small_rows = -(-sum(int(np.prod(like[nm].shape)) for nm in _SMALL) // 1024) * 8
    small_sum = all_reduce_small(_pack_small(gs, small_rows))
    gz_mix["w_mod"], small_sum = lax.optimization_barrier((gz_mix["w_mod"], small_sum))
    g_small = _unpack_small(small_sum, like)
    for nm, width in (("conv_qkv_w", CONV_W), ("ffn_conv_w", 2 * D_FF)):
        g_small[nm] = lax.dynamic_slice_in_dim(g_small[nm], chip * (width // N_CHIPS), width // N_CHIPS, axis=1)[None]

    gz_mix["w_in"] = _w_in_ungroup(gz_mix.pop("w_in_main"), gz_mix.pop("w_in_small"))
    g32 = {nm: _blocks_of_full(nm, gz_mix[nm]) for nm in _EARLY}
    theirs = sibling_halves(g32, "rs_sibling")
    sums = {nm: _presum(nm, sel, g32[nm], theirs[nm]) for nm in _EARLY}
    scattering, zero = {}, None
    for nm in _EARLY:
        src = sums[nm] if zero is None else sums[nm] + zero.astype(BF16)
        scattering[nm] = push_start("rs_early_start_" + nm, {nm: src}, {nm: (3,) + src.shape[1:]},
                                    _scatter_copies, 3)
        zero = scattering[nm][-1][0:1, 0:1]
    got_late = {nm: a + zero.astype(BF16) if nm in ("w_up", "w_down") else a for nm, a in got_late.items()}
    g_big = sibling_assemble({nm: _finalsum(nm, sel, g32_late[nm], theirs_late[nm], got_late[nm]) for nm in _LATE},
                             "rs_assemble_late")
    grads, deltas, new_m, new_v = {}, {}, {}, {}

    def adamw_big(nm):
        g = g_big[nm].T if nm == "w_in" else g_big[nm]
        grads[nm], deltas[nm], new_m[nm], new_v[nm] = (
            o[None] for o in _adamw(w_sh[nm][0], g, m_sh[nm][0], v_sh[nm][0], "adamw_" + nm))

    for nm in _LATE:
        adamw_big(nm)

    shard_like = {nm: w_sh[nm] for nm in _SMALL}
    rows_l = -(-sum(int(np.prod(shard_like[nm].shape)) for nm in _SMALL) // 1024) * 8
    g_l = _pack_small({nm: g_small[nm].reshape(w_sh[nm].shape) for nm in _SMALL}, rows_l) + zero
    outs = _adamw(_pack_small(w_sh, rows_l), g_l, _pack_small(m_sh, rows_l), _pack_small(v_sh, rows_l), "adamw_small")
    for tree, packed in zip((grads, deltas, new_m, new_v), outs):
        tree.update(_unpack_small(packed, shard_like))

    done_meanwhile = deltas["w_up"][0, :8, :HEAD_DIM] + outs[1][:8, :]
    for nm in _EARLY:
        got = push_wait("rs_early_wait_" + nm, scattering[nm], _scatter_copies, done_meanwhile)
        g_big.update(sibling_assemble({nm: _finalsum(nm, sel, g32[nm], theirs[nm], got[nm])}, "rs_assemble_" + nm))
        adamw_big(nm)
        done_meanwhile = deltas[nm][0, :8, :HEAD_DIM]

    return (loss, gx[None], *[grads[nm] for nm in names], *[deltas[nm] for nm in names],
            *[new_m[nm] for nm in names], *[new_v[nm] for nm in names])
```

```python
import functools
import math

import jax
import jax.numpy as jnp
import numpy as np
from jax import lax
from jax.experimental import pallas as pl
from jax.experimental.pallas import tpu as pltpu

F32 = jnp.float32
BF16 = jnp.bfloat16
HIGHEST = lax.Precision.HIGHEST
MESH = pl.DeviceIdType.MESH

D_MODEL = 1024
GRID_W = 64
ATTN_HEADS = 8
ATTN_KV_HEADS = 2
ATTN_GROUP = ATTN_HEADS // ATTN_KV_HEADS
HEAD_DIM = 128
ROPE_THETA = 10000.0
GDN_HEADS = 8
GDN_CHUNK = 64
D_FF = 2816
NORM_EPS = 1e-6
KV_W = ATTN_KV_HEADS * HEAD_DIM
Q_W = ATTN_HEADS * HEAD_DIM
GDN_W = GDN_HEADS * HEAD_DIM
CONV_W = 3 * GDN_W
MOD_W = 6 * D_MODEL
IN_COLS = 2 * KV_W + CONV_W + 4 * GDN_HEADS + Q_W + GDN_W + 2 * D_MODEL
IN_MAIN = IN_COLS - 4 * GDN_HEADS
SMALL_AT = 2 * KV_W + CONV_W
N_CHIPS = 4
N_DEV = 8

ADAM_LR = 0.001
ADAM_B1 = 0.9
ADAM_B2 = 0.999
ADAM_EPS = 1e-08
ADAM_WD = 0.01
ADAM_STEP = 10

VMEM_LIMIT = 48 * 1024 * 1024
MATMUL_VMEM_BUDGET = 40 * 1024 * 1024
MATMUL_STEP_BYTES = 1200 * 1024


def _pick(dim, prefs):
    for p in prefs:
        if p <= dim and dim % p == 0:
            return p
    return dim


_DIMS = {
    "nn": (((1,), (0,)), ((), ())),
    "nt": (((1,), (1,)), ((), ())),
    "tn": (((0,), (0,)), ((), ())),
}


def _matmul_plan(m, n, k, a_bytes, b_bytes):
    best = None
    for tm in (2304, 2048, 1152, 1024, 768, 512, 384, 256, 128, m):
        for tn in (2560, 1536, 1408, 1024, 768, 512, 256, 128, n):
            for tk in (3840, 2816, 2560, 2304, 2048, 1920, 1408, 1152, 1024, 768, 512, 256, 128, k):
                if tm > m or tn > n or tk > k or m % tm or n % tn or k % tk:
                    continue
                blocks = tm * tk * a_bytes + tk * tn * b_bytes + tm * tn * 4
                casts = (tm * tk * 2 if a_bytes > 2 else 0) + (tk * tn * 2 if b_bytes > 2 else 0) + tm * tn * 4
                if 2 * blocks + casts > MATMUL_VMEM_BUDGET:
                    continue
                nm, nn, nk = m // tm, n // tn, k // tk
                size_a, size_b = m * k * a_bytes, k * n * b_bytes
                for n_inner in (True, False):
                    if n_inner:
                        traffic = (size_a if nk == 1 else nn * size_a) + nm * size_b
                    else:
                        traffic = nn * size_a + (size_b if nk == 1 else nm * size_b)
                    cost = traffic + nm * nn * nk * MATMUL_STEP_BYTES + (nk - 1) * m * n * 4
                    if best is None or cost < best[0]:
                        best = (cost, tm, tn, tk, n_inner)
    return best[1:]


def _matmul(a, b, mode, name):
    if mode == "nn":
        (m, k), (_, n) = a.shape, b.shape
    elif mode == "nt":
        (m, k), (n, _) = a.shape, b.shape
    else:
        (k, m), (_, n) = a.shape, b.shape
    tm, tn, tk, n_inner = _matmul_plan(m, n, k, a.dtype.itemsize, b.dtype.itemsize)
    nk = k // tk
    ij = (lambda g0, g1: (g0, g1)) if n_inner else (lambda g0, g1: (g1, g0))
    if mode == "tn":
        a_spec = pl.BlockSpec((tk, tm), lambda g0, g1, l: (l, ij(g0, g1)[0]))
    else:
        a_spec = pl.BlockSpec((tm, tk), lambda g0, g1, l: (ij(g0, g1)[0], l))
    if mode == "nt":
        b_spec = pl.BlockSpec((tn, tk), lambda g0, g1, l: (ij(g0, g1)[1], l))
    else:
        b_spec = pl.BlockSpec((tk, tn), lambda g0, g1, l: (l, ij(g0, g1)[1]))
    dims = _DIMS[mode]

    def body(a_ref, b_ref, o_ref):
        part = lax.dot_general(a_ref[...].astype(BF16), b_ref[...].astype(BF16), dims, preferred_element_type=F32)
        if nk == 1:
            o_ref[...] = part
        else:
            l = pl.program_id(2)

            @pl.when(l == 0)
            def _():
                o_ref[...] = part

            @pl.when(l > 0)
            def _():
                o_ref[...] += part

    return pl.pallas_call(
        body,
        name=name,
        grid=(m // tm, n // tn, nk) if n_inner else (n // tn, m // tm, nk),
        in_specs=[a_spec, b_spec],
        out_specs=pl.BlockSpec((tm, tn), lambda g0, g1, l: ij(g0, g1)),
        out_shape=jax.ShapeDtypeStruct((m, n), F32),
        compiler_params=pltpu.CompilerParams(dimension_semantics=("parallel", "parallel", "arbitrary"),
                                             vmem_limit_bytes=VMEM_LIMIT),
    )(a, b)


@functools.partial(jax.custom_vjp, nondiff_argnums=(3,))
def pmm(a, w, wz, name):
    del wz
    return _matmul(a, w, "nn", name + "_f")


def _pmm_fwd(a, w, wz, name):
    del wz
    return _matmul(a, w, "nn", name + "_f"), (a, w)


def _pmm_bwd(name, res, g):
    a, w = res
    da = _matmul(g, w, "nt", name + "_da")
    if a.shape[0] < 128:
        pad = 128 - a.shape[0]
        at = jnp.pad(a.T, ((0, 0), (0, pad)))
        gp = jnp.pad(g, ((0, pad), (0, 0)))
        dw = _matmul(at, gp, "nn", name + "_dw")
    else:
        dw = _matmul(a, g, "tn", name + "_dw")
    return da, jnp.zeros_like(w), dw


pmm.defvjp(_pmm_fwd, _pmm_bwd)


@functools.partial(jax.custom_vjp, nondiff_argnums=(3,))
def pmm_t(a, wt, wtz, name):
    del wtz
    return _matmul(a, wt, "nt", name + "_f")


def _pmm_t_fwd(a, wt, wtz, name):
    del wtz
    return _matmul(a, wt, "nt", name + "_f"), (a, wt)


def _pmm_t_bwd(name, res, g):
    a, wt = res
    return _matmul(g, wt, "nn", name + "_da"), jnp.zeros_like(wt), _matmul(g, a, "tn", name + "_dw")


pmm_t.defvjp(_pmm_t_fwd, _pmm_t_bwd)


def rowop(fn, name, rows, bcs=(), crows=(), cbcs=(), tr=256):
    rows, bcs, crows, cbcs = tuple(rows), tuple(bcs), tuple(crows), tuple(cbcs)
    n_rows = rows[0].shape[0]
    tr = _pick(n_rows, (tr, 128, 64, 32, 16, 8))
    nr, nb, ncr, ncb = len(rows), len(bcs), len(crows), len(cbcs)
    n_in = nr + nb + ncr + ncb
    grid = (n_rows // tr,)

    def blk(arr):
        return jax.ShapeDtypeStruct((tr, arr.shape[1]), arr.dtype)

    def row_spec(arr):
        return pl.BlockSpec((tr, arr.shape[1]), lambda i: (i, 0))

    def bc_spec(arr):
        return pl.BlockSpec(arr.shape, lambda i: (0, 0))

    out_blk = jax.eval_shape(fn, *[blk(r) for r in rows], *bcs, *[blk(r) for r in crows], *cbcs)
    n_out = len(out_blk)
    out_shape = tuple(jax.ShapeDtypeStruct((n_rows, o.shape[1]), o.dtype) for o in out_blk)
    in_specs = ([row_spec(r) for r in rows] + [bc_spec(b) for b in bcs]
                + [row_spec(r) for r in crows] + [bc_spec(b) for b in cbcs])

    def order(vals):
        return vals

    def fwd_call(args):
        def body(*refs):
            vals = [r[...] for r in refs[:n_in]]
            res = fn(*order(vals))
            for o_ref, r in zip(refs[n_in:], res):
                o_ref[...] = r

        return pl.pallas_call(
            body, name=name + "_f", grid=grid, in_specs=in_specs,
            out_specs=[row_spec(o) for o in out_shape], out_shape=out_shape,
            compiler_params=pltpu.CompilerParams(dimension_semantics=("parallel",), vmem_limit_bytes=VMEM_LIMIT),
        )(*args)

    def bwd_call(args, cts):
        def body(*refs):
            vals = [r[...] for r in refs[:n_in]]
            ct_refs = refs[n_in:n_in + n_out]
            d_rows = refs[n_in + n_out:n_in + n_out + nr]
            d_bcs = refs[n_in + n_out + nr:]
            consts = vals[nr + nb:]
            _, vjp = jax.vjp(lambda *p: fn(*p, *consts), *vals[:nr + nb])
            grads = vjp(tuple(c[...] for c in ct_refs))
            for ref, g in zip(d_rows, grads[:nr]):
                ref[...] = g

            @pl.when(pl.program_id(0) == 0)
            def _():
                for ref in d_bcs:
                    ref[...] = jnp.zeros_like(ref)

            for ref, g in zip(d_bcs, grads[nr:]):
                ref[...] += g

        d_shape = tuple(jax.ShapeDtypeStruct(r.shape, r.dtype) for r in rows + bcs)
        return pl.pallas_call(
            body, name=name + "_b", grid=grid,
            in_specs=in_specs + [row_spec(o) for o in out_shape],
            out_specs=[row_spec(r) for r in rows] + [bc_spec(b) for b in bcs], out_shape=d_shape,
            compiler_params=pltpu.CompilerParams(dimension_semantics=("arbitrary",), vmem_limit_bytes=VMEM_LIMIT),
        )(*args, *cts)

    @jax.custom_vjp
    def op(diff, const):
        return fwd_call(diff + const)

    def op_fwd(diff, const):
        return fwd_call(diff + const), (diff, const)

    def op_bwd(res, cts):
        diff, const = res
        grads = bwd_call(diff + const, tuple(cts))
        return tuple(grads), tuple(jnp.zeros_like(c) for c in const)

    op.defvjp(op_fwd, op_bwd)
    return op(rows + bcs, crows + cbcs)


def colop(fn, name, arrays, uses, n_const, nblk, cw=128):
    arrays = tuple(arrays)
    n_diff = len(arrays) - n_const
    nd = sum(1 for u in uses if u[0] < n_diff)
    assert all(u[0] < n_diff for u in uses[:nd]) and all(u[0] >= n_diff for u in uses[nd:])

    def spec(u):
        return pl.BlockSpec((arrays[u[0]].shape[0], cw), lambda j, off=u[1]: (0, off + j))

    def out_spec(rows):
        return pl.BlockSpec((rows, cw), lambda j: (0, j))

    out_blk = jax.eval_shape(fn, *[jax.ShapeDtypeStruct((arrays[u[0]].shape[0], cw), arrays[u[0]].dtype)
                                   for u in uses])
    out_shape = tuple(jax.ShapeDtypeStruct((o.shape[0], nblk * cw), o.dtype) for o in out_blk)
    params = pltpu.CompilerParams(dimension_semantics=("parallel",), vmem_limit_bytes=VMEM_LIMIT)

    def fwd_call(arrs):
        def body(*refs):
            res = fn(*[r[...] for r in refs[:len(uses)]])
            for o_ref, r in zip(refs[len(uses):], res):
                o_ref[...] = r

        return pl.pallas_call(
            body, name=name + "_f", grid=(nblk,), in_specs=[spec(u) for u in uses],
            out_specs=[out_spec(o.shape[0]) for o in out_shape], out_shape=out_shape, compiler_params=params,
        )(*[arrs[u[0]] for u in uses])

    def bwd_call(arrs, cts):
        def body(*refs):
            vals = [r[...] for r in refs[:len(uses)]]
            ct_refs = refs[len(uses):len(uses) + len(out_shape)]
            _, vjp = jax.vjp(lambda *p: fn(*p, *vals[nd:]), *vals[:nd])
            for ref, g in zip(refs[len(uses) + len(out_shape):], vjp(tuple(c[...] for c in ct_refs))):
                ref[...] = g

        d_shape = tuple(jax.ShapeDtypeStruct((arrays[u[0]].shape[0], nblk * cw), F32) for u in uses[:nd])
        return pl.pallas_call(
            body, name=name + "_b", grid=(nblk,),
            in_specs=[spec(u) for u in uses] + [out_spec(o.shape[0]) for o in out_shape],
            out_specs=[out_spec(s.shape[0]) for s in d_shape], out_shape=d_shape, compiler_params=params,
        )(*[arrs[u[0]] for u in uses], *cts)

    @jax.custom_vjp
    def op(diff, const):
        return fwd_call(diff + const)

    def op_fwd(diff, const):
        return fwd_call(diff + const), (diff, const)

    def op_bwd(res, cts):
        diff, const = res
        d_uses = bwd_call(diff + const, tuple(cts))
        grads = []
        for i in range(n_diff):
            parts = sorted([(u[1], k) for k, u in enumerate(uses[:nd]) if u[0] == i])
            grads.append(d_uses[parts[0][1]] if len(parts) == 1
                         else jnp.concatenate([d_uses[k] for _, k in parts], axis=1))
        return tuple(grads), tuple(jnp.zeros_like(c) for c in const)

    op.defvjp(op_fwd, op_bwd)
    return op(arrays[:n_diff], arrays[n_diff:])


@functools.partial(jax.custom_vjp, nondiff_argnums=(1,))
def _roll_rows(x, k):
    return pltpu.roll(x, k % x.shape[0], 0)


def _roll_rows_fwd(x, k):
    return _roll_rows(x, k), None


def _roll_rows_bwd(k, _, g):
    return (_roll_rows(g, -k),)


_roll_rows.defvjp(_roll_rows_fwd, _roll_rows_bwd)


def _conv3(x, w0, w1, w2, starts):
    rows = lax.broadcasted_iota(jnp.int32, x.shape, 0)
    ends = tuple(s - 1 for s in starts[1:]) + (x.shape[0] - 1,)
    first = functools.reduce(jnp.logical_or, [rows == s for s in starts])
    last = functools.reduce(jnp.logical_or, [rows == e for e in ends])
    prev = jnp.where(first, 0.0, _roll_rows(x, 1))
    nxt = jnp.where(last, 0.0, _roll_rows(x, -1))
    return prev * w0 + x * w1 + nxt * w2


def _rms(x):
    return x * lax.rsqrt(jnp.mean(x * x, axis=-1, keepdims=True) + NORM_EPS)


def _heads(x, n):
    return [x[:, h * HEAD_DIM:(h + 1) * HEAD_DIM] for h in range(n)]


_NT = (((1,), (1,)), ((), ()))
_TN = (((0,), (0,)), ((), ()))
_TQ = 256


_N_SUB = 2


def _sub_rows(ref, i):
    rows = ref.shape[0] // _N_SUB
    return ref[i * rows:(i + 1) * rows, :].astype(BF16)


def _attn_probs(qs, k):
    s = _each(lambda q: lax.dot_general(q, k, _NT, preferred_element_type=F32) * (HEAD_DIM ** -0.5), qs)
    m = _each(lambda a: jnp.max(a, axis=-1, keepdims=True), s)
    e = _each(lambda a, b: jnp.exp(a - b), s, m)
    inv = _each(lambda a: 1.0 / jnp.sum(a, axis=-1, keepdims=True), e)
    return _each(lambda a, b: a * b, e, inv)


def _attn_fwd_call(q, k, v):
    n, t = q.shape[0], k.shape[0]
    tq = _pick(n, (_TQ, 128))

    def body(q_ref, k_ref, v_ref, o_ref):
        vb = v_ref[...].astype(BF16)
        ps = _attn_probs([_sub_rows(q_ref, i) for i in range(_N_SUB)], k_ref[...].astype(BF16))
        rows = tq // _N_SUB
        for i, p in enumerate(ps):
            o_ref[i * rows:(i + 1) * rows, :] = jnp.dot(p.astype(BF16), vb, preferred_element_type=F32)

    return pl.pallas_call(
        body, name="attn_f", grid=(ATTN_HEADS, n // tq),
        in_specs=[pl.BlockSpec((tq, HEAD_DIM), lambda h, i: (i, h)),
                  pl.BlockSpec((t, HEAD_DIM), lambda h, i: (0, h // ATTN_GROUP)),
                  pl.BlockSpec((t, HEAD_DIM), lambda h, i: (0, h // ATTN_GROUP))],
        out_specs=pl.BlockSpec((tq, HEAD_DIM), lambda h, i: (i, h)),
        out_shape=jax.ShapeDtypeStruct(q.shape, F32),
        compiler_params=pltpu.CompilerParams(dimension_semantics=("parallel", "parallel"),
                                             vmem_limit_bytes=VMEM_LIMIT),
    )(q, k, v)


def _attn_bwd_call(q, k, v, do):
    n, t = q.shape[0], k.shape[0]
    tq = _pick(n, (_TQ, 128))

    def body(q_ref, k_ref, v_ref, do_ref, dq_ref, dk_ref, dv_ref):
        @pl.when((pl.program_id(1) == 0) & (pl.program_id(2) == 0))
        def _():
            dk_ref[...] = jnp.zeros_like(dk_ref)
            dv_ref[...] = jnp.zeros_like(dv_ref)

        kb, vb = k_ref[...].astype(BF16), v_ref[...].astype(BF16)
        qs = [_sub_rows(q_ref, i) for i in range(_N_SUB)]
        dos = [_sub_rows(do_ref, i) for i in range(_N_SUB)]
        ps = _attn_probs(qs, kb)
        dps = _each(lambda d: lax.dot_general(d, vb, _NT, preferred_element_type=F32), dos)
        dss = _each(lambda p, dp: (p * (dp - jnp.sum(p * dp, axis=-1, keepdims=True)) * (HEAD_DIM ** -0.5)).astype(BF16),
                    ps, dps)
        rows = tq // _N_SUB
        for i, ds in enumerate(dss):
            dq_ref[i * rows:(i + 1) * rows, :] = jnp.dot(ds, kb, preferred_element_type=F32)
        dk_ref[...] += sum(_each(lambda ds, q: lax.dot_general(ds, q, _TN, preferred_element_type=F32), dss, qs))
        dv_ref[...] += sum(_each(lambda p, d: lax.dot_general(p.astype(BF16), d, _TN, preferred_element_type=F32),
                                 ps, dos))

    q_spec = pl.BlockSpec((tq, HEAD_DIM), lambda kh, g, i: (i, kh * ATTN_GROUP + g))
    kv_spec = pl.BlockSpec((t, HEAD_DIM), lambda kh, g, i: (0, kh))
    return pl.pallas_call(
        body, name="attn_b", grid=(ATTN_KV_HEADS, ATTN_GROUP, n // tq),
        in_specs=[q_spec, kv_spec, kv_spec, q_spec],
        out_specs=[q_spec, kv_spec, kv_spec],
        out_shape=(jax.ShapeDtypeStruct(q.shape, F32), jax.ShapeDtypeStruct(k.shape, F32),
                   jax.ShapeDtypeStruct(v.shape, F32)),
        compiler_params=pltpu.CompilerParams(dimension_semantics=("parallel", "arbitrary", "arbitrary"),
                                             vmem_limit_bytes=VMEM_LIMIT),
    )(q, k, v, do)


@jax.custom_vjp
def attention(q, k, v):
    return _attn_fwd_call(q, k, v)


def _attention_fwd(q, k, v):
    return _attn_fwd_call(q, k, v), (q, k, v)


def _attention_bwd(res, do):
    return _attn_bwd_call(*res, do)


attention.defvjp(_attention_fwd, _attention_bwd)


_C = GDN_CHUNK


def _pdot(a, b):
    return jnp.dot(a, b, precision=lax.Precision.HIGH, preferred_element_type=F32)


@jax.custom_vjp
def _hdot(a, b):
    return jnp.dot(a.astype(BF16), b.astype(BF16), preferred_element_type=F32)


def _hdot_fwd(a, b):
    return _hdot(a, b), (a, b)


def _hdot_bwd(res, g):
    a, b = res
    gb = g.astype(BF16)
    return (lax.dot_general(gb, b.astype(BF16), _NT, preferred_element_type=F32),
            lax.dot_general(a.astype(BF16), gb, _TN, preferred_element_type=F32))


_hdot.defvjp(_hdot_fwd, _hdot_bwd)


def _each(fn, *lists):
    return [fn(*args) for args in zip(*lists)]


@jax.custom_vjp
def _unit_lower_inverse(low, blockdiag):
    return _unit_lower_inverse_chain(low, blockdiag)


def _unit_lower_inverse_fwd(low, blockdiag):
    t_inv = _unit_lower_inverse_chain(low, blockdiag)
    return t_inv, (t_inv, blockdiag)


def _unit_lower_inverse_bwd(res, d_inv):
    t_inv, blockdiag = res
    bf = lambda a: a.astype(BF16)
    left = _each(lambda t, g: lax.dot_general(bf(t), bf(g), _TN, preferred_element_type=F32), t_inv, d_inv)
    d_low = _each(lambda a, t: -lax.dot_general(bf(a), bf(t), _NT, preferred_element_type=F32), left, t_inv)
    return d_low, jnp.zeros_like(blockdiag)


_unit_lower_inverse.defvjp(_unit_lower_inverse_fwd, _unit_lower_inverse_bwd)


def _unit_lower_inverse_chain(low, blockdiag):
    eye = (lax.broadcasted_iota(jnp.int32, (_C, _C), 0) == lax.broadcasted_iota(jnp.int32, (_C, _C), 1)).astype(F32)
    ld = _each(lambda a: a * blockdiag, low)
    lo = _each(lambda a, d: a - d, low, ld)
    l2 = _each(_hdot, ld, ld)
    l4 = _each(_hdot, l2, l2)
    l8 = _each(_hdot, l4, l4)
    td = _each(lambda d, a2: _hdot(eye - d, eye + a2), ld, l2)
    td = _each(lambda t, a4: _hdot(t, eye + a4), td, l4)
    td = _each(lambda t, a8: _hdot(t, eye + a8), td, l8)
    nn = _each(_hdot, td, lo)
    n2 = _each(_hdot, nn, nn)
    out = _each(lambda n, m2: _hdot(eye - n, eye + m2), nn, n2)
    return _each(_hdot, out, td)


@jax.custom_vjp
def _inverse_given(low, t_inv):
    del low
    return t_inv


_inverse_given.defvjp(lambda low, t_inv: (t_inv, (t_inv, low[0])),
                      lambda res, d_inv: (_unit_lower_inverse_bwd(res, d_inv)[0], [jnp.zeros_like(t) for t in res[0]]))


def _gdn_chunks(heads, blockdiag, kept_inverses=None):
    q, k, v, b_b, be_b, e_b, kd_b, m1, dec, gl, s = (list(col) for col in zip(*heads))
    f32dot = lambda a, b: jnp.dot(a, b, preferred_element_type=F32)
    nt = lambda a, b: lax.dot_general(a, b, _NT, preferred_element_type=F32)
    kq_k = _each(lambda kx, qq: nt(jnp.concatenate([kx, qq], axis=0), kx), k, q)
    low = _each(lambda m, a: m * a[:_C], m1, kq_k)
    t_inv = _unit_lower_inverse(low, blockdiag) if kept_inverses is None else _inverse_given(low, kept_inverses)
    uw = _each(lambda t, b, x, be, kx: _hdot(t, jnp.concatenate([b * x, be * kx], axis=1)), t_inv, b_b, v, be_b, k)
    wq_s = _each(lambda a, qq, e, ss: f32dot(jnp.concatenate([a[:, HEAD_DIM:], qq * e], axis=0), ss), uw, q, e_b, s)
    delta = _each(lambda a, ws: a[:, :HEAD_DIM] - ws[:_C], uw, wq_s)
    p = _each(lambda d, a: d * a[_C:], dec, kq_k)
    o = _each(lambda ws, pp, dd: ws[_C:] + f32dot(pp, dd), wq_s, p, delta)
    s_new = _each(lambda g, ss, kx, kd, dd: g * ss + lax.dot_general(kx * kd, dd, _TN, preferred_element_type=F32),
                  gl, s, k, kd_b, delta)
    return (o, s_new) if kept_inverses is not None else (o, s_new, t_inv)


def _blockdiag_mask():
    r = lax.broadcasted_iota(jnp.int32, (_C, _C), 0) >> 4
    c = lax.broadcasted_iota(jnp.int32, (_C, _C), 1) >> 4
    return (r == c).astype(F32)


_N_DIR = 2


def _scan_chunk(s, nc, ncc, reverse):
    return jnp.where(s < ncc, ncc - 1 - s, nc + ncc - 1 - s) if reverse else s


def _gdn_specs(nc, ncc, backward):
    step = (lambda s: nc - 1 - s) if backward else (lambda s: s)
    chunk = [lambda s, d=d: _scan_chunk(step(s), nc, ncc, d == 1) for d in range(_N_DIR)]
    tok = [pl.BlockSpec((_C, 3 * GDN_W), lambda s, d=d: (chunk[d](s), 0)) for d in range(_N_DIR)]
    park = [0, nc - ncc - 1]
    out = [pl.BlockSpec((_C, GDN_W), lambda s, d=d: (jnp.where(chunk[d](s) >= ncc, chunk[d](s) - ncc, park[d]), 0))
           for d in range(_N_DIR)]
    per_tok = pl.BlockSpec((_N_DIR, GDN_HEADS, _C, HEAD_DIM), lambda s: (0, 0, step(s), 0))
    mat = pl.BlockSpec((_N_DIR, GDN_HEADS, None, _C, _C), lambda s: (0, 0, step(s), 0, 0))
    row = pl.BlockSpec((_N_DIR, GDN_HEADS, None, 1, HEAD_DIM), lambda s: (0, 0, step(s), 0, 0))
    state = pl.BlockSpec((_N_DIR, GDN_HEADS, None, HEAD_DIM, HEAD_DIM), lambda s: (0, 0, step(s), 0, 0))
    return tok, out, per_tok, mat, row, state, chunk


def _head_cols(h, part):
    return slice((part * GDN_HEADS + h) * HEAD_DIM, (part * GDN_HEADS + h + 1) * HEAD_DIM)


def _gdn_heads(qkv_refs, factor_refs, state_of):
    return [[qkv_refs[d][:, _head_cols(h, 0)], qkv_refs[d][:, _head_cols(h, 1)], qkv_refs[d][:, _head_cols(h, 2)]]
            + [r[d, h] for r in factor_refs] + [state_of(d, h)]
            for d in range(_N_DIR) for h in range(GDN_HEADS)]


def _gdn_fwd_call(ncc, qkv, factors):
    t = qkv.shape[0]
    nc = t // _C
    tok, out, per_tok, mat, row, state, _ = _gdn_specs(nc, ncc, False)

    def body(*refs):
        qkv_refs, f_refs = refs[:_N_DIR], refs[_N_DIR:_N_DIR + 7]
        o_refs, sall_ref, tall_ref, s_ref = refs[_N_DIR + 7:2 * _N_DIR + 7], refs[2 * _N_DIR + 7], refs[-2], refs[-1]

        @pl.when(pl.program_id(0) == 0)
        def _():
            s_ref[...] = jnp.zeros_like(s_ref)

        heads = _gdn_heads(qkv_refs, f_refs, lambda d, h: s_ref[d, h])
        o, s_new, t_inv = _gdn_chunks(heads, _blockdiag_mask())
        for d in range(_N_DIR):
            for h in range(GDN_HEADS):
                i = GDN_HEADS * d + h
                sall_ref[d, h] = heads[i][10]
                tall_ref[d, h] = t_inv[i]
                o_refs[d][:, _head_cols(h, 0)] = o[i]
                s_ref[d, h] = s_new[i]

    o_shape = jax.ShapeDtypeStruct((t - ncc * _C, GDN_W), F32)
    s_shape = (_N_DIR, GDN_HEADS, nc, HEAD_DIM, HEAD_DIM)
    t_shape = (_N_DIR, GDN_HEADS, nc, _C, _C)
    return pl.pallas_call(
        body, name="gdn_f", grid=(nc,),
        in_specs=[*tok, per_tok, per_tok, per_tok, per_tok, mat, mat, row],
        out_specs=[*out, state, mat],
        out_shape=[o_shape, o_shape, jax.ShapeDtypeStruct(s_shape, F32), jax.ShapeDtypeStruct(t_shape, F32)],
        scratch_shapes=[pltpu.VMEM((_N_DIR, GDN_HEADS, HEAD_DIM, HEAD_DIM), F32)],
        compiler_params=pltpu.CompilerParams(dimension_semantics=("arbitrary",), vmem_limit_bytes=VMEM_LIMIT),
    )(qkv, qkv, *factors)


def _gdn_bwd_call(ncc, qkv, factors, sall, tall, dos):
    t = qkv.shape[0]
    nc = t // _C
    tok, out, per_tok, mat, row, state, chunk = _gdn_specs(nc, ncc, True)

    def body(*refs):
        qkv_refs, f_refs, sall_ref, tall_ref = refs[:_N_DIR], refs[_N_DIR:_N_DIR + 7], refs[_N_DIR + 7], refs[_N_DIR + 8]
        do_refs = refs[_N_DIR + 9:2 * _N_DIR + 9]
        dqkv_refs = refs[2 * _N_DIR + 9:3 * _N_DIR + 9]
        df_refs, ds_ref = refs[3 * _N_DIR + 9:3 * _N_DIR + 16], refs[-1]

        @pl.when(pl.program_id(0) == 0)
        def _():
            ds_ref[...] = jnp.zeros_like(ds_ref)

        bd = _blockdiag_mask()
        heads = _gdn_heads(qkv_refs, f_refs, lambda d, h: sall_ref[d, h])
        kept = [tall_ref[d, h] for d in range(_N_DIR) for h in range(GDN_HEADS)]
        _, vjp = jax.vjp(lambda hs: _gdn_chunks(hs, bd, kept), heads)
        live = [chunk[d](pl.program_id(0)) >= ncc for d in range(_N_DIR)]
        (all_grads,) = vjp(([jnp.where(live[d], do_refs[d][:, _head_cols(h, 0)], 0.0)
                             for d in range(_N_DIR) for h in range(GDN_HEADS)],
                            [ds_ref[d, h] for d in range(_N_DIR) for h in range(GDN_HEADS)]))
        for d in range(_N_DIR):
            for h in range(GDN_HEADS):
                grads = all_grads[GDN_HEADS * d + h]
                for part in range(3):
                    dqkv_refs[d][:, _head_cols(h, part)] = grads[part]
                for ref, g in zip(df_refs, grads[3:10]):
                    ref[d, h] = g
                ds_ref[d, h] = grads[10]

    shp = lambda a: jax.ShapeDtypeStruct(a.shape, F32)
    res = pl.pallas_call(
        body, name="gdn_b", grid=(nc,),
        in_specs=[*tok, per_tok, per_tok, per_tok, per_tok, mat, mat, row, state, mat, *out],
        out_specs=[*tok, per_tok, per_tok, per_tok, per_tok, mat, mat, row],
        out_shape=[shp(qkv), shp(qkv)] + [shp(a) for a in factors],
        scratch_shapes=[pltpu.VMEM((_N_DIR, GDN_HEADS, HEAD_DIM, HEAD_DIM), F32)],
        compiler_params=pltpu.CompilerParams(dimension_semantics=("arbitrary",), vmem_limit_bytes=VMEM_LIMIT),
    )(qkv, qkv, *factors, sall, tall, *dos)
    return res[0] + res[1], tuple(res[_N_DIR:])


@functools.partial(jax.custom_vjp, nondiff_argnums=(0,))
def gdn_scan(ncc, qkv, factors):
    o0, o1, _, _ = _gdn_fwd_call(ncc, qkv, factors)
    return o0, o1


def _gdn_scan_fwd(ncc, qkv, factors):
    o0, o1, sall, tall = _gdn_fwd_call(ncc, qkv, factors)
    return (o0, o1), (qkv, factors, sall, tall)


def _gdn_scan_bwd(ncc, res, dos):
    qkv, factors, sall, tall = res
    return _gdn_bwd_call(ncc, qkv, factors, sall, tall, list(dos))


gdn_scan.defvjp(_gdn_scan_fwd, _gdn_scan_bwd)


def _rope_tables(n, cl):
    t = np.arange(n)
    inv_freq = (ROPE_THETA ** (-np.arange(0, HEAD_DIM // 2, 2, dtype=np.float32) / (HEAD_DIM // 2))).astype(np.float32)
    ang_r = (t // GRID_W).astype(np.float32)[:, None] * inv_freq
    ang_c = (t % GRID_W).astype(np.float32)[:, None] * inv_freq
    cos = np.concatenate([np.cos(ang_r), np.cos(ang_r), np.cos(ang_c), np.cos(ang_c)], axis=1)
    sin = np.concatenate([-np.sin(ang_r), np.sin(ang_r), -np.sin(ang_c), np.sin(ang_c)], axis=1)
    cos_all = np.concatenate([np.ones((cl, HEAD_DIM), np.float32), cos], axis=0)
    sin_all = np.concatenate([np.zeros((cl, HEAD_DIM), np.float32), sin], axis=0)
    j = np.arange(HEAD_DIM)
    src = np.where((j % 64) < 32, j + 32, j - 32)
    perm = np.zeros((HEAD_DIM, HEAD_DIM), np.float32)
    perm[src, j] = 1.0
    return (jnp.asarray(cos.astype(np.float32)), jnp.asarray(sin.astype(np.float32)),
            jnp.asarray(cos_all), jnp.asarray(sin_all), jnp.asarray(perm))


def _gdn_factors(log_a, beta, ncc):
    t = log_a.shape[0]
    nc = t // _C
    la = log_a.reshape(nc, _C, _N_DIR, GDN_HEADS).transpose(2, 3, 0, 1)
    be = beta.reshape(nc, _C, _N_DIR, GDN_HEADS).transpose(2, 3, 0, 1)
    scan_order = lambda a: jnp.stack([a[0], jnp.concatenate([jnp.flip(a[1][:, :ncc], axis=1),
                                                              jnp.flip(a[1][:, ncc:], axis=1)], axis=1)])
    la, be = scan_order(la), scan_order(be)
    rev = jnp.asarray(np.array([False, True])[:, None, None, None])
    run = jnp.cumsum(la, axis=3)
    gam = jnp.where(rev, jnp.sum(la, axis=3, keepdims=True) - run + la, run)
    idx = np.arange(_C)
    incl = jnp.asarray(np.stack([idx[:, None] >= idx[None, :], idx[:, None] <= idx[None, :]])[:, None, None])
    strict = jnp.asarray(np.stack([idx[:, None] > idx[None, :], idx[:, None] < idx[None, :]])[:, None, None])
    dec = jnp.exp(jnp.where(incl, gam[..., :, None] - gam[..., None, :], -jnp.inf))
    m1 = jnp.where(strict, be[..., :, None] * dec, 0.0)
    e = jnp.exp(gam)
    g_last = jnp.where(rev, gam[..., :1], gam[..., -1:])
    lanes = lambda a: jnp.broadcast_to(a.reshape(_N_DIR, GDN_HEADS, t, 1), (_N_DIR, GDN_HEADS, t, HEAD_DIM))
    gl = jnp.broadcast_to(jnp.exp(g_last)[..., None], (_N_DIR, GDN_HEADS, nc, 1, HEAD_DIM))
    return lanes(be), lanes(be * e), lanes(e), lanes(jnp.exp(g_last - gam)), m1, dec, gl


def local_loss(x, wz, wb, ws, c, ctx, target):
    return channel_mixing(token_mixing(x, wz, wb, ws, c, ctx), wz, wb, ws, target)


def token_mixing(x, wz, wb, ws, c, ctx):
    n, cl = x.shape[0], ctx.shape[0]
    cos_q, sin_q, cos_k, sin_k, perm = _rope_tables(n, cl)

    sc_in = jnp.concatenate([jax.nn.silu(c), jax.nn.silu(ws["c_ctx"])[None, :], jnp.zeros((14, D_MODEL), F32)], axis=0)
    mod = pmm(sc_in, wb["w_mod"], wz["w_mod"], "mm_mod") + ws["b_mod"]
    sh1, sc1, g1, sh2, sc2, g2 = [mod[0:1, i * D_MODEL:(i + 1) * D_MODEL] for i in range(6)]
    csh1, csc1 = mod[1:2, 0:D_MODEL], mod[1:2, D_MODEL:2 * D_MODEL]

    def norm_mod(a, sh, sc):
        return (_rms(a) * (1.0 + sc) + sh,)

    (hx,) = rowop(norm_mod, "normmod_x", (x,), (sh1, sc1))
    (hc,) = rowop(norm_mod, "normmod_c", (ctx,), (csh1, csc1))
    h_all = jnp.concatenate([hc, hx], axis=0)
    p_main = pmm_t(h_all, wb["w_in_main"], wz["w_in_main"], "mm_in")
    p_small = pmm_t(h_all, wb["w_in_small"], wz["w_in_small"], "mm_ins")
    ak, av, qkv, aq, z, gate = jnp.split(p_main, [KV_W, 2 * KV_W, SMALL_AT, SMALL_AT + Q_W, SMALL_AT + Q_W + GDN_W],
                                         axis=1)
    db, da = p_small[:, :2 * GDN_HEADS], p_small[:, 2 * GDN_HEADS:4 * GDN_HEADS]

    def qk_prep(nh):
        def fn(a, w, cos, sin, pm):
            outs = []
            for ah in _heads(a, nh):
                y = _rms(ah) * w
                outs.append(y * cos + _pdot(y, pm) * sin)
            return (jnp.concatenate(outs, axis=1),)
        return fn

    (q_x,) = rowop(qk_prep(ATTN_HEADS), "q_prep", (aq[cl:],), (ws["q_norm_w"],), (cos_q, sin_q), (perm,))
    (k_all,) = rowop(qk_prep(ATTN_KV_HEADS), "k_prep", (ak,), (ws["k_norm_w"],), (cos_k, sin_k), (perm,))
    attn_x = attention(q_x, k_all, av)

    cw = ws["conv_qkv_w"]
    normed = jnp.asarray(np.repeat([1.0, 1.0, 0.0], GDN_W)[None, :], F32)
    scale = jnp.asarray(np.repeat([HEAD_DIM ** -0.5, 1.0, 1.0], GDN_W)[None, :], F32)

    def gdn_prep(a, w0, w1, w2, nf, sc):
        s = jax.nn.silu(_conv3(a, w0, w1, w2, (0, cl)))
        inv = lax.rsqrt(jnp.sum(s * s, axis=-1, keepdims=True) + NORM_EPS)
        return (s * jnp.where(nf > 0.0, inv * sc, 1.0),)

    (qkvn,) = colop(gdn_prep, "gdn_prep", (qkv, cw[0:1], cw[1:2], cw[2:3], normed, scale),
                    [(i, 0) for i in range(6)], 2, 3 * GDN_HEADS)
    beta = jax.nn.sigmoid(db).reshape(-1, 2, GDN_HEADS)
    log_a = -jnp.exp(ws["a_log"])[None] * jax.nn.softplus(da.reshape(-1, 2, GDN_HEADS) + ws["dt_bias"][None])
    o_fwd, o_rev = gdn_scan(cl // _C, qkvn, _gdn_factors(log_a, beta, cl // _C))
    o_x = o_fwd + o_rev

    def gdn_out(o, zz, w):
        outs = [_rms(oh) * w * jax.nn.silu(zh) for oh, zh in zip(_heads(o, GDN_HEADS), _heads(zz, GDN_HEADS))]
        return (jnp.concatenate(outs, axis=1),)

    (gdn_x,) = rowop(gdn_out, "gdn_out", (o_x, z[cl:]), (ws["gdn_norm_w"],))
    return dict(x=x, attn_x=attn_x, gdn_x=gdn_x, gate=gate[cl:], g1=g1, sh2=sh2, sc2=sc2, g2=g2)


def channel_mixing(mixed, wz, wb, ws, target):
    x, attn_x, gdn_x, gate = mixed["x"], mixed["attn_x"], mixed["gdn_x"], mixed["gate"]
    g1, sh2, sc2, g2 = mixed["g1"], mixed["sh2"], mixed["sc2"], mixed["g2"]
    pa = pmm(attn_x, wb["w_pa"], wz["w_pa"], "mm_pa")
    pd = pmm(gdn_x, wb["w_pd"], wz["w_pd"], "mm_pd")

    def merge(a, d, g):
        return (jax.nn.sigmoid(g[:, :D_MODEL]) * a + jax.nn.sigmoid(g[:, D_MODEL:]) * d,)

    (y,) = rowop(merge, "merge", (pa, pd, gate))
    mo = pmm(y, wb["w_out"], wz["w_out"], "mm_out")

    def res_norm_mod(xx, m, g, sh, sc):
        x1 = xx + g * m
        return x1, _rms(x1) * (1.0 + sc) + sh

    x1, h2 = rowop(res_norm_mod, "res1", (x, mo), (g1, sh2, sc2))
    up = pmm(h2, wb["w_up"], wz["w_up"], "mm_up")
    fw = ws["ffn_conv_w"]

    def ffn_act(ug, uv, w0g, w0v, w1g, w1v, w2g, w2v, bg, bv):
        g = _conv3(ug, w0g, w1g, w2g, (0,)) + bg
        v = _conv3(uv, w0v, w1v, w2v, (0,)) + bv
        return (jax.nn.silu(g) * v,)

    half = D_FF // HEAD_DIM
    (act,) = colop(ffn_act, "ffn_act", (up, fw[0:1], fw[1:2], fw[2:3], ws["ffn_conv_b"]),
                   [(i, off) for i in range(5) for off in (0, half)], 0, half)
    dn = pmm(act, wb["w_down"], wz["w_down"], "mm_down")

    def head(xx, m, g, w, tgt):
        yy = _rms(xx + g * m) * w
        err = (yy - tgt) ** 2
        return (jnp.broadcast_to(0.5 * jnp.mean(err, axis=-1, keepdims=True), (xx.shape[0], HEAD_DIM)),)

    (row_loss,) = rowop(head, "head", (x1, dn), (g2, ws["final_norm_w"][None, :]), (target,))
    return jnp.sum(row_loss[:, 0])


_HBM = pl.BlockSpec(memory_space=pltpu.HBM)


def _chip_peers():
    x, y = lax.axis_index("x"), lax.axis_index("y")
    return [(1 - x, y), (x, 1 - y), (1 - x, 1 - y)]


_SPLIT_COLS = ("w_in",)


def _half_of(view, nm, idx, lead=0):
    r, cdim = view.shape[-2:]
    pre = (slice(None),) * lead
    if nm in _SPLIT_COLS:
        return view.at[pre + (slice(None), pl.ds(pl.multiple_of(idx * (cdim // 2), 128), cdim // 2))]
    return view.at[pre + (pl.ds(pl.multiple_of(idx * (r // 2), 16), r // 2), slice(None))]


def _remote(src, dst, send_sem, recv_sem, dev):
    return pltpu.make_async_remote_copy(src_ref=src, dst_ref=dst, send_sem=send_sem, recv_sem=recv_sem,
                                        device_id=dev, device_id_type=MESH)


def _hbm_call(body, name, ins, out_shape, n_sems, in_place=False):
    names = tuple(ins)
    return dict(zip(names, pl.pallas_call(
        body, name=name, in_specs=[_HBM] * len(names), out_specs=[_HBM] * len(names),
        out_shape=[out_shape(nm, ins[nm]) for nm in names],
        scratch_shapes=[pltpu.SemaphoreType.DMA((k,)) for k in n_sems],
        input_output_aliases={i: i for i in range(len(names))} if in_place else {},
    )(*[ins[nm] for nm in names])))


def all_gather_chips(shards):
    names = tuple(shards)
    n = len(names)

    def body(*refs):
        ins, outs = dict(zip(names, refs[:n])), dict(zip(names, refs[n:2 * n]))
        ici_send, ici_recv, d2d_send, d2d_recv, own_send, own_recv = refs[2 * n:]
        x, y, c = lax.axis_index("x"), lax.axis_index("y"), lax.axis_index("c")
        me, sib = 2 * x + y, (x, y, 1 - c)
        own = [_remote(ins[nm], outs[nm].at[me], own_send.at[i], own_recv.at[i], sib) for i, nm in enumerate(names)]
        for cp in own:
            cp.start()
        sends = []
        for k, (px, py) in enumerate(_chip_peers()):
            for i, nm in enumerate(names):
                cp = _remote(_half_of(ins[nm], nm, c), _half_of(outs[nm].at[me], nm, c), ici_send.at[k * n + i],
                             ici_recv.at[k * n + i], (px, py, c))
                cp.start()
                sends.append(cp)
        for k, (px, py) in enumerate(_chip_peers()):
            for i, nm in enumerate(names):
                landed = _half_of(outs[nm].at[2 * px + py], nm, c)
                _remote(landed, landed, ici_send.at[k * n + i], ici_recv.at[k * n + i], (px, py, c)).wait_recv()
                fw = _remote(landed, landed, d2d_send.at[k * n + i], d2d_recv.at[k * n + i], sib)
                fw.start()
                sends.append(fw)
        for k, (px, py) in enumerate(_chip_peers()):
            for i, nm in enumerate(names):
                other = _half_of(outs[nm].at[2 * px + py], nm, 1 - c)
                _remote(other, other, d2d_send.at[k * n + i], d2d_recv.at[k * n + i], sib).wait_recv()
        for cp in sends:
            cp.wait_send()
        for cp in own:
            cp.wait()

    return _hbm_call(body, "ag_weights", shards, lambda nm, a: jax.ShapeDtypeStruct((N_CHIPS,) + a.shape, a.dtype),
                     (3 * n, 3 * n, 3 * n, 3 * n, n, n))


_SEM = pl.BlockSpec(memory_space=pltpu.SEMAPHORE)


def push_start(name, arrays, land_shapes, copies, n_copies):
    names = tuple(arrays)
    n = len(names)

    def body(*refs):
        send_sems, recv_sems, token = refs[2 * n], refs[2 * n + 1], refs[-1]
        for j, (src, dst, dev) in enumerate(copies(refs[:n], refs[n:2 * n])):
            _remote(src, dst, send_sems.at[j], recv_sems.at[j], dev).start()
        token[...] = jnp.zeros_like(token)

    hbm = lambda a: pltpu.with_memory_space_constraint(a, pltpu.HBM)
    lands = [lax.empty(land_shapes[nm], arrays[nm].dtype) for nm in names]
    res = pl.pallas_call(
        body, name=name,
        out_shape=(pltpu.SemaphoreType.DMA((n_copies,)), pltpu.SemaphoreType.DMA((n_copies,)),
                   *[pltpu.HBM(arrays[nm].shape, arrays[nm].dtype) for nm in names],
                   *[pltpu.HBM(a.shape, a.dtype) for a in lands], jax.ShapeDtypeStruct((8, 128), F32)),
        in_specs=[_HBM] * (2 * n),
        out_specs=(_SEM, _SEM, *[_HBM] * (2 * n), pl.BlockSpec(memory_space=pltpu.VMEM)),
        input_output_aliases={i: 2 + i for i in range(2 * n)},
        compiler_params=pltpu.CompilerParams(has_side_effects=pltpu.SideEffectType.DATAFLOW_SIDE_EFFECTING),
    )(*[hbm(arrays[nm]) for nm in names], *[hbm(a) for a in lands])
    return names, res[0], res[1], res[2:2 + n], res[2 + n:2 + 2 * n], res[-1]


def push_wait(name, started, copies, after):
    names, send_sems, recv_sems, srcs, lands, _ = started
    n = len(names)

    def body(*refs):
        send_ref, recv_ref = refs[2 * n], refs[2 * n + 1]
        for j, (src, dst, dev) in enumerate(copies(refs[:n], refs[n:2 * n])):
            cp = _remote(src, dst, send_ref.at[j], recv_ref.at[j], dev)
            cp.wait_send()
            cp.wait_recv()

    res = pl.pallas_call(
        body, name=name,
        out_shape=(*[pltpu.HBM(a.shape, a.dtype) for a in srcs], *[pltpu.HBM(a.shape, a.dtype) for a in lands]),
        in_specs=[_HBM] * (2 * n) + [_SEM, _SEM, pl.BlockSpec(memory_space=pl.ANY)],
        out_specs=tuple([_HBM] * (2 * n)),
        input_output_aliases={i: i for i in range(2 * n)},
        compiler_params=pltpu.CompilerParams(has_side_effects=pltpu.SideEffectType.DATAFLOW_SIDE_EFFECTING),
    )(*srcs, *lands, send_sems, recv_sems, after)
    return dict(zip(names, res[n:]))


def _gather_copies(srcs, lands):
    x, y, c = lax.axis_index("x"), lax.axis_index("y"), lax.axis_index("c")
    devs = [(px, py, c) for px, py in _chip_peers()] + [(x, y, 1 - c)]
    return [(src, land.at[2 * x + y], dev) for src, land in zip(srcs, lands) for dev in devs]


def _scatter_copies(srcs, lands):
    c = lax.axis_index("c")
    return [(src.at[2 * px + py], land.at[k], (px, py, c))
            for src, land in zip(srcs, lands) for k, (px, py) in enumerate(_chip_peers())]


def sibling_halves(blocks, name):
    names = tuple(blocks)

    def body(*refs):
        n = len(names)
        ins, outs = dict(zip(names, refs[:n])), dict(zip(names, refs[n:2 * n]))
        send_sems, recv_sems = refs[2 * n:]
        x, y, c = lax.axis_index("x"), lax.axis_index("y"), lax.axis_index("c")
        cps = [_remote(_half_of(ins[nm], nm, 1 - c, lead=1), outs[nm], send_sems.at[i], recv_sems.at[i], (x, y, 1 - c))
               for i, nm in enumerate(names)]
        for cp in cps:
            cp.start()
        for cp in cps:
            cp.wait()

    def half_shape(nm, a):
        r, cdim = a.shape[-2:]
        return jax.ShapeDtypeStruct((N_CHIPS, r, cdim // 2) if nm in _SPLIT_COLS else (N_CHIPS, r // 2, cdim), a.dtype)

    return _hbm_call(body, name, blocks, half_shape, (len(names), len(names)))


def scatter_halves(blocks):
    names = tuple(blocks)
    n = len(names)

    def body(*refs):
        ins, outs = dict(zip(names, refs[:n])), dict(zip(names, refs[n:2 * n]))
        send_sems, recv_sems = refs[2 * n:]
        c = lax.axis_index("c")
        cps = [_remote(ins[nm].at[2 * px + py], outs[nm].at[k], send_sems.at[k * n + i], recv_sems.at[k * n + i],
                       (px, py, c))
               for k, (px, py) in enumerate(_chip_peers()) for i, nm in enumerate(names)]
        for cp in cps:
            cp.start()
        for cp in cps:
            cp.wait_recv()
        for cp in cps:
            cp.wait_send()

    return _hbm_call(body, "rs_grads", blocks, lambda nm, a: jax.ShapeDtypeStruct((3,) + a.shape[1:], a.dtype),
                     (3 * n, 3 * n))


def sibling_assemble(arrays, name):
    names = tuple(arrays)

    def body(*refs):
        n = len(names)
        ins, outs = dict(zip(names, refs[:n])), dict(zip(names, refs[n:2 * n]))
        send_sems, recv_sems = refs[2 * n:]
        x, y, c = lax.axis_index("x"), lax.axis_index("y"), lax.axis_index("c")
        cps = [_remote(_half_of(ins[nm], nm, c), _half_of(outs[nm], nm, c), send_sems.at[i], recv_sems.at[i],
                       (x, y, 1 - c)) for i, nm in enumerate(names)]
        for cp in cps:
            cp.start()
        for i, nm in enumerate(names):
            other = _half_of(outs[nm], nm, 1 - c)
            _remote(other, other, send_sems.at[i], recv_sems.at[i], (x, y, 1 - c)).wait_recv()
        for cp in cps:
            cp.wait_send()

    return _hbm_call(body, name, arrays, lambda nm, a: jax.ShapeDtypeStruct(a.shape, a.dtype),
                     (len(names), len(names)), in_place=True)


def all_reduce_small(v):
    def body(v_ref, tot_ref, gath_ref, send_sems, recv_sems):
        x, y, c = lax.axis_index("x"), lax.axis_index("y"), lax.axis_index("c")
        me = 4 * x + 2 * y + c
        gath_ref[me] = v_ref[...]

        def peer(k):
            m = k + 1
            return (x ^ (m >> 2 & 1), y ^ (m >> 1 & 1), c ^ (m & 1))

        sends = [pltpu.make_async_remote_copy(src_ref=v_ref, dst_ref=gath_ref.at[me], send_sem=send_sems.at[k],
                                              recv_sem=recv_sems.at[k], device_id=peer(k), device_id_type=MESH)
                 for k in range(N_DEV - 1)]
        for cp in sends:
            cp.start()
        for k in range(N_DEV - 1):
            px, py, pc = peer(k)
            pltpu.make_async_remote_copy(src_ref=v_ref, dst_ref=gath_ref.at[4 * px + 2 * py + pc],
                                         send_sem=send_sems.at[k], recv_sem=recv_sems.at[k], device_id=peer(k),
                                         device_id_type=MESH).wait_recv()
        for cp in sends:
            cp.wait_send()
        acc = gath_ref[0]
        for d in range(1, N_DEV):
            acc = acc + gath_ref[d]
        tot_ref[...] = acc

    vm = pl.BlockSpec(memory_space=pltpu.VMEM)
    return pl.pallas_call(
        body, name="ar_small", in_specs=[vm], out_specs=[vm, vm],
        out_shape=(jax.ShapeDtypeStruct(v.shape, v.dtype), jax.ShapeDtypeStruct((N_DEV,) + v.shape, v.dtype)),
        scratch_shapes=[pltpu.SemaphoreType.DMA((N_DEV - 1,)), pltpu.SemaphoreType.DMA((N_DEV - 1,))],
    )(v)[0]


def _elementwise(fn, name, ins, n_out, out_dtype=F32):
    r, cdim = ins[0].shape
    tr = _pick(r, tuple(p for p in (488, 256, 128, 104, 64, 32, 16, 8) if p * cdim * 4 <= 2 * 1024 * 1024))
    spec = pl.BlockSpec((tr, cdim), lambda i: (i, 0))

    def body(*refs):
        res = fn(*[ref[...] for ref in refs[:len(ins)]])
        for o_ref, v in zip(refs[len(ins):], res):
            o_ref[...] = v

    return pl.pallas_call(
        body, name=name, grid=(r // tr,), in_specs=[spec] * len(ins), out_specs=[spec] * n_out,
        out_shape=tuple(jax.ShapeDtypeStruct((r, cdim), out_dtype) for _ in range(n_out)),
        compiler_params=pltpu.CompilerParams(dimension_semantics=("parallel",), vmem_limit_bytes=VMEM_LIMIT),
    )(*ins)


def _half_block_specs(nm, shard_shape):
    r, cdim = shard_shape
    if nm in _SPLIT_COLS:
        return (None, r, cdim // 2), (lambda j, c: (j, 0, c))
    return (None, r // 2, cdim), (lambda j, c: (j, c, 0))


def _presum(nm, sel, g32, a):
    blk, at = _half_block_specs(nm, g32.shape[1:])

    def body(s_ref, g_ref, a_ref, o_ref):
        del s_ref
        o_ref[...] = (g_ref[...] + a_ref[...]).astype(BF16)

    return pl.pallas_call(
        body, name="rs_presum_" + nm,
        grid_spec=pltpu.PrefetchScalarGridSpec(
            num_scalar_prefetch=1, grid=(N_CHIPS,),
            in_specs=[pl.BlockSpec(blk, lambda j, s: at(j, s[0])), pl.BlockSpec(blk, lambda j, s: (j, 0, 0))],
            out_specs=pl.BlockSpec(blk, lambda j, s: (j, 0, 0))),
        out_shape=jax.ShapeDtypeStruct(a.shape, BF16),
        compiler_params=pltpu.CompilerParams(dimension_semantics=("parallel",), vmem_limit_bytes=VMEM_LIMIT),
    )(sel, g32, a)


def _finalsum(nm, sel, g32, a, got):
    blk, at = _half_block_specs(nm, g32.shape[1:])

    def body(s_ref, g_ref, a_ref, r_ref, o_ref):
        del s_ref
        acc = g_ref[...] + a_ref[...]
        for k in range(3):
            acc = acc + r_ref[k].astype(F32)
        o_ref[...] = acc

    return pl.pallas_call(
        body, name="rs_final_" + nm,
        grid_spec=pltpu.PrefetchScalarGridSpec(
            num_scalar_prefetch=1, grid=(1,),
            in_specs=[pl.BlockSpec(blk, lambda i, s: at(s[1], s[0])), pl.BlockSpec(blk, lambda i, s: (s[1], 0, 0)),
                      pl.BlockSpec(got.shape, lambda i, s: (0, 0, 0))],
            out_specs=pl.BlockSpec(blk[1:], lambda i, s: at(0, s[0])[1:])),
        out_shape=jax.ShapeDtypeStruct(g32.shape[1:], F32),
        compiler_params=pltpu.CompilerParams(dimension_semantics=("arbitrary",), vmem_limit_bytes=VMEM_LIMIT),
    )(sel, g32, a, got)


def _adamw(w, g, m, v, name):
    shape = w.shape
    to2 = lambda a: a.reshape(-1, shape[-1])

    def fn(w_, g_, m_, v_):
        m_new = ADAM_B1 * m_ + (1.0 - ADAM_B1) * g_
        v_new = ADAM_B2 * v_ + (1.0 - ADAM_B2) * (g_ * g_)
        m_hat = m_new / (1.0 - ADAM_B1 ** ADAM_STEP)
        v_hat = v_new / (1.0 - ADAM_B2 ** ADAM_STEP)
        delta = -ADAM_LR * (m_hat / (jnp.sqrt(v_hat) + ADAM_EPS) + ADAM_WD * w_)
        return g_, delta, m_new, v_new

    outs = _elementwise(fn, name, [to2(a) for a in (w, g, m, v)], 4)
    return tuple(o.reshape(shape) for o in outs)


_BIG = ("w_mod", "w_in", "w_pa", "w_pd", "w_out", "w_up", "w_down")
_EARLY = ("w_mod", "w_in")
_LATE = ("w_pa", "w_pd", "w_out", "w_up", "w_down")
_COL_SHARDED = ("w_mod", "w_up")
_FULL_SHAPE = {"w_mod": (D_MODEL, MOD_W), "w_in": (IN_COLS, D_MODEL), "w_pa": (Q_W, D_MODEL), "w_pd": (GDN_W, D_MODEL),
               "w_out": (D_MODEL, D_MODEL), "w_up": (D_MODEL, 2 * D_FF), "w_down": (D_FF, D_MODEL)}


def _shard_shape(name):
    r, cdim = _FULL_SHAPE[name]
    return (r, cdim // N_CHIPS) if name in _COL_SHARDED else (r // N_CHIPS, cdim)


_CONV_ELEMS = 2 * (3 * CONV_W // N_CHIPS + 3 * 2 * D_FF // N_CHIPS)
_CONV_ROWS = 32


def _blocks_of_full(name, full):
    r, cdim = _FULL_SHAPE[name]
    if name in _COL_SHARDED:
        return full.reshape(r, N_CHIPS, cdim // N_CHIPS).transpose(1, 0, 2)
    return full.reshape(N_CHIPS, r // N_CHIPS, cdim)


def _full_of_blocks(name, blocks):
    r, cdim = _FULL_SHAPE[name]
    if name in _COL_SHARDED:
        return blocks.transpose(1, 0, 2).reshape(r, cdim)
    return blocks.reshape(r, cdim)


def _w_in_regroup(w_in_t):
    main = jnp.concatenate([w_in_t[:SMALL_AT], w_in_t[SMALL_AT + 4 * GDN_HEADS:]], axis=0)
    small = jnp.pad(w_in_t[SMALL_AT:SMALL_AT + 4 * GDN_HEADS], ((0, HEAD_DIM - 4 * GDN_HEADS), (0, 0)))
    return main, small


def _w_in_ungroup(main, small):
    return jnp.concatenate([main[:SMALL_AT], small[:4 * GDN_HEADS], main[SMALL_AT:]], axis=0)


_SMALL = ("c_ctx", "b_mod", "q_norm_w", "k_norm_w", "conv_qkv_w", "a_log", "dt_bias", "gdn_norm_w", "ffn_conv_w",
          "ffn_conv_b", "final_norm_w")


def _pack_small(tree, rows):
    flat = jnp.concatenate([tree[nm].reshape(-1) for nm in _SMALL])
    return jnp.pad(flat, (0, rows * 128 - flat.shape[0])).reshape(rows, 128)


def _unpack_small(packed, like):
    flat, out, off = packed.reshape(-1), {}, 0
    for nm in _SMALL:
        size = int(np.prod(like[nm].shape))
        out[nm] = flat[off:off + size].reshape(like[nm].shape)
        off += size
    return out


def kernel(x, c, ctx, c_ctx, w_mod, b_mod, w_in, q_norm_w, k_norm_w, conv_qkv_w, a_log, dt_bias, gdn_norm_w, w_pa, w_pd, w_out, w_up, ffn_conv_w, ffn_conv_b, w_down, final_norm_w, loss_target, m_c_ctx, m_w_mod, m_b_mod, m_w_in, m_q_norm_w, m_k_norm_w, m_conv_qkv_w, m_a_log, m_dt_bias, m_gdn_norm_w, m_w_pa, m_w_pd, m_w_out, m_w_up, m_ffn_conv_w, m_ffn_conv_b, m_w_down, m_final_norm_w, v_c_ctx, v_w_mod, v_b_mod, v_w_in, v_q_norm_w, v_k_norm_w, v_conv_qkv_w, v_a_log, v_dt_bias, v_gdn_norm_w, v_w_pa, v_w_pd, v_w_out, v_w_up, v_ffn_conv_w, v_ffn_conv_b, v_w_down, v_final_norm_w):
    names = ("c_ctx", "w_mod", "b_mod", "w_in", "q_norm_w", "k_norm_w", "conv_qkv_w", "a_log", "dt_bias", "gdn_norm_w",
             "w_pa", "w_pd", "w_out", "w_up", "ffn_conv_w", "ffn_conv_b", "w_down", "final_norm_w")
    w_sh = dict(c_ctx=c_ctx, w_mod=w_mod, b_mod=b_mod, w_in=w_in, q_norm_w=q_norm_w, k_norm_w=k_norm_w,
                conv_qkv_w=conv_qkv_w, a_log=a_log, dt_bias=dt_bias, gdn_norm_w=gdn_norm_w, w_pa=w_pa, w_pd=w_pd,
                w_out=w_out, w_up=w_up, ffn_conv_w=ffn_conv_w, ffn_conv_b=ffn_conv_b, w_down=w_down,
                final_norm_w=final_norm_w)
    m_sh = dict(c_ctx=m_c_ctx, w_mod=m_w_mod, b_mod=m_b_mod, w_in=m_w_in, q_norm_w=m_q_norm_w, k_norm_w=m_k_norm_w,
                conv_qkv_w=m_conv_qkv_w, a_log=m_a_log, dt_bias=m_dt_bias, gdn_norm_w=m_gdn_norm_w, w_pa=m_w_pa,
                w_pd=m_w_pd, w_out=m_w_out, w_up=m_w_up, ffn_conv_w=m_ffn_conv_w, ffn_conv_b=m_ffn_conv_b,
                w_down=m_w_down, final_norm_w=m_final_norm_w)
    v_sh = dict(c_ctx=v_c_ctx, w_mod=v_w_mod, b_mod=v_b_mod, w_in=v_w_in, q_norm_w=v_q_norm_w, k_norm_w=v_k_norm_w,
                conv_qkv_w=v_conv_qkv_w, a_log=v_a_log, dt_bias=v_dt_bias, gdn_norm_w=v_gdn_norm_w, w_pa=v_w_pa,
                w_pd=v_w_pd, w_out=v_w_out, w_up=v_w_up, ffn_conv_w=v_ffn_conv_w, ffn_conv_b=v_ffn_conv_b,
                w_down=v_w_down, final_norm_w=v_final_norm_w)
    chip = 2 * lax.axis_index("x") + lax.axis_index("y")

    conv_bits = jnp.concatenate([lax.bitcast_convert_type(w_sh[nm][0], BF16).reshape(-1)
                                 for nm in ("conv_qkv_w", "ffn_conv_w")])
    shards = {nm: w_sh[nm][0].astype(BF16).T if nm == "w_in" else w_sh[nm][0].astype(BF16) for nm in _BIG}
    shards["conv"] = jnp.pad(conv_bits, (0, _CONV_ROWS * D_MODEL - _CONV_ELEMS)).reshape(_CONV_ROWS, D_MODEL)
    gathered = all_gather_chips({nm: shards[nm] for nm in _EARLY + ("conv",)})
    gathered, late_shards = lax.optimization_barrier((gathered, {nm: shards[nm] for nm in _LATE}))
    started = push_start("ag_late_start", late_shards, {nm: (N_CHIPS,) + a.shape for nm, a in late_shards.items()},
                         _gather_copies, 4 * len(_LATE))
    c = c + started[-1][0:1, 0:1]

    wb = {nm: _full_of_blocks(nm, gathered[nm]) for nm in _EARLY}
    wb["w_in_main"], wb["w_in_small"] = _w_in_regroup(wb.pop("w_in"))
    conv_all = gathered["conv"].reshape(N_CHIPS, -1)[:, :_CONV_ELEMS]
    n_cq = 2 * 3 * CONV_W // N_CHIPS
    unbits = lambda a, w: lax.bitcast_convert_type(a.reshape(N_CHIPS, 3, w // N_CHIPS, 2), F32).transpose(1, 0, 2).reshape(3, w)
    ws = dict(c_ctx=c_ctx, b_mod=b_mod, q_norm_w=q_norm_w, k_norm_w=k_norm_w, a_log=a_log[0], dt_bias=dt_bias[0],
              gdn_norm_w=gdn_norm_w, ffn_conv_b=ffn_conv_b, final_norm_w=final_norm_w,
              conv_qkv_w=unbits(conv_all[:, :n_cq], CONV_W), ffn_conv_w=unbits(conv_all[:, n_cq:], 2 * D_FF))
    wz = {nm: jnp.zeros(a.shape, F32) for nm, a in wb.items()}
    wz.update({nm: jnp.zeros(_FULL_SHAPE[nm], F32) for nm in _LATE})

    mixed, vjp_mix = jax.vjp(lambda x_, wz_, ws_: token_mixing(x_, wz_, wb, ws_, c, ctx[0]), x[0], wz, ws)
    got = push_wait("ag_late_wait", started, _gather_copies, mixed["gdn_x"])
    wb_late = {nm: _full_of_blocks(nm, got[nm]) for nm in _LATE}
    loss_local, vjp_chan = jax.vjp(
        lambda mixed_, wz_, ws_: channel_mixing(mixed_, wz_, wb_late, ws_, loss_target[0]), mixed, wz, ws)
    d_mixed, gz_chan, gs_chan = vjp_chan(jnp.ones((), F32))

    sel = jnp.stack([lax.axis_index("c"), chip]).astype(jnp.int32)
    g32_late = {nm: _blocks_of_full(nm, gz_chan[nm]) for nm in _LATE}
    theirs_late = sibling_halves(g32_late, "rs_sibling_late")
    sums_late = {nm: _presum(nm, sel, g32_late[nm], theirs_late[nm]) for nm in _LATE}
    scattering = push_start("rs_late_start", sums_late, {nm: (3,) + a.shape[1:] for nm, a in sums_late.items()},
                            _scatter_copies, 3 * len(_LATE))
    d_mixed = {**d_mixed, "gdn_x": d_mixed["gdn_x"] + scattering[-1][0:1, 0:1]}
    gx, gz_mix, gs_mix = vjp_mix(d_mixed)
    gs = jax.tree.map(jnp.add, gs_mix, gs_chan)
    got_late = push_wait("rs_late_wait", scattering, _scatter_copies, gx)
    loss = lax.psum(loss_local, ("x", "y", "c"))

    gs["a_log"], gs["dt_bias"] = gs["a_log"][None], gs["dt_bias"][None]
    like = {nm: gs[nm] for nm in _SMALL}
    small_rows = -(-sum(int(np.prod(like[nm].shape)) for nm in _SMALL) // 1024) * 8
    small_sum = all_reduce_small(_pack_small(gs, small_rows))
    gz_mix["w_mod"], small_sum = lax.optimization_barrier((gz_mix["w_mod"], small_sum))
    g_small = _unpack_small(small_sum, like)
    for nm, width in (("conv_qkv_w", CONV_W), ("ffn_conv_w", 2 * D_FF)):
        g_small[nm] = lax.dynamic_slice_in_dim(g_small[nm], chip * (width // N_CHIPS), width // N_CHIPS, axis=1)[None]

    gz_mix["w_in"] = _w_in_ungroup(gz_mix.pop("w_in_main"), gz_mix.pop("w_in_small"))
    g32 = {nm: _blocks_of_full(nm, gz_mix[nm]) for nm in _EARLY}
    theirs = sibling_halves(g32, "rs_sibling")
    sums = {nm: _presum(nm, sel, g32[nm], theirs[nm]) for nm in _EARLY}
    scattering, zero = {}, None
    for nm in _EARLY[::-1]:
        src = sums[nm] if zero is None else sums[nm] + zero.astype(BF16)
        scattering[nm] = push_start("rs_early_start_" + nm, {nm: src}, {nm: (3,) + src.shape[1:]},
                                    _scatter_copies, 3)
        zero = scattering[nm][-1][0:1, 0:1]
    got_late = {nm: a + zero.astype(BF16) if nm in ("w_up", "w_down") else a for nm, a in got_late.items()}
    g_big = sibling_assemble({nm: _finalsum(nm, sel, g32_late[nm], theirs_late[nm], got_late[nm]) for nm in _LATE},
                             "rs_assemble_late")
    grads, deltas, new_m, new_v = {}, {}, {}, {}

    def adamw_big(nm):
        g = g_big[nm].T if nm == "w_in" else g_big[nm]
        grads[nm], deltas[nm], new_m[nm], new_v[nm] = (
            o[None] for o in _adamw(w_sh[nm][0], g, m_sh[nm][0], v_sh[nm][0], "adamw_" + nm))

    for nm in _LATE:
        adamw_big(nm)

    shard_like = {nm: w_sh[nm] for nm in _SMALL}
    rows_l = -(-sum(int(np.prod(shard_like[nm].shape)) for nm in _SMALL) // 1024) * 8
    g_l = _pack_small({nm: g_small[nm].reshape(w_sh[nm].shape) for nm in _SMALL}, rows_l) + zero
    outs = _adamw(_pack_small(w_sh, rows_l), g_l, _pack_small(m_sh, rows_l), _pack_small(v_sh, rows_l), "adamw_small")
    for tree, packed in zip((grads, deltas, new_m, new_v), outs):
        tree.update(_unpack_small(packed, shard_like))

    done_meanwhile = deltas["w_up"][0, :8, :HEAD_DIM] + outs[1][:8, :]
    for nm in _EARLY[::-1]:
        got = push_wait("rs_early_wait_" + nm, scattering[nm], _scatter_copies, done_meanwhile)
        g_big.update(sibling_assemble({nm: _finalsum(nm, sel, g32[nm], theirs[nm], got[nm])}, "rs_assemble_" + nm))
        adamw_big(nm)
        done_meanwhile = deltas[nm][0, :8, :HEAD_DIM]

    return (loss, gx[None], *[grads[nm] for nm in names], *[deltas[nm] for nm in names],
            *[new_m[nm] for nm in names], *[new_v[nm] for nm in names])
```

```python
import functools
import math

import jax
import jax.numpy as jnp
import numpy as np
from jax import lax
from jax.experimental import pallas as pl
from jax.experimental.pallas import tpu as pltpu

F32 = jnp.float32
BF16 = jnp.bfloat16
HIGHEST = lax.Precision.HIGHEST
MESH = pl.DeviceIdType.MESH

D_MODEL = 1024
GRID_W = 64
ATTN_HEADS = 8
ATTN_KV_HEADS = 2
ATTN_GROUP = ATTN_HEADS // ATTN_KV_HEADS
HEAD_DIM = 128
ROPE_THETA = 10000.0
GDN_HEADS = 8
GDN_CHUNK = 64
D_FF = 2816
NORM_EPS = 1e-6
KV_W = ATTN_KV_HEADS * HEAD_DIM
Q_W = ATTN_HEADS * HEAD_DIM
GDN_W = GDN_HEADS * HEAD_DIM
CONV_W = 3 * GDN_W
MOD_W = 6 * D_MODEL
IN_COLS = 2 * KV_W + CONV_W + 4 * GDN_HEADS + Q_W + GDN_W + 2 * D_MODEL
IN_MAIN = IN_COLS - 4 * GDN_HEADS
SMALL_AT = 2 * KV_W + CONV_W
N_CHIPS = 4
N_DEV = 8

ADAM_LR = 0.001
ADAM_B1 = 0.9
ADAM_B2 = 0.999
ADAM_EPS = 1e-08
ADAM_WD = 0.01
ADAM_STEP = 10

VMEM_LIMIT = 48 * 1024 * 1024
MATMUL_VMEM_BUDGET = 40 * 1024 * 1024
MATMUL_STEP_BYTES = 1200 * 1024


def _pick(dim, prefs):
    for p in prefs:
        if p <= dim and dim % p == 0:
            return p
    return dim


_DIMS = {
    "nn": (((1,), (0,)), ((), ())),
    "nt": (((1,), (1,)), ((), ())),
    "tn": (((0,), (0,)), ((), ())),
}


def _matmul_plan(m, n, k, a_bytes, b_bytes, n_unit=None, k_unit=None):
    best = None
    for tm in (2304, 2048, 1152, 1024, 768, 512, 384, 256, 128, m):
        for tn in (2560, 1536, 1408, 1024, 768, 512, 256, 128, n):
            for tk in (3840, 2816, 2560, 2304, 2048, 1920, 1408, 1152, 1024, 768, 512, 256, 128, k):
                if tm > m or tn > n or tk > k or m % tm or n % tn or k % tk:
                    continue
                if (n_unit and n_unit % tn) or (k_unit and k_unit % tk):
                    continue
                blocks = tm * tk * a_bytes + tk * tn * b_bytes + tm * tn * 4
                casts = (tm * tk * 2 if a_bytes > 2 else 0) + (tk * tn * 2 if b_bytes > 2 else 0) + tm * tn * 4
                if 2 * blocks + casts > MATMUL_VMEM_BUDGET:
                    continue
                nm, nn, nk = m // tm, n // tn, k // tk
                size_a, size_b = m * k * a_bytes, k * n * b_bytes
                for n_inner in (True, False):
                    if n_inner:
                        traffic = (size_a if nk == 1 else nn * size_a) + nm * size_b
                    else:
                        traffic = nn * size_a + (size_b if nk == 1 else nm * size_b)
                    cost = traffic + nm * nn * nk * MATMUL_STEP_BYTES + (nk - 1) * m * n * 4
                    if best is None or cost < best[0]:
                        best = (cost, tm, tn, tk, n_inner)
    return best[1:]


def _matmul(a, b, mode, name, blocked=None):
    b_rows, b_cols = (b.shape[-2], b.shape[-1] * (N_CHIPS if blocked == "b" else 1))
    if mode == "nn":
        (m, k), n = a.shape, b_cols
    elif mode == "nt":
        (m, k), n = a.shape, b_rows
    else:
        (k, m), n = a.shape, b_cols
    n_unit = n // N_CHIPS if blocked == "out" or (blocked == "b" and mode != "nt") else None
    k_unit = k // N_CHIPS if blocked == "b" and mode == "nt" else None
    tm, tn, tk, n_inner = _matmul_plan(m, n, k, a.dtype.itemsize, b.dtype.itemsize, n_unit, k_unit)
    nk = k // tk
    ij = (lambda g0, g1: (g0, g1)) if n_inner else (lambda g0, g1: (g1, g0))
    if mode == "tn":
        a_spec = pl.BlockSpec((tk, tm), lambda g0, g1, l: (l, ij(g0, g1)[0]))
    else:
        a_spec = pl.BlockSpec((tm, tk), lambda g0, g1, l: (ij(g0, g1)[0], l))
    if mode == "nt":
        b_blk, b_idx = (tn, tk), lambda g0, g1, l: (ij(g0, g1)[1], l)
    else:
        b_blk, b_idx = (tk, tn), lambda g0, g1, l: (l, ij(g0, g1)[1])

    def in_blocks(blk, idx):
        per = (n_unit or k_unit) // blk[1]
        return (None,) + blk, lambda g0, g1, l: (idx(g0, g1, l)[1] // per, idx(g0, g1, l)[0], idx(g0, g1, l)[1] % per)

    b_spec = pl.BlockSpec(*(in_blocks(b_blk, b_idx) if blocked == "b" else (b_blk, b_idx)))
    o_blk, o_idx = (tm, tn), lambda g0, g1, l: ij(g0, g1)
    o_spec = pl.BlockSpec(*(in_blocks(o_blk, o_idx) if blocked == "out" else (o_blk, o_idx)))
    o_shape = (N_CHIPS, m, n // N_CHIPS) if blocked == "out" else (m, n)
    dims = _DIMS[mode]

    def body(a_ref, b_ref, o_ref):
        part = lax.dot_general(a_ref[...].astype(BF16), b_ref[...].astype(BF16), dims, preferred_element_type=F32)
        if nk == 1:
            o_ref[...] = part
        else:
            l = pl.program_id(2)

            @pl.when(l == 0)
            def _():
                o_ref[...] = part

            @pl.when(l > 0)
            def _():
                o_ref[...] += part

    return pl.pallas_call(
        body,
        name=name,
        grid=(m // tm, n // tn, nk) if n_inner else (n // tn, m // tm, nk),
        in_specs=[a_spec, b_spec],
        out_specs=o_spec,
        out_shape=jax.ShapeDtypeStruct(o_shape, F32),
        compiler_params=pltpu.CompilerParams(dimension_semantics=("parallel", "parallel", "arbitrary"),
                                             vmem_limit_bytes=VMEM_LIMIT),
    )(a, b)


@functools.partial(jax.custom_vjp, nondiff_argnums=(3,))
def pmm(a, w, wz, name):
    del wz
    return _matmul(a, w, "nn", name + "_f", "b" if w.ndim == 3 else None)


def _pmm_fwd(a, w, wz, name):
    del wz
    return _matmul(a, w, "nn", name + "_f", "b" if w.ndim == 3 else None), (a, w)


def _pmm_bwd(name, res, g):
    a, w = res
    da = _matmul(g, w, "nt", name + "_da", "b" if w.ndim == 3 else None)
    dw_form = "out" if w.ndim == 3 else None
    if a.shape[0] < 128:
        pad = 128 - a.shape[0]
        at = jnp.pad(a.T, ((0, 0), (0, pad)))
        gp = jnp.pad(g, ((0, pad), (0, 0)))
        dw = _matmul(at, gp, "nn", name + "_dw", dw_form)
    else:
        dw = _matmul(a, g, "tn", name + "_dw", dw_form)
    return da, jnp.zeros_like(w), dw


pmm.defvjp(_pmm_fwd, _pmm_bwd)


@functools.partial(jax.custom_vjp, nondiff_argnums=(3,))
def pmm_t(a, wt, wtz, name):
    del wtz
    return _matmul(a, wt, "nt", name + "_f")


def _pmm_t_fwd(a, wt, wtz, name):
    del wtz
    return _matmul(a, wt, "nt", name + "_f"), (a, wt)


def _pmm_t_bwd(name, res, g):
    a, wt = res
    return _matmul(g, wt, "nn", name + "_da"), jnp.zeros_like(wt), _matmul(g, a, "tn", name + "_dw")


pmm_t.defvjp(_pmm_t_fwd, _pmm_t_bwd)


def rowop(fn, name, rows, bcs=(), crows=(), cbcs=(), tr=256):
    rows, bcs, crows, cbcs = tuple(rows), tuple(bcs), tuple(crows), tuple(cbcs)
    n_rows = rows[0].shape[0]
    tr = _pick(n_rows, (tr, 128, 64, 32, 16, 8))
    nr, nb, ncr, ncb = len(rows), len(bcs), len(crows), len(cbcs)
    n_in = nr + nb + ncr + ncb
    grid = (n_rows // tr,)

    def blk(arr):
        return jax.ShapeDtypeStruct((tr, arr.shape[1]), arr.dtype)

    def row_spec(arr):
        return pl.BlockSpec((tr, arr.shape[1]), lambda i: (i, 0))

    def bc_spec(arr):
        return pl.BlockSpec(arr.shape, lambda i: (0, 0))

    out_blk = jax.eval_shape(fn, *[blk(r) for r in rows], *bcs, *[blk(r) for r in crows], *cbcs)
    n_out = len(out_blk)
    out_shape = tuple(jax.ShapeDtypeStruct((n_rows, o.shape[1]), o.dtype) for o in out_blk)
    in_specs = ([row_spec(r) for r in rows] + [bc_spec(b) for b in bcs]
                + [row_spec(r) for r in crows] + [bc_spec(b) for b in cbcs])

    def order(vals):
        return vals

    def fwd_call(args):
        def body(*refs):
            vals = [r[...] for r in refs[:n_in]]
            res = fn(*order(vals))
            for o_ref, r in zip(refs[n_in:], res):
                o_ref[...] = r

        return pl.pallas_call(
            body, name=name + "_f", grid=grid, in_specs=in_specs,
            out_specs=[row_spec(o) for o in out_shape], out_shape=out_shape,
            compiler_params=pltpu.CompilerParams(dimension_semantics=("parallel",), vmem_limit_bytes=VMEM_LIMIT),
        )(*args)

    def bwd_call(args, cts):
        def body(*refs):
            vals = [r[...] for r in refs[:n_in]]
            ct_refs = refs[n_in:n_in + n_out]
            d_rows = refs[n_in + n_out:n_in + n_out + nr]
            d_bcs = refs[n_in + n_out + nr:]
            consts = vals[nr + nb:]
            _, vjp = jax.vjp(lambda *p: fn(*p, *consts), *vals[:nr + nb])
            grads = vjp(tuple(c[...] for c in ct_refs))
            for ref, g in zip(d_rows, grads[:nr]):
                ref[...] = g

            @pl.when(pl.program_id(0) == 0)
            def _():
                for ref in d_bcs:
                    ref[...] = jnp.zeros_like(ref)

            for ref, g in zip(d_bcs, grads[nr:]):
                ref[...] += g

        d_shape = tuple(jax.ShapeDtypeStruct(r.shape, r.dtype) for r in rows + bcs)
        return pl.pallas_call(
            body, name=name + "_b", grid=grid,
            in_specs=in_specs + [row_spec(o) for o in out_shape],
            out_specs=[row_spec(r) for r in rows] + [bc_spec(b) for b in bcs], out_shape=d_shape,
            compiler_params=pltpu.CompilerParams(dimension_semantics=("arbitrary",), vmem_limit_bytes=VMEM_LIMIT),
        )(*args, *cts)

    @jax.custom_vjp
    def op(diff, const):
        return fwd_call(diff + const)

    def op_fwd(diff, const):
        return fwd_call(diff + const), (diff, const)

    def op_bwd(res, cts):
        diff, const = res
        grads = bwd_call(diff + const, tuple(cts))
        return tuple(grads), tuple(jnp.zeros_like(c) for c in const)

    op.defvjp(op_fwd, op_bwd)
    return op(rows + bcs, crows + cbcs)


def colop(fn, name, arrays, uses, n_const, nblk, cw=128):
    arrays = tuple(arrays)
    n_diff = len(arrays) - n_const
    nd = sum(1 for u in uses if u[0] < n_diff)
    assert all(u[0] < n_diff for u in uses[:nd]) and all(u[0] >= n_diff for u in uses[nd:])

    def spec(u):
        return pl.BlockSpec((arrays[u[0]].shape[0], cw), lambda j, off=u[1]: (0, off + j))

    def out_spec(rows):
        return pl.BlockSpec((rows, cw), lambda j: (0, j))

    out_blk = jax.eval_shape(fn, *[jax.ShapeDtypeStruct((arrays[u[0]].shape[0], cw), arrays[u[0]].dtype)
                                   for u in uses])
    out_shape = tuple(jax.ShapeDtypeStruct((o.shape[0], nblk * cw), o.dtype) for o in out_blk)
    params = pltpu.CompilerParams(dimension_semantics=("parallel",), vmem_limit_bytes=VMEM_LIMIT)

    def fwd_call(arrs):
        def body(*refs):
            res = fn(*[r[...] for r in refs[:len(uses)]])
            for o_ref, r in zip(refs[len(uses):], res):
                o_ref[...] = r

        return pl.pallas_call(
            body, name=name + "_f", grid=(nblk,), in_specs=[spec(u) for u in uses],
            out_specs=[out_spec(o.shape[0]) for o in out_shape], out_shape=out_shape, compiler_params=params,
        )(*[arrs[u[0]] for u in uses])

    def bwd_call(arrs, cts):
        def body(*refs):
            vals = [r[...] for r in refs[:len(uses)]]
            ct_refs = refs[len(uses):len(uses) + len(out_shape)]
            _, vjp = jax.vjp(lambda *p: fn(*p, *vals[nd:]), *vals[:nd])
            for ref, g in zip(refs[len(uses) + len(out_shape):], vjp(tuple(c[...] for c in ct_refs))):
                ref[...] = g

        d_shape = tuple(jax.ShapeDtypeStruct((arrays[u[0]].shape[0], nblk * cw), F32) for u in uses[:nd])
        return pl.pallas_call(
            body, name=name + "_b", grid=(nblk,),
            in_specs=[spec(u) for u in uses] + [out_spec(o.shape[0]) for o in out_shape],
            out_specs=[out_spec(s.shape[0]) for s in d_shape], out_shape=d_shape, compiler_params=params,
        )(*[arrs[u[0]] for u in uses], *cts)

    @jax.custom_vjp
    def op(diff, const):
        return fwd_call(diff + const)

    def op_fwd(diff, const):
        return fwd_call(diff + const), (diff, const)

    def op_bwd(res, cts):
        diff, const = res
        d_uses = bwd_call(diff + const, tuple(cts))
        grads = []
        for i in range(n_diff):
            parts = sorted([(u[1], k) for k, u in enumerate(uses[:nd]) if u[0] == i])
            grads.append(d_uses[parts[0][1]] if len(parts) == 1
                         else jnp.concatenate([d_uses[k] for _, k in parts], axis=1))
        return tuple(grads), tuple(jnp.zeros_like(c) for c in const)

    op.defvjp(op_fwd, op_bwd)
    return op(arrays[:n_diff], arrays[n_diff:])


@functools.partial(jax.custom_vjp, nondiff_argnums=(1,))
def _roll_rows(x, k):
    return pltpu.roll(x, k % x.shape[0], 0)


def _roll_rows_fwd(x, k):
    return _roll_rows(x, k), None


def _roll_rows_bwd(k, _, g):
    return (_roll_rows(g, -k),)


_roll_rows.defvjp(_roll_rows_fwd, _roll_rows_bwd)


def _conv3(x, w0, w1, w2, starts):
    rows = lax.broadcasted_iota(jnp.int32, x.shape, 0)
    ends = tuple(s - 1 for s in starts[1:]) + (x.shape[0] - 1,)
    first = functools.reduce(jnp.logical_or, [rows == s for s in starts])
    last = functools.reduce(jnp.logical_or, [rows == e for e in ends])
    prev = jnp.where(first, 0.0, _roll_rows(x, 1))
    nxt = jnp.where(last, 0.0, _roll_rows(x, -1))
    return prev * w0 + x * w1 + nxt * w2


def _rms(x):
    return x * lax.rsqrt(jnp.mean(x * x, axis=-1, keepdims=True) + NORM_EPS)


def _heads(x, n):
    return [x[:, h * HEAD_DIM:(h + 1) * HEAD_DIM] for h in range(n)]


_NT = (((1,), (1,)), ((), ()))
_TN = (((0,), (0,)), ((), ()))
_TQ = 256


_N_SUB = 2


def _sub_rows(ref, i):
    rows = ref.shape[0] // _N_SUB
    return ref[i * rows:(i + 1) * rows, :].astype(BF16)


def _attn_probs(qs, k):
    s = _each(lambda q: lax.dot_general(q, k, _NT, preferred_element_type=F32) * (HEAD_DIM ** -0.5), qs)
    m = _each(lambda a: jnp.max(a, axis=-1, keepdims=True), s)
    e = _each(lambda a, b: jnp.exp(a - b), s, m)
    inv = _each(lambda a: 1.0 / jnp.sum(a, axis=-1, keepdims=True), e)
    return _each(lambda a, b: a * b, e, inv)


def _attn_fwd_call(q, k, v):
    n, t = q.shape[0], k.shape[0]
    tq = _pick(n, (_TQ, 128))

    def body(q_ref, k_ref, v_ref, o_ref):
        vb = v_ref[...].astype(BF16)
        ps = _attn_probs([_sub_rows(q_ref, i) for i in range(_N_SUB)], k_ref[...].astype(BF16))
        rows = tq // _N_SUB
        for i, p in enumerate(ps):
            o_ref[i * rows:(i + 1) * rows, :] = jnp.dot(p.astype(BF16), vb, preferred_element_type=F32)

    return pl.pallas_call(
        body, name="attn_f", grid=(ATTN_HEADS, n // tq),
        in_specs=[pl.BlockSpec((tq, HEAD_DIM), lambda h, i: (i, h)),
                  pl.BlockSpec((t, HEAD_DIM), lambda h, i: (0, h // ATTN_GROUP)),
                  pl.BlockSpec((t, HEAD_DIM), lambda h, i: (0, h // ATTN_GROUP))],
        out_specs=pl.BlockSpec((tq, HEAD_DIM), lambda h, i: (i, h)),
        out_shape=jax.ShapeDtypeStruct(q.shape, F32),
        compiler_params=pltpu.CompilerParams(dimension_semantics=("parallel", "parallel"),
                                             vmem_limit_bytes=VMEM_LIMIT),
    )(q, k, v)


def _attn_bwd_call(q, k, v, do):
    n, t = q.shape[0], k.shape[0]
    tq = _pick(n, (_TQ, 128))

    def body(q_ref, k_ref, v_ref, do_ref, dq_ref, dk_ref, dv_ref):
        @pl.when((pl.program_id(1) == 0) & (pl.program_id(2) == 0))
        def _():
            dk_ref[...] = jnp.zeros_like(dk_ref)
            dv_ref[...] = jnp.zeros_like(dv_ref)

        kb, vb = k_ref[...].astype(BF16), v_ref[...].astype(BF16)
        qs = [_sub_rows(q_ref, i) for i in range(_N_SUB)]
        dos = [_sub_rows(do_ref, i) for i in range(_N_SUB)]
        ps = _attn_probs(qs, kb)
        dps = _each(lambda d: lax.dot_general(d, vb, _NT, preferred_element_type=F32), dos)
        dss = _each(lambda p, dp: (p * (dp - jnp.sum(p * dp, axis=-1, keepdims=True)) * (HEAD_DIM ** -0.5)).astype(BF16),
                    ps, dps)
        rows = tq // _N_SUB
        for i, ds in enumerate(dss):
            dq_ref[i * rows:(i + 1) * rows, :] = jnp.dot(ds, kb, preferred_element_type=F32)
        dk_ref[...] += sum(_each(lambda ds, q: lax.dot_general(ds, q, _TN, preferred_element_type=F32), dss, qs))
        dv_ref[...] += sum(_each(lambda p, d: lax.dot_general(p.astype(BF16), d, _TN, preferred_element_type=F32),
                                 ps, dos))

    q_spec = pl.BlockSpec((tq, HEAD_DIM), lambda kh, g, i: (i, kh * ATTN_GROUP + g))
    kv_spec = pl.BlockSpec((t, HEAD_DIM), lambda kh, g, i: (0, kh))
    return pl.pallas_call(
        body, name="attn_b", grid=(ATTN_KV_HEADS, ATTN_GROUP, n // tq),
        in_specs=[q_spec, kv_spec, kv_spec, q_spec],
        out_specs=[q_spec, kv_spec, kv_spec],
        out_shape=(jax.ShapeDtypeStruct(q.shape, F32), jax.ShapeDtypeStruct(k.shape, F32),
                   jax.ShapeDtypeStruct(v.shape, F32)),
        compiler_params=pltpu.CompilerParams(dimension_semantics=("parallel", "arbitrary", "arbitrary"),
                                             vmem_limit_bytes=VMEM_LIMIT),
    )(q, k, v, do)


@jax.custom_vjp
def attention(q, k, v):
    return _attn_fwd_call(q, k, v)


def _attention_fwd(q, k, v):
    return _attn_fwd_call(q, k, v), (q, k, v)


def _attention_bwd(res, do):
    return _attn_bwd_call(*res, do)


attention.defvjp(_attention_fwd, _attention_bwd)


_C = GDN_CHUNK


def _pdot(a, b):
    return jnp.dot(a, b, precision=lax.Precision.HIGH, preferred_element_type=F32)


@jax.custom_vjp
def _hdot(a, b):
    return jnp.dot(a.astype(BF16), b.astype(BF16), preferred_element_type=F32)


def _hdot_fwd(a, b):
    return _hdot(a, b), (a, b)


def _hdot_bwd(res, g):
    a, b = res
    gb = g.astype(BF16)
    return (lax.dot_general(gb, b.astype(BF16), _NT, preferred_element_type=F32),
            lax.dot_general(a.astype(BF16), gb, _TN, preferred_element_type=F32))


_hdot.defvjp(_hdot_fwd, _hdot_bwd)


def _each(fn, *lists):
    return [fn(*args) for args in zip(*lists)]


@jax.custom_vjp
def _unit_lower_inverse(low, blockdiag):
    return _unit_lower_inverse_chain(low, blockdiag)


def _unit_lower_inverse_fwd(low, blockdiag):
    t_inv = _unit_lower_inverse_chain(low, blockdiag)
    return t_inv, (t_inv, blockdiag)


def _unit_lower_inverse_bwd(res, d_inv):
    t_inv, blockdiag = res
    bf = lambda a: a.astype(BF16)
    left = _each(lambda t, g: lax.dot_general(bf(t), bf(g), _TN, preferred_element_type=F32), t_inv, d_inv)
    d_low = _each(lambda a, t: -lax.dot_general(bf(a), bf(t), _NT, preferred_element_type=F32), left, t_inv)
    return d_low, jnp.zeros_like(blockdiag)


_unit_lower_inverse.defvjp(_unit_lower_inverse_fwd, _unit_lower_inverse_bwd)


def _unit_lower_inverse_chain(low, blockdiag):
    eye = (lax.broadcasted_iota(jnp.int32, (_C, _C), 0) == lax.broadcasted_iota(jnp.int32, (_C, _C), 1)).astype(F32)
    ld = _each(lambda a: a * blockdiag, low)
    lo = _each(lambda a, d: a - d, low, ld)
    l2 = _each(_hdot, ld, ld)
    l4 = _each(_hdot, l2, l2)
    l8 = _each(_hdot, l4, l4)
    td = _each(lambda d, a2: _hdot(eye - d, eye + a2), ld, l2)
    td = _each(lambda t, a4: _hdot(t, eye + a4), td, l4)
    td = _each(lambda t, a8: _hdot(t, eye + a8), td, l8)
    nn = _each(_hdot, td, lo)
    n2 = _each(_hdot, nn, nn)
    out = _each(lambda n, m2: _hdot(eye - n, eye + m2), nn, n2)
    return _each(_hdot, out, td)


@jax.custom_vjp
def _inverse_given(low, t_inv):
    del low
    return t_inv


_inverse_given.defvjp(lambda low, t_inv: (t_inv, (t_inv, low[0])),
                      lambda res, d_inv: (_unit_lower_inverse_bwd(res, d_inv)[0], [jnp.zeros_like(t) for t in res[0]]))


def _gdn_chunks(heads, blockdiag, kept_inverses=None):
    q, k, v, b_b, be_b, e_b, kd_b, m1, dec, gl, s = (list(col) for col in zip(*heads))
    f32dot = lambda a, b: jnp.dot(a, b, preferred_element_type=F32)
    nt = lambda a, b: lax.dot_general(a, b, _NT, preferred_element_type=F32)
    kq_k = _each(lambda kx, qq: nt(jnp.concatenate([kx, qq], axis=0), kx), k, q)
    low = _each(lambda m, a: m * a[:_C], m1, kq_k)
    t_inv = _unit_lower_inverse(low, blockdiag) if kept_inverses is None else _inverse_given(low, kept_inverses)
    uw = _each(lambda t, b, x, be, kx: _hdot(t, jnp.concatenate([b * x, be * kx], axis=1)), t_inv, b_b, v, be_b, k)
    wq_s = _each(lambda a, qq, e, ss: f32dot(jnp.concatenate([a[:, HEAD_DIM:], qq * e], axis=0), ss), uw, q, e_b, s)
    delta = _each(lambda a, ws: a[:, :HEAD_DIM] - ws[:_C], uw, wq_s)
    p = _each(lambda d, a: d * a[_C:], dec, kq_k)
    o = _each(lambda ws, pp, dd: ws[_C:] + f32dot(pp, dd), wq_s, p, delta)
    s_new = _each(lambda g, ss, kx, kd, dd: g * ss + lax.dot_general(kx * kd, dd, _TN, preferred_element_type=F32),
                  gl, s, k, kd_b, delta)
    return (o, s_new) if kept_inverses is not None else (o, s_new, t_inv)


def _blockdiag_mask():
    r = lax.broadcasted_iota(jnp.int32, (_C, _C), 0) >> 4
    c = lax.broadcasted_iota(jnp.int32, (_C, _C), 1) >> 4
    return (r == c).astype(F32)


_N_DIR = 2


def _scan_chunk(s, nc, ncc, reverse):
    return jnp.where(s < ncc, ncc - 1 - s, nc + ncc - 1 - s) if reverse else s


def _gdn_specs(nc, ncc, backward):
    step = (lambda s: nc - 1 - s) if backward else (lambda s: s)
    chunk = [lambda s, d=d: _scan_chunk(step(s), nc, ncc, d == 1) for d in range(_N_DIR)]
    tok = [pl.BlockSpec((_C, 3 * GDN_W), lambda s, d=d: (chunk[d](s), 0)) for d in range(_N_DIR)]
    park = [0, nc - ncc - 1]
    out = [pl.BlockSpec((_C, GDN_W), lambda s, d=d: (jnp.where(chunk[d](s) >= ncc, chunk[d](s) - ncc, park[d]), 0))
           for d in range(_N_DIR)]
    per_tok = pl.BlockSpec((_N_DIR, GDN_HEADS, _C, HEAD_DIM), lambda s: (0, 0, step(s), 0))
    mat = pl.BlockSpec((_N_DIR, GDN_HEADS, None, _C, _C), lambda s: (0, 0, step(s), 0, 0))
    row = pl.BlockSpec((_N_DIR, GDN_HEADS, None, 1, HEAD_DIM), lambda s: (0, 0, step(s), 0, 0))
    state = pl.BlockSpec((_N_DIR, GDN_HEADS, None, HEAD_DIM, HEAD_DIM), lambda s: (0, 0, step(s), 0, 0))
    return tok, out, per_tok, mat, row, state, chunk


def _head_cols(h, part):
    return slice((part * GDN_HEADS + h) * HEAD_DIM, (part * GDN_HEADS + h + 1) * HEAD_DIM)


def _gdn_heads(qkv_refs, factor_refs, state_of):
    return [[qkv_refs[d][:, _head_cols(h, 0)], qkv_refs[d][:, _head_cols(h, 1)], qkv_refs[d][:, _head_cols(h, 2)]]
            + [r[d, h] for r in factor_refs] + [state_of(d, h)]
            for d in range(_N_DIR) for h in range(GDN_HEADS)]


def _gdn_fwd_call(ncc, qkv, factors):
    t = qkv.shape[0]
    nc = t // _C
    tok, out, per_tok, mat, row, state, _ = _gdn_specs(nc, ncc, False)

    def body(*refs):
        qkv_refs, f_refs = refs[:_N_DIR], refs[_N_DIR:_N_DIR + 7]
        o_refs, sall_ref, tall_ref, s_ref = refs[_N_DIR + 7:2 * _N_DIR + 7], refs[2 * _N_DIR + 7], refs[-2], refs[-1]

        @pl.when(pl.program_id(0) == 0)
        def _():
            s_ref[...] = jnp.zeros_like(s_ref)

        heads = _gdn_heads(qkv_refs, f_refs, lambda d, h: s_ref[d, h])
        o, s_new, t_inv = _gdn_chunks(heads, _blockdiag_mask())
        for d in range(_N_DIR):
            for h in range(GDN_HEADS):
                i = GDN_HEADS * d + h
                sall_ref[d, h] = heads[i][10]
                tall_ref[d, h] = t_inv[i]
                o_refs[d][:, _head_cols(h, 0)] = o[i]
                s_ref[d, h] = s_new[i]

    o_shape = jax.ShapeDtypeStruct((t - ncc * _C, GDN_W), F32)
    s_shape = (_N_DIR, GDN_HEADS, nc, HEAD_DIM, HEAD_DIM)
    t_shape = (_N_DIR, GDN_HEADS, nc, _C, _C)
    return pl.pallas_call(
        body, name="gdn_f", grid=(nc,),
        in_specs=[*tok, per_tok, per_tok, per_tok, per_tok, mat, mat, row],
        out_specs=[*out, state, mat],
        out_shape=[o_shape, o_shape, jax.ShapeDtypeStruct(s_shape, F32), jax.ShapeDtypeStruct(t_shape, F32)],
        scratch_shapes=[pltpu.VMEM((_N_DIR, GDN_HEADS, HEAD_DIM, HEAD_DIM), F32)],
        compiler_params=pltpu.CompilerParams(dimension_semantics=("arbitrary",), vmem_limit_bytes=VMEM_LIMIT),
    )(qkv, qkv, *factors)


def _gdn_bwd_call(ncc, qkv, factors, sall, tall, dos):
    t = qkv.shape[0]
    nc = t // _C
    tok, out, per_tok, mat, row, state, chunk = _gdn_specs(nc, ncc, True)

    def body(*refs):
        qkv_refs, f_refs, sall_ref, tall_ref = refs[:_N_DIR], refs[_N_DIR:_N_DIR + 7], refs[_N_DIR + 7], refs[_N_DIR + 8]
        do_refs = refs[_N_DIR + 9:2 * _N_DIR + 9]
        dqkv_refs = refs[2 * _N_DIR + 9:3 * _N_DIR + 9]
        df_refs, ds_ref = refs[3 * _N_DIR + 9:3 * _N_DIR + 16], refs[-1]

        @pl.when(pl.program_id(0) == 0)
        def _():
            ds_ref[...] = jnp.zeros_like(ds_ref)

        bd = _blockdiag_mask()
        heads = _gdn_heads(qkv_refs, f_refs, lambda d, h: sall_ref[d, h])
        kept = [tall_ref[d, h] for d in range(_N_DIR) for h in range(GDN_HEADS)]
        _, vjp = jax.vjp(lambda hs: _gdn_chunks(hs, bd, kept), heads)
        live = [chunk[d](pl.program_id(0)) >= ncc for d in range(_N_DIR)]
        (all_grads,) = vjp(([jnp.where(live[d], do_refs[d][:, _head_cols(h, 0)], 0.0)
                             for d in range(_N_DIR) for h in range(GDN_HEADS)],
                            [ds_ref[d, h] for d in range(_N_DIR) for h in range(GDN_HEADS)]))
        for d in range(_N_DIR):
            for h in range(GDN_HEADS):
                grads = all_grads[GDN_HEADS * d + h]
                for part in range(3):
                    dqkv_refs[d][:, _head_cols(h, part)] = grads[part]
                for ref, g in zip(df_refs, grads[3:10]):
                    ref[d, h] = g
                ds_ref[d, h] = grads[10]

    shp = lambda a: jax.ShapeDtypeStruct(a.shape, F32)
    res = pl.pallas_call(
        body, name="gdn_b", grid=(nc,),
        in_specs=[*tok, per_tok, per_tok, per_tok, per_tok, mat, mat, row, state, mat, *out],
        out_specs=[*tok, per_tok, per_tok, per_tok, per_tok, mat, mat, row],
        out_shape=[shp(qkv), shp(qkv)] + [shp(a) for a in factors],
        scratch_shapes=[pltpu.VMEM((_N_DIR, GDN_HEADS, HEAD_DIM, HEAD_DIM), F32)],
        compiler_params=pltpu.CompilerParams(dimension_semantics=("arbitrary",), vmem_limit_bytes=VMEM_LIMIT),
    )(qkv, qkv, *factors, sall, tall, *dos)
    return res[0] + res[1], tuple(res[_N_DIR:])


@functools.partial(jax.custom_vjp, nondiff_argnums=(0,))
def gdn_scan(ncc, qkv, factors):
    o0, o1, _, _ = _gdn_fwd_call(ncc, qkv, factors)
    return o0, o1


def _gdn_scan_fwd(ncc, qkv, factors):
    o0, o1, sall, tall = _gdn_fwd_call(ncc, qkv, factors)
    return (o0, o1), (qkv, factors, sall, tall)


def _gdn_scan_bwd(ncc, res, dos):
    qkv, factors, sall, tall = res
    return _gdn_bwd_call(ncc, qkv, factors, sall, tall, list(dos))


gdn_scan.defvjp(_gdn_scan_fwd, _gdn_scan_bwd)


def _rope_tables(n, cl):
    t = np.arange(n)
    inv_freq = (ROPE_THETA ** (-np.arange(0, HEAD_DIM // 2, 2, dtype=np.float32) / (HEAD_DIM // 2))).astype(np.float32)
    ang_r = (t // GRID_W).astype(np.float32)[:, None] * inv_freq
    ang_c = (t % GRID_W).astype(np.float32)[:, None] * inv_freq
    cos = np.concatenate([np.cos(ang_r), np.cos(ang_r), np.cos(ang_c), np.cos(ang_c)], axis=1)
    sin = np.concatenate([-np.sin(ang_r), np.sin(ang_r), -np.sin(ang_c), np.sin(ang_c)], axis=1)
    cos_all = np.concatenate([np.ones((cl, HEAD_DIM), np.float32), cos], axis=0)
    sin_all = np.concatenate([np.zeros((cl, HEAD_DIM), np.float32), sin], axis=0)
    j = np.arange(HEAD_DIM)
    src = np.where((j % 64) < 32, j + 32, j - 32)
    perm = np.zeros((HEAD_DIM, HEAD_DIM), np.float32)
    perm[src, j] = 1.0
    return (jnp.asarray(cos.astype(np.float32)), jnp.asarray(sin.astype(np.float32)),
            jnp.asarray(cos_all), jnp.asarray(sin_all), jnp.asarray(perm))


def _gdn_factors(log_a, beta, ncc):
    t = log_a.shape[0]
    nc = t // _C
    la = log_a.reshape(nc, _C, _N_DIR, GDN_HEADS).transpose(2, 3, 0, 1)
    be = beta.reshape(nc, _C, _N_DIR, GDN_HEADS).transpose(2, 3, 0, 1)
    scan_order = lambda a: jnp.stack([a[0], jnp.concatenate([jnp.flip(a[1][:, :ncc], axis=1),
                                                              jnp.flip(a[1][:, ncc:], axis=1)], axis=1)])
    la, be = scan_order(la), scan_order(be)
    rev = jnp.asarray(np.array([False, True])[:, None, None, None])
    run = jnp.cumsum(la, axis=3)
    gam = jnp.where(rev, jnp.sum(la, axis=3, keepdims=True) - run + la, run)
    idx = np.arange(_C)
    incl = jnp.asarray(np.stack([idx[:, None] >= idx[None, :], idx[:, None] <= idx[None, :]])[:, None, None])
    strict = jnp.asarray(np.stack([idx[:, None] > idx[None, :], idx[:, None] < idx[None, :]])[:, None, None])
    dec = jnp.exp(jnp.where(incl, gam[..., :, None] - gam[..., None, :], -jnp.inf))
    m1 = jnp.where(strict, be[..., :, None] * dec, 0.0)
    e = jnp.exp(gam)
    g_last = jnp.where(rev, gam[..., :1], gam[..., -1:])
    lanes = lambda a: jnp.broadcast_to(a.reshape(_N_DIR, GDN_HEADS, t, 1), (_N_DIR, GDN_HEADS, t, HEAD_DIM))
    gl = jnp.broadcast_to(jnp.exp(g_last)[..., None], (_N_DIR, GDN_HEADS, nc, 1, HEAD_DIM))
    return lanes(be), lanes(be * e), lanes(e), lanes(jnp.exp(g_last - gam)), m1, dec, gl


def local_loss(x, wz, wb, ws, c, ctx, target):
    return channel_mixing(token_mixing(x, wz, wb, ws, c, ctx), wz, wb, ws, target)


def token_mixing(x, wz, wb, ws, c, ctx):
    n, cl = x.shape[0], ctx.shape[0]
    cos_q, sin_q, cos_k, sin_k, perm = _rope_tables(n, cl)

    sc_in = jnp.concatenate([jax.nn.silu(c), jax.nn.silu(ws["c_ctx"])[None, :], jnp.zeros((14, D_MODEL), F32)], axis=0)
    mod = pmm(sc_in, wb["w_mod"], wz["w_mod"], "mm_mod") + ws["b_mod"]
    sh1, sc1, g1, sh2, sc2, g2 = [mod[0:1, i * D_MODEL:(i + 1) * D_MODEL] for i in range(6)]
    csh1, csc1 = mod[1:2, 0:D_MODEL], mod[1:2, D_MODEL:2 * D_MODEL]

    def norm_mod(a, sh, sc):
        return (_rms(a) * (1.0 + sc) + sh,)

    (hx,) = rowop(norm_mod, "normmod_x", (x,), (sh1, sc1))
    (hc,) = rowop(norm_mod, "normmod_c", (ctx,), (csh1, csc1))
    h_all = jnp.concatenate([hc, hx], axis=0)
    p_main = pmm_t(h_all, wb["w_in_main"], wz["w_in_main"], "mm_in")
    p_small = pmm_t(h_all, wb["w_in_small"], wz["w_in_small"], "mm_ins")
    ak, av, qkv, aq, z, gate = jnp.split(p_main, [KV_W, 2 * KV_W, SMALL_AT, SMALL_AT + Q_W, SMALL_AT + Q_W + GDN_W],
                                         axis=1)
    db, da = p_small[:, :2 * GDN_HEADS], p_small[:, 2 * GDN_HEADS:4 * GDN_HEADS]

    def qk_prep(nh):
        def fn(a, w, cos, sin, pm):
            outs = []
            for ah in _heads(a, nh):
                y = _rms(ah) * w
                outs.append(y * cos + _pdot(y, pm) * sin)
            return (jnp.concatenate(outs, axis=1),)
        return fn

    (q_x,) = rowop(qk_prep(ATTN_HEADS), "q_prep", (aq[cl:],), (ws["q_norm_w"],), (cos_q, sin_q), (perm,))
    (k_all,) = rowop(qk_prep(ATTN_KV_HEADS), "k_prep", (ak,), (ws["k_norm_w"],), (cos_k, sin_k), (perm,))
    attn_x = attention(q_x, k_all, av)

    cw = ws["conv_qkv_w"]
    normed = jnp.asarray(np.repeat([1.0, 1.0, 0.0], GDN_W)[None, :], F32)
    scale = jnp.asarray(np.repeat([HEAD_DIM ** -0.5, 1.0, 1.0], GDN_W)[None, :], F32)

    def gdn_prep(a, w0, w1, w2, nf, sc):
        s = jax.nn.silu(_conv3(a, w0, w1, w2, (0, cl)))
        inv = lax.rsqrt(jnp.sum(s * s, axis=-1, keepdims=True) + NORM_EPS)
        return (s * jnp.where(nf > 0.0, inv * sc, 1.0),)

    (qkvn,) = colop(gdn_prep, "gdn_prep", (qkv, cw[0:1], cw[1:2], cw[2:3], normed, scale),
                    [(i, 0) for i in range(6)], 2, 3 * GDN_HEADS)
    beta = jax.nn.sigmoid(db).reshape(-1, 2, GDN_HEADS)
    log_a = -jnp.exp(ws["a_log"])[None] * jax.nn.softplus(da.reshape(-1, 2, GDN_HEADS) + ws["dt_bias"][None])
    o_fwd, o_rev = gdn_scan(cl // _C, qkvn, _gdn_factors(log_a, beta, cl // _C))
    o_x = o_fwd + o_rev

    def gdn_out(o, zz, w):
        outs = [_rms(oh) * w * jax.nn.silu(zh) for oh, zh in zip(_heads(o, GDN_HEADS), _heads(zz, GDN_HEADS))]
        return (jnp.concatenate(outs, axis=1),)

    (gdn_x,) = rowop(gdn_out, "gdn_out", (o_x, z[cl:]), (ws["gdn_norm_w"],))
    return dict(x=x, attn_x=attn_x, gdn_x=gdn_x, gate=gate[cl:], g1=g1, sh2=sh2, sc2=sc2, g2=g2)


def channel_mixing(mixed, wz, wb, ws, target):
    x, attn_x, gdn_x, gate = mixed["x"], mixed["attn_x"], mixed["gdn_x"], mixed["gate"]
    g1, sh2, sc2, g2 = mixed["g1"], mixed["sh2"], mixed["sc2"], mixed["g2"]
    pa = pmm(attn_x, wb["w_pa"], wz["w_pa"], "mm_pa")
    pd = pmm(gdn_x, wb["w_pd"], wz["w_pd"], "mm_pd")

    def merge(a, d, g):
        return (jax.nn.sigmoid(g[:, :D_MODEL]) * a + jax.nn.sigmoid(g[:, D_MODEL:]) * d,)

    (y,) = rowop(merge, "merge", (pa, pd, gate))
    mo = pmm(y, wb["w_out"], wz["w_out"], "mm_out")

    def res_norm_mod(xx, m, g, sh, sc):
        x1 = xx + g * m
        return x1, _rms(x1) * (1.0 + sc) + sh

    x1, h2 = rowop(res_norm_mod, "res1", (x, mo), (g1, sh2, sc2))
    up = pmm(h2, wb["w_up"], wz["w_up"], "mm_up")
    fw = ws["ffn_conv_w"]

    def ffn_act(ug, uv, w0g, w0v, w1g, w1v, w2g, w2v, bg, bv):
        g = _conv3(ug, w0g, w1g, w2g, (0,)) + bg
        v = _conv3(uv, w0v, w1v, w2v, (0,)) + bv
        return (jax.nn.silu(g) * v,)

    half = D_FF // HEAD_DIM
    (act,) = colop(ffn_act, "ffn_act", (up, fw[0:1], fw[1:2], fw[2:3], ws["ffn_conv_b"]),
                   [(i, off) for i in range(5) for off in (0, half)], 0, half)
    dn = pmm(act, wb["w_down"], wz["w_down"], "mm_down")

    def head(xx, m, g, w, tgt):
        yy = _rms(xx + g * m) * w
        err = (yy - tgt) ** 2
        return (jnp.broadcast_to(0.5 * jnp.mean(err, axis=-1, keepdims=True), (xx.shape[0], HEAD_DIM)),)

    (row_loss,) = rowop(head, "head", (x1, dn), (g2, ws["final_norm_w"][None, :]), (target,))
    return jnp.sum(row_loss[:, 0])


_HBM = pl.BlockSpec(memory_space=pltpu.HBM)


def _chip_peers():
    x, y = lax.axis_index("x"), lax.axis_index("y")
    return [(1 - x, y), (x, 1 - y), (1 - x, 1 - y)]


_SPLIT_COLS = ("w_in",)


def _half_of(view, nm, idx, lead=0):
    r, cdim = view.shape[-2:]
    pre = (slice(None),) * lead
    if nm in _SPLIT_COLS:
        return view.at[pre + (slice(None), pl.ds(pl.multiple_of(idx * (cdim // 2), 128), cdim // 2))]
    return view.at[pre + (pl.ds(pl.multiple_of(idx * (r // 2), 16), r // 2), slice(None))]


def _remote(src, dst, send_sem, recv_sem, dev):
    return pltpu.make_async_remote_copy(src_ref=src, dst_ref=dst, send_sem=send_sem, recv_sem=recv_sem,
                                        device_id=dev, device_id_type=MESH)


def _hbm_call(body, name, ins, out_shape, n_sems, in_place=False):
    names = tuple(ins)
    return dict(zip(names, pl.pallas_call(
        body, name=name, in_specs=[_HBM] * len(names), out_specs=[_HBM] * len(names),
        out_shape=[out_shape(nm, ins[nm]) for nm in names],
        scratch_shapes=[pltpu.SemaphoreType.DMA((k,)) for k in n_sems],
        input_output_aliases={i: i for i in range(len(names))} if in_place else {},
    )(*[ins[nm] for nm in names])))


def all_gather_chips(shards):
    names = tuple(shards)
    n = len(names)

    def body(*refs):
        ins, outs = dict(zip(names, refs[:n])), dict(zip(names, refs[n:2 * n]))
        ici_send, ici_recv, d2d_send, d2d_recv, own_send, own_recv = refs[2 * n:]
        x, y, c = lax.axis_index("x"), lax.axis_index("y"), lax.axis_index("c")
        me, sib = 2 * x + y, (x, y, 1 - c)
        own = [_remote(ins[nm], outs[nm].at[me], own_send.at[i], own_recv.at[i], sib) for i, nm in enumerate(names)]
        for cp in own:
            cp.start()
        sends = []
        for k, (px, py) in enumerate(_chip_peers()):
            for i, nm in enumerate(names):
                cp = _remote(_half_of(ins[nm], nm, c), _half_of(outs[nm].at[me], nm, c), ici_send.at[k * n + i],
                             ici_recv.at[k * n + i], (px, py, c))
                cp.start()
                sends.append(cp)
        for k, (px, py) in enumerate(_chip_peers()):
            for i, nm in enumerate(names):
                landed = _half_of(outs[nm].at[2 * px + py], nm, c)
                _remote(landed, landed, ici_send.at[k * n + i], ici_recv.at[k * n + i], (px, py, c)).wait_recv()
                fw = _remote(landed, landed, d2d_send.at[k * n + i], d2d_recv.at[k * n + i], sib)
                fw.start()
                sends.append(fw)
        for k, (px, py) in enumerate(_chip_peers()):
            for i, nm in enumerate(names):
                other = _half_of(outs[nm].at[2 * px + py], nm, 1 - c)
                _remote(other, other, d2d_send.at[k * n + i], d2d_recv.at[k * n + i], sib).wait_recv()
        for cp in sends:
            cp.wait_send()
        for cp in own:
            cp.wait()

    return _hbm_call(body, "ag_weights", shards, lambda nm, a: jax.ShapeDtypeStruct((N_CHIPS,) + a.shape, a.dtype),
                     (3 * n, 3 * n, 3 * n, 3 * n, n, n))


_SEM = pl.BlockSpec(memory_space=pltpu.SEMAPHORE)


def push_start(name, arrays, land_shapes, copies, n_copies):
    names = tuple(arrays)
    n = len(names)

    def body(*refs):
        send_sems, recv_sems, token = refs[2 * n], refs[2 * n + 1], refs[-1]
        for j, (src, dst, dev) in enumerate(copies(refs[:n], refs[n:2 * n])):
            _remote(src, dst, send_sems.at[j], recv_sems.at[j], dev).start()
        token[...] = jnp.zeros_like(token)

    hbm = lambda a: pltpu.with_memory_space_constraint(a, pltpu.HBM)
    lands = [lax.empty(land_shapes[nm], arrays[nm].dtype) for nm in names]
    res = pl.pallas_call(
        body, name=name,
        out_shape=(pltpu.SemaphoreType.DMA((n_copies,)), pltpu.SemaphoreType.DMA((n_copies,)),
                   *[pltpu.HBM(arrays[nm].shape, arrays[nm].dtype) for nm in names],
                   *[pltpu.HBM(a.shape, a.dtype) for a in lands], jax.ShapeDtypeStruct((8, 128), F32)),
        in_specs=[_HBM] * (2 * n),
        out_specs=(_SEM, _SEM, *[_HBM] * (2 * n), pl.BlockSpec(memory_space=pltpu.VMEM)),
        input_output_aliases={i: 2 + i for i in range(2 * n)},
        compiler_params=pltpu.CompilerParams(has_side_effects=pltpu.SideEffectType.DATAFLOW_SIDE_EFFECTING),
    )(*[hbm(arrays[nm]) for nm in names], *[hbm(a) for a in lands])
    return names, res[0], res[1], res[2:2 + n], res[2 + n:2 + 2 * n], res[-1]


def push_wait(name, started, copies, after):
    names, send_sems, recv_sems, srcs, lands, _ = started
    n = len(names)

    def body(*refs):
        send_ref, recv_ref = refs[2 * n], refs[2 * n + 1]
        for j, (src, dst, dev) in enumerate(copies(refs[:n], refs[n:2 * n])):
            cp = _remote(src, dst, send_ref.at[j], recv_ref.at[j], dev)
            cp.wait_send()
            cp.wait_recv()

    res = pl.pallas_call(
        body, name=name,
        out_shape=(*[pltpu.HBM(a.shape, a.dtype) for a in srcs], *[pltpu.HBM(a.shape, a.dtype) for a in lands]),
        in_specs=[_HBM] * (2 * n) + [_SEM, _SEM, pl.BlockSpec(memory_space=pl.ANY)],
        out_specs=tuple([_HBM] * (2 * n)),
        input_output_aliases={i: i for i in range(2 * n)},
        compiler_params=pltpu.CompilerParams(has_side_effects=pltpu.SideEffectType.DATAFLOW_SIDE_EFFECTING),
    )(*srcs, *lands, send_sems, recv_sems, after)
    return dict(zip(names, res[n:]))


def _gather_copies(srcs, lands):
    x, y, c = lax.axis_index("x"), lax.axis_index("y"), lax.axis_index("c")
    devs = [(px, py, c) for px, py in _chip_peers()] + [(x, y, 1 - c)]
    return [(src, land.at[2 * x + y], dev) for src, land in zip(srcs, lands) for dev in devs]


def _scatter_copies(srcs, lands):
    c = lax.axis_index("c")
    return [(src.at[2 * px + py], land.at[k], (px, py, c))
            for src, land in zip(srcs, lands) for k, (px, py) in enumerate(_chip_peers())]


def sibling_halves(blocks, name):
    names = tuple(blocks)

    def body(*refs):
        n = len(names)
        ins, outs = dict(zip(names, refs[:n])), dict(zip(names, refs[n:2 * n]))
        send_sems, recv_sems = refs[2 * n:]
        x, y, c = lax.axis_index("x"), lax.axis_index("y"), lax.axis_index("c")
        cps = [_remote(_half_of(ins[nm], nm, 1 - c, lead=1), outs[nm], send_sems.at[i], recv_sems.at[i], (x, y, 1 - c))
               for i, nm in enumerate(names)]
        for cp in cps:
            cp.start()
        for cp in cps:
            cp.wait()

    def half_shape(nm, a):
        r, cdim = a.shape[-2:]
        return jax.ShapeDtypeStruct((N_CHIPS, r, cdim // 2) if nm in _SPLIT_COLS else (N_CHIPS, r // 2, cdim), a.dtype)

    return _hbm_call(body, name, blocks, half_shape, (len(names), len(names)))


def scatter_halves(blocks):
    names = tuple(blocks)
    n = len(names)

    def body(*refs):
        ins, outs = dict(zip(names, refs[:n])), dict(zip(names, refs[n:2 * n]))
        send_sems, recv_sems = refs[2 * n:]
        c = lax.axis_index("c")
        cps = [_remote(ins[nm].at[2 * px + py], outs[nm].at[k], send_sems.at[k * n + i], recv_sems.at[k * n + i],
                       (px, py, c))
               for k, (px, py) in enumerate(_chip_peers()) for i, nm in enumerate(names)]
        for cp in cps:
            cp.start()
        for cp in cps:
            cp.wait_recv()
        for cp in cps:
            cp.wait_send()

    return _hbm_call(body, "rs_grads", blocks, lambda nm, a: jax.ShapeDtypeStruct((3,) + a.shape[1:], a.dtype),
                     (3 * n, 3 * n))


def sibling_assemble(arrays, name):
    names = tuple(arrays)

    def body(*refs):
        n = len(names)
        ins, outs = dict(zip(names, refs[:n])), dict(zip(names, refs[n:2 * n]))
        send_sems, recv_sems = refs[2 * n:]
        x, y, c = lax.axis_index("x"), lax.axis_index("y"), lax.axis_index("c")
        cps = [_remote(_half_of(ins[nm], nm, c), _half_of(outs[nm], nm, c), send_sems.at[i], recv_sems.at[i],
                       (x, y, 1 - c)) for i, nm in enumerate(names)]
        for cp in cps:
            cp.start()
        for i, nm in enumerate(names):
            other = _half_of(outs[nm], nm, 1 - c)
            _remote(other, other, send_sems.at[i], recv_sems.at[i], (x, y, 1 - c)).wait_recv()
        for cp in cps:
            cp.wait_send()

    return _hbm_call(body, name, arrays, lambda nm, a: jax.ShapeDtypeStruct(a.shape, a.dtype),
                     (len(names), len(names)), in_place=True)


def all_reduce_small(v):
    def body(v_ref, tot_ref, gath_ref, send_sems, recv_sems):
        x, y, c = lax.axis_index("x"), lax.axis_index("y"), lax.axis_index("c")
        me = 4 * x + 2 * y + c
        gath_ref[me] = v_ref[...]

        def peer(k):
            m = k + 1
            return (x ^ (m >> 2 & 1), y ^ (m >> 1 & 1), c ^ (m & 1))

        sends = [pltpu.make_async_remote_copy(src_ref=v_ref, dst_ref=gath_ref.at[me], send_sem=send_sems.at[k],
                                              recv_sem=recv_sems.at[k], device_id=peer(k), device_id_type=MESH)
                 for k in range(N_DEV - 1)]
        for cp in sends:
            cp.start()
        for k in range(N_DEV - 1):
            px, py, pc = peer(k)
            pltpu.make_async_remote_copy(src_ref=v_ref, dst_ref=gath_ref.at[4 * px + 2 * py + pc],
                                         send_sem=send_sems.at[k], recv_sem=recv_sems.at[k], device_id=peer(k),
                                         device_id_type=MESH).wait_recv()
        for cp in sends:
            cp.wait_send()
        acc = gath_ref[0]
        for d in range(1, N_DEV):
            acc = acc + gath_ref[d]
        tot_ref[...] = acc

    vm = pl.BlockSpec(memory_space=pltpu.VMEM)
    return pl.pallas_call(
        body, name="ar_small", in_specs=[vm], out_specs=[vm, vm],
        out_shape=(jax.ShapeDtypeStruct(v.shape, v.dtype), jax.ShapeDtypeStruct((N_DEV,) + v.shape, v.dtype)),
        scratch_shapes=[pltpu.SemaphoreType.DMA((N_DEV - 1,)), pltpu.SemaphoreType.DMA((N_DEV - 1,))],
    )(v)[0]


def _elementwise(fn, name, ins, n_out, out_dtype=F32):
    r, cdim = ins[0].shape
    tr = _pick(r, tuple(p for p in (488, 256, 128, 104, 64, 32, 16, 8) if p * cdim * 4 <= 2 * 1024 * 1024))
    spec = pl.BlockSpec((tr, cdim), lambda i: (i, 0))

    def body(*refs):
        res = fn(*[ref[...] for ref in refs[:len(ins)]])
        for o_ref, v in zip(refs[len(ins):], res):
            o_ref[...] = v

    return pl.pallas_call(
        body, name=name, grid=(r // tr,), in_specs=[spec] * len(ins), out_specs=[spec] * n_out,
        out_shape=tuple(jax.ShapeDtypeStruct((r, cdim), out_dtype) for _ in range(n_out)),
        compiler_params=pltpu.CompilerParams(dimension_semantics=("parallel",), vmem_limit_bytes=VMEM_LIMIT),
    )(*ins)


def _half_block_specs(nm, shard_shape):
    r, cdim = shard_shape
    if nm in _SPLIT_COLS:
        return (None, r, cdim // 2), (lambda j, c: (j, 0, c))
    return (None, r // 2, cdim), (lambda j, c: (j, c, 0))


def _presum(nm, sel, g32, a):
    blk, at = _half_block_specs(nm, g32.shape[1:])

    def body(s_ref, g_ref, a_ref, o_ref):
        del s_ref
        o_ref[...] = (g_ref[...] + a_ref[...]).astype(BF16)

    return pl.pallas_call(
        body, name="rs_presum_" + nm,
        grid_spec=pltpu.PrefetchScalarGridSpec(
            num_scalar_prefetch=1, grid=(N_CHIPS,),
            in_specs=[pl.BlockSpec(blk, lambda j, s: at(j, s[0])), pl.BlockSpec(blk, lambda j, s: (j, 0, 0))],
            out_specs=pl.BlockSpec(blk, lambda j, s: (j, 0, 0))),
        out_shape=jax.ShapeDtypeStruct(a.shape, BF16),
        compiler_params=pltpu.CompilerParams(dimension_semantics=("parallel",), vmem_limit_bytes=VMEM_LIMIT),
    )(sel, g32, a)


def _finalsum(nm, sel, g32, a, got):
    blk, at = _half_block_specs(nm, g32.shape[1:])

    def body(s_ref, g_ref, a_ref, r_ref, o_ref):
        del s_ref
        acc = g_ref[...] + a_ref[...]
        for k in range(3):
            acc = acc + r_ref[k].astype(F32)
        o_ref[...] = acc

    return pl.pallas_call(
        body, name="rs_final_" + nm,
        grid_spec=pltpu.PrefetchScalarGridSpec(
            num_scalar_prefetch=1, grid=(1,),
            in_specs=[pl.BlockSpec(blk, lambda i, s: at(s[1], s[0])), pl.BlockSpec(blk, lambda i, s: (s[1], 0, 0)),
                      pl.BlockSpec(got.shape, lambda i, s: (0, 0, 0))],
            out_specs=pl.BlockSpec(blk[1:], lambda i, s: at(0, s[0])[1:])),
        out_shape=jax.ShapeDtypeStruct(g32.shape[1:], F32),
        compiler_params=pltpu.CompilerParams(dimension_semantics=("arbitrary",), vmem_limit_bytes=VMEM_LIMIT),
    )(sel, g32, a, got)


def _adamw(w, g, m, v, name):
    shape = w.shape
    to2 = lambda a: a.reshape(-1, shape[-1])

    def fn(w_, g_, m_, v_):
        m_new = ADAM_B1 * m_ + (1.0 - ADAM_B1) * g_
        v_new = ADAM_B2 * v_ + (1.0 - ADAM_B2) * (g_ * g_)
        m_hat = m_new / (1.0 - ADAM_B1 ** ADAM_STEP)
        v_hat = v_new / (1.0 - ADAM_B2 ** ADAM_STEP)
        delta = -ADAM_LR * (m_hat / (jnp.sqrt(v_hat) + ADAM_EPS) + ADAM_WD * w_)
        return g_, delta, m_new, v_new

    outs = _elementwise(fn, name, [to2(a) for a in (w, g, m, v)], 4)
    return tuple(o.reshape(shape) for o in outs)


_BIG = ("w_mod", "w_in", "w_pa", "w_pd", "w_out", "w_up", "w_down")
_EARLY = ("w_mod", "w_in")
_LATE = ("w_pa", "w_pd", "w_out", "w_up", "w_down")
_COL_SHARDED = ("w_mod", "w_up")
_FULL_SHAPE = {"w_mod": (D_MODEL, MOD_W), "w_in": (IN_COLS, D_MODEL), "w_pa": (Q_W, D_MODEL), "w_pd": (GDN_W, D_MODEL),
               "w_out": (D_MODEL, D_MODEL), "w_up": (D_MODEL, 2 * D_FF), "w_down": (D_FF, D_MODEL)}


def _shard_shape(name):
    r, cdim = _FULL_SHAPE[name]
    return (r, cdim // N_CHIPS) if name in _COL_SHARDED else (r // N_CHIPS, cdim)


_CONV_ELEMS = 2 * (3 * CONV_W // N_CHIPS + 3 * 2 * D_FF // N_CHIPS)
_CONV_ROWS = 32


def _blocks_of_full(name, full):
    r, cdim = _FULL_SHAPE[name]
    return full if name in _COL_SHARDED else full.reshape(N_CHIPS, r // N_CHIPS, cdim)


def _full_of_blocks(name, blocks):
    return blocks if name in _COL_SHARDED else blocks.reshape(_FULL_SHAPE[name])


def _w_in_regroup(w_in_t):
    main = jnp.concatenate([w_in_t[:SMALL_AT], w_in_t[SMALL_AT + 4 * GDN_HEADS:]], axis=0)
    small = jnp.pad(w_in_t[SMALL_AT:SMALL_AT + 4 * GDN_HEADS], ((0, HEAD_DIM - 4 * GDN_HEADS), (0, 0)))
    return main, small


def _w_in_ungroup(main, small):
    return jnp.concatenate([main[:SMALL_AT], small[:4 * GDN_HEADS], main[SMALL_AT:]], axis=0)


_SMALL = ("c_ctx", "b_mod", "q_norm_w", "k_norm_w", "conv_qkv_w", "a_log", "dt_bias", "gdn_norm_w", "ffn_conv_w",
          "ffn_conv_b", "final_norm_w")


def _pack_small(tree, rows):
    flat = jnp.concatenate([tree[nm].reshape(-1) for nm in _SMALL])
    return jnp.pad(flat, (0, rows * 128 - flat.shape[0])).reshape(rows, 128)


def _unpack_small(packed, like):
    flat, out, off = packed.reshape(-1), {}, 0
    for nm in _SMALL:
        size = int(np.prod(like[nm].shape))
        out[nm] = flat[off:off + size].reshape(like[nm].shape)
        off += size
    return out


def kernel(x, c, ctx, c_ctx, w_mod, b_mod, w_in, q_norm_w, k_norm_w, conv_qkv_w, a_log, dt_bias, gdn_norm_w, w_pa, w_pd, w_out, w_up, ffn_conv_w, ffn_conv_b, w_down, final_norm_w, loss_target, m_c_ctx, m_w_mod, m_b_mod, m_w_in, m_q_norm_w, m_k_norm_w, m_conv_qkv_w, m_a_log, m_dt_bias, m_gdn_norm_w, m_w_pa, m_w_pd, m_w_out, m_w_up, m_ffn_conv_w, m_ffn_conv_b, m_w_down, m_final_norm_w, v_c_ctx, v_w_mod, v_b_mod, v_w_in, v_q_norm_w, v_k_norm_w, v_conv_qkv_w, v_a_log, v_dt_bias, v_gdn_norm_w, v_w_pa, v_w_pd, v_w_out, v_w_up, v_ffn_conv_w, v_ffn_conv_b, v_w_down, v_final_norm_w):
    names = ("c_ctx", "w_mod", "b_mod", "w_in", "q_norm_w", "k_norm_w", "conv_qkv_w", "a_log", "dt_bias", "gdn_norm_w",
             "w_pa", "w_pd", "w_out", "w_up", "ffn_conv_w", "ffn_conv_b", "w_down", "final_norm_w")
    w_sh = dict(c_ctx=c_ctx, w_mod=w_mod, b_mod=b_mod, w_in=w_in, q_norm_w=q_norm_w, k_norm_w=k_norm_w,
                conv_qkv_w=conv_qkv_w, a_log=a_log, dt_bias=dt_bias, gdn_norm_w=gdn_norm_w, w_pa=w_pa, w_pd=w_pd,
                w_out=w_out, w_up=w_up, ffn_conv_w=ffn_conv_w, ffn_conv_b=ffn_conv_b, w_down=w_down,
                final_norm_w=final_norm_w)
    m_sh = dict(c_ctx=m_c_ctx, w_mod=m_w_mod, b_mod=m_b_mod, w_in=m_w_in, q_norm_w=m_q_norm_w, k_norm_w=m_k_norm_w,
                conv_qkv_w=m_conv_qkv_w, a_log=m_a_log, dt_bias=m_dt_bias, gdn_norm_w=m_gdn_norm_w, w_pa=m_w_pa,
                w_pd=m_w_pd, w_out=m_w_out, w_up=m_w_up, ffn_conv_w=m_ffn_conv_w, ffn_conv_b=m_ffn_conv_b,
                w_down=m_w_down, final_norm_w=m_final_norm_w)
    v_sh = dict(c_ctx=v_c_ctx, w_mod=v_w_mod, b_mod=v_b_mod, w_in=v_w_in, q_norm_w=v_q_norm_w, k_norm_w=v_k_norm_w,
                conv_qkv_w=v_conv_qkv_w, a_log=v_a_log, dt_bias=v_dt_bias, gdn_norm_w=v_gdn_norm_w, w_pa=v_w_pa,
                w_pd=v_w_pd, w_out=v_w_out, w_up=v_w_up, ffn_conv_w=v_ffn_conv_w, ffn_conv_b=v_ffn_conv_b,
                w_down=v_w_down, final_norm_w=v_final_norm_w)
    chip = 2 * lax.axis_index("x") + lax.axis_index("y")

    conv_bits = jnp.concatenate([lax.bitcast_convert_type(w_sh[nm][0], BF16).reshape(-1)
                                 for nm in ("conv_qkv_w", "ffn_conv_w")])
    shards = {nm: w_sh[nm][0].astype(BF16).T if nm == "w_in" else w_sh[nm][0].astype(BF16) for nm in _BIG}
    shards["conv"] = jnp.pad(conv_bits, (0, _CONV_ROWS * D_MODEL - _CONV_ELEMS)).reshape(_CONV_ROWS, D_MODEL)
    gathered = all_gather_chips({nm: shards[nm] for nm in _EARLY + ("conv",)})
    gathered, late_shards = lax.optimization_barrier((gathered, {nm: shards[nm] for nm in _LATE}))
    started = push_start("ag_late_start", late_shards, {nm: (N_CHIPS,) + a.shape for nm, a in late_shards.items()},
                         _gather_copies, 4 * len(_LATE))
    c = c + started[-1][0:1, 0:1]

    wb = {nm: _full_of_blocks(nm, gathered[nm]) for nm in _EARLY}
    wb["w_in_main"], wb["w_in_small"] = _w_in_regroup(wb.pop("w_in"))
    conv_all = gathered["conv"].reshape(N_CHIPS, -1)[:, :_CONV_ELEMS]
    n_cq = 2 * 3 * CONV_W // N_CHIPS
    unbits = lambda a, w: lax.bitcast_convert_type(a.reshape(N_CHIPS, 3, w // N_CHIPS, 2), F32).transpose(1, 0, 2).reshape(3, w)
    ws = dict(c_ctx=c_ctx, b_mod=b_mod, q_norm_w=q_norm_w, k_norm_w=k_norm_w, a_log=a_log[0], dt_bias=dt_bias[0],
              gdn_norm_w=gdn_norm_w, ffn_conv_b=ffn_conv_b, final_norm_w=final_norm_w,
              conv_qkv_w=unbits(conv_all[:, :n_cq], CONV_W), ffn_conv_w=unbits(conv_all[:, n_cq:], 2 * D_FF))
    wz = {nm: jnp.zeros(a.shape, F32) for nm, a in wb.items()}
    wz.update({nm: jnp.zeros((N_CHIPS,) + _shard_shape(nm) if nm in _COL_SHARDED else _FULL_SHAPE[nm], F32)
               for nm in _LATE})

    mixed, vjp_mix = jax.vjp(lambda x_, wz_, ws_: token_mixing(x_, wz_, wb, ws_, c, ctx[0]), x[0], wz, ws)
    got = push_wait("ag_late_wait", started, _gather_copies, mixed["gdn_x"])
    wb_late = {nm: _full_of_blocks(nm, got[nm]) for nm in _LATE}
    loss_local, vjp_chan = jax.vjp(
        lambda mixed_, wz_, ws_: channel_mixing(mixed_, wz_, wb_late, ws_, loss_target[0]), mixed, wz, ws)
    d_mixed, gz_chan, gs_chan = vjp_chan(jnp.ones((), F32))

    sel = jnp.stack([lax.axis_index("c"), chip]).astype(jnp.int32)
    g32_late = {nm: _blocks_of_full(nm, gz_chan[nm]) for nm in _LATE}
    theirs_late = sibling_halves(g32_late, "rs_sibling_late")
    sums_late = {nm: _presum(nm, sel, g32_late[nm], theirs_late[nm]) for nm in _LATE}
    scattering = push_start("rs_late_start", sums_late, {nm: (3,) + a.shape[1:] for nm, a in sums_late.items()},
                            _scatter_copies, 3 * len(_LATE))
    d_mixed = {**d_mixed, "gdn_x": d_mixed["gdn_x"] + scattering[-1][0:1, 0:1]}
    gx, gz_mix, gs_mix = vjp_mix(d_mixed)
    gs = jax.tree.map(jnp.add, gs_mix, gs_chan)
    got_late = push_wait("rs_late_wait", scattering, _scatter_copies, gx)
    loss = lax.psum(loss_local, ("x", "y", "c"))

    gs["a_log"], gs["dt_bias"] = gs["a_log"][None], gs["dt_bias"][None]
    like = {nm: gs[nm] for nm in _SMALL}
    small_rows = -(-sum(int(np.prod(like[nm].shape)) for nm in _SMALL) // 1024) * 8
    small_sum = all_reduce_small(_pack_small(gs, small_rows))
    gz_mix["w_mod"], small_sum = lax.optimization_barrier((gz_mix["w_mod"], small_sum))
    g_small = _unpack_small(small_sum, like)
    for nm, width in (("conv_qkv_w", CONV_W), ("ffn_conv_w", 2 * D_FF)):
        g_small[nm] = lax.dynamic_slice_in_dim(g_small[nm], chip * (width // N_CHIPS), width // N_CHIPS, axis=1)[None]

    gz_mix["w_in"] = _w_in_ungroup(gz_mix.pop("w_in_main"), gz_mix.pop("w_in_small"))
    g32 = {nm: _blocks_of_full(nm, gz_mix[nm]) for nm in _EARLY}
    theirs = sibling_halves(g32, "rs_sibling")
    sums = {nm: _presum(nm, sel, g32[nm], theirs[nm]) for nm in _EARLY}
    scattering, zero = {}, None
    for nm in _EARLY[::-1]:
        src = sums[nm] if zero is None else sums[nm] + zero.astype(BF16)
        scattering[nm] = push_start("rs_early_start_" + nm, {nm: src}, {nm: (3,) + src.shape[1:]},
                                    _scatter_copies, 3)
        zero = scattering[nm][-1][0:1, 0:1]
    got_late = {nm: a + zero.astype(BF16) if nm in ("w_up", "w_down") else a for nm, a in got_late.items()}
    g_big = sibling_assemble({nm: _finalsum(nm, sel, g32_late[nm], theirs_late[nm], got_late[nm]) for nm in _LATE},
                             "rs_assemble_late")
    grads, deltas, new_m, new_v = {}, {}, {}, {}

    def adamw_big(nm):
        g = g_big[nm].T if nm == "w_in" else g_big[nm]
        grads[nm], deltas[nm], new_m[nm], new_v[nm] = (
            o[None] for o in _adamw(w_sh[nm][0], g, m_sh[nm][0], v_sh[nm][0], "adamw_" + nm))

    for nm in _LATE:
        adamw_big(nm)

    shard_like = {nm: w_sh[nm] for nm in _SMALL}
    rows_l = -(-sum(int(np.prod(shard_like[nm].shape)) for nm in _SMALL) // 1024) * 8
    g_l = _pack_small({nm: g_small[nm].reshape(w_sh[nm].shape) for nm in _SMALL}, rows_l) + zero
    outs = _adamw(_pack_small(w_sh, rows_l), g_l, _pack_small(m_sh, rows_l), _pack_small(v_sh, rows_l), "adamw_small")
    for tree, packed in zip((grads, deltas, new_m, new_v), outs):
        tree.update(_unpack_small(packed, shard_like))

    done_meanwhile = deltas["w_up"][0, :8, :HEAD_DIM] + outs[1][:8, :]
    for nm in _EARLY[::-1]:
        got = push_wait("rs_early_wait_" + nm, scattering[nm], _scatter_copies, done_meanwhile)
        g_big.update(sibling_assemble({nm: _finalsum(nm, sel, g32[nm], theirs[nm], got[nm])}, "rs_assemble_" + nm))
        adamw_big(nm)
        done_meanwhile = deltas[nm][0, :8, :HEAD_DIM]

    return (loss, gx[None], *[grads[nm] for nm in names], *[deltas[nm] for nm in names],
            *[new_m[nm] for nm in names], *[new_v[nm] for nm in names])
```

```python
import functools
import math

import jax
import jax.numpy as jnp
import numpy as np
from jax import lax
from jax.experimental import pallas as pl
from jax.experimental.pallas import tpu as pltpu

F32 = jnp.float32
BF16 = jnp.bfloat16
HIGHEST = lax.Precision.HIGHEST
MESH = pl.DeviceIdType.MESH

D_MODEL = 1024
GRID_W = 64
ATTN_HEADS = 8
ATTN_KV_HEADS = 2
ATTN_GROUP = ATTN_HEADS // ATTN_KV_HEADS
HEAD_DIM = 128
ROPE_THETA = 10000.0
GDN_HEADS = 8
GDN_CHUNK = 64
D_FF = 2816
NORM_EPS = 1e-6
KV_W = ATTN_KV_HEADS * HEAD_DIM
Q_W = ATTN_HEADS * HEAD_DIM
GDN_W = GDN_HEADS * HEAD_DIM
CONV_W = 3 * GDN_W
MOD_W = 6 * D_MODEL
IN_COLS = 2 * KV_W + CONV_W + 4 * GDN_HEADS + Q_W + GDN_W + 2 * D_MODEL
IN_MAIN = IN_COLS - 4 * GDN_HEADS
SMALL_AT = 2 * KV_W + CONV_W
N_CHIPS = 4
N_DEV = 8

ADAM_LR = 0.001
ADAM_B1 = 0.9
ADAM_B2 = 0.999
ADAM_EPS = 1e-08
ADAM_WD = 0.01
ADAM_STEP = 10

VMEM_LIMIT = 48 * 1024 * 1024
MATMUL_VMEM_BUDGET = 40 * 1024 * 1024
MATMUL_STEP_BYTES = 1200 * 1024


def _pick(dim, prefs):
    for p in prefs:
        if p <= dim and dim % p == 0:
            return p
    return dim


_DIMS = {
    "nn": (((1,), (0,)), ((), ())),
    "nt": (((1,), (1,)), ((), ())),
    "tn": (((0,), (0,)), ((), ())),
}


def _matmul_plan(m, n, k, a_bytes, b_bytes, n_unit=None, k_unit=None):
    best = None
    for tm in (2304, 2048, 1152, 1024, 768, 512, 384, 256, 128, m):
        for tn in (2560, 1536, 1408, 1024, 768, 512, 256, 128, n):
            for tk in (3840, 2816, 2560, 2304, 2048, 1920, 1408, 1152, 1024, 768, 512, 256, 128, k):
                if tm > m or tn > n or tk > k or m % tm or n % tn or k % tk:
                    continue
                if (n_unit and n_unit % tn) or (k_unit and k_unit % tk):
                    continue
                blocks = tm * tk * a_bytes + tk * tn * b_bytes + tm * tn * 4
                casts = (tm * tk * 2 if a_bytes > 2 else 0) + (tk * tn * 2 if b_bytes > 2 else 0) + tm * tn * 4
                if 2 * blocks + casts > MATMUL_VMEM_BUDGET:
                    continue
                nm, nn, nk = m // tm, n // tn, k // tk
                size_a, size_b = m * k * a_bytes, k * n * b_bytes
                for n_inner in (True, False):
                    if n_inner:
                        traffic = (size_a if nk == 1 else nn * size_a) + nm * size_b
                    else:
                        traffic = nn * size_a + (size_b if nk == 1 else nm * size_b)
                    cost = traffic + nm * nn * nk * MATMUL_STEP_BYTES + (nk - 1) * m * n * 4
                    if best is None or cost < best[0]:
                        best = (cost, tm, tn, tk, n_inner)
    return best[1:]


def _matmul(a, b, mode, name, blocked=None):
    b_rows, b_cols = (b.shape[-2], b.shape[-1] * (N_CHIPS if blocked == "b" else 1))
    if mode == "nn":
        (m, k), n = a.shape, b_cols
    elif mode == "nt":
        (m, k), n = a.shape, b_rows
    else:
        (k, m), n = a.shape, b_cols
    n_unit = n // N_CHIPS if blocked == "out" or (blocked == "b" and mode != "nt") else None
    k_unit = k // N_CHIPS if blocked == "b" and mode == "nt" else None
    tm, tn, tk, n_inner = _matmul_plan(m, n, k, a.dtype.itemsize, b.dtype.itemsize, n_unit, k_unit)
    nk = k // tk
    ij = (lambda g0, g1: (g0, g1)) if n_inner else (lambda g0, g1: (g1, g0))
    if mode == "tn":
        a_spec = pl.BlockSpec((tk, tm), lambda g0, g1, l: (l, ij(g0, g1)[0]))
    else:
        a_spec = pl.BlockSpec((tm, tk), lambda g0, g1, l: (ij(g0, g1)[0], l))
    if mode == "nt":
        b_blk, b_idx = (tn, tk), lambda g0, g1, l: (ij(g0, g1)[1], l)
    else:
        b_blk, b_idx = (tk, tn), lambda g0, g1, l: (l, ij(g0, g1)[1])

    def in_blocks(blk, idx):
        per = (n_unit or k_unit) // blk[1]
        return (None,) + blk, lambda g0, g1, l: (idx(g0, g1, l)[1] // per, idx(g0, g1, l)[0], idx(g0, g1, l)[1] % per)

    b_spec = pl.BlockSpec(*(in_blocks(b_blk, b_idx) if blocked == "b" else (b_blk, b_idx)))
    o_blk, o_idx = (tm, tn), lambda g0, g1, l: ij(g0, g1)
    o_spec = pl.BlockSpec(*(in_blocks(o_blk, o_idx) if blocked == "out" else (o_blk, o_idx)))
    o_shape = (N_CHIPS, m, n // N_CHIPS) if blocked == "out" else (m, n)
    dims = _DIMS[mode]

    def body(a_ref, b_ref, o_ref):
        part = lax.dot_general(a_ref[...].astype(BF16), b_ref[...].astype(BF16), dims, preferred_element_type=F32)
        if nk == 1:
            o_ref[...] = part
        else:
            l = pl.program_id(2)

            @pl.when(l == 0)
            def _():
                o_ref[...] = part

            @pl.when(l > 0)
            def _():
                o_ref[...] += part

    return pl.pallas_call(
        body,
        name=name,
        grid=(m // tm, n // tn, nk) if n_inner else (n // tn, m // tm, nk),
        in_specs=[a_spec, b_spec],
        out_specs=o_spec,
        out_shape=jax.ShapeDtypeStruct(o_shape, F32),
        compiler_params=pltpu.CompilerParams(dimension_semantics=("parallel", "parallel", "arbitrary"),
                                             vmem_limit_bytes=VMEM_LIMIT),
    )(a, b)


@functools.partial(jax.custom_vjp, nondiff_argnums=(3,))
def pmm(a, w, wz, name):
    del wz
    return _matmul(a, w, "nn", name + "_f", "b" if w.ndim == 3 else None)


def _pmm_fwd(a, w, wz, name):
    del wz
    return _matmul(a, w, "nn", name + "_f", "b" if w.ndim == 3 else None), (a, w)


def _pmm_bwd(name, res, g):
    a, w = res
    da = _matmul(g, w, "nt", name + "_da", "b" if w.ndim == 3 else None)
    dw_form = "out" if w.ndim == 3 else None
    if a.shape[0] < 128:
        pad = 128 - a.shape[0]
        at = jnp.pad(a.T, ((0, 0), (0, pad)))
        gp = jnp.pad(g, ((0, pad), (0, 0)))
        dw = _matmul(at, gp, "nn", name + "_dw", dw_form)
    else:
        dw = _matmul(a, g, "tn", name + "_dw", dw_form)
    return da, jnp.zeros_like(w), dw


pmm.defvjp(_pmm_fwd, _pmm_bwd)


@functools.partial(jax.custom_vjp, nondiff_argnums=(3,))
def pmm_t(a, wt, wtz, name):
    del wtz
    return _matmul(a, wt, "nt", name + "_f")


def _pmm_t_fwd(a, wt, wtz, name):
    del wtz
    return _matmul(a, wt, "nt", name + "_f"), (a, wt)


def _pmm_t_bwd(name, res, g):
    a, wt = res
    return _matmul(g, wt, "nn", name + "_da"), jnp.zeros_like(wt), _matmul(g, a, "tn", name + "_dw")


pmm_t.defvjp(_pmm_t_fwd, _pmm_t_bwd)


def rowop(fn, name, rows, bcs=(), crows=(), cbcs=(), tr=256):
    rows, bcs, crows, cbcs = tuple(rows), tuple(bcs), tuple(crows), tuple(cbcs)
    n_rows = rows[0].shape[0]
    tr = _pick(n_rows, (tr, 128, 64, 32, 16, 8))
    nr, nb, ncr, ncb = len(rows), len(bcs), len(crows), len(cbcs)
    n_in = nr + nb + ncr + ncb
    grid = (n_rows // tr,)

    def blk(arr):
        return jax.ShapeDtypeStruct((tr, arr.shape[1]), arr.dtype)

    def row_spec(arr):
        return pl.BlockSpec((tr, arr.shape[1]), lambda i: (i, 0))

    def bc_spec(arr):
        return pl.BlockSpec(arr.shape, lambda i: (0, 0))

    out_blk = jax.eval_shape(fn, *[blk(r) for r in rows], *bcs, *[blk(r) for r in crows], *cbcs)
    n_out = len(out_blk)
    out_shape = tuple(jax.ShapeDtypeStruct((n_rows, o.shape[1]), o.dtype) for o in out_blk)
    in_specs = ([row_spec(r) for r in rows] + [bc_spec(b) for b in bcs]
                + [row_spec(r) for r in crows] + [bc_spec(b) for b in cbcs])

    def order(vals):
        return vals

    def fwd_call(args):
        def body(*refs):
            vals = [r[...] for r in refs[:n_in]]
            res = fn(*order(vals))
            for o_ref, r in zip(refs[n_in:], res):
                o_ref[...] = r

        return pl.pallas_call(
            body, name=name + "_f", grid=grid, in_specs=in_specs,
            out_specs=[row_spec(o) for o in out_shape], out_shape=out_shape,
            compiler_params=pltpu.CompilerParams(dimension_semantics=("parallel",), vmem_limit_bytes=VMEM_LIMIT),
        )(*args)

    def bwd_call(args, cts):
        def body(*refs):
            vals = [r[...] for r in refs[:n_in]]
            ct_refs = refs[n_in:n_in + n_out]
            d_rows = refs[n_in + n_out:n_in + n_out + nr]
            d_bcs = refs[n_in + n_out + nr:]
            consts = vals[nr + nb:]
            _, vjp = jax.vjp(lambda *p: fn(*p, *consts), *vals[:nr + nb])
            grads = vjp(tuple(c[...] for c in ct_refs))
            for ref, g in zip(d_rows, grads[:nr]):
                ref[...] = g

            @pl.when(pl.program_id(0) == 0)
            def _():
                for ref in d_bcs:
                    ref[...] = jnp.zeros_like(ref)

            for ref, g in zip(d_bcs, grads[nr:]):
                ref[...] += g

        d_shape = tuple(jax.ShapeDtypeStruct(r.shape, r.dtype) for r in rows + bcs)
        return pl.pallas_call(
            body, name=name + "_b", grid=grid,
            in_specs=in_specs + [row_spec(o) for o in out_shape],
            out_specs=[row_spec(r) for r in rows] + [bc_spec(b) for b in bcs], out_shape=d_shape,
            compiler_params=pltpu.CompilerParams(dimension_semantics=("arbitrary",), vmem_limit_bytes=VMEM_LIMIT),
        )(*args, *cts)

    @jax.custom_vjp
    def op(diff, const):
        return fwd_call(diff + const)

    def op_fwd(diff, const):
        return fwd_call(diff + const), (diff, const)

    def op_bwd(res, cts):
        diff, const = res
        grads = bwd_call(diff + const, tuple(cts))
        return tuple(grads), tuple(jnp.zeros_like(c) for c in const)

    op.defvjp(op_fwd, op_bwd)
    return op(rows + bcs, crows + cbcs)


def colop(fn, name, arrays, uses, n_const, nblk, cw=128):
    arrays = tuple(arrays)
    n_diff = len(arrays) - n_const
    nd = sum(1 for u in uses if u[0] < n_diff)
    assert all(u[0] < n_diff for u in uses[:nd]) and all(u[0] >= n_diff for u in uses[nd:])

    def spec(u):
        return pl.BlockSpec((arrays[u[0]].shape[0], cw), lambda j, off=u[1]: (0, off + j))

    def out_spec(rows):
        return pl.BlockSpec((rows, cw), lambda j: (0, j))

    out_blk = jax.eval_shape(fn, *[jax.ShapeDtypeStruct((arrays[u[0]].shape[0], cw), arrays[u[0]].dtype)
                                   for u in uses])
    out_shape = tuple(jax.ShapeDtypeStruct((o.shape[0], nblk * cw), o.dtype) for o in out_blk)
    params = pltpu.CompilerParams(dimension_semantics=("parallel",), vmem_limit_bytes=VMEM_LIMIT)

    def fwd_call(arrs):
        def body(*refs):
            res = fn(*[r[...] for r in refs[:len(uses)]])
            for o_ref, r in zip(refs[len(uses):], res):
                o_ref[...] = r

        return pl.pallas_call(
            body, name=name + "_f", grid=(nblk,), in_specs=[spec(u) for u in uses],
            out_specs=[out_spec(o.shape[0]) for o in out_shape], out_shape=out_shape, compiler_params=params,
        )(*[arrs[u[0]] for u in uses])

    def bwd_call(arrs, cts):
        def body(*refs):
            vals = [r[...] for r in refs[:len(uses)]]
            ct_refs = refs[len(uses):len(uses) + len(out_shape)]
            _, vjp = jax.vjp(lambda *p: fn(*p, *vals[nd:]), *vals[:nd])
            for ref, g in zip(refs[len(uses) + len(out_shape):], vjp(tuple(c[...] for c in ct_refs))):
                ref[...] = g

        d_shape = tuple(jax.ShapeDtypeStruct((arrays[u[0]].shape[0], nblk * cw), F32) for u in uses[:nd])
        return pl.pallas_call(
            body, name=name + "_b", grid=(nblk,),
            in_specs=[spec(u) for u in uses] + [out_spec(o.shape[0]) for o in out_shape],
            out_specs=[out_spec(s.shape[0]) for s in d_shape], out_shape=d_shape, compiler_params=params,
        )(*[arrs[u[0]] for u in uses], *cts)

    @jax.custom_vjp
    def op(diff, const):
        return fwd_call(diff + const)

    def op_fwd(diff, const):
        return fwd_call(diff + const), (diff, const)

    def op_bwd(res, cts):
        diff, const = res
        d_uses = bwd_call(diff + const, tuple(cts))
        grads = []
        for i in range(n_diff):
            parts = sorted([(u[1], k) for k, u in enumerate(uses[:nd]) if u[0] == i])
            grads.append(d_uses[parts[0][1]] if len(parts) == 1
                         else jnp.concatenate([d_uses[k] for _, k in parts], axis=1))
        return tuple(grads), tuple(jnp.zeros_like(c) for c in const)

    op.defvjp(op_fwd, op_bwd)
    return op(arrays[:n_diff], arrays[n_diff:])


@functools.partial(jax.custom_vjp, nondiff_argnums=(1,))
def _roll_rows(x, k):
    return pltpu.roll(x, k % x.shape[0], 0)


def _roll_rows_fwd(x, k):
    return _roll_rows(x, k), None


def _roll_rows_bwd(k, _, g):
    return (_roll_rows(g, -k),)


_roll_rows.defvjp(_roll_rows_fwd, _roll_rows_bwd)


def _conv3(x, w0, w1, w2, starts):
    rows = lax.broadcasted_iota(jnp.int32, x.shape, 0)
    ends = tuple(s - 1 for s in starts[1:]) + (x.shape[0] - 1,)
    first = functools.reduce(jnp.logical_or, [rows == s for s in starts])
    last = functools.reduce(jnp.logical_or, [rows == e for e in ends])
    prev = jnp.where(first, 0.0, _roll_rows(x, 1))
    nxt = jnp.where(last, 0.0, _roll_rows(x, -1))
    return prev * w0 + x * w1 + nxt * w2


def _rms(x):
    return x * lax.rsqrt(jnp.mean(x * x, axis=-1, keepdims=True) + NORM_EPS)


def _heads(x, n):
    return [x[:, h * HEAD_DIM:(h + 1) * HEAD_DIM] for h in range(n)]


_NT = (((1,), (1,)), ((), ()))
_TN = (((0,), (0,)), ((), ()))
_TQ = 256


_N_SUB = 2


def _sub_rows(ref, i):
    rows = ref.shape[0] // _N_SUB
    return ref[i * rows:(i + 1) * rows, :].astype(BF16)


def _attn_probs(qs, k):
    s = _each(lambda q: lax.dot_general(q, k, _NT, preferred_element_type=F32) * (HEAD_DIM ** -0.5), qs)
    m = _each(lambda a: jnp.max(a, axis=-1, keepdims=True), s)
    e = _each(lambda a, b: jnp.exp(a - b), s, m)
    inv = _each(lambda a: 1.0 / jnp.sum(a, axis=-1, keepdims=True), e)
    return _each(lambda a, b: a * b, e, inv)


def _attn_fwd_call(q, k, v):
    n, t = q.shape[0], k.shape[0]
    tq = _pick(n, (_TQ, 128))

    def body(q_ref, k_ref, v_ref, o_ref):
        vb = v_ref[...].astype(BF16)
        ps = _attn_probs([_sub_rows(q_ref, i) for i in range(_N_SUB)], k_ref[...].astype(BF16))
        rows = tq // _N_SUB
        for i, p in enumerate(ps):
            o_ref[i * rows:(i + 1) * rows, :] = jnp.dot(p.astype(BF16), vb, preferred_element_type=F32)

    return pl.pallas_call(
        body, name="attn_f", grid=(ATTN_HEADS, n // tq),
        in_specs=[pl.BlockSpec((tq, HEAD_DIM), lambda h, i: (i, h)),
                  pl.BlockSpec((t, HEAD_DIM), lambda h, i: (0, h // ATTN_GROUP)),
                  pl.BlockSpec((t, HEAD_DIM), lambda h, i: (0, h // ATTN_GROUP))],
        out_specs=pl.BlockSpec((tq, HEAD_DIM), lambda h, i: (i, h)),
        out_shape=jax.ShapeDtypeStruct(q.shape, F32),
        compiler_params=pltpu.CompilerParams(dimension_semantics=("parallel", "parallel"),
                                             vmem_limit_bytes=VMEM_LIMIT),
    )(q, k, v)


def _attn_bwd_call(q, k, v, do):
    n, t = q.shape[0], k.shape[0]
    tq = _pick(n, (_TQ, 128))

    def body(q_ref, k_ref, v_ref, do_ref, dq_ref, dk_ref, dv_ref):
        @pl.when((pl.program_id(1) == 0) & (pl.program_id(2) == 0))
        def _():
            dk_ref[...] = jnp.zeros_like(dk_ref)
            dv_ref[...] = jnp.zeros_like(dv_ref)

        kb, vb = k_ref[...].astype(BF16), v_ref[...].astype(BF16)
        qs = [_sub_rows(q_ref, i) for i in range(_N_SUB)]
        dos = [_sub_rows(do_ref, i) for i in range(_N_SUB)]
        ps = _attn_probs(qs, kb)
        dps = _each(lambda d: lax.dot_general(d, vb, _NT, preferred_element_type=F32), dos)
        dss = _each(lambda p, dp: (p * (dp - jnp.sum(p * dp, axis=-1, keepdims=True)) * (HEAD_DIM ** -0.5)).astype(BF16),
                    ps, dps)
        rows = tq // _N_SUB
        for i, ds in enumerate(dss):
            dq_ref[i * rows:(i + 1) * rows, :] = jnp.dot(ds, kb, preferred_element_type=F32)
        dk_ref[...] += sum(_each(lambda ds, q: lax.dot_general(ds, q, _TN, preferred_element_type=F32), dss, qs))
        dv_ref[...] += sum(_each(lambda p, d: lax.dot_general(p.astype(BF16), d, _TN, preferred_element_type=F32),
                                 ps, dos))

    q_spec = pl.BlockSpec((tq, HEAD_DIM), lambda kh, g, i: (i, kh * ATTN_GROUP + g))
    kv_spec = pl.BlockSpec((t, HEAD_DIM), lambda kh, g, i: (0, kh))
    return pl.pallas_call(
        body, name="attn_b", grid=(ATTN_KV_HEADS, ATTN_GROUP, n // tq),
        in_specs=[q_spec, kv_spec, kv_spec, q_spec],
        out_specs=[q_spec, kv_spec, kv_spec],
        out_shape=(jax.ShapeDtypeStruct(q.shape, F32), jax.ShapeDtypeStruct(k.shape, F32),
                   jax.ShapeDtypeStruct(v.shape, F32)),
        compiler_params=pltpu.CompilerParams(dimension_semantics=("parallel", "arbitrary", "arbitrary"),
                                             vmem_limit_bytes=VMEM_LIMIT),
    )(q, k, v, do)


@jax.custom_vjp
def attention(q, k, v):
    return _attn_fwd_call(q, k, v)


def _attention_fwd(q, k, v):
    return _attn_fwd_call(q, k, v), (q, k, v)


def _attention_bwd(res, do):
    return _attn_bwd_call(*res, do)


attention.defvjp(_attention_fwd, _attention_bwd)


_C = GDN_CHUNK


def _pdot(a, b):
    return jnp.dot(a, b, precision=lax.Precision.HIGH, preferred_element_type=F32)


@jax.custom_vjp
def _hdot(a, b):
    return jnp.dot(a.astype(BF16), b.astype(BF16), preferred_element_type=F32)


def _hdot_fwd(a, b):
    return _hdot(a, b), (a, b)


def _hdot_bwd(res, g):
    a, b = res
    gb = g.astype(BF16)
    return (lax.dot_general(gb, b.astype(BF16), _NT, preferred_element_type=F32),
            lax.dot_general(a.astype(BF16), gb, _TN, preferred_element_type=F32))


_hdot.defvjp(_hdot_fwd, _hdot_bwd)


def _each(fn, *lists):
    return [fn(*args) for args in zip(*lists)]


@jax.custom_vjp
def _unit_lower_inverse(low, blockdiag):
    return _unit_lower_inverse_chain(low, blockdiag)


def _unit_lower_inverse_fwd(low, blockdiag):
    t_inv = _unit_lower_inverse_chain(low, blockdiag)
    return t_inv, (t_inv, blockdiag)


def _unit_lower_inverse_bwd(res, d_inv):
    t_inv, blockdiag = res
    bf = lambda a: a.astype(BF16)
    left = _each(lambda t, g: lax.dot_general(bf(t), bf(g), _TN, preferred_element_type=F32), t_inv, d_inv)
    d_low = _each(lambda a, t: -lax.dot_general(bf(a), bf(t), _NT, preferred_element_type=F32), left, t_inv)
    return d_low, jnp.zeros_like(blockdiag)


_unit_lower_inverse.defvjp(_unit_lower_inverse_fwd, _unit_lower_inverse_bwd)


def _unit_lower_inverse_chain(low, blockdiag):
    eye = (lax.broadcasted_iota(jnp.int32, (_C, _C), 0) == lax.broadcasted_iota(jnp.int32, (_C, _C), 1)).astype(F32)
    ld = _each(lambda a: a * blockdiag, low)
    lo = _each(lambda a, d: a - d, low, ld)
    l2 = _each(_hdot, ld, ld)
    l4 = _each(_hdot, l2, l2)
    l8 = _each(_hdot, l4, l4)
    td = _each(lambda d, a2: _hdot(eye - d, eye + a2), ld, l2)
    td = _each(lambda t, a4: _hdot(t, eye + a4), td, l4)
    td = _each(lambda t, a8: _hdot(t, eye + a8), td, l8)
    nn = _each(_hdot, td, lo)
    n2 = _each(_hdot, nn, nn)
    out = _each(lambda n, m2: _hdot(eye - n, eye + m2), nn, n2)
    return _each(_hdot, out, td)


@jax.custom_vjp
def _inverse_given(low, t_inv):
    del low
    return t_inv


_inverse_given.defvjp(lambda low, t_inv: (t_inv, (t_inv, low[0])),
                      lambda res, d_inv: (_unit_lower_inverse_bwd(res, d_inv)[0], [jnp.zeros_like(t) for t in res[0]]))


def _gdn_chunks(heads, blockdiag, kept_inverses=None):
    q, k, v, b_b, be_b, e_b, kd_b, m1, dec, gl, s = (list(col) for col in zip(*heads))
    f32dot = lambda a, b: jnp.dot(a, b, preferred_element_type=F32)
    nt = lambda a, b: lax.dot_general(a, b, _NT, preferred_element_type=F32)
    kq_k = _each(lambda kx, qq: nt(jnp.concatenate([kx, qq], axis=0), kx), k, q)
    low = _each(lambda m, a: m * a[:_C], m1, kq_k)
    t_inv = _unit_lower_inverse(low, blockdiag) if kept_inverses is None else _inverse_given(low, kept_inverses)
    uw = _each(lambda t, b, x, be, kx: _hdot(t, jnp.concatenate([b * x, be * kx], axis=1)), t_inv, b_b, v, be_b, k)
    wq_s = _each(lambda a, qq, e, ss: f32dot(jnp.concatenate([a[:, HEAD_DIM:], qq * e], axis=0), ss), uw, q, e_b, s)
    delta = _each(lambda a, ws: a[:, :HEAD_DIM] - ws[:_C], uw, wq_s)
    p = _each(lambda d, a: d * a[_C:], dec, kq_k)
    o = _each(lambda ws, pp, dd: ws[_C:] + f32dot(pp, dd), wq_s, p, delta)
    s_new = _each(lambda g, ss, kx, kd, dd: g * ss + lax.dot_general(kx * kd, dd, _TN, preferred_element_type=F32),
                  gl, s, k, kd_b, delta)
    return (o, s_new) if kept_inverses is not None else (o, s_new, t_inv)


def _blockdiag_mask():
    r = lax.broadcasted_iota(jnp.int32, (_C, _C), 0) >> 4
    c = lax.broadcasted_iota(jnp.int32, (_C, _C), 1) >> 4
    return (r == c).astype(F32)


_N_DIR = 2


def _scan_chunk(s, nc, ncc, reverse):
    return jnp.where(s < ncc, ncc - 1 - s, nc + ncc - 1 - s) if reverse else s


def _gdn_specs(nc, ncc, backward):
    step = (lambda s: nc - 1 - s) if backward else (lambda s: s)
    chunk = [lambda s, d=d: _scan_chunk(step(s), nc, ncc, d == 1) for d in range(_N_DIR)]
    tok = [pl.BlockSpec((_C, 3 * GDN_W), lambda s, d=d: (chunk[d](s), 0)) for d in range(_N_DIR)]
    park = [0, nc - ncc - 1]
    out = [pl.BlockSpec((_C, GDN_W), lambda s, d=d: (jnp.where(chunk[d](s) >= ncc, chunk[d](s) - ncc, park[d]), 0))
           for d in range(_N_DIR)]
    per_tok = pl.BlockSpec((_N_DIR, GDN_HEADS, _C, HEAD_DIM), lambda s: (0, 0, step(s), 0))
    mat = pl.BlockSpec((_N_DIR, GDN_HEADS, None, _C, _C), lambda s: (0, 0, step(s), 0, 0))
    row = pl.BlockSpec((_N_DIR, GDN_HEADS, None, 1, HEAD_DIM), lambda s: (0, 0, step(s), 0, 0))
    state = pl.BlockSpec((_N_DIR, GDN_HEADS, None, HEAD_DIM, HEAD_DIM), lambda s: (0, 0, step(s), 0, 0))
    return tok, out, per_tok, mat, row, state, chunk


def _head_cols(h, part):
    return slice((part * GDN_HEADS + h) * HEAD_DIM, (part * GDN_HEADS + h + 1) * HEAD_DIM)


def _gdn_heads(qkv_refs, factor_refs, state_of):
    return [[qkv_refs[d][:, _head_cols(h, 0)], qkv_refs[d][:, _head_cols(h, 1)], qkv_refs[d][:, _head_cols(h, 2)]]
            + [r[d, h] for r in factor_refs] + [state_of(d, h)]
            for d in range(_N_DIR) for h in range(GDN_HEADS)]


def _gdn_fwd_call(ncc, qkv, factors):
    t = qkv.shape[0]
    nc = t // _C
    tok, out, per_tok, mat, row, state, _ = _gdn_specs(nc, ncc, False)

    def body(*refs):
        qkv_refs, f_refs = refs[:_N_DIR], refs[_N_DIR:_N_DIR + 7]
        o_refs, sall_ref, tall_ref, s_ref = refs[_N_DIR + 7:2 * _N_DIR + 7], refs[2 * _N_DIR + 7], refs[-2], refs[-1]

        @pl.when(pl.program_id(0) == 0)
        def _():
            s_ref[...] = jnp.zeros_like(s_ref)

        heads = _gdn_heads(qkv_refs, f_refs, lambda d, h: s_ref[d, h])
        o, s_new, t_inv = _gdn_chunks(heads, _blockdiag_mask())
        for d in range(_N_DIR):
            for h in range(GDN_HEADS):
                i = GDN_HEADS * d + h
                sall_ref[d, h] = heads[i][10]
                tall_ref[d, h] = t_inv[i]
                o_refs[d][:, _head_cols(h, 0)] = o[i]
                s_ref[d, h] = s_new[i]

    o_shape = jax.ShapeDtypeStruct((t - ncc * _C, GDN_W), F32)
    s_shape = (_N_DIR, GDN_HEADS, nc, HEAD_DIM, HEAD_DIM)
    t_shape = (_N_DIR, GDN_HEADS, nc, _C, _C)
    return pl.pallas_call(
        body, name="gdn_f", grid=(nc,),
        in_specs=[*tok, per_tok, per_tok, per_tok, per_tok, mat, mat, row],
        out_specs=[*out, state, mat],
        out_shape=[o_shape, o_shape, jax.ShapeDtypeStruct(s_shape, F32), jax.ShapeDtypeStruct(t_shape, F32)],
        scratch_shapes=[pltpu.VMEM((_N_DIR, GDN_HEADS, HEAD_DIM, HEAD_DIM), F32)],
        compiler_params=pltpu.CompilerParams(dimension_semantics=("arbitrary",), vmem_limit_bytes=VMEM_LIMIT),
    )(qkv, qkv, *factors)


def _gdn_bwd_call(ncc, qkv, factors, sall, tall, dos):
    t = qkv.shape[0]
    nc = t // _C
    tok, out, per_tok, mat, row, state, chunk = _gdn_specs(nc, ncc, True)

    def body(*refs):
        qkv_refs, f_refs, sall_ref, tall_ref = refs[:_N_DIR], refs[_N_DIR:_N_DIR + 7], refs[_N_DIR + 7], refs[_N_DIR + 8]
        do_refs = refs[_N_DIR + 9:2 * _N_DIR + 9]
        dqkv_refs = refs[2 * _N_DIR + 9:3 * _N_DIR + 9]
        df_refs, ds_ref = refs[3 * _N_DIR + 9:3 * _N_DIR + 16], refs[-1]

        @pl.when(pl.program_id(0) == 0)
        def _():
            ds_ref[...] = jnp.zeros_like(ds_ref)

        bd = _blockdiag_mask()
        heads = _gdn_heads(qkv_refs, f_refs, lambda d, h: sall_ref[d, h])
        kept = [tall_ref[d, h] for d in range(_N_DIR) for h in range(GDN_HEADS)]
        _, vjp = jax.vjp(lambda hs: _gdn_chunks(hs, bd, kept), heads)
        live = [chunk[d](pl.program_id(0)) >= ncc for d in range(_N_DIR)]
        (all_grads,) = vjp(([jnp.where(live[d], do_refs[d][:, _head_cols(h, 0)], 0.0)
                             for d in range(_N_DIR) for h in range(GDN_HEADS)],
                            [ds_ref[d, h] for d in range(_N_DIR) for h in range(GDN_HEADS)]))
        for d in range(_N_DIR):
            for h in range(GDN_HEADS):
                grads = all_grads[GDN_HEADS * d + h]
                for part in range(3):
                    dqkv_refs[d][:, _head_cols(h, part)] = grads[part]
                for ref, g in zip(df_refs, grads[3:10]):
                    ref[d, h] = g
                ds_ref[d, h] = grads[10]

    shp = lambda a: jax.ShapeDtypeStruct(a.shape, F32)
    res = pl.pallas_call(
        body, name="gdn_b", grid=(nc,),
        in_specs=[*tok, per_tok, per_tok, per_tok, per_tok, mat, mat, row, state, mat, *out],
        out_specs=[*tok, per_tok, per_tok, per_tok, per_tok, mat, mat, row],
        out_shape=[shp(qkv), shp(qkv)] + [shp(a) for a in factors],
        scratch_shapes=[pltpu.VMEM((_N_DIR, GDN_HEADS, HEAD_DIM, HEAD_DIM), F32)],
        compiler_params=pltpu.CompilerParams(dimension_semantics=("arbitrary",), vmem_limit_bytes=VMEM_LIMIT),
    )(qkv, qkv, *factors, sall, tall, *dos)
    return res[0] + res[1], tuple(res[_N_DIR:])


@functools.partial(jax.custom_vjp, nondiff_argnums=(0,))
def gdn_scan(ncc, qkv, factors):
    o0, o1, _, _ = _gdn_fwd_call(ncc, qkv, factors)
    return o0, o1


def _gdn_scan_fwd(ncc, qkv, factors):
    o0, o1, sall, tall = _gdn_fwd_call(ncc, qkv, factors)
    return (o0, o1), (qkv, factors, sall, tall)


def _gdn_scan_bwd(ncc, res, dos):
    qkv, factors, sall, tall = res
    return _gdn_bwd_call(ncc, qkv, factors, sall, tall, list(dos))


gdn_scan.defvjp(_gdn_scan_fwd, _gdn_scan_bwd)


def _rope_tables(n, cl):
    t = np.arange(n)
    inv_freq = (ROPE_THETA ** (-np.arange(0, HEAD_DIM // 2, 2, dtype=np.float32) / (HEAD_DIM // 2))).astype(np.float32)
    ang_r = (t // GRID_W).astype(np.float32)[:, None] * inv_freq
    ang_c = (t % GRID_W).astype(np.float32)[:, None] * inv_freq
    cos = np.concatenate([np.cos(ang_r), np.cos(ang_r), np.cos(ang_c), np.cos(ang_c)], axis=1)
    sin = np.concatenate([-np.sin(ang_r), np.sin(ang_r), -np.sin(ang_c), np.sin(ang_c)], axis=1)
    cos_all = np.concatenate([np.ones((cl, HEAD_DIM), np.float32), cos], axis=0)
    sin_all = np.concatenate([np.zeros((cl, HEAD_DIM), np.float32), sin], axis=0)
    j = np.arange(HEAD_DIM)
    src = np.where((j % 64) < 32, j + 32, j - 32)
    perm = np.zeros((HEAD_DIM, HEAD_DIM), np.float32)
    perm[src, j] = 1.0
    return (jnp.asarray(cos.astype(np.float32)), jnp.asarray(sin.astype(np.float32)),
            jnp.asarray(cos_all), jnp.asarray(sin_all), jnp.asarray(perm))


def _gdn_factors(log_a, beta, ncc):
    t = log_a.shape[0]
    nc = t // _C
    la = log_a.reshape(nc, _C, _N_DIR, GDN_HEADS).transpose(2, 3, 0, 1)
    be = beta.reshape(nc, _C, _N_DIR, GDN_HEADS).transpose(2, 3, 0, 1)
    scan_order = lambda a: jnp.stack([a[0], jnp.concatenate([jnp.flip(a[1][:, :ncc], axis=1),
                                                              jnp.flip(a[1][:, ncc:], axis=1)], axis=1)])
    la, be = scan_order(la), scan_order(be)
    rev = jnp.asarray(np.array([False, True])[:, None, None, None])
    run = jnp.cumsum(la, axis=3)
    gam = jnp.where(rev, jnp.sum(la, axis=3, keepdims=True) - run + la, run)
    idx = np.arange(_C)
    incl = jnp.asarray(np.stack([idx[:, None] >= idx[None, :], idx[:, None] <= idx[None, :]])[:, None, None])
    strict = jnp.asarray(np.stack([idx[:, None] > idx[None, :], idx[:, None] < idx[None, :]])[:, None, None])
    dec = jnp.exp(jnp.where(incl, gam[..., :, None] - gam[..., None, :], -jnp.inf))
    m1 = jnp.where(strict, be[..., :, None] * dec, 0.0)
    e = jnp.exp(gam)
    g_last = jnp.where(rev, gam[..., :1], gam[..., -1:])
    lanes = lambda a: jnp.broadcast_to(a.reshape(_N_DIR, GDN_HEADS, t, 1), (_N_DIR, GDN_HEADS, t, HEAD_DIM))
    gl = jnp.broadcast_to(jnp.exp(g_last)[..., None], (_N_DIR, GDN_HEADS, nc, 1, HEAD_DIM))
    return lanes(be), lanes(be * e), lanes(e), lanes(jnp.exp(g_last - gam)), m1, dec, gl


def local_loss(x, wz, wb, ws, c, ctx, target):
    return channel_mixing(token_mixing(x, wz, wb, ws, c, ctx), wz, wb, ws, target)


def token_mixing(x, wz, wb, ws, c, ctx):
    n, cl = x.shape[0], ctx.shape[0]
    cos_q, sin_q, cos_k, sin_k, perm = _rope_tables(n, cl)

    sc_in = jnp.concatenate([jax.nn.silu(c), jax.nn.silu(ws["c_ctx"])[None, :], jnp.zeros((14, D_MODEL), F32)], axis=0)
    mod = pmm(sc_in, wb["w_mod"], wz["w_mod"], "mm_mod") + ws["b_mod"]
    sh1, sc1, g1, sh2, sc2, g2 = [mod[0:1, i * D_MODEL:(i + 1) * D_MODEL] for i in range(6)]
    csh1, csc1 = mod[1:2, 0:D_MODEL], mod[1:2, D_MODEL:2 * D_MODEL]

    def norm_mod(a, sh, sc):
        return (_rms(a) * (1.0 + sc) + sh,)

    (hx,) = rowop(norm_mod, "normmod_x", (x,), (sh1, sc1))
    (hc,) = rowop(norm_mod, "normmod_c", (ctx,), (csh1, csc1))
    h_all = jnp.concatenate([hc, hx], axis=0)
    p_main = pmm_t(h_all, wb["w_in_main"], wz["w_in_main"], "mm_in")
    p_small = pmm_t(h_all, wb["w_in_small"], wz["w_in_small"], "mm_ins")
    ak, av, qkv, aq, z, gate = jnp.split(p_main, [KV_W, 2 * KV_W, SMALL_AT, SMALL_AT + Q_W, SMALL_AT + Q_W + GDN_W],
                                         axis=1)
    db, da = p_small[:, :2 * GDN_HEADS], p_small[:, 2 * GDN_HEADS:4 * GDN_HEADS]

    def qk_prep(nh):
        def fn(a, w, cos, sin, pm):
            outs = []
            for ah in _heads(a, nh):
                y = _rms(ah) * w
                outs.append(y * cos + _pdot(y, pm) * sin)
            return (jnp.concatenate(outs, axis=1),)
        return fn

    (q_x,) = rowop(qk_prep(ATTN_HEADS), "q_prep", (aq[cl:],), (ws["q_norm_w"],), (cos_q, sin_q), (perm,))
    (k_all,) = rowop(qk_prep(ATTN_KV_HEADS), "k_prep", (ak,), (ws["k_norm_w"],), (cos_k, sin_k), (perm,))
    attn_x = attention(q_x, k_all, av)

    cw = ws["conv_qkv_w"]
    normed = jnp.asarray(np.repeat([1.0, 1.0, 0.0], GDN_W)[None, :], F32)
    scale = jnp.asarray(np.repeat([HEAD_DIM ** -0.5, 1.0, 1.0], GDN_W)[None, :], F32)

    def gdn_prep(a, w0, w1, w2, nf, sc):
        s = jax.nn.silu(_conv3(a, w0, w1, w2, (0, cl)))
        inv = lax.rsqrt(jnp.sum(s * s, axis=-1, keepdims=True) + NORM_EPS)
        return (s * jnp.where(nf > 0.0, inv * sc, 1.0),)

    (qkvn,) = colop(gdn_prep, "gdn_prep", (qkv, cw[0:1], cw[1:2], cw[2:3], normed, scale),
                    [(i, 0) for i in range(6)], 2, 3 * GDN_HEADS)
    beta = jax.nn.sigmoid(db).reshape(-1, 2, GDN_HEADS)
    log_a = -jnp.exp(ws["a_log"])[None] * jax.nn.softplus(da.reshape(-1, 2, GDN_HEADS) + ws["dt_bias"][None])
    o_fwd, o_rev = gdn_scan(cl // _C, qkvn, _gdn_factors(log_a, beta, cl // _C))
    o_x = o_fwd + o_rev

    def gdn_out(o, zz, w):
        outs = [_rms(oh) * w * jax.nn.silu(zh) for oh, zh in zip(_heads(o, GDN_HEADS), _heads(zz, GDN_HEADS))]
        return (jnp.concatenate(outs, axis=1),)

    (gdn_x,) = rowop(gdn_out, "gdn_out", (o_x, z[cl:]), (ws["gdn_norm_w"],))
    return dict(x=x, attn_x=attn_x, gdn_x=gdn_x, gate=gate[cl:], g1=g1, sh2=sh2, sc2=sc2, g2=g2)


def channel_mixing(mixed, wz, wb, ws, target):
    x, attn_x, gdn_x, gate = mixed["x"], mixed["attn_x"], mixed["gdn_x"], mixed["gate"]
    g1, sh2, sc2, g2 = mixed["g1"], mixed["sh2"], mixed["sc2"], mixed["g2"]
    pa = pmm(attn_x, wb["w_pa"], wz["w_pa"], "mm_pa")
    pd = pmm(gdn_x, wb["w_pd"], wz["w_pd"], "mm_pd")

    def merge(a, d, g):
        return (jax.nn.sigmoid(g[:, :D_MODEL]) * a + jax.nn.sigmoid(g[:, D_MODEL:]) * d,)

    (y,) = rowop(merge, "merge", (pa, pd, gate))
    mo = pmm(y, wb["w_out"], wz["w_out"], "mm_out")

    def res_norm_mod(xx, m, g, sh, sc):
        x1 = xx + g * m
        return x1, _rms(x1) * (1.0 + sc) + sh

    x1, h2 = rowop(res_norm_mod, "res1", (x, mo), (g1, sh2, sc2))
    up = pmm(h2, wb["w_up"], wz["w_up"], "mm_up")
    fw = ws["ffn_conv_w"]

    def ffn_act(ug, uv, w0g, w0v, w1g, w1v, w2g, w2v, bg, bv):
        g = _conv3(ug, w0g, w1g, w2g, (0,)) + bg
        v = _conv3(uv, w0v, w1v, w2v, (0,)) + bv
        return (jax.nn.silu(g) * v,)

    half = D_FF // HEAD_DIM
    (act,) = colop(ffn_act, "ffn_act", (up, fw[0:1], fw[1:2], fw[2:3], ws["ffn_conv_b"]),
                   [(i, off) for i in range(5) for off in (0, half)], 0, half)
    dn = pmm(act, wb["w_down"], wz["w_down"], "mm_down")

    def head(xx, m, g, w, tgt):
        yy = _rms(xx + g * m) * w
        err = (yy - tgt) ** 2
        return (jnp.broadcast_to(0.5 * jnp.mean(err, axis=-1, keepdims=True), (xx.shape[0], HEAD_DIM)),)

    (row_loss,) = rowop(head, "head", (x1, dn), (g2, ws["final_norm_w"][None, :]), (target,))
    return jnp.sum(row_loss[:, 0])


_HBM = pl.BlockSpec(memory_space=pltpu.HBM)


def _chip_peers():
    x, y = lax.axis_index("x"), lax.axis_index("y")
    return [(1 - x, y), (x, 1 - y), (1 - x, 1 - y)]


_SPLIT_COLS = ("w_in",)


def _half_of(view, nm, idx, lead=0):
    r, cdim = view.shape[-2:]
    pre = (slice(None),) * lead
    if nm in _SPLIT_COLS:
        return view.at[pre + (slice(None), pl.ds(pl.multiple_of(idx * (cdim // 2), 128), cdim // 2))]
    return view.at[pre + (pl.ds(pl.multiple_of(idx * (r // 2), 16), r // 2), slice(None))]


def _remote(src, dst, send_sem, recv_sem, dev):
    return pltpu.make_async_remote_copy(src_ref=src, dst_ref=dst, send_sem=send_sem, recv_sem=recv_sem,
                                        device_id=dev, device_id_type=MESH)


def _hbm_call(body, name, ins, out_shape, n_sems, in_place=False):
    names = tuple(ins)
    return dict(zip(names, pl.pallas_call(
        body, name=name, in_specs=[_HBM] * len(names), out_specs=[_HBM] * len(names),
        out_shape=[out_shape(nm, ins[nm]) for nm in names],
        scratch_shapes=[pltpu.SemaphoreType.DMA((k,)) for k in n_sems],
        input_output_aliases={i: i for i in range(len(names))} if in_place else {},
    )(*[ins[nm] for nm in names])))


def all_gather_chips(shards):
    names = tuple(shards)
    n = len(names)

    def body(*refs):
        ins, outs = dict(zip(names, refs[:n])), dict(zip(names, refs[n:2 * n]))
        ici_send, ici_recv, d2d_send, d2d_recv, own_send, own_recv = refs[2 * n:]
        x, y, c = lax.axis_index("x"), lax.axis_index("y"), lax.axis_index("c")
        me, sib = 2 * x + y, (x, y, 1 - c)
        own = [_remote(ins[nm], outs[nm].at[me], own_send.at[i], own_recv.at[i], sib) for i, nm in enumerate(names)]
        for cp in own:
            cp.start()
        sends = []
        for k, (px, py) in enumerate(_chip_peers()):
            for i, nm in enumerate(names):
                cp = _remote(_half_of(ins[nm], nm, c), _half_of(outs[nm].at[me], nm, c), ici_send.at[k * n + i],
                             ici_recv.at[k * n + i], (px, py, c))
                cp.start()
                sends.append(cp)
        for k, (px, py) in enumerate(_chip_peers()):
            for i, nm in enumerate(names):
                landed = _half_of(outs[nm].at[2 * px + py], nm, c)
                _remote(landed, landed, ici_send.at[k * n + i], ici_recv.at[k * n + i], (px, py, c)).wait_recv()
                fw = _remote(landed, landed, d2d_send.at[k * n + i], d2d_recv.at[k * n + i], sib)
                fw.start()
                sends.append(fw)
        for k, (px, py) in enumerate(_chip_peers()):
            for i, nm in enumerate(names):
                other = _half_of(outs[nm].at[2 * px + py], nm, 1 - c)
                _remote(other, other, d2d_send.at[k * n + i], d2d_recv.at[k * n + i], sib).wait_recv()
        for cp in sends:
            cp.wait_send()
        for cp in own:
            cp.wait()

    return _hbm_call(body, "ag_weights", shards, lambda nm, a: jax.ShapeDtypeStruct((N_CHIPS,) + a.shape, a.dtype),
                     (3 * n, 3 * n, 3 * n, 3 * n, n, n))


_SEM = pl.BlockSpec(memory_space=pltpu.SEMAPHORE)


def push_start(name, arrays, land_shapes, copies, n_copies):
    names = tuple(arrays)
    n = len(names)

    def body(*refs):
        send_sems, recv_sems, token = refs[2 * n], refs[2 * n + 1], refs[-1]
        for j, (src, dst, dev) in enumerate(copies(refs[:n], refs[n:2 * n])):
            _remote(src, dst, send_sems.at[j], recv_sems.at[j], dev).start()
        token[...] = jnp.zeros_like(token)

    hbm = lambda a: pltpu.with_memory_space_constraint(a, pltpu.HBM)
    lands = [lax.empty(land_shapes[nm], arrays[nm].dtype) for nm in names]
    res = pl.pallas_call(
        body, name=name,
        out_shape=(pltpu.SemaphoreType.DMA((n_copies,)), pltpu.SemaphoreType.DMA((n_copies,)),
                   *[pltpu.HBM(arrays[nm].shape, arrays[nm].dtype) for nm in names],
                   *[pltpu.HBM(a.shape, a.dtype) for a in lands], jax.ShapeDtypeStruct((8, 128), F32)),
        in_specs=[_HBM] * (2 * n),
        out_specs=(_SEM, _SEM, *[_HBM] * (2 * n), pl.BlockSpec(memory_space=pltpu.VMEM)),
        input_output_aliases={i: 2 + i for i in range(2 * n)},
        compiler_params=pltpu.CompilerParams(has_side_effects=pltpu.SideEffectType.DATAFLOW_SIDE_EFFECTING),
    )(*[hbm(arrays[nm]) for nm in names], *[hbm(a) for a in lands])
    return names, res[0], res[1], res[2:2 + n], res[2 + n:2 + 2 * n], res[-1]


def push_wait(name, started, copies, after):
    names, send_sems, recv_sems, srcs, lands, _ = started
    n = len(names)

    def body(*refs):
        send_ref, recv_ref = refs[2 * n], refs[2 * n + 1]
        for j, (src, dst, dev) in enumerate(copies(refs[:n], refs[n:2 * n])):
            cp = _remote(src, dst, send_ref.at[j], recv_ref.at[j], dev)
            cp.wait_send()
            cp.wait_recv()

    res = pl.pallas_call(
        body, name=name,
        out_shape=(*[pltpu.HBM(a.shape, a.dtype) for a in srcs], *[pltpu.HBM(a.shape, a.dtype) for a in lands]),
        in_specs=[_HBM] * (2 * n) + [_SEM, _SEM, pl.BlockSpec(memory_space=pl.ANY)],
        out_specs=tuple([_HBM] * (2 * n)),
        input_output_aliases={i: i for i in range(2 * n)},
        compiler_params=pltpu.CompilerParams(has_side_effects=pltpu.SideEffectType.DATAFLOW_SIDE_EFFECTING),
    )(*srcs, *lands, send_sems, recv_sems, after)
    return dict(zip(names, res[n:]))


def _gather_copies(srcs, lands):
    x, y, c = lax.axis_index("x"), lax.axis_index("y"), lax.axis_index("c")
    devs = [(px, py, c) for px, py in _chip_peers()] + [(x, y, 1 - c)]
    return [(src, land.at[2 * x + y], dev) for src, land in zip(srcs, lands) for dev in devs]


def _scatter_copies(srcs, lands):
    c = lax.axis_index("c")
    return [(src.at[2 * px + py], land.at[k], (px, py, c))
            for src, land in zip(srcs, lands) for k, (px, py) in enumerate(_chip_peers())]


def sibling_halves(blocks, name):
    names = tuple(blocks)

    def body(*refs):
        n = len(names)
        ins, outs = dict(zip(names, refs[:n])), dict(zip(names, refs[n:2 * n]))
        send_sems, recv_sems = refs[2 * n:]
        x, y, c = lax.axis_index("x"), lax.axis_index("y"), lax.axis_index("c")
        cps = [_remote(_half_of(ins[nm], nm, 1 - c, lead=1), outs[nm], send_sems.at[i], recv_sems.at[i], (x, y, 1 - c))
               for i, nm in enumerate(names)]
        for cp in cps:
            cp.start()
        for cp in cps:
            cp.wait()

    def half_shape(nm, a):
        r, cdim = a.shape[-2:]
        return jax.ShapeDtypeStruct((N_CHIPS, r, cdim // 2) if nm in _SPLIT_COLS else (N_CHIPS, r // 2, cdim), a.dtype)

    return _hbm_call(body, name, blocks, half_shape, (len(names), len(names)))


def scatter_halves(blocks):
    names = tuple(blocks)
    n = len(names)

    def body(*refs):
        ins, outs = dict(zip(names, refs[:n])), dict(zip(names, refs[n:2 * n]))
        send_sems, recv_sems = refs[2 * n:]
        c = lax.axis_index("c")
        cps = [_remote(ins[nm].at[2 * px + py], outs[nm].at[k], send_sems.at[k * n + i], recv_sems.at[k * n + i],
                       (px, py, c))
               for k, (px, py) in enumerate(_chip_peers()) for i, nm in enumerate(names)]
        for cp in cps:
            cp.start()
        for cp in cps:
            cp.wait_recv()
        for cp in cps:
            cp.wait_send()

    return _hbm_call(body, "rs_grads", blocks, lambda nm, a: jax.ShapeDtypeStruct((3,) + a.shape[1:], a.dtype),
                     (3 * n, 3 * n))


def sibling_assemble(arrays, name):
    names = tuple(arrays)

    def body(*refs):
        n = len(names)
        ins, outs = dict(zip(names, refs[:n])), dict(zip(names, refs[n:2 * n]))
        send_sems, recv_sems = refs[2 * n:]
        x, y, c = lax.axis_index("x"), lax.axis_index("y"), lax.axis_index("c")
        cps = [_remote(_half_of(ins[nm], nm, c), _half_of(outs[nm], nm, c), send_sems.at[i], recv_sems.at[i],
                       (x, y, 1 - c)) for i, nm in enumerate(names)]
        for cp in cps:
            cp.start()
        for i, nm in enumerate(names):
            other = _half_of(outs[nm], nm, 1 - c)
            _remote(other, other, send_sems.at[i], recv_sems.at[i], (x, y, 1 - c)).wait_recv()
        for cp in cps:
            cp.wait_send()

    return _hbm_call(body, name, arrays, lambda nm, a: jax.ShapeDtypeStruct(a.shape, a.dtype),
                     (len(names), len(names)), in_place=True)


def all_reduce_small(v):
    def body(v_ref, tot_ref, gath_ref, send_sems, recv_sems):
        x, y, c = lax.axis_index("x"), lax.axis_index("y"), lax.axis_index("c")
        me = 4 * x + 2 * y + c
        gath_ref[me] = v_ref[...]

        def peer(k):
            m = k + 1
            return (x ^ (m >> 2 & 1), y ^ (m >> 1 & 1), c ^ (m & 1))

        sends = [pltpu.make_async_remote_copy(src_ref=v_ref, dst_ref=gath_ref.at[me], send_sem=send_sems.at[k],
                                              recv_sem=recv_sems.at[k], device_id=peer(k), device_id_type=MESH)
                 for k in range(N_DEV - 1)]
        for cp in sends:
            cp.start()
        for k in range(N_DEV - 1):
            px, py, pc = peer(k)
            pltpu.make_async_remote_copy(src_ref=v_ref, dst_ref=gath_ref.at[4 * px + 2 * py + pc],
                                         send_sem=send_sems.at[k], recv_sem=recv_sems.at[k], device_id=peer(k),
                                         device_id_type=MESH).wait_recv()
        for cp in sends:
            cp.wait_send()
        acc = gath_ref[0]
        for d in range(1, N_DEV):
            acc = acc + gath_ref[d]
        tot_ref[...] = acc

    vm = pl.BlockSpec(memory_space=pltpu.VMEM)
    return pl.pallas_call(
        body, name="ar_small", in_specs=[vm], out_specs=[vm, vm],
        out_shape=(jax.ShapeDtypeStruct(v.shape, v.dtype), jax.ShapeDtypeStruct((N_DEV,) + v.shape, v.dtype)),
        scratch_shapes=[pltpu.SemaphoreType.DMA((N_DEV - 1,)), pltpu.SemaphoreType.DMA((N_DEV - 1,))],
    )(v)[0]


def _elementwise(fn, name, ins, n_out, out_dtype=F32):
    r, cdim = ins[0].shape
    tr = _pick(r, tuple(p for p in (488, 256, 128, 104, 64, 32, 16, 8) if p * cdim * 4 <= 2 * 1024 * 1024))
    spec, steps = pl.BlockSpec((tr, cdim), lambda i: (i, 0)), r // tr
    if tr < 16 and r > 1024 and cdim % 256 == 0:
        spec, steps = pl.BlockSpec((r, 256), lambda i: (0, i)), cdim // 256

    def body(*refs):
        res = fn(*[ref[...] for ref in refs[:len(ins)]])
        for o_ref, v in zip(refs[len(ins):], res):
            o_ref[...] = v

    return pl.pallas_call(
        body, name=name, grid=(steps,), in_specs=[spec] * len(ins), out_specs=[spec] * n_out,
        out_shape=tuple(jax.ShapeDtypeStruct((r, cdim), out_dtype) for _ in range(n_out)),
        compiler_params=pltpu.CompilerParams(dimension_semantics=("parallel",), vmem_limit_bytes=VMEM_LIMIT),
    )(*ins)


def _half_block_specs(nm, shard_shape):
    r, cdim = shard_shape
    if nm in _SPLIT_COLS:
        return (None, r, cdim // 2), (lambda j, c: (j, 0, c))
    return (None, r // 2, cdim), (lambda j, c: (j, c, 0))


def _presum(nm, sel, g32, a):
    blk, at = _half_block_specs(nm, g32.shape[1:])

    def body(s_ref, g_ref, a_ref, o_ref):
        del s_ref
        o_ref[...] = (g_ref[...] + a_ref[...]).astype(BF16)

    return pl.pallas_call(
        body, name="rs_presum_" + nm,
        grid_spec=pltpu.PrefetchScalarGridSpec(
            num_scalar_prefetch=1, grid=(N_CHIPS,),
            in_specs=[pl.BlockSpec(blk, lambda j, s: at(j, s[0])), pl.BlockSpec(blk, lambda j, s: (j, 0, 0))],
            out_specs=pl.BlockSpec(blk, lambda j, s: (j, 0, 0))),
        out_shape=jax.ShapeDtypeStruct(a.shape, BF16),
        compiler_params=pltpu.CompilerParams(dimension_semantics=("parallel",), vmem_limit_bytes=VMEM_LIMIT),
    )(sel, g32, a)


def _finalsum(nm, sel, g32, a, got):
    blk, at = _half_block_specs(nm, g32.shape[1:])

    def body(s_ref, g_ref, a_ref, r_ref, o_ref):
        del s_ref
        acc = g_ref[...] + a_ref[...]
        for k in range(3):
            acc = acc + r_ref[k].astype(F32)
        o_ref[...] = acc

    return pl.pallas_call(
        body, name="rs_final_" + nm,
        grid_spec=pltpu.PrefetchScalarGridSpec(
            num_scalar_prefetch=1, grid=(1,),
            in_specs=[pl.BlockSpec(blk, lambda i, s: at(s[1], s[0])), pl.BlockSpec(blk, lambda i, s: (s[1], 0, 0)),
                      pl.BlockSpec(got.shape, lambda i, s: (0, 0, 0))],
            out_specs=pl.BlockSpec(blk[1:], lambda i, s: at(0, s[0])[1:])),
        out_shape=jax.ShapeDtypeStruct(g32.shape[1:], F32),
        compiler_params=pltpu.CompilerParams(dimension_semantics=("arbitrary",), vmem_limit_bytes=VMEM_LIMIT),
    )(sel, g32, a, got)


def _adamw(w, g, m, v, name):
    shape = w.shape
    to2 = lambda a: a.reshape(-1, shape[-1])

    def fn(w_, g_, m_, v_):
        m_new = ADAM_B1 * m_ + (1.0 - ADAM_B1) * g_
        v_new = ADAM_B2 * v_ + (1.0 - ADAM_B2) * (g_ * g_)
        m_hat = m_new / (1.0 - ADAM_B1 ** ADAM_STEP)
        v_hat = v_new / (1.0 - ADAM_B2 ** ADAM_STEP)
        delta = -ADAM_LR * (m_hat / (jnp.sqrt(v_hat) + ADAM_EPS) + ADAM_WD * w_)
        return g_, delta, m_new, v_new

    outs = _elementwise(fn, name, [to2(a) for a in (w, g, m, v)], 4)
    return tuple(o.reshape(shape) for o in outs)


_BIG = ("w_mod", "w_in", "w_pa", "w_pd", "w_out", "w_up", "w_down")
_EARLY = ("w_mod", "w_in")
_LATE = ("w_pa", "w_pd", "w_out", "w_up", "w_down")
_COL_SHARDED = ("w_mod", "w_up")
_FULL_SHAPE = {"w_mod": (D_MODEL, MOD_W), "w_in": (IN_COLS, D_MODEL), "w_pa": (Q_W, D_MODEL), "w_pd": (GDN_W, D_MODEL),
               "w_out": (D_MODEL, D_MODEL), "w_up": (D_MODEL, 2 * D_FF), "w_down": (D_FF, D_MODEL)}


def _shard_shape(name):
    r, cdim = _FULL_SHAPE[name]
    return (r, cdim // N_CHIPS) if name in _COL_SHARDED else (r // N_CHIPS, cdim)


_CONV_ELEMS = 2 * (3 * CONV_W // N_CHIPS + 3 * 2 * D_FF // N_CHIPS)
_CONV_ROWS = 32


def _blocks_of_full(name, full):
    r, cdim = _FULL_SHAPE[name]
    return full if name in _COL_SHARDED else full.reshape(N_CHIPS, r // N_CHIPS, cdim)


def _full_of_blocks(name, blocks):
    return blocks if name in _COL_SHARDED else blocks.reshape(_FULL_SHAPE[name])


def _w_in_regroup(w_in_t):
    main = jnp.concatenate([w_in_t[:SMALL_AT], w_in_t[SMALL_AT + 4 * GDN_HEADS:]], axis=0)
    small = jnp.pad(w_in_t[SMALL_AT:SMALL_AT + 4 * GDN_HEADS], ((0, HEAD_DIM - 4 * GDN_HEADS), (0, 0)))
    return main, small


def _w_in_ungroup(main, small):
    return jnp.concatenate([main[:SMALL_AT], small[:4 * GDN_HEADS], main[SMALL_AT:]], axis=0)


_SMALL = ("c_ctx", "b_mod", "q_norm_w", "k_norm_w", "conv_qkv_w", "a_log", "dt_bias", "gdn_norm_w", "ffn_conv_w",
          "ffn_conv_b", "final_norm_w")


def _pack_small(tree, rows):
    flat = jnp.concatenate([tree[nm].reshape(-1) for nm in _SMALL])
    return jnp.pad(flat, (0, rows * 128 - flat.shape[0])).reshape(rows, 128)


def _unpack_small(packed, like):
    flat, out, off = packed.reshape(-1), {}, 0
    for nm in _SMALL:
        size = int(np.prod(like[nm].shape))
        out[nm] = flat[off:off + size].reshape(like[nm].shape)
        off += size
    return out


def kernel(x, c, ctx, c_ctx, w_mod, b_mod, w_in, q_norm_w, k_norm_w, conv_qkv_w, a_log, dt_bias, gdn_norm_w, w_pa, w_pd, w_out, w_up, ffn_conv_w, ffn_conv_b, w_down, final_norm_w, loss_target, m_c_ctx, m_w_mod, m_b_mod, m_w_in, m_q_norm_w, m_k_norm_w, m_conv_qkv_w, m_a_log, m_dt_bias, m_gdn_norm_w, m_w_pa, m_w_pd, m_w_out, m_w_up, m_ffn_conv_w, m_ffn_conv_b, m_w_down, m_final_norm_w, v_c_ctx, v_w_mod, v_b_mod, v_w_in, v_q_norm_w, v_k_norm_w, v_conv_qkv_w, v_a_log, v_dt_bias, v_gdn_norm_w, v_w_pa, v_w_pd, v_w_out, v_w_up, v_ffn_conv_w, v_ffn_conv_b, v_w_down, v_final_norm_w):
    names = ("c_ctx", "w_mod", "b_mod", "w_in", "q_norm_w", "k_norm_w", "conv_qkv_w", "a_log", "dt_bias", "gdn_norm_w",
             "w_pa", "w_pd", "w_out", "w_up", "ffn_conv_w", "ffn_conv_b", "w_down", "final_norm_w")
    w_sh = dict(c_ctx=c_ctx, w_mod=w_mod, b_mod=b_mod, w_in=w_in, q_norm_w=q_norm_w, k_norm_w=k_norm_w,
                conv_qkv_w=conv_qkv_w, a_log=a_log, dt_bias=dt_bias, gdn_norm_w=gdn_norm_w, w_pa=w_pa, w_pd=w_pd,
                w_out=w_out, w_up=w_up, ffn_conv_w=ffn_conv_w, ffn_conv_b=ffn_conv_b, w_down=w_down,
                final_norm_w=final_norm_w)
    m_sh = dict(c_ctx=m_c_ctx, w_mod=m_w_mod, b_mod=m_b_mod, w_in=m_w_in, q_norm_w=m_q_norm_w, k_norm_w=m_k_norm_w,
                conv_qkv_w=m_conv_qkv_w, a_log=m_a_log, dt_bias=m_dt_bias, gdn_norm_w=m_gdn_norm_w, w_pa=m_w_pa,
                w_pd=m_w_pd, w_out=m_w_out, w_up=m_w_up, ffn_conv_w=m_ffn_conv_w, ffn_conv_b=m_ffn_conv_b,
                w_down=m_w_down, final_norm_w=m_final_norm_w)
    v_sh = dict(c_ctx=v_c_ctx, w_mod=v_w_mod, b_mod=v_b_mod, w_in=v_w_in, q_norm_w=v_q_norm_w, k_norm_w=v_k_norm_w,
                conv_qkv_w=v_conv_qkv_w, a_log=v_a_log, dt_bias=v_dt_bias, gdn_norm_w=v_gdn_norm_w, w_pa=v_w_pa,
                w_pd=v_w_pd, w_out=v_w_out, w_up=v_w_up, ffn_conv_w=v_ffn_conv_w, ffn_conv_b=v_ffn_conv_b,
                w_down=v_w_down, final_norm_w=v_final_norm_w)
    chip = 2 * lax.axis_index("x") + lax.axis_index("y")

    conv_bits = jnp.concatenate([lax.bitcast_convert_type(w_sh[nm][0], BF16).reshape(-1)
                                 for nm in ("conv_qkv_w", "ffn_conv_w")])
    shards = {nm: w_sh[nm][0].astype(BF16).T if nm == "w_in" else w_sh[nm][0].astype(BF16) for nm in _BIG}
    shards["conv"] = jnp.pad(conv_bits, (0, _CONV_ROWS * D_MODEL - _CONV_ELEMS)).reshape(_CONV_ROWS, D_MODEL)
    gathered = all_gather_chips({nm: shards[nm] for nm in _EARLY + ("conv",)})
    gathered, late_shards = lax.optimization_barrier((gathered, {nm: shards[nm] for nm in _LATE}))
    started = push_start("ag_late_start", late_shards, {nm: (N_CHIPS,) + a.shape for nm, a in late_shards.items()},
                         _gather_copies, 4 * len(_LATE))
    c = c + started[-1][0:1, 0:1]

    wb = {nm: _full_of_blocks(nm, gathered[nm]) for nm in _EARLY}
    wb["w_in_main"], wb["w_in_small"] = _w_in_regroup(wb.pop("w_in"))
    conv_all = gathered["conv"].reshape(N_CHIPS, -1)[:, :_CONV_ELEMS]
    n_cq = 2 * 3 * CONV_W // N_CHIPS
    unbits = lambda a, w: lax.bitcast_convert_type(a.reshape(N_CHIPS, 3, w // N_CHIPS, 2), F32).transpose(1, 0, 2).reshape(3, w)
    ws = dict(c_ctx=c_ctx, b_mod=b_mod, q_norm_w=q_norm_w, k_norm_w=k_norm_w, a_log=a_log[0], dt_bias=dt_bias[0],
              gdn_norm_w=gdn_norm_w, ffn_conv_b=ffn_conv_b, final_norm_w=final_norm_w,
              conv_qkv_w=unbits(conv_all[:, :n_cq], CONV_W), ffn_conv_w=unbits(conv_all[:, n_cq:], 2 * D_FF))
    wz = {nm: jnp.zeros(a.shape, F32) for nm, a in wb.items()}
    wz.update({nm: jnp.zeros((N_CHIPS,) + _shard_shape(nm) if nm in _COL_SHARDED else _FULL_SHAPE[nm], F32)
               for nm in _LATE})

    mixed, vjp_mix = jax.vjp(lambda x_, wz_, ws_: token_mixing(x_, wz_, wb, ws_, c, ctx[0]), x[0], wz, ws)
    got = push_wait("ag_late_wait", started, _gather_copies, mixed["gdn_x"])
    wb_late = {nm: _full_of_blocks(nm, got[nm]) for nm in _LATE}
    loss_local, vjp_chan = jax.vjp(
        lambda mixed_, wz_, ws_: channel_mixing(mixed_, wz_, wb_late, ws_, loss_target[0]), mixed, wz, ws)
    d_mixed, gz_chan, gs_chan = vjp_chan(jnp.ones((), F32))

    sel = jnp.stack([lax.axis_index("c"), chip]).astype(jnp.int32)
    g32_late = {nm: _blocks_of_full(nm, gz_chan[nm]) for nm in _LATE}
    theirs_late = sibling_halves(g32_late, "rs_sibling_late")
    sums_late = {nm: _presum(nm, sel, g32_late[nm], theirs_late[nm]) for nm in _LATE}
    scattering = push_start("rs_late_start", sums_late, {nm: (3,) + a.shape[1:] for nm, a in sums_late.items()},
                            _scatter_copies, 3 * len(_LATE))
    d_mixed = {**d_mixed, "gdn_x": d_mixed["gdn_x"] + scattering[-1][0:1, 0:1]}
    gx, gz_mix, gs_mix = vjp_mix(d_mixed)
    gs = jax.tree.map(jnp.add, gs_mix, gs_chan)
    got_late = push_wait("rs_late_wait", scattering, _scatter_copies, gx)
    loss = lax.psum(loss_local, ("x", "y", "c"))

    gs["a_log"], gs["dt_bias"] = gs["a_log"][None], gs["dt_bias"][None]
    like = {nm: gs[nm] for nm in _SMALL}
    small_rows = -(-sum(int(np.prod(like[nm].shape)) for nm in _SMALL) // 1024) * 8
    small_sum = all_reduce_small(_pack_small(gs, small_rows))
    gz_mix["w_mod"], small_sum = lax.optimization_barrier((gz_mix["w_mod"], small_sum))
    g_small = _unpack_small(small_sum, like)
    for nm, width in (("conv_qkv_w", CONV_W), ("ffn_conv_w", 2 * D_FF)):
        g_small[nm] = lax.dynamic_slice_in_dim(g_small[nm], chip * (width // N_CHIPS), width // N_CHIPS, axis=1)[None]

    gz_mix["w_in"] = _w_in_ungroup(gz_mix.pop("w_in_main"), gz_mix.pop("w_in_small"))
    g32 = {nm: _blocks_of_full(nm, gz_mix[nm]) for nm in _EARLY}
    theirs = sibling_halves(g32, "rs_sibling")
    sums = {nm: _presum(nm, sel, g32[nm], theirs[nm]) for nm in _EARLY}
    scattering, zero = {}, None
    for nm in _EARLY[::-1]:
        src = sums[nm] if zero is None else sums[nm] + zero.astype(BF16)
        scattering[nm] = push_start("rs_early_start_" + nm, {nm: src}, {nm: (3,) + src.shape[1:]},
                                    _scatter_copies, 3)
        zero = scattering[nm][-1][0:1, 0:1]
    got_late = {nm: a + zero.astype(BF16) if nm in ("w_up", "w_down") else a for nm, a in got_late.items()}
    g_big = sibling_assemble({nm: _finalsum(nm, sel, g32_late[nm], theirs_late[nm], got_late[nm]) for nm in _LATE},
                             "rs_assemble_late")
    grads, deltas, new_m, new_v = {}, {}, {}, {}

    def adamw_big(nm):
        form = (lambda a: a.T) if nm == "w_in" else (lambda a: a)
        grads[nm], deltas[nm], new_m[nm], new_v[nm] = (
            form(o)[None] for o in _adamw(form(w_sh[nm][0]), g_big[nm], form(m_sh[nm][0]), form(v_sh[nm][0]),
                                          "adamw_" + nm))

    for nm in _LATE:
        adamw_big(nm)

    shard_like = {nm: w_sh[nm] for nm in _SMALL}
    rows_l = -(-sum(int(np.prod(shard_like[nm].shape)) for nm in _SMALL) // 1024) * 8
    g_l = _pack_small({nm: g_small[nm].reshape(w_sh[nm].shape) for nm in _SMALL}, rows_l) + zero
    outs = _adamw(_pack_small(w_sh, rows_l), g_l, _pack_small(m_sh, rows_l), _pack_small(v_sh, rows_l), "adamw_small")
    for tree, packed in zip((grads, deltas, new_m, new_v), outs):
        tree.update(_unpack_small(packed, shard_like))

    done_meanwhile = deltas["w_up"][0, :8, :HEAD_DIM] + outs[1][:8, :]
    for nm in _EARLY[::-1]:
        got = push_wait("rs_early_wait_" + nm, scattering[nm], _scatter_copies, done_meanwhile)
        g_big.update(sibling_assemble({nm: _finalsum(nm, sel, g32[nm], theirs[nm], got[nm])}, "rs_assemble_" + nm))
        adamw_big(nm)
        done_meanwhile = deltas[nm][0, :8, :HEAD_DIM]

    return (loss, gx[None], *[grads[nm] for nm in names], *[deltas[nm] for nm in names],
            *[new_m[nm] for nm in names], *[new_v[nm] for nm in names])
```
